```python
import jax, jax.numpy as jnp
from jax import lax
import numpy as np


D_MODEL = 1024
BATCH = 8
SEQ = 8192
DEPTH = 2

POOL_WINDOWS = (2, 4, 8, 16)
POOL_GROUPS = len(POOL_WINDOWS)
POOL_WIDTH = D_MODEL
POOL_GROUP = POOL_WIDTH // POOL_GROUPS
SGU_CHUNK = 128
SGU_WIDTH = D_MODEL
SGU_HEADS = 8
SGU_HEAD_DIM = SGU_WIDTH // SGU_HEADS
N_BRANCHES = 2
IN_WIDTH = POOL_WIDTH + 2 * SGU_WIDTH + N_BRANCHES * D_MODEL
D_FF = 2816
CONV_WIDTH = 3
PLE_DIM = 256
EPS = 1e-6

kernel_name = "hybrid_pool_sgu_convffn_ple"


def rmsnorm(x, g):
    xf = x.astype(jnp.float32)
    y = xf * lax.rsqrt(jnp.mean(xf * xf, axis=-1, keepdims=True) + EPS)
    return (y * g.astype(jnp.float32)).astype(x.dtype)


def pool_mixer(h, w_pool, pool_scale):
    T = h.shape[1]
    hf = h.astype(jnp.float32)
    c = jnp.cumsum(hf, axis=1)
    t = jnp.arange(T)
    outs = []
    for gi, w in enumerate(POOL_WINDOWS):
        sl = slice(gi * POOL_GROUP, (gi + 1) * POOL_GROUP)
        cg = c[..., sl]
        prev = jnp.pad(cg, ((0, 0), (w, 0), (0, 0)))[:, :T]
        cnt = jnp.minimum(t + 1, w).astype(jnp.float32)[None, :, None]
        outs.append((cg - prev) / cnt - hf[..., sl])
    pooled = jnp.stack(outs, axis=2).astype(h.dtype)
    y = jnp.einsum('btgc,gcd->btgd', pooled, w_pool)
    return y.reshape(h.shape) * pool_scale


def spatial_gating(z_uv, sgu_norm, w_spatial, b_spatial):
    B, T, _ = z_uv.shape
    z = jax.nn.gelu(z_uv, approximate=False)
    u, v = z[..., :SGU_WIDTH], z[..., SGU_WIDTH:]
    v = rmsnorm(v, sgu_norm)
    nc = T // SGU_CHUNK
    v = v.reshape(B, nc, SGU_CHUNK, SGU_HEADS, SGU_HEAD_DIM)
    mask = jnp.tril(jnp.ones((SGU_CHUNK, SGU_CHUNK), dtype=w_spatial.dtype))
    ws = w_spatial * mask[None]
    mixed = jnp.einsum('hts,bnshd->bnthd', ws, v)
    mixed = mixed + jnp.transpose(b_spatial)[None, None, :, :, None]
    return u * mixed.reshape(B, T, SGU_WIDTH)


def conv_ffn(h, w_up, conv_w, conv_b, w_down):
    T = h.shape[1]
    up = h @ w_up
    up_pad = jnp.pad(up, ((0, 0), (CONV_WIDTH - 1, 0), (0, 0)))
    conv = conv_b
    for k in range(CONV_WIDTH):
        conv = conv + conv_w[k] * up_pad[:, k:k + T]
    a, b = conv[..., :D_FF], conv[..., D_FF:]
    return (jax.nn.gelu(a, approximate=False) * b) @ w_down


def _fwd_setup_inputs(seed: int = 0) -> dict:
    key = jax.random.key(seed)
    ks = jax.random.split(key, 24)
    f32 = jnp.float32
    L, D = DEPTH, D_MODEL

    def nrm(k, shape, scale):
        return jax.random.normal(k, shape, f32) * scale

    def gain(k, shape):
        return 1.0 + 0.05 * jax.random.normal(k, shape, f32)

    return {
        "x": nrm(ks[0], (BATCH, SEQ, D), 1.0),
        "p": nrm(ks[1], (DEPTH, BATCH, SEQ, PLE_DIM), 1.0),
        "mix_norm": gain(ks[2], (L, D)),
        "w_in": nrm(ks[3], (L, D, IN_WIDTH), D ** -0.5),
        "w_pool": nrm(ks[4], (L, POOL_GROUPS, POOL_GROUP, POOL_GROUP), POOL_GROUP ** -0.5),
        "pool_scale": gain(ks[5], (L, POOL_WIDTH)),
        "sgu_norm": gain(ks[6], (L, SGU_WIDTH)),
        "w_spatial": nrm(ks[7], (L, SGU_HEADS, SGU_CHUNK, SGU_CHUNK), 0.5 * SGU_CHUNK ** -0.5),
        "b_spatial": gain(ks[8], (L, SGU_HEADS, SGU_CHUNK)),
        "w_branch_a": nrm(ks[9], (L, POOL_WIDTH, D), POOL_WIDTH ** -0.5),
        "w_branch_b": nrm(ks[10], (L, SGU_WIDTH, D), SGU_WIDTH ** -0.5),
        "w_out": nrm(ks[11], (L, D, D), D ** -0.5),
        "ffn_norm": gain(ks[12], (L, D)),
        "w_up": nrm(ks[13], (L, D, 2 * D_FF), D ** -0.5),
        "conv_w": nrm(ks[14], (L, CONV_WIDTH, 2 * D_FF), CONV_WIDTH ** -0.5),
        "conv_b": nrm(ks[15], (L, 2 * D_FF), 0.02),
        "w_down": nrm(ks[16], (L, D_FF, D), D_FF ** -0.5),
        "ple_norm": gain(ks[17], (L, D)),
        "w_ple_gate": nrm(ks[18], (L, D, D), D ** -0.5),
        "w_ple": nrm(ks[19], (L, PLE_DIM, D), PLE_DIM ** -0.5),
        "final_norm": gain(ks[20], (D,)),
    }


def _fwd_reference(x, p, mix_norm, w_in, w_pool, pool_scale, sgu_norm, w_spatial, b_spatial,
              w_branch_a, w_branch_b, w_out, ffn_norm, w_up, conv_w, conv_b, w_down,
              ple_norm, w_ple_gate, w_ple, final_norm):
    o_uv = POOL_WIDTH
    o_gate = POOL_WIDTH + 2 * SGU_WIDTH
    for i in range(DEPTH):
        h = rmsnorm(x, mix_norm[i])
        z = h @ w_in[i]
        z_pool = z[..., :o_uv]
        z_uv = z[..., o_uv:o_gate]
        z_gate = z[..., o_gate:]
        y_a = pool_mixer(z_pool, w_pool[i], pool_scale[i]) @ w_branch_a[i]
        y_b = spatial_gating(z_uv, sgu_norm[i], w_spatial[i], b_spatial[i]) @ w_branch_b[i]
        gates = jax.nn.sigmoid(z_gate.astype(jnp.float32)).astype(x.dtype)
        g_a, g_b = gates[..., :D_MODEL], gates[..., D_MODEL:]
        x = x + (g_a * y_a + g_b * y_b) @ w_out[i]
        h = rmsnorm(x, ffn_norm[i])
        x = x + conv_ffn(h, w_up[i], conv_w[i], conv_b[i], w_down[i])
        gate = jax.nn.sigmoid((rmsnorm(x, ple_norm[i]) @ w_ple_gate[i]).astype(jnp.float32)).astype(x.dtype)
        x = x + gate * (p[i] @ w_ple[i])
    return rmsnorm(x, final_norm)


import jax as _jax
import jax.numpy as _jnp

TWIN_FORMAT = 'train_step'
FWD_PARAMS = ['x', 'p', 'mix_norm', 'w_in', 'w_pool', 'pool_scale', 'sgu_norm', 'w_spatial', 'b_spatial', 'w_branch_a', 'w_branch_b', 'w_out', 'ffn_norm', 'w_up', 'conv_w', 'conv_b', 'w_down', 'ple_norm', 'w_ple_gate', 'w_ple', 'final_norm']
TWIN_WEIGHTS = ['mix_norm', 'w_in', 'w_pool', 'pool_scale', 'sgu_norm', 'w_spatial', 'b_spatial', 'w_branch_a', 'w_branch_b', 'w_out', 'ffn_norm', 'w_up', 'conv_w', 'conv_b', 'w_down', 'ple_norm', 'w_ple_gate', 'w_ple', 'final_norm']
TWIN_DIFF_INPUT = 'x'
TWIN_INPUTS = ['x', 'p', 'mix_norm', 'w_in', 'w_pool', 'pool_scale', 'sgu_norm', 'w_spatial', 'b_spatial', 'w_branch_a', 'w_branch_b', 'w_out', 'ffn_norm', 'w_up', 'conv_w', 'conv_b', 'w_down', 'ple_norm', 'w_ple_gate', 'w_ple', 'final_norm', 'loss_target', 'm_mix_norm', 'm_w_in', 'm_w_pool', 'm_pool_scale', 'm_sgu_norm', 'm_w_spatial', 'm_b_spatial', 'm_w_branch_a', 'm_w_branch_b', 'm_w_out', 'm_ffn_norm', 'm_w_up', 'm_conv_w', 'm_conv_b', 'm_w_down', 'm_ple_norm', 'm_w_ple_gate', 'm_w_ple', 'm_final_norm', 'v_mix_norm', 'v_w_in', 'v_w_pool', 'v_pool_scale', 'v_sgu_norm', 'v_w_spatial', 'v_b_spatial', 'v_w_branch_a', 'v_w_branch_b', 'v_w_out', 'v_ffn_norm', 'v_w_up', 'v_conv_w', 'v_conv_b', 'v_w_down', 'v_ple_norm', 'v_w_ple_gate', 'v_w_ple', 'v_final_norm']
TWIN_OUTPUTS = ['loss', 'grad_x', 'grad_mix_norm', 'grad_w_in', 'grad_w_pool', 'grad_pool_scale', 'grad_sgu_norm', 'grad_w_spatial', 'grad_b_spatial', 'grad_w_branch_a', 'grad_w_branch_b', 'grad_w_out', 'grad_ffn_norm', 'grad_w_up', 'grad_conv_w', 'grad_conv_b', 'grad_w_down', 'grad_ple_norm', 'grad_w_ple_gate', 'grad_w_ple', 'grad_final_norm', 'delta_mix_norm', 'delta_w_in', 'delta_w_pool', 'delta_pool_scale', 'delta_sgu_norm', 'delta_w_spatial', 'delta_b_spatial', 'delta_w_branch_a', 'delta_w_branch_b', 'delta_w_out', 'delta_ffn_norm', 'delta_w_up', 'delta_conv_w', 'delta_conv_b', 'delta_w_down', 'delta_ple_norm', 'delta_w_ple_gate', 'delta_w_ple', 'delta_final_norm', 'new_m_mix_norm', 'new_m_w_in', 'new_m_w_pool', 'new_m_pool_scale', 'new_m_sgu_norm', 'new_m_w_spatial', 'new_m_b_spatial', 'new_m_w_branch_a', 'new_m_w_branch_b', 'new_m_w_out', 'new_m_ffn_norm', 'new_m_w_up', 'new_m_conv_w', 'new_m_conv_b', 'new_m_w_down', 'new_m_ple_norm', 'new_m_w_ple_gate', 'new_m_w_ple', 'new_m_final_norm', 'new_v_mix_norm', 'new_v_w_in', 'new_v_w_pool', 'new_v_pool_scale', 'new_v_sgu_norm', 'new_v_w_spatial', 'new_v_b_spatial', 'new_v_w_branch_a', 'new_v_w_branch_b', 'new_v_w_out', 'new_v_ffn_norm', 'new_v_w_up', 'new_v_conv_w', 'new_v_conv_b', 'new_v_w_down', 'new_v_ple_norm', 'new_v_w_ple_gate', 'new_v_w_ple', 'new_v_final_norm']
TWIN_LEAF_KINDS = {'loss': 'loss', 'grad_x': 'grad_x', 'grad_mix_norm': 'grad_w', 'grad_w_in': 'grad_w', 'grad_w_pool': 'grad_w', 'grad_pool_scale': 'grad_w', 'grad_sgu_norm': 'grad_w', 'grad_w_spatial': 'grad_w', 'grad_b_spatial': 'grad_w', 'grad_w_branch_a': 'grad_w', 'grad_w_branch_b': 'grad_w', 'grad_w_out': 'grad_w', 'grad_ffn_norm': 'grad_w', 'grad_w_up': 'grad_w', 'grad_conv_w': 'grad_w', 'grad_conv_b': 'grad_w', 'grad_w_down': 'grad_w', 'grad_ple_norm': 'grad_w', 'grad_w_ple_gate': 'grad_w', 'grad_w_ple': 'grad_w', 'grad_final_norm': 'grad_w', 'delta_mix_norm': 'delta_w', 'delta_w_in': 'delta_w', 'delta_w_pool': 'delta_w', 'delta_pool_scale': 'delta_w', 'delta_sgu_norm': 'delta_w', 'delta_w_spatial': 'delta_w', 'delta_b_spatial': 'delta_w', 'delta_w_branch_a': 'delta_w', 'delta_w_branch_b': 'delta_w', 'delta_w_out': 'delta_w', 'delta_ffn_norm': 'delta_w', 'delta_w_up': 'delta_w', 'delta_conv_w': 'delta_w', 'delta_conv_b': 'delta_w', 'delta_w_down': 'delta_w', 'delta_ple_norm': 'delta_w', 'delta_w_ple_gate': 'delta_w', 'delta_w_ple': 'delta_w', 'delta_final_norm': 'delta_w', 'new_m_mix_norm': 'new_m', 'new_m_w_in': 'new_m', 'new_m_w_pool': 'new_m', 'new_m_pool_scale': 'new_m', 'new_m_sgu_norm': 'new_m', 'new_m_w_spatial': 'new_m', 'new_m_b_spatial': 'new_m', 'new_m_w_branch_a': 'new_m', 'new_m_w_branch_b': 'new_m', 'new_m_w_out': 'new_m', 'new_m_ffn_norm': 'new_m', 'new_m_w_up': 'new_m', 'new_m_conv_w': 'new_m', 'new_m_conv_b': 'new_m', 'new_m_w_down': 'new_m', 'new_m_ple_norm': 'new_m', 'new_m_w_ple_gate': 'new_m', 'new_m_w_ple': 'new_m', 'new_m_final_norm': 'new_m', 'new_v_mix_norm': 'new_v', 'new_v_w_in': 'new_v', 'new_v_w_pool': 'new_v', 'new_v_pool_scale': 'new_v', 'new_v_sgu_norm': 'new_v', 'new_v_w_spatial': 'new_v', 'new_v_b_spatial': 'new_v', 'new_v_w_branch_a': 'new_v', 'new_v_w_branch_b': 'new_v', 'new_v_w_out': 'new_v', 'new_v_ffn_norm': 'new_v', 'new_v_w_up': 'new_v', 'new_v_conv_w': 'new_v', 'new_v_conv_b': 'new_v', 'new_v_w_down': 'new_v', 'new_v_ple_norm': 'new_v', 'new_v_w_ple_gate': 'new_v', 'new_v_w_ple': 'new_v', 'new_v_final_norm': 'new_v'}


def _forward(args):
    return _fwd_reference(*[args[k] for k in FWD_PARAMS])


def _output_shape():
    def fwd():
        inp = _fwd_setup_inputs(0)
        return _fwd_reference(*[inp[k] for k in FWD_PARAMS])
    out = _jax.eval_shape(fwd)
    return out.shape, out.dtype

N_MICROBATCH = 1
ADAM_LR = 0.001
ADAM_B1 = 0.9
ADAM_B2 = 0.999
ADAM_EPS = 1e-08
ADAM_WD = 0.01
ADAM_STEP = 10
PER_EXAMPLE_BATCH_AXIS = {'x': 0, 'p': 1, 'loss_target': 0}
SHARED_INPUTS = []
_WEIGHT_DTYPES = {'mix_norm': _jnp.float32, 'w_in': _jnp.float32, 'w_pool': _jnp.float32, 'pool_scale': _jnp.float32, 'sgu_norm': _jnp.float32, 'w_spatial': _jnp.float32, 'b_spatial': _jnp.float32, 'w_branch_a': _jnp.float32, 'w_branch_b': _jnp.float32, 'w_out': _jnp.float32, 'ffn_norm': _jnp.float32, 'w_up': _jnp.float32, 'conv_w': _jnp.float32, 'conv_b': _jnp.float32, 'w_down': _jnp.float32, 'ple_norm': _jnp.float32, 'w_ple_gate': _jnp.float32, 'w_ple': _jnp.float32, 'final_norm': _jnp.float32}
MOMENT_SCALE = {'mix_norm': 1.414717e-01, 'w_in': 6.274916e-02, 'w_pool': 1.004285e-01, 'pool_scale': 1.028572e-01, 'sgu_norm': 2.748586e-02, 'w_spatial': 5.142595e-02, 'b_spatial': 7.334937e-02, 'w_branch_a': 1.002471e-01, 'w_branch_b': 8.522286e-02, 'w_out': 1.321235e-01, 'ffn_norm': 1.567833e-01, 'w_up': 6.694512e-02, 'conv_w': 6.873212e-02, 'conv_b': 7.365529e-02, 'w_down': 1.109372e-01, 'ple_norm': 3.557065e-02, 'w_ple_gate': 3.679204e-02, 'w_ple': 9.032745e-02, 'final_norm': 6.425832e+01}


def _to_microbatches(a, axis):
    t = _jnp.moveaxis(a, axis, 0)
    t = t.reshape((N_MICROBATCH, t.shape[0] // N_MICROBATCH) + t.shape[1:])
    return _jnp.moveaxis(t, 1, axis + 1)


def setup_inputs(seed: int = 0) -> dict:
    inp = _fwd_setup_inputs(seed)
    key = _jax.random.fold_in(_jax.random.key(seed), 7919)
    shape, _ = _output_shape()
    out = dict(inp)
    out["loss_target"] = _jax.random.normal(_jax.random.fold_in(key, 0), shape, _jnp.float32)
    for i, name in enumerate(TWIN_WEIGHTS):
        w = inp[name].astype(_jnp.float32)
        if MOMENT_SCALE is None:
            s = _jnp.sqrt(_jnp.mean(_jnp.square(w)) + 1e-30)
        else:
            s = MOMENT_SCALE[name]
        km, kv = _jax.random.split(_jax.random.fold_in(key, i + 1))
        out[name] = w
        out["m_" + name] = s * _jax.random.normal(km, w.shape, _jnp.float32)
        out["v_" + name] = (s * s) * _jax.random.uniform(kv, w.shape, _jnp.float32, 0.5, 1.5)
    if N_MICROBATCH > 1:
        for name, axis in PER_EXAMPLE_BATCH_AXIS.items():
            out[name] = _to_microbatches(out[name], axis)
    return {'x': out['x'], 'p': out['p'], 'mix_norm': out['mix_norm'], 'w_in': out['w_in'], 'w_pool': out['w_pool'], 'pool_scale': out['pool_scale'], 'sgu_norm': out['sgu_norm'], 'w_spatial': out['w_spatial'], 'b_spatial': out['b_spatial'], 'w_branch_a': out['w_branch_a'], 'w_branch_b': out['w_branch_b'], 'w_out': out['w_out'], 'ffn_norm': out['ffn_norm'], 'w_up': out['w_up'], 'conv_w': out['conv_w'], 'conv_b': out['conv_b'], 'w_down': out['w_down'], 'ple_norm': out['ple_norm'], 'w_ple_gate': out['w_ple_gate'], 'w_ple': out['w_ple'], 'final_norm': out['final_norm'], 'loss_target': out['loss_target'], 'm_mix_norm': out['m_mix_norm'], 'm_w_in': out['m_w_in'], 'm_w_pool': out['m_w_pool'], 'm_pool_scale': out['m_pool_scale'], 'm_sgu_norm': out['m_sgu_norm'], 'm_w_spatial': out['m_w_spatial'], 'm_b_spatial': out['m_b_spatial'], 'm_w_branch_a': out['m_w_branch_a'], 'm_w_branch_b': out['m_w_branch_b'], 'm_w_out': out['m_w_out'], 'm_ffn_norm': out['m_ffn_norm'], 'm_w_up': out['m_w_up'], 'm_conv_w': out['m_conv_w'], 'm_conv_b': out['m_conv_b'], 'm_w_down': out['m_w_down'], 'm_ple_norm': out['m_ple_norm'], 'm_w_ple_gate': out['m_w_ple_gate'], 'm_w_ple': out['m_w_ple'], 'm_final_norm': out['m_final_norm'], 'v_mix_norm': out['v_mix_norm'], 'v_w_in': out['v_w_in'], 'v_w_pool': out['v_w_pool'], 'v_pool_scale': out['v_pool_scale'], 'v_sgu_norm': out['v_sgu_norm'], 'v_w_spatial': out['v_w_spatial'], 'v_b_spatial': out['v_b_spatial'], 'v_w_branch_a': out['v_w_branch_a'], 'v_w_branch_b': out['v_w_branch_b'], 'v_w_out': out['v_w_out'], 'v_ffn_norm': out['v_ffn_norm'], 'v_w_up': out['v_w_up'], 'v_conv_w': out['v_conv_w'], 'v_conv_b': out['v_conv_b'], 'v_w_down': out['v_w_down'], 'v_ple_norm': out['v_ple_norm'], 'v_w_ple_gate': out['v_w_ple_gate'], 'v_w_ple': out['v_w_ple'], 'v_final_norm': out['v_final_norm']}


def _loss(weights, diff, rest, loss_target):
    with _jax.named_scope("forward"):
        args = {**rest, TWIN_DIFF_INPUT: diff, **{k: w.astype(_WEIGHT_DTYPES[k]) for k, w in weights.items()}}
        y = _forward(args)
    with _jax.named_scope("loss_head"):
        err = _jnp.square(y.astype(_jnp.float32) - loss_target)
        return 0.5 * _jnp.sum(_jnp.mean(err, axis=-1)) if err.ndim else 0.5 * err


def _adamw(w, g, m, v):
    m = ADAM_B1 * m + (1.0 - ADAM_B1) * g
    v = ADAM_B2 * v + (1.0 - ADAM_B2) * _jnp.square(g)
    m_hat = m / (1.0 - ADAM_B1 ** ADAM_STEP)
    v_hat = v / (1.0 - ADAM_B2 ** ADAM_STEP)
    delta = -ADAM_LR * (m_hat / (_jnp.sqrt(v_hat) + ADAM_EPS) + ADAM_WD * w)
    return delta, m, v


def reference(x, p, mix_norm, w_in, w_pool, pool_scale, sgu_norm, w_spatial, b_spatial, w_branch_a, w_branch_b, w_out, ffn_norm, w_up, conv_w, conv_b, w_down, ple_norm, w_ple_gate, w_ple, final_norm, loss_target, m_mix_norm, m_w_in, m_w_pool, m_pool_scale, m_sgu_norm, m_w_spatial, m_b_spatial, m_w_branch_a, m_w_branch_b, m_w_out, m_ffn_norm, m_w_up, m_conv_w, m_conv_b, m_w_down, m_ple_norm, m_w_ple_gate, m_w_ple, m_final_norm, v_mix_norm, v_w_in, v_w_pool, v_pool_scale, v_sgu_norm, v_w_spatial, v_b_spatial, v_w_branch_a, v_w_branch_b, v_w_out, v_ffn_norm, v_w_up, v_conv_w, v_conv_b, v_w_down, v_ple_norm, v_w_ple_gate, v_w_ple, v_final_norm):
    given = dict(x=x, p=p, mix_norm=mix_norm, w_in=w_in, w_pool=w_pool, pool_scale=pool_scale, sgu_norm=sgu_norm, w_spatial=w_spatial, b_spatial=b_spatial, w_branch_a=w_branch_a, w_branch_b=w_branch_b, w_out=w_out, ffn_norm=ffn_norm, w_up=w_up, conv_w=conv_w, conv_b=conv_b, w_down=w_down, ple_norm=ple_norm, w_ple_gate=w_ple_gate, w_ple=w_ple, final_norm=final_norm, loss_target=loss_target, m_mix_norm=m_mix_norm, m_w_in=m_w_in, m_w_pool=m_w_pool, m_pool_scale=m_pool_scale, m_sgu_norm=m_sgu_norm, m_w_spatial=m_w_spatial, m_b_spatial=m_b_spatial, m_w_branch_a=m_w_branch_a, m_w_branch_b=m_w_branch_b, m_w_out=m_w_out, m_ffn_norm=m_ffn_norm, m_w_up=m_w_up, m_conv_w=m_conv_w, m_conv_b=m_conv_b, m_w_down=m_w_down, m_ple_norm=m_ple_norm, m_w_ple_gate=m_w_ple_gate, m_w_ple=m_w_ple, m_final_norm=m_final_norm, v_mix_norm=v_mix_norm, v_w_in=v_w_in, v_w_pool=v_w_pool, v_pool_scale=v_pool_scale, v_sgu_norm=v_sgu_norm, v_w_spatial=v_w_spatial, v_b_spatial=v_b_spatial, v_w_branch_a=v_w_branch_a, v_w_branch_b=v_w_branch_b, v_w_out=v_w_out, v_ffn_norm=v_ffn_norm, v_w_up=v_w_up, v_conv_w=v_conv_w, v_conv_b=v_conv_b, v_w_down=v_w_down, v_ple_norm=v_ple_norm, v_w_ple_gate=v_w_ple_gate, v_w_ple=v_w_ple, v_final_norm=v_final_norm)
    weights = {n: given[n] for n in TWIN_WEIGHTS}
    shared = {n: given[n] for n in SHARED_INPUTS}
    per_example = {n: given[n] for n in ['x', 'p']}
    grad_fn = _jax.value_and_grad(_loss, argnums=(0, 1))

    def one_microbatch(ex, loss_target):
        ex = dict(ex)
        diff = ex.pop(TWIN_DIFF_INPUT)
        return grad_fn(weights, diff, {**shared, **ex}, loss_target)

    if N_MICROBATCH == 1:
        loss, (grad_w, grad_x) = one_microbatch(per_example, given["loss_target"])
    else:
        def body(carry, xs):
            loss_sum, grad_sum = carry
            l_k, (gw_k, gx_k) = one_microbatch(xs[0], xs[1])
            with _jax.named_scope("update"):
                return (loss_sum + l_k, _jax.tree.map(_jnp.add, grad_sum, gw_k)), gx_k

        init = (_jnp.zeros((), _jnp.float32), _jax.tree.map(_jnp.zeros_like, weights))
        (loss, grad_w), grad_x = _jax.lax.scan(body, init, (per_example, given["loss_target"]))
    with _jax.named_scope("update"):
        delta_w, new_m, new_v = {}, {}, {}
        for n in TWIN_WEIGHTS:
            delta_w[n], new_m[n], new_v[n] = _adamw(weights[n], grad_w[n], given["m_" + n], given["v_" + n])
    return (loss, grad_x, *[grad_w[n] for n in TWIN_WEIGHTS], *[delta_w[n] for n in TWIN_WEIGHTS],
            *[new_m[n] for n in TWIN_WEIGHTS], *[new_v[n] for n in TWIN_WEIGHTS])
```

```python
import functools

import jax
import jax.numpy as jnp
from jax import lax
from jax.experimental import pallas as pl
from jax.experimental.pallas import tpu as pltpu

F32 = jnp.float32
BF16 = jnp.bfloat16
EPS = 1e-6
D = 1024
POOL_WINDOWS = (2, 4, 8, 16)
PG = 256
POOL_HALO = 16
CHUNK = 128
HEADS = 8
DFF = 2816
CONV_HALO = 8
CONV_TC = 256
N_CHIPS = 4
LANES = 128
VMEM_LIMIT = 56 * 1024 * 1024
MESH = pl.DeviceIdType.MESH
ANY = pl.BlockSpec(memory_space=pl.ANY)

ADAM_LR = 0.001
ADAM_B1 = 0.9
ADAM_B2 = 0.999
ADAM_EPS = 1e-08
ADAM_WD = 0.01
ADAM_STEP = 10

BIG = (
    ("w_in", (D, 5 * D), 1, 5 * D // N_CHIPS),
    ("w_pool", (4, PG, PG), 1, PG // N_CHIPS),
    ("w_branch_a", (D, D), 0, D // N_CHIPS),
    ("w_branch_b", (D, D), 0, D // N_CHIPS),
    ("w_out", (D, D), 0, D // N_CHIPS),
    ("w_up", (D, 2 * DFF), 1, 2 * DFF // N_CHIPS),
    ("w_down", (DFF, D), 0, DFF // N_CHIPS),
    ("w_ple_gate", (D, D), 0, D // N_CHIPS),
    ("w_ple", (PG, D), 1, D // N_CHIPS),
)
BIG_GRAD = tuple(t for t in BIG if t[0] != "w_pool")
CONV_ROWS = 8


def _params(n_axes):
    return pltpu.CompilerParams(dimension_semantics=("arbitrary",) * n_axes, vmem_limit_bytes=VMEM_LIMIT)


def _gelu(x):
    return 0.5 * x * (1.0 + lax.erf(x * 0.7071067811865476))


def _gelu_grad(x):
    return 0.5 * (1.0 + lax.erf(x * 0.7071067811865476)) + x * jnp.exp(-0.5 * x * x) * 0.3989422804014327


def _shard_shape(shape, axis, size):
    return tuple(size if a == axis else s for a, s in enumerate(shape))


def _block(ref, axis, j, size):
    idx = tuple(pl.ds(j * size, size) if a == axis else slice(None) for a in range(len(ref.shape)))
    return ref.at[idx]


def mm_nn(a, w, l, *, name, rows, out_dtype=F32, resid=None, a_row_blk_off=0, out=None, out_cols=None,
          out_col_off=0, tm=512, tn=None, tk=None):
    K, N = w.shape[1], w.shape[2]
    tn = tn or N
    tk = tk or K
    nk = K // tk
    out_cols = out_cols or N
    assert rows % tm == 0 and N % tn == 0 and K % tk == 0 and out_col_off % tn == 0
    has_resid, has_out = resid is not None, out is not None

    def body(*refs):
        refs = list(refs)
        a_ref, w_ref = refs[0], refs[1]
        r_ref = refs[2] if has_resid else None
        o_ref = refs[2 + has_resid + has_out]
        part = jnp.dot(a_ref[...].astype(BF16), w_ref[...], preferred_element_type=F32)
        if nk == 1:
            if has_resid:
                part = part + r_ref[...]
            o_ref[...] = part.astype(o_ref.dtype)
        else:
            acc = refs[-1]
            k = pl.program_id(2)

            @pl.when(k == 0)
            def _():
                acc[...] = part

            @pl.when(k > 0)
            def _():
                acc[...] += part

            @pl.when(k == nk - 1)
            def _():
                r = acc[...]
                if has_resid:
                    r = r + r_ref[...]
                o_ref[...] = r.astype(o_ref.dtype)

    in_specs = [pl.BlockSpec((tm, tk), lambda j, i, k: (i + a_row_blk_off, k)),
                pl.BlockSpec((None, tk, tn), lambda j, i, k: (l, k, j))]
    args = [a, w]
    if has_resid:
        in_specs.append(pl.BlockSpec((tm, tn), lambda j, i, k: (i, j)))
        args.append(resid)
    aliases = {}
    if has_out:
        in_specs.append(ANY)
        aliases = {len(args): 0}
        args.append(out)
    return pl.pallas_call(
        body, name=name, grid=(N // tn, rows // tm, nk),
        in_specs=in_specs,
        out_specs=pl.BlockSpec((tm, tn), lambda j, i, k: (i, j + out_col_off // tn)),
        out_shape=jax.ShapeDtypeStruct((rows, out_cols), out_dtype),
        scratch_shapes=[pltpu.VMEM((tm, tn), F32)] if nk > 1 else [],
        input_output_aliases=aliases, compiler_params=_params(3))(*args)


def mm_nt(a, w, l, *, name, rows, kdim=None, a_col_off=0, out_dtype=F32, tm=512, tn=None, tk=None):
    R = w.shape[1]
    kdim = kdim or w.shape[2]
    tn = tn or R
    tk = tk or kdim
    nk = kdim // tk
    assert rows % tm == 0 and R % tn == 0 and kdim % tk == 0 and a_col_off % tk == 0

    def body(a_ref, w_ref, o_ref, *scr):
        part = lax.dot_general(a_ref[...].astype(BF16), w_ref[...], (((1,), (1,)), ((), ())),
                               preferred_element_type=F32)
        if nk == 1:
            o_ref[...] = part.astype(o_ref.dtype)
        else:
            acc = scr[0]
            k = pl.program_id(2)

            @pl.when(k == 0)
            def _():
                acc[...] = part

            @pl.when(k > 0)
            def _():
                acc[...] += part

            @pl.when(k == nk - 1)
            def _():
                o_ref[...] = acc[...].astype(o_ref.dtype)

    return pl.pallas_call(
        body, name=name, grid=(R // tn, rows // tm, nk),
        in_specs=[pl.BlockSpec((tm, tk), lambda j, i, k: (i, k + a_col_off // tk)),
                  pl.BlockSpec((None, tn, tk), lambda j, i, k: (l, j, k))],
        out_specs=pl.BlockSpec((tm, tn), lambda j, i, k: (i, j)),
        out_shape=jax.ShapeDtypeStruct((rows, R), out_dtype),
        scratch_shapes=[pltpu.VMEM((tm, tn), F32)] if nk > 1 else [],
        compiler_params=_params(3))(a, w)


def mm_tn(a, b, l, *, name, rows, ka, nb, out=None, a_row_blk_off=0, b_col_off=0, tm=None, tn=None, tk=512):
    tm = tm or ka
    tn = tn or nb
    nk = rows // tk
    assert ka % tm == 0 and nb % tn == 0 and rows % tk == 0 and b_col_off % tn == 0
    has_out = out is not None

    def body(*refs):
        a_ref, b_ref = refs[0], refs[1]
        o_ref, acc = refs[2 + has_out], refs[-1]
        part = lax.dot_general(a_ref[...].astype(BF16), b_ref[...].astype(BF16), (((0,), (0,)), ((), ())),
                               preferred_element_type=F32)
        k = pl.program_id(2)

        @pl.when(k == 0)
        def _():
            acc[...] = part

        @pl.when(k > 0)
        def _():
            acc[...] += part

        @pl.when(k == nk - 1)
        def _():
            o_ref[...] = acc[...].astype(o_ref.dtype)

    in_specs = [pl.BlockSpec((tk, tm), lambda j, i, k: (k + a_row_blk_off, i)),
                pl.BlockSpec((tk, tn), lambda j, i, k: (k, j + b_col_off // tn))]
    args = [a, b]
    aliases = {}
    if has_out:
        in_specs.append(ANY)
        aliases = {2: 0}
        args.append(out)
    return pl.pallas_call(
        body, name=name, grid=(nb // tn, ka // tm, nk),
        in_specs=in_specs,
        out_specs=pl.BlockSpec((None, tm, tn), lambda j, i, k: (l, i, j)),
        out_shape=jax.ShapeDtypeStruct((2, ka, nb), BF16),
        scratch_shapes=[pltpu.VMEM((tm, tn), F32)],
        input_output_aliases=aliases, compiler_params=_params(3))(*args)


def _row_spec(tm, width, col=0):
    return pl.BlockSpec((tm, width), lambda i: (i, col))


def _gain_spec(l, width=D):
    return pl.BlockSpec((None, 1, width), lambda i: (l, 0, 0))


def norm_fwd(x, g3, l, *, name, tm=512):
    T = x.shape[0]

    def body(x_ref, g_ref, o_ref):
        xv = x_ref[...]
        r = lax.rsqrt(jnp.mean(xv * xv, axis=-1, keepdims=True) + EPS)
        o_ref[...] = (xv * r * g_ref[...]).astype(BF16)

    return pl.pallas_call(
        body, name=name, grid=(T // tm,),
        in_specs=[_row_spec(tm, D), _gain_spec(l)], out_specs=_row_spec(tm, D),
        out_shape=jax.ShapeDtypeStruct((T, D), BF16), compiler_params=_params(1))(x, g3)


def _winsum_back(ext, w):
    s, span = ext, 1
    while span < w:
        s = s + pltpu.roll(s, span, 0)
        span *= 2
    return s


def _winsum_fwd(ext, w):
    rows = ext.shape[0]
    s, span = ext, 1
    while span < w:
        s = s + pltpu.roll(s, rows - span, 0)
        span *= 2
    return s


def _pooled(ext, z, t, g, w):
    sl = slice(g * PG, (g + 1) * PG)
    s = _winsum_back(ext[:, sl], w)[POOL_HALO:, :]
    return s / jnp.minimum(t + 1, w).astype(F32) - z[:, sl]


def pool_fwd(z, wpool, scale3, l, *, name, tm=256):
    T = z.shape[0]
    hb = tm // POOL_HALO

    def body(z_ref, zp_ref, wp_ref, sc_ref, o_ref):
        i = pl.program_id(0)
        zv = z_ref[...]
        prev = jnp.where(i == 0, 0.0, zp_ref[...])
        ext = jnp.concatenate([prev, zv], axis=0)
        t = i * tm + lax.broadcasted_iota(jnp.int32, (tm, 1), 0)
        for g, w in enumerate(POOL_WINDOWS):
            sl = slice(g * PG, (g + 1) * PG)
            pooled = _pooled(ext, zv, t, g, w)
            q = jnp.dot(pooled.astype(BF16), wp_ref[g], preferred_element_type=F32)
            o_ref[:, sl] = (q * sc_ref[:, sl]).astype(BF16)

    return pl.pallas_call(
        body, name=name, grid=(T // tm,),
        in_specs=[_row_spec(tm, D),
                  pl.BlockSpec((POOL_HALO, D), lambda i: (jnp.maximum(i * hb - 1, 0), 0)),
                  pl.BlockSpec((None, 4, PG, PG), lambda i: (l, 0, 0, 0)),
                  _gain_spec(l)],
        out_specs=_row_spec(tm, D),
        out_shape=jax.ShapeDtypeStruct((T, D), BF16), compiler_params=_params(1))(z, z, wpool, scale3)


def sgu_fwd(z, g3, wsm, bT, l, *, name, tm=256):
    T = z.shape[0]

    def body(zu_ref, zv_ref, g_ref, ws_ref, b_ref, o_ref):
        gu = _gelu(zu_ref[...])
        gv = _gelu(zv_ref[...])
        rv = lax.rsqrt(jnp.mean(gv * gv, axis=-1, keepdims=True) + EPS)
        vn = (gv * rv * g_ref[...]).astype(BF16)
        for n in range(tm // CHUNK):
            r = slice(n * CHUNK, (n + 1) * CHUNK)
            for h in range(HEADS):
                cs = slice(h * CHUNK, (h + 1) * CHUNK)
                mixed = jnp.dot(ws_ref[h], vn[r, cs], preferred_element_type=F32) + b_ref[:, h:h + 1]
                o_ref[r, cs] = (gu[r, cs] * mixed).astype(BF16)

    return pl.pallas_call(
        body, name=name, grid=(T // tm,),
        in_specs=[_row_spec(tm, D, 1), _row_spec(tm, D, 2), _gain_spec(l),
                  pl.BlockSpec((None, HEADS, CHUNK, CHUNK), lambda i: (l, 0, 0, 0)),
                  pl.BlockSpec((None, CHUNK, HEADS), lambda i: (l, 0, 0))],
        out_specs=_row_spec(tm, D),
        out_shape=jax.ShapeDtypeStruct((T, D), BF16), compiler_params=_params(1))(z, z, g3, wsm, bT)


def gate_fwd(z, yab, *, name, tm=512):
    T = z.shape[0]

    def body(za_ref, zb_ref, y_ref, o_ref):
        ga = jax.nn.sigmoid(za_ref[...])
        gb = jax.nn.sigmoid(zb_ref[...])
        o_ref[...] = (ga * y_ref[:, :D] + gb * y_ref[:, D:]).astype(BF16)

    return pl.pallas_call(
        body, name=name, grid=(T // tm,),
        in_specs=[_row_spec(tm, D, 3), _row_spec(tm, D, 4), _row_spec(tm, 2 * D)],
        out_specs=_row_spec(tm, D),
        out_shape=jax.ShapeDtypeStruct((T, D), BF16), compiler_params=_params(1))(z, z, yab)


def _conv(ext, w_ref, b_ref):
    c = b_ref[...] + w_ref[0:1, :] * pltpu.roll(ext, 2, 0)
    c = c + w_ref[1:2, :] * pltpu.roll(ext, 1, 0)
    return c + w_ref[2:3, :] * ext


def conv_fwd(up, convw, convb3, l, *, name, tm=512):
    T = up.shape[0]
    tc = CONV_TC
    nc = DFF // tc
    hb = tm // CONV_HALO

    def body(ua_ref, uap_ref, ub_ref, ubp_ref, wa_ref, wb_ref, ba_ref, bb_ref, o_ref):
        i = pl.program_id(1)

        def conv_of(u_ref, p_ref, w_ref, b_ref):
            ext = jnp.concatenate([jnp.where(i == 0, 0.0, p_ref[...]), u_ref[...]], axis=0)
            return _conv(ext, w_ref, b_ref)[CONV_HALO:, :]

        ca = conv_of(ua_ref, uap_ref, wa_ref, ba_ref)
        cb = conv_of(ub_ref, ubp_ref, wb_ref, bb_ref)
        o_ref[...] = (_gelu(ca) * cb).astype(BF16)

    def cur(off):
        return pl.BlockSpec((tm, tc), lambda j, i: (i, j + off))

    def prev(off):
        return pl.BlockSpec((CONV_HALO, tc), lambda j, i: (jnp.maximum(i * hb - 1, 0), j + off))

    def wspec(off):
        return pl.BlockSpec((None, CONV_ROWS, tc), lambda j, i: (l, 0, j + off))

    def bspec(off):
        return pl.BlockSpec((None, 1, tc), lambda j, i: (l, 0, j + off))

    return pl.pallas_call(
        body, name=name, grid=(nc, T // tm),
        in_specs=[cur(0), prev(0), cur(nc), prev(nc), wspec(0), wspec(nc), bspec(0), bspec(nc)],
        out_specs=pl.BlockSpec((tm, tc), lambda j, i: (i, j)),
        out_shape=jax.ShapeDtypeStruct((T, DFF), BF16),
        compiler_params=_params(2))(up, up, up, up, convw, convw, convb3, convb3)


def ple_fwd(x2, pg, e, *, name, tm=512):
    T = x2.shape[0]

    def body(x_ref, pg_ref, e_ref, o_ref):
        o_ref[...] = x_ref[...] + jax.nn.sigmoid(pg_ref[...]) * e_ref[...]

    return pl.pallas_call(
        body, name=name, grid=(T // tm,),
        in_specs=[_row_spec(tm, D)] * 3, out_specs=_row_spec(tm, D),
        out_shape=jax.ShapeDtypeStruct((T, D), F32), compiler_params=_params(1))(x2, pg, e)


def loss_head(x, g3, tgt, *, name, tm=512):
    T = x.shape[0]

    def body(x_ref, g_ref, t_ref, loss_ref, dx_ref, dg_ref):
        @pl.when(pl.program_id(0) == 0)
        def _():
            loss_ref[...] = jnp.zeros_like(loss_ref)
            dg_ref[...] = jnp.zeros_like(dg_ref)

        xv, g = x_ref[...], g_ref[...]
        r = lax.rsqrt(jnp.mean(xv * xv, axis=-1, keepdims=True) + EPS)
        xh = xv * r
        err = xh * g - t_ref[...]
        loss_ref[...] += 0.5 * jnp.sum(jnp.mean(err * err, axis=-1, keepdims=True))
        dy = err * (1.0 / D)
        dyg = dy * g
        dx_ref[...] = r * (dyg - xh * jnp.mean(dyg * xh, axis=-1, keepdims=True))
        dg_ref[0:1, :] += jnp.sum(dy * xh, axis=0, keepdims=True)

    return pl.pallas_call(
        body, name=name, grid=(T // tm,),
        in_specs=[_row_spec(tm, D), pl.BlockSpec((1, D), lambda i: (0, 0)), _row_spec(tm, D)],
        out_specs=[pl.BlockSpec((8, LANES), lambda i: (0, 0)), _row_spec(tm, D),
                   pl.BlockSpec((8, D), lambda i: (0, 0))],
        out_shape=[jax.ShapeDtypeStruct((8, LANES), F32), jax.ShapeDtypeStruct((T, D), F32),
                   jax.ShapeDtypeStruct((8, D), F32)],
        compiler_params=_params(1))(x, g3, tgt)


def norm_bwd(dh, x, g3, l, dx_in, *, name, tm=512):
    T = x.shape[0]

    def body(dh_ref, x_ref, g_ref, dxi_ref, dx_ref, dg_ref):
        @pl.when(pl.program_id(0) == 0)
        def _():
            dg_ref[...] = jnp.zeros_like(dg_ref)

        xv, dh_v = x_ref[...], dh_ref[...]
        r = lax.rsqrt(jnp.mean(xv * xv, axis=-1, keepdims=True) + EPS)
        xh = xv * r
        dhg = dh_v * g_ref[...]
        dx_ref[...] = dxi_ref[...] + r * (dhg - xh * jnp.mean(dhg * xh, axis=-1, keepdims=True))
        dg_ref[0:1, :] += jnp.sum(dh_v * xh, axis=0, keepdims=True)

    return pl.pallas_call(
        body, name=name, grid=(T // tm,),
        in_specs=[_row_spec(tm, D), _row_spec(tm, D), _gain_spec(l), _row_spec(tm, D)],
        out_specs=[_row_spec(tm, D), pl.BlockSpec((8, D), lambda i: (0, 0))],
        out_shape=[jax.ShapeDtypeStruct((T, D), F32), jax.ShapeDtypeStruct((8, D), F32)],
        compiler_params=_params(1))(dh, x, g3, dx_in)


def ple_bwd(dx, pg, e, *, name, tm=512):
    T = dx.shape[0]

    def body(dx_ref, pg_ref, e_ref, de_ref, dpg_ref):
        gate = jax.nn.sigmoid(pg_ref[...])
        dxv = dx_ref[...]
        de_ref[...] = (dxv * gate).astype(BF16)
        dpg_ref[...] = (dxv * e_ref[...] * gate * (1.0 - gate)).astype(BF16)

    return pl.pallas_call(
        body, name=name, grid=(T // tm,),
        in_specs=[_row_spec(tm, D)] * 3, out_specs=[_row_spec(tm, D)] * 2,
        out_shape=[jax.ShapeDtypeStruct((T, D), BF16)] * 2, compiler_params=_params(1))(dx, pg, e)


def conv_bwd(df, up, convw, convb3, l, *, name, tm=512):
    T = up.shape[0]
    tc = CONV_TC
    nc = DFF // tc
    hb = tm // CONV_HALO
    nt = T // tm
    rows = tm + 2 * CONV_HALO
    own = slice(CONV_HALO, CONV_HALO + tm)

    def body(df_ref, dfn_ref, uo_ref, uop_ref, uon_ref, up_ref, upp_ref, upn_ref, wo_ref, wp_ref, bo_ref, bp_ref,
             dup_ref, dcw_ref, dcb_ref):
        jj, i = pl.program_id(0), pl.program_id(1)

        @pl.when(i == 0)
        def _():
            dcw_ref[...] = jnp.zeros_like(dcw_ref)
            dcb_ref[...] = jnp.zeros_like(dcb_ref)

        def ext_of(c_ref, p_ref, n_ref):
            return jnp.concatenate([jnp.where(i == 0, 0.0, p_ref[...]), c_ref[...],
                                    jnp.where(i == nt - 1, 0.0, n_ref[...])], axis=0)

        eo = ext_of(uo_ref, uop_ref, uon_ref)
        co = _conv(eo, wo_ref, bo_ref)
        cp = _conv(ext_of(up_ref, upp_ref, upn_ref), wp_ref, bp_ref)
        df_ext = jnp.concatenate([jnp.zeros((CONV_HALO, tc), F32), df_ref[...],
                                  jnp.where(i == nt - 1, 0.0, dfn_ref[...])], axis=0)

        def finish(dc):
            dup = wo_ref[2:3, :] * dc + wo_ref[1:2, :] * pltpu.roll(dc, rows - 1, 0)
            dup = dup + wo_ref[0:1, :] * pltpu.roll(dc, rows - 2, 0)
            dup_ref[...] = dup[own, :].astype(BF16)
            dco = dc[own, :]
            dcb_ref[0:1, :] += jnp.sum(dco, axis=0, keepdims=True)
            dcw_ref[0:1, :] += jnp.sum(dco * pltpu.roll(eo, 2, 0)[own, :], axis=0, keepdims=True)
            dcw_ref[1:2, :] += jnp.sum(dco * pltpu.roll(eo, 1, 0)[own, :], axis=0, keepdims=True)
            dcw_ref[2:3, :] += jnp.sum(dco * eo[own, :], axis=0, keepdims=True)

        @pl.when(jj < nc)
        def _():
            finish(df_ext * cp * _gelu_grad(co))

        @pl.when(jj >= nc)
        def _():
            finish(df_ext * _gelu(cp))

    def partner(jj):
        return (jj + nc) % (2 * nc)

    def nxt(i):
        return jnp.minimum((i + 1) * hb, T // CONV_HALO - 1)

    def prv(i):
        return jnp.maximum(i * hb - 1, 0)

    in_specs = [
        pl.BlockSpec((tm, tc), lambda jj, i: (i, jj % nc)),
        pl.BlockSpec((CONV_HALO, tc), lambda jj, i: (nxt(i), jj % nc)),
        pl.BlockSpec((tm, tc), lambda jj, i: (i, jj)),
        pl.BlockSpec((CONV_HALO, tc), lambda jj, i: (prv(i), jj)),
        pl.BlockSpec((CONV_HALO, tc), lambda jj, i: (nxt(i), jj)),
        pl.BlockSpec((tm, tc), lambda jj, i: (i, partner(jj))),
        pl.BlockSpec((CONV_HALO, tc), lambda jj, i: (prv(i), partner(jj))),
        pl.BlockSpec((CONV_HALO, tc), lambda jj, i: (nxt(i), partner(jj))),
        pl.BlockSpec((None, CONV_ROWS, tc), lambda jj, i: (l, 0, jj)),
        pl.BlockSpec((None, CONV_ROWS, tc), lambda jj, i: (l, 0, partner(jj))),
        pl.BlockSpec((None, 1, tc), lambda jj, i: (l, 0, jj)),
        pl.BlockSpec((None, 1, tc), lambda jj, i: (l, 0, partner(jj))),
    ]
    return pl.pallas_call(
        body, name=name, grid=(2 * nc, nt), in_specs=in_specs,
        out_specs=[pl.BlockSpec((tm, tc), lambda jj, i: (i, jj)),
                   pl.BlockSpec((8, tc), lambda jj, i: (0, jj)),
                   pl.BlockSpec((8, tc), lambda jj, i: (0, jj))],
        out_shape=[jax.ShapeDtypeStruct((T, 2 * DFF), BF16), jax.ShapeDtypeStruct((8, 2 * DFF), F32),
                   jax.ShapeDtypeStruct((8, 2 * DFF), F32)],
        compiler_params=_params(2))(df, df, up, up, up, up, up, up, convw, convw, convb3, convb3)


def gate_bwd(dmo, z, yab, *, name, tm=512):
    T = z.shape[0]

    def body(dmo_ref, zg_ref, y_ref, dz_ref, dy_ref):
        g = jax.nn.sigmoid(zg_ref[...])
        dmo_v = dmo_ref[...]
        dy_ref[...] = (dmo_v * g).astype(BF16)
        dz_ref[...] = (dmo_v * y_ref[...] * g * (1.0 - g)).astype(BF16)

    return pl.pallas_call(
        body, name=name, grid=(T // tm, 2),
        in_specs=[pl.BlockSpec((tm, D), lambda i, s: (i, 0)),
                  pl.BlockSpec((tm, D), lambda i, s: (i, 3 + s)),
                  pl.BlockSpec((tm, D), lambda i, s: (i, s))],
        out_specs=[pl.BlockSpec((tm, D), lambda i, s: (i, 3 + s)),
                   pl.BlockSpec((tm, D), lambda i, s: (i, s))],
        out_shape=[jax.ShapeDtypeStruct((T, 5 * D), BF16), jax.ShapeDtypeStruct((T, 2 * D), BF16)],
        compiler_params=_params(2))(dmo, z, yab)


def mixer_bwd(da, ds, z, dz, wpool, scale3, g3, wsm, wsmT, bT, l, *, name, tm=256):
    T = z.shape[0]
    hb = tm // POOL_HALO
    nt = T // tm

    def body(da_ref, dan_ref, ds_ref, zp_ref, zpp_ref, zu_ref, zv_ref, wp_ref, sc_ref, g_ref, ws_ref, wst_ref,
             b_ref, dzin_ref, dz_ref, dwp_ref, dsc_ref, dws_ref, dbt_ref, dgs_ref, mixed_scr, dvn_scr, db_scr):
        del dzin_ref
        i = pl.program_id(0)

        @pl.when(i == 0)
        def _():
            dwp_ref[...] = jnp.zeros_like(dwp_ref)
            dsc_ref[...] = jnp.zeros_like(dsc_ref)
            dws_ref[...] = jnp.zeros_like(dws_ref)
            dgs_ref[...] = jnp.zeros_like(dgs_ref)
            db_scr[...] = jnp.zeros_like(db_scr)

        zv_p = zp_ref[...]
        ext = jnp.concatenate([jnp.where(i == 0, 0.0, zpp_ref[...]), zv_p], axis=0)
        da_v = da_ref[...]
        da_ext = jnp.concatenate([da_v, jnp.where(i == nt - 1, 0.0, dan_ref[...])], axis=0)
        t = i * tm + lax.broadcasted_iota(jnp.int32, (tm, 1), 0)
        t_ext = i * tm + lax.broadcasted_iota(jnp.int32, (tm + POOL_HALO, 1), 0)
        for g, w in enumerate(POOL_WINDOWS):
            sl = slice(g * PG, (g + 1) * PG)
            pooled = _pooled(ext, zv_p, t, g, w).astype(BF16)
            q = jnp.dot(pooled, wp_ref[g], preferred_element_type=F32)
            dsc_ref[0:1, sl] += jnp.sum(da_v[:, sl] * q, axis=0, keepdims=True)
            dq_ext = (da_ext[:, sl] * sc_ref[:, sl]).astype(BF16)
            dwp_ref[g] += lax.dot_general(pooled, dq_ext[:tm, :], (((0,), (0,)), ((), ())),
                                          preferred_element_type=F32)
            dpool = lax.dot_general(dq_ext, wp_ref[g], (((1,), (1,)), ((), ())), preferred_element_type=F32)
            spread = _winsum_fwd(dpool / jnp.minimum(t_ext + 1, w).astype(F32), w)
            dz_ref[:, sl] = (spread[:tm, :] - dpool[:tm, :]).astype(BF16)

        zu, zv, ds_v, gain = zu_ref[...], zv_ref[...], ds_ref[...], g_ref[...]
        gu, gv = _gelu(zu), _gelu(zv)
        rv = lax.rsqrt(jnp.mean(gv * gv, axis=-1, keepdims=True) + EPS)
        vh = gv * rv
        vn = (vh * gain).astype(BF16)
        dmix = ds_v * gu
        dmix_b = dmix.astype(BF16)
        for n in range(tm // CHUNK):
            r = slice(n * CHUNK, (n + 1) * CHUNK)
            db_scr[...] += dmix[r, :]
            for h in range(HEADS):
                cs = slice(h * CHUNK, (h + 1) * CHUNK)
                mixed_scr[r, cs] = jnp.dot(ws_ref[h], vn[r, cs], preferred_element_type=F32) + b_ref[:, h:h + 1]
                dws_ref[h] += lax.dot_general(dmix_b[r, cs], vn[r, cs], (((1,), (1,)), ((), ())),
                                              preferred_element_type=F32)
                dvn_scr[r, cs] = jnp.dot(wst_ref[h], dmix_b[r, cs], preferred_element_type=F32)
        dz_ref[:, D:2 * D] = (ds_v * mixed_scr[...] * _gelu_grad(zu)).astype(BF16)
        dvn = dvn_scr[...]
        dgs_ref[0:1, :] += jnp.sum(dvn * vh, axis=0, keepdims=True)
        dvg = dvn * gain
        dgv = rv * (dvg - vh * jnp.mean(dvg * vh, axis=-1, keepdims=True))
        dz_ref[:, 2 * D:3 * D] = (dgv * _gelu_grad(zv)).astype(BF16)

        @pl.when(i == nt - 1)
        def _():
            tril = (lax.broadcasted_iota(jnp.int32, (CHUNK, CHUNK), 0)
                    >= lax.broadcasted_iota(jnp.int32, (CHUNK, CHUNK), 1)).astype(F32)
            for h in range(HEADS):
                dws_ref[h] = dws_ref[h] * tril
                dbt_ref[:, h:h + 1] = jnp.sum(db_scr[:, h * CHUNK:(h + 1) * CHUNK], axis=1, keepdims=True)

    const4 = lambda i: (l, 0, 0, 0)
    in_specs = [
        _row_spec(tm, D),
        pl.BlockSpec((POOL_HALO, D), lambda i: (jnp.minimum((i + 1) * hb, T // POOL_HALO - 1), 0)),
        _row_spec(tm, D),
        _row_spec(tm, D, 0),
        pl.BlockSpec((POOL_HALO, D), lambda i: (jnp.maximum(i * hb - 1, 0), 0)),
        _row_spec(tm, D, 1), _row_spec(tm, D, 2),
        pl.BlockSpec((None, 4, PG, PG), const4),
        _gain_spec(l), _gain_spec(l),
        pl.BlockSpec((None, HEADS, CHUNK, CHUNK), const4),
        pl.BlockSpec((None, HEADS, CHUNK, CHUNK), const4),
        pl.BlockSpec((None, CHUNK, HEADS), lambda i: (l, 0, 0)),
        ANY,
    ]
    out_specs = [
        pl.BlockSpec((tm, 3 * D), lambda i: (i, 0)),
        pl.BlockSpec((4, PG, PG), lambda i: (0, 0, 0)),
        pl.BlockSpec((8, D), lambda i: (0, 0)),
        pl.BlockSpec((HEADS, CHUNK, CHUNK), lambda i: (0, 0, 0)),
        pl.BlockSpec((CHUNK, HEADS), lambda i: (0, 0)),
        pl.BlockSpec((8, D), lambda i: (0, 0)),
    ]
    out_shape = [
        jax.ShapeDtypeStruct((T, 5 * D), BF16), jax.ShapeDtypeStruct((4, PG, PG), F32),
        jax.ShapeDtypeStruct((8, D), F32), jax.ShapeDtypeStruct((HEADS, CHUNK, CHUNK), F32),
        jax.ShapeDtypeStruct((CHUNK, HEADS), F32), jax.ShapeDtypeStruct((8, D), F32),
    ]
    return pl.pallas_call(
        body, name=name, grid=(nt,), in_specs=in_specs, out_specs=out_specs, out_shape=out_shape,
        scratch_shapes=[pltpu.VMEM((tm, D), F32), pltpu.VMEM((tm, D), F32), pltpu.VMEM((CHUNK, D), F32)],
        input_output_aliases={13: 0}, compiler_params=_params(1))(
            da, da, ds, z, z, z, z, wpool, scale3, g3, wsm, wsmT, bT, dz)


def _row_tile(rows, cols, sub):
    cap = max(sub, (2 * 1024 * 1024) // (4 * cols))
    best = None
    for tr in range(sub, min(rows, cap) + 1, sub):
        if rows % tr == 0:
            best = tr
    return best or rows


def elementwise(fn, ins, out_dtypes, *, name, row_blk_offs=None, rows=None, uses_mesh=False):
    cols = ins[0].shape[1]
    rows = rows or ins[0].shape[0]
    tr = _row_tile(rows, cols, 16)
    offs = row_blk_offs or [0] * len(ins)
    n_in = len(ins)

    def body(*refs):
        outs = fn(*[r[...] for r in refs[:n_in]])
        for o_ref, o in zip(refs[n_in:], outs):
            o_ref[...] = o.astype(o_ref.dtype)

    del uses_mesh
    return pl.pallas_call(
        body, name=name, grid=(rows // tr,),
        in_specs=[pl.BlockSpec((tr, cols), functools.partial(lambda i, o: (i + o * (rows // tr), 0), o=o))
                  for o in offs],
        out_specs=[pl.BlockSpec((tr, cols), lambda i: (i, 0)) for _ in out_dtypes],
        out_shape=[jax.ShapeDtypeStruct((rows, cols), dt) for dt in out_dtypes],
        compiler_params=_params(1))(*ins)


def _adamw(w, g, m, v):
    m = ADAM_B1 * m + (1.0 - ADAM_B1) * g
    v = ADAM_B2 * v + (1.0 - ADAM_B2) * jnp.square(g)
    m_hat = m / (1.0 - ADAM_B1 ** ADAM_STEP)
    v_hat = v / (1.0 - ADAM_B2 ** ADAM_STEP)
    delta = -ADAM_LR * (m_hat / (jnp.sqrt(v_hat) + ADAM_EPS) + ADAM_WD * w)
    return delta, m, v


def _view2d(a):
    return a.reshape(-1, a.shape[-1])


def _place():
    x, y, c = lax.axis_index("x"), lax.axis_index("y"), lax.axis_index("c")
    others = [(1 - x, y), (x, 1 - y), (1 - x, 1 - y)]
    return x, y, c, 2 * x + y, others


def _remote(src, dst, send_sems, recv_sems, k, to):
    return pltpu.make_async_remote_copy(src_ref=src, dst_ref=dst, send_sem=send_sems.at[k], recv_sem=recv_sems.at[k],
                                        device_id=to, device_id_type=MESH)


def all_gather_weights(shards):
    specs = list(shards)
    n = len(specs)

    def body(*refs):
        srcs, outs = refs[:n], refs[n:2 * n]
        send_sems, recv_sems, local_sems = refs[2 * n:]
        x, y, c, j, others = _place()
        sib = (x, y, 1 - c)
        pending = []
        for t, (_, _, axis, size) in enumerate(specs):
            for lyr in range(2):
                cp = pltpu.make_async_copy(srcs[t].at[lyr], _block(outs[t].at[lyr], axis, j, size),
                                           local_sems.at[2 * t + lyr])
                cp.start()
                pending.append(cp.wait)
            for k, (ox, oy) in enumerate(others):
                cp = _remote(srcs[t].at[c], _block(outs[t].at[c], axis, j, size), send_sems, recv_sems,
                             6 * t + k, (ox, oy, c))
                cp.start()
                pending.append(cp.wait_send)
        for t, (_, _, axis, size) in enumerate(specs):
            for k, (ox, oy) in enumerate(others):
                landed = _block(outs[t].at[c], axis, 2 * ox + oy, size)
                _remote(landed, landed, send_sems, recv_sems, 6 * t + k, (ox, oy, c)).wait_recv()
                fwd = _remote(landed, landed, send_sems, recv_sems, 6 * t + 3 + k, sib)
                fwd.start()
                pending.append(fwd.wait_send)
        for t, (_, _, axis, size) in enumerate(specs):
            for k, (ox, oy) in enumerate(others):
                got = _block(outs[t].at[1 - c], axis, 2 * ox + oy, size)
                _remote(got, got, send_sems, recv_sems, 6 * t + 3 + k, sib).wait_recv()
        for wait in pending:
            wait()

    arrays = [a for (_, a, _, _) in specs]
    out_shape = [jax.ShapeDtypeStruct((2,) + tuple(size * N_CHIPS if ax == axis else s
                                                   for ax, s in enumerate(a.shape[1:])), a.dtype)
                 for (_, a, axis, size) in specs]
    return pl.pallas_call(
        body, name="all_gather_weights", in_specs=[ANY] * n, out_specs=[ANY] * n, out_shape=out_shape,
        scratch_shapes=[pltpu.SemaphoreType.DMA((6 * n,)), pltpu.SemaphoreType.DMA((6 * n,)),
                        pltpu.SemaphoreType.DMA((2 * n,))],
        compiler_params=pltpu.CompilerParams(has_side_effects=True))(*arrays)


def swap_layers_with_sibling(grads):
    n = len(grads)

    def body(*refs):
        srcs, outs = refs[:n], refs[n:2 * n]
        send_sems, recv_sems = refs[2 * n:]
        x, y, c, _, _ = _place()
        cps = [_remote(srcs[t].at[1 - c], outs[t], send_sems, recv_sems, t, (x, y, 1 - c)) for t in range(n)]
        for cp in cps:
            cp.start()
        for cp in cps:
            cp.wait()

    return pl.pallas_call(
        body, name="swap_layers_with_sibling", in_specs=[ANY] * n, out_specs=[ANY] * n,
        out_shape=[jax.ShapeDtypeStruct(g.shape[1:], g.dtype) for g in grads],
        scratch_shapes=[pltpu.SemaphoreType.DMA((n,)), pltpu.SemaphoreType.DMA((n,))],
        compiler_params=pltpu.CompilerParams(has_side_effects=True))(*grads)


def scatter_to_chips(parts, geom):
    n = len(parts)

    def body(*refs):
        srcs, outs = refs[:n], refs[n:2 * n]
        send_sems, recv_sems, local_sems = refs[2 * n:]
        x, y, c, j, others = _place()
        pending = []
        for t, (axis, size) in enumerate(geom):
            cp = pltpu.make_async_copy(_block(srcs[t], axis, j, size), outs[t].at[j], local_sems.at[t])
            cp.start()
            pending.append(cp.wait)
            for k, (ox, oy) in enumerate(others):
                cp = _remote(_block(srcs[t], axis, 2 * ox + oy, size), outs[t].at[j], send_sems, recv_sems,
                             3 * t + k, (ox, oy, c))
                cp.start()
                pending.append(cp.wait_send)
        for t in range(n):
            for k, (ox, oy) in enumerate(others):
                slot = outs[t].at[2 * ox + oy]
                _remote(slot, slot, send_sems, recv_sems, 3 * t + k, (ox, oy, c)).wait_recv()
        for wait in pending:
            wait()

    out_shape = [jax.ShapeDtypeStruct((N_CHIPS,) + _shard_shape(p.shape, axis, size), p.dtype)
                 for p, (axis, size) in zip(parts, geom)]
    return pl.pallas_call(
        body, name="scatter_to_chips", in_specs=[ANY] * n, out_specs=[ANY] * n, out_shape=out_shape,
        scratch_shapes=[pltpu.SemaphoreType.DMA((3 * n,)), pltpu.SemaphoreType.DMA((3 * n,)),
                        pltpu.SemaphoreType.DMA((n,))],
        compiler_params=pltpu.CompilerParams(has_side_effects=True))(*parts)


def share_with_sibling(halves):
    n = len(halves)

    def body(*refs):
        srcs, outs = refs[:n], refs[n:2 * n]
        send_sems, recv_sems, local_sems = refs[2 * n:]
        x, y, c, _, _ = _place()
        local = [pltpu.make_async_copy(srcs[t], outs[t].at[c], local_sems.at[t]) for t in range(n)]
        cps = [_remote(srcs[t], outs[t].at[c], send_sems, recv_sems, t, (x, y, 1 - c)) for t in range(n)]
        for cp in local + cps:
            cp.start()
        for t in range(n):
            got = outs[t].at[1 - c]
            _remote(got, got, send_sems, recv_sems, t, (x, y, 1 - c)).wait_recv()
        for cp in cps:
            cp.wait_send()
        for cp in local:
            cp.wait()

    return pl.pallas_call(
        body, name="share_with_sibling", in_specs=[ANY] * n, out_specs=[ANY] * n,
        out_shape=[jax.ShapeDtypeStruct((2,) + h.shape, h.dtype) for h in halves],
        scratch_shapes=[pltpu.SemaphoreType.DMA((n,)), pltpu.SemaphoreType.DMA((n,)),
                        pltpu.SemaphoreType.DMA((n,))],
        compiler_params=pltpu.CompilerParams(has_side_effects=True))(*halves)


def all_reduce_small(s):
    rows = s.shape[0]
    half = rows // 2
    assert half % 8 == 0

    def body(s_ref, o_ref, a_ref, b_ref, p_ref, send_sems, recv_sems):
        x, y, c, j, others = _place()
        sib = (x, y, 1 - c)
        swap = _remote(s_ref, a_ref, send_sems, recv_sems, 0, sib)
        swap.start()
        swap.wait()
        p_ref[...] = s_ref[...] + a_ref[...]
        mine = pl.ds(pl.multiple_of(c * half, 8), half)
        b_ref[j] = p_ref[mine, :]
        cps = [_remote(p_ref.at[mine, :], b_ref.at[j], send_sems, recv_sems, 1 + k, (ox, oy, c))
               for k, (ox, oy) in enumerate(others)]
        for cp in cps:
            cp.start()
        for k, (ox, oy) in enumerate(others):
            slot = b_ref.at[2 * ox + oy]
            _remote(slot, slot, send_sems, recv_sems, 1 + k, (ox, oy, c)).wait_recv()
        for cp in cps:
            cp.wait_send()
        o_ref[mine, :] = ((b_ref[0] + b_ref[1]) + b_ref[2]) + b_ref[3]
        back = _remote(o_ref.at[mine, :], o_ref.at[mine, :], send_sems, recv_sems, 4, sib)
        back.start()
        back.wait_send()
        theirs = pl.ds(pl.multiple_of((1 - c) * half, 8), half)
        _remote(o_ref.at[theirs, :], o_ref.at[theirs, :], send_sems, recv_sems, 4, sib).wait_recv()

    vmem = pl.BlockSpec(memory_space=pltpu.VMEM)
    return pl.pallas_call(
        body, name="all_reduce_small", in_specs=[vmem], out_specs=vmem,
        out_shape=jax.ShapeDtypeStruct((rows, LANES), F32),
        scratch_shapes=[pltpu.VMEM((rows, LANES), F32), pltpu.VMEM((N_CHIPS, half, LANES), F32),
                        pltpu.VMEM((rows, LANES), F32), pltpu.SemaphoreType.DMA((5,)),
                        pltpu.SemaphoreType.DMA((5,))],
        compiler_params=pltpu.CompilerParams(vmem_limit_bytes=VMEM_LIMIT, has_side_effects=True))(s)


def pair_sum(g, got, *, name):
    shape = g.shape[1:]
    cols = shape[-1]
    g2 = g.reshape(2, -1, cols)
    got2 = got.reshape(-1, cols)
    rows = got2.shape[0]
    tr = _row_tile(rows, cols, 16)

    def body(g0_ref, g1_ref, got_ref, o_ref):
        own = jnp.where(lax.axis_index("c") == 0, g0_ref[...].astype(F32), g1_ref[...].astype(F32))
        o_ref[...] = (own + got_ref[...].astype(F32)).astype(BF16)

    out = pl.pallas_call(
        body, name=name, grid=(rows // tr,),
        in_specs=[pl.BlockSpec((None, tr, cols), lambda i: (0, i, 0)),
                  pl.BlockSpec((None, tr, cols), lambda i: (1, i, 0)),
                  pl.BlockSpec((tr, cols), lambda i: (i, 0))],
        out_specs=pl.BlockSpec((tr, cols), lambda i: (i, 0)),
        out_shape=jax.ShapeDtypeStruct((rows, cols), BF16), compiler_params=_params(1))(g2, g2, got2)
    return out.reshape(shape)


def chip_sum(slots, *, name):
    shape = slots.shape[1:]
    s2 = slots.reshape(-1, shape[-1])
    rows = s2.shape[0] // N_CHIPS

    def fn(a, b, c, d):
        return [((a.astype(F32) + b.astype(F32)) + c.astype(F32)) + d.astype(F32)]

    out, = elementwise(fn, [s2] * N_CHIPS, [F32], name=name, row_blk_offs=list(range(N_CHIPS)), rows=rows)
    return out.reshape(shape)


def _local_step(x, p2, tgt, W, small):
    T = x.shape[0]
    as3 = lambda a: a.reshape(2, 1, a.shape[-1])
    mix3, scale3, sgu3 = as3(small["mix_norm"]), as3(small["pool_scale"]), as3(small["sgu_norm"])
    ffn3, ple3, convb3 = as3(small["ffn_norm"]), as3(small["ple_norm"]), as3(small["conv_b"])
    tril = jnp.tril(jnp.ones((CHUNK, CHUNK), F32))
    ws_masked = small["w_spatial"] * tril
    wsm = ws_masked.astype(BF16)
    wsmT = jnp.swapaxes(ws_masked, -1, -2).astype(BF16)
    bT = jnp.swapaxes(small["b_spatial"], -1, -2)
    final3 = small["final_norm"].reshape(1, D)

    saved = []
    for l in range(2):
        n = lambda s: f"{s}_l{l}"
        hb = norm_fwd(x, mix3, l, name=n("mix_norm_fwd"))
        z = mm_nn(hb, W["w_in"], l, name=n("in_proj"), rows=T, tn=1280)
        a_in = pool_fwd(z, W["w_pool"], scale3, l, name=n("pool_fwd"))
        s_in = sgu_fwd(z, sgu3, wsm, bT, l, name=n("sgu_fwd"))
        yab = mm_nn(a_in, W["w_branch_a"], l, name=n("branch_a"), rows=T, out_cols=2 * D)
        yab = mm_nn(s_in, W["w_branch_b"], l, name=n("branch_b"), rows=T, out=yab, out_cols=2 * D, out_col_off=D)
        mo = gate_fwd(z, yab, name=n("gate_fwd"))
        x1 = mm_nn(mo, W["w_out"], l, name=n("out_proj"), rows=T, resid=x)
        h2b = norm_fwd(x1, ffn3, l, name=n("ffn_norm_fwd"))
        up = mm_nn(h2b, W["w_up"], l, name=n("up_proj"), rows=T, tn=1408)
        f = conv_fwd(up, W["conv_w"], convb3, l, name=n("conv_fwd"))
        x2 = mm_nn(f, W["w_down"], l, name=n("down_proj"), rows=T, resid=x1, tk=1408)
        h3b = norm_fwd(x2, ple3, l, name=n("ple_norm_fwd"))
        pg = mm_nn(h3b, W["w_ple_gate"], l, name=n("ple_gate_proj"), rows=T)
        e = mm_nn(p2, W["w_ple"], l, name=n("ple_proj"), rows=T, a_row_blk_off=l * (T // 512))
        x3 = ple_fwd(x2, pg, e, name=n("ple_fwd"))
        saved.append(dict(x=x, hb=hb, z=z, a_in=a_in, s_in=s_in, yab=yab, mo=mo, x1=x1, h2b=h2b, up=up, f=f,
                          x2=x2, h3b=h3b, pg=pg, e=e))
        x = x3

    loss_acc, dx, dg_final = loss_head(x, final3, tgt, name="loss_head")

    G = {name: None for (name, _, _, _) in BIG_GRAD}
    small_grads = [None, None]
    for l in (1, 0):
        n = lambda s: f"{s}_l{l}"
        a = saved[l]
        de, dpg = ple_bwd(dx, a["pg"], a["e"], name=n("ple_bwd"))
        G["w_ple"] = mm_tn(p2, de, l, name=n("d_w_ple"), rows=T, ka=PG, nb=D, out=G["w_ple"],
                           a_row_blk_off=l * (T // 512))
        G["w_ple_gate"] = mm_tn(a["h3b"], dpg, l, name=n("d_w_ple_gate"), rows=T, ka=D, nb=D, out=G["w_ple_gate"])
        dh3 = mm_nt(dpg, W["w_ple_gate"], l, name=n("d_ple_norm_out"), rows=T)
        dx2, dg_ple = norm_bwd(dh3, a["x2"], ple3, l, dx, name=n("ple_norm_bwd"))
        df = mm_nt(dx2, W["w_down"], l, name=n("d_ffn_act"), rows=T, tn=1408)
        G["w_down"] = mm_tn(a["f"], dx2, l, name=n("d_w_down"), rows=T, ka=DFF, nb=D, out=G["w_down"], tm=1408)
        dup, dcw, dcb = conv_bwd(df, a["up"], W["conv_w"], convb3, l, name=n("conv_bwd"))
        G["w_up"] = mm_tn(a["h2b"], dup, l, name=n("d_w_up"), rows=T, ka=D, nb=2 * DFF, out=G["w_up"], tn=1408)
        dh2 = mm_nt(dup, W["w_up"], l, name=n("d_ffn_norm_out"), rows=T, tk=1408)
        dx1, dg_ffn = norm_bwd(dh2, a["x1"], ffn3, l, dx2, name=n("ffn_norm_bwd"))
        dmo = mm_nt(dx1, W["w_out"], l, name=n("d_gated"), rows=T)
        G["w_out"] = mm_tn(a["mo"], dx1, l, name=n("d_w_out"), rows=T, ka=D, nb=D, out=G["w_out"])
        dz, dyab = gate_bwd(dmo, a["z"], a["yab"], name=n("gate_bwd"))
        G["w_branch_a"] = mm_tn(a["a_in"], dyab, l, name=n("d_w_branch_a"), rows=T, ka=D, nb=D,
                                out=G["w_branch_a"])
        G["w_branch_b"] = mm_tn(a["s_in"], dyab, l, name=n("d_w_branch_b"), rows=T, ka=D, nb=D,
                                out=G["w_branch_b"], b_col_off=D)
        da = mm_nt(dyab, W["w_branch_a"], l, name=n("d_pool_out"), rows=T, kdim=D)
        ds = mm_nt(dyab, W["w_branch_b"], l, name=n("d_sgu_out"), rows=T, kdim=D, a_col_off=D)
        dz, dwp, dsc, dws, dbt, dgs = mixer_bwd(da, ds, a["z"], dz, W["w_pool"], scale3, sgu3, wsm, wsmT, bT, l,
                                                name=n("mixer_bwd"))
        G["w_in"] = mm_tn(a["hb"], dz, l, name=n("d_w_in"), rows=T, ka=D, nb=5 * D, out=G["w_in"], tn=1280)
        dh = mm_nt(dz, W["w_in"], l, name=n("d_mix_norm_out"), rows=T, tk=1280)
        dx, dg_mix = norm_bwd(dh, a["x"], mix3, l, dx1, name=n("mix_norm_bwd"))
        small_grads[l] = dict(
            mix_norm=dg_mix[0], pool_scale=dsc[0], sgu_norm=dgs[0], w_spatial=dws, b_spatial=dbt.T,
            ffn_norm=dg_ffn[0], conv_b=dcb[0], ple_norm=dg_ple[0], conv_w=dcw[:3], w_pool=dwp)
    return loss_acc, dx, G, small_grads, dg_final[0]


SMALL_ORDER = ("mix_norm", "pool_scale", "sgu_norm", "w_spatial", "b_spatial", "ffn_norm", "conv_b", "ple_norm",
               "conv_w", "w_pool")


def _pack_rows(pieces, row_multiple):
    flat = jnp.concatenate([a.reshape(-1) for a in pieces])
    rows = -(-flat.shape[0] // LANES)
    rows = -(-rows // row_multiple) * row_multiple
    return jnp.pad(flat, (0, rows * LANES - flat.shape[0])).reshape(rows, LANES)


def _unpack(flat, shapes):
    out, off = [], 0
    for shp in shapes:
        size = 1
        for s in shp:
            size *= s
        out.append(flat[off:off + size].reshape(shp))
        off += size
    return out


def kernel(x, p, mix_norm, w_in, w_pool, pool_scale, sgu_norm, w_spatial, b_spatial, w_branch_a, w_branch_b, w_out, ffn_norm, w_up, conv_w, conv_b, w_down, ple_norm, w_ple_gate, w_ple, final_norm, loss_target, m_mix_norm, m_w_in, m_w_pool, m_pool_scale, m_sgu_norm, m_w_spatial, m_b_spatial, m_w_branch_a, m_w_branch_b, m_w_out, m_ffn_norm, m_w_up, m_conv_w, m_conv_b, m_w_down, m_ple_norm, m_w_ple_gate, m_w_ple, m_final_norm, v_mix_norm, v_w_in, v_w_pool, v_pool_scale, v_sgu_norm, v_w_spatial, v_b_spatial, v_w_branch_a, v_w_branch_b, v_w_out, v_ffn_norm, v_w_up, v_conv_w, v_conv_b, v_w_down, v_ple_norm, v_w_ple_gate, v_w_ple, v_final_norm):
    names = ["mix_norm", "w_in", "w_pool", "pool_scale", "sgu_norm", "w_spatial", "b_spatial", "w_branch_a",
             "w_branch_b", "w_out", "ffn_norm", "w_up", "conv_w", "conv_b", "w_down", "ple_norm", "w_ple_gate",
             "w_ple", "final_norm"]
    w = dict(zip(names, [mix_norm, w_in, w_pool, pool_scale, sgu_norm, w_spatial, b_spatial, w_branch_a, w_branch_b,
                         w_out, ffn_norm, w_up, conv_w, conv_b, w_down, ple_norm, w_ple_gate, w_ple, final_norm]))
    m = dict(zip(names, [m_mix_norm, m_w_in, m_w_pool, m_pool_scale, m_sgu_norm, m_w_spatial, m_b_spatial,
                         m_w_branch_a, m_w_branch_b, m_w_out, m_ffn_norm, m_w_up, m_conv_w, m_conv_b, m_w_down,
                         m_ple_norm, m_w_ple_gate, m_w_ple, m_final_norm]))
    v = dict(zip(names, [v_mix_norm, v_w_in, v_w_pool, v_pool_scale, v_sgu_norm, v_w_spatial, v_b_spatial,
                         v_w_branch_a, v_w_branch_b, v_w_out, v_ffn_norm, v_w_up, v_conv_w, v_conv_b, v_w_down,
                         v_ple_norm, v_w_ple_gate, v_w_ple, v_final_norm]))
    T = x.shape[1]
    chip = 2 * lax.axis_index("x") + lax.axis_index("y")

    shards = []
    for name, _, axis, size in BIG:
        src = w[name]
        b16, = elementwise(lambda a: [a], [_view2d(src)], [BF16], name=f"to_bf16_{name}")
        shards.append((name, b16.reshape(src.shape), axis, size))
    conv_w8 = jnp.pad(conv_w, ((0, 0), (0, CONV_ROWS - conv_w.shape[1]), (0, 0)))
    shards.append(("conv_w", conv_w8, 1, conv_w.shape[2]))
    gathered = all_gather_weights(shards)
    W = {name: g for (name, _, _, _), g in zip(shards, gathered)}

    small = {k: w[k] for k in ("mix_norm", "pool_scale", "sgu_norm", "w_spatial", "b_spatial", "ffn_norm",
                               "conv_b", "ple_norm", "final_norm")}
    loss_acc, dx, G, small_grads, dg_final = _local_step(
        x.reshape(T, D), p.reshape(2 * T, p.shape[-1]), loss_target.reshape(T, D), W, small)
    loss = lax.psum(loss_acc[0, 0], ("x", "y", "c"))

    glist = [G[name] for (name, _, _, _) in BIG_GRAD]
    got = swap_layers_with_sibling(glist)
    parts = [pair_sum(g, r, name=f"pair_sum_{name}") for (name, _, _, _), g, r in zip(BIG_GRAD, glist, got)]
    slots = scatter_to_chips(parts, [(axis, size) for (_, _, axis, size) in BIG_GRAD])
    halves = [chip_sum(s, name=f"chip_sum_{name}") for (name, _, _, _), s in zip(BIG_GRAD, slots)]
    full = share_with_sibling(halves)
    grads = {name: g for (name, _, _, _), g in zip(BIG_GRAD, full)}

    pieces = [small_grads[l][k] for l in range(2) for k in SMALL_ORDER] + [dg_final]
    shapes = [a.shape for a in pieces]
    total = all_reduce_small(_pack_rows(pieces, 16)).reshape(-1)
    summed = _unpack(total, shapes)
    per_layer = {k: jnp.stack([summed[i], summed[len(SMALL_ORDER) + i]]) for i, k in enumerate(SMALL_ORDER)}
    for k in ("mix_norm", "pool_scale", "sgu_norm", "w_spatial", "b_spatial", "ffn_norm", "conv_b", "ple_norm"):
        grads[k] = per_layer[k]
    grads["final_norm"] = summed[-1]
    cw = conv_w.shape[2]
    grads["conv_w"] = lax.dynamic_slice_in_dim(per_layer["conv_w"], chip * cw, cw, axis=2)
    ps = w_pool.shape[2]
    grads["w_pool"] = lax.dynamic_slice_in_dim(per_layer["w_pool"], chip * ps, ps, axis=2)

    delta, new_m, new_v = {}, {}, {}
    big_names = [name for (name, _, _, _) in BIG]
    for name in big_names:
        shp = w[name].shape
        d_, m_, v_ = elementwise(_adamw, [_view2d(a) for a in (w[name], grads[name], m[name], v[name])],
                                 [F32, F32, F32], name=f"adamw_{name}")
        delta[name], new_m[name], new_v[name] = d_.reshape(shp), m_.reshape(shp), v_.reshape(shp)
    small_names = [k for k in names if k not in big_names]
    small_shapes = [w[k].shape for k in small_names]
    packed = [_pack_rows([src[k] for k in small_names], 8) for src in (w, grads, m, v)]
    outs = elementwise(_adamw, packed, [F32, F32, F32], name="adamw_small")
    for dst, o in zip((delta, new_m, new_v), outs):
        for k, a in zip(small_names, _unpack(o.reshape(-1), small_shapes)):
            dst[k] = a

    return (loss, dx.reshape(1, T, D), *[grads[k] for k in names], *[delta[k] for k in names],
            *[new_m[k] for k in names], *[new_v[k] for k in names])
```

```python
import functools

import jax
import jax.numpy as jnp
from jax import lax
from jax.experimental import pallas as pl
from jax.experimental.pallas import tpu as pltpu

F32 = jnp.float32
BF16 = jnp.bfloat16
EPS = 1e-6
D = 1024
POOL_WINDOWS = (2, 4, 8, 16)
PG = 256
POOL_HALO = 16
CHUNK = 128
HEADS = 8
DFF = 2816
CONV_HALO = 8
CONV_TC = 1408
N_CHIPS = 4
LANES = 128
VMEM_LIMIT = 56 * 1024 * 1024
MESH = pl.DeviceIdType.MESH
ANY = pl.BlockSpec(memory_space=pl.ANY)

ADAM_LR = 0.001
ADAM_B1 = 0.9
ADAM_B2 = 0.999
ADAM_EPS = 1e-08
ADAM_WD = 0.01
ADAM_STEP = 10

BIG = (
    ("w_in", (D, 5 * D), 1, 5 * D // N_CHIPS),
    ("w_pool", (4, PG, PG), 1, PG // N_CHIPS),
    ("w_branch_a", (D, D), 0, D // N_CHIPS),
    ("w_branch_b", (D, D), 0, D // N_CHIPS),
    ("w_out", (D, D), 0, D // N_CHIPS),
    ("w_up", (D, 2 * DFF), 1, 2 * DFF // N_CHIPS),
    ("w_down", (DFF, D), 0, DFF // N_CHIPS),
    ("w_ple_gate", (D, D), 0, D // N_CHIPS),
    ("w_ple", (PG, D), 1, D // N_CHIPS),
)
CONV_ROWS = 8


def _params(n_axes):
    return pltpu.CompilerParams(dimension_semantics=("arbitrary",) * n_axes, vmem_limit_bytes=VMEM_LIMIT)


def _gelu(x):
    return 0.5 * x * (1.0 + lax.erf(x * 0.7071067811865476))


def _gelu_grad(x):
    return 0.5 * (1.0 + lax.erf(x * 0.7071067811865476)) + x * jnp.exp(-0.5 * x * x) * 0.3989422804014327


def _shard_shape(shape, axis, size):
    return tuple(size if a == axis else s for a, s in enumerate(shape))


def _block(ref, axis, j, size):
    idx = tuple(pl.ds(j * size, size) if a == axis else slice(None) for a in range(len(ref.shape)))
    return ref.at[idx]


def mm_nn(a, w, l, *, name, rows, out_dtype=F32, resid=None, a_row_blk_off=0, out=None, out_cols=None,
          out_col_off=0, tm=1024, tn=None, tk=None):
    K, N = w.shape[1], w.shape[2]
    tn = tn or N
    tk = tk or K
    nk = K // tk
    out_cols = out_cols or N
    assert rows % tm == 0 and N % tn == 0 and K % tk == 0 and out_col_off % tn == 0
    has_resid, has_out = resid is not None, out is not None

    def body(*refs):
        refs = list(refs)
        a_ref, w_ref = refs[0], refs[1]
        r_ref = refs[2] if has_resid else None
        o_ref = refs[2 + has_resid + has_out]
        part = jnp.dot(a_ref[...].astype(BF16), w_ref[...], preferred_element_type=F32)
        if nk == 1:
            if has_resid:
                part = part + r_ref[...]
            o_ref[...] = part.astype(o_ref.dtype)
        else:
            acc = refs[-1]
            k = pl.program_id(2)

            @pl.when(k == 0)
            def _():
                acc[...] = part

            @pl.when(k > 0)
            def _():
                acc[...] += part

            @pl.when(k == nk - 1)
            def _():
                r = acc[...]
                if has_resid:
                    r = r + r_ref[...]
                o_ref[...] = r.astype(o_ref.dtype)

    in_specs = [pl.BlockSpec((tm, tk), lambda j, i, k: (i + a_row_blk_off, k)),
                pl.BlockSpec((None, tk, tn), lambda j, i, k: (l, k, j))]
    args = [a, w]
    if has_resid:
        in_specs.append(pl.BlockSpec((tm, tn), lambda j, i, k: (i, j)))
        args.append(resid)
    aliases = {}
    if has_out:
        in_specs.append(ANY)
        aliases = {len(args): 0}
        args.append(out)
    return pl.pallas_call(
        body, name=name, grid=(N // tn, rows // tm, nk),
        in_specs=in_specs,
        out_specs=pl.BlockSpec((tm, tn), lambda j, i, k: (i, j + out_col_off // tn)),
        out_shape=jax.ShapeDtypeStruct((rows, out_cols), out_dtype),
        scratch_shapes=[pltpu.VMEM((tm, tn), F32)] if nk > 1 else [],
        input_output_aliases=aliases, compiler_params=_params(3))(*args)


def mm_nt(a, w, l, *, name, rows, kdim=None, a_col_off=0, out_dtype=F32, tm=1024, tn=None, tk=None):
    R = w.shape[1]
    kdim = kdim or w.shape[2]
    tn = tn or R
    tk = tk or kdim
    nk = kdim // tk
    assert rows % tm == 0 and R % tn == 0 and kdim % tk == 0 and a_col_off % tk == 0

    def body(a_ref, w_ref, o_ref, *scr):
        part = lax.dot_general(a_ref[...].astype(BF16), w_ref[...], (((1,), (1,)), ((), ())),
                               preferred_element_type=F32)
        if nk == 1:
            o_ref[...] = part.astype(o_ref.dtype)
        else:
            acc = scr[0]
            k = pl.program_id(2)

            @pl.when(k == 0)
            def _():
                acc[...] = part

            @pl.when(k > 0)
            def _():
                acc[...] += part

            @pl.when(k == nk - 1)
            def _():
                o_ref[...] = acc[...].astype(o_ref.dtype)

    if a.ndim == 3:
        per = a.shape[2] // tk
        a_spec = pl.BlockSpec((None, tm, tk), lambda j, i, k: (k // per, i, k % per))
    else:
        a_spec = pl.BlockSpec((tm, tk), lambda j, i, k: (i, k + a_col_off // tk))
    return pl.pallas_call(
        body, name=name, grid=(R // tn, rows // tm, nk),
        in_specs=[a_spec, pl.BlockSpec((None, tn, tk), lambda j, i, k: (l, j, k))],
        out_specs=pl.BlockSpec((tm, tn), lambda j, i, k: (i, j)),
        out_shape=jax.ShapeDtypeStruct((rows, R), out_dtype),
        scratch_shapes=[pltpu.VMEM((tm, tn), F32)] if nk > 1 else [],
        compiler_params=_params(3))(a, w)


def mm_tn(a, b, l, *, name, rows, ka, nb, out=None, a_row_blk_off=0, b_col_off=0, tm=None, tn=None, tk=1024):
    tm = tm or ka
    tn = tn or nb
    nk = rows // tk
    assert ka % tm == 0 and nb % tn == 0 and rows % tk == 0 and b_col_off % tn == 0
    has_out = out is not None

    def body(*refs):
        a_ref, b_ref = refs[0], refs[1]
        o_ref, acc = refs[2 + has_out], refs[-1]
        part = lax.dot_general(a_ref[...].astype(BF16), b_ref[...].astype(BF16), (((0,), (0,)), ((), ())),
                               preferred_element_type=F32)
        k = pl.program_id(2)

        @pl.when(k == 0)
        def _():
            acc[...] = part

        @pl.when(k > 0)
        def _():
            acc[...] += part

        @pl.when(k == nk - 1)
        def _():
            o_ref[...] = acc[...].astype(o_ref.dtype)

    if b.ndim == 3:
        per = b.shape[2] // tn
        b_spec = pl.BlockSpec((None, tk, tn), lambda j, i, k: (j // per, k, j % per))
    else:
        b_spec = pl.BlockSpec((tk, tn), lambda j, i, k: (k, j + b_col_off // tn))
    in_specs = [pl.BlockSpec((tk, tm), lambda j, i, k: (k + a_row_blk_off, i)), b_spec]
    args = [a, b]
    aliases = {}
    if has_out:
        in_specs.append(ANY)
        aliases = {2: 0}
        args.append(out)
    return pl.pallas_call(
        body, name=name, grid=(nb // tn, ka // tm, nk),
        in_specs=in_specs,
        out_specs=pl.BlockSpec((None, tm, tn), lambda j, i, k: (l, i, j)),
        out_shape=jax.ShapeDtypeStruct((2, ka, nb), BF16),
        scratch_shapes=[pltpu.VMEM((tm, tn), F32)],
        input_output_aliases=aliases, compiler_params=_params(3))(*args)


def _row_spec(tm, width, col=0):
    return pl.BlockSpec((tm, width), lambda i: (i, col))


def _gain_spec(l, width=D):
    return pl.BlockSpec((None, 1, width), lambda i: (l, 0, 0))


def norm_fwd(x, g3, l, *, name, tm=512):
    T = x.shape[0]

    def body(x_ref, g_ref, o_ref):
        xv = x_ref[...]
        r = lax.rsqrt(jnp.mean(xv * xv, axis=-1, keepdims=True) + EPS)
        o_ref[...] = (xv * r * g_ref[...]).astype(BF16)

    return pl.pallas_call(
        body, name=name, grid=(T // tm,),
        in_specs=[_row_spec(tm, D), _gain_spec(l)], out_specs=_row_spec(tm, D),
        out_shape=jax.ShapeDtypeStruct((T, D), BF16), compiler_params=_params(1))(x, g3)


def _winsum_back(ext, w):
    s, span = ext, 1
    while span < w:
        s = s + pltpu.roll(s, span, 0)
        span *= 2
    return s


def _winsum_fwd(ext, w):
    rows = ext.shape[0]
    s, span = ext, 1
    while span < w:
        s = s + pltpu.roll(s, rows - span, 0)
        span *= 2
    return s


def _pooled(ext, z, t, g, w):
    sl = slice(g * PG, (g + 1) * PG)
    s = _winsum_back(ext[:, sl], w)[POOL_HALO:, :]
    return s / jnp.minimum(t + 1, w).astype(F32) - z[:, sl]


def pool_fwd(z, wpool, scale3, l, *, name, tm=256):
    T = z.shape[0]
    hb = tm // POOL_HALO

    def body(z_ref, zp_ref, wp_ref, sc_ref, o_ref):
        i = pl.program_id(0)
        zv = z_ref[...]
        prev = jnp.where(i == 0, 0.0, zp_ref[...])
        ext = jnp.concatenate([prev, zv], axis=0)
        t = i * tm + lax.broadcasted_iota(jnp.int32, (tm, 1), 0)
        for g, w in enumerate(POOL_WINDOWS):
            sl = slice(g * PG, (g + 1) * PG)
            pooled = _pooled(ext, zv, t, g, w)
            q = jnp.dot(pooled.astype(BF16), wp_ref[g], preferred_element_type=F32)
            o_ref[:, sl] = (q * sc_ref[:, sl]).astype(BF16)

    return pl.pallas_call(
        body, name=name, grid=(T // tm,),
        in_specs=[_row_spec(tm, D),
                  pl.BlockSpec((POOL_HALO, D), lambda i: (jnp.maximum(i * hb - 1, 0), 0)),
                  pl.BlockSpec((None, 4, PG, PG), lambda i: (l, 0, 0, 0)),
                  _gain_spec(l)],
        out_specs=_row_spec(tm, D),
        out_shape=jax.ShapeDtypeStruct((T, D), BF16), compiler_params=_params(1))(z, z, wpool, scale3)


def sgu_fwd(z, g3, wsm, bT, l, *, name, tm=256):
    T = z.shape[0]

    def body(zu_ref, zv_ref, g_ref, ws_ref, b_ref, o_ref):
        gu = _gelu(zu_ref[...])
        gv = _gelu(zv_ref[...])
        rv = lax.rsqrt(jnp.mean(gv * gv, axis=-1, keepdims=True) + EPS)
        vn = (gv * rv * g_ref[...]).astype(BF16)
        for n in range(tm // CHUNK):
            r = slice(n * CHUNK, (n + 1) * CHUNK)
            for h in range(HEADS):
                cs = slice(h * CHUNK, (h + 1) * CHUNK)
                mixed = jnp.dot(ws_ref[h], vn[r, cs], preferred_element_type=F32) + b_ref[:, h:h + 1]
                o_ref[r, cs] = (gu[r, cs] * mixed).astype(BF16)

    return pl.pallas_call(
        body, name=name, grid=(T // tm,),
        in_specs=[_row_spec(tm, D, 1), _row_spec(tm, D, 2), _gain_spec(l),
                  pl.BlockSpec((None, HEADS, CHUNK, CHUNK), lambda i: (l, 0, 0, 0)),
                  pl.BlockSpec((None, CHUNK, HEADS), lambda i: (l, 0, 0))],
        out_specs=_row_spec(tm, D),
        out_shape=jax.ShapeDtypeStruct((T, D), BF16), compiler_params=_params(1))(z, z, g3, wsm, bT)


def gate_fwd(z, yab, *, name, tm=512):
    T = z.shape[0]

    def body(za_ref, zb_ref, y_ref, o_ref):
        ga = jax.nn.sigmoid(za_ref[...])
        gb = jax.nn.sigmoid(zb_ref[...])
        o_ref[...] = (ga * y_ref[:, :D] + gb * y_ref[:, D:]).astype(BF16)

    return pl.pallas_call(
        body, name=name, grid=(T // tm,),
        in_specs=[_row_spec(tm, D, 3), _row_spec(tm, D, 4), _row_spec(tm, 2 * D)],
        out_specs=_row_spec(tm, D),
        out_shape=jax.ShapeDtypeStruct((T, D), BF16), compiler_params=_params(1))(z, z, yab)


def _conv(ext, w_ref, b_ref):
    c = b_ref[...] + w_ref[0:1, :] * pltpu.roll(ext, 2, 0)
    c = c + w_ref[1:2, :] * pltpu.roll(ext, 1, 0)
    return c + w_ref[2:3, :] * ext


def conv_fwd(up, convw, convb3, l, *, name, tm=256):
    T = up.shape[0]
    tc = CONV_TC
    nc = DFF // tc
    hb = tm // CONV_HALO

    def body(ua_ref, uap_ref, ub_ref, ubp_ref, wa_ref, wb_ref, ba_ref, bb_ref, o_ref):
        i = pl.program_id(1)

        def conv_of(u_ref, p_ref, w_ref, b_ref):
            ext = jnp.concatenate([jnp.where(i == 0, 0.0, p_ref[...]), u_ref[...]], axis=0)
            return _conv(ext, w_ref, b_ref)[CONV_HALO:, :]

        ca = conv_of(ua_ref, uap_ref, wa_ref, ba_ref)
        cb = conv_of(ub_ref, ubp_ref, wb_ref, bb_ref)
        o_ref[...] = (_gelu(ca) * cb).astype(BF16)

    def cur(off):
        return pl.BlockSpec((tm, tc), lambda j, i: (i, j + off))

    def prev(off):
        return pl.BlockSpec((CONV_HALO, tc), lambda j, i: (jnp.maximum(i * hb - 1, 0), j + off))

    def wspec(off):
        return pl.BlockSpec((None, CONV_ROWS, tc), lambda j, i: (l, 0, j + off))

    def bspec(off):
        return pl.BlockSpec((None, 1, tc), lambda j, i: (l, 0, j + off))

    return pl.pallas_call(
        body, name=name, grid=(nc, T // tm),
        in_specs=[cur(0), prev(0), cur(nc), prev(nc), wspec(0), wspec(nc), bspec(0), bspec(nc)],
        out_specs=pl.BlockSpec((tm, tc), lambda j, i: (i, j)),
        out_shape=jax.ShapeDtypeStruct((T, DFF), BF16),
        compiler_params=_params(2))(up, up, up, up, convw, convw, convb3, convb3)


def ple_fwd(x2, pg, e, *, name, tm=512):
    T = x2.shape[0]

    def body(x_ref, pg_ref, e_ref, o_ref):
        o_ref[...] = x_ref[...] + jax.nn.sigmoid(pg_ref[...]) * e_ref[...]

    return pl.pallas_call(
        body, name=name, grid=(T // tm,),
        in_specs=[_row_spec(tm, D)] * 3, out_specs=_row_spec(tm, D),
        out_shape=jax.ShapeDtypeStruct((T, D), F32), compiler_params=_params(1))(x2, pg, e)


def loss_head(x, g3, tgt, *, name, tm=512):
    T = x.shape[0]

    def body(x_ref, g_ref, t_ref, loss_ref, dx_ref, dg_ref):
        @pl.when(pl.program_id(0) == 0)
        def _():
            loss_ref[...] = jnp.zeros_like(loss_ref)
            dg_ref[...] = jnp.zeros_like(dg_ref)

        xv, g = x_ref[...], g_ref[...]
        r = lax.rsqrt(jnp.mean(xv * xv, axis=-1, keepdims=True) + EPS)
        xh = xv * r
        err = xh * g - t_ref[...]
        loss_ref[...] += 0.5 * jnp.sum(jnp.mean(err * err, axis=-1, keepdims=True))
        dy = err * (1.0 / D)
        dyg = dy * g
        dx_ref[...] = r * (dyg - xh * jnp.mean(dyg * xh, axis=-1, keepdims=True))
        dg_ref[0:1, :] += jnp.sum(dy * xh, axis=0, keepdims=True)

    return pl.pallas_call(
        body, name=name, grid=(T // tm,),
        in_specs=[_row_spec(tm, D), pl.BlockSpec((1, D), lambda i: (0, 0)), _row_spec(tm, D)],
        out_specs=[pl.BlockSpec((8, LANES), lambda i: (0, 0)), _row_spec(tm, D),
                   pl.BlockSpec((8, D), lambda i: (0, 0))],
        out_shape=[jax.ShapeDtypeStruct((8, LANES), F32), jax.ShapeDtypeStruct((T, D), F32),
                   jax.ShapeDtypeStruct((8, D), F32)],
        compiler_params=_params(1))(x, g3, tgt)


def norm_bwd(dh, x, g3, l, dx_in, *, name, tm=512):
    T = x.shape[0]

    def body(dh_ref, x_ref, g_ref, dxi_ref, dx_ref, dg_ref):
        @pl.when(pl.program_id(0) == 0)
        def _():
            dg_ref[...] = jnp.zeros_like(dg_ref)

        xv, dh_v = x_ref[...], dh_ref[...]
        r = lax.rsqrt(jnp.mean(xv * xv, axis=-1, keepdims=True) + EPS)
        xh = xv * r
        dhg = dh_v * g_ref[...]
        dx_ref[...] = dxi_ref[...] + r * (dhg - xh * jnp.mean(dhg * xh, axis=-1, keepdims=True))
        dg_ref[0:1, :] += jnp.sum(dh_v * xh, axis=0, keepdims=True)

    return pl.pallas_call(
        body, name=name, grid=(T // tm,),
        in_specs=[_row_spec(tm, D), _row_spec(tm, D), _gain_spec(l), _row_spec(tm, D)],
        out_specs=[_row_spec(tm, D), pl.BlockSpec((8, D), lambda i: (0, 0))],
        out_shape=[jax.ShapeDtypeStruct((T, D), F32), jax.ShapeDtypeStruct((8, D), F32)],
        compiler_params=_params(1))(dh, x, g3, dx_in)


def ple_bwd(dx, pg, e, *, name, tm=512):
    T = dx.shape[0]

    def body(dx_ref, pg_ref, e_ref, de_ref, dpg_ref):
        gate = jax.nn.sigmoid(pg_ref[...])
        dxv = dx_ref[...]
        de_ref[...] = (dxv * gate).astype(BF16)
        dpg_ref[...] = (dxv * e_ref[...] * gate * (1.0 - gate)).astype(BF16)

    return pl.pallas_call(
        body, name=name, grid=(T // tm,),
        in_specs=[_row_spec(tm, D)] * 3, out_specs=[_row_spec(tm, D)] * 2,
        out_shape=[jax.ShapeDtypeStruct((T, D), BF16)] * 2, compiler_params=_params(1))(dx, pg, e)


def conv_bwd(df, up, convw, convb3, l, *, name, tm=256):
    T = up.shape[0]
    tc = CONV_TC
    nc = DFF // tc
    hb = tm // CONV_HALO
    nt = T // tm
    rows = tm + 2 * CONV_HALO
    own = slice(CONV_HALO, CONV_HALO + tm)

    def body(df_ref, dfn_ref, ua_ref, uap_ref, uan_ref, ub_ref, ubp_ref, ubn_ref, wa_ref, wb_ref, ba_ref, bb_ref,
             dup_ref, dcw_ref, dcb_ref):
        i = pl.program_id(1)

        @pl.when(i == 0)
        def _():
            dcw_ref[...] = jnp.zeros_like(dcw_ref)
            dcb_ref[...] = jnp.zeros_like(dcb_ref)

        def ext_of(c_ref, p_ref, n_ref):
            return jnp.concatenate([jnp.where(i == 0, 0.0, p_ref[...]), c_ref[...],
                                    jnp.where(i == nt - 1, 0.0, n_ref[...])], axis=0)

        ea = ext_of(ua_ref, uap_ref, uan_ref)
        eb = ext_of(ub_ref, ubp_ref, ubn_ref)
        ca = _conv(ea, wa_ref, ba_ref)
        cb = _conv(eb, wb_ref, bb_ref)
        df_ext = jnp.concatenate([jnp.zeros((CONV_HALO, tc), F32), df_ref[...],
                                  jnp.where(i == nt - 1, 0.0, dfn_ref[...])], axis=0)
        cdf = 0.5 * (1.0 + lax.erf(ca * 0.7071067811865476))
        da = df_ext * cb * (cdf + ca * jnp.exp(-0.5 * ca * ca) * 0.3989422804014327)
        db = df_ext * (ca * cdf)

        def finish(h, dc, e, w_ref):
            dup = w_ref[2:3, :] * dc + w_ref[1:2, :] * pltpu.roll(dc, rows - 1, 0)
            dup = dup + w_ref[0:1, :] * pltpu.roll(dc, rows - 2, 0)
            dup_ref[h] = dup[own, :].astype(BF16)
            dco = dc[own, :]
            dcb_ref[h, 0:1, :] += jnp.sum(dco, axis=0, keepdims=True)
            dcw_ref[h, 0:1, :] += jnp.sum(dco * pltpu.roll(e, 2, 0)[own, :], axis=0, keepdims=True)
            dcw_ref[h, 1:2, :] += jnp.sum(dco * pltpu.roll(e, 1, 0)[own, :], axis=0, keepdims=True)
            dcw_ref[h, 2:3, :] += jnp.sum(dco * e[own, :], axis=0, keepdims=True)

        finish(0, da, ea, wa_ref)
        finish(1, db, eb, wb_ref)

    def nxt(i):
        return jnp.minimum((i + 1) * hb, T // CONV_HALO - 1)

    def prv(i):
        return jnp.maximum(i * hb - 1, 0)

    def up_specs(off):
        return [pl.BlockSpec((tm, tc), lambda j, i: (i, j + off)),
                pl.BlockSpec((CONV_HALO, tc), lambda j, i: (prv(i), j + off)),
                pl.BlockSpec((CONV_HALO, tc), lambda j, i: (nxt(i), j + off))]

    in_specs = [pl.BlockSpec((tm, tc), lambda j, i: (i, j)),
                pl.BlockSpec((CONV_HALO, tc), lambda j, i: (nxt(i), j)),
                *up_specs(0), *up_specs(nc),
                pl.BlockSpec((None, CONV_ROWS, tc), lambda j, i: (l, 0, j)),
                pl.BlockSpec((None, CONV_ROWS, tc), lambda j, i: (l, 0, j + nc)),
                pl.BlockSpec((None, 1, tc), lambda j, i: (l, 0, j)),
                pl.BlockSpec((None, 1, tc), lambda j, i: (l, 0, j + nc))]
    return pl.pallas_call(
        body, name=name, grid=(nc, nt), in_specs=in_specs,
        out_specs=[pl.BlockSpec((2, tm, tc), lambda j, i: (0, i, j)),
                   pl.BlockSpec((2, 8, tc), lambda j, i: (0, 0, j)),
                   pl.BlockSpec((2, 8, tc), lambda j, i: (0, 0, j))],
        out_shape=[jax.ShapeDtypeStruct((2, T, DFF), BF16), jax.ShapeDtypeStruct((2, 8, DFF), F32),
                   jax.ShapeDtypeStruct((2, 8, DFF), F32)],
        compiler_params=_params(2))(df, df, up, up, up, up, up, up, convw, convw, convb3, convb3)


def gate_bwd(dmo, z, yab, *, name, tm=512):
    T = z.shape[0]

    def body(dmo_ref, zg_ref, y_ref, dz_ref, dy_ref):
        g = jax.nn.sigmoid(zg_ref[...])
        dmo_v = dmo_ref[...]
        dy_ref[...] = (dmo_v * g).astype(BF16)
        dz_ref[...] = (dmo_v * y_ref[...] * g * (1.0 - g)).astype(BF16)

    return pl.pallas_call(
        body, name=name, grid=(T // tm, 2),
        in_specs=[pl.BlockSpec((tm, D), lambda i, s: (i, 0)),
                  pl.BlockSpec((tm, D), lambda i, s: (i, 3 + s)),
                  pl.BlockSpec((tm, D), lambda i, s: (i, s))],
        out_specs=[pl.BlockSpec((tm, D), lambda i, s: (i, 3 + s)),
                   pl.BlockSpec((tm, D), lambda i, s: (i, s))],
        out_shape=[jax.ShapeDtypeStruct((T, 5 * D), BF16), jax.ShapeDtypeStruct((T, 2 * D), BF16)],
        compiler_params=_params(2))(dmo, z, yab)


def mixer_bwd(da, ds, z, dz, wpool, scale3, g3, wsm, wsmT, bT, l, *, name, tm=256):
    T = z.shape[0]
    hb = tm // POOL_HALO
    nt = T // tm

    def body(da_ref, dan_ref, ds_ref, zp_ref, zpp_ref, zu_ref, zv_ref, wp_ref, sc_ref, g_ref, ws_ref, wst_ref,
             b_ref, dzin_ref, dz_ref, dwp_ref, dsc_ref, dws_ref, dbt_ref, dgs_ref, mixed_scr, dvn_scr, db_scr):
        del dzin_ref
        i = pl.program_id(0)

        @pl.when(i == 0)
        def _():
            dwp_ref[...] = jnp.zeros_like(dwp_ref)
            dsc_ref[...] = jnp.zeros_like(dsc_ref)
            dws_ref[...] = jnp.zeros_like(dws_ref)
            dgs_ref[...] = jnp.zeros_like(dgs_ref)
            db_scr[...] = jnp.zeros_like(db_scr)

        zv_p = zp_ref[...]
        ext = jnp.concatenate([jnp.where(i == 0, 0.0, zpp_ref[...]), zv_p], axis=0)
        da_v = da_ref[...]
        da_ext = jnp.concatenate([da_v, jnp.where(i == nt - 1, 0.0, dan_ref[...])], axis=0)
        t = i * tm + lax.broadcasted_iota(jnp.int32, (tm, 1), 0)
        t_ext = i * tm + lax.broadcasted_iota(jnp.int32, (tm + POOL_HALO, 1), 0)
        for g, w in enumerate(POOL_WINDOWS):
            sl = slice(g * PG, (g + 1) * PG)
            pooled = _pooled(ext, zv_p, t, g, w).astype(BF16)
            q = jnp.dot(pooled, wp_ref[g], preferred_element_type=F32)
            dsc_ref[0:1, sl] += jnp.sum(da_v[:, sl] * q, axis=0, keepdims=True)
            dq_ext = (da_ext[:, sl] * sc_ref[:, sl]).astype(BF16)
            dwp_ref[g] += lax.dot_general(pooled, dq_ext[:tm, :], (((0,), (0,)), ((), ())),
                                          preferred_element_type=F32)
            dpool = lax.dot_general(dq_ext, wp_ref[g], (((1,), (1,)), ((), ())), preferred_element_type=F32)
            spread = _winsum_fwd(dpool / jnp.minimum(t_ext + 1, w).astype(F32), w)
            dz_ref[:, sl] = (spread[:tm, :] - dpool[:tm, :]).astype(BF16)

        zu, zv, ds_v, gain = zu_ref[...], zv_ref[...], ds_ref[...], g_ref[...]
        gu, gv = _gelu(zu), _gelu(zv)
        rv = lax.rsqrt(jnp.mean(gv * gv, axis=-1, keepdims=True) + EPS)
        vh = gv * rv
        vn = (vh * gain).astype(BF16)
        dmix = ds_v * gu
        dmix_b = dmix.astype(BF16)
        for n in range(tm // CHUNK):
            r = slice(n * CHUNK, (n + 1) * CHUNK)
            db_scr[...] += dmix[r, :]
            for h in range(HEADS):
                cs = slice(h * CHUNK, (h + 1) * CHUNK)
                mixed_scr[r, cs] = jnp.dot(ws_ref[h], vn[r, cs], preferred_element_type=F32) + b_ref[:, h:h + 1]
                dws_ref[h] += lax.dot_general(dmix_b[r, cs], vn[r, cs], (((1,), (1,)), ((), ())),
                                              preferred_element_type=F32)
                dvn_scr[r, cs] = jnp.dot(wst_ref[h], dmix_b[r, cs], preferred_element_type=F32)
        dz_ref[:, D:2 * D] = (ds_v * mixed_scr[...] * _gelu_grad(zu)).astype(BF16)
        dvn = dvn_scr[...]
        dgs_ref[0:1, :] += jnp.sum(dvn * vh, axis=0, keepdims=True)
        dvg = dvn * gain
        dgv = rv * (dvg - vh * jnp.mean(dvg * vh, axis=-1, keepdims=True))
        dz_ref[:, 2 * D:3 * D] = (dgv * _gelu_grad(zv)).astype(BF16)

        @pl.when(i == nt - 1)
        def _():
            tril = (lax.broadcasted_iota(jnp.int32, (CHUNK, CHUNK), 0)
                    >= lax.broadcasted_iota(jnp.int32, (CHUNK, CHUNK), 1)).astype(F32)
            for h in range(HEADS):
                dws_ref[h] = dws_ref[h] * tril
                dbt_ref[:, h:h + 1] = jnp.sum(db_scr[:, h * CHUNK:(h + 1) * CHUNK], axis=1, keepdims=True)

    const4 = lambda i: (l, 0, 0, 0)
    in_specs = [
        _row_spec(tm, D),
        pl.BlockSpec((POOL_HALO, D), lambda i: (jnp.minimum((i + 1) * hb, T // POOL_HALO - 1), 0)),
        _row_spec(tm, D),
        _row_spec(tm, D, 0),
        pl.BlockSpec((POOL_HALO, D), lambda i: (jnp.maximum(i * hb - 1, 0), 0)),
        _row_spec(tm, D, 1), _row_spec(tm, D, 2),
        pl.BlockSpec((None, 4, PG, PG), const4),
        _gain_spec(l), _gain_spec(l),
        pl.BlockSpec((None, HEADS, CHUNK, CHUNK), const4),
        pl.BlockSpec((None, HEADS, CHUNK, CHUNK), const4),
        pl.BlockSpec((None, CHUNK, HEADS), lambda i: (l, 0, 0)),
        ANY,
    ]
    out_specs = [
        pl.BlockSpec((tm, 3 * D), lambda i: (i, 0)),
        pl.BlockSpec((4, PG, PG), lambda i: (0, 0, 0)),
        pl.BlockSpec((8, D), lambda i: (0, 0)),
        pl.BlockSpec((HEADS, CHUNK, CHUNK), lambda i: (0, 0, 0)),
        pl.BlockSpec((CHUNK, HEADS), lambda i: (0, 0)),
        pl.BlockSpec((8, D), lambda i: (0, 0)),
    ]
    out_shape = [
        jax.ShapeDtypeStruct((T, 5 * D), BF16), jax.ShapeDtypeStruct((4, PG, PG), F32),
        jax.ShapeDtypeStruct((8, D), F32), jax.ShapeDtypeStruct((HEADS, CHUNK, CHUNK), F32),
        jax.ShapeDtypeStruct((CHUNK, HEADS), F32), jax.ShapeDtypeStruct((8, D), F32),
    ]
    return pl.pallas_call(
        body, name=name, grid=(nt,), in_specs=in_specs, out_specs=out_specs, out_shape=out_shape,
        scratch_shapes=[pltpu.VMEM((tm, D), F32), pltpu.VMEM((tm, D), F32), pltpu.VMEM((CHUNK, D), F32)],
        input_output_aliases={13: 0}, compiler_params=_params(1))(
            da, da, ds, z, z, z, z, wpool, scale3, g3, wsm, wsmT, bT, dz)


def _row_tile(rows, cols, sub):
    cap = max(sub, (2 * 1024 * 1024) // (4 * cols))
    best = None
    for tr in range(sub, min(rows, cap) + 1, sub):
        if rows % tr == 0:
            best = tr
    return best or rows


def elementwise(fn, ins, out_dtypes, *, name, row_blk_offs=None, rows=None, uses_mesh=False):
    cols = ins[0].shape[1]
    rows = rows or ins[0].shape[0]
    tr = _row_tile(rows, cols, 16)
    offs = row_blk_offs or [0] * len(ins)
    n_in = len(ins)

    def body(*refs):
        outs = fn(*[r[...] for r in refs[:n_in]])
        for o_ref, o in zip(refs[n_in:], outs):
            o_ref[...] = o.astype(o_ref.dtype)

    del uses_mesh
    return pl.pallas_call(
        body, name=name, grid=(rows // tr,),
        in_specs=[pl.BlockSpec((tr, cols), functools.partial(lambda i, o: (i + o * (rows // tr), 0), o=o))
                  for o in offs],
        out_specs=[pl.BlockSpec((tr, cols), lambda i: (i, 0)) for _ in out_dtypes],
        out_shape=[jax.ShapeDtypeStruct((rows, cols), dt) for dt in out_dtypes],
        compiler_params=_params(1))(*ins)


def _adamw(w, g, m, v):
    m = ADAM_B1 * m + (1.0 - ADAM_B1) * g
    v = ADAM_B2 * v + (1.0 - ADAM_B2) * jnp.square(g)
    m_hat = m / (1.0 - ADAM_B1 ** ADAM_STEP)
    v_hat = v / (1.0 - ADAM_B2 ** ADAM_STEP)
    delta = -ADAM_LR * (m_hat / (jnp.sqrt(v_hat) + ADAM_EPS) + ADAM_WD * w)
    return delta, m, v


def _view2d(a):
    return a.reshape(-1, a.shape[-1])


def _place():
    x, y, c = lax.axis_index("x"), lax.axis_index("y"), lax.axis_index("c")
    others = [(1 - x, y), (x, 1 - y), (1 - x, 1 - y)]
    return x, y, c, 2 * x + y, others


def _remote(src, dst, send_sems, recv_sems, k, to):
    return pltpu.make_async_remote_copy(src_ref=src, dst_ref=dst, send_sem=send_sems.at[k], recv_sem=recv_sems.at[k],
                                        device_id=to, device_id_type=MESH)


def place_shard(src, axis, size, out_dtype, place, *, name):
    shard = src.shape[1:]
    natural = tuple(size * N_CHIPS if a == axis else s for a, s in enumerate(shard))
    if len(shard) == 3:
        blk = (None,) + shard
        grid = (2, 1)
        in_map = lambda lyr, i, pr: (lyr, 0, 0, 0)
        out_map = lambda lyr, i, pr: (lyr, 0, pr[0], 0)
    else:
        tr = _row_tile(shard[0], shard[1], 16)
        steps = shard[0] // tr
        blk = (None, tr, shard[1])
        grid = (2, steps)
        in_map = lambda lyr, i, pr: (lyr, i, 0)
        if axis == 0:
            out_map = lambda lyr, i, pr: (lyr, pr[0] * steps + i, 0)
        else:
            out_map = lambda lyr, i, pr: (lyr, i, pr[0])

    def body(pr_ref, s_ref, o_ref):
        del pr_ref
        o_ref[...] = s_ref[...].astype(o_ref.dtype)

    return pl.pallas_call(
        body, name=name,
        grid_spec=pltpu.PrefetchScalarGridSpec(
            num_scalar_prefetch=1, grid=grid, in_specs=[pl.BlockSpec(blk, in_map)],
            out_specs=pl.BlockSpec(blk, out_map)),
        out_shape=jax.ShapeDtypeStruct((2,) + natural, out_dtype), compiler_params=_params(2))(place, src)


def all_gather_weights(arrays, geom):
    n = len(arrays)

    def body(*refs):
        outs = refs[n:2 * n]
        send_sems, recv_sems = refs[2 * n:]
        x, y, c, j, others = _place()
        sib = (x, y, 1 - c)
        pending = []
        for t, (axis, size) in enumerate(geom):
            mine = _block(outs[t].at[c], axis, j, size)
            for k, (ox, oy) in enumerate(others):
                cp = _remote(mine, mine, send_sems, recv_sems, 6 * t + k, (ox, oy, c))
                cp.start()
                pending.append(cp.wait_send)
        for t, (axis, size) in enumerate(geom):
            for k, (ox, oy) in enumerate(others):
                landed = _block(outs[t].at[c], axis, 2 * ox + oy, size)
                _remote(landed, landed, send_sems, recv_sems, 6 * t + k, (ox, oy, c)).wait_recv()
                fwd = _remote(landed, landed, send_sems, recv_sems, 6 * t + 3 + k, sib)
                fwd.start()
                pending.append(fwd.wait_send)
        for t, (axis, size) in enumerate(geom):
            for k, (ox, oy) in enumerate(others):
                got = _block(outs[t].at[1 - c], axis, 2 * ox + oy, size)
                _remote(got, got, send_sems, recv_sems, 6 * t + 3 + k, sib).wait_recv()
        for wait in pending:
            wait()

    return pl.pallas_call(
        body, name="all_gather_weights", in_specs=[ANY] * n, out_specs=[ANY] * n,
        out_shape=[jax.ShapeDtypeStruct(a.shape, a.dtype) for a in arrays],
        scratch_shapes=[pltpu.SemaphoreType.DMA((6 * n,)), pltpu.SemaphoreType.DMA((6 * n,))],
        input_output_aliases={t: t for t in range(n)},
        compiler_params=pltpu.CompilerParams(has_side_effects=True))(*arrays)


def swap_layers_with_sibling(grads):
    n = len(grads)

    def body(*refs):
        srcs, outs = refs[:n], refs[n:2 * n]
        send_sems, recv_sems = refs[2 * n:]
        x, y, c, _, _ = _place()
        cps = [_remote(srcs[t].at[1 - c], outs[t], send_sems, recv_sems, t, (x, y, 1 - c)) for t in range(n)]
        for cp in cps:
            cp.start()
        for cp in cps:
            cp.wait()

    return pl.pallas_call(
        body, name="swap_layers_with_sibling", in_specs=[ANY] * n, out_specs=[ANY] * n,
        out_shape=[jax.ShapeDtypeStruct(g.shape[1:], g.dtype) for g in grads],
        scratch_shapes=[pltpu.SemaphoreType.DMA((n,)), pltpu.SemaphoreType.DMA((n,))],
        compiler_params=pltpu.CompilerParams(has_side_effects=True))(*grads)


def scatter_to_chips(parts, geom):
    n = len(parts)

    def body(*refs):
        srcs, outs = refs[:n], refs[n:2 * n]
        send_sems, recv_sems = refs[2 * n:]
        x, y, c, j, others = _place()
        cps = []
        for t, (axis, size) in enumerate(geom):
            for k, (ox, oy) in enumerate(others):
                cps.append(_remote(_block(srcs[t], axis, 2 * ox + oy, size), outs[t].at[k], send_sems, recv_sems,
                                   3 * t + k, (ox, oy, c)))
        for cp in cps:
            cp.start()
        for cp in cps:
            cp.wait_recv()
        for cp in cps:
            cp.wait_send()

    out_shape = [jax.ShapeDtypeStruct((3,) + _shard_shape(p.shape, axis, size), p.dtype)
                 for p, (axis, size) in zip(parts, geom)]
    return pl.pallas_call(
        body, name="scatter_to_chips", in_specs=[ANY] * n, out_specs=[ANY] * n, out_shape=out_shape,
        scratch_shapes=[pltpu.SemaphoreType.DMA((3 * n,)), pltpu.SemaphoreType.DMA((3 * n,))],
        compiler_params=pltpu.CompilerParams(has_side_effects=True))(*parts)


def share_with_sibling(grads):
    n = len(grads)

    def body(*refs):
        outs = refs[n:2 * n]
        send_sems, recv_sems = refs[2 * n:]
        x, y, c, _, _ = _place()
        cps = [_remote(outs[t].at[c], outs[t].at[c], send_sems, recv_sems, t, (x, y, 1 - c)) for t in range(n)]
        for cp in cps:
            cp.start()
        for t in range(n):
            got = outs[t].at[1 - c]
            _remote(got, got, send_sems, recv_sems, t, (x, y, 1 - c)).wait_recv()
        for cp in cps:
            cp.wait_send()

    return pl.pallas_call(
        body, name="share_with_sibling", in_specs=[ANY] * n, out_specs=[ANY] * n,
        out_shape=[jax.ShapeDtypeStruct(g.shape, g.dtype) for g in grads],
        scratch_shapes=[pltpu.SemaphoreType.DMA((n,)), pltpu.SemaphoreType.DMA((n,))],
        input_output_aliases={t: t for t in range(n)},
        compiler_params=pltpu.CompilerParams(has_side_effects=True))(*grads)


def all_reduce_small(s):
    rows = s.shape[0]
    half = rows // 2
    assert half % 8 == 0

    def body(s_ref, o_ref, a_ref, b_ref, p_ref, send_sems, recv_sems):
        x, y, c, j, others = _place()
        sib = (x, y, 1 - c)
        swap = _remote(s_ref, a_ref, send_sems, recv_sems, 0, sib)
        swap.start()
        swap.wait()
        p_ref[...] = s_ref[...] + a_ref[...]
        mine = pl.ds(pl.multiple_of(c * half, 8), half)
        b_ref[j] = p_ref[mine, :]
        cps = [_remote(p_ref.at[mine, :], b_ref.at[j], send_sems, recv_sems, 1 + k, (ox, oy, c))
               for k, (ox, oy) in enumerate(others)]
        for cp in cps:
            cp.start()
        for k, (ox, oy) in enumerate(others):
            slot = b_ref.at[2 * ox + oy]
            _remote(slot, slot, send_sems, recv_sems, 1 + k, (ox, oy, c)).wait_recv()
        for cp in cps:
            cp.wait_send()
        o_ref[mine, :] = ((b_ref[0] + b_ref[1]) + b_ref[2]) + b_ref[3]
        back = _remote(o_ref.at[mine, :], o_ref.at[mine, :], send_sems, recv_sems, 4, sib)
        back.start()
        back.wait_send()
        theirs = pl.ds(pl.multiple_of((1 - c) * half, 8), half)
        _remote(o_ref.at[theirs, :], o_ref.at[theirs, :], send_sems, recv_sems, 4, sib).wait_recv()

    vmem = pl.BlockSpec(memory_space=pltpu.VMEM)
    return pl.pallas_call(
        body, name="all_reduce_small", in_specs=[vmem], out_specs=vmem,
        out_shape=jax.ShapeDtypeStruct((rows, LANES), F32),
        scratch_shapes=[pltpu.VMEM((rows, LANES), F32), pltpu.VMEM((N_CHIPS, half, LANES), F32),
                        pltpu.VMEM((rows, LANES), F32), pltpu.SemaphoreType.DMA((5,)),
                        pltpu.SemaphoreType.DMA((5,))],
        compiler_params=pltpu.CompilerParams(vmem_limit_bytes=VMEM_LIMIT, has_side_effects=True))(s)


def pair_sum(g, got, *, name):
    shape = g.shape[1:]
    cols = shape[-1]
    g2 = g.reshape(2, -1, cols)
    got2 = got.reshape(-1, cols)
    rows = got2.shape[0]
    tr = _row_tile(rows, cols, 16)

    def body(g0_ref, g1_ref, got_ref, o_ref):
        own = jnp.where(lax.axis_index("c") == 0, g0_ref[...].astype(F32), g1_ref[...].astype(F32))
        o_ref[...] = (own + got_ref[...].astype(F32)).astype(BF16)

    out = pl.pallas_call(
        body, name=name, grid=(rows // tr,),
        in_specs=[pl.BlockSpec((None, tr, cols), lambda i: (0, i, 0)),
                  pl.BlockSpec((None, tr, cols), lambda i: (1, i, 0)),
                  pl.BlockSpec((tr, cols), lambda i: (i, 0))],
        out_specs=pl.BlockSpec((tr, cols), lambda i: (i, 0)),
        out_shape=jax.ShapeDtypeStruct((rows, cols), BF16), compiler_params=_params(1))(g2, g2, got2)
    return out.reshape(shape)


def chip_sum(part, slots, axis, size, place, *, name):
    shard = slots.shape[1:]
    if len(shard) == 3:
        grid = (1,)
        p_spec = pl.BlockSpec(shard, lambda i, pr: (0, pr[0], 0))
        s_specs = [pl.BlockSpec((None,) + shard, functools.partial(lambda i, pr, k: (k, 0, 0, 0), k=k))
                   for k in range(3)]
        o_spec = pl.BlockSpec((None,) + shard, lambda i, pr: (pr[1], 0, 0, 0))
    else:
        tr = _row_tile(shard[0], shard[1], 16)
        steps = shard[0] // tr
        grid = (steps,)
        if axis == 0:
            p_spec = pl.BlockSpec((tr, shard[1]), lambda i, pr: (pr[0] * steps + i, 0))
        else:
            p_spec = pl.BlockSpec((tr, shard[1]), lambda i, pr: (i, pr[0]))
        s_specs = [pl.BlockSpec((None, tr, shard[1]), functools.partial(lambda i, pr, k: (k, i, 0), k=k))
                   for k in range(3)]
        o_spec = pl.BlockSpec((None, tr, shard[1]), lambda i, pr: (pr[1], i, 0))

    def body(pr_ref, p_ref, s0_ref, s1_ref, s2_ref, o_ref):
        del pr_ref
        o_ref[...] = ((p_ref[...].astype(F32) + s0_ref[...].astype(F32)) + s1_ref[...].astype(F32)) \
            + s2_ref[...].astype(F32)

    return pl.pallas_call(
        body, name=name,
        grid_spec=pltpu.PrefetchScalarGridSpec(num_scalar_prefetch=1, grid=grid, in_specs=[p_spec] + s_specs,
                                               out_specs=o_spec),
        out_shape=jax.ShapeDtypeStruct((2,) + shard, F32), compiler_params=_params(1))(
            place, part, slots, slots, slots)


def _local_step(x, p2, tgt, W, small):
    T = x.shape[0]
    as3 = lambda a: a.reshape(2, 1, a.shape[-1])
    mix3, scale3, sgu3 = as3(small["mix_norm"]), as3(small["pool_scale"]), as3(small["sgu_norm"])
    ffn3, ple3, convb3 = as3(small["ffn_norm"]), as3(small["ple_norm"]), as3(small["conv_b"])
    tril = jnp.tril(jnp.ones((CHUNK, CHUNK), F32))
    ws_masked = small["w_spatial"] * tril
    wsm = ws_masked.astype(BF16)
    wsmT = jnp.swapaxes(ws_masked, -1, -2).astype(BF16)
    bT = jnp.swapaxes(small["b_spatial"], -1, -2)
    final3 = small["final_norm"].reshape(1, D)

    saved = []
    for l in range(2):
        n = lambda s: f"{s}_l{l}"
        hb = norm_fwd(x, mix3, l, name=n("mix_norm_fwd"))
        z = mm_nn(hb, W["w_in"], l, name=n("in_proj"), rows=T, tn=1280)
        a_in = pool_fwd(z, W["w_pool"], scale3, l, name=n("pool_fwd"))
        s_in = sgu_fwd(z, sgu3, wsm, bT, l, name=n("sgu_fwd"))
        yab = mm_nn(a_in, W["w_branch_a"], l, name=n("branch_a"), rows=T, out_cols=2 * D)
        yab = mm_nn(s_in, W["w_branch_b"], l, name=n("branch_b"), rows=T, out=yab, out_cols=2 * D, out_col_off=D)
        mo = gate_fwd(z, yab, name=n("gate_fwd"))
        x1 = mm_nn(mo, W["w_out"], l, name=n("out_proj"), rows=T, resid=x)
        h2b = norm_fwd(x1, ffn3, l, name=n("ffn_norm_fwd"))
        up = mm_nn(h2b, W["w_up"], l, name=n("up_proj"), rows=T, tn=1408)
        f = conv_fwd(up, W["conv_w"], convb3, l, name=n("conv_fwd"))
        x2 = mm_nn(f, W["w_down"], l, name=n("down_proj"), rows=T, resid=x1, tk=1408)
        h3b = norm_fwd(x2, ple3, l, name=n("ple_norm_fwd"))
        pg = mm_nn(h3b, W["w_ple_gate"], l, name=n("ple_gate_proj"), rows=T)
        e = mm_nn(p2, W["w_ple"], l, name=n("ple_proj"), rows=T, a_row_blk_off=l * (T // 1024))
        x3 = ple_fwd(x2, pg, e, name=n("ple_fwd"))
        saved.append(dict(x=x, hb=hb, z=z, a_in=a_in, s_in=s_in, yab=yab, mo=mo, x1=x1, h2b=h2b, up=up, f=f,
                          x2=x2, h3b=h3b, pg=pg, e=e))
        x = x3

    loss_acc, dx, dg_final = loss_head(x, final3, tgt, name="loss_head")

    G = {name: None for (name, _, _, _) in BIG}
    small_grads = [None, None]
    d_pool = [None, None]
    for l in (1, 0):
        n = lambda s: f"{s}_l{l}"
        a = saved[l]
        de, dpg = ple_bwd(dx, a["pg"], a["e"], name=n("ple_bwd"))
        G["w_ple"] = mm_tn(p2, de, l, name=n("d_w_ple"), rows=T, ka=PG, nb=D, out=G["w_ple"],
                           a_row_blk_off=l * (T // 1024))
        G["w_ple_gate"] = mm_tn(a["h3b"], dpg, l, name=n("d_w_ple_gate"), rows=T, ka=D, nb=D, out=G["w_ple_gate"])
        dh3 = mm_nt(dpg, W["w_ple_gate"], l, name=n("d_ple_norm_out"), rows=T)
        dx2, dg_ple = norm_bwd(dh3, a["x2"], ple3, l, dx, name=n("ple_norm_bwd"))
        df = mm_nt(dx2, W["w_down"], l, name=n("d_ffn_act"), rows=T, tn=1408)
        G["w_down"] = mm_tn(a["f"], dx2, l, name=n("d_w_down"), rows=T, ka=DFF, nb=D, out=G["w_down"], tm=1408)
        dup, dcw, dcb = conv_bwd(df, a["up"], W["conv_w"], convb3, l, name=n("conv_bwd"))
        G["w_up"] = mm_tn(a["h2b"], dup, l, name=n("d_w_up"), rows=T, ka=D, nb=2 * DFF, out=G["w_up"], tn=1408)
        dh2 = mm_nt(dup, W["w_up"], l, name=n("d_ffn_norm_out"), rows=T, tk=1408)
        dx1, dg_ffn = norm_bwd(dh2, a["x1"], ffn3, l, dx2, name=n("ffn_norm_bwd"))
        dmo = mm_nt(dx1, W["w_out"], l, name=n("d_gated"), rows=T)
        G["w_out"] = mm_tn(a["mo"], dx1, l, name=n("d_w_out"), rows=T, ka=D, nb=D, out=G["w_out"])
        dz, dyab = gate_bwd(dmo, a["z"], a["yab"], name=n("gate_bwd"))
        G["w_branch_a"] = mm_tn(a["a_in"], dyab, l, name=n("d_w_branch_a"), rows=T, ka=D, nb=D,
                                out=G["w_branch_a"])
        G["w_branch_b"] = mm_tn(a["s_in"], dyab, l, name=n("d_w_branch_b"), rows=T, ka=D, nb=D,
                                out=G["w_branch_b"], b_col_off=D)
        da = mm_nt(dyab, W["w_branch_a"], l, name=n("d_pool_out"), rows=T, kdim=D)
        ds = mm_nt(dyab, W["w_branch_b"], l, name=n("d_sgu_out"), rows=T, kdim=D, a_col_off=D)
        dz, dwp, dsc, dws, dbt, dgs = mixer_bwd(da, ds, a["z"], dz, W["w_pool"], scale3, sgu3, wsm, wsmT, bT, l,
                                                name=n("mixer_bwd"))
        G["w_in"] = mm_tn(a["hb"], dz, l, name=n("d_w_in"), rows=T, ka=D, nb=5 * D, out=G["w_in"], tn=1280)
        dh = mm_nt(dz, W["w_in"], l, name=n("d_mix_norm_out"), rows=T, tk=1280)
        dx, dg_mix = norm_bwd(dh, a["x"], mix3, l, dx1, name=n("mix_norm_bwd"))
        d_pool[l] = dwp
        small_grads[l] = dict(
            mix_norm=dg_mix[0], pool_scale=dsc[0], sgu_norm=dgs[0], w_spatial=dws, b_spatial=dbt.T,
            ffn_norm=dg_ffn[0], conv_b=jnp.concatenate([dcb[0, 0], dcb[1, 0]]), ple_norm=dg_ple[0],
            conv_w=jnp.concatenate([dcw[0, :3], dcw[1, :3]], axis=1))
    G["w_pool"] = jnp.stack(d_pool).astype(BF16)
    return loss_acc, dx, G, small_grads, dg_final[0]


SMALL_ORDER = ("mix_norm", "pool_scale", "sgu_norm", "w_spatial", "b_spatial", "ffn_norm", "conv_b", "ple_norm",
               "conv_w")


def _pack_rows(pieces, row_multiple):
    flat = jnp.concatenate([a.reshape(-1) for a in pieces])
    rows = -(-flat.shape[0] // LANES)
    rows = -(-rows // row_multiple) * row_multiple
    return jnp.pad(flat, (0, rows * LANES - flat.shape[0])).reshape(rows, LANES)


def _unpack(flat, shapes):
    out, off = [], 0
    for shp in shapes:
        size = 1
        for s in shp:
            size *= s
        out.append(flat[off:off + size].reshape(shp))
        off += size
    return out


def kernel(x, p, mix_norm, w_in, w_pool, pool_scale, sgu_norm, w_spatial, b_spatial, w_branch_a, w_branch_b, w_out, ffn_norm, w_up, conv_w, conv_b, w_down, ple_norm, w_ple_gate, w_ple, final_norm, loss_target, m_mix_norm, m_w_in, m_w_pool, m_pool_scale, m_sgu_norm, m_w_spatial, m_b_spatial, m_w_branch_a, m_w_branch_b, m_w_out, m_ffn_norm, m_w_up, m_conv_w, m_conv_b, m_w_down, m_ple_norm, m_w_ple_gate, m_w_ple, m_final_norm, v_mix_norm, v_w_in, v_w_pool, v_pool_scale, v_sgu_norm, v_w_spatial, v_b_spatial, v_w_branch_a, v_w_branch_b, v_w_out, v_ffn_norm, v_w_up, v_conv_w, v_conv_b, v_w_down, v_ple_norm, v_w_ple_gate, v_w_ple, v_final_norm):
    names = ["mix_norm", "w_in", "w_pool", "pool_scale", "sgu_norm", "w_spatial", "b_spatial", "w_branch_a",
             "w_branch_b", "w_out", "ffn_norm", "w_up", "conv_w", "conv_b", "w_down", "ple_norm", "w_ple_gate",
             "w_ple", "final_norm"]
    w = dict(zip(names, [mix_norm, w_in, w_pool, pool_scale, sgu_norm, w_spatial, b_spatial, w_branch_a, w_branch_b,
                         w_out, ffn_norm, w_up, conv_w, conv_b, w_down, ple_norm, w_ple_gate, w_ple, final_norm]))
    m = dict(zip(names, [m_mix_norm, m_w_in, m_w_pool, m_pool_scale, m_sgu_norm, m_w_spatial, m_b_spatial,
                         m_w_branch_a, m_w_branch_b, m_w_out, m_ffn_norm, m_w_up, m_conv_w, m_conv_b, m_w_down,
                         m_ple_norm, m_w_ple_gate, m_w_ple, m_final_norm]))
    v = dict(zip(names, [v_mix_norm, v_w_in, v_w_pool, v_pool_scale, v_sgu_norm, v_w_spatial, v_b_spatial,
                         v_w_branch_a, v_w_branch_b, v_w_out, v_ffn_norm, v_w_up, v_conv_w, v_conv_b, v_w_down,
                         v_ple_norm, v_w_ple_gate, v_w_ple, v_final_norm]))
    T = x.shape[1]
    chip = 2 * lax.axis_index("x") + lax.axis_index("y")
    place = jnp.stack([chip, lax.axis_index("c")]).astype(jnp.int32)

    geom = [(axis, size) for (_, _, axis, size) in BIG]
    placed = [place_shard(w[name], axis, size, BF16, place, name=f"place_{name}") for (name, _, axis, size) in BIG]
    conv_w8 = jnp.pad(conv_w, ((0, 0), (0, CONV_ROWS - conv_w.shape[1]), (0, 0)))
    placed.append(place_shard(conv_w8, 1, conv_w.shape[2], F32, place, name="place_conv_w"))
    gathered = all_gather_weights(placed, geom + [(1, conv_w.shape[2])])
    W = {name: g for name, g in zip([t[0] for t in BIG] + ["conv_w"], gathered)}

    small = {k: w[k] for k in ("mix_norm", "pool_scale", "sgu_norm", "w_spatial", "b_spatial", "ffn_norm",
                               "conv_b", "ple_norm", "final_norm")}
    loss_acc, dx, G, small_grads, dg_final = _local_step(
        x.reshape(T, D), p.reshape(2 * T, p.shape[-1]), loss_target.reshape(T, D), W, small)
    loss = lax.psum(loss_acc[0, 0], ("x", "y", "c"))

    glist = [G[name] for (name, _, _, _) in BIG]
    got = swap_layers_with_sibling(glist)
    parts = [pair_sum(g, r, name=f"pair_sum_{name}") for (name, _, _, _), g, r in zip(BIG, glist, got)]
    slots = scatter_to_chips(parts, geom)
    halves = [chip_sum(q, s, axis, size, place, name=f"chip_sum_{name}")
              for (name, _, axis, size), q, s in zip(BIG, parts, slots)]
    full = share_with_sibling(halves)
    grads = {name: g for (name, _, _, _), g in zip(BIG, full)}

    pieces = [small_grads[l][k] for l in range(2) for k in SMALL_ORDER] + [dg_final]
    shapes = [a.shape for a in pieces]
    total = all_reduce_small(_pack_rows(pieces, 16)).reshape(-1)
    summed = _unpack(total, shapes)
    per_layer = {k: jnp.stack([summed[i], summed[len(SMALL_ORDER) + i]]) for i, k in enumerate(SMALL_ORDER)}
    for k in ("mix_norm", "pool_scale", "sgu_norm", "w_spatial", "b_spatial", "ffn_norm", "conv_b", "ple_norm"):
        grads[k] = per_layer[k]
    grads["final_norm"] = summed[-1]
    cw = conv_w.shape[2]
    grads["conv_w"] = lax.dynamic_slice_in_dim(per_layer["conv_w"], chip * cw, cw, axis=2)

    delta, new_m, new_v = {}, {}, {}
    big_names = [name for (name, _, _, _) in BIG]
    for name in big_names:
        shp = w[name].shape
        d_, m_, v_ = elementwise(_adamw, [_view2d(a) for a in (w[name], grads[name], m[name], v[name])],
                                 [F32, F32, F32], name=f"adamw_{name}")
        delta[name], new_m[name], new_v[name] = d_.reshape(shp), m_.reshape(shp), v_.reshape(shp)
    small_names = [k for k in names if k not in big_names]
    small_shapes = [w[k].shape for k in small_names]
    packed = [_pack_rows([src[k] for k in small_names], 8) for src in (w, grads, m, v)]
    outs = elementwise(_adamw, packed, [F32, F32, F32], name="adamw_small")
    for dst, o in zip((delta, new_m, new_v), outs):
        for k, a in zip(small_names, _unpack(o.reshape(-1), small_shapes)):
            dst[k] = a

    return (loss, dx.reshape(1, T, D), *[grads[k] for k in names], *[delta[k] for k in names],
            *[new_m[k] for k in names], *[new_v[k] for k in names])
```

```python
import functools

import jax
import jax.numpy as jnp
from jax import lax
from jax.experimental import pallas as pl
from jax.experimental.pallas import tpu as pltpu

F32 = jnp.float32
BF16 = jnp.bfloat16
EPS = 1e-6
D = 1024
POOL_WINDOWS = (2, 4, 8, 16)
PG = 256
POOL_HALO = 16
CHUNK = 128
HEADS = 8
DFF = 2816
CONV_HALO = 8
CONV_TC = 1408
N_CHIPS = 4
LANES = 128
VMEM_LIMIT = 56 * 1024 * 1024
MESH = pl.DeviceIdType.MESH
ANY = pl.BlockSpec(memory_space=pl.ANY)

ADAM_LR = 0.001
ADAM_B1 = 0.9
ADAM_B2 = 0.999
ADAM_EPS = 1e-08
ADAM_WD = 0.01
ADAM_STEP = 10

BIG = (
    ("w_in", (D, 5 * D), 1, 5 * D // N_CHIPS),
    ("w_pool", (4, PG, PG), 1, PG // N_CHIPS),
    ("w_branch_a", (D, D), 0, D // N_CHIPS),
    ("w_branch_b", (D, D), 0, D // N_CHIPS),
    ("w_out", (D, D), 0, D // N_CHIPS),
    ("w_up", (D, 2 * DFF), 1, 2 * DFF // N_CHIPS),
    ("w_down", (DFF, D), 0, DFF // N_CHIPS),
    ("w_ple_gate", (D, D), 0, D // N_CHIPS),
    ("w_ple", (PG, D), 1, D // N_CHIPS),
)
CONV_ROWS = 8


def _params(n_axes):
    return pltpu.CompilerParams(dimension_semantics=("arbitrary",) * n_axes, vmem_limit_bytes=VMEM_LIMIT)


def _gelu(x):
    return 0.5 * x * (1.0 + lax.erf(x * 0.7071067811865476))


def _gelu_grad(x):
    return 0.5 * (1.0 + lax.erf(x * 0.7071067811865476)) + x * jnp.exp(-0.5 * x * x) * 0.3989422804014327


def _shard_shape(shape, axis, size):
    return tuple(size if a == axis else s for a, s in enumerate(shape))


def _block(ref, axis, j, size):
    idx = tuple(pl.ds(j * size, size) if a == axis else slice(None) for a in range(len(ref.shape)))
    return ref.at[idx]


def mm_nn(a, w, l, *, name, rows, out_dtype=F32, resid=None, a_row_blk_off=0, out=None, out_cols=None,
          out_col_off=0, norm_gain=None, tm=1024, tn=None, tk=None):
    K, N = w.shape[1], w.shape[2]
    tn = tn or N
    tk = tk or K
    nk = K // tk
    out_cols = out_cols or N
    assert rows % tm == 0 and N % tn == 0 and K % tk == 0 and out_col_off % tn == 0
    has_resid, has_out, has_norm = resid is not None, out is not None, norm_gain is not None
    assert not has_norm or (tn == N and not has_out)

    def body(*refs):
        refs = list(refs)
        a_ref, w_ref = refs[0], refs[1]
        r_ref = refs[2] if has_resid else None
        g_ref = refs[2 + has_resid] if has_norm else None
        o_ref = refs[2 + has_resid + has_norm + has_out]
        part = jnp.dot(a_ref[...].astype(BF16), w_ref[...], preferred_element_type=F32)

        def finish(r):
            if has_resid:
                r = r + r_ref[...]
            o_ref[...] = r.astype(o_ref.dtype)
            if has_norm:
                h_ref = refs[3 + has_resid + has_norm + has_out]
                scale = lax.rsqrt(jnp.mean(r * r, axis=-1, keepdims=True) + EPS)
                h_ref[...] = (r * scale * g_ref[...]).astype(BF16)

        if nk == 1:
            finish(part)
        else:
            acc = refs[-1]
            k = pl.program_id(2)

            @pl.when(k == 0)
            def _():
                acc[...] = part

            @pl.when(k > 0)
            def _():
                acc[...] += part

            @pl.when(k == nk - 1)
            def _():
                finish(acc[...])

    in_specs = [pl.BlockSpec((tm, tk), lambda j, i, k: (i + a_row_blk_off, k)),
                pl.BlockSpec((None, tk, tn), lambda j, i, k: (l, k, j))]
    args = [a, w]
    if has_resid:
        in_specs.append(pl.BlockSpec((tm, tn), lambda j, i, k: (i, j)))
        args.append(resid)
    if has_norm:
        in_specs.append(pl.BlockSpec((None, 1, tn), lambda j, i, k: (l, 0, 0)))
        args.append(norm_gain)
    aliases = {}
    if has_out:
        in_specs.append(ANY)
        aliases = {len(args): 0}
        args.append(out)
    out_spec = pl.BlockSpec((tm, tn), lambda j, i, k: (i, j + out_col_off // tn))
    out_shape = jax.ShapeDtypeStruct((rows, out_cols), out_dtype)
    return pl.pallas_call(
        body, name=name, grid=(N // tn, rows // tm, nk),
        in_specs=in_specs,
        out_specs=[out_spec, pl.BlockSpec((tm, tn), lambda j, i, k: (i, j))] if has_norm else out_spec,
        out_shape=[out_shape, jax.ShapeDtypeStruct((rows, N), BF16)] if has_norm else out_shape,
        scratch_shapes=[pltpu.VMEM((tm, tn), F32)] if nk > 1 else [],
        input_output_aliases=aliases, compiler_params=_params(3))(*args)


def mm_nt(a, w, l, *, name, rows, kdim=None, a_col_off=0, out_dtype=BF16, tm=1024, tn=None, tk=None):
    R = w.shape[1]
    kdim = kdim or w.shape[2]
    tn = tn or R
    tk = tk or kdim
    nk = kdim // tk
    assert rows % tm == 0 and R % tn == 0 and kdim % tk == 0 and a_col_off % tk == 0

    def body(a_ref, w_ref, o_ref, *scr):
        part = lax.dot_general(a_ref[...].astype(BF16), w_ref[...], (((1,), (1,)), ((), ())),
                               preferred_element_type=F32)
        if nk == 1:
            o_ref[...] = part.astype(o_ref.dtype)
        else:
            acc = scr[0]
            k = pl.program_id(2)

            @pl.when(k == 0)
            def _():
                acc[...] = part

            @pl.when(k > 0)
            def _():
                acc[...] += part

            @pl.when(k == nk - 1)
            def _():
                o_ref[...] = acc[...].astype(o_ref.dtype)

    if a.ndim == 3:
        per = a.shape[2] // tk
        a_spec = pl.BlockSpec((None, tm, tk), lambda j, i, k: (k // per, i, k % per))
    else:
        a_spec = pl.BlockSpec((tm, tk), lambda j, i, k: (i, k + a_col_off // tk))
    return pl.pallas_call(
        body, name=name, grid=(R // tn, rows // tm, nk),
        in_specs=[a_spec, pl.BlockSpec((None, tn, tk), lambda j, i, k: (l, j, k))],
        out_specs=pl.BlockSpec((tm, tn), lambda j, i, k: (i, j)),
        out_shape=jax.ShapeDtypeStruct((rows, R), out_dtype),
        scratch_shapes=[pltpu.VMEM((tm, tn), F32)] if nk > 1 else [],
        compiler_params=_params(3))(a, w)


def mm_tn(a, b, l, *, name, rows, ka, nb, out=None, a_row_blk_off=0, b_col_off=0, tm=None, tn=None, tk=1024):
    tm = tm or ka
    tn = tn or nb
    nk = rows // tk
    assert ka % tm == 0 and nb % tn == 0 and rows % tk == 0 and b_col_off % tn == 0
    has_out = out is not None

    def body(*refs):
        a_ref, b_ref = refs[0], refs[1]
        o_ref, acc = refs[2 + has_out], refs[-1]
        part = lax.dot_general(a_ref[...].astype(BF16), b_ref[...].astype(BF16), (((0,), (0,)), ((), ())),
                               preferred_element_type=F32)
        k = pl.program_id(2)

        @pl.when(k == 0)
        def _():
            acc[...] = part

        @pl.when(k > 0)
        def _():
            acc[...] += part

        @pl.when(k == nk - 1)
        def _():
            o_ref[...] = acc[...].astype(o_ref.dtype)

    if b.ndim == 3:
        per = b.shape[2] // tn
        b_spec = pl.BlockSpec((None, tk, tn), lambda j, i, k: (j // per, k, j % per))
    else:
        b_spec = pl.BlockSpec((tk, tn), lambda j, i, k: (k, j + b_col_off // tn))
    in_specs = [pl.BlockSpec((tk, tm), lambda j, i, k: (k + a_row_blk_off, i)), b_spec]
    args = [a, b]
    aliases = {}
    if has_out:
        in_specs.append(ANY)
        aliases = {2: 0}
        args.append(out)
    return pl.pallas_call(
        body, name=name, grid=(nb // tn, ka // tm, nk),
        in_specs=in_specs,
        out_specs=pl.BlockSpec((None, tm, tn), lambda j, i, k: (l, i, j)),
        out_shape=jax.ShapeDtypeStruct((2, ka, nb), BF16),
        scratch_shapes=[pltpu.VMEM((tm, tn), F32)],
        input_output_aliases=aliases, compiler_params=_params(3))(*args)


def _row_spec(tm, width, col=0):
    return pl.BlockSpec((tm, width), lambda i: (i, col))


def _gain_spec(l, width=D):
    return pl.BlockSpec((None, 1, width), lambda i: (l, 0, 0))


def norm_fwd(x, g3, l, *, name, tm=512):
    T = x.shape[0]

    def body(x_ref, g_ref, o_ref):
        xv = x_ref[...]
        r = lax.rsqrt(jnp.mean(xv * xv, axis=-1, keepdims=True) + EPS)
        o_ref[...] = (xv * r * g_ref[...]).astype(BF16)

    return pl.pallas_call(
        body, name=name, grid=(T // tm,),
        in_specs=[_row_spec(tm, D), _gain_spec(l)], out_specs=_row_spec(tm, D),
        out_shape=jax.ShapeDtypeStruct((T, D), BF16), compiler_params=_params(1))(x, g3)


def _winsum_back(ext, w):
    s, span = ext, 1
    while span < w:
        s = s + pltpu.roll(s, span, 0)
        span *= 2
    return s


def _winsum_fwd(ext, w):
    rows = ext.shape[0]
    s, span = ext, 1
    while span < w:
        s = s + pltpu.roll(s, rows - span, 0)
        span *= 2
    return s


def _pooled(ext, z, t, g, w):
    sl = slice(g * PG, (g + 1) * PG)
    s = _winsum_back(ext[:, sl], w)[POOL_HALO:, :]
    return s / jnp.minimum(t + 1, w).astype(F32) - z[:, sl]


def pool_fwd(z, wpool, scale3, l, *, name, tm=256):
    T = z.shape[0]
    hb = tm // POOL_HALO

    def body(z_ref, zp_ref, wp_ref, sc_ref, o_ref):
        i = pl.program_id(0)
        zv = z_ref[...].astype(F32)
        prev = jnp.where(i == 0, 0.0, zp_ref[...].astype(F32))
        ext = jnp.concatenate([prev, zv], axis=0)
        t = i * tm + lax.broadcasted_iota(jnp.int32, (tm, 1), 0)
        for g, w in enumerate(POOL_WINDOWS):
            sl = slice(g * PG, (g + 1) * PG)
            pooled = _pooled(ext, zv, t, g, w)
            q = jnp.dot(pooled.astype(BF16), wp_ref[g], preferred_element_type=F32)
            o_ref[:, sl] = (q * sc_ref[:, sl]).astype(BF16)

    return pl.pallas_call(
        body, name=name, grid=(T // tm,),
        in_specs=[_row_spec(tm, D),
                  pl.BlockSpec((POOL_HALO, D), lambda i: (jnp.maximum(i * hb - 1, 0), 0)),
                  pl.BlockSpec((None, 4, PG, PG), lambda i: (l, 0, 0, 0)),
                  _gain_spec(l)],
        out_specs=_row_spec(tm, D),
        out_shape=jax.ShapeDtypeStruct((T, D), BF16), compiler_params=_params(1))(z, z, wpool, scale3)


def sgu_fwd(z, g3, wsm, bT, l, *, name, tm=256):
    T = z.shape[0]

    def body(zu_ref, zv_ref, g_ref, ws_ref, b_ref, o_ref):
        gu = _gelu(zu_ref[...].astype(F32))
        gv = _gelu(zv_ref[...].astype(F32))
        rv = lax.rsqrt(jnp.mean(gv * gv, axis=-1, keepdims=True) + EPS)
        vn = (gv * rv * g_ref[...]).astype(BF16)
        for n in range(tm // CHUNK):
            r = slice(n * CHUNK, (n + 1) * CHUNK)
            for h in range(HEADS):
                cs = slice(h * CHUNK, (h + 1) * CHUNK)
                mixed = jnp.dot(ws_ref[h], vn[r, cs], preferred_element_type=F32) + b_ref[:, h:h + 1]
                o_ref[r, cs] = (gu[r, cs] * mixed).astype(BF16)

    return pl.pallas_call(
        body, name=name, grid=(T // tm,),
        in_specs=[_row_spec(tm, D, 1), _row_spec(tm, D, 2), _gain_spec(l),
                  pl.BlockSpec((None, HEADS, CHUNK, CHUNK), lambda i: (l, 0, 0, 0)),
                  pl.BlockSpec((None, CHUNK, HEADS), lambda i: (l, 0, 0))],
        out_specs=_row_spec(tm, D),
        out_shape=jax.ShapeDtypeStruct((T, D), BF16), compiler_params=_params(1))(z, z, g3, wsm, bT)


def gate_fwd(z, yab, *, name, tm=512):
    T = z.shape[0]

    def body(za_ref, zb_ref, y_ref, o_ref):
        ga = jax.nn.sigmoid(za_ref[...].astype(F32))
        gb = jax.nn.sigmoid(zb_ref[...].astype(F32))
        o_ref[...] = (ga * y_ref[:, :D].astype(F32) + gb * y_ref[:, D:].astype(F32)).astype(BF16)

    return pl.pallas_call(
        body, name=name, grid=(T // tm,),
        in_specs=[_row_spec(tm, D, 3), _row_spec(tm, D, 4), _row_spec(tm, 2 * D)],
        out_specs=_row_spec(tm, D),
        out_shape=jax.ShapeDtypeStruct((T, D), BF16), compiler_params=_params(1))(z, z, yab)


def _conv(ext, w_ref, b_ref):
    c = b_ref[...] + w_ref[0:1, :] * pltpu.roll(ext, 2, 0)
    c = c + w_ref[1:2, :] * pltpu.roll(ext, 1, 0)
    return c + w_ref[2:3, :] * ext


def conv_fwd(up, convw, convb3, l, *, name, tm=256):
    T = up.shape[0]
    tc = CONV_TC
    nc = DFF // tc
    hb = tm // CONV_HALO

    def body(ua_ref, uap_ref, ub_ref, ubp_ref, wa_ref, wb_ref, ba_ref, bb_ref, o_ref):
        i = pl.program_id(1)

        def conv_of(u_ref, p_ref, w_ref, b_ref):
            ext = jnp.concatenate([jnp.where(i == 0, 0.0, p_ref[...]), u_ref[...]], axis=0)
            return _conv(ext, w_ref, b_ref)[CONV_HALO:, :]

        ca = conv_of(ua_ref, uap_ref, wa_ref, ba_ref)
        cb = conv_of(ub_ref, ubp_ref, wb_ref, bb_ref)
        o_ref[...] = (_gelu(ca) * cb).astype(BF16)

    def cur(off):
        return pl.BlockSpec((tm, tc), lambda j, i: (i, j + off))

    def prev(off):
        return pl.BlockSpec((CONV_HALO, tc), lambda j, i: (jnp.maximum(i * hb - 1, 0), j + off))

    def wspec(off):
        return pl.BlockSpec((None, CONV_ROWS, tc), lambda j, i: (l, 0, j + off))

    def bspec(off):
        return pl.BlockSpec((None, 1, tc), lambda j, i: (l, 0, j + off))

    return pl.pallas_call(
        body, name=name, grid=(nc, T // tm),
        in_specs=[cur(0), prev(0), cur(nc), prev(nc), wspec(0), wspec(nc), bspec(0), bspec(nc)],
        out_specs=pl.BlockSpec((tm, tc), lambda j, i: (i, j)),
        out_shape=jax.ShapeDtypeStruct((T, DFF), BF16),
        compiler_params=_params(2))(up, up, up, up, convw, convw, convb3, convb3)


def ple_fwd(x2, pg, e, g3, l, *, name, tm=512):
    T = x2.shape[0]
    has_norm = g3 is not None

    def body(x_ref, pg_ref, e_ref, *rest):
        xv = x_ref[...] + jax.nn.sigmoid(pg_ref[...].astype(F32)) * e_ref[...].astype(F32)
        if has_norm:
            g_ref, o_ref, h_ref = rest
            r = lax.rsqrt(jnp.mean(xv * xv, axis=-1, keepdims=True) + EPS)
            h_ref[...] = (xv * r * g_ref[...]).astype(BF16)
        else:
            o_ref, = rest
        o_ref[...] = xv

    x_shape = jax.ShapeDtypeStruct((T, D), F32)
    return pl.pallas_call(
        body, name=name, grid=(T // tm,),
        in_specs=[_row_spec(tm, D)] * 3 + ([_gain_spec(l)] if has_norm else []),
        out_specs=[_row_spec(tm, D)] * 2 if has_norm else _row_spec(tm, D),
        out_shape=[x_shape, jax.ShapeDtypeStruct((T, D), BF16)] if has_norm else x_shape,
        compiler_params=_params(1))(x2, pg, e, *([g3] if has_norm else []))


def loss_head(x, g3, tgt, *, name, tm=512):
    T = x.shape[0]

    def body(x_ref, g_ref, t_ref, loss_ref, dx_ref, dg_ref):
        @pl.when(pl.program_id(0) == 0)
        def _():
            loss_ref[...] = jnp.zeros_like(loss_ref)
            dg_ref[...] = jnp.zeros_like(dg_ref)

        xv, g = x_ref[...], g_ref[...]
        r = lax.rsqrt(jnp.mean(xv * xv, axis=-1, keepdims=True) + EPS)
        xh = xv * r
        err = xh * g - t_ref[...]
        loss_ref[...] += 0.5 * jnp.sum(jnp.mean(err * err, axis=-1, keepdims=True))
        dy = err * (1.0 / D)
        dyg = dy * g
        dx_ref[...] = r * (dyg - xh * jnp.mean(dyg * xh, axis=-1, keepdims=True))
        dg_ref[0:1, :] += jnp.sum(dy * xh, axis=0, keepdims=True)

    return pl.pallas_call(
        body, name=name, grid=(T // tm,),
        in_specs=[_row_spec(tm, D), pl.BlockSpec((1, D), lambda i: (0, 0)), _row_spec(tm, D)],
        out_specs=[pl.BlockSpec((8, LANES), lambda i: (0, 0)), _row_spec(tm, D),
                   pl.BlockSpec((8, D), lambda i: (0, 0))],
        out_shape=[jax.ShapeDtypeStruct((8, LANES), F32), jax.ShapeDtypeStruct((T, D), F32),
                   jax.ShapeDtypeStruct((8, D), F32)],
        compiler_params=_params(1))(x, g3, tgt)


def norm_bwd(dh, x, g3, l, dx_in, *, name, tm=512):
    T = x.shape[0]

    def body(dh_ref, x_ref, g_ref, dxi_ref, dx_ref, dg_ref):
        @pl.when(pl.program_id(0) == 0)
        def _():
            dg_ref[...] = jnp.zeros_like(dg_ref)

        xv, dh_v = x_ref[...], dh_ref[...].astype(F32)
        r = lax.rsqrt(jnp.mean(xv * xv, axis=-1, keepdims=True) + EPS)
        xh = xv * r
        dhg = dh_v * g_ref[...]
        dx_ref[...] = dxi_ref[...] + r * (dhg - xh * jnp.mean(dhg * xh, axis=-1, keepdims=True))
        dg_ref[0:1, :] += jnp.sum(dh_v * xh, axis=0, keepdims=True)

    return pl.pallas_call(
        body, name=name, grid=(T // tm,),
        in_specs=[_row_spec(tm, D), _row_spec(tm, D), _gain_spec(l), _row_spec(tm, D)],
        out_specs=[_row_spec(tm, D), pl.BlockSpec((8, D), lambda i: (0, 0))],
        out_shape=[jax.ShapeDtypeStruct((T, D), F32), jax.ShapeDtypeStruct((8, D), F32)],
        compiler_params=_params(1))(dh, x, g3, dx_in)


def ple_bwd(dx, pg, e, *, name, tm=512):
    T = dx.shape[0]

    def body(dx_ref, pg_ref, e_ref, de_ref, dpg_ref):
        gate = jax.nn.sigmoid(pg_ref[...].astype(F32))
        dxv = dx_ref[...]
        de_ref[...] = (dxv * gate).astype(BF16)
        dpg_ref[...] = (dxv * e_ref[...].astype(F32) * gate * (1.0 - gate)).astype(BF16)

    return pl.pallas_call(
        body, name=name, grid=(T // tm,),
        in_specs=[_row_spec(tm, D)] * 3, out_specs=[_row_spec(tm, D)] * 2,
        out_shape=[jax.ShapeDtypeStruct((T, D), BF16)] * 2, compiler_params=_params(1))(dx, pg, e)


def conv_bwd(df, up, convw, convb3, l, *, name, tm=256):
    T = up.shape[0]
    tc = CONV_TC
    nc = DFF // tc
    hb = tm // CONV_HALO
    nt = T // tm
    rows = tm + 2 * CONV_HALO
    own = slice(CONV_HALO, CONV_HALO + tm)

    def body(df_ref, dfn_ref, ua_ref, uap_ref, uan_ref, ub_ref, ubp_ref, ubn_ref, wa_ref, wb_ref, ba_ref, bb_ref,
             dup_ref, dcw_ref, dcb_ref):
        i = pl.program_id(1)

        @pl.when(i == 0)
        def _():
            dcw_ref[...] = jnp.zeros_like(dcw_ref)
            dcb_ref[...] = jnp.zeros_like(dcb_ref)

        def ext_of(c_ref, p_ref, n_ref):
            return jnp.concatenate([jnp.where(i == 0, 0.0, p_ref[...]), c_ref[...],
                                    jnp.where(i == nt - 1, 0.0, n_ref[...])], axis=0)

        ea = ext_of(ua_ref, uap_ref, uan_ref)
        eb = ext_of(ub_ref, ubp_ref, ubn_ref)
        ca = _conv(ea, wa_ref, ba_ref)
        cb = _conv(eb, wb_ref, bb_ref)
        df_ext = jnp.concatenate([jnp.zeros((CONV_HALO, tc), F32), df_ref[...],
                                  jnp.where(i == nt - 1, 0.0, dfn_ref[...])], axis=0)
        cdf = 0.5 * (1.0 + lax.erf(ca * 0.7071067811865476))
        da = df_ext * cb * (cdf + ca * jnp.exp(-0.5 * ca * ca) * 0.3989422804014327)
        db = df_ext * (ca * cdf)

        def finish(h, dc, e, w_ref):
            dup = w_ref[2:3, :] * dc + w_ref[1:2, :] * pltpu.roll(dc, rows - 1, 0)
            dup = dup + w_ref[0:1, :] * pltpu.roll(dc, rows - 2, 0)
            dup_ref[h] = dup[own, :].astype(BF16)
            dco = dc[own, :]
            dcb_ref[h, 0:1, :] += jnp.sum(dco, axis=0, keepdims=True)
            dcw_ref[h, 0:1, :] += jnp.sum(dco * pltpu.roll(e, 2, 0)[own, :], axis=0, keepdims=True)
            dcw_ref[h, 1:2, :] += jnp.sum(dco * pltpu.roll(e, 1, 0)[own, :], axis=0, keepdims=True)
            dcw_ref[h, 2:3, :] += jnp.sum(dco * e[own, :], axis=0, keepdims=True)

        finish(0, da, ea, wa_ref)
        finish(1, db, eb, wb_ref)

    def nxt(i):
        return jnp.minimum((i + 1) * hb, T // CONV_HALO - 1)

    def prv(i):
        return jnp.maximum(i * hb - 1, 0)

    def up_specs(off):
        return [pl.BlockSpec((tm, tc), lambda j, i: (i, j + off)),
                pl.BlockSpec((CONV_HALO, tc), lambda j, i: (prv(i), j + off)),
                pl.BlockSpec((CONV_HALO, tc), lambda j, i: (nxt(i), j + off))]

    in_specs = [pl.BlockSpec((tm, tc), lambda j, i: (i, j)),
                pl.BlockSpec((CONV_HALO, tc), lambda j, i: (nxt(i), j)),
                *up_specs(0), *up_specs(nc),
                pl.BlockSpec((None, CONV_ROWS, tc), lambda j, i: (l, 0, j)),
                pl.BlockSpec((None, CONV_ROWS, tc), lambda j, i: (l, 0, j + nc)),
                pl.BlockSpec((None, 1, tc), lambda j, i: (l, 0, j)),
                pl.BlockSpec((None, 1, tc), lambda j, i: (l, 0, j + nc))]
    return pl.pallas_call(
        body, name=name, grid=(nc, nt), in_specs=in_specs,
        out_specs=[pl.BlockSpec((2, tm, tc), lambda j, i: (0, i, j)),
                   pl.BlockSpec((2, 8, tc), lambda j, i: (0, 0, j)),
                   pl.BlockSpec((2, 8, tc), lambda j, i: (0, 0, j))],
        out_shape=[jax.ShapeDtypeStruct((2, T, DFF), BF16), jax.ShapeDtypeStruct((2, 8, DFF), F32),
                   jax.ShapeDtypeStruct((2, 8, DFF), F32)],
        compiler_params=_params(2))(df, df, up, up, up, up, up, up, convw, convw, convb3, convb3)


def gate_bwd(dmo, z, yab, *, name, tm=512):
    T = z.shape[0]

    def body(dmo_ref, zg_ref, y_ref, dz_ref, dy_ref):
        g = jax.nn.sigmoid(zg_ref[...].astype(F32))
        dmo_v = dmo_ref[...].astype(F32)
        dy_ref[...] = (dmo_v * g).astype(BF16)
        dz_ref[...] = (dmo_v * y_ref[...].astype(F32) * g * (1.0 - g)).astype(BF16)

    return pl.pallas_call(
        body, name=name, grid=(T // tm, 2),
        in_specs=[pl.BlockSpec((tm, D), lambda i, s: (i, 0)),
                  pl.BlockSpec((tm, D), lambda i, s: (i, 3 + s)),
                  pl.BlockSpec((tm, D), lambda i, s: (i, s))],
        out_specs=[pl.BlockSpec((tm, D), lambda i, s: (i, 3 + s)),
                   pl.BlockSpec((tm, D), lambda i, s: (i, s))],
        out_shape=[jax.ShapeDtypeStruct((T, 5 * D), BF16), jax.ShapeDtypeStruct((T, 2 * D), BF16)],
        compiler_params=_params(2))(dmo, z, yab)


def mixer_bwd(da, ds, z, dz, wpool, scale3, g3, wsm, wsmT, bT, l, *, name, tm=256):
    T = z.shape[0]
    hb = tm // POOL_HALO
    nt = T // tm

    def body(da_ref, dan_ref, ds_ref, zp_ref, zpp_ref, zu_ref, zv_ref, wp_ref, sc_ref, g_ref, ws_ref, wst_ref,
             b_ref, dzin_ref, dz_ref, dwp_ref, dsc_ref, dws_ref, dbt_ref, dgs_ref, mixed_scr, dvn_scr, db_scr):
        del dzin_ref
        i = pl.program_id(0)

        @pl.when(i == 0)
        def _():
            dwp_ref[...] = jnp.zeros_like(dwp_ref)
            dsc_ref[...] = jnp.zeros_like(dsc_ref)
            dws_ref[...] = jnp.zeros_like(dws_ref)
            dgs_ref[...] = jnp.zeros_like(dgs_ref)
            db_scr[...] = jnp.zeros_like(db_scr)

        zv_p = zp_ref[...].astype(F32)
        ext = jnp.concatenate([jnp.where(i == 0, 0.0, zpp_ref[...].astype(F32)), zv_p], axis=0)
        da_v = da_ref[...].astype(F32)
        da_ext = jnp.concatenate([da_v, jnp.where(i == nt - 1, 0.0, dan_ref[...].astype(F32))], axis=0)
        t = i * tm + lax.broadcasted_iota(jnp.int32, (tm, 1), 0)
        t_ext = i * tm + lax.broadcasted_iota(jnp.int32, (tm + POOL_HALO, 1), 0)
        for g, w in enumerate(POOL_WINDOWS):
            sl = slice(g * PG, (g + 1) * PG)
            pooled = _pooled(ext, zv_p, t, g, w).astype(BF16)
            q = jnp.dot(pooled, wp_ref[g], preferred_element_type=F32)
            dsc_ref[0:1, sl] += jnp.sum(da_v[:, sl] * q, axis=0, keepdims=True)
            dq_ext = (da_ext[:, sl] * sc_ref[:, sl]).astype(BF16)
            dwp_ref[g] += lax.dot_general(pooled, dq_ext[:tm, :], (((0,), (0,)), ((), ())),
                                          preferred_element_type=F32)
            dpool = lax.dot_general(dq_ext, wp_ref[g], (((1,), (1,)), ((), ())), preferred_element_type=F32)
            spread = _winsum_fwd(dpool / jnp.minimum(t_ext + 1, w).astype(F32), w)
            dz_ref[:, sl] = (spread[:tm, :] - dpool[:tm, :]).astype(BF16)

        zu, zv, ds_v = zu_ref[...].astype(F32), zv_ref[...].astype(F32), ds_ref[...].astype(F32)
        gain = g_ref[...]
        gu, gv = _gelu(zu), _gelu(zv)
        rv = lax.rsqrt(jnp.mean(gv * gv, axis=-1, keepdims=True) + EPS)
        vh = gv * rv
        vn = (vh * gain).astype(BF16)
        dmix = ds_v * gu
        dmix_b = dmix.astype(BF16)
        for n in range(tm // CHUNK):
            r = slice(n * CHUNK, (n + 1) * CHUNK)
            db_scr[...] += dmix[r, :]
            for h in range(HEADS):
                cs = slice(h * CHUNK, (h + 1) * CHUNK)
                mixed_scr[r, cs] = jnp.dot(ws_ref[h], vn[r, cs], preferred_element_type=F32) + b_ref[:, h:h + 1]
                dws_ref[h] += lax.dot_general(dmix_b[r, cs], vn[r, cs], (((1,), (1,)), ((), ())),
                                              preferred_element_type=F32)
                dvn_scr[r, cs] = jnp.dot(wst_ref[h], dmix_b[r, cs], preferred_element_type=F32)
        dz_ref[:, D:2 * D] = (ds_v * mixed_scr[...] * _gelu_grad(zu)).astype(BF16)
        dvn = dvn_scr[...]
        dgs_ref[0:1, :] += jnp.sum(dvn * vh, axis=0, keepdims=True)
        dvg = dvn * gain
        dgv = rv * (dvg - vh * jnp.mean(dvg * vh, axis=-1, keepdims=True))
        dz_ref[:, 2 * D:3 * D] = (dgv * _gelu_grad(zv)).astype(BF16)

        @pl.when(i == nt - 1)
        def _():
            tril = (lax.broadcasted_iota(jnp.int32, (CHUNK, CHUNK), 0)
                    >= lax.broadcasted_iota(jnp.int32, (CHUNK, CHUNK), 1)).astype(F32)
            for h in range(HEADS):
                dws_ref[h] = dws_ref[h] * tril
                dbt_ref[:, h:h + 1] = jnp.sum(db_scr[:, h * CHUNK:(h + 1) * CHUNK], axis=1, keepdims=True)

    const4 = lambda i: (l, 0, 0, 0)
    in_specs = [
        _row_spec(tm, D),
        pl.BlockSpec((POOL_HALO, D), lambda i: (jnp.minimum((i + 1) * hb, T // POOL_HALO - 1), 0)),
        _row_spec(tm, D),
        _row_spec(tm, D, 0),
        pl.BlockSpec((POOL_HALO, D), lambda i: (jnp.maximum(i * hb - 1, 0), 0)),
        _row_spec(tm, D, 1), _row_spec(tm, D, 2),
        pl.BlockSpec((None, 4, PG, PG), const4),
        _gain_spec(l), _gain_spec(l),
        pl.BlockSpec((None, HEADS, CHUNK, CHUNK), const4),
        pl.BlockSpec((None, HEADS, CHUNK, CHUNK), const4),
        pl.BlockSpec((None, CHUNK, HEADS), lambda i: (l, 0, 0)),
        ANY,
    ]
    out_specs = [
        pl.BlockSpec((tm, 3 * D), lambda i: (i, 0)),
        pl.BlockSpec((4, PG, PG), lambda i: (0, 0, 0)),
        pl.BlockSpec((8, D), lambda i: (0, 0)),
        pl.BlockSpec((HEADS, CHUNK, CHUNK), lambda i: (0, 0, 0)),
        pl.BlockSpec((CHUNK, HEADS), lambda i: (0, 0)),
        pl.BlockSpec((8, D), lambda i: (0, 0)),
    ]
    out_shape = [
        jax.ShapeDtypeStruct((T, 5 * D), BF16), jax.ShapeDtypeStruct((4, PG, PG), F32),
        jax.ShapeDtypeStruct((8, D), F32), jax.ShapeDtypeStruct((HEADS, CHUNK, CHUNK), F32),
        jax.ShapeDtypeStruct((CHUNK, HEADS), F32), jax.ShapeDtypeStruct((8, D), F32),
    ]
    return pl.pallas_call(
        body, name=name, grid=(nt,), in_specs=in_specs, out_specs=out_specs, out_shape=out_shape,
        scratch_shapes=[pltpu.VMEM((tm, D), F32), pltpu.VMEM((tm, D), F32), pltpu.VMEM((CHUNK, D), F32)],
        input_output_aliases={13: 0}, compiler_params=_params(1))(
            da, da, ds, z, z, z, z, wpool, scale3, g3, wsm, wsmT, bT, dz)


def _row_tile(rows, cols, sub):
    cap = max(sub, (2 * 1024 * 1024) // (4 * cols))
    best = None
    for tr in range(sub, min(rows, cap) + 1, sub):
        if rows % tr == 0:
            best = tr
    return best or rows


def elementwise(fn, ins, out_dtypes, *, name, row_blk_offs=None, rows=None, uses_mesh=False):
    cols = ins[0].shape[1]
    rows = rows or ins[0].shape[0]
    tr = _row_tile(rows, cols, 16)
    offs = row_blk_offs or [0] * len(ins)
    n_in = len(ins)

    def body(*refs):
        outs = fn(*[r[...] for r in refs[:n_in]])
        for o_ref, o in zip(refs[n_in:], outs):
            o_ref[...] = o.astype(o_ref.dtype)

    del uses_mesh
    return pl.pallas_call(
        body, name=name, grid=(rows // tr,),
        in_specs=[pl.BlockSpec((tr, cols), functools.partial(lambda i, o: (i + o * (rows // tr), 0), o=o))
                  for o in offs],
        out_specs=[pl.BlockSpec((tr, cols), lambda i: (i, 0)) for _ in out_dtypes],
        out_shape=[jax.ShapeDtypeStruct((rows, cols), dt) for dt in out_dtypes],
        compiler_params=_params(1))(*ins)


def _adamw(w, g, m, v):
    m = ADAM_B1 * m + (1.0 - ADAM_B1) * g
    v = ADAM_B2 * v + (1.0 - ADAM_B2) * jnp.square(g)
    m_hat = m / (1.0 - ADAM_B1 ** ADAM_STEP)
    v_hat = v / (1.0 - ADAM_B2 ** ADAM_STEP)
    delta = -ADAM_LR * (m_hat / (jnp.sqrt(v_hat) + ADAM_EPS) + ADAM_WD * w)
    return delta, m, v


def _view2d(a):
    return a.reshape(-1, a.shape[-1])


def _place():
    x, y, c = lax.axis_index("x"), lax.axis_index("y"), lax.axis_index("c")
    others = [(1 - x, y), (x, 1 - y), (1 - x, 1 - y)]
    return x, y, c, 2 * x + y, others


def _remote(src, dst, send_sems, recv_sems, k, to):
    return pltpu.make_async_remote_copy(src_ref=src, dst_ref=dst, send_sem=send_sems.at[k], recv_sem=recv_sems.at[k],
                                        device_id=to, device_id_type=MESH)


def place_shard(src, axis, size, out_dtype, place, *, name):
    shard = src.shape[1:]
    natural = tuple(size * N_CHIPS if a == axis else s for a, s in enumerate(shard))
    if len(shard) == 3:
        blk = (None,) + shard
        grid = (2, 1)
        in_map = lambda lyr, i, pr: (lyr, 0, 0, 0)
        out_map = lambda lyr, i, pr: (lyr, 0, pr[0], 0)
    else:
        tr = _row_tile(shard[0], shard[1], 16)
        steps = shard[0] // tr
        blk = (None, tr, shard[1])
        grid = (2, steps)
        in_map = lambda lyr, i, pr: (lyr, i, 0)
        if axis == 0:
            out_map = lambda lyr, i, pr: (lyr, pr[0] * steps + i, 0)
        else:
            out_map = lambda lyr, i, pr: (lyr, i, pr[0])

    def body(pr_ref, s_ref, o_ref):
        del pr_ref
        o_ref[...] = s_ref[...].astype(o_ref.dtype)

    return pl.pallas_call(
        body, name=name,
        grid_spec=pltpu.PrefetchScalarGridSpec(
            num_scalar_prefetch=1, grid=grid, in_specs=[pl.BlockSpec(blk, in_map)],
            out_specs=pl.BlockSpec(blk, out_map)),
        out_shape=jax.ShapeDtypeStruct((2,) + natural, out_dtype), compiler_params=_params(2))(place, src)


def all_gather_weights(arrays, geom):
    n = len(arrays)

    def body(*refs):
        outs = refs[n:2 * n]
        send_sems, recv_sems = refs[2 * n:]
        x, y, c, j, others = _place()
        sib = (x, y, 1 - c)
        pending = []
        for t, (axis, size) in enumerate(geom):
            mine = _block(outs[t].at[c], axis, j, size)
            for k, (ox, oy) in enumerate(others):
                cp = _remote(mine, mine, send_sems, recv_sems, 6 * t + k, (ox, oy, c))
                cp.start()
                pending.append(cp.wait_send)
        for t, (axis, size) in enumerate(geom):
            for k, (ox, oy) in enumerate(others):
                landed = _block(outs[t].at[c], axis, 2 * ox + oy, size)
                _remote(landed, landed, send_sems, recv_sems, 6 * t + k, (ox, oy, c)).wait_recv()
                fwd = _remote(landed, landed, send_sems, recv_sems, 6 * t + 3 + k, sib)
                fwd.start()
                pending.append(fwd.wait_send)
        for t, (axis, size) in enumerate(geom):
            for k, (ox, oy) in enumerate(others):
                got = _block(outs[t].at[1 - c], axis, 2 * ox + oy, size)
                _remote(got, got, send_sems, recv_sems, 6 * t + 3 + k, sib).wait_recv()
        for wait in pending:
            wait()

    return pl.pallas_call(
        body, name="all_gather_weights", in_specs=[ANY] * n, out_specs=[ANY] * n,
        out_shape=[jax.ShapeDtypeStruct(a.shape, a.dtype) for a in arrays],
        scratch_shapes=[pltpu.SemaphoreType.DMA((6 * n,)), pltpu.SemaphoreType.DMA((6 * n,))],
        input_output_aliases={t: t for t in range(n)},
        compiler_params=pltpu.CompilerParams(has_side_effects=True))(*arrays)


def swap_layers_with_sibling(grads):
    n = len(grads)

    def body(*refs):
        srcs, outs = refs[:n], refs[n:2 * n]
        send_sems, recv_sems = refs[2 * n:]
        x, y, c, _, _ = _place()
        cps = [_remote(srcs[t].at[1 - c], outs[t], send_sems, recv_sems, t, (x, y, 1 - c)) for t in range(n)]
        for cp in cps:
            cp.start()
        for cp in cps:
            cp.wait()

    return pl.pallas_call(
        body, name="swap_layers_with_sibling", in_specs=[ANY] * n, out_specs=[ANY] * n,
        out_shape=[jax.ShapeDtypeStruct(g.shape[1:], g.dtype) for g in grads],
        scratch_shapes=[pltpu.SemaphoreType.DMA((n,)), pltpu.SemaphoreType.DMA((n,))],
        compiler_params=pltpu.CompilerParams(has_side_effects=True))(*grads)


def scatter_to_chips(parts, geom):
    n = len(parts)

    def body(*refs):
        srcs, outs = refs[:n], refs[n:2 * n]
        send_sems, recv_sems = refs[2 * n:]
        x, y, c, j, others = _place()
        cps = []
        for t, (axis, size) in enumerate(geom):
            for k, (ox, oy) in enumerate(others):
                cps.append(_remote(_block(srcs[t], axis, 2 * ox + oy, size), outs[t].at[k], send_sems, recv_sems,
                                   3 * t + k, (ox, oy, c)))
        for cp in cps:
            cp.start()
        for cp in cps:
            cp.wait_recv()
        for cp in cps:
            cp.wait_send()

    out_shape = [jax.ShapeDtypeStruct((3,) + _shard_shape(p.shape, axis, size), p.dtype)
                 for p, (axis, size) in zip(parts, geom)]
    return pl.pallas_call(
        body, name="scatter_to_chips", in_specs=[ANY] * n, out_specs=[ANY] * n, out_shape=out_shape,
        scratch_shapes=[pltpu.SemaphoreType.DMA((3 * n,)), pltpu.SemaphoreType.DMA((3 * n,))],
        compiler_params=pltpu.CompilerParams(has_side_effects=True))(*parts)


def share_with_sibling(grads):
    n = len(grads)

    def body(*refs):
        outs = refs[n:2 * n]
        send_sems, recv_sems = refs[2 * n:]
        x, y, c, _, _ = _place()
        cps = [_remote(outs[t].at[c], outs[t].at[c], send_sems, recv_sems, t, (x, y, 1 - c)) for t in range(n)]
        for cp in cps:
            cp.start()
        for t in range(n):
            got = outs[t].at[1 - c]
            _remote(got, got, send_sems, recv_sems, t, (x, y, 1 - c)).wait_recv()
        for cp in cps:
            cp.wait_send()

    return pl.pallas_call(
        body, name="share_with_sibling", in_specs=[ANY] * n, out_specs=[ANY] * n,
        out_shape=[jax.ShapeDtypeStruct(g.shape, g.dtype) for g in grads],
        scratch_shapes=[pltpu.SemaphoreType.DMA((n,)), pltpu.SemaphoreType.DMA((n,))],
        input_output_aliases={t: t for t in range(n)},
        compiler_params=pltpu.CompilerParams(has_side_effects=True))(*grads)


def all_reduce_small(s):
    rows = s.shape[0]
    half = rows // 2
    assert half % 8 == 0

    def body(s_ref, o_ref, a_ref, b_ref, p_ref, send_sems, recv_sems):
        x, y, c, j, others = _place()
        sib = (x, y, 1 - c)
        swap = _remote(s_ref, a_ref, send_sems, recv_sems, 0, sib)
        swap.start()
        swap.wait()
        p_ref[...] = s_ref[...] + a_ref[...]
        mine = pl.ds(pl.multiple_of(c * half, 8), half)
        b_ref[j] = p_ref[mine, :]
        cps = [_remote(p_ref.at[mine, :], b_ref.at[j], send_sems, recv_sems, 1 + k, (ox, oy, c))
               for k, (ox, oy) in enumerate(others)]
        for cp in cps:
            cp.start()
        for k, (ox, oy) in enumerate(others):
            slot = b_ref.at[2 * ox + oy]
            _remote(slot, slot, send_sems, recv_sems, 1 + k, (ox, oy, c)).wait_recv()
        for cp in cps:
            cp.wait_send()
        o_ref[mine, :] = ((b_ref[0] + b_ref[1]) + b_ref[2]) + b_ref[3]
        back = _remote(o_ref.at[mine, :], o_ref.at[mine, :], send_sems, recv_sems, 4, sib)
        back.start()
        back.wait_send()
        theirs = pl.ds(pl.multiple_of((1 - c) * half, 8), half)
        _remote(o_ref.at[theirs, :], o_ref.at[theirs, :], send_sems, recv_sems, 4, sib).wait_recv()

    vmem = pl.BlockSpec(memory_space=pltpu.VMEM)
    return pl.pallas_call(
        body, name="all_reduce_small", in_specs=[vmem], out_specs=vmem,
        out_shape=jax.ShapeDtypeStruct((rows, LANES), F32),
        scratch_shapes=[pltpu.VMEM((rows, LANES), F32), pltpu.VMEM((N_CHIPS, half, LANES), F32),
                        pltpu.VMEM((rows, LANES), F32), pltpu.SemaphoreType.DMA((5,)),
                        pltpu.SemaphoreType.DMA((5,))],
        compiler_params=pltpu.CompilerParams(vmem_limit_bytes=VMEM_LIMIT, has_side_effects=True))(s)


def pair_sum(g, got, *, name):
    shape = g.shape[1:]
    cols = shape[-1]
    g2 = g.reshape(2, -1, cols)
    got2 = got.reshape(-1, cols)
    rows = got2.shape[0]
    tr = _row_tile(rows, cols, 16)

    def body(g0_ref, g1_ref, got_ref, o_ref):
        own = jnp.where(lax.axis_index("c") == 0, g0_ref[...].astype(F32), g1_ref[...].astype(F32))
        o_ref[...] = (own + got_ref[...].astype(F32)).astype(BF16)

    out = pl.pallas_call(
        body, name=name, grid=(rows // tr,),
        in_specs=[pl.BlockSpec((None, tr, cols), lambda i: (0, i, 0)),
                  pl.BlockSpec((None, tr, cols), lambda i: (1, i, 0)),
                  pl.BlockSpec((tr, cols), lambda i: (i, 0))],
        out_specs=pl.BlockSpec((tr, cols), lambda i: (i, 0)),
        out_shape=jax.ShapeDtypeStruct((rows, cols), BF16), compiler_params=_params(1))(g2, g2, got2)
    return out.reshape(shape)


def chip_sum(part, slots, axis, size, place, *, name):
    shard = slots.shape[1:]
    if len(shard) == 3:
        grid = (1,)
        p_spec = pl.BlockSpec(shard, lambda i, pr: (0, pr[0], 0))
        s_specs = [pl.BlockSpec((None,) + shard, functools.partial(lambda i, pr, k: (k, 0, 0, 0), k=k))
                   for k in range(3)]
        o_spec = pl.BlockSpec((None,) + shard, lambda i, pr: (pr[1], 0, 0, 0))
    else:
        tr = _row_tile(shard[0], shard[1], 16)
        steps = shard[0] // tr
        grid = (steps,)
        if axis == 0:
            p_spec = pl.BlockSpec((tr, shard[1]), lambda i, pr: (pr[0] * steps + i, 0))
        else:
            p_spec = pl.BlockSpec((tr, shard[1]), lambda i, pr: (i, pr[0]))
        s_specs = [pl.BlockSpec((None, tr, shard[1]), functools.partial(lambda i, pr, k: (k, i, 0), k=k))
                   for k in range(3)]
        o_spec = pl.BlockSpec((None, tr, shard[1]), lambda i, pr: (pr[1], i, 0))

    def body(pr_ref, p_ref, s0_ref, s1_ref, s2_ref, o_ref):
        del pr_ref
        o_ref[...] = ((p_ref[...].astype(F32) + s0_ref[...].astype(F32)) + s1_ref[...].astype(F32)) \
            + s2_ref[...].astype(F32)

    return pl.pallas_call(
        body, name=name,
        grid_spec=pltpu.PrefetchScalarGridSpec(num_scalar_prefetch=1, grid=grid, in_specs=[p_spec] + s_specs,
                                               out_specs=o_spec),
        out_shape=jax.ShapeDtypeStruct((2,) + shard, F32), compiler_params=_params(1))(
            place, part, slots, slots, slots)


def _local_step(x, p2, tgt, W, small):
    T = x.shape[0]
    as3 = lambda a: a.reshape(2, 1, a.shape[-1])
    mix3, scale3, sgu3 = as3(small["mix_norm"]), as3(small["pool_scale"]), as3(small["sgu_norm"])
    ffn3, ple3, convb3 = as3(small["ffn_norm"]), as3(small["ple_norm"]), as3(small["conv_b"])
    tril = jnp.tril(jnp.ones((CHUNK, CHUNK), F32))
    ws_masked = small["w_spatial"] * tril
    wsm = ws_masked.astype(BF16)
    wsmT = jnp.swapaxes(ws_masked, -1, -2).astype(BF16)
    bT = jnp.swapaxes(small["b_spatial"], -1, -2)
    final3 = small["final_norm"].reshape(1, D)

    saved = []
    hb = norm_fwd(x, mix3, 0, name="mix_norm_fwd_l0")
    for l in range(2):
        n = lambda s: f"{s}_l{l}"
        z = mm_nn(hb, W["w_in"], l, name=n("in_proj"), rows=T, tn=1280, out_dtype=BF16)
        a_in = pool_fwd(z, W["w_pool"], scale3, l, name=n("pool_fwd"))
        s_in = sgu_fwd(z, sgu3, wsm, bT, l, name=n("sgu_fwd"))
        yab = mm_nn(a_in, W["w_branch_a"], l, name=n("branch_a"), rows=T, out_cols=2 * D, out_dtype=BF16)
        yab = mm_nn(s_in, W["w_branch_b"], l, name=n("branch_b"), rows=T, out=yab, out_cols=2 * D, out_col_off=D,
                    out_dtype=BF16)
        mo = gate_fwd(z, yab, name=n("gate_fwd"))
        x1, h2b = mm_nn(mo, W["w_out"], l, name=n("out_proj"), rows=T, resid=x, norm_gain=ffn3)
        up = mm_nn(h2b, W["w_up"], l, name=n("up_proj"), rows=T, tn=1408)
        f = conv_fwd(up, W["conv_w"], convb3, l, name=n("conv_fwd"))
        x2, h3b = mm_nn(f, W["w_down"], l, name=n("down_proj"), rows=T, resid=x1, tk=1408, norm_gain=ple3)
        pg = mm_nn(h3b, W["w_ple_gate"], l, name=n("ple_gate_proj"), rows=T, out_dtype=BF16)
        e = mm_nn(p2, W["w_ple"], l, name=n("ple_proj"), rows=T, a_row_blk_off=l * (T // 1024), out_dtype=BF16)
        saved.append(dict(x=x, hb=hb, z=z, a_in=a_in, s_in=s_in, yab=yab, mo=mo, x1=x1, h2b=h2b, up=up, f=f,
                          x2=x2, h3b=h3b, pg=pg, e=e))
        if l == 0:
            x, hb = ple_fwd(x2, pg, e, mix3, 1, name=n("ple_fwd"))
        else:
            x = ple_fwd(x2, pg, e, None, 0, name=n("ple_fwd"))

    loss_acc, dx, dg_final = loss_head(x, final3, tgt, name="loss_head")

    G = {name: None for (name, _, _, _) in BIG}
    small_grads = [None, None]
    d_pool = [None, None]
    for l in (1, 0):
        n = lambda s: f"{s}_l{l}"
        a = saved[l]
        de, dpg = ple_bwd(dx, a["pg"], a["e"], name=n("ple_bwd"))
        G["w_ple"] = mm_tn(p2, de, l, name=n("d_w_ple"), rows=T, ka=PG, nb=D, out=G["w_ple"],
                           a_row_blk_off=l * (T // 1024))
        G["w_ple_gate"] = mm_tn(a["h3b"], dpg, l, name=n("d_w_ple_gate"), rows=T, ka=D, nb=D, out=G["w_ple_gate"])
        dh3 = mm_nt(dpg, W["w_ple_gate"], l, name=n("d_ple_norm_out"), rows=T)
        dx2, dg_ple = norm_bwd(dh3, a["x2"], ple3, l, dx, name=n("ple_norm_bwd"))
        df = mm_nt(dx2, W["w_down"], l, name=n("d_ffn_act"), rows=T, tn=1408, out_dtype=F32)
        G["w_down"] = mm_tn(a["f"], dx2, l, name=n("d_w_down"), rows=T, ka=DFF, nb=D, out=G["w_down"], tm=1408)
        dup, dcw, dcb = conv_bwd(df, a["up"], W["conv_w"], convb3, l, name=n("conv_bwd"))
        G["w_up"] = mm_tn(a["h2b"], dup, l, name=n("d_w_up"), rows=T, ka=D, nb=2 * DFF, out=G["w_up"], tn=1408)
        dh2 = mm_nt(dup, W["w_up"], l, name=n("d_ffn_norm_out"), rows=T, tk=1408)
        dx1, dg_ffn = norm_bwd(dh2, a["x1"], ffn3, l, dx2, name=n("ffn_norm_bwd"))
        dmo = mm_nt(dx1, W["w_out"], l, name=n("d_gated"), rows=T)
        G["w_out"] = mm_tn(a["mo"], dx1, l, name=n("d_w_out"), rows=T, ka=D, nb=D, out=G["w_out"])
        dz, dyab = gate_bwd(dmo, a["z"], a["yab"], name=n("gate_bwd"))
        G["w_branch_a"] = mm_tn(a["a_in"], dyab, l, name=n("d_w_branch_a"), rows=T, ka=D, nb=D,
                                out=G["w_branch_a"])
        G["w_branch_b"] = mm_tn(a["s_in"], dyab, l, name=n("d_w_branch_b"), rows=T, ka=D, nb=D,
                                out=G["w_branch_b"], b_col_off=D)
        da = mm_nt(dyab, W["w_branch_a"], l, name=n("d_pool_out"), rows=T, kdim=D)
        ds = mm_nt(dyab, W["w_branch_b"], l, name=n("d_sgu_out"), rows=T, kdim=D, a_col_off=D)
        dz, dwp, dsc, dws, dbt, dgs = mixer_bwd(da, ds, a["z"], dz, W["w_pool"], scale3, sgu3, wsm, wsmT, bT, l,
                                                name=n("mixer_bwd"))
        G["w_in"] = mm_tn(a["hb"], dz, l, name=n("d_w_in"), rows=T, ka=D, nb=5 * D, out=G["w_in"], tn=1280)
        dh = mm_nt(dz, W["w_in"], l, name=n("d_mix_norm_out"), rows=T, tk=1280)
        dx, dg_mix = norm_bwd(dh, a["x"], mix3, l, dx1, name=n("mix_norm_bwd"))
        d_pool[l] = dwp
        small_grads[l] = dict(
            mix_norm=dg_mix[0], pool_scale=dsc[0], sgu_norm=dgs[0], w_spatial=dws, b_spatial=dbt.T,
            ffn_norm=dg_ffn[0], conv_b=jnp.concatenate([dcb[0, 0], dcb[1, 0]]), ple_norm=dg_ple[0],
            conv_w=jnp.concatenate([dcw[0, :3], dcw[1, :3]], axis=1))
    G["w_pool"] = jnp.stack(d_pool).astype(BF16)
    return loss_acc, dx, G, small_grads, dg_final[0]


SMALL_ORDER = ("mix_norm", "pool_scale", "sgu_norm", "w_spatial", "b_spatial", "ffn_norm", "conv_b", "ple_norm",
               "conv_w")


def _pack_rows(pieces, row_multiple):
    flat = jnp.concatenate([a.reshape(-1) for a in pieces])
    rows = -(-flat.shape[0] // LANES)
    rows = -(-rows // row_multiple) * row_multiple
    return jnp.pad(flat, (0, rows * LANES - flat.shape[0])).reshape(rows, LANES)


def _unpack(flat, shapes):
    out, off = [], 0
    for shp in shapes:
        size = 1
        for s in shp:
            size *= s
        out.append(flat[off:off + size].reshape(shp))
        off += size
    return out


def kernel(x, p, mix_norm, w_in, w_pool, pool_scale, sgu_norm, w_spatial, b_spatial, w_branch_a, w_branch_b, w_out, ffn_norm, w_up, conv_w, conv_b, w_down, ple_norm, w_ple_gate, w_ple, final_norm, loss_target, m_mix_norm, m_w_in, m_w_pool, m_pool_scale, m_sgu_norm, m_w_spatial, m_b_spatial, m_w_branch_a, m_w_branch_b, m_w_out, m_ffn_norm, m_w_up, m_conv_w, m_conv_b, m_w_down, m_ple_norm, m_w_ple_gate, m_w_ple, m_final_norm, v_mix_norm, v_w_in, v_w_pool, v_pool_scale, v_sgu_norm, v_w_spatial, v_b_spatial, v_w_branch_a, v_w_branch_b, v_w_out, v_ffn_norm, v_w_up, v_conv_w, v_conv_b, v_w_down, v_ple_norm, v_w_ple_gate, v_w_ple, v_final_norm):
    names = ["mix_norm", "w_in", "w_pool", "pool_scale", "sgu_norm", "w_spatial", "b_spatial", "w_branch_a",
             "w_branch_b", "w_out", "ffn_norm", "w_up", "conv_w", "conv_b", "w_down", "ple_norm", "w_ple_gate",
             "w_ple", "final_norm"]
    w = dict(zip(names, [mix_norm, w_in, w_pool, pool_scale, sgu_norm, w_spatial, b_spatial, w_branch_a, w_branch_b,
                         w_out, ffn_norm, w_up, conv_w, conv_b, w_down, ple_norm, w_ple_gate, w_ple, final_norm]))
    m = dict(zip(names, [m_mix_norm, m_w_in, m_w_pool, m_pool_scale, m_sgu_norm, m_w_spatial, m_b_spatial,
                         m_w_branch_a, m_w_branch_b, m_w_out, m_ffn_norm, m_w_up, m_conv_w, m_conv_b, m_w_down,
                         m_ple_norm, m_w_ple_gate, m_w_ple, m_final_norm]))
    v = dict(zip(names, [v_mix_norm, v_w_in, v_w_pool, v_pool_scale, v_sgu_norm, v_w_spatial, v_b_spatial,
                         v_w_branch_a, v_w_branch_b, v_w_out, v_ffn_norm, v_w_up, v_conv_w, v_conv_b, v_w_down,
                         v_ple_norm, v_w_ple_gate, v_w_ple, v_final_norm]))
    T = x.shape[1]
    chip = 2 * lax.axis_index("x") + lax.axis_index("y")
    place = jnp.stack([chip, lax.axis_index("c")]).astype(jnp.int32)

    geom = [(axis, size) for (_, _, axis, size) in BIG]
    placed = [place_shard(w[name], axis, size, BF16, place, name=f"place_{name}") for (name, _, axis, size) in BIG]
    conv_w8 = jnp.pad(conv_w, ((0, 0), (0, CONV_ROWS - conv_w.shape[1]), (0, 0)))
    placed.append(place_shard(conv_w8, 1, conv_w.shape[2], F32, place, name="place_conv_w"))
    gathered = all_gather_weights(placed, geom + [(1, conv_w.shape[2])])
    W = {name: g for name, g in zip([t[0] for t in BIG] + ["conv_w"], gathered)}

    small = {k: w[k] for k in ("mix_norm", "pool_scale", "sgu_norm", "w_spatial", "b_spatial", "ffn_norm",
                               "conv_b", "ple_norm", "final_norm")}
    loss_acc, dx, G, small_grads, dg_final = _local_step(
        x.reshape(T, D), p.reshape(2 * T, p.shape[-1]), loss_target.reshape(T, D), W, small)
    loss = lax.psum(loss_acc[0, 0], ("x", "y", "c"))

    glist = [G[name] for (name, _, _, _) in BIG]
    got = swap_layers_with_sibling(glist)
    parts = [pair_sum(g, r, name=f"pair_sum_{name}") for (name, _, _, _), g, r in zip(BIG, glist, got)]
    slots = scatter_to_chips(parts, geom)
    halves = [chip_sum(q, s, axis, size, place, name=f"chip_sum_{name}")
              for (name, _, axis, size), q, s in zip(BIG, parts, slots)]
    full = share_with_sibling(halves)
    grads = {name: g for (name, _, _, _), g in zip(BIG, full)}

    pieces = [small_grads[l][k] for l in range(2) for k in SMALL_ORDER] + [dg_final]
    shapes = [a.shape for a in pieces]
    total = all_reduce_small(_pack_rows(pieces, 16)).reshape(-1)
    summed = _unpack(total, shapes)
    per_layer = {k: jnp.stack([summed[i], summed[len(SMALL_ORDER) + i]]) for i, k in enumerate(SMALL_ORDER)}
    for k in ("mix_norm", "pool_scale", "sgu_norm", "w_spatial", "b_spatial", "ffn_norm", "conv_b", "ple_norm"):
        grads[k] = per_layer[k]
    grads["final_norm"] = summed[-1]
    cw = conv_w.shape[2]
    grads["conv_w"] = lax.dynamic_slice_in_dim(per_layer["conv_w"], chip * cw, cw, axis=2)

    delta, new_m, new_v = {}, {}, {}
    big_names = [name for (name, _, _, _) in BIG]
    for name in big_names:
        shp = w[name].shape
        d_, m_, v_ = elementwise(_adamw, [_view2d(a) for a in (w[name], grads[name], m[name], v[name])],
                                 [F32, F32, F32], name=f"adamw_{name}")
        delta[name], new_m[name], new_v[name] = d_.reshape(shp), m_.reshape(shp), v_.reshape(shp)
    small_names = [k for k in names if k not in big_names]
    small_shapes = [w[k].shape for k in small_names]
    packed = [_pack_rows([src[k] for k in small_names], 8) for src in (w, grads, m, v)]
    outs = elementwise(_adamw, packed, [F32, F32, F32], name="adamw_small")
    for dst, o in zip((delta, new_m, new_v), outs):
        for k, a in zip(small_names, _unpack(o.reshape(-1), small_shapes)):
            dst[k] = a

    return (loss, dx.reshape(1, T, D), *[grads[k] for k in names], *[delta[k] for k in names],
            *[new_m[k] for k in names], *[new_v[k] for k in names])
```

```python
import functools

import jax
import jax.numpy as jnp
from jax import lax
from jax.experimental import pallas as pl
from jax.experimental.pallas import tpu as pltpu

F32 = jnp.float32
BF16 = jnp.bfloat16
EPS = 1e-6
D = 1024
POOL_WINDOWS = (2, 4, 8, 16)
PG = 256
POOL_HALO = 16
CHUNK = 128
HEADS = 8
DFF = 2816
CONV_HALO = 8
CONV_TC = 1408
N_CHIPS = 4
LANES = 128
VMEM_LIMIT = 56 * 1024 * 1024
MESH = pl.DeviceIdType.MESH
ANY = pl.BlockSpec(memory_space=pl.ANY)

ADAM_LR = 0.001
ADAM_B1 = 0.9
ADAM_B2 = 0.999
ADAM_EPS = 1e-08
ADAM_WD = 0.01
ADAM_STEP = 10

BIG = (
    ("w_in", (D, 5 * D), 1, 5 * D // N_CHIPS),
    ("w_pool", (4, PG, PG), 1, PG // N_CHIPS),
    ("w_branch_a", (D, D), 0, D // N_CHIPS),
    ("w_branch_b", (D, D), 0, D // N_CHIPS),
    ("w_out", (D, D), 0, D // N_CHIPS),
    ("w_up", (D, 2 * DFF), 1, 2 * DFF // N_CHIPS),
    ("w_down", (DFF, D), 0, DFF // N_CHIPS),
    ("w_ple_gate", (D, D), 0, D // N_CHIPS),
    ("w_ple", (PG, D), 1, D // N_CHIPS),
)
CONV_ROWS = 8


def _params(n_axes):
    return pltpu.CompilerParams(dimension_semantics=("arbitrary",) * n_axes, vmem_limit_bytes=VMEM_LIMIT)


def _gelu(x):
    return 0.5 * x * (1.0 + lax.erf(x * 0.7071067811865476))


def _gelu_grad(x):
    return 0.5 * (1.0 + lax.erf(x * 0.7071067811865476)) + x * jnp.exp(-0.5 * x * x) * 0.3989422804014327


def _shard_shape(shape, axis, size):
    return tuple(size if a == axis else s for a, s in enumerate(shape))


def _block(ref, axis, j, size):
    idx = tuple(pl.ds(j * size, size) if a == axis else slice(None) for a in range(len(ref.shape)))
    return ref.at[idx]


def mm_nn(a, w, l, *, name, rows, out_dtype=F32, resid=None, a_row_off=0, out=None, out_cols=None,
          out_col_off=0, norm_gain=None, host=None, tm=1024, tn=None, tk=None):
    K, N = w.shape[1], w.shape[2]
    tn = tn or N
    tk = tk or K
    nk = K // tk
    out_cols = out_cols or N
    assert rows % tm == 0 and N % tn == 0 and K % tk == 0 and out_col_off % tn == 0 and a_row_off % tm == 0
    has_resid, has_out, has_norm = resid is not None, out is not None, norm_gain is not None
    assert not has_norm or (tn == N and not has_out)
    grid = (N // tn, rows // tm, nk)
    hosting = _Hosting(host)
    n_in = 2 + has_resid + has_norm + has_out
    n_host_in, n_host_out = len(hosting.arrays), len(hosting.out_shapes)
    n_own_out = 1 + has_norm

    def body(*refs):
        refs = list(refs)
        a_ref, w_ref = refs[0], refs[1]
        r_ref = refs[2] if has_resid else None
        g_ref = refs[2 + has_resid] if has_norm else None
        host_in = refs[n_in:n_in + n_host_in]
        o_base = n_in + n_host_in
        o_ref = refs[o_base]
        host_out = refs[o_base + n_own_out:o_base + n_own_out + n_host_out]
        scratch = refs[o_base + n_own_out + n_host_out:]
        if hosting.plan:
            first, last = _first_last(grid)
            sems = scratch[-2:]

            @pl.when(first)
            def _():
                hosting.begin(host_in, host_out, *sems)

        part = jnp.dot(a_ref[...].astype(BF16), w_ref[...], preferred_element_type=F32)

        def finish(r):
            if has_resid:
                r = r + r_ref[...]
            o_ref[...] = r.astype(o_ref.dtype)
            if has_norm:
                scale = lax.rsqrt(jnp.mean(r * r, axis=-1, keepdims=True) + EPS)
                refs[o_base + 1][...] = (r * scale * g_ref[...]).astype(BF16)

        if nk == 1:
            finish(part)
        else:
            acc = scratch[0]
            k = pl.program_id(2)

            @pl.when(k == 0)
            def _():
                acc[...] = part

            @pl.when(k > 0)
            def _():
                acc[...] += part

            @pl.when(k == nk - 1)
            def _():
                finish(acc[...])

        if hosting.plan:
            @pl.when(last)
            def _():
                hosting.finish(host_in, host_out, *sems)

    in_specs = [pl.BlockSpec((tm, tk), lambda j, i, k: (i + a_row_off // tm, k)),
                pl.BlockSpec((None, tk, tn), lambda j, i, k: (l, k, j))]
    args = [a, w]
    if has_resid:
        in_specs.append(pl.BlockSpec((tm, tn), lambda j, i, k: (i, j)))
        args.append(resid)
    if has_norm:
        in_specs.append(pl.BlockSpec((None, 1, tn), lambda j, i, k: (l, 0, 0)))
        args.append(norm_gain)
    aliases = {}
    if has_out:
        in_specs.append(ANY)
        aliases = {len(args): 0}
        args.append(out)
    aliases.update(hosting.aliases(n_in, n_own_out))
    out_specs = [pl.BlockSpec((tm, tn), lambda j, i, k: (i, j + out_col_off // tn))]
    out_shape = [jax.ShapeDtypeStruct((rows, out_cols), out_dtype)]
    if has_norm:
        out_specs.append(pl.BlockSpec((tm, tn), lambda j, i, k: (i, j)))
        out_shape.append(jax.ShapeDtypeStruct((rows, N), BF16))
    res = pl.pallas_call(
        body, name=name, grid=grid,
        in_specs=in_specs + [ANY] * n_host_in,
        out_specs=out_specs + [ANY] * n_host_out,
        out_shape=out_shape + hosting.out_shapes,
        scratch_shapes=([pltpu.VMEM((tm, tn), F32)] if nk > 1 else []) + hosting.scratch(),
        input_output_aliases=aliases, compiler_params=_params(3))(*args, *hosting.arrays)
    own = res[0] if n_own_out == 1 else tuple(res[:n_own_out])
    return (own, list(res[n_own_out:])) if hosting.plan else own


def mm_nt(a, w, l, *, name, rows, kdim=None, a_col_off=0, out_dtype=BF16, tm=1024, tn=None, tk=None):
    R = w.shape[1]
    kdim = kdim or w.shape[2]
    tn = tn or R
    tk = tk or kdim
    nk = kdim // tk
    assert rows % tm == 0 and R % tn == 0 and kdim % tk == 0 and a_col_off % tk == 0

    def body(a_ref, w_ref, o_ref, *scr):
        part = lax.dot_general(a_ref[...].astype(BF16), w_ref[...], (((1,), (1,)), ((), ())),
                               preferred_element_type=F32)
        if nk == 1:
            o_ref[...] = part.astype(o_ref.dtype)
        else:
            acc = scr[0]
            k = pl.program_id(2)

            @pl.when(k == 0)
            def _():
                acc[...] = part

            @pl.when(k > 0)
            def _():
                acc[...] += part

            @pl.when(k == nk - 1)
            def _():
                o_ref[...] = acc[...].astype(o_ref.dtype)

    if a.ndim == 3:
        per = a.shape[2] // tk
        a_spec = pl.BlockSpec((None, tm, tk), lambda j, i, k: (k // per, i, k % per))
    else:
        a_spec = pl.BlockSpec((tm, tk), lambda j, i, k: (i, k + a_col_off // tk))
    return pl.pallas_call(
        body, name=name, grid=(R // tn, rows // tm, nk),
        in_specs=[a_spec, pl.BlockSpec((None, tn, tk), lambda j, i, k: (l, j, k))],
        out_specs=pl.BlockSpec((tm, tn), lambda j, i, k: (i, j)),
        out_shape=jax.ShapeDtypeStruct((rows, R), out_dtype),
        scratch_shapes=[pltpu.VMEM((tm, tn), F32)] if nk > 1 else [],
        compiler_params=_params(3))(a, w)


def mm_tn(a, b, *, name, rows, ka, nb, a_row_off=0, b_col_off=0, tm=None, tn=None, tk=2048):
    tm = tm or ka
    tn = tn or nb
    tk = min(tk, rows)
    nk = rows // tk
    assert ka % tm == 0 and nb % tn == 0 and rows % tk == 0 and b_col_off % tn == 0 and a_row_off % tk == 0

    def body(a_ref, b_ref, o_ref, acc):
        part = lax.dot_general(a_ref[...].astype(BF16), b_ref[...].astype(BF16), (((0,), (0,)), ((), ())),
                               preferred_element_type=F32)
        k = pl.program_id(2)

        @pl.when(k == 0)
        def _():
            acc[...] = part

        @pl.when(k > 0)
        def _():
            acc[...] += part

        @pl.when(k == nk - 1)
        def _():
            o_ref[...] = acc[...].astype(o_ref.dtype)

    if b.ndim == 3:
        per = b.shape[2] // tn
        b_spec = pl.BlockSpec((None, tk, tn), lambda j, i, k: (j // per, k, j % per))
    else:
        b_spec = pl.BlockSpec((tk, tn), lambda j, i, k: (k, j + b_col_off // tn))
    return pl.pallas_call(
        body, name=name, grid=(nb // tn, ka // tm, nk),
        in_specs=[pl.BlockSpec((tk, tm), lambda j, i, k: (k + a_row_off // tk, i)), b_spec],
        out_specs=pl.BlockSpec((None, tm, tn), lambda j, i, k: (0, i, j)),
        out_shape=jax.ShapeDtypeStruct((1, ka, nb), BF16),
        scratch_shapes=[pltpu.VMEM((tm, tn), F32)], compiler_params=_params(3))(a, b)


def _row_spec(tm, width, col=0):
    return pl.BlockSpec((tm, width), lambda i: (i, col))


def _gain_spec(l, width=D):
    return pl.BlockSpec((None, 1, width), lambda i: (l, 0, 0))


def norm_fwd(x, g3, l, *, name, tm=512):
    T = x.shape[0]

    def body(x_ref, g_ref, o_ref):
        xv = x_ref[...]
        r = lax.rsqrt(jnp.mean(xv * xv, axis=-1, keepdims=True) + EPS)
        o_ref[...] = (xv * r * g_ref[...]).astype(BF16)

    return pl.pallas_call(
        body, name=name, grid=(T // tm,),
        in_specs=[_row_spec(tm, D), _gain_spec(l)], out_specs=_row_spec(tm, D),
        out_shape=jax.ShapeDtypeStruct((T, D), BF16), compiler_params=_params(1))(x, g3)


def _winsum_back(ext, w):
    s, span = ext, 1
    while span < w:
        s = s + pltpu.roll(s, span, 0)
        span *= 2
    return s


def _winsum_fwd(ext, w):
    rows = ext.shape[0]
    s, span = ext, 1
    while span < w:
        s = s + pltpu.roll(s, rows - span, 0)
        span *= 2
    return s


def _pooled(ext, z, t, g, w):
    sl = slice(g * PG, (g + 1) * PG)
    s = _winsum_back(ext[:, sl], w)[POOL_HALO:, :]
    return s / jnp.minimum(t + 1, w).astype(F32) - z[:, sl]


def pool_fwd(z, wpool, scale3, l, *, name, tm=256):
    T = z.shape[0]
    hb = tm // POOL_HALO
    wl = l if wpool.shape[0] > 1 else 0

    def body(z_ref, zp_ref, wp_ref, sc_ref, o_ref):
        i = pl.program_id(0)
        zv = z_ref[...].astype(F32)
        prev = jnp.where(i == 0, 0.0, zp_ref[...].astype(F32))
        ext = jnp.concatenate([prev, zv], axis=0)
        t = i * tm + lax.broadcasted_iota(jnp.int32, (tm, 1), 0)
        for g, w in enumerate(POOL_WINDOWS):
            sl = slice(g * PG, (g + 1) * PG)
            pooled = _pooled(ext, zv, t, g, w)
            q = jnp.dot(pooled.astype(BF16), wp_ref[g], preferred_element_type=F32)
            o_ref[:, sl] = (q * sc_ref[:, sl]).astype(BF16)

    return pl.pallas_call(
        body, name=name, grid=(T // tm,),
        in_specs=[_row_spec(tm, D),
                  pl.BlockSpec((POOL_HALO, D), lambda i: (jnp.maximum(i * hb - 1, 0), 0)),
                  pl.BlockSpec((None, 4, PG, PG), lambda i: (wl, 0, 0, 0)),
                  _gain_spec(l)],
        out_specs=_row_spec(tm, D),
        out_shape=jax.ShapeDtypeStruct((T, D), BF16), compiler_params=_params(1))(z, z, wpool, scale3)


def sgu_fwd(z, g3, wsm, bT, l, *, name, tm=256):
    T = z.shape[0]

    def body(zu_ref, zv_ref, g_ref, ws_ref, b_ref, o_ref):
        gu = _gelu(zu_ref[...].astype(F32))
        gv = _gelu(zv_ref[...].astype(F32))
        rv = lax.rsqrt(jnp.mean(gv * gv, axis=-1, keepdims=True) + EPS)
        vn = (gv * rv * g_ref[...]).astype(BF16)
        for n in range(tm // CHUNK):
            r = slice(n * CHUNK, (n + 1) * CHUNK)
            for h in range(HEADS):
                cs = slice(h * CHUNK, (h + 1) * CHUNK)
                mixed = jnp.dot(ws_ref[h], vn[r, cs], preferred_element_type=F32) + b_ref[:, h:h + 1]
                o_ref[r, cs] = (gu[r, cs] * mixed).astype(BF16)

    return pl.pallas_call(
        body, name=name, grid=(T // tm,),
        in_specs=[_row_spec(tm, D, 1), _row_spec(tm, D, 2), _gain_spec(l),
                  pl.BlockSpec((None, HEADS, CHUNK, CHUNK), lambda i: (l, 0, 0, 0)),
                  pl.BlockSpec((None, CHUNK, HEADS), lambda i: (l, 0, 0))],
        out_specs=_row_spec(tm, D),
        out_shape=jax.ShapeDtypeStruct((T, D), BF16), compiler_params=_params(1))(z, z, g3, wsm, bT)


def gate_fwd(z, yab, *, name, tm=512):
    T = z.shape[0]

    def body(za_ref, zb_ref, y_ref, o_ref):
        ga = jax.nn.sigmoid(za_ref[...].astype(F32))
        gb = jax.nn.sigmoid(zb_ref[...].astype(F32))
        o_ref[...] = (ga * y_ref[:, :D].astype(F32) + gb * y_ref[:, D:].astype(F32)).astype(BF16)

    return pl.pallas_call(
        body, name=name, grid=(T // tm,),
        in_specs=[_row_spec(tm, D, 3), _row_spec(tm, D, 4), _row_spec(tm, 2 * D)],
        out_specs=_row_spec(tm, D),
        out_shape=jax.ShapeDtypeStruct((T, D), BF16), compiler_params=_params(1))(z, z, yab)


def _conv(ext, w_ref, b_ref):
    c = b_ref[...] + w_ref[0:1, :] * pltpu.roll(ext, 2, 0)
    c = c + w_ref[1:2, :] * pltpu.roll(ext, 1, 0)
    return c + w_ref[2:3, :] * ext


def conv_fwd(up, convw, convb3, l, *, name, tm=256):
    T = up.shape[0]
    tc = CONV_TC
    nc = DFF // tc
    hb = tm // CONV_HALO

    def body(ua_ref, uap_ref, ub_ref, ubp_ref, wa_ref, wb_ref, ba_ref, bb_ref, o_ref):
        i = pl.program_id(1)

        def conv_of(u_ref, p_ref, w_ref, b_ref):
            ext = jnp.concatenate([jnp.where(i == 0, 0.0, p_ref[...]), u_ref[...]], axis=0)
            return _conv(ext, w_ref, b_ref)[CONV_HALO:, :]

        ca = conv_of(ua_ref, uap_ref, wa_ref, ba_ref)
        cb = conv_of(ub_ref, ubp_ref, wb_ref, bb_ref)
        o_ref[...] = (_gelu(ca) * cb).astype(BF16)

    def cur(off):
        return pl.BlockSpec((tm, tc), lambda j, i: (i, j + off))

    def prev(off):
        return pl.BlockSpec((CONV_HALO, tc), lambda j, i: (jnp.maximum(i * hb - 1, 0), j + off))

    def wspec(off):
        return pl.BlockSpec((None, CONV_ROWS, tc), lambda j, i: (l, 0, j + off))

    def bspec(off):
        return pl.BlockSpec((None, 1, tc), lambda j, i: (l, 0, j + off))

    return pl.pallas_call(
        body, name=name, grid=(nc, T // tm),
        in_specs=[cur(0), prev(0), cur(nc), prev(nc), wspec(0), wspec(nc), bspec(0), bspec(nc)],
        out_specs=pl.BlockSpec((tm, tc), lambda j, i: (i, j)),
        out_shape=jax.ShapeDtypeStruct((T, DFF), BF16),
        compiler_params=_params(2))(up, up, up, up, convw, convw, convb3, convb3)


def ple_fwd(x2, pg, e, g3, l, *, name, tm=512):
    T = x2.shape[0]
    has_norm = g3 is not None

    def body(x_ref, pg_ref, e_ref, *rest):
        xv = x_ref[...] + jax.nn.sigmoid(pg_ref[...].astype(F32)) * e_ref[...].astype(F32)
        if has_norm:
            g_ref, o_ref, h_ref = rest
            r = lax.rsqrt(jnp.mean(xv * xv, axis=-1, keepdims=True) + EPS)
            h_ref[...] = (xv * r * g_ref[...]).astype(BF16)
        else:
            o_ref, = rest
        o_ref[...] = xv

    x_shape = jax.ShapeDtypeStruct((T, D), F32)
    return pl.pallas_call(
        body, name=name, grid=(T // tm,),
        in_specs=[_row_spec(tm, D)] * 3 + ([_gain_spec(l)] if has_norm else []),
        out_specs=[_row_spec(tm, D)] * 2 if has_norm else _row_spec(tm, D),
        out_shape=[x_shape, jax.ShapeDtypeStruct((T, D), BF16)] if has_norm else x_shape,
        compiler_params=_params(1))(x2, pg, e, *([g3] if has_norm else []))


def loss_head(x, g3, tgt, *, name, tm=512):
    T = x.shape[0]

    def body(x_ref, g_ref, t_ref, loss_ref, dx_ref, dg_ref):
        @pl.when(pl.program_id(0) == 0)
        def _():
            loss_ref[...] = jnp.zeros_like(loss_ref)
            dg_ref[...] = jnp.zeros_like(dg_ref)

        xv, g = x_ref[...], g_ref[...]
        r = lax.rsqrt(jnp.mean(xv * xv, axis=-1, keepdims=True) + EPS)
        xh = xv * r
        err = xh * g - t_ref[...]
        loss_ref[...] += 0.5 * jnp.sum(jnp.mean(err * err, axis=-1, keepdims=True))
        dy = err * (1.0 / D)
        dyg = dy * g
        dx_ref[...] = r * (dyg - xh * jnp.mean(dyg * xh, axis=-1, keepdims=True))
        dg_ref[0:1, :] += jnp.sum(dy * xh, axis=0, keepdims=True)

    return pl.pallas_call(
        body, name=name, grid=(T // tm,),
        in_specs=[_row_spec(tm, D), pl.BlockSpec((1, D), lambda i: (0, 0)), _row_spec(tm, D)],
        out_specs=[pl.BlockSpec((8, LANES), lambda i: (0, 0)), _row_spec(tm, D),
                   pl.BlockSpec((8, D), lambda i: (0, 0))],
        out_shape=[jax.ShapeDtypeStruct((8, LANES), F32), jax.ShapeDtypeStruct((T, D), F32),
                   jax.ShapeDtypeStruct((8, D), F32)],
        compiler_params=_params(1))(x, g3, tgt)


def norm_bwd(dh, x, g3, l, dx_in, *, name, tm=512):
    T = x.shape[0]

    def body(dh_ref, x_ref, g_ref, dxi_ref, dx_ref, dg_ref):
        @pl.when(pl.program_id(0) == 0)
        def _():
            dg_ref[...] = jnp.zeros_like(dg_ref)

        xv, dh_v = x_ref[...], dh_ref[...].astype(F32)
        r = lax.rsqrt(jnp.mean(xv * xv, axis=-1, keepdims=True) + EPS)
        xh = xv * r
        dhg = dh_v * g_ref[...]
        dx_ref[...] = dxi_ref[...] + r * (dhg - xh * jnp.mean(dhg * xh, axis=-1, keepdims=True))
        dg_ref[0:1, :] += jnp.sum(dh_v * xh, axis=0, keepdims=True)

    return pl.pallas_call(
        body, name=name, grid=(T // tm,),
        in_specs=[_row_spec(tm, D), _row_spec(tm, D), _gain_spec(l), _row_spec(tm, D)],
        out_specs=[_row_spec(tm, D), pl.BlockSpec((8, D), lambda i: (0, 0))],
        out_shape=[jax.ShapeDtypeStruct((T, D), F32), jax.ShapeDtypeStruct((8, D), F32)],
        compiler_params=_params(1))(dh, x, g3, dx_in)


def ple_bwd(dx, pg, e, *, name, tm=512):
    T = dx.shape[0]

    def body(dx_ref, pg_ref, e_ref, de_ref, dpg_ref):
        gate = jax.nn.sigmoid(pg_ref[...].astype(F32))
        dxv = dx_ref[...]
        de_ref[...] = (dxv * gate).astype(BF16)
        dpg_ref[...] = (dxv * e_ref[...].astype(F32) * gate * (1.0 - gate)).astype(BF16)

    return pl.pallas_call(
        body, name=name, grid=(T // tm,),
        in_specs=[_row_spec(tm, D)] * 3, out_specs=[_row_spec(tm, D)] * 2,
        out_shape=[jax.ShapeDtypeStruct((T, D), BF16)] * 2, compiler_params=_params(1))(dx, pg, e)


def conv_bwd(df, up, convw, convb3, l, *, name, host=None, tm=256):
    T = up.shape[0]
    tc = CONV_TC
    nc = DFF // tc
    hb = tm // CONV_HALO
    nt = T // tm
    rows = tm + 2 * CONV_HALO
    own = slice(CONV_HALO, CONV_HALO + tm)

    hosting = _Hosting(host)
    n_host_in, n_host_out = len(hosting.arrays), len(hosting.out_shapes)

    def body(df_ref, dfn_ref, ua_ref, uap_ref, uan_ref, ub_ref, ubp_ref, ubn_ref, wa_ref, wb_ref, ba_ref, bb_ref,
             *rest):
        host_in = rest[:n_host_in]
        dup_ref, dcw_ref, dcb_ref = rest[n_host_in:n_host_in + 3]
        host_out = rest[n_host_in + 3:n_host_in + 3 + n_host_out]
        sems = rest[n_host_in + 3 + n_host_out:]
        i = pl.program_id(1)
        if hosting.plan:
            first, last = _first_last((nc, nt))

            @pl.when(first)
            def _():
                hosting.begin(host_in, host_out, *sems)

        @pl.when(i == 0)
        def _():
            dcw_ref[...] = jnp.zeros_like(dcw_ref)
            dcb_ref[...] = jnp.zeros_like(dcb_ref)

        def ext_of(c_ref, p_ref, n_ref):
            return jnp.concatenate([jnp.where(i == 0, 0.0, p_ref[...]), c_ref[...],
                                    jnp.where(i == nt - 1, 0.0, n_ref[...])], axis=0)

        ea = ext_of(ua_ref, uap_ref, uan_ref)
        eb = ext_of(ub_ref, ubp_ref, ubn_ref)
        ca = _conv(ea, wa_ref, ba_ref)
        cb = _conv(eb, wb_ref, bb_ref)
        df_ext = jnp.concatenate([jnp.zeros((CONV_HALO, tc), F32), df_ref[...],
                                  jnp.where(i == nt - 1, 0.0, dfn_ref[...])], axis=0)
        cdf = 0.5 * (1.0 + lax.erf(ca * 0.7071067811865476))
        da = df_ext * cb * (cdf + ca * jnp.exp(-0.5 * ca * ca) * 0.3989422804014327)
        db = df_ext * (ca * cdf)

        def finish(h, dc, e, w_ref):
            dup = w_ref[2:3, :] * dc + w_ref[1:2, :] * pltpu.roll(dc, rows - 1, 0)
            dup = dup + w_ref[0:1, :] * pltpu.roll(dc, rows - 2, 0)
            dup_ref[h] = dup[own, :].astype(BF16)
            dco = dc[own, :]
            dcb_ref[h, 0:1, :] += jnp.sum(dco, axis=0, keepdims=True)
            dcw_ref[h, 0:1, :] += jnp.sum(dco * pltpu.roll(e, 2, 0)[own, :], axis=0, keepdims=True)
            dcw_ref[h, 1:2, :] += jnp.sum(dco * pltpu.roll(e, 1, 0)[own, :], axis=0, keepdims=True)
            dcw_ref[h, 2:3, :] += jnp.sum(dco * e[own, :], axis=0, keepdims=True)

        finish(0, da, ea, wa_ref)
        finish(1, db, eb, wb_ref)
        if hosting.plan:
            @pl.when(last)
            def _():
                hosting.finish(host_in, host_out, *sems)

    def nxt(i):
        return jnp.minimum((i + 1) * hb, T // CONV_HALO - 1)

    def prv(i):
        return jnp.maximum(i * hb - 1, 0)

    def up_specs(off):
        return [pl.BlockSpec((tm, tc), lambda j, i: (i, j + off)),
                pl.BlockSpec((CONV_HALO, tc), lambda j, i: (prv(i), j + off)),
                pl.BlockSpec((CONV_HALO, tc), lambda j, i: (nxt(i), j + off))]

    in_specs = [pl.BlockSpec((tm, tc), lambda j, i: (i, j)),
                pl.BlockSpec((CONV_HALO, tc), lambda j, i: (nxt(i), j)),
                *up_specs(0), *up_specs(nc),
                pl.BlockSpec((None, CONV_ROWS, tc), lambda j, i: (l, 0, j)),
                pl.BlockSpec((None, CONV_ROWS, tc), lambda j, i: (l, 0, j + nc)),
                pl.BlockSpec((None, 1, tc), lambda j, i: (l, 0, j)),
                pl.BlockSpec((None, 1, tc), lambda j, i: (l, 0, j + nc))]
    res = pl.pallas_call(
        body, name=name, grid=(nc, nt), in_specs=in_specs + [ANY] * n_host_in,
        out_specs=[pl.BlockSpec((2, tm, tc), lambda j, i: (0, i, j)),
                   pl.BlockSpec((2, 8, tc), lambda j, i: (0, 0, j)),
                   pl.BlockSpec((2, 8, tc), lambda j, i: (0, 0, j))] + [ANY] * n_host_out,
        out_shape=[jax.ShapeDtypeStruct((2, T, DFF), BF16), jax.ShapeDtypeStruct((2, 8, DFF), F32),
                   jax.ShapeDtypeStruct((2, 8, DFF), F32)] + hosting.out_shapes,
        scratch_shapes=hosting.scratch(), input_output_aliases=hosting.aliases(12, 3),
        compiler_params=_params(2))(df, df, up, up, up, up, up, up, convw, convw, convb3, convb3, *hosting.arrays)
    return res[0], res[1], res[2], list(res[3:])


def gate_bwd(dmo, z, yab, *, name, tm=512):
    T = z.shape[0]

    def body(dmo_ref, zg_ref, y_ref, dz_ref, dy_ref):
        g = jax.nn.sigmoid(zg_ref[...].astype(F32))
        dmo_v = dmo_ref[...].astype(F32)
        dy_ref[...] = (dmo_v * g).astype(BF16)
        dz_ref[...] = (dmo_v * y_ref[...].astype(F32) * g * (1.0 - g)).astype(BF16)

    return pl.pallas_call(
        body, name=name, grid=(T // tm, 2),
        in_specs=[pl.BlockSpec((tm, D), lambda i, s: (i, 0)),
                  pl.BlockSpec((tm, D), lambda i, s: (i, 3 + s)),
                  pl.BlockSpec((tm, D), lambda i, s: (i, s))],
        out_specs=[pl.BlockSpec((tm, D), lambda i, s: (i, 3 + s)),
                   pl.BlockSpec((tm, D), lambda i, s: (i, s))],
        out_shape=[jax.ShapeDtypeStruct((T, 5 * D), BF16), jax.ShapeDtypeStruct((T, 2 * D), BF16)],
        compiler_params=_params(2))(dmo, z, yab)


def mixer_bwd(da, ds, z, dz, wpool, scale3, g3, wsm, wsmT, bT, l, *, name, tm=256):
    T = z.shape[0]
    hb = tm // POOL_HALO
    nt = T // tm

    def body(da_ref, dan_ref, ds_ref, zp_ref, zpp_ref, zu_ref, zv_ref, wp_ref, sc_ref, g_ref, ws_ref, wst_ref,
             b_ref, dzin_ref, dz_ref, dwp_ref, dsc_ref, dws_ref, dbt_ref, dgs_ref, mixed_scr, dvn_scr, db_scr):
        del dzin_ref
        i = pl.program_id(0)

        @pl.when(i == 0)
        def _():
            dwp_ref[...] = jnp.zeros_like(dwp_ref)
            dsc_ref[...] = jnp.zeros_like(dsc_ref)
            dws_ref[...] = jnp.zeros_like(dws_ref)
            dgs_ref[...] = jnp.zeros_like(dgs_ref)
            db_scr[...] = jnp.zeros_like(db_scr)

        zv_p = zp_ref[...].astype(F32)
        ext = jnp.concatenate([jnp.where(i == 0, 0.0, zpp_ref[...].astype(F32)), zv_p], axis=0)
        da_v = da_ref[...].astype(F32)
        da_ext = jnp.concatenate([da_v, jnp.where(i == nt - 1, 0.0, dan_ref[...].astype(F32))], axis=0)
        t = i * tm + lax.broadcasted_iota(jnp.int32, (tm, 1), 0)
        t_ext = i * tm + lax.broadcasted_iota(jnp.int32, (tm + POOL_HALO, 1), 0)
        for g, w in enumerate(POOL_WINDOWS):
            sl = slice(g * PG, (g + 1) * PG)
            pooled = _pooled(ext, zv_p, t, g, w).astype(BF16)
            q = jnp.dot(pooled, wp_ref[g], preferred_element_type=F32)
            dsc_ref[0:1, sl] += jnp.sum(da_v[:, sl] * q, axis=0, keepdims=True)
            dq_ext = (da_ext[:, sl] * sc_ref[:, sl]).astype(BF16)
            dwp_ref[g] += lax.dot_general(pooled, dq_ext[:tm, :], (((0,), (0,)), ((), ())),
                                          preferred_element_type=F32)
            dpool = lax.dot_general(dq_ext, wp_ref[g], (((1,), (1,)), ((), ())), preferred_element_type=F32)
            spread = _winsum_fwd(dpool / jnp.minimum(t_ext + 1, w).astype(F32), w)
            dz_ref[:, sl] = (spread[:tm, :] - dpool[:tm, :]).astype(BF16)

        zu, zv, ds_v = zu_ref[...].astype(F32), zv_ref[...].astype(F32), ds_ref[...].astype(F32)
        gain = g_ref[...]
        gu, gv = _gelu(zu), _gelu(zv)
        rv = lax.rsqrt(jnp.mean(gv * gv, axis=-1, keepdims=True) + EPS)
        vh = gv * rv
        vn = (vh * gain).astype(BF16)
        dmix = ds_v * gu
        dmix_b = dmix.astype(BF16)
        for n in range(tm // CHUNK):
            r = slice(n * CHUNK, (n + 1) * CHUNK)
            db_scr[...] += dmix[r, :]
            for h in range(HEADS):
                cs = slice(h * CHUNK, (h + 1) * CHUNK)
                mixed_scr[r, cs] = jnp.dot(ws_ref[h], vn[r, cs], preferred_element_type=F32) + b_ref[:, h:h + 1]
                dws_ref[h] += lax.dot_general(dmix_b[r, cs], vn[r, cs], (((1,), (1,)), ((), ())),
                                              preferred_element_type=F32)
                dvn_scr[r, cs] = jnp.dot(wst_ref[h], dmix_b[r, cs], preferred_element_type=F32)
        dz_ref[:, D:2 * D] = (ds_v * mixed_scr[...] * _gelu_grad(zu)).astype(BF16)
        dvn = dvn_scr[...]
        dgs_ref[0:1, :] += jnp.sum(dvn * vh, axis=0, keepdims=True)
        dvg = dvn * gain
        dgv = rv * (dvg - vh * jnp.mean(dvg * vh, axis=-1, keepdims=True))
        dz_ref[:, 2 * D:3 * D] = (dgv * _gelu_grad(zv)).astype(BF16)

        @pl.when(i == nt - 1)
        def _():
            tril = (lax.broadcasted_iota(jnp.int32, (CHUNK, CHUNK), 0)
                    >= lax.broadcasted_iota(jnp.int32, (CHUNK, CHUNK), 1)).astype(F32)
            for h in range(HEADS):
                dws_ref[h] = dws_ref[h] * tril
                dbt_ref[:, h:h + 1] = jnp.sum(db_scr[:, h * CHUNK:(h + 1) * CHUNK], axis=1, keepdims=True)

    const4 = lambda i: (l, 0, 0, 0)
    wl = l if wpool.shape[0] > 1 else 0
    in_specs = [
        _row_spec(tm, D),
        pl.BlockSpec((POOL_HALO, D), lambda i: (jnp.minimum((i + 1) * hb, T // POOL_HALO - 1), 0)),
        _row_spec(tm, D),
        _row_spec(tm, D, 0),
        pl.BlockSpec((POOL_HALO, D), lambda i: (jnp.maximum(i * hb - 1, 0), 0)),
        _row_spec(tm, D, 1), _row_spec(tm, D, 2),
        pl.BlockSpec((None, 4, PG, PG), lambda i: (wl, 0, 0, 0)),
        _gain_spec(l), _gain_spec(l),
        pl.BlockSpec((None, HEADS, CHUNK, CHUNK), const4),
        pl.BlockSpec((None, HEADS, CHUNK, CHUNK), const4),
        pl.BlockSpec((None, CHUNK, HEADS), lambda i: (l, 0, 0)),
        ANY,
    ]
    out_specs = [
        pl.BlockSpec((tm, 3 * D), lambda i: (i, 0)),
        pl.BlockSpec((4, PG, PG), lambda i: (0, 0, 0)),
        pl.BlockSpec((8, D), lambda i: (0, 0)),
        pl.BlockSpec((HEADS, CHUNK, CHUNK), lambda i: (0, 0, 0)),
        pl.BlockSpec((CHUNK, HEADS), lambda i: (0, 0)),
        pl.BlockSpec((8, D), lambda i: (0, 0)),
    ]
    out_shape = [
        jax.ShapeDtypeStruct((T, 5 * D), BF16), jax.ShapeDtypeStruct((4, PG, PG), F32),
        jax.ShapeDtypeStruct((8, D), F32), jax.ShapeDtypeStruct((HEADS, CHUNK, CHUNK), F32),
        jax.ShapeDtypeStruct((CHUNK, HEADS), F32), jax.ShapeDtypeStruct((8, D), F32),
    ]
    return pl.pallas_call(
        body, name=name, grid=(nt,), in_specs=in_specs, out_specs=out_specs, out_shape=out_shape,
        scratch_shapes=[pltpu.VMEM((tm, D), F32), pltpu.VMEM((tm, D), F32), pltpu.VMEM((CHUNK, D), F32)],
        input_output_aliases={13: 0}, compiler_params=_params(1))(
            da, da, ds, z, z, z, z, wpool, scale3, g3, wsm, wsmT, bT, dz)


def _row_tile(rows, cols, sub):
    cap = max(sub, (2 * 1024 * 1024) // (4 * cols))
    best = None
    for tr in range(sub, min(rows, cap) + 1, sub):
        if rows % tr == 0:
            best = tr
    return best or rows


def elementwise(fn, ins, out_dtypes, *, name, row_blk_offs=None, rows=None):
    cols = ins[0].shape[1]
    rows = rows or ins[0].shape[0]
    tr = _row_tile(rows, cols, 16)
    offs = row_blk_offs or [0] * len(ins)
    n_in = len(ins)

    def body(*refs):
        outs = fn(*[r[...] for r in refs[:n_in]])
        for o_ref, o in zip(refs[n_in:], outs):
            o_ref[...] = o.astype(o_ref.dtype)

    return pl.pallas_call(
        body, name=name, grid=(rows // tr,),
        in_specs=[pl.BlockSpec((tr, cols), functools.partial(lambda i, o: (i + o * (rows // tr), 0), o=o))
                  for o in offs],
        out_specs=[pl.BlockSpec((tr, cols), lambda i: (i, 0)) for _ in out_dtypes],
        out_shape=[jax.ShapeDtypeStruct((rows, cols), dt) for dt in out_dtypes],
        compiler_params=_params(1))(*ins)


def _adamw(w, g, m, v):
    m = ADAM_B1 * m + (1.0 - ADAM_B1) * g
    v = ADAM_B2 * v + (1.0 - ADAM_B2) * jnp.square(g)
    m_hat = m / (1.0 - ADAM_B1 ** ADAM_STEP)
    v_hat = v / (1.0 - ADAM_B2 ** ADAM_STEP)
    delta = -ADAM_LR * (m_hat / (jnp.sqrt(v_hat) + ADAM_EPS) + ADAM_WD * w)
    return delta, m, v


def _view2d(a):
    return a.reshape(-1, a.shape[-1])


def _place():
    x, y, c = lax.axis_index("x"), lax.axis_index("y"), lax.axis_index("c")
    others = [(1 - x, y), (x, 1 - y), (1 - x, 1 - y)]
    return x, y, c, 2 * x + y, others


def _remote(src, dst, send_sems, recv_sems, k, to):
    return pltpu.make_async_remote_copy(src_ref=src, dst_ref=dst, send_sem=send_sems.at[k], recv_sem=recv_sems.at[k],
                                        device_id=to, device_id_type=MESH)


def _half(ref, axis, j, size, h):
    if len(ref.shape) == 3:
        return ref.at[:, pl.ds(j * size + h * (size // 2), size // 2), :]
    if axis == 0:
        return ref.at[pl.ds(j * size + h * (size // 2), size // 2), :]
    rows = ref.shape[0] // 2
    return ref.at[pl.ds(h * rows, rows), pl.ds(j * size, size)]


def _half_shard_shape(shape, axis, size):
    if len(shape) == 3:
        return (shape[0], size // 2, shape[2])
    if axis == 0:
        return (size // 2, shape[1])
    return (shape[0] // 2, size)


class Exchange:
    def __init__(self, arrays, out_shapes, aliases, n_sems, begin, finish):
        self.arrays, self.out_shapes, self.aliases, self.n_sems = list(arrays), list(out_shapes), aliases, n_sems
        self.begin, self.finish = begin, finish


class _Hosting:
    def __init__(self, plan):
        self.plan = list(plan or [])
        self.arrays = [a for ex in self.plan for a in ex.arrays]
        self.out_shapes = [o for ex in self.plan for o in ex.out_shapes]
        self.n_sems = sum(ex.n_sems for ex in self.plan)

    def scratch(self):
        return [pltpu.SemaphoreType.DMA((self.n_sems,)), pltpu.SemaphoreType.DMA((self.n_sems,))] if self.plan else []

    def aliases(self, in_base, out_base):
        out, i0, o0 = {}, in_base, out_base
        for ex in self.plan:
            out.update({i0 + i: o0 + o for i, o in ex.aliases.items()})
            i0, o0 = i0 + len(ex.arrays), o0 + len(ex.out_shapes)
        return out

    def _each(self, in_refs, out_refs):
        i0 = o0 = s0 = 0
        for ex in self.plan:
            yield ex, in_refs[i0:i0 + len(ex.arrays)], out_refs[o0:o0 + len(ex.out_shapes)], s0
            i0, o0, s0 = i0 + len(ex.arrays), o0 + len(ex.out_shapes), s0 + ex.n_sems

    def begin(self, in_refs, out_refs, send_sems, recv_sems):
        for ex, ins, outs, s0 in self._each(in_refs, out_refs):
            ex.begin(ins, outs, send_sems, recv_sems, s0)

    def finish(self, in_refs, out_refs, send_sems, recv_sems):
        for ex, ins, outs, s0 in self._each(in_refs, out_refs):
            ex.finish(ins, outs, send_sems, recv_sems, s0)


def _first_last(grid):
    ids = [pl.program_id(a) for a in range(len(grid))]
    first = functools.reduce(jnp.logical_and, [i == 0 for i in ids])
    last = functools.reduce(jnp.logical_and, [i == g - 1 for i, g in zip(ids, grid)])
    return first, last


def run_exchanges(plan, *, name):
    host = _Hosting(plan)
    n_in, n_out = len(host.arrays), len(host.out_shapes)

    def body(*refs):
        ins, outs = refs[:n_in], refs[n_in:n_in + n_out]
        send_sems, recv_sems = refs[n_in + n_out:]
        host.begin(ins, outs, send_sems, recv_sems)
        host.finish(ins, outs, send_sems, recv_sems)

    return pl.pallas_call(
        body, name=name, in_specs=[ANY] * n_in, out_specs=[ANY] * n_out, out_shape=host.out_shapes,
        scratch_shapes=host.scratch(), input_output_aliases=host.aliases(0, 0),
        compiler_params=pltpu.CompilerParams(has_side_effects=True))(*host.arrays)


def place_shard(src, l, axis, size, out_dtype, place, *, name):
    shard = src.shape[1:]
    natural = tuple(size * N_CHIPS if a == axis else s for a, s in enumerate(shard))
    if len(shard) == 3:
        blk = (None,) + shard
        grid = (1,)
        in_map = lambda i, pr: (l, 0, 0, 0)
        out_map = lambda i, pr: (0, 0, pr[0], 0)
    else:
        tr = _row_tile(shard[0], shard[1], 16)
        steps = shard[0] // tr
        blk = (None, tr, shard[1])
        grid = (steps,)
        in_map = lambda i, pr: (l, i, 0)
        if axis == 0:
            out_map = lambda i, pr: (0, pr[0] * steps + i, 0)
        else:
            out_map = lambda i, pr: (0, i, pr[0])

    def body(pr_ref, s_ref, o_ref):
        del pr_ref
        o_ref[...] = s_ref[...].astype(o_ref.dtype)

    return pl.pallas_call(
        body, name=name,
        grid_spec=pltpu.PrefetchScalarGridSpec(
            num_scalar_prefetch=1, grid=grid, in_specs=[pl.BlockSpec(blk, in_map)],
            out_specs=pl.BlockSpec(blk, out_map)),
        out_shape=jax.ShapeDtypeStruct((1,) + natural, out_dtype), compiler_params=_params(1))(place, src)


def place_both_layers(src, axis, size, place, *, name):
    rows, cols = src.shape[1], src.shape[2]

    def body(pr_ref, s_ref, o_ref):
        del pr_ref
        o_ref[...] = s_ref[...]

    return pl.pallas_call(
        body, name=name,
        grid_spec=pltpu.PrefetchScalarGridSpec(
            num_scalar_prefetch=1, grid=(2,), in_specs=[pl.BlockSpec((None, rows, cols), lambda lyr, pr: (lyr, 0, 0))],
            out_specs=pl.BlockSpec((None, rows, cols), lambda lyr, pr: (lyr, 0, pr[0]))),
        out_shape=jax.ShapeDtypeStruct((2, rows, cols * N_CHIPS), src.dtype), compiler_params=_params(1))(place, src)


def gather_exchange(arrays, geom):
    n = len(arrays)

    def begin(ins, outs, send_sems, recv_sems, s0):
        x, y, c, j, others = _place()
        for t, (axis, size) in enumerate(geom):
            mine = _half(outs[t].at[0], axis, j, size, c)
            for k, (ox, oy) in enumerate(others):
                _remote(mine, mine, send_sems, recv_sems, s0 + 6 * t + k, (ox, oy, c)).start()

    def finish(ins, outs, send_sems, recv_sems, s0):
        x, y, c, j, others = _place()
        sib = (x, y, 1 - c)
        passed = []
        for t, (axis, size) in enumerate(geom):
            for k, (ox, oy) in enumerate(others):
                landed = _half(outs[t].at[0], axis, 2 * ox + oy, size, c)
                _remote(landed, landed, send_sems, recv_sems, s0 + 6 * t + k, (ox, oy, c)).wait_recv()
                fwd = _remote(landed, landed, send_sems, recv_sems, s0 + 6 * t + 3 + k, sib)
                fwd.start()
                passed.append(fwd)
        for t, (axis, size) in enumerate(geom):
            for k, (ox, oy) in enumerate(others):
                got = _half(outs[t].at[0], axis, 2 * ox + oy, size, 1 - c)
                _remote(got, got, send_sems, recv_sems, s0 + 6 * t + 3 + k, sib).wait_recv()
        for fwd in passed:
            fwd.wait_send()
        for t, (axis, size) in enumerate(geom):
            mine = _half(outs[t].at[0], axis, j, size, c)
            for k, (ox, oy) in enumerate(others):
                _remote(mine, mine, send_sems, recv_sems, s0 + 6 * t + k, (ox, oy, c)).wait_send()

    return Exchange(arrays, [jax.ShapeDtypeStruct(a.shape, a.dtype) for a in arrays], {t: t for t in range(n)},
                    6 * n, begin, finish)


def gather_by_layer_exchange(array, axis, size):
    def blocks(out, others, lyr):
        return [_block(out.at[lyr], axis, 2 * ox + oy, size) for (ox, oy) in others]

    def begin(ins, outs, send_sems, recv_sems, s0):
        x, y, c, j, others = _place()
        mine = _block(outs[0].at[c], axis, j, size)
        for k, (ox, oy) in enumerate(others):
            _remote(mine, mine, send_sems, recv_sems, s0 + k, (ox, oy, c)).start()

    def finish(ins, outs, send_sems, recv_sems, s0):
        x, y, c, j, others = _place()
        sib = (x, y, 1 - c)
        passed = []
        for k, ((ox, oy), landed) in enumerate(zip(others, blocks(outs[0], others, c))):
            _remote(landed, landed, send_sems, recv_sems, s0 + k, (ox, oy, c)).wait_recv()
            fwd = _remote(landed, landed, send_sems, recv_sems, s0 + 3 + k, sib)
            fwd.start()
            passed.append(fwd)
        for k, got in enumerate(blocks(outs[0], others, 1 - c)):
            _remote(got, got, send_sems, recv_sems, s0 + 3 + k, sib).wait_recv()
        for fwd in passed:
            fwd.wait_send()
        mine = _block(outs[0].at[c], axis, j, size)
        for k, (ox, oy) in enumerate(others):
            _remote(mine, mine, send_sems, recv_sems, s0 + k, (ox, oy, c)).wait_send()

    return Exchange([array], [jax.ShapeDtypeStruct(array.shape, array.dtype)], {0: 0}, 6, begin, finish)


def swap_exchange(grads, geom):
    def pieces(t, g, dst, h):
        axis, size = geom[t]
        if len(g.shape) == 2 and axis == 1:
            rows = g.shape[0] // 2
            return [(g.at[pl.ds(h * rows, rows), :], dst)]
        return [(_half(g, axis, jb, size, h), dst.at[jb]) for jb in range(N_CHIPS)]

    counts = [1 if (len(g.shape) == 3 and a == 1) else N_CHIPS for g, (a, _) in zip(grads, geom)]
    bases = [sum(counts[:t]) for t in range(len(grads))]

    def copies(ins, outs, send_sems, recv_sems, s0):
        x, y, c, _, _ = _place()
        cps = []
        for t in range(len(grads)):
            for q, (src, dst) in enumerate(pieces(t, ins[t].at[0], outs[t], 1 - c)):
                cps.append(_remote(src, dst, send_sems, recv_sems, s0 + bases[t] + q, (x, y, 1 - c)))
        return cps

    def begin(*a):
        for cp in copies(*a):
            cp.start()

    def finish(*a):
        for cp in copies(*a):
            cp.wait()

    out_shapes = []
    for g, (axis, size) in zip(grads, geom):
        shp = g.shape[1:]
        if len(shp) == 2 and axis == 1:
            out_shapes.append(jax.ShapeDtypeStruct((shp[0] // 2, shp[1]), g.dtype))
        else:
            out_shapes.append(jax.ShapeDtypeStruct((N_CHIPS,) + _half_shard_shape(shp, axis, size), g.dtype))
    return Exchange(grads, out_shapes, {}, sum(counts), begin, finish)


def scatter_exchange(parts, geom, shapes):
    def copies(ins, outs, send_sems, recv_sems, s0):
        x, y, c, j, others = _place()
        cps = []
        for t, ((axis, size), shp) in enumerate(zip(geom, shapes)):
            for k, (ox, oy) in enumerate(others):
                jp = 2 * ox + oy
                src = ins[t].at[:, pl.ds(jp * size, size)] if (len(shp) == 2 and axis == 1) else ins[t].at[jp]
                cps.append(_remote(src, outs[t].at[k], send_sems, recv_sems, s0 + 3 * t + k, (ox, oy, c)))
        return cps

    def begin(*a):
        for cp in copies(*a):
            cp.start()

    def finish(*a):
        for cp in copies(*a):
            cp.wait_recv()
        for cp in copies(*a):
            cp.wait_send()

    out_shapes = [jax.ShapeDtypeStruct((3,) + _half_shard_shape(shp, axis, size), p.dtype)
                  for p, (axis, size), shp in zip(parts, geom, shapes)]
    return Exchange(parts, out_shapes, {}, 3 * len(parts), begin, finish)


def share_exchange(grads):
    n = 2 * len(grads)

    def my_half(refs, t, h):
        lyr = refs[t // 2].at[t % 2]
        if len(lyr.shape) == 3:
            rows = lyr.shape[1] // 2
            return lyr.at[:, pl.ds(h * rows, rows), :]
        rows = lyr.shape[0] // 2
        return lyr.at[pl.ds(h * rows, rows), :]

    def begin(ins, outs, send_sems, recv_sems, s0):
        x, y, c, _, _ = _place()
        for t in range(n):
            mine = my_half(outs, t, c)
            _remote(mine, mine, send_sems, recv_sems, s0 + t, (x, y, 1 - c)).start()

    def finish(ins, outs, send_sems, recv_sems, s0):
        x, y, c, _, _ = _place()
        for t in range(n):
            got = my_half(outs, t, 1 - c)
            _remote(got, got, send_sems, recv_sems, s0 + t, (x, y, 1 - c)).wait_recv()
        for t in range(n):
            mine = my_half(outs, t, c)
            _remote(mine, mine, send_sems, recv_sems, s0 + t, (x, y, 1 - c)).wait_send()

    return Exchange(grads, [jax.ShapeDtypeStruct(g.shape, g.dtype) for g in grads],
                    {t: t for t in range(len(grads))}, n, begin, finish)


def all_reduce_small(s):
    rows = s.shape[0]
    half = rows // 2
    assert half % 8 == 0

    def body(s_ref, o_ref, a_ref, b_ref, p_ref, send_sems, recv_sems):
        x, y, c, j, others = _place()
        sib = (x, y, 1 - c)
        swap = _remote(s_ref, a_ref, send_sems, recv_sems, 0, sib)
        swap.start()
        swap.wait()
        p_ref[...] = s_ref[...] + a_ref[...]
        mine = pl.ds(pl.multiple_of(c * half, 8), half)
        b_ref[j] = p_ref[mine, :]
        cps = [_remote(p_ref.at[mine, :], b_ref.at[j], send_sems, recv_sems, 1 + k, (ox, oy, c))
               for k, (ox, oy) in enumerate(others)]
        for cp in cps:
            cp.start()
        for k, (ox, oy) in enumerate(others):
            slot = b_ref.at[2 * ox + oy]
            _remote(slot, slot, send_sems, recv_sems, 1 + k, (ox, oy, c)).wait_recv()
        for cp in cps:
            cp.wait_send()
        o_ref[mine, :] = ((b_ref[0] + b_ref[1]) + b_ref[2]) + b_ref[3]
        back = _remote(o_ref.at[mine, :], o_ref.at[mine, :], send_sems, recv_sems, 4, sib)
        back.start()
        back.wait_send()
        theirs = pl.ds(pl.multiple_of((1 - c) * half, 8), half)
        _remote(o_ref.at[theirs, :], o_ref.at[theirs, :], send_sems, recv_sems, 4, sib).wait_recv()

    vmem = pl.BlockSpec(memory_space=pltpu.VMEM)
    return pl.pallas_call(
        body, name="all_reduce_small", in_specs=[vmem], out_specs=vmem,
        out_shape=jax.ShapeDtypeStruct((rows, LANES), F32),
        scratch_shapes=[pltpu.VMEM((rows, LANES), F32), pltpu.VMEM((N_CHIPS, half, LANES), F32),
                        pltpu.VMEM((rows, LANES), F32), pltpu.SemaphoreType.DMA((5,)),
                        pltpu.SemaphoreType.DMA((5,))],
        compiler_params=pltpu.CompilerParams(vmem_limit_bytes=VMEM_LIMIT, has_side_effects=True))(s)


def pair_sum(g, got, axis, size, place, *, name):
    shp = g.shape[1:]
    if len(shp) == 3:
        hs = size // 2
        grid = (N_CHIPS,)
        g_spec = pl.BlockSpec((None, shp[0], hs, shp[2]), lambda jb, pr: (0, 0, 2 * jb + pr[1], 0))
        r_spec = pl.BlockSpec((None, shp[0], hs, shp[2]), lambda jb, pr: (jb, 0, 0, 0))
    elif axis == 0:
        hs = size // 2
        tr = _row_tile(hs, shp[1], 16)
        steps = hs // tr
        grid = (N_CHIPS, steps)
        g_spec = pl.BlockSpec((None, tr, shp[1]), lambda jb, i, pr: (0, (2 * jb + pr[1]) * steps + i, 0))
        r_spec = pl.BlockSpec((None, tr, shp[1]), lambda jb, i, pr: (jb, i, 0))
    else:
        rows = shp[0] // 2
        tr = _row_tile(rows, shp[1], 16)
        steps = rows // tr
        grid = (steps,)
        g_spec = pl.BlockSpec((None, tr, shp[1]), lambda i, pr: (0, pr[1] * steps + i, 0))
        r_spec = pl.BlockSpec((tr, shp[1]), lambda i, pr: (i, 0))

    def body(pr_ref, g_ref, r_ref, o_ref):
        del pr_ref
        o_ref[...] = (g_ref[...].astype(F32) + r_ref[...].astype(F32)).astype(BF16)

    return pl.pallas_call(
        body, name=name,
        grid_spec=pltpu.PrefetchScalarGridSpec(num_scalar_prefetch=1, grid=grid, in_specs=[g_spec, r_spec],
                                               out_specs=r_spec),
        out_shape=jax.ShapeDtypeStruct(got.shape, BF16), compiler_params=_params(len(grid)))(place, g, got)


def chip_sum(part, slots, shp, axis, size, l, place, out, *, name):
    shard = _shard_shape(shp, axis, size)
    hshape = slots.shape[1:]
    if len(shp) == 3:
        grid = (1,)
        p_spec = pl.BlockSpec((None,) + hshape, lambda i, pr: (pr[0], 0, 0, 0))
        s_specs = [pl.BlockSpec((None,) + hshape, functools.partial(lambda i, pr, k: (k, 0, 0, 0), k=k))
                   for k in range(3)]
        o_spec = pl.BlockSpec((None,) + hshape, lambda i, pr: (l, 0, pr[1], 0))
    else:
        tr = _row_tile(hshape[0], hshape[1], 16)
        steps = hshape[0] // tr
        grid = (steps,)
        if axis == 0:
            p_spec = pl.BlockSpec((None, tr, hshape[1]), lambda i, pr: (pr[0], i, 0))
        else:
            p_spec = pl.BlockSpec((tr, hshape[1]), lambda i, pr: (i, pr[0]))
        s_specs = [pl.BlockSpec((None, tr, hshape[1]), functools.partial(lambda i, pr, k: (k, i, 0), k=k))
                   for k in range(3)]
        o_spec = pl.BlockSpec((None, tr, hshape[1]), lambda i, pr: (l, pr[1] * steps + i, 0))
    has_out = out is not None

    def body(pr_ref, p_ref, s0_ref, s1_ref, s2_ref, *rest):
        del pr_ref
        rest[-1][...] = ((p_ref[...].astype(F32) + s0_ref[...].astype(F32)) + s1_ref[...].astype(F32)) \
            + s2_ref[...].astype(F32)

    return pl.pallas_call(
        body, name=name,
        grid_spec=pltpu.PrefetchScalarGridSpec(
            num_scalar_prefetch=1, grid=grid, in_specs=[p_spec] + s_specs + ([ANY] if has_out else []),
            out_specs=o_spec),
        out_shape=jax.ShapeDtypeStruct((2,) + shard, F32), input_output_aliases={5: 0} if has_out else {},
        compiler_params=_params(1))(place, part, slots, slots, slots, *([out] if has_out else []))


GEOM = {name: (axis, size) for (name, _, axis, size) in BIG}
SHAPE = {name: shape for (name, shape, _, _) in BIG}
GATHER_GROUPS = (("w_in", "w_pool", "w_branch_a", "w_branch_b", "w_out"), ("w_up", "w_down", "w_ple_gate", "w_ple"))


def _reduce_layer_start(G, place, l):
    names = [t[0] for t in BIG]
    geom = [GEOM[k] for k in names]
    got = run_exchanges([swap_exchange([G[k] for k in names], geom)], name=f"swap_halves_l{l}")
    parts = [pair_sum(G[k], r, *GEOM[k], place, name=f"pair_sum_{k}_l{l}") for k, r in zip(names, got)]
    return scatter_exchange(parts, geom, [SHAPE[k] for k in names]), parts


def _reduce_layer_end(parts, slots, place, l, outs):
    names = [t[0] for t in BIG]
    return [chip_sum(q, s, SHAPE[k], *GEOM[k], l, place, o, name=f"chip_sum_{k}_l{l}")
            for k, q, s, o in zip(names, parts, slots, outs)]


def _local_step(x, p2, tgt, W0, W1_placed, conv_w, small, place):
    T = x.shape[0]
    as3 = lambda a: a.reshape(2, 1, a.shape[-1])
    mix3, scale3, sgu3 = as3(small["mix_norm"]), as3(small["pool_scale"]), as3(small["sgu_norm"])
    ffn3, ple3, convb3 = as3(small["ffn_norm"]), as3(small["ple_norm"]), as3(small["conv_b"])
    tril = jnp.tril(jnp.ones((CHUNK, CHUNK), F32))
    ws_masked = small["w_spatial"] * tril
    wsm = ws_masked.astype(BF16)
    wsmT = jnp.swapaxes(ws_masked, -1, -2).astype(BF16)
    bT = jnp.swapaxes(small["b_spatial"], -1, -2)
    final3 = small["final_norm"].reshape(1, D)
    W = [W0, dict(W1_placed)]

    def gather_group(names):
        return [gather_exchange([W[1][k] for k in names], [GEOM[k] for k in names])]

    saved = []
    hb = norm_fwd(x, mix3, 0, name="mix_norm_fwd_l0")
    for l in range(2):
        n = lambda s: f"{s}_l{l}"
        Wl = W[l]
        if l == 0:
            z, got = mm_nn(hb, Wl["w_in"], 0, name=n("in_proj"), rows=T, tn=1280, out_dtype=BF16,
                           host=gather_group(GATHER_GROUPS[0]))
            W[1].update(zip(GATHER_GROUPS[0], got))
        else:
            z = mm_nn(hb, Wl["w_in"], 0, name=n("in_proj"), rows=T, tn=1280, out_dtype=BF16)
        a_in = pool_fwd(z, Wl["w_pool"], scale3, l, name=n("pool_fwd"))
        s_in = sgu_fwd(z, sgu3, wsm, bT, l, name=n("sgu_fwd"))
        yab = mm_nn(a_in, Wl["w_branch_a"], 0, name=n("branch_a"), rows=T, out_cols=2 * D, out_dtype=BF16)
        yab = mm_nn(s_in, Wl["w_branch_b"], 0, name=n("branch_b"), rows=T, out=yab, out_cols=2 * D, out_col_off=D,
                    out_dtype=BF16)
        mo = gate_fwd(z, yab, name=n("gate_fwd"))
        x1, h2b = mm_nn(mo, Wl["w_out"], 0, name=n("out_proj"), rows=T, resid=x, norm_gain=ffn3[l:l + 1])
        if l == 0:
            up, got = mm_nn(h2b, Wl["w_up"], 0, name=n("up_proj"), rows=T, tn=1408,
                            host=gather_group(GATHER_GROUPS[1]))
            W[1].update(zip(GATHER_GROUPS[1], got))
        else:
            up = mm_nn(h2b, Wl["w_up"], 0, name=n("up_proj"), rows=T, tn=1408)
        f = conv_fwd(up, conv_w, convb3, l, name=n("conv_fwd"))
        x2, h3b = mm_nn(f, Wl["w_down"], 0, name=n("down_proj"), rows=T, resid=x1, norm_gain=ple3[l:l + 1])
        pg = mm_nn(h3b, Wl["w_ple_gate"], 0, name=n("ple_gate_proj"), rows=T, out_dtype=BF16)
        e = mm_nn(p2, Wl["w_ple"], 0, name=n("ple_proj"), rows=T, a_row_off=l * T, out_dtype=BF16)
        saved.append(dict(x=x, hb=hb, z=z, a_in=a_in, s_in=s_in, yab=yab, mo=mo, x1=x1, h2b=h2b, up=up, f=f,
                          x2=x2, h3b=h3b, pg=pg, e=e))
        if l == 0:
            x, hb = ple_fwd(x2, pg, e, mix3, 1, name=n("ple_fwd"))
        else:
            x = ple_fwd(x2, pg, e, None, 0, name=n("ple_fwd"))

    loss_acc, dx, dg_final = loss_head(x, final3, tgt, name="loss_head")

    small_grads = [None, None]
    scatter1 = parts1 = slots1 = None
    for l in (1, 0):
        n = lambda s: f"{s}_l{l}"
        a, Wl, G = saved[l], W[l], {}
        de, dpg = ple_bwd(dx, a["pg"], a["e"], name=n("ple_bwd"))
        G["w_ple"] = mm_tn(p2, de, name=n("d_w_ple"), rows=T, ka=PG, nb=D, a_row_off=l * T)
        G["w_ple_gate"] = mm_tn(a["h3b"], dpg, name=n("d_w_ple_gate"), rows=T, ka=D, nb=D)
        dh3 = mm_nt(dpg, Wl["w_ple_gate"], 0, name=n("d_ple_norm_out"), rows=T)
        dx2, dg_ple = norm_bwd(dh3, a["x2"], ple3, l, dx, name=n("ple_norm_bwd"))
        df = mm_nt(dx2, Wl["w_down"], 0, name=n("d_ffn_act"), rows=T, tn=1408, out_dtype=F32)
        G["w_down"] = mm_tn(a["f"], dx2, name=n("d_w_down"), rows=T, ka=DFF, nb=D, tm=1408)
        if l == 0:
            dup, dcw, dcb, slots1 = conv_bwd(df, a["up"], conv_w, convb3, l, name=n("conv_bwd"), host=[scatter1])
        else:
            dup, dcw, dcb, _ = conv_bwd(df, a["up"], conv_w, convb3, l, name=n("conv_bwd"))
        G["w_up"] = mm_tn(a["h2b"], dup, name=n("d_w_up"), rows=T, ka=D, nb=2 * DFF, tn=1408)
        dh2 = mm_nt(dup, Wl["w_up"], 0, name=n("d_ffn_norm_out"), rows=T, tk=2816)
        dx1, dg_ffn = norm_bwd(dh2, a["x1"], ffn3, l, dx2, name=n("ffn_norm_bwd"))
        dmo = mm_nt(dx1, Wl["w_out"], 0, name=n("d_gated"), rows=T)
        G["w_out"] = mm_tn(a["mo"], dx1, name=n("d_w_out"), rows=T, ka=D, nb=D)
        dz, dyab = gate_bwd(dmo, a["z"], a["yab"], name=n("gate_bwd"))
        G["w_branch_a"] = mm_tn(a["a_in"], dyab, name=n("d_w_branch_a"), rows=T, ka=D, nb=D)
        G["w_branch_b"] = mm_tn(a["s_in"], dyab, name=n("d_w_branch_b"), rows=T, ka=D, nb=D, b_col_off=D)
        da = mm_nt(dyab, Wl["w_branch_a"], 0, name=n("d_pool_out"), rows=T, kdim=D)
        ds = mm_nt(dyab, Wl["w_branch_b"], 0, name=n("d_sgu_out"), rows=T, kdim=D, a_col_off=D)
        dz, dwp, dsc, dws, dbt, dgs = mixer_bwd(da, ds, a["z"], dz, Wl["w_pool"], scale3, sgu3, wsm, wsmT, bT, l,
                                                name=n("mixer_bwd"))
        G["w_pool"] = dwp.astype(BF16)[None]
        G["w_in"] = mm_tn(a["hb"], dz, name=n("d_w_in"), rows=T, ka=D, nb=5 * D, tn=1280)
        dh = mm_nt(dz, Wl["w_in"], 0, name=n("d_mix_norm_out"), rows=T, tk=2560)
        dx, dg_mix = norm_bwd(dh, a["x"], mix3, l, dx1, name=n("mix_norm_bwd"))
        small_grads[l] = dict(
            mix_norm=dg_mix[0], pool_scale=dsc[0], sgu_norm=dgs[0], w_spatial=dws, b_spatial=dbt.T,
            ffn_norm=dg_ffn[0], conv_b=jnp.concatenate([dcb[0, 0], dcb[1, 0]]), ple_norm=dg_ple[0],
            conv_w=jnp.concatenate([dcw[0, :3], dcw[1, :3]], axis=1))
        if l == 1:
            scatter1, parts1 = _reduce_layer_start(G, place, 1)
        else:
            scatter0, parts0 = _reduce_layer_start(G, place, 0)
    reduced = _reduce_layer_end(parts1, slots1, place, 1, [None] * len(BIG))
    slots0 = run_exchanges([scatter0], name="scatter_to_chips_l0")
    reduced = _reduce_layer_end(parts0, slots0, place, 0, reduced)
    return loss_acc, dx, reduced, small_grads, dg_final[0]


SMALL_ORDER = ("mix_norm", "pool_scale", "sgu_norm", "w_spatial", "b_spatial", "ffn_norm", "conv_b", "ple_norm",
               "conv_w")


def _pack_rows(pieces, row_multiple):
    flat = jnp.concatenate([a.reshape(-1) for a in pieces])
    rows = -(-flat.shape[0] // LANES)
    rows = -(-rows // row_multiple) * row_multiple
    return jnp.pad(flat, (0, rows * LANES - flat.shape[0])).reshape(rows, LANES)


def _unpack(flat, shapes):
    out, off = [], 0
    for shp in shapes:
        size = 1
        for s in shp:
            size *= s
        out.append(flat[off:off + size].reshape(shp))
        off += size
    return out


def kernel(x, p, mix_norm, w_in, w_pool, pool_scale, sgu_norm, w_spatial, b_spatial, w_branch_a, w_branch_b, w_out, ffn_norm, w_up, conv_w, conv_b, w_down, ple_norm, w_ple_gate, w_ple, final_norm, loss_target, m_mix_norm, m_w_in, m_w_pool, m_pool_scale, m_sgu_norm, m_w_spatial, m_b_spatial, m_w_branch_a, m_w_branch_b, m_w_out, m_ffn_norm, m_w_up, m_conv_w, m_conv_b, m_w_down, m_ple_norm, m_w_ple_gate, m_w_ple, m_final_norm, v_mix_norm, v_w_in, v_w_pool, v_pool_scale, v_sgu_norm, v_w_spatial, v_b_spatial, v_w_branch_a, v_w_branch_b, v_w_out, v_ffn_norm, v_w_up, v_conv_w, v_conv_b, v_w_down, v_ple_norm, v_w_ple_gate, v_w_ple, v_final_norm):
    names = ["mix_norm", "w_in", "w_pool", "pool_scale", "sgu_norm", "w_spatial", "b_spatial", "w_branch_a",
             "w_branch_b", "w_out", "ffn_norm", "w_up", "conv_w", "conv_b", "w_down", "ple_norm", "w_ple_gate",
             "w_ple", "final_norm"]
    w = dict(zip(names, [mix_norm, w_in, w_pool, pool_scale, sgu_norm, w_spatial, b_spatial, w_branch_a, w_branch_b,
                         w_out, ffn_norm, w_up, conv_w, conv_b, w_down, ple_norm, w_ple_gate, w_ple, final_norm]))
    m = dict(zip(names, [m_mix_norm, m_w_in, m_w_pool, m_pool_scale, m_sgu_norm, m_w_spatial, m_b_spatial,
                         m_w_branch_a, m_w_branch_b, m_w_out, m_ffn_norm, m_w_up, m_conv_w, m_conv_b, m_w_down,
                         m_ple_norm, m_w_ple_gate, m_w_ple, m_final_norm]))
    v = dict(zip(names, [v_mix_norm, v_w_in, v_w_pool, v_pool_scale, v_sgu_norm, v_w_spatial, v_b_spatial,
                         v_w_branch_a, v_w_branch_b, v_w_out, v_ffn_norm, v_w_up, v_conv_w, v_conv_b, v_w_down,
                         v_ple_norm, v_w_ple_gate, v_w_ple, v_final_norm]))
    T = x.shape[1]
    chip = 2 * lax.axis_index("x") + lax.axis_index("y")
    place = jnp.stack([chip, lax.axis_index("c")]).astype(jnp.int32)

    big_names = [t[0] for t in BIG]
    placed = [{k: place_shard(w[k], l, *GEOM[k], BF16, place, name=f"place_{k}_l{l}") for k in big_names}
              for l in range(2)]
    conv_w8 = jnp.pad(conv_w, ((0, 0), (0, CONV_ROWS - conv_w.shape[1]), (0, 0)))
    conv_placed = place_both_layers(conv_w8, 1, conv_w.shape[2], place, name="place_conv_w")
    gathered = run_exchanges([gather_exchange([placed[0][k] for k in big_names], [GEOM[k] for k in big_names]),
                              gather_by_layer_exchange(conv_placed, 1, conv_w.shape[2])], name="gather_layer0")
    W0 = dict(zip(big_names, gathered[:-1]))

    small = {k: w[k] for k in ("mix_norm", "pool_scale", "sgu_norm", "w_spatial", "b_spatial", "ffn_norm",
                               "conv_b", "ple_norm", "final_norm")}
    loss_acc, dx, reduced, small_grads, dg_final = _local_step(
        x.reshape(T, D), p.reshape(2 * T, p.shape[-1]), loss_target.reshape(T, D), W0, placed[1], gathered[-1],
        small, place)
    loss = lax.psum(loss_acc[0, 0], ("x", "y", "c"))
    full = run_exchanges([share_exchange(reduced)], name="share_halves")
    grads = dict(zip(big_names, full))

    pieces = [small_grads[l][k] for l in range(2) for k in SMALL_ORDER] + [dg_final]
    shapes = [a.shape for a in pieces]
    total = all_reduce_small(_pack_rows(pieces, 16)).reshape(-1)
    summed = _unpack(total, shapes)
    per_layer = {k: jnp.stack([summed[i], summed[len(SMALL_ORDER) + i]]) for i, k in enumerate(SMALL_ORDER)}
    for k in ("mix_norm", "pool_scale", "sgu_norm", "w_spatial", "b_spatial", "ffn_norm", "conv_b", "ple_norm"):
        grads[k] = per_layer[k]
    grads["final_norm"] = summed[-1]
    cw = conv_w.shape[2]
    grads["conv_w"] = lax.dynamic_slice_in_dim(per_layer["conv_w"], chip * cw, cw, axis=2)

    delta, new_m, new_v = {}, {}, {}
    for name in big_names:
        shp = w[name].shape
        d_, m_, v_ = elementwise(_adamw, [_view2d(a) for a in (w[name], grads[name], m[name], v[name])],
                                 [F32, F32, F32], name=f"adamw_{name}")
        delta[name], new_m[name], new_v[name] = d_.reshape(shp), m_.reshape(shp), v_.reshape(shp)
    small_names = [k for k in names if k not in big_names]
    small_shapes = [w[k].shape for k in small_names]
    packed = [_pack_rows([src[k] for k in small_names], 8) for src in (w, grads, m, v)]
    outs = elementwise(_adamw, packed, [F32, F32, F32], name="adamw_small")
    for dst, o in zip((delta, new_m, new_v), outs):
        for k, a in zip(small_names, _unpack(o.reshape(-1), small_shapes)):
            dst[k] = a

    return (loss, dx.reshape(1, T, D), *[grads[k] for k in names], *[delta[k] for k in names],
            *[new_m[k] for k in names], *[new_v[k] for k in names])
```

```python
import functools

import jax
import jax.numpy as jnp
from jax import lax
from jax.experimental import pallas as pl
from jax.experimental.pallas import tpu as pltpu

F32 = jnp.float32
BF16 = jnp.bfloat16
EPS = 1e-6
D = 1024
POOL_WINDOWS = (2, 4, 8, 16)
PG = 256
POOL_HALO = 16
CHUNK = 128
HEADS = 8
DFF = 2816
CONV_HALO = 8
CONV_TC = 1408
N_CHIPS = 4
LANES = 128
VMEM_LIMIT = 56 * 1024 * 1024
MESH = pl.DeviceIdType.MESH
ANY = pl.BlockSpec(memory_space=pl.ANY)

ADAM_LR = 0.001
ADAM_B1 = 0.9
ADAM_B2 = 0.999
ADAM_EPS = 1e-08
ADAM_WD = 0.01
ADAM_STEP = 10

BIG = (
    ("w_in", (D, 5 * D), 1, 5 * D // N_CHIPS),
    ("w_pool", (4, PG, PG), 1, PG // N_CHIPS),
    ("w_branch_a", (D, D), 0, D // N_CHIPS),
    ("w_branch_b", (D, D), 0, D // N_CHIPS),
    ("w_out", (D, D), 0, D // N_CHIPS),
    ("w_up", (D, 2 * DFF), 1, 2 * DFF // N_CHIPS),
    ("w_down", (DFF, D), 0, DFF // N_CHIPS),
    ("w_ple_gate", (D, D), 0, D // N_CHIPS),
    ("w_ple", (PG, D), 1, D // N_CHIPS),
)
CONV_ROWS = 8


def _params(n_axes):
    return pltpu.CompilerParams(dimension_semantics=("arbitrary",) * n_axes, vmem_limit_bytes=VMEM_LIMIT)


def _gelu(x):
    return 0.5 * x * (1.0 + lax.erf(x * 0.7071067811865476))


def _gelu_grad(x):
    return 0.5 * (1.0 + lax.erf(x * 0.7071067811865476)) + x * jnp.exp(-0.5 * x * x) * 0.3989422804014327


def _shard_shape(shape, axis, size):
    return tuple(size if a == axis else s for a, s in enumerate(shape))


def _block(ref, axis, j, size):
    idx = tuple(pl.ds(j * size, size) if a == axis else slice(None) for a in range(len(ref.shape)))
    return ref.at[idx]


def mm_nn(a, w, l, *, name, rows, out_dtype=F32, resid=None, a_row_off=0, out=None, out_cols=None,
          out_col_off=0, norm_gain=None, host=None, tm=1024, tn=None, tk=None):
    K, N = w.shape[1], w.shape[2]
    tn = tn or N
    tk = tk or K
    nk = K // tk
    out_cols = out_cols or N
    assert rows % tm == 0 and N % tn == 0 and K % tk == 0 and out_col_off % tn == 0 and a_row_off % tm == 0
    has_resid, has_out, has_norm = resid is not None, out is not None, norm_gain is not None
    assert not has_norm or (tn == N and not has_out)
    grid = (N // tn, rows // tm, nk)
    hosting = _Hosting(host)
    n_in = 2 + has_resid + has_norm + has_out
    n_host_in, n_host_out = len(hosting.arrays), len(hosting.out_shapes)
    n_own_out = 1 + has_norm

    def body(*refs):
        refs = list(refs)
        a_ref, w_ref = refs[0], refs[1]
        r_ref = refs[2] if has_resid else None
        g_ref = refs[2 + has_resid] if has_norm else None
        host_in = refs[n_in:n_in + n_host_in]
        o_base = n_in + n_host_in
        o_ref = refs[o_base]
        host_out = refs[o_base + n_own_out:o_base + n_own_out + n_host_out]
        scratch = refs[o_base + n_own_out + n_host_out:]
        if hosting.plan:
            first, last = _first_last(grid)
            sems = scratch[-2:]

            @pl.when(first)
            def _():
                hosting.begin(host_in, host_out, *sems)

        part = jnp.dot(a_ref[...].astype(BF16), w_ref[...], preferred_element_type=F32)

        def finish(r):
            if has_resid:
                r = r + r_ref[...]
            o_ref[...] = r.astype(o_ref.dtype)
            if has_norm:
                scale = lax.rsqrt(jnp.mean(r * r, axis=-1, keepdims=True) + EPS)
                refs[o_base + 1][...] = (r * scale * g_ref[...]).astype(BF16)

        if nk == 1:
            finish(part)
        else:
            acc = scratch[0]
            k = pl.program_id(2)

            @pl.when(k == 0)
            def _():
                acc[...] = part

            @pl.when(k > 0)
            def _():
                acc[...] += part

            @pl.when(k == nk - 1)
            def _():
                finish(acc[...])

        if hosting.plan:
            @pl.when(last)
            def _():
                hosting.finish(host_in, host_out, *sems)

    in_specs = [pl.BlockSpec((tm, tk), lambda j, i, k: (i + a_row_off // tm, k)),
                pl.BlockSpec((None, tk, tn), lambda j, i, k: (l, k, j))]
    args = [a, w]
    if has_resid:
        in_specs.append(pl.BlockSpec((tm, tn), lambda j, i, k: (i, j)))
        args.append(resid)
    if has_norm:
        in_specs.append(pl.BlockSpec((None, 1, tn), lambda j, i, k: (l, 0, 0)))
        args.append(norm_gain)
    aliases = {}
    if has_out:
        in_specs.append(ANY)
        aliases = {len(args): 0}
        args.append(out)
    aliases.update(hosting.aliases(n_in, n_own_out))
    out_specs = [pl.BlockSpec((tm, tn), lambda j, i, k: (i, j + out_col_off // tn))]
    out_shape = [jax.ShapeDtypeStruct((rows, out_cols), out_dtype)]
    if has_norm:
        out_specs.append(pl.BlockSpec((tm, tn), lambda j, i, k: (i, j)))
        out_shape.append(jax.ShapeDtypeStruct((rows, N), BF16))
    res = pl.pallas_call(
        body, name=name, grid=grid,
        in_specs=in_specs + [ANY] * n_host_in,
        out_specs=out_specs + [ANY] * n_host_out,
        out_shape=out_shape + hosting.out_shapes,
        scratch_shapes=([pltpu.VMEM((tm, tn), F32)] if nk > 1 else []) + hosting.scratch(),
        input_output_aliases=aliases, compiler_params=_params(3))(*args, *hosting.arrays)
    own = res[0] if n_own_out == 1 else tuple(res[:n_own_out])
    return (own, list(res[n_own_out:])) if hosting.plan else own


def mm_nt(a, w, l, *, name, rows, kdim=None, a_col_off=0, out_dtype=BF16, norm_bwd_of=None, tm=1024, tn=None,
          tk=None):
    R = w.shape[1]
    kdim = kdim or w.shape[2]
    tn = tn or R
    tk = tk or kdim
    nk = kdim // tk
    assert rows % tm == 0 and R % tn == 0 and kdim % tk == 0 and a_col_off % tk == 0
    fused = norm_bwd_of is not None
    assert not fused or tn == R

    def body(a_ref, w_ref, *rest):
        part = lax.dot_general(a_ref[...].astype(BF16), w_ref[...], (((1,), (1,)), ((), ())),
                               preferred_element_type=F32)
        i, k = pl.program_id(1), pl.program_id(2)

        def finish(dh):
            if not fused:
                rest[0][...] = dh.astype(rest[0].dtype)
                return
            x_ref, g_ref, dxi_ref, dx_ref, dg_ref = rest[:5]

            @pl.when(i == 0)
            def _():
                dg_ref[...] = jnp.zeros_like(dg_ref)

            xv = x_ref[...]
            r = lax.rsqrt(jnp.mean(xv * xv, axis=-1, keepdims=True) + EPS)
            xh = xv * r
            dhg = dh * g_ref[...]
            dx_ref[...] = dxi_ref[...] + r * (dhg - xh * jnp.mean(dhg * xh, axis=-1, keepdims=True))
            dg_ref[0:1, :] += jnp.sum(dh * xh, axis=0, keepdims=True)

        if nk == 1:
            finish(part)
        else:
            acc = rest[-1]

            @pl.when(k == 0)
            def _():
                acc[...] = part

            @pl.when(k > 0)
            def _():
                acc[...] += part

            @pl.when(k == nk - 1)
            def _():
                finish(acc[...])

    if a.ndim == 3:
        per = a.shape[2] // tk
        a_spec = pl.BlockSpec((None, tm, tk), lambda j, i, k: (k // per, i, k % per))
    else:
        a_spec = pl.BlockSpec((tm, tk), lambda j, i, k: (i, k + a_col_off // tk))
    in_specs = [a_spec, pl.BlockSpec((None, tn, tk), lambda j, i, k: (l, j, k))]
    args = [a, w]
    row_tile = pl.BlockSpec((tm, tn), lambda j, i, k: (i, j))
    if fused:
        x, gain, gl, dx_in = norm_bwd_of
        in_specs += [row_tile, pl.BlockSpec((None, 1, tn), lambda j, i, k: (gl, 0, 0)), row_tile]
        args += [x, gain, dx_in]
        out_specs = [row_tile, pl.BlockSpec((8, tn), lambda j, i, k: (0, 0))]
        out_shape = [jax.ShapeDtypeStruct((rows, R), F32), jax.ShapeDtypeStruct((8, R), F32)]
    else:
        out_specs, out_shape = row_tile, jax.ShapeDtypeStruct((rows, R), out_dtype)
    return pl.pallas_call(
        body, name=name, grid=(R // tn, rows // tm, nk), in_specs=in_specs, out_specs=out_specs, out_shape=out_shape,
        scratch_shapes=[pltpu.VMEM((tm, tn), F32)] if nk > 1 else [],
        compiler_params=_params(3))(*args)


def mm_tn(a, b, *, name, rows, ka, nb, a_row_off=0, b_col_off=0, tm=None, tn=None, tk=2048):
    tm = tm or ka
    tn = tn or nb
    tk = min(tk, rows)
    nk = rows // tk
    assert ka % tm == 0 and nb % tn == 0 and rows % tk == 0 and b_col_off % tn == 0 and a_row_off % tk == 0

    def body(a_ref, b_ref, o_ref, acc):
        part = lax.dot_general(a_ref[...].astype(BF16), b_ref[...].astype(BF16), (((0,), (0,)), ((), ())),
                               preferred_element_type=F32)
        k = pl.program_id(2)

        @pl.when(k == 0)
        def _():
            acc[...] = part

        @pl.when(k > 0)
        def _():
            acc[...] += part

        @pl.when(k == nk - 1)
        def _():
            o_ref[...] = acc[...].astype(o_ref.dtype)

    if b.ndim == 3:
        per = b.shape[2] // tn
        b_spec = pl.BlockSpec((None, tk, tn), lambda j, i, k: (j // per, k, j % per))
    else:
        b_spec = pl.BlockSpec((tk, tn), lambda j, i, k: (k, j + b_col_off // tn))
    return pl.pallas_call(
        body, name=name, grid=(nb // tn, ka // tm, nk),
        in_specs=[pl.BlockSpec((tk, tm), lambda j, i, k: (k + a_row_off // tk, i)), b_spec],
        out_specs=pl.BlockSpec((None, tm, tn), lambda j, i, k: (0, i, j)),
        out_shape=jax.ShapeDtypeStruct((1, ka, nb), BF16),
        scratch_shapes=[pltpu.VMEM((tm, tn), F32)], compiler_params=_params(3))(a, b)


def _row_spec(tm, width, col=0):
    return pl.BlockSpec((tm, width), lambda i: (i, col))


def _gain_spec(l, width=D):
    return pl.BlockSpec((None, 1, width), lambda i: (l, 0, 0))


def norm_fwd(x, g3, l, *, name, tm=512):
    T = x.shape[0]

    def body(x_ref, g_ref, o_ref):
        xv = x_ref[...]
        r = lax.rsqrt(jnp.mean(xv * xv, axis=-1, keepdims=True) + EPS)
        o_ref[...] = (xv * r * g_ref[...]).astype(BF16)

    return pl.pallas_call(
        body, name=name, grid=(T // tm,),
        in_specs=[_row_spec(tm, D), _gain_spec(l)], out_specs=_row_spec(tm, D),
        out_shape=jax.ShapeDtypeStruct((T, D), BF16), compiler_params=_params(1))(x, g3)


def _winsum_back(ext, w):
    s, span = ext, 1
    while span < w:
        s = s + pltpu.roll(s, span, 0)
        span *= 2
    return s


def _winsum_fwd(ext, w):
    rows = ext.shape[0]
    s, span = ext, 1
    while span < w:
        s = s + pltpu.roll(s, rows - span, 0)
        span *= 2
    return s


def _pooled(ext, z, t, g, w):
    sl = slice(g * PG, (g + 1) * PG)
    s = _winsum_back(ext[:, sl], w)[POOL_HALO:, :]
    return s / jnp.minimum(t + 1, w).astype(F32) - z[:, sl]


def pool_fwd(z, wpool, scale3, l, *, name, tm=256):
    T = z.shape[0]
    hb = tm // POOL_HALO
    wl = l if wpool.shape[0] > 1 else 0

    def body(z_ref, zp_ref, wp_ref, sc_ref, o_ref):
        i = pl.program_id(0)
        zv = z_ref[...].astype(F32)
        prev = jnp.where(i == 0, 0.0, zp_ref[...].astype(F32))
        ext = jnp.concatenate([prev, zv], axis=0)
        t = i * tm + lax.broadcasted_iota(jnp.int32, (tm, 1), 0)
        for g, w in enumerate(POOL_WINDOWS):
            sl = slice(g * PG, (g + 1) * PG)
            pooled = _pooled(ext, zv, t, g, w)
            q = jnp.dot(pooled.astype(BF16), wp_ref[g], preferred_element_type=F32)
            o_ref[:, sl] = (q * sc_ref[:, sl]).astype(BF16)

    return pl.pallas_call(
        body, name=name, grid=(T // tm,),
        in_specs=[_row_spec(tm, D),
                  pl.BlockSpec((POOL_HALO, D), lambda i: (jnp.maximum(i * hb - 1, 0), 0)),
                  pl.BlockSpec((None, 4, PG, PG), lambda i: (wl, 0, 0, 0)),
                  _gain_spec(l)],
        out_specs=_row_spec(tm, D),
        out_shape=jax.ShapeDtypeStruct((T, D), BF16), compiler_params=_params(1))(z, z, wpool, scale3)


def sgu_fwd(z, g3, wsm, bT, l, *, name, tm=256):
    T = z.shape[0]

    def body(zu_ref, zv_ref, g_ref, ws_ref, b_ref, o_ref):
        gu = _gelu(zu_ref[...].astype(F32))
        gv = _gelu(zv_ref[...].astype(F32))
        rv = lax.rsqrt(jnp.mean(gv * gv, axis=-1, keepdims=True) + EPS)
        vn = (gv * rv * g_ref[...]).astype(BF16)
        for n in range(tm // CHUNK):
            r = slice(n * CHUNK, (n + 1) * CHUNK)
            for h in range(HEADS):
                cs = slice(h * CHUNK, (h + 1) * CHUNK)
                mixed = jnp.dot(ws_ref[h], vn[r, cs], preferred_element_type=F32) + b_ref[:, h:h + 1]
                o_ref[r, cs] = (gu[r, cs] * mixed).astype(BF16)

    return pl.pallas_call(
        body, name=name, grid=(T // tm,),
        in_specs=[_row_spec(tm, D, 1), _row_spec(tm, D, 2), _gain_spec(l),
                  pl.BlockSpec((None, HEADS, CHUNK, CHUNK), lambda i: (l, 0, 0, 0)),
                  pl.BlockSpec((None, CHUNK, HEADS), lambda i: (l, 0, 0))],
        out_specs=_row_spec(tm, D),
        out_shape=jax.ShapeDtypeStruct((T, D), BF16), compiler_params=_params(1))(z, z, g3, wsm, bT)


def gate_fwd(z, yab, *, name, tm=512):
    T = z.shape[0]

    def body(za_ref, zb_ref, y_ref, o_ref):
        ga = jax.nn.sigmoid(za_ref[...].astype(F32))
        gb = jax.nn.sigmoid(zb_ref[...].astype(F32))
        o_ref[...] = (ga * y_ref[:, :D].astype(F32) + gb * y_ref[:, D:].astype(F32)).astype(BF16)

    return pl.pallas_call(
        body, name=name, grid=(T // tm,),
        in_specs=[_row_spec(tm, D, 3), _row_spec(tm, D, 4), _row_spec(tm, 2 * D)],
        out_specs=_row_spec(tm, D),
        out_shape=jax.ShapeDtypeStruct((T, D), BF16), compiler_params=_params(1))(z, z, yab)


def _conv(ext, w_ref, b_ref):
    down1, down2 = pltpu.roll(ext, 1, 0), pltpu.roll(ext, 2, 0)
    c = b_ref[...] + w_ref[0:1, :] * down2
    c = c + w_ref[1:2, :] * down1
    return c + w_ref[2:3, :] * ext, down1, down2


def conv_fwd(up, convw, convb3, l, *, name, tm=256):
    T = up.shape[0]
    tc = CONV_TC
    nc = DFF // tc
    hb = tm // CONV_HALO

    def body(ua_ref, uap_ref, ub_ref, ubp_ref, wa_ref, wb_ref, ba_ref, bb_ref, o_ref):
        i = pl.program_id(1)

        def conv_of(u_ref, p_ref, w_ref, b_ref):
            ext = jnp.concatenate([jnp.where(i == 0, 0.0, p_ref[...]), u_ref[...]], axis=0)
            return _conv(ext, w_ref, b_ref)[0][CONV_HALO:, :]

        ca = conv_of(ua_ref, uap_ref, wa_ref, ba_ref)
        cb = conv_of(ub_ref, ubp_ref, wb_ref, bb_ref)
        o_ref[...] = (_gelu(ca) * cb).astype(BF16)

    def cur(off):
        return pl.BlockSpec((tm, tc), lambda j, i: (i, j + off))

    def prev(off):
        return pl.BlockSpec((CONV_HALO, tc), lambda j, i: (jnp.maximum(i * hb - 1, 0), j + off))

    def wspec(off):
        return pl.BlockSpec((None, CONV_ROWS, tc), lambda j, i: (l, 0, j + off))

    def bspec(off):
        return pl.BlockSpec((None, 1, tc), lambda j, i: (l, 0, j + off))

    return pl.pallas_call(
        body, name=name, grid=(nc, T // tm),
        in_specs=[cur(0), prev(0), cur(nc), prev(nc), wspec(0), wspec(nc), bspec(0), bspec(nc)],
        out_specs=pl.BlockSpec((tm, tc), lambda j, i: (i, j)),
        out_shape=jax.ShapeDtypeStruct((T, DFF), BF16),
        compiler_params=_params(2))(up, up, up, up, convw, convw, convb3, convb3)


def ple_fwd(x2, pg, e, g3, l, *, name, tm=512):
    T = x2.shape[0]
    has_norm = g3 is not None

    def body(x_ref, pg_ref, e_ref, *rest):
        xv = x_ref[...] + jax.nn.sigmoid(pg_ref[...].astype(F32)) * e_ref[...].astype(F32)
        if has_norm:
            g_ref, o_ref, h_ref = rest
            r = lax.rsqrt(jnp.mean(xv * xv, axis=-1, keepdims=True) + EPS)
            h_ref[...] = (xv * r * g_ref[...]).astype(BF16)
        else:
            o_ref, = rest
        o_ref[...] = xv

    x_shape = jax.ShapeDtypeStruct((T, D), F32)
    return pl.pallas_call(
        body, name=name, grid=(T // tm,),
        in_specs=[_row_spec(tm, D)] * 3 + ([_gain_spec(l)] if has_norm else []),
        out_specs=[_row_spec(tm, D)] * 2 if has_norm else _row_spec(tm, D),
        out_shape=[x_shape, jax.ShapeDtypeStruct((T, D), BF16)] if has_norm else x_shape,
        compiler_params=_params(1))(x2, pg, e, *([g3] if has_norm else []))


def loss_head(x, g3, tgt, *, name, tm=512):
    T = x.shape[0]

    def body(x_ref, g_ref, t_ref, loss_ref, dx_ref, dg_ref):
        @pl.when(pl.program_id(0) == 0)
        def _():
            loss_ref[...] = jnp.zeros_like(loss_ref)
            dg_ref[...] = jnp.zeros_like(dg_ref)

        xv, g = x_ref[...], g_ref[...]
        r = lax.rsqrt(jnp.mean(xv * xv, axis=-1, keepdims=True) + EPS)
        xh = xv * r
        err = xh * g - t_ref[...]
        loss_ref[...] += 0.5 * jnp.sum(jnp.mean(err * err, axis=-1, keepdims=True))
        dy = err * (1.0 / D)
        dyg = dy * g
        dx_ref[...] = r * (dyg - xh * jnp.mean(dyg * xh, axis=-1, keepdims=True))
        dg_ref[0:1, :] += jnp.sum(dy * xh, axis=0, keepdims=True)

    return pl.pallas_call(
        body, name=name, grid=(T // tm,),
        in_specs=[_row_spec(tm, D), pl.BlockSpec((1, D), lambda i: (0, 0)), _row_spec(tm, D)],
        out_specs=[pl.BlockSpec((8, LANES), lambda i: (0, 0)), _row_spec(tm, D),
                   pl.BlockSpec((8, D), lambda i: (0, 0))],
        out_shape=[jax.ShapeDtypeStruct((8, LANES), F32), jax.ShapeDtypeStruct((T, D), F32),
                   jax.ShapeDtypeStruct((8, D), F32)],
        compiler_params=_params(1))(x, g3, tgt)


def ple_bwd(dx, pg, e, *, name, tm=512):
    T = dx.shape[0]

    def body(dx_ref, pg_ref, e_ref, de_ref, dpg_ref):
        gate = jax.nn.sigmoid(pg_ref[...].astype(F32))
        dxv = dx_ref[...]
        de_ref[...] = (dxv * gate).astype(BF16)
        dpg_ref[...] = (dxv * e_ref[...].astype(F32) * gate * (1.0 - gate)).astype(BF16)

    return pl.pallas_call(
        body, name=name, grid=(T // tm,),
        in_specs=[_row_spec(tm, D)] * 3, out_specs=[_row_spec(tm, D)] * 2,
        out_shape=[jax.ShapeDtypeStruct((T, D), BF16)] * 2, compiler_params=_params(1))(dx, pg, e)


def conv_bwd(df, up, convw, convb3, l, *, name, host=None, tm=256):
    T = up.shape[0]
    tc = CONV_TC
    nc = DFF // tc
    hb = tm // CONV_HALO
    nt = T // tm
    rows = tm + 2 * CONV_HALO
    own = slice(CONV_HALO, CONV_HALO + tm)

    hosting = _Hosting(host)
    n_host_in, n_host_out = len(hosting.arrays), len(hosting.out_shapes)

    def body(df_ref, dfn_ref, ua_ref, uap_ref, uan_ref, ub_ref, ubp_ref, ubn_ref, wa_ref, wb_ref, ba_ref, bb_ref,
             *rest):
        host_in = rest[:n_host_in]
        dup_ref, dcw_ref, dcb_ref = rest[n_host_in:n_host_in + 3]
        host_out = rest[n_host_in + 3:n_host_in + 3 + n_host_out]
        sems = rest[n_host_in + 3 + n_host_out:]
        i = pl.program_id(1)
        if hosting.plan:
            first, last = _first_last((nc, nt))

            @pl.when(first)
            def _():
                hosting.begin(host_in, host_out, *sems)

        @pl.when(i == 0)
        def _():
            dcw_ref[...] = jnp.zeros_like(dcw_ref)
            dcb_ref[...] = jnp.zeros_like(dcb_ref)

        def ext_of(c_ref, p_ref, n_ref):
            return jnp.concatenate([jnp.where(i == 0, 0.0, p_ref[...]), c_ref[...],
                                    jnp.where(i == nt - 1, 0.0, n_ref[...])], axis=0)

        ea = ext_of(ua_ref, uap_ref, uan_ref)
        eb = ext_of(ub_ref, ubp_ref, ubn_ref)
        ca, ea1, ea2 = _conv(ea, wa_ref, ba_ref)
        cb, eb1, eb2 = _conv(eb, wb_ref, bb_ref)
        df_ext =jnp.concatenate([jnp.zeros((CONV_HALO, tc), F32), df_ref[...],
                                  jnp.where(i == nt - 1, 0.0, dfn_ref[...])], axis=0)
        cdf = 0.5 * (1.0 + lax.erf(ca * 0.7071067811865476))
        da = df_ext * cb * (cdf + ca * jnp.exp(-0.5 * ca * ca) * 0.3989422804014327)
        db = df_ext * (ca * cdf)

        def finish(h, dc, e, e1, e2, w_ref):
            dup = w_ref[2:3, :] * dc + w_ref[1:2, :] * pltpu.roll(dc, rows - 1, 0)
            dup = dup + w_ref[0:1, :] * pltpu.roll(dc, rows - 2, 0)
            dup_ref[h] = dup[own, :].astype(BF16)
            dco = dc[own, :]
            dcb_ref[h, 0:1, :] += jnp.sum(dco, axis=0, keepdims=True)
            dcw_ref[h, 0:1, :] += jnp.sum(dco * e2[own, :], axis=0, keepdims=True)
            dcw_ref[h, 1:2, :] += jnp.sum(dco * e1[own, :], axis=0, keepdims=True)
            dcw_ref[h, 2:3, :] += jnp.sum(dco * e[own, :], axis=0, keepdims=True)

        finish(0, da, ea, ea1, ea2, wa_ref)
        finish(1, db, eb, eb1, eb2, wb_ref)
        if hosting.plan:
            @pl.when(last)
            def _():
                hosting.finish(host_in, host_out, *sems)

    def nxt(i):
        return jnp.minimum((i + 1) * hb, T // CONV_HALO - 1)

    def prv(i):
        return jnp.maximum(i * hb - 1, 0)

    def up_specs(off):
        return [pl.BlockSpec((tm, tc), lambda j, i: (i, j + off)),
                pl.BlockSpec((CONV_HALO, tc), lambda j, i: (prv(i), j + off)),
                pl.BlockSpec((CONV_HALO, tc), lambda j, i: (nxt(i), j + off))]

    in_specs = [pl.BlockSpec((tm, tc), lambda j, i: (i, j)),
                pl.BlockSpec((CONV_HALO, tc), lambda j, i: (nxt(i), j)),
                *up_specs(0), *up_specs(nc),
                pl.BlockSpec((None, CONV_ROWS, tc), lambda j, i: (l, 0, j)),
                pl.BlockSpec((None, CONV_ROWS, tc), lambda j, i: (l, 0, j + nc)),
                pl.BlockSpec((None, 1, tc), lambda j, i: (l, 0, j)),
                pl.BlockSpec((None, 1, tc), lambda j, i: (l, 0, j + nc))]
    res = pl.pallas_call(
        body, name=name, grid=(nc, nt), in_specs=in_specs + [ANY] * n_host_in,
        out_specs=[pl.BlockSpec((2, tm, tc), lambda j, i: (0, i, j)),
                   pl.BlockSpec((2, 8, tc), lambda j, i: (0, 0, j)),
                   pl.BlockSpec((2, 8, tc), lambda j, i: (0, 0, j))] + [ANY] * n_host_out,
        out_shape=[jax.ShapeDtypeStruct((2, T, DFF), BF16), jax.ShapeDtypeStruct((2, 8, DFF), F32),
                   jax.ShapeDtypeStruct((2, 8, DFF), F32)] + hosting.out_shapes,
        scratch_shapes=hosting.scratch(), input_output_aliases=hosting.aliases(12, 3),
        compiler_params=_params(2))(df, df, up, up, up, up, up, up, convw, convw, convb3, convb3, *hosting.arrays)
    return res[0], res[1], res[2], list(res[3:])


def gate_bwd(dmo, z, yab, *, name, tm=512):
    T = z.shape[0]

    def body(dmo_ref, zg_ref, y_ref, dz_ref, dy_ref):
        g = jax.nn.sigmoid(zg_ref[...].astype(F32))
        dmo_v = dmo_ref[...].astype(F32)
        dy_ref[...] = (dmo_v * g).astype(BF16)
        dz_ref[...] = (dmo_v * y_ref[...].astype(F32) * g * (1.0 - g)).astype(BF16)

    return pl.pallas_call(
        body, name=name, grid=(T // tm, 2),
        in_specs=[pl.BlockSpec((tm, D), lambda i, s: (i, 0)),
                  pl.BlockSpec((tm, D), lambda i, s: (i, 3 + s)),
                  pl.BlockSpec((tm, D), lambda i, s: (i, s))],
        out_specs=[pl.BlockSpec((tm, D), lambda i, s: (i, 3 + s)),
                   pl.BlockSpec((tm, D), lambda i, s: (i, s))],
        out_shape=[jax.ShapeDtypeStruct((T, 5 * D), BF16), jax.ShapeDtypeStruct((T, 2 * D), BF16)],
        compiler_params=_params(2))(dmo, z, yab)


def mixer_bwd(da, ds, z, dz, wpool, scale3, g3, wsm, wsmT, bT, l, *, name, tm=256):
    T = z.shape[0]
    hb = tm // POOL_HALO
    nt = T // tm

    def body(da_ref, dan_ref, ds_ref, zp_ref, zpp_ref, zu_ref, zv_ref, wp_ref, sc_ref, g_ref, ws_ref, wst_ref,
             b_ref, dzin_ref, dz_ref, dwp_ref, dsc_ref, dws_ref, dbt_ref, dgs_ref, mixed_scr, dvn_scr, db_scr):
        del dzin_ref
        i = pl.program_id(0)

        @pl.when(i == 0)
        def _():
            dwp_ref[...] = jnp.zeros_like(dwp_ref)
            dsc_ref[...] = jnp.zeros_like(dsc_ref)
            dws_ref[...] = jnp.zeros_like(dws_ref)
            dgs_ref[...] = jnp.zeros_like(dgs_ref)
            db_scr[...] = jnp.zeros_like(db_scr)

        zv_p = zp_ref[...].astype(F32)
        ext = jnp.concatenate([jnp.where(i == 0, 0.0, zpp_ref[...].astype(F32)), zv_p], axis=0)
        da_v = da_ref[...].astype(F32)
        da_ext = jnp.concatenate([da_v, jnp.where(i == nt - 1, 0.0, dan_ref[...].astype(F32))], axis=0)
        t = i * tm + lax.broadcasted_iota(jnp.int32, (tm, 1), 0)
        t_ext = i * tm + lax.broadcasted_iota(jnp.int32, (tm + POOL_HALO, 1), 0)
        for g, w in enumerate(POOL_WINDOWS):
            sl = slice(g * PG, (g + 1) * PG)
            pooled = _pooled(ext, zv_p, t, g, w).astype(BF16)
            q = jnp.dot(pooled, wp_ref[g], preferred_element_type=F32)
            dsc_ref[0:1, sl] += jnp.sum(da_v[:, sl] * q, axis=0, keepdims=True)
            dq_ext = (da_ext[:, sl] * sc_ref[:, sl]).astype(BF16)
            dwp_ref[g] += lax.dot_general(pooled, dq_ext[:tm, :], (((0,), (0,)), ((), ())),
                                          preferred_element_type=F32)
            dpool = lax.dot_general(dq_ext, wp_ref[g], (((1,), (1,)), ((), ())), preferred_element_type=F32)
            spread = _winsum_fwd(dpool / jnp.minimum(t_ext + 1, w).astype(F32), w)
            dz_ref[:, sl] = (spread[:tm, :] - dpool[:tm, :]).astype(BF16)

        zu, zv, ds_v = zu_ref[...].astype(F32), zv_ref[...].astype(F32), ds_ref[...].astype(F32)
        gain = g_ref[...]
        gu, gv = _gelu(zu), _gelu(zv)
        rv = lax.rsqrt(jnp.mean(gv * gv, axis=-1, keepdims=True) + EPS)
        vh = gv * rv
        vn = (vh * gain).astype(BF16)
        dmix = ds_v * gu
        dmix_b = dmix.astype(BF16)
        for n in range(tm // CHUNK):
            r = slice(n * CHUNK, (n + 1) * CHUNK)
            db_scr[...] += dmix[r, :]
            for h in range(HEADS):
                cs = slice(h * CHUNK, (h + 1) * CHUNK)
                mixed_scr[r, cs] = jnp.dot(ws_ref[h], vn[r, cs], preferred_element_type=F32) + b_ref[:, h:h + 1]
                dws_ref[h] += lax.dot_general(dmix_b[r, cs], vn[r, cs], (((1,), (1,)), ((), ())),
                                              preferred_element_type=F32)
                dvn_scr[r, cs] = jnp.dot(wst_ref[h], dmix_b[r, cs], preferred_element_type=F32)
        dz_ref[:, D:2 * D] = (ds_v * mixed_scr[...] * _gelu_grad(zu)).astype(BF16)
        dvn = dvn_scr[...]
        dgs_ref[0:1, :] += jnp.sum(dvn * vh, axis=0, keepdims=True)
        dvg = dvn * gain
        dgv = rv * (dvg - vh * jnp.mean(dvg * vh, axis=-1, keepdims=True))
        dz_ref[:, 2 * D:3 * D] = (dgv * _gelu_grad(zv)).astype(BF16)

        @pl.when(i == nt - 1)
        def _():
            tril = (lax.broadcasted_iota(jnp.int32, (CHUNK, CHUNK), 0)
                    >= lax.broadcasted_iota(jnp.int32, (CHUNK, CHUNK), 1)).astype(F32)
            for h in range(HEADS):
                dws_ref[h] = dws_ref[h] * tril
                dbt_ref[:, h:h + 1] = jnp.sum(db_scr[:, h * CHUNK:(h + 1) * CHUNK], axis=1, keepdims=True)

    const4 = lambda i: (l, 0, 0, 0)
    wl = l if wpool.shape[0] > 1 else 0
    in_specs = [
        _row_spec(tm, D),
        pl.BlockSpec((POOL_HALO, D), lambda i: (jnp.minimum((i + 1) * hb, T // POOL_HALO - 1), 0)),
        _row_spec(tm, D),
        _row_spec(tm, D, 0),
        pl.BlockSpec((POOL_HALO, D), lambda i: (jnp.maximum(i * hb - 1, 0), 0)),
        _row_spec(tm, D, 1), _row_spec(tm, D, 2),
        pl.BlockSpec((None, 4, PG, PG), lambda i: (wl, 0, 0, 0)),
        _gain_spec(l), _gain_spec(l),
        pl.BlockSpec((None, HEADS, CHUNK, CHUNK), const4),
        pl.BlockSpec((None, HEADS, CHUNK, CHUNK), const4),
        pl.BlockSpec((None, CHUNK, HEADS), lambda i: (l, 0, 0)),
        ANY,
    ]
    out_specs = [
        pl.BlockSpec((tm, 3 * D), lambda i: (i, 0)),
        pl.BlockSpec((4, PG, PG), lambda i: (0, 0, 0)),
        pl.BlockSpec((8, D), lambda i: (0, 0)),
        pl.BlockSpec((HEADS, CHUNK, CHUNK), lambda i: (0, 0, 0)),
        pl.BlockSpec((CHUNK, HEADS), lambda i: (0, 0)),
        pl.BlockSpec((8, D), lambda i: (0, 0)),
    ]
    out_shape = [
        jax.ShapeDtypeStruct((T, 5 * D), BF16), jax.ShapeDtypeStruct((4, PG, PG), F32),
        jax.ShapeDtypeStruct((8, D), F32), jax.ShapeDtypeStruct((HEADS, CHUNK, CHUNK), F32),
        jax.ShapeDtypeStruct((CHUNK, HEADS), F32), jax.ShapeDtypeStruct((8, D), F32),
    ]
    return pl.pallas_call(
        body, name=name, grid=(nt,), in_specs=in_specs, out_specs=out_specs, out_shape=out_shape,
        scratch_shapes=[pltpu.VMEM((tm, D), F32), pltpu.VMEM((tm, D), F32), pltpu.VMEM((CHUNK, D), F32)],
        input_output_aliases={13: 0}, compiler_params=_params(1))(
            da, da, ds, z, z, z, z, wpool, scale3, g3, wsm, wsmT, bT, dz)


def _row_tile(rows, cols, sub):
    cap = max(sub, (2 * 1024 * 1024) // (4 * cols))
    best = None
    for tr in range(sub, min(rows, cap) + 1, sub):
        if rows % tr == 0:
            best = tr
    return best or rows


def elementwise(fn, ins, out_dtypes, *, name, row_blk_offs=None, rows=None):
    cols = ins[0].shape[1]
    rows = rows or ins[0].shape[0]
    tr = _row_tile(rows, cols, 16)
    offs = row_blk_offs or [0] * len(ins)
    n_in = len(ins)

    def body(*refs):
        outs = fn(*[r[...] for r in refs[:n_in]])
        for o_ref, o in zip(refs[n_in:], outs):
            o_ref[...] = o.astype(o_ref.dtype)

    return pl.pallas_call(
        body, name=name, grid=(rows // tr,),
        in_specs=[pl.BlockSpec((tr, cols), functools.partial(lambda i, o: (i + o * (rows // tr), 0), o=o))
                  for o in offs],
        out_specs=[pl.BlockSpec((tr, cols), lambda i: (i, 0)) for _ in out_dtypes],
        out_shape=[jax.ShapeDtypeStruct((rows, cols), dt) for dt in out_dtypes],
        compiler_params=_params(1))(*ins)


def _adamw(w, g, m, v):
    m = ADAM_B1 * m + (1.0 - ADAM_B1) * g
    v = ADAM_B2 * v + (1.0 - ADAM_B2) * jnp.square(g)
    m_hat = m / (1.0 - ADAM_B1 ** ADAM_STEP)
    v_hat = v / (1.0 - ADAM_B2 ** ADAM_STEP)
    delta = -ADAM_LR * (m_hat / (jnp.sqrt(v_hat) + ADAM_EPS) + ADAM_WD * w)
    return delta, m, v


def _view2d(a):
    return a.reshape(-1, a.shape[-1])


def _place():
    x, y, c = lax.axis_index("x"), lax.axis_index("y"), lax.axis_index("c")
    others = [(1 - x, y), (x, 1 - y), (1 - x, 1 - y)]
    return x, y, c, 2 * x + y, others


def _remote(src, dst, send_sems, recv_sems, k, to):
    return pltpu.make_async_remote_copy(src_ref=src, dst_ref=dst, send_sem=send_sems.at[k], recv_sem=recv_sems.at[k],
                                        device_id=to, device_id_type=MESH)


def _half(ref, axis, j, size, h):
    if len(ref.shape) == 3:
        return ref.at[:, pl.ds(j * size + h * (size // 2), size // 2), :]
    if axis == 0:
        return ref.at[pl.ds(j * size + h * (size // 2), size // 2), :]
    rows = ref.shape[0] // 2
    return ref.at[pl.ds(h * rows, rows), pl.ds(j * size, size)]


def _half_shard_shape(shape, axis, size):
    if len(shape) == 3:
        return (shape[0], size // 2, shape[2])
    if axis == 0:
        return (size // 2, shape[1])
    return (shape[0] // 2, size)


class Exchange:
    def __init__(self, arrays, out_shapes, aliases, n_sems, begin, finish):
        self.arrays, self.out_shapes, self.aliases, self.n_sems = list(arrays), list(out_shapes), aliases, n_sems
        self.begin, self.finish = begin, finish


class _Hosting:
    def __init__(self, plan):
        self.plan = list(plan or [])
        self.arrays = [a for ex in self.plan for a in ex.arrays]
        self.out_shapes = [o for ex in self.plan for o in ex.out_shapes]
        self.n_sems = sum(ex.n_sems for ex in self.plan)

    def scratch(self):
        return [pltpu.SemaphoreType.DMA((self.n_sems,)), pltpu.SemaphoreType.DMA((self.n_sems,))] if self.plan else []

    def aliases(self, in_base, out_base):
        out, i0, o0 = {}, in_base, out_base
        for ex in self.plan:
            out.update({i0 + i: o0 + o for i, o in ex.aliases.items()})
            i0, o0 = i0 + len(ex.arrays), o0 + len(ex.out_shapes)
        return out

    def _each(self, in_refs, out_refs):
        i0 = o0 = s0 = 0
        for ex in self.plan:
            yield ex, in_refs[i0:i0 + len(ex.arrays)], out_refs[o0:o0 + len(ex.out_shapes)], s0
            i0, o0, s0 = i0 + len(ex.arrays), o0 + len(ex.out_shapes), s0 + ex.n_sems

    def begin(self, in_refs, out_refs, send_sems, recv_sems):
        for ex, ins, outs, s0 in self._each(in_refs, out_refs):
            ex.begin(ins, outs, send_sems, recv_sems, s0)

    def finish(self, in_refs, out_refs, send_sems, recv_sems):
        for ex, ins, outs, s0 in self._each(in_refs, out_refs):
            ex.finish(ins, outs, send_sems, recv_sems, s0)


def _first_last(grid):
    ids = [pl.program_id(a) for a in range(len(grid))]
    first = functools.reduce(jnp.logical_and, [i == 0 for i in ids])
    last = functools.reduce(jnp.logical_and, [i == g - 1 for i, g in zip(ids, grid)])
    return first, last


def run_exchanges(plan, *, name):
    host = _Hosting(plan)
    n_in, n_out = len(host.arrays), len(host.out_shapes)

    def body(*refs):
        ins, outs = refs[:n_in], refs[n_in:n_in + n_out]
        send_sems, recv_sems = refs[n_in + n_out:]
        host.begin(ins, outs, send_sems, recv_sems)
        host.finish(ins, outs, send_sems, recv_sems)

    return pl.pallas_call(
        body, name=name, in_specs=[ANY] * n_in, out_specs=[ANY] * n_out, out_shape=host.out_shapes,
        scratch_shapes=host.scratch(), input_output_aliases=host.aliases(0, 0),
        compiler_params=pltpu.CompilerParams(has_side_effects=True))(*host.arrays)


def place_shard(src, l, axis, size, out_dtype, place, *, name):
    shard = src.shape[1:]
    natural = tuple(size * N_CHIPS if a == axis else s for a, s in enumerate(shard))
    if len(shard) == 3:
        blk = (None,) + shard
        grid = (1,)
        in_map = lambda i, pr: (l, 0, 0, 0)
        out_map = lambda i, pr: (0, 0, pr[0], 0)
    else:
        tr = _row_tile(shard[0], shard[1], 16)
        steps = shard[0] // tr
        blk = (None, tr, shard[1])
        grid = (steps,)
        in_map = lambda i, pr: (l, i, 0)
        if axis == 0:
            out_map = lambda i, pr: (0, pr[0] * steps + i, 0)
        else:
            out_map = lambda i, pr: (0, i, pr[0])

    def body(pr_ref, s_ref, o_ref):
        del pr_ref
        o_ref[...] = s_ref[...].astype(o_ref.dtype)

    return pl.pallas_call(
        body, name=name,
        grid_spec=pltpu.PrefetchScalarGridSpec(
            num_scalar_prefetch=1, grid=grid, in_specs=[pl.BlockSpec(blk, in_map)],
            out_specs=pl.BlockSpec(blk, out_map)),
        out_shape=jax.ShapeDtypeStruct((1,) + natural, out_dtype), compiler_params=_params(1))(place, src)


def place_both_layers(src, axis, size, place, *, name):
    rows, cols = src.shape[1], src.shape[2]

    def body(pr_ref, s_ref, o_ref):
        del pr_ref
        o_ref[...] = s_ref[...]

    return pl.pallas_call(
        body, name=name,
        grid_spec=pltpu.PrefetchScalarGridSpec(
            num_scalar_prefetch=1, grid=(2,), in_specs=[pl.BlockSpec((None, rows, cols), lambda lyr, pr: (lyr, 0, 0))],
            out_specs=pl.BlockSpec((None, rows, cols), lambda lyr, pr: (lyr, 0, pr[0]))),
        out_shape=jax.ShapeDtypeStruct((2, rows, cols * N_CHIPS), src.dtype), compiler_params=_params(1))(place, src)


def gather_exchange(arrays, geom):
    n = len(arrays)

    def begin(ins, outs, send_sems, recv_sems, s0):
        x, y, c, j, others = _place()
        for t, (axis, size) in enumerate(geom):
            mine = _half(outs[t].at[0], axis, j, size, c)
            for k, (ox, oy) in enumerate(others):
                _remote(mine, mine, send_sems, recv_sems, s0 + 6 * t + k, (ox, oy, c)).start()

    def finish(ins, outs, send_sems, recv_sems, s0):
        x, y, c, j, others = _place()
        sib = (x, y, 1 - c)
        passed = []
        for t, (axis, size) in enumerate(geom):
            for k, (ox, oy) in enumerate(others):
                landed = _half(outs[t].at[0], axis, 2 * ox + oy, size, c)
                _remote(landed, landed, send_sems, recv_sems, s0 + 6 * t + k, (ox, oy, c)).wait_recv()
                fwd = _remote(landed, landed, send_sems, recv_sems, s0 + 6 * t + 3 + k, sib)
                fwd.start()
                passed.append(fwd)
        for t, (axis, size) in enumerate(geom):
            for k, (ox, oy) in enumerate(others):
                got = _half(outs[t].at[0], axis, 2 * ox + oy, size, 1 - c)
                _remote(got, got, send_sems, recv_sems, s0 + 6 * t + 3 + k, sib).wait_recv()
        for fwd in passed:
            fwd.wait_send()
        for t, (axis, size) in enumerate(geom):
            mine = _half(outs[t].at[0], axis, j, size, c)
            for k, (ox, oy) in enumerate(others):
                _remote(mine, mine, send_sems, recv_sems, s0 + 6 * t + k, (ox, oy, c)).wait_send()

    return Exchange(arrays, [jax.ShapeDtypeStruct(a.shape, a.dtype) for a in arrays], {t: t for t in range(n)},
                    6 * n, begin, finish)


def gather_by_layer_exchange(array, axis, size):
    def blocks(out, others, lyr):
        return [_block(out.at[lyr], axis, 2 * ox + oy, size) for (ox, oy) in others]

    def begin(ins, outs, send_sems, recv_sems, s0):
        x, y, c, j, others = _place()
        mine = _block(outs[0].at[c], axis, j, size)
        for k, (ox, oy) in enumerate(others):
            _remote(mine, mine, send_sems, recv_sems, s0 + k, (ox, oy, c)).start()

    def finish(ins, outs, send_sems, recv_sems, s0):
        x, y, c, j, others = _place()
        sib = (x, y, 1 - c)
        passed = []
        for k, ((ox, oy), landed) in enumerate(zip(others, blocks(outs[0], others, c))):
            _remote(landed, landed, send_sems, recv_sems, s0 + k, (ox, oy, c)).wait_recv()
            fwd = _remote(landed, landed, send_sems, recv_sems, s0 + 3 + k, sib)
            fwd.start()
            passed.append(fwd)
        for k, got in enumerate(blocks(outs[0], others, 1 - c)):
            _remote(got, got, send_sems, recv_sems, s0 + 3 + k, sib).wait_recv()
        for fwd in passed:
            fwd.wait_send()
        mine = _block(outs[0].at[c], axis, j, size)
        for k, (ox, oy) in enumerate(others):
            _remote(mine, mine, send_sems, recv_sems, s0 + k, (ox, oy, c)).wait_send()

    return Exchange([array], [jax.ShapeDtypeStruct(array.shape, array.dtype)], {0: 0}, 6, begin, finish)


def swap_exchange(grads, geom):
    def pieces(t, g, dst, h):
        axis, size = geom[t]
        if len(g.shape) == 2 and axis == 1:
            rows = g.shape[0] // 2
            return [(g.at[pl.ds(h * rows, rows), :], dst)]
        return [(_half(g, axis, jb, size, h), dst.at[jb]) for jb in range(N_CHIPS)]

    counts = [1 if (len(g.shape) == 3 and a == 1) else N_CHIPS for g, (a, _) in zip(grads, geom)]
    bases = [sum(counts[:t]) for t in range(len(grads))]

    def copies(ins, outs, send_sems, recv_sems, s0):
        x, y, c, _, _ = _place()
        cps = []
        for t in range(len(grads)):
            for q, (src, dst) in enumerate(pieces(t, ins[t].at[0], outs[t], 1 - c)):
                cps.append(_remote(src, dst, send_sems, recv_sems, s0 + bases[t] + q, (x, y, 1 - c)))
        return cps

    def begin(*a):
        for cp in copies(*a):
            cp.start()

    def finish(*a):
        for cp in copies(*a):
            cp.wait()

    out_shapes = []
    for g, (axis, size) in zip(grads, geom):
        shp = g.shape[1:]
        if len(shp) == 2 and axis == 1:
            out_shapes.append(jax.ShapeDtypeStruct((shp[0] // 2, shp[1]), g.dtype))
        else:
            out_shapes.append(jax.ShapeDtypeStruct((N_CHIPS,) + _half_shard_shape(shp, axis, size), g.dtype))
    return Exchange(grads, out_shapes, {}, sum(counts), begin, finish)


def scatter_exchange(parts, geom, shapes):
    def copies(ins, outs, send_sems, recv_sems, s0):
        x, y, c, j, others = _place()
        cps = []
        for t, ((axis, size), shp) in enumerate(zip(geom, shapes)):
            for k, (ox, oy) in enumerate(others):
                jp = 2 * ox + oy
                src = ins[t].at[:, pl.ds(jp * size, size)] if (len(shp) == 2 and axis == 1) else ins[t].at[jp]
                cps.append(_remote(src, outs[t].at[k], send_sems, recv_sems, s0 + 3 * t + k, (ox, oy, c)))
        return cps

    def begin(*a):
        for cp in copies(*a):
            cp.start()

    def finish(*a):
        for cp in copies(*a):
            cp.wait_recv()
        for cp in copies(*a):
            cp.wait_send()

    out_shapes = [jax.ShapeDtypeStruct((3,) + _half_shard_shape(shp, axis, size), p.dtype)
                  for p, (axis, size), shp in zip(parts, geom, shapes)]
    return Exchange(parts, out_shapes, {}, 3 * len(parts), begin, finish)


def share_exchange(grads):
    n = 2 * len(grads)

    def my_half(refs, t, h):
        lyr = refs[t // 2].at[t % 2]
        if len(lyr.shape) == 3:
            rows = lyr.shape[1] // 2
            return lyr.at[:, pl.ds(h * rows, rows), :]
        rows = lyr.shape[0] // 2
        return lyr.at[pl.ds(h * rows, rows), :]

    def begin(ins, outs, send_sems, recv_sems, s0):
        x, y, c, _, _ = _place()
        for t in range(n):
            mine = my_half(outs, t, c)
            _remote(mine, mine, send_sems, recv_sems, s0 + t, (x, y, 1 - c)).start()

    def finish(ins, outs, send_sems, recv_sems, s0):
        x, y, c, _, _ = _place()
        for t in range(n):
            got = my_half(outs, t, 1 - c)
            _remote(got, got, send_sems, recv_sems, s0 + t, (x, y, 1 - c)).wait_recv()
        for t in range(n):
            mine = my_half(outs, t, c)
            _remote(mine, mine, send_sems, recv_sems, s0 + t, (x, y, 1 - c)).wait_send()

    return Exchange(grads, [jax.ShapeDtypeStruct(g.shape, g.dtype) for g in grads],
                    {t: t for t in range(len(grads))}, n, begin, finish)


def all_reduce_small(s):
    rows = s.shape[0]
    half = rows // 2
    assert half % 8 == 0

    def body(s_ref, o_ref, a_ref, b_ref, p_ref, send_sems, recv_sems):
        x, y, c, j, others = _place()
        sib = (x, y, 1 - c)
        swap = _remote(s_ref, a_ref, send_sems, recv_sems, 0, sib)
        swap.start()
        swap.wait()
        p_ref[...] = s_ref[...] + a_ref[...]
        mine = pl.ds(pl.multiple_of(c * half, 8), half)
        b_ref[j] = p_ref[mine, :]
        cps = [_remote(p_ref.at[mine, :], b_ref.at[j], send_sems, recv_sems, 1 + k, (ox, oy, c))
               for k, (ox, oy) in enumerate(others)]
        for cp in cps:
            cp.start()
        for k, (ox, oy) in enumerate(others):
            slot = b_ref.at[2 * ox + oy]
            _remote(slot, slot, send_sems, recv_sems, 1 + k, (ox, oy, c)).wait_recv()
        for cp in cps:
            cp.wait_send()
        o_ref[mine, :] = ((b_ref[0] + b_ref[1]) + b_ref[2]) + b_ref[3]
        back = _remote(o_ref.at[mine, :], o_ref.at[mine, :], send_sems, recv_sems, 4, sib)
        back.start()
        back.wait_send()
        theirs = pl.ds(pl.multiple_of((1 - c) * half, 8), half)
        _remote(o_ref.at[theirs, :], o_ref.at[theirs, :], send_sems, recv_sems, 4, sib).wait_recv()

    vmem = pl.BlockSpec(memory_space=pltpu.VMEM)
    return pl.pallas_call(
        body, name="all_reduce_small", in_specs=[vmem], out_specs=vmem,
        out_shape=jax.ShapeDtypeStruct((rows, LANES), F32),
        scratch_shapes=[pltpu.VMEM((rows, LANES), F32), pltpu.VMEM((N_CHIPS, half, LANES), F32),
                        pltpu.VMEM((rows, LANES), F32), pltpu.SemaphoreType.DMA((5,)),
                        pltpu.SemaphoreType.DMA((5,))],
        compiler_params=pltpu.CompilerParams(vmem_limit_bytes=VMEM_LIMIT, has_side_effects=True))(s)


def pair_sum(g, got, axis, size, place, *, name):
    shp = g.shape[1:]
    if len(shp) == 3:
        hs = size // 2
        grid = (N_CHIPS,)
        g_spec = pl.BlockSpec((None, shp[0], hs, shp[2]), lambda jb, pr: (0, 0, 2 * jb + pr[1], 0))
        r_spec = pl.BlockSpec((None, shp[0], hs, shp[2]), lambda jb, pr: (jb, 0, 0, 0))
    elif axis == 0:
        hs = size // 2
        tr = _row_tile(hs, shp[1], 16)
        steps = hs // tr
        grid = (N_CHIPS, steps)
        g_spec = pl.BlockSpec((None, tr, shp[1]), lambda jb, i, pr: (0, (2 * jb + pr[1]) * steps + i, 0))
        r_spec = pl.BlockSpec((None, tr, shp[1]), lambda jb, i, pr: (jb, i, 0))
    else:
        rows = shp[0] // 2
        tr = _row_tile(rows, shp[1], 16)
        steps = rows // tr
        grid = (steps,)
        g_spec = pl.BlockSpec((None, tr, shp[1]), lambda i, pr: (0, pr[1] * steps + i, 0))
        r_spec = pl.BlockSpec((tr, shp[1]), lambda i, pr: (i, 0))

    def body(pr_ref, g_ref, r_ref, o_ref):
        del pr_ref
        o_ref[...] = (g_ref[...].astype(F32) + r_ref[...].astype(F32)).astype(BF16)

    return pl.pallas_call(
        body, name=name,
        grid_spec=pltpu.PrefetchScalarGridSpec(num_scalar_prefetch=1, grid=grid, in_specs=[g_spec, r_spec],
                                               out_specs=r_spec),
        out_shape=jax.ShapeDtypeStruct(got.shape, BF16), compiler_params=_params(len(grid)))(place, g, got)


def chip_sum(part, slots, shp, axis, size, l, place, out, *, name):
    shard = _shard_shape(shp, axis, size)
    hshape = slots.shape[1:]
    if len(shp) == 3:
        grid = (1,)
        p_spec = pl.BlockSpec((None,) + hshape, lambda i, pr: (pr[0], 0, 0, 0))
        s_specs = [pl.BlockSpec((None,) + hshape, functools.partial(lambda i, pr, k: (k, 0, 0, 0), k=k))
                   for k in range(3)]
        o_spec = pl.BlockSpec((None,) + hshape, lambda i, pr: (l, 0, pr[1], 0))
    else:
        tr = _row_tile(hshape[0], hshape[1], 16)
        steps = hshape[0] // tr
        grid = (steps,)
        if axis == 0:
            p_spec = pl.BlockSpec((None, tr, hshape[1]), lambda i, pr: (pr[0], i, 0))
        else:
            p_spec = pl.BlockSpec((tr, hshape[1]), lambda i, pr: (i, pr[0]))
        s_specs = [pl.BlockSpec((None, tr, hshape[1]), functools.partial(lambda i, pr, k: (k, i, 0), k=k))
                   for k in range(3)]
        o_spec = pl.BlockSpec((None, tr, hshape[1]), lambda i, pr: (l, pr[1] * steps + i, 0))
    has_out = out is not None

    def body(pr_ref, p_ref, s0_ref, s1_ref, s2_ref, *rest):
        del pr_ref
        rest[-1][...] = ((p_ref[...].astype(F32) + s0_ref[...].astype(F32)) + s1_ref[...].astype(F32)) \
            + s2_ref[...].astype(F32)

    return pl.pallas_call(
        body, name=name,
        grid_spec=pltpu.PrefetchScalarGridSpec(
            num_scalar_prefetch=1, grid=grid, in_specs=[p_spec] + s_specs + ([ANY] if has_out else []),
            out_specs=o_spec),
        out_shape=jax.ShapeDtypeStruct((2,) + shard, F32), input_output_aliases={5: 0} if has_out else {},
        compiler_params=_params(1))(place, part, slots, slots, slots, *([out] if has_out else []))


GEOM = {name: (axis, size) for (name, _, axis, size) in BIG}
SHAPE = {name: shape for (name, shape, _, _) in BIG}
GATHER_GROUPS = (("w_in", "w_pool", "w_branch_a", "w_branch_b", "w_out"), ("w_up", "w_down", "w_ple_gate", "w_ple"))


def _reduce_layer_start(G, place, l):
    names = [t[0] for t in BIG]
    geom = [GEOM[k] for k in names]
    got = run_exchanges([swap_exchange([G[k] for k in names], geom)], name=f"swap_halves_l{l}")
    parts = [pair_sum(G[k], r, *GEOM[k], place, name=f"pair_sum_{k}_l{l}") for k, r in zip(names, got)]
    return scatter_exchange(parts, geom, [SHAPE[k] for k in names]), parts


def _reduce_layer_end(parts, slots, place, l, outs):
    names = [t[0] for t in BIG]
    return [chip_sum(q, s, SHAPE[k], *GEOM[k], l, place, o, name=f"chip_sum_{k}_l{l}")
            for k, q, s, o in zip(names, parts, slots, outs)]


def _local_step(x, p2, tgt, W0, W1_placed, conv_w, small, place):
    T = x.shape[0]
    as3 = lambda a: a.reshape(2, 1, a.shape[-1])
    mix3, scale3, sgu3 = as3(small["mix_norm"]), as3(small["pool_scale"]), as3(small["sgu_norm"])
    ffn3, ple3, convb3 = as3(small["ffn_norm"]), as3(small["ple_norm"]), as3(small["conv_b"])
    tril = jnp.tril(jnp.ones((CHUNK, CHUNK), F32))
    ws_masked = small["w_spatial"] * tril
    wsm = ws_masked.astype(BF16)
    wsmT = jnp.swapaxes(ws_masked, -1, -2).astype(BF16)
    bT = jnp.swapaxes(small["b_spatial"], -1, -2)
    final3 = small["final_norm"].reshape(1, D)
    W = [W0, dict(W1_placed)]

    def gather_group(names):
        return [gather_exchange([W[1][k] for k in names], [GEOM[k] for k in names])]

    saved = []
    hb = norm_fwd(x, mix3, 0, name="mix_norm_fwd_l0")
    for l in range(2):
        n = lambda s: f"{s}_l{l}"
        Wl = W[l]
        if l == 0:
            z, got = mm_nn(hb, Wl["w_in"], 0, name=n("in_proj"), rows=T, tn=1280, out_dtype=BF16,
                           host=gather_group(GATHER_GROUPS[0]))
            W[1].update(zip(GATHER_GROUPS[0], got))
        else:
            z = mm_nn(hb, Wl["w_in"], 0, name=n("in_proj"), rows=T, tn=1280, out_dtype=BF16)
        a_in = pool_fwd(z, Wl["w_pool"], scale3, l, name=n("pool_fwd"))
        s_in = sgu_fwd(z, sgu3, wsm, bT, l, name=n("sgu_fwd"))
        yab = mm_nn(a_in, Wl["w_branch_a"], 0, name=n("branch_a"), rows=T, out_cols=2 * D, out_dtype=BF16)
        yab = mm_nn(s_in, Wl["w_branch_b"], 0, name=n("branch_b"), rows=T, out=yab, out_cols=2 * D, out_col_off=D,
                    out_dtype=BF16)
        mo = gate_fwd(z, yab, name=n("gate_fwd"))
        x1, h2b = mm_nn(mo, Wl["w_out"], 0, name=n("out_proj"), rows=T, resid=x, norm_gain=ffn3[l:l + 1])
        if l == 0:
            up, got = mm_nn(h2b, Wl["w_up"], 0, name=n("up_proj"), rows=T, tn=DFF,
                            host=gather_group(GATHER_GROUPS[1]))
            W[1].update(zip(GATHER_GROUPS[1], got))
        else:
            up = mm_nn(h2b, Wl["w_up"], 0, name=n("up_proj"), rows=T, tn=DFF)
        f = conv_fwd(up, conv_w, convb3, l, name=n("conv_fwd"))
        x2, h3b = mm_nn(f, Wl["w_down"], 0, name=n("down_proj"), rows=T, resid=x1, norm_gain=ple3[l:l + 1])
        pg = mm_nn(h3b, Wl["w_ple_gate"], 0, name=n("ple_gate_proj"), rows=T, out_dtype=BF16)
        e = mm_nn(p2, Wl["w_ple"], 0, name=n("ple_proj"), rows=T, a_row_off=l * T, out_dtype=BF16)
        saved.append(dict(x=x, hb=hb, z=z, a_in=a_in, s_in=s_in, yab=yab, mo=mo, x1=x1, h2b=h2b, up=up, f=f,
                          x2=x2, h3b=h3b, pg=pg, e=e))
        if l == 0:
            x, hb = ple_fwd(x2, pg, e, mix3, 1, name=n("ple_fwd"))
        else:
            x = ple_fwd(x2, pg, e, None, 0, name=n("ple_fwd"))

    loss_acc, dx, dg_final = loss_head(x, final3, tgt, name="loss_head")

    small_grads = [None, None]
    scatter1 = parts1 = slots1 = None
    for l in (1, 0):
        n = lambda s: f"{s}_l{l}"
        a, Wl, G = saved[l], W[l], {}
        de, dpg = ple_bwd(dx, a["pg"], a["e"], name=n("ple_bwd"))
        G["w_ple"] = mm_tn(p2, de, name=n("d_w_ple"), rows=T, ka=PG, nb=D, a_row_off=l * T)
        G["w_ple_gate"] = mm_tn(a["h3b"], dpg, name=n("d_w_ple_gate"), rows=T, ka=D, nb=D)
        dx2, dg_ple = mm_nt(dpg, Wl["w_ple_gate"], 0, name=n("ple_norm_bwd"), rows=T,
                            norm_bwd_of=(a["x2"], ple3, l, dx))
        df = mm_nt(dx2, Wl["w_down"], 0, name=n("d_ffn_act"), rows=T, out_dtype=F32)
        G["w_down"] = mm_tn(a["f"], dx2, name=n("d_w_down"), rows=T, ka=DFF, nb=D, tm=1408)
        if l == 0:
            dup, dcw, dcb, slots1 = conv_bwd(df, a["up"], conv_w, convb3, l, name=n("conv_bwd"), host=[scatter1])
        else:
            dup, dcw, dcb, _ = conv_bwd(df, a["up"], conv_w, convb3, l, name=n("conv_bwd"))
        G["w_up"] = mm_tn(a["h2b"], dup, name=n("d_w_up"), rows=T, ka=D, nb=2 * DFF, tn=DFF, tk=1024)
        dx1, dg_ffn = mm_nt(dup, Wl["w_up"], 0, name=n("ffn_norm_bwd"), rows=T, tk=1408,
                            norm_bwd_of=(a["x1"], ffn3, l, dx2))
        dmo = mm_nt(dx1, Wl["w_out"], 0, name=n("d_gated"), rows=T)
        G["w_out"] = mm_tn(a["mo"], dx1, name=n("d_w_out"), rows=T, ka=D, nb=D)
        dz, dyab = gate_bwd(dmo, a["z"], a["yab"], name=n("gate_bwd"))
        G["w_branch_a"] = mm_tn(a["a_in"], dyab, name=n("d_w_branch_a"), rows=T, ka=D, nb=D)
        G["w_branch_b"] = mm_tn(a["s_in"], dyab, name=n("d_w_branch_b"), rows=T, ka=D, nb=D, b_col_off=D)
        da = mm_nt(dyab, Wl["w_branch_a"], 0, name=n("d_pool_out"), rows=T, kdim=D)
        ds = mm_nt(dyab, Wl["w_branch_b"], 0, name=n("d_sgu_out"), rows=T, kdim=D, a_col_off=D)
        dz, dwp, dsc, dws, dbt, dgs = mixer_bwd(da, ds, a["z"], dz, Wl["w_pool"], scale3, sgu3, wsm, wsmT, bT, l,
                                                name=n("mixer_bwd"))
        G["w_pool"] = dwp.astype(BF16)[None]
        G["w_in"] = mm_tn(a["hb"], dz, name=n("d_w_in"), rows=T, ka=D, nb=5 * D, tn=1280)
        dx, dg_mix = mm_nt(dz, Wl["w_in"], 0, name=n("mix_norm_bwd"), rows=T, tk=1280,
                           norm_bwd_of=(a["x"], mix3, l, dx1))
        small_grads[l] = dict(
            mix_norm=dg_mix[0], pool_scale=dsc[0], sgu_norm=dgs[0], w_spatial=dws, b_spatial=dbt.T,
            ffn_norm=dg_ffn[0], conv_b=jnp.concatenate([dcb[0, 0], dcb[1, 0]]), ple_norm=dg_ple[0],
            conv_w=jnp.concatenate([dcw[0, :3], dcw[1, :3]], axis=1))
        if l == 1:
            scatter1, parts1 = _reduce_layer_start(G, place, 1)
        else:
            scatter0, parts0 = _reduce_layer_start(G, place, 0)
    reduced = _reduce_layer_end(parts1, slots1, place, 1, [None] * len(BIG))
    slots0 = run_exchanges([scatter0], name="scatter_to_chips_l0")
    reduced = _reduce_layer_end(parts0, slots0, place, 0, reduced)
    return loss_acc, dx, reduced, small_grads, dg_final[0]


SMALL_ORDER = ("mix_norm", "pool_scale", "sgu_norm", "w_spatial", "b_spatial", "ffn_norm", "conv_b", "ple_norm",
               "conv_w")


def _pack_rows(pieces, row_multiple):
    flat = jnp.concatenate([a.reshape(-1) for a in pieces])
    rows = -(-flat.shape[0] // LANES)
    rows = -(-rows // row_multiple) * row_multiple
    return jnp.pad(flat, (0, rows * LANES - flat.shape[0])).reshape(rows, LANES)


def _unpack(flat, shapes):
    out, off = [], 0
    for shp in shapes:
        size = 1
        for s in shp:
            size *= s
        out.append(flat[off:off + size].reshape(shp))
        off += size
    return out


def kernel(x, p, mix_norm, w_in, w_pool, pool_scale, sgu_norm, w_spatial, b_spatial, w_branch_a, w_branch_b, w_out, ffn_norm, w_up, conv_w, conv_b, w_down, ple_norm, w_ple_gate, w_ple, final_norm, loss_target, m_mix_norm, m_w_in, m_w_pool, m_pool_scale, m_sgu_norm, m_w_spatial, m_b_spatial, m_w_branch_a, m_w_branch_b, m_w_out, m_ffn_norm, m_w_up, m_conv_w, m_conv_b, m_w_down, m_ple_norm, m_w_ple_gate, m_w_ple, m_final_norm, v_mix_norm, v_w_in, v_w_pool, v_pool_scale, v_sgu_norm, v_w_spatial, v_b_spatial, v_w_branch_a, v_w_branch_b, v_w_out, v_ffn_norm, v_w_up, v_conv_w, v_conv_b, v_w_down, v_ple_norm, v_w_ple_gate, v_w_ple, v_final_norm):
    names = ["mix_norm", "w_in", "w_pool", "pool_scale", "sgu_norm", "w_spatial", "b_spatial", "w_branch_a",
             "w_branch_b", "w_out", "ffn_norm", "w_up", "conv_w", "conv_b", "w_down", "ple_norm", "w_ple_gate",
             "w_ple", "final_norm"]
    w = dict(zip(names, [mix_norm, w_in, w_pool, pool_scale, sgu_norm, w_spatial, b_spatial, w_branch_a, w_branch_b,
                         w_out, ffn_norm, w_up, conv_w, conv_b, w_down, ple_norm, w_ple_gate, w_ple, final_norm]))
    m = dict(zip(names, [m_mix_norm, m_w_in, m_w_pool, m_pool_scale, m_sgu_norm, m_w_spatial, m_b_spatial,
                         m_w_branch_a, m_w_branch_b, m_w_out, m_ffn_norm, m_w_up, m_conv_w, m_conv_b, m_w_down,
                         m_ple_norm, m_w_ple_gate, m_w_ple, m_final_norm]))
    v = dict(zip(names, [v_mix_norm, v_w_in, v_w_pool, v_pool_scale, v_sgu_norm, v_w_spatial, v_b_spatial,
                         v_w_branch_a, v_w_branch_b, v_w_out, v_ffn_norm, v_w_up, v_conv_w, v_conv_b, v_w_down,
                         v_ple_norm, v_w_ple_gate, v_w_ple, v_final_norm]))
    T = x.shape[1]
    chip = 2 * lax.axis_index("x") + lax.axis_index("y")
    place = jnp.stack([chip, lax.axis_index("c")]).astype(jnp.int32)

    big_names = [t[0] for t in BIG]
    placed = [{k: place_shard(w[k], l, *GEOM[k], BF16, place, name=f"place_{k}_l{l}") for k in big_names}
              for l in range(2)]
    conv_w8 = jnp.pad(conv_w, ((0, 0), (0, CONV_ROWS - conv_w.shape[1]), (0, 0)))
    conv_placed = place_both_layers(conv_w8, 1, conv_w.shape[2], place, name="place_conv_w")
    by_cols = [k for k in big_names if GEOM[k][0] == 1 and len(SHAPE[k]) == 2]
    by_rows = [k for k in big_names if k not in by_cols]
    got_cols = run_exchanges([gather_exchange([placed[0][k] for k in by_cols], [GEOM[k] for k in by_cols])],
                             name="gather_layer0_column_sharded")
    gathered = run_exchanges([gather_exchange([placed[0][k] for k in by_rows], [GEOM[k] for k in by_rows]),
                              gather_by_layer_exchange(conv_placed, 1, conv_w.shape[2])],
                             name="gather_layer0_row_sharded")
    W0 = dict(zip(by_cols + by_rows, list(got_cols) + list(gathered[:-1])))

    small = {k: w[k] for k in ("mix_norm", "pool_scale", "sgu_norm", "w_spatial", "b_spatial", "ffn_norm",
                               "conv_b", "ple_norm", "final_norm")}
    loss_acc, dx, reduced, small_grads, dg_final = _local_step(
        x.reshape(T, D), p.reshape(2 * T, p.shape[-1]), loss_target.reshape(T, D), W0, placed[1], gathered[-1],
        small, place)
    loss = lax.psum(loss_acc[0, 0], ("x", "y", "c"))
    full = run_exchanges([share_exchange(reduced)], name="share_halves")
    grads = dict(zip(big_names, full))

    pieces = [small_grads[l][k] for l in range(2) for k in SMALL_ORDER] + [dg_final]
    shapes = [a.shape for a in pieces]
    total = all_reduce_small(_pack_rows(pieces, 16)).reshape(-1)
    summed = _unpack(total, shapes)
    per_layer = {k: jnp.stack([summed[i], summed[len(SMALL_ORDER) + i]]) for i, k in enumerate(SMALL_ORDER)}
    for k in ("mix_norm", "pool_scale", "sgu_norm", "w_spatial", "b_spatial", "ffn_norm", "conv_b", "ple_norm"):
        grads[k] = per_layer[k]
    grads["final_norm"] = summed[-1]
    cw = conv_w.shape[2]
    grads["conv_w"] = lax.dynamic_slice_in_dim(per_layer["conv_w"], chip * cw, cw, axis=2)

    delta, new_m, new_v = {}, {}, {}
    for name in big_names:
        shp = w[name].shape
        d_, m_, v_ = elementwise(_adamw, [_view2d(a) for a in (w[name], grads[name], m[name], v[name])],
                                 [F32, F32, F32], name=f"adamw_{name}")
        delta[name], new_m[name], new_v[name] = d_.reshape(shp), m_.reshape(shp), v_.reshape(shp)
    small_names = [k for k in names if k not in big_names]
    small_shapes = [w[k].shape for k in small_names]
    packed = [_pack_rows([src[k] for k in small_names], 8) for src in (w, grads, m, v)]
    outs = elementwise(_adamw, packed, [F32, F32, F32], name="adamw_small")
    for dst, o in zip((delta, new_m, new_v), outs):
        for k, a in zip(small_names, _unpack(o.reshape(-1), small_shapes)):
            dst[k] = a

    return (loss, dx.reshape(1, T, D), *[grads[k] for k in names], *[delta[k] for k in names],
            *[new_m[k] for k in names], *[new_v[k] for k in names])
```

```python
import functools

import jax
import jax.numpy as jnp
from jax import lax
from jax.experimental import pallas as pl
from jax.experimental.pallas import tpu as pltpu

F32 = jnp.float32
BF16 = jnp.bfloat16
EPS = 1e-6
D = 1024
POOL_WINDOWS = (2, 4, 8, 16)
PG = 256
POOL_HALO = 16
CHUNK = 128
HEADS = 8
DFF = 2816
CONV_HALO = 8
CONV_TC = 1408
N_CHIPS = 4
LANES = 128
VMEM_LIMIT = 56 * 1024 * 1024
MESH = pl.DeviceIdType.MESH
ANY = pl.BlockSpec(memory_space=pl.ANY)

ADAM_LR = 0.001
ADAM_B1 = 0.9
ADAM_B2 = 0.999
ADAM_EPS = 1e-08
ADAM_WD = 0.01
ADAM_STEP = 10

BIG = (
    ("w_in", (D, 5 * D), 1, 5 * D // N_CHIPS),
    ("w_pool", (4, PG, PG), 1, PG // N_CHIPS),
    ("w_branch_a", (D, D), 0, D // N_CHIPS),
    ("w_branch_b", (D, D), 0, D // N_CHIPS),
    ("w_out", (D, D), 0, D // N_CHIPS),
    ("w_up", (D, 2 * DFF), 1, 2 * DFF // N_CHIPS),
    ("w_down", (DFF, D), 0, DFF // N_CHIPS),
    ("w_ple_gate", (D, D), 0, D // N_CHIPS),
    ("w_ple", (PG, D), 1, D // N_CHIPS),
)
CONV_ROWS = 8


def _params(n_axes):
    return pltpu.CompilerParams(dimension_semantics=("arbitrary",) * n_axes, vmem_limit_bytes=VMEM_LIMIT)


def _gelu(x):
    return 0.5 * x * (1.0 + lax.erf(x * 0.7071067811865476))


def _gelu_grad(x):
    return 0.5 * (1.0 + lax.erf(x * 0.7071067811865476)) + x * jnp.exp(-0.5 * x * x) * 0.3989422804014327


def _shard_shape(shape, axis, size):
    return tuple(size if a == axis else s for a, s in enumerate(shape))


def _block(ref, axis, j, size):
    idx = tuple(pl.ds(j * size, size) if a == axis else slice(None) for a in range(len(ref.shape)))
    return ref.at[idx]


def mm_nn(a, w, l, *, name, rows, out_dtype=F32, resid=None, a_row_off=0, out=None, out_cols=None,
          out_col_off=0, norm_gain=None, host=None, tm=1024, tn=None, tk=None):
    K, N = w.shape[1], w.shape[2]
    tn = tn or N
    tk = tk or K
    nk = K // tk
    out_cols = out_cols or N
    assert rows % tm == 0 and N % tn == 0 and K % tk == 0 and out_col_off % tn == 0 and a_row_off % tm == 0
    has_resid, has_out, has_norm = resid is not None, out is not None, norm_gain is not None
    assert not has_norm or (tn == N and not has_out)
    grid = (N // tn, rows // tm, nk)
    hosting = _Hosting(host)
    n_in = 2 + has_resid + has_norm + has_out
    n_host_in, n_host_out = len(hosting.arrays), len(hosting.out_shapes)
    n_own_out = 1 + has_norm

    def body(*refs):
        refs = list(refs)
        a_ref, w_ref = refs[0], refs[1]
        r_ref = refs[2] if has_resid else None
        g_ref = refs[2 + has_resid] if has_norm else None
        host_in = refs[n_in:n_in + n_host_in]
        o_base = n_in + n_host_in
        o_ref = refs[o_base]
        host_out = refs[o_base + n_own_out:o_base + n_own_out + n_host_out]
        scratch = refs[o_base + n_own_out + n_host_out:]
        if hosting.plan:
            first, last = _first_last(grid)
            sems = scratch[-2:]

            @pl.when(first)
            def _():
                hosting.begin(host_in, host_out, *sems)

        part = jnp.dot(a_ref[...].astype(BF16), w_ref[...], preferred_element_type=F32)

        def finish(r):
            if has_resid:
                r = r + r_ref[...]
            o_ref[...] = r.astype(o_ref.dtype)
            if has_norm:
                scale = lax.rsqrt(jnp.mean(r * r, axis=-1, keepdims=True) + EPS)
                refs[o_base + 1][...] = (r * scale * g_ref[...]).astype(BF16)

        if nk == 1:
            finish(part)
        else:
            acc = scratch[0]
            k = pl.program_id(2)

            @pl.when(k == 0)
            def _():
                acc[...] = part

            @pl.when(k > 0)
            def _():
                acc[...] += part

            @pl.when(k == nk - 1)
            def _():
                finish(acc[...])

        if hosting.plan:
            @pl.when(last)
            def _():
                hosting.finish(host_in, host_out, *sems)

    in_specs = [pl.BlockSpec((tm, tk), lambda j, i, k: (i + a_row_off // tm, k)),
                pl.BlockSpec((None, tk, tn), lambda j, i, k: (l, k, j))]
    args = [a, w]
    if has_resid:
        in_specs.append(pl.BlockSpec((tm, tn), lambda j, i, k: (i, j)))
        args.append(resid)
    if has_norm:
        in_specs.append(pl.BlockSpec((None, 1, tn), lambda j, i, k: (l, 0, 0)))
        args.append(norm_gain)
    aliases = {}
    if has_out:
        in_specs.append(ANY)
        aliases = {len(args): 0}
        args.append(out)
    aliases.update(hosting.aliases(n_in, n_own_out))
    out_specs = [pl.BlockSpec((tm, tn), lambda j, i, k: (i, j + out_col_off // tn))]
    out_shape = [jax.ShapeDtypeStruct((rows, out_cols), out_dtype)]
    if has_norm:
        out_specs.append(pl.BlockSpec((tm, tn), lambda j, i, k: (i, j)))
        out_shape.append(jax.ShapeDtypeStruct((rows, N), BF16))
    res = pl.pallas_call(
        body, name=name, grid=grid,
        in_specs=in_specs + [ANY] * n_host_in,
        out_specs=out_specs + [ANY] * n_host_out,
        out_shape=out_shape + hosting.out_shapes,
        scratch_shapes=([pltpu.VMEM((tm, tn), F32)] if nk > 1 else []) + hosting.scratch(),
        input_output_aliases=aliases, compiler_params=_params(3))(*args, *hosting.arrays)
    own = res[0] if n_own_out == 1 else tuple(res[:n_own_out])
    return (own, list(res[n_own_out:])) if hosting.plan else own


def mm_nt(a, w, l, *, name, rows, kdim=None, a_col_off=0, out_dtype=BF16, norm_bwd_of=None, host=None, tm=1024,
          tn=None, tk=None):
    R = w.shape[1]
    kdim = kdim or w.shape[2]
    tn = tn or R
    tk = tk or kdim
    nk = kdim // tk
    assert rows % tm == 0 and R % tn == 0 and kdim % tk == 0 and a_col_off % tk == 0
    fused = norm_bwd_of is not None
    assert not fused or tn == R
    grid = (R // tn, rows // tm, nk)
    hosting = _Hosting(host)
    n_host_in, n_host_out = len(hosting.arrays), len(hosting.out_shapes)
    n_own_in, n_own_out = (3, 2) if fused else (0, 1)

    def body(a_ref, w_ref, *refs):
        host_in = refs[n_own_in:n_own_in + n_host_in]
        host_out = refs[n_own_in + n_host_in + n_own_out:n_own_in + n_host_in + n_own_out + n_host_out]
        scratch = refs[n_own_in + n_host_in + n_own_out + n_host_out:]
        rest = list(refs[:n_own_in]) + list(refs[n_own_in + n_host_in:n_own_in + n_host_in + n_own_out]) \
            + ([scratch[0]] if nk > 1 else [])
        if hosting.plan:
            first, last = _first_last(grid)
            sems = scratch[-2:]

            @pl.when(first)
            def _():
                hosting.begin(host_in, host_out, *sems)

        part = lax.dot_general(a_ref[...].astype(BF16), w_ref[...], (((1,), (1,)), ((), ())),
                               preferred_element_type=F32)
        i, k = pl.program_id(1), pl.program_id(2)

        def finish(dh):
            if not fused:
                rest[0][...] = dh.astype(rest[0].dtype)
                return
            x_ref, g_ref, dxi_ref, dx_ref, dg_ref = rest[:5]

            @pl.when(i == 0)
            def _():
                dg_ref[...] = jnp.zeros_like(dg_ref)

            xv = x_ref[...]
            r = lax.rsqrt(jnp.mean(xv * xv, axis=-1, keepdims=True) + EPS)
            xh = xv * r
            dhg = dh * g_ref[...]
            dx_ref[...] = dxi_ref[...] + r * (dhg - xh * jnp.mean(dhg * xh, axis=-1, keepdims=True))
            dg_ref[0:1, :] += jnp.sum(dh * xh, axis=0, keepdims=True)

        if nk == 1:
            finish(part)
        else:
            acc = rest[-1]

            @pl.when(k == 0)
            def _():
                acc[...] = part

            @pl.when(k > 0)
            def _():
                acc[...] += part

            @pl.when(k == nk - 1)
            def _():
                finish(acc[...])

        if hosting.plan:
            @pl.when(last)
            def _():
                hosting.finish(host_in, host_out, *sems)

    if a.ndim == 3:
        per = a.shape[2] // tk
        a_spec = pl.BlockSpec((None, tm, tk), lambda j, i, k: (k // per, i, k % per))
    else:
        a_spec = pl.BlockSpec((tm, tk), lambda j, i, k: (i, k + a_col_off // tk))
    in_specs = [a_spec, pl.BlockSpec((None, tn, tk), lambda j, i, k: (l, j, k))]
    args = [a, w]
    row_tile = pl.BlockSpec((tm, tn), lambda j, i, k: (i, j))
    if fused:
        x, gain, gl, dx_in = norm_bwd_of
        in_specs += [row_tile, pl.BlockSpec((None, 1, tn), lambda j, i, k: (gl, 0, 0)), row_tile]
        args += [x, gain, dx_in]
        out_specs = [row_tile, pl.BlockSpec((8, tn), lambda j, i, k: (0, 0))]
        out_shape = [jax.ShapeDtypeStruct((rows, R), F32), jax.ShapeDtypeStruct((8, R), F32)]
    else:
        out_specs, out_shape = [row_tile], [jax.ShapeDtypeStruct((rows, R), out_dtype)]
    res = pl.pallas_call(
        body, name=name, grid=grid, in_specs=in_specs + [ANY] * n_host_in,
        out_specs=out_specs + [ANY] * n_host_out, out_shape=out_shape + hosting.out_shapes,
        scratch_shapes=([pltpu.VMEM((tm, tn), F32)] if nk > 1 else []) + hosting.scratch(),
        input_output_aliases=hosting.aliases(2 + n_own_in, n_own_out), compiler_params=_params(3))(
            *args, *hosting.arrays)
    own = tuple(res[:n_own_out]) if fused else res[0]
    return (own, list(res[n_own_out:])) if hosting.plan else own


def mm_tn(a, b, *, name, rows, ka, nb, a_row_off=0, b_col_off=0, tm=None, tn=None, tk=2048):
    tm = tm or ka
    tn = tn or nb
    tk = min(tk, rows)
    nk = rows // tk
    assert ka % tm == 0 and nb % tn == 0 and rows % tk == 0 and b_col_off % tn == 0 and a_row_off % tk == 0

    def body(a_ref, b_ref, o_ref, acc):
        part = lax.dot_general(a_ref[...].astype(BF16), b_ref[...].astype(BF16), (((0,), (0,)), ((), ())),
                               preferred_element_type=F32)
        k = pl.program_id(2)

        @pl.when(k == 0)
        def _():
            acc[...] = part

        @pl.when(k > 0)
        def _():
            acc[...] += part

        @pl.when(k == nk - 1)
        def _():
            o_ref[...] = acc[...].astype(o_ref.dtype)

    if b.ndim == 3:
        per = b.shape[2] // tn
        b_spec = pl.BlockSpec((None, tk, tn), lambda j, i, k: (j // per, k, j % per))
    else:
        b_spec = pl.BlockSpec((tk, tn), lambda j, i, k: (k, j + b_col_off // tn))
    return pl.pallas_call(
        body, name=name, grid=(nb // tn, ka // tm, nk),
        in_specs=[pl.BlockSpec((tk, tm), lambda j, i, k: (k + a_row_off // tk, i)), b_spec],
        out_specs=pl.BlockSpec((None, tm, tn), lambda j, i, k: (0, i, j)),
        out_shape=jax.ShapeDtypeStruct((1, ka, nb), BF16),
        scratch_shapes=[pltpu.VMEM((tm, tn), F32)], compiler_params=_params(3))(a, b)


def _row_spec(tm, width, col=0):
    return pl.BlockSpec((tm, width), lambda i: (i, col))


def _gain_spec(l, width=D):
    return pl.BlockSpec((None, 1, width), lambda i: (l, 0, 0))


def norm_fwd(x, g3, l, *, name, tm=512):
    T = x.shape[0]

    def body(x_ref, g_ref, o_ref):
        xv = x_ref[...]
        r = lax.rsqrt(jnp.mean(xv * xv, axis=-1, keepdims=True) + EPS)
        o_ref[...] = (xv * r * g_ref[...]).astype(BF16)

    return pl.pallas_call(
        body, name=name, grid=(T // tm,),
        in_specs=[_row_spec(tm, D), _gain_spec(l)], out_specs=_row_spec(tm, D),
        out_shape=jax.ShapeDtypeStruct((T, D), BF16), compiler_params=_params(1))(x, g3)


def _winsum_back(ext, w):
    s, span = ext, 1
    while span < w:
        s = s + pltpu.roll(s, span, 0)
        span *= 2
    return s


def _winsum_fwd(ext, w):
    rows = ext.shape[0]
    s, span = ext, 1
    while span < w:
        s = s + pltpu.roll(s, rows - span, 0)
        span *= 2
    return s


def _pooled(ext, z, t, g, w):
    sl = slice(g * PG, (g + 1) * PG)
    s = _winsum_back(ext[:, sl], w)[POOL_HALO:, :]
    return s / jnp.minimum(t + 1, w).astype(F32) - z[:, sl]


def pool_fwd(z, wpool, scale3, l, *, name, tm=256):
    T = z.shape[0]
    hb = tm // POOL_HALO
    wl = l if wpool.shape[0] > 1 else 0

    def body(z_ref, zp_ref, wp_ref, sc_ref, o_ref):
        i = pl.program_id(0)
        zv = z_ref[...].astype(F32)
        prev = jnp.where(i == 0, 0.0, zp_ref[...].astype(F32))
        ext = jnp.concatenate([prev, zv], axis=0)
        t = i * tm + lax.broadcasted_iota(jnp.int32, (tm, 1), 0)
        for g, w in enumerate(POOL_WINDOWS):
            sl = slice(g * PG, (g + 1) * PG)
            pooled = _pooled(ext, zv, t, g, w)
            q = jnp.dot(pooled.astype(BF16), wp_ref[g], preferred_element_type=F32)
            o_ref[:, sl] = (q * sc_ref[:, sl]).astype(BF16)

    return pl.pallas_call(
        body, name=name, grid=(T // tm,),
        in_specs=[_row_spec(tm, D),
                  pl.BlockSpec((POOL_HALO, D), lambda i: (jnp.maximum(i * hb - 1, 0), 0)),
                  pl.BlockSpec((None, 4, PG, PG), lambda i: (wl, 0, 0, 0)),
                  _gain_spec(l)],
        out_specs=_row_spec(tm, D),
        out_shape=jax.ShapeDtypeStruct((T, D), BF16), compiler_params=_params(1))(z, z, wpool, scale3)


def sgu_fwd(z, g3, wsm, bT, l, *, name, tm=256):
    T = z.shape[0]

    def body(zu_ref, zv_ref, g_ref, ws_ref, b_ref, o_ref):
        gu = _gelu(zu_ref[...].astype(F32))
        gv = _gelu(zv_ref[...].astype(F32))
        rv = lax.rsqrt(jnp.mean(gv * gv, axis=-1, keepdims=True) + EPS)
        vn = (gv * rv * g_ref[...]).astype(BF16)
        for n in range(tm // CHUNK):
            r = slice(n * CHUNK, (n + 1) * CHUNK)
            for h in range(HEADS):
                cs = slice(h * CHUNK, (h + 1) * CHUNK)
                mixed = jnp.dot(ws_ref[h], vn[r, cs], preferred_element_type=F32) + b_ref[:, h:h + 1]
                o_ref[r, cs] = (gu[r, cs] * mixed).astype(BF16)

    return pl.pallas_call(
        body, name=name, grid=(T // tm,),
        in_specs=[_row_spec(tm, D, 1), _row_spec(tm, D, 2), _gain_spec(l),
                  pl.BlockSpec((None, HEADS, CHUNK, CHUNK), lambda i: (l, 0, 0, 0)),
                  pl.BlockSpec((None, CHUNK, HEADS), lambda i: (l, 0, 0))],
        out_specs=_row_spec(tm, D),
        out_shape=jax.ShapeDtypeStruct((T, D), BF16), compiler_params=_params(1))(z, z, g3, wsm, bT)


def gate_fwd(z, yab, *, name, tm=512):
    T = z.shape[0]

    def body(za_ref, zb_ref, y_ref, o_ref):
        ga = jax.nn.sigmoid(za_ref[...].astype(F32))
        gb = jax.nn.sigmoid(zb_ref[...].astype(F32))
        o_ref[...] = (ga * y_ref[:, :D].astype(F32) + gb * y_ref[:, D:].astype(F32)).astype(BF16)

    return pl.pallas_call(
        body, name=name, grid=(T // tm,),
        in_specs=[_row_spec(tm, D, 3), _row_spec(tm, D, 4), _row_spec(tm, 2 * D)],
        out_specs=_row_spec(tm, D),
        out_shape=jax.ShapeDtypeStruct((T, D), BF16), compiler_params=_params(1))(z, z, yab)


def _conv(ext, w_ref, b_ref):
    down1, down2 = pltpu.roll(ext, 1, 0), pltpu.roll(ext, 2, 0)
    c = b_ref[...] + w_ref[0:1, :] * down2
    c = c + w_ref[1:2, :] * down1
    return c + w_ref[2:3, :] * ext, down1, down2


def conv_fwd(up, convw, convb3, l, *, name, tm=256):
    T = up.shape[0]
    tc = CONV_TC
    nc = DFF // tc
    hb = tm // CONV_HALO

    def body(ua_ref, uap_ref, ub_ref, ubp_ref, wa_ref, wb_ref, ba_ref, bb_ref, o_ref):
        i = pl.program_id(1)

        def conv_of(u_ref, p_ref, w_ref, b_ref):
            ext = jnp.concatenate([jnp.where(i == 0, 0.0, p_ref[...]), u_ref[...]], axis=0)
            return _conv(ext, w_ref, b_ref)[0][CONV_HALO:, :]

        ca = conv_of(ua_ref, uap_ref, wa_ref, ba_ref)
        cb = conv_of(ub_ref, ubp_ref, wb_ref, bb_ref)
        o_ref[...] = (_gelu(ca) * cb).astype(BF16)

    def cur(off):
        return pl.BlockSpec((tm, tc), lambda j, i: (i, j + off))

    def prev(off):
        return pl.BlockSpec((CONV_HALO, tc), lambda j, i: (jnp.maximum(i * hb - 1, 0), j + off))

    def wspec(off):
        return pl.BlockSpec((None, CONV_ROWS, tc), lambda j, i: (l, 0, j + off))

    def bspec(off):
        return pl.BlockSpec((None, 1, tc), lambda j, i: (l, 0, j + off))

    return pl.pallas_call(
        body, name=name, grid=(nc, T // tm),
        in_specs=[cur(0), prev(0), cur(nc), prev(nc), wspec(0), wspec(nc), bspec(0), bspec(nc)],
        out_specs=pl.BlockSpec((tm, tc), lambda j, i: (i, j)),
        out_shape=jax.ShapeDtypeStruct((T, DFF), BF16),
        compiler_params=_params(2))(up, up, up, up, convw, convw, convb3, convb3)


def ple_fwd(x2, pg, e, g3, l, *, name, tm=512):
    T = x2.shape[0]
    has_norm = g3 is not None

    def body(x_ref, pg_ref, e_ref, *rest):
        xv = x_ref[...] + jax.nn.sigmoid(pg_ref[...].astype(F32)) * e_ref[...].astype(F32)
        if has_norm:
            g_ref, o_ref, h_ref = rest
            r = lax.rsqrt(jnp.mean(xv * xv, axis=-1, keepdims=True) + EPS)
            h_ref[...] = (xv * r * g_ref[...]).astype(BF16)
        else:
            o_ref, = rest
        o_ref[...] = xv

    x_shape = jax.ShapeDtypeStruct((T, D), F32)
    return pl.pallas_call(
        body, name=name, grid=(T // tm,),
        in_specs=[_row_spec(tm, D)] * 3 + ([_gain_spec(l)] if has_norm else []),
        out_specs=[_row_spec(tm, D)] * 2 if has_norm else _row_spec(tm, D),
        out_shape=[x_shape, jax.ShapeDtypeStruct((T, D), BF16)] if has_norm else x_shape,
        compiler_params=_params(1))(x2, pg, e, *([g3] if has_norm else []))


def loss_head(x, g3, tgt, *, name, tm=512):
    T = x.shape[0]

    def body(x_ref, g_ref, t_ref, loss_ref, dx_ref, dg_ref):
        @pl.when(pl.program_id(0) == 0)
        def _():
            loss_ref[...] = jnp.zeros_like(loss_ref)
            dg_ref[...] = jnp.zeros_like(dg_ref)

        xv, g = x_ref[...], g_ref[...]
        r = lax.rsqrt(jnp.mean(xv * xv, axis=-1, keepdims=True) + EPS)
        xh = xv * r
        err = xh * g - t_ref[...]
        loss_ref[...] += 0.5 * jnp.sum(jnp.mean(err * err, axis=-1, keepdims=True))
        dy = err * (1.0 / D)
        dyg = dy * g
        dx_ref[...] = r * (dyg - xh * jnp.mean(dyg * xh, axis=-1, keepdims=True))
        dg_ref[0:1, :] += jnp.sum(dy * xh, axis=0, keepdims=True)

    return pl.pallas_call(
        body, name=name, grid=(T // tm,),
        in_specs=[_row_spec(tm, D), pl.BlockSpec((1, D), lambda i: (0, 0)), _row_spec(tm, D)],
        out_specs=[pl.BlockSpec((8, LANES), lambda i: (0, 0)), _row_spec(tm, D),
                   pl.BlockSpec((8, D), lambda i: (0, 0))],
        out_shape=[jax.ShapeDtypeStruct((8, LANES), F32), jax.ShapeDtypeStruct((T, D), F32),
                   jax.ShapeDtypeStruct((8, D), F32)],
        compiler_params=_params(1))(x, g3, tgt)


def ple_bwd(dx, pg, e, *, name, tm=512):
    T = dx.shape[0]

    def body(dx_ref, pg_ref, e_ref, de_ref, dpg_ref):
        gate = jax.nn.sigmoid(pg_ref[...].astype(F32))
        dxv = dx_ref[...]
        de_ref[...] = (dxv * gate).astype(BF16)
        dpg_ref[...] = (dxv * e_ref[...].astype(F32) * gate * (1.0 - gate)).astype(BF16)

    return pl.pallas_call(
        body, name=name, grid=(T // tm,),
        in_specs=[_row_spec(tm, D)] * 3, out_specs=[_row_spec(tm, D)] * 2,
        out_shape=[jax.ShapeDtypeStruct((T, D), BF16)] * 2, compiler_params=_params(1))(dx, pg, e)


def conv_bwd(df, up, convw, convb3, l, *, name, host=None, tm=256):
    T = up.shape[0]
    tc = CONV_TC
    nc = DFF // tc
    hb = tm // CONV_HALO
    nt = T // tm
    rows = tm + 2 * CONV_HALO
    own = slice(CONV_HALO, CONV_HALO + tm)

    hosting = _Hosting(host)
    n_host_in, n_host_out = len(hosting.arrays), len(hosting.out_shapes)

    def body(df_ref, dfn_ref, ua_ref, uap_ref, uan_ref, ub_ref, ubp_ref, ubn_ref, wa_ref, wb_ref, ba_ref, bb_ref,
             *rest):
        host_in = rest[:n_host_in]
        dup_ref, dcw_ref, dcb_ref = rest[n_host_in:n_host_in + 3]
        host_out = rest[n_host_in + 3:n_host_in + 3 + n_host_out]
        sems = rest[n_host_in + 3 + n_host_out:]
        i = pl.program_id(1)
        if hosting.plan:
            first, last = _first_last((nc, nt))

            @pl.when(first)
            def _():
                hosting.begin(host_in, host_out, *sems)

        @pl.when(i == 0)
        def _():
            dcw_ref[...] = jnp.zeros_like(dcw_ref)
            dcb_ref[...] = jnp.zeros_like(dcb_ref)

        def ext_of(c_ref, p_ref, n_ref):
            return jnp.concatenate([jnp.where(i == 0, 0.0, p_ref[...]), c_ref[...],
                                    jnp.where(i == nt - 1, 0.0, n_ref[...])], axis=0)

        ea = ext_of(ua_ref, uap_ref, uan_ref)
        eb = ext_of(ub_ref, ubp_ref, ubn_ref)
        ca, ea1, ea2 = _conv(ea, wa_ref, ba_ref)
        cb, eb1, eb2 = _conv(eb, wb_ref, bb_ref)
        df_ext =jnp.concatenate([jnp.zeros((CONV_HALO, tc), F32), df_ref[...],
                                  jnp.where(i == nt - 1, 0.0, dfn_ref[...])], axis=0)
        cdf = 0.5 * (1.0 + lax.erf(ca * 0.7071067811865476))
        da = df_ext * cb * (cdf + ca * jnp.exp(-0.5 * ca * ca) * 0.3989422804014327)
        db = df_ext * (ca * cdf)

        def finish(h, dc, e, e1, e2, w_ref):
            dup = w_ref[2:3, :] * dc + w_ref[1:2, :] * pltpu.roll(dc, rows - 1, 0)
            dup = dup + w_ref[0:1, :] * pltpu.roll(dc, rows - 2, 0)
            dup_ref[h] = dup[own, :].astype(BF16)
            dco = dc[own, :]
            dcb_ref[h, 0:1, :] += jnp.sum(dco, axis=0, keepdims=True)
            dcw_ref[h, 0:1, :] += jnp.sum(dco * e2[own, :], axis=0, keepdims=True)
            dcw_ref[h, 1:2, :] += jnp.sum(dco * e1[own, :], axis=0, keepdims=True)
            dcw_ref[h, 2:3, :] += jnp.sum(dco * e[own, :], axis=0, keepdims=True)

        finish(0, da, ea, ea1, ea2, wa_ref)
        finish(1, db, eb, eb1, eb2, wb_ref)
        if hosting.plan:
            @pl.when(last)
            def _():
                hosting.finish(host_in, host_out, *sems)

    def nxt(i):
        return jnp.minimum((i + 1) * hb, T // CONV_HALO - 1)

    def prv(i):
        return jnp.maximum(i * hb - 1, 0)

    def up_specs(off):
        return [pl.BlockSpec((tm, tc), lambda j, i: (i, j + off)),
                pl.BlockSpec((CONV_HALO, tc), lambda j, i: (prv(i), j + off)),
                pl.BlockSpec((CONV_HALO, tc), lambda j, i: (nxt(i), j + off))]

    in_specs = [pl.BlockSpec((tm, tc), lambda j, i: (i, j)),
                pl.BlockSpec((CONV_HALO, tc), lambda j, i: (nxt(i), j)),
                *up_specs(0), *up_specs(nc),
                pl.BlockSpec((None, CONV_ROWS, tc), lambda j, i: (l, 0, j)),
                pl.BlockSpec((None, CONV_ROWS, tc), lambda j, i: (l, 0, j + nc)),
                pl.BlockSpec((None, 1, tc), lambda j, i: (l, 0, j)),
                pl.BlockSpec((None, 1, tc), lambda j, i: (l, 0, j + nc))]
    res = pl.pallas_call(
        body, name=name, grid=(nc, nt), in_specs=in_specs + [ANY] * n_host_in,
        out_specs=[pl.BlockSpec((2, tm, tc), lambda j, i: (0, i, j)),
                   pl.BlockSpec((2, 8, tc), lambda j, i: (0, 0, j)),
                   pl.BlockSpec((2, 8, tc), lambda j, i: (0, 0, j))] + [ANY] * n_host_out,
        out_shape=[jax.ShapeDtypeStruct((2, T, DFF), BF16), jax.ShapeDtypeStruct((2, 8, DFF), F32),
                   jax.ShapeDtypeStruct((2, 8, DFF), F32)] + hosting.out_shapes,
        scratch_shapes=hosting.scratch(), input_output_aliases=hosting.aliases(12, 3),
        compiler_params=_params(2))(df, df, up, up, up, up, up, up, convw, convw, convb3, convb3, *hosting.arrays)
    return res[0], res[1], res[2], list(res[3:])


def gate_bwd(dmo, z, yab, *, name, tm=512):
    T = z.shape[0]

    def body(dmo_ref, zg_ref, y_ref, dz_ref, dy_ref):
        g = jax.nn.sigmoid(zg_ref[...].astype(F32))
        dmo_v = dmo_ref[...].astype(F32)
        dy_ref[...] = (dmo_v * g).astype(BF16)
        dz_ref[...] = (dmo_v * y_ref[...].astype(F32) * g * (1.0 - g)).astype(BF16)

    return pl.pallas_call(
        body, name=name, grid=(T // tm, 2),
        in_specs=[pl.BlockSpec((tm, D), lambda i, s: (i, 0)),
                  pl.BlockSpec((tm, D), lambda i, s: (i, 3 + s)),
                  pl.BlockSpec((tm, D), lambda i, s: (i, s))],
        out_specs=[pl.BlockSpec((tm, D), lambda i, s: (i, 3 + s)),
                   pl.BlockSpec((tm, D), lambda i, s: (i, s))],
        out_shape=[jax.ShapeDtypeStruct((T, 5 * D), BF16), jax.ShapeDtypeStruct((T, 2 * D), BF16)],
        compiler_params=_params(2))(dmo, z, yab)


def mixer_bwd(da, ds, z, dz, wpool, scale3, g3, wsm, wsmT, bT, l, *, name, tm=256):
    T = z.shape[0]
    hb = tm // POOL_HALO
    nt = T // tm

    def body(da_ref, dan_ref, ds_ref, zp_ref, zpp_ref, zu_ref, zv_ref, wp_ref, sc_ref, g_ref, ws_ref, wst_ref,
             b_ref, dzin_ref, dz_ref, dwp_ref, dsc_ref, dws_ref, dbt_ref, dgs_ref, mixed_scr, dvn_scr, db_scr):
        del dzin_ref
        i = pl.program_id(0)

        @pl.when(i == 0)
        def _():
            dwp_ref[...] = jnp.zeros_like(dwp_ref)
            dsc_ref[...] = jnp.zeros_like(dsc_ref)
            dws_ref[...] = jnp.zeros_like(dws_ref)
            dgs_ref[...] = jnp.zeros_like(dgs_ref)
            db_scr[...] = jnp.zeros_like(db_scr)

        zv_p = zp_ref[...].astype(F32)
        ext = jnp.concatenate([jnp.where(i == 0, 0.0, zpp_ref[...].astype(F32)), zv_p], axis=0)
        da_v = da_ref[...].astype(F32)
        da_ext = jnp.concatenate([da_v, jnp.where(i == nt - 1, 0.0, dan_ref[...].astype(F32))], axis=0)
        t = i * tm + lax.broadcasted_iota(jnp.int32, (tm, 1), 0)
        t_ext = i * tm + lax.broadcasted_iota(jnp.int32, (tm + POOL_HALO, 1), 0)
        for g, w in enumerate(POOL_WINDOWS):
            sl = slice(g * PG, (g + 1) * PG)
            pooled = _pooled(ext, zv_p, t, g, w).astype(BF16)
            q = jnp.dot(pooled, wp_ref[g], preferred_element_type=F32)
            dsc_ref[0:1, sl] += jnp.sum(da_v[:, sl] * q, axis=0, keepdims=True)
            dq_ext = (da_ext[:, sl] * sc_ref[:, sl]).astype(BF16)
            dwp_ref[g] += lax.dot_general(pooled, dq_ext[:tm, :], (((0,), (0,)), ((), ())),
                                          preferred_element_type=F32)
            dpool = lax.dot_general(dq_ext, wp_ref[g], (((1,), (1,)), ((), ())), preferred_element_type=F32)
            spread = _winsum_fwd(dpool / jnp.minimum(t_ext + 1, w).astype(F32), w)
            dz_ref[:, sl] = (spread[:tm, :] - dpool[:tm, :]).astype(BF16)

        zu, zv, ds_v = zu_ref[...].astype(F32), zv_ref[...].astype(F32), ds_ref[...].astype(F32)
        gain = g_ref[...]
        gu, gv = _gelu(zu), _gelu(zv)
        rv = lax.rsqrt(jnp.mean(gv * gv, axis=-1, keepdims=True) + EPS)
        vh = gv * rv
        vn = (vh * gain).astype(BF16)
        dmix = ds_v * gu
        dmix_b = dmix.astype(BF16)
        for n in range(tm // CHUNK):
            r = slice(n * CHUNK, (n + 1) * CHUNK)
            db_scr[...] += dmix[r, :]
            for h in range(HEADS):
                cs = slice(h * CHUNK, (h + 1) * CHUNK)
                mixed_scr[r, cs] = jnp.dot(ws_ref[h], vn[r, cs], preferred_element_type=F32) + b_ref[:, h:h + 1]
                dws_ref[h] += lax.dot_general(dmix_b[r, cs], vn[r, cs], (((1,), (1,)), ((), ())),
                                              preferred_element_type=F32)
                dvn_scr[r, cs] = jnp.dot(wst_ref[h], dmix_b[r, cs], preferred_element_type=F32)
        dz_ref[:, D:2 * D] = (ds_v * mixed_scr[...] * _gelu_grad(zu)).astype(BF16)
        dvn = dvn_scr[...]
        dgs_ref[0:1, :] += jnp.sum(dvn * vh, axis=0, keepdims=True)
        dvg = dvn * gain
        dgv = rv * (dvg - vh * jnp.mean(dvg * vh, axis=-1, keepdims=True))
        dz_ref[:, 2 * D:3 * D] = (dgv * _gelu_grad(zv)).astype(BF16)

        @pl.when(i == nt - 1)
        def _():
            tril = (lax.broadcasted_iota(jnp.int32, (CHUNK, CHUNK), 0)
                    >= lax.broadcasted_iota(jnp.int32, (CHUNK, CHUNK), 1)).astype(F32)
            for h in range(HEADS):
                dws_ref[h] = dws_ref[h] * tril
                dbt_ref[:, h:h + 1] = jnp.sum(db_scr[:, h * CHUNK:(h + 1) * CHUNK], axis=1, keepdims=True)

    const4 = lambda i: (l, 0, 0, 0)
    wl = l if wpool.shape[0] > 1 else 0
    in_specs = [
        _row_spec(tm, D),
        pl.BlockSpec((POOL_HALO, D), lambda i: (jnp.minimum((i + 1) * hb, T // POOL_HALO - 1), 0)),
        _row_spec(tm, D),
        _row_spec(tm, D, 0),
        pl.BlockSpec((POOL_HALO, D), lambda i: (jnp.maximum(i * hb - 1, 0), 0)),
        _row_spec(tm, D, 1), _row_spec(tm, D, 2),
        pl.BlockSpec((None, 4, PG, PG), lambda i: (wl, 0, 0, 0)),
        _gain_spec(l), _gain_spec(l),
        pl.BlockSpec((None, HEADS, CHUNK, CHUNK), const4),
        pl.BlockSpec((None, HEADS, CHUNK, CHUNK), const4),
        pl.BlockSpec((None, CHUNK, HEADS), lambda i: (l, 0, 0)),
        ANY,
    ]
    out_specs = [
        pl.BlockSpec((tm, 3 * D), lambda i: (i, 0)),
        pl.BlockSpec((4, PG, PG), lambda i: (0, 0, 0)),
        pl.BlockSpec((8, D), lambda i: (0, 0)),
        pl.BlockSpec((HEADS, CHUNK, CHUNK), lambda i: (0, 0, 0)),
        pl.BlockSpec((CHUNK, HEADS), lambda i: (0, 0)),
        pl.BlockSpec((8, D), lambda i: (0, 0)),
    ]
    out_shape = [
        jax.ShapeDtypeStruct((T, 5 * D), BF16), jax.ShapeDtypeStruct((4, PG, PG), F32),
        jax.ShapeDtypeStruct((8, D), F32), jax.ShapeDtypeStruct((HEADS, CHUNK, CHUNK), F32),
        jax.ShapeDtypeStruct((CHUNK, HEADS), F32), jax.ShapeDtypeStruct((8, D), F32),
    ]
    return pl.pallas_call(
        body, name=name, grid=(nt,), in_specs=in_specs, out_specs=out_specs, out_shape=out_shape,
        scratch_shapes=[pltpu.VMEM((tm, D), F32), pltpu.VMEM((tm, D), F32), pltpu.VMEM((CHUNK, D), F32)],
        input_output_aliases={13: 0}, compiler_params=_params(1))(
            da, da, ds, z, z, z, z, wpool, scale3, g3, wsm, wsmT, bT, dz)


def _row_tile(rows, cols, sub):
    cap = max(sub, (2 * 1024 * 1024) // (4 * cols))
    best = None
    for tr in range(sub, min(rows, cap) + 1, sub):
        if rows % tr == 0:
            best = tr
    return best or rows


def elementwise(fn, ins, out_dtypes, *, name, row_blk_offs=None, rows=None):
    cols = ins[0].shape[1]
    rows = rows or ins[0].shape[0]
    tr = _row_tile(rows, cols, 16)
    offs = row_blk_offs or [0] * len(ins)
    n_in = len(ins)

    def body(*refs):
        outs = fn(*[r[...] for r in refs[:n_in]])
        for o_ref, o in zip(refs[n_in:], outs):
            o_ref[...] = o.astype(o_ref.dtype)

    return pl.pallas_call(
        body, name=name, grid=(rows // tr,),
        in_specs=[pl.BlockSpec((tr, cols), functools.partial(lambda i, o: (i + o * (rows // tr), 0), o=o))
                  for o in offs],
        out_specs=[pl.BlockSpec((tr, cols), lambda i: (i, 0)) for _ in out_dtypes],
        out_shape=[jax.ShapeDtypeStruct((rows, cols), dt) for dt in out_dtypes],
        compiler_params=_params(1))(*ins)


def _adamw(w, g, m, v):
    m = ADAM_B1 * m + (1.0 - ADAM_B1) * g
    v = ADAM_B2 * v + (1.0 - ADAM_B2) * jnp.square(g)
    m_hat = m / (1.0 - ADAM_B1 ** ADAM_STEP)
    v_hat = v / (1.0 - ADAM_B2 ** ADAM_STEP)
    delta = -ADAM_LR * (m_hat / (jnp.sqrt(v_hat) + ADAM_EPS) + ADAM_WD * w)
    return delta, m, v


def _view2d(a):
    return a.reshape(-1, a.shape[-1])


def _place():
    x, y, c = lax.axis_index("x"), lax.axis_index("y"), lax.axis_index("c")
    others = [(1 - x, y), (x, 1 - y), (1 - x, 1 - y)]
    return x, y, c, 2 * x + y, others


def _remote(src, dst, send_sems, recv_sems, k, to):
    return pltpu.make_async_remote_copy(src_ref=src, dst_ref=dst, send_sem=send_sems.at[k], recv_sem=recv_sems.at[k],
                                        device_id=to, device_id_type=MESH)


def _half(ref, axis, j, size, h):
    if len(ref.shape) == 3:
        return ref.at[:, pl.ds(j * size + h * (size // 2), size // 2), :]
    if axis == 0:
        return ref.at[pl.ds(j * size + h * (size // 2), size // 2), :]
    rows = ref.shape[0] // 2
    return ref.at[pl.ds(h * rows, rows), pl.ds(j * size, size)]


def _half_shard_shape(shape, axis, size):
    if len(shape) == 3:
        return (shape[0], size // 2, shape[2])
    if axis == 0:
        return (size // 2, shape[1])
    return (shape[0] // 2, size)


class Exchange:
    def __init__(self, arrays, out_shapes, aliases, n_sems, begin, finish):
        self.arrays, self.out_shapes, self.aliases, self.n_sems = list(arrays), list(out_shapes), aliases, n_sems
        self.begin, self.finish = begin, finish


class _Hosting:
    def __init__(self, plan):
        self.plan = list(plan or [])
        self.arrays = [a for ex in self.plan for a in ex.arrays]
        self.out_shapes = [o for ex in self.plan for o in ex.out_shapes]
        self.n_sems = sum(ex.n_sems for ex in self.plan)

    def scratch(self):
        return [pltpu.SemaphoreType.DMA((self.n_sems,)), pltpu.SemaphoreType.DMA((self.n_sems,))] if self.plan else []

    def aliases(self, in_base, out_base):
        out, i0, o0 = {}, in_base, out_base
        for ex in self.plan:
            out.update({i0 + i: o0 + o for i, o in ex.aliases.items()})
            i0, o0 = i0 + len(ex.arrays), o0 + len(ex.out_shapes)
        return out

    def _each(self, in_refs, out_refs):
        i0 = o0 = s0 = 0
        for ex in self.plan:
            yield ex, in_refs[i0:i0 + len(ex.arrays)], out_refs[o0:o0 + len(ex.out_shapes)], s0
            i0, o0, s0 = i0 + len(ex.arrays), o0 + len(ex.out_shapes), s0 + ex.n_sems

    def begin(self, in_refs, out_refs, send_sems, recv_sems):
        for ex, ins, outs, s0 in self._each(in_refs, out_refs):
            ex.begin(ins, outs, send_sems, recv_sems, s0)

    def finish(self, in_refs, out_refs, send_sems, recv_sems):
        for ex, ins, outs, s0 in self._each(in_refs, out_refs):
            ex.finish(ins, outs, send_sems, recv_sems, s0)


def _first_last(grid):
    ids = [pl.program_id(a) for a in range(len(grid))]
    first = functools.reduce(jnp.logical_and, [i == 0 for i in ids])
    last = functools.reduce(jnp.logical_and, [i == g - 1 for i, g in zip(ids, grid)])
    return first, last


def run_exchanges(plan, *, name):
    host = _Hosting(plan)
    n_in, n_out = len(host.arrays), len(host.out_shapes)

    def body(*refs):
        ins, outs = refs[:n_in], refs[n_in:n_in + n_out]
        send_sems, recv_sems = refs[n_in + n_out:]
        host.begin(ins, outs, send_sems, recv_sems)
        host.finish(ins, outs, send_sems, recv_sems)

    return pl.pallas_call(
        body, name=name, in_specs=[ANY] * n_in, out_specs=[ANY] * n_out, out_shape=host.out_shapes,
        scratch_shapes=host.scratch(), input_output_aliases=host.aliases(0, 0),
        compiler_params=pltpu.CompilerParams(has_side_effects=True))(*host.arrays)


def place_shard(src, l, axis, size, out_dtype, place, *, name):
    shard = src.shape[1:]
    natural = tuple(size * N_CHIPS if a == axis else s for a, s in enumerate(shard))
    if len(shard) == 3:
        blk = (None,) + shard
        grid = (1,)
        in_map = lambda i, pr: (l, 0, 0, 0)
        out_map = lambda i, pr: (0, 0, pr[0], 0)
    else:
        tr = _row_tile(shard[0], shard[1], 16)
        steps = shard[0] // tr
        blk = (None, tr, shard[1])
        grid = (steps,)
        in_map = lambda i, pr: (l, i, 0)
        if axis == 0:
            out_map = lambda i, pr: (0, pr[0] * steps + i, 0)
        else:
            out_map = lambda i, pr: (0, i, pr[0])

    def body(pr_ref, s_ref, o_ref):
        del pr_ref
        o_ref[...] = s_ref[...].astype(o_ref.dtype)

    return pl.pallas_call(
        body, name=name,
        grid_spec=pltpu.PrefetchScalarGridSpec(
            num_scalar_prefetch=1, grid=grid, in_specs=[pl.BlockSpec(blk, in_map)],
            out_specs=pl.BlockSpec(blk, out_map)),
        out_shape=jax.ShapeDtypeStruct((1,) + natural, out_dtype), compiler_params=_params(1))(place, src)


def place_both_layers(src, axis, size, place, *, name):
    rows, cols = src.shape[1], src.shape[2]

    def body(pr_ref, s_ref, o_ref):
        del pr_ref
        o_ref[...] = s_ref[...]

    return pl.pallas_call(
        body, name=name,
        grid_spec=pltpu.PrefetchScalarGridSpec(
            num_scalar_prefetch=1, grid=(2,), in_specs=[pl.BlockSpec((None, rows, cols), lambda lyr, pr: (lyr, 0, 0))],
            out_specs=pl.BlockSpec((None, rows, cols), lambda lyr, pr: (lyr, 0, pr[0]))),
        out_shape=jax.ShapeDtypeStruct((2, rows, cols * N_CHIPS), src.dtype), compiler_params=_params(1))(place, src)


def gather_exchange(arrays, geom):
    n = len(arrays)

    def begin(ins, outs, send_sems, recv_sems, s0):
        x, y, c, j, others = _place()
        for t, (axis, size) in enumerate(geom):
            mine = _half(outs[t].at[0], axis, j, size, c)
            for k, (ox, oy) in enumerate(others):
                _remote(mine, mine, send_sems, recv_sems, s0 + 6 * t + k, (ox, oy, c)).start()

    def finish(ins, outs, send_sems, recv_sems, s0):
        x, y, c, j, others = _place()
        sib = (x, y, 1 - c)
        passed = []
        for t, (axis, size) in enumerate(geom):
            for k, (ox, oy) in enumerate(others):
                landed = _half(outs[t].at[0], axis, 2 * ox + oy, size, c)
                _remote(landed, landed, send_sems, recv_sems, s0 + 6 * t + k, (ox, oy, c)).wait_recv()
                fwd = _remote(landed, landed, send_sems, recv_sems, s0 + 6 * t + 3 + k, sib)
                fwd.start()
                passed.append(fwd)
        for t, (axis, size) in enumerate(geom):
            for k, (ox, oy) in enumerate(others):
                got = _half(outs[t].at[0], axis, 2 * ox + oy, size, 1 - c)
                _remote(got, got, send_sems, recv_sems, s0 + 6 * t + 3 + k, sib).wait_recv()
        for fwd in passed:
            fwd.wait_send()
        for t, (axis, size) in enumerate(geom):
            mine = _half(outs[t].at[0], axis, j, size, c)
            for k, (ox, oy) in enumerate(others):
                _remote(mine, mine, send_sems, recv_sems, s0 + 6 * t + k, (ox, oy, c)).wait_send()

    return Exchange(arrays, [jax.ShapeDtypeStruct(a.shape, a.dtype) for a in arrays], {t: t for t in range(n)},
                    6 * n, begin, finish)


def gather_by_layer_exchange(array, axis, size):
    def blocks(out, others, lyr):
        return [_block(out.at[lyr], axis, 2 * ox + oy, size) for (ox, oy) in others]

    def begin(ins, outs, send_sems, recv_sems, s0):
        x, y, c, j, others = _place()
        mine = _block(outs[0].at[c], axis, j, size)
        for k, (ox, oy) in enumerate(others):
            _remote(mine, mine, send_sems, recv_sems, s0 + k, (ox, oy, c)).start()

    def finish(ins, outs, send_sems, recv_sems, s0):
        x, y, c, j, others = _place()
        sib = (x, y, 1 - c)
        passed = []
        for k, ((ox, oy), landed) in enumerate(zip(others, blocks(outs[0], others, c))):
            _remote(landed, landed, send_sems, recv_sems, s0 + k, (ox, oy, c)).wait_recv()
            fwd = _remote(landed, landed, send_sems, recv_sems, s0 + 3 + k, sib)
            fwd.start()
            passed.append(fwd)
        for k, got in enumerate(blocks(outs[0], others, 1 - c)):
            _remote(got, got, send_sems, recv_sems, s0 + 3 + k, sib).wait_recv()
        for fwd in passed:
            fwd.wait_send()
        mine = _block(outs[0].at[c], axis, j, size)
        for k, (ox, oy) in enumerate(others):
            _remote(mine, mine, send_sems, recv_sems, s0 + k, (ox, oy, c)).wait_send()

    return Exchange([array], [jax.ShapeDtypeStruct(array.shape, array.dtype)], {0: 0}, 6, begin, finish)


def swap_exchange(grads, geom):
    def pieces(t, g, dst, h):
        axis, size = geom[t]
        if len(g.shape) == 2 and axis == 1:
            rows = g.shape[0] // 2
            return [(g.at[pl.ds(h * rows, rows), :], dst)]
        return [(_half(g, axis, jb, size, h), dst.at[jb]) for jb in range(N_CHIPS)]

    counts = [1 if (len(g.shape) == 3 and a == 1) else N_CHIPS for g, (a, _) in zip(grads, geom)]
    bases = [sum(counts[:t]) for t in range(len(grads))]

    def copies(ins, outs, send_sems, recv_sems, s0):
        x, y, c, _, _ = _place()
        cps = []
        for t in range(len(grads)):
            for q, (src, dst) in enumerate(pieces(t, ins[t].at[0], outs[t], 1 - c)):
                cps.append(_remote(src, dst, send_sems, recv_sems, s0 + bases[t] + q, (x, y, 1 - c)))
        return cps

    def begin(*a):
        for cp in copies(*a):
            cp.start()

    def finish(*a):
        for cp in copies(*a):
            cp.wait()

    out_shapes = []
    for g, (axis, size) in zip(grads, geom):
        shp = g.shape[1:]
        if len(shp) == 2 and axis == 1:
            out_shapes.append(jax.ShapeDtypeStruct((shp[0] // 2, shp[1]), g.dtype))
        else:
            out_shapes.append(jax.ShapeDtypeStruct((N_CHIPS,) + _half_shard_shape(shp, axis, size), g.dtype))
    return Exchange(grads, out_shapes, {}, sum(counts), begin, finish)


def scatter_exchange(parts, geom, shapes):
    def copies(ins, outs, send_sems, recv_sems, s0):
        x, y, c, j, others = _place()
        cps = []
        for t, ((axis, size), shp) in enumerate(zip(geom, shapes)):
            for k, (ox, oy) in enumerate(others):
                jp = 2 * ox + oy
                src = ins[t].at[:, pl.ds(jp * size, size)] if (len(shp) == 2 and axis == 1) else ins[t].at[jp]
                cps.append(_remote(src, outs[t].at[k], send_sems, recv_sems, s0 + 3 * t + k, (ox, oy, c)))
        return cps

    def begin(*a):
        for cp in copies(*a):
            cp.start()

    def finish(*a):
        for cp in copies(*a):
            cp.wait_recv()
        for cp in copies(*a):
            cp.wait_send()

    out_shapes = [jax.ShapeDtypeStruct((3,) + _half_shard_shape(shp, axis, size), p.dtype)
                  for p, (axis, size), shp in zip(parts, geom, shapes)]
    return Exchange(parts, out_shapes, {}, 3 * len(parts), begin, finish)


def share_exchange(grads):
    n = 2 * len(grads)

    def my_half(refs, t, h):
        lyr = refs[t // 2].at[t % 2]
        if len(lyr.shape) == 3:
            rows = lyr.shape[1] // 2
            return lyr.at[:, pl.ds(h * rows, rows), :]
        rows = lyr.shape[0] // 2
        return lyr.at[pl.ds(h * rows, rows), :]

    def begin(ins, outs, send_sems, recv_sems, s0):
        x, y, c, _, _ = _place()
        for t in range(n):
            mine = my_half(outs, t, c)
            _remote(mine, mine, send_sems, recv_sems, s0 + t, (x, y, 1 - c)).start()

    def finish(ins, outs, send_sems, recv_sems, s0):
        x, y, c, _, _ = _place()
        for t in range(n):
            got = my_half(outs, t, 1 - c)
            _remote(got, got, send_sems, recv_sems, s0 + t, (x, y, 1 - c)).wait_recv()
        for t in range(n):
            mine = my_half(outs, t, c)
            _remote(mine, mine, send_sems, recv_sems, s0 + t, (x, y, 1 - c)).wait_send()

    return Exchange(grads, [jax.ShapeDtypeStruct(g.shape, g.dtype) for g in grads],
                    {t: t for t in range(len(grads))}, n, begin, finish)


def all_reduce_small(s):
    rows = s.shape[0]
    half = rows // 2
    assert half % 8 == 0

    def body(s_ref, o_ref, a_ref, b_ref, p_ref, send_sems, recv_sems):
        x, y, c, j, others = _place()
        sib = (x, y, 1 - c)
        swap = _remote(s_ref, a_ref, send_sems, recv_sems, 0, sib)
        swap.start()
        swap.wait()
        p_ref[...] = s_ref[...] + a_ref[...]
        mine = pl.ds(pl.multiple_of(c * half, 8), half)
        b_ref[j] = p_ref[mine, :]
        cps = [_remote(p_ref.at[mine, :], b_ref.at[j], send_sems, recv_sems, 1 + k, (ox, oy, c))
               for k, (ox, oy) in enumerate(others)]
        for cp in cps:
            cp.start()
        for k, (ox, oy) in enumerate(others):
            slot = b_ref.at[2 * ox + oy]
            _remote(slot, slot, send_sems, recv_sems, 1 + k, (ox, oy, c)).wait_recv()
        for cp in cps:
            cp.wait_send()
        o_ref[mine, :] = ((b_ref[0] + b_ref[1]) + b_ref[2]) + b_ref[3]
        back = _remote(o_ref.at[mine, :], o_ref.at[mine, :], send_sems, recv_sems, 4, sib)
        back.start()
        back.wait_send()
        theirs = pl.ds(pl.multiple_of((1 - c) * half, 8), half)
        _remote(o_ref.at[theirs, :], o_ref.at[theirs, :], send_sems, recv_sems, 4, sib).wait_recv()

    vmem = pl.BlockSpec(memory_space=pltpu.VMEM)
    return pl.pallas_call(
        body, name="all_reduce_small", in_specs=[vmem], out_specs=vmem,
        out_shape=jax.ShapeDtypeStruct((rows, LANES), F32),
        scratch_shapes=[pltpu.VMEM((rows, LANES), F32), pltpu.VMEM((N_CHIPS, half, LANES), F32),
                        pltpu.VMEM((rows, LANES), F32), pltpu.SemaphoreType.DMA((5,)),
                        pltpu.SemaphoreType.DMA((5,))],
        compiler_params=pltpu.CompilerParams(vmem_limit_bytes=VMEM_LIMIT, has_side_effects=True))(s)


def pair_sum(g, got, axis, size, place, *, name):
    shp = g.shape[1:]
    if len(shp) == 3:
        hs = size // 2
        grid = (N_CHIPS,)
        g_spec = pl.BlockSpec((None, shp[0], hs, shp[2]), lambda jb, pr: (0, 0, 2 * jb + pr[1], 0))
        r_spec = pl.BlockSpec((None, shp[0], hs, shp[2]), lambda jb, pr: (jb, 0, 0, 0))
    elif axis == 0:
        hs = size // 2
        tr = _row_tile(hs, shp[1], 16)
        steps = hs // tr
        grid = (N_CHIPS, steps)
        g_spec = pl.BlockSpec((None, tr, shp[1]), lambda jb, i, pr: (0, (2 * jb + pr[1]) * steps + i, 0))
        r_spec = pl.BlockSpec((None, tr, shp[1]), lambda jb, i, pr: (jb, i, 0))
    else:
        rows = shp[0] // 2
        tr = _row_tile(rows, shp[1], 16)
        steps = rows // tr
        grid = (steps,)
        g_spec = pl.BlockSpec((None, tr, shp[1]), lambda i, pr: (0, pr[1] * steps + i, 0))
        r_spec = pl.BlockSpec((tr, shp[1]), lambda i, pr: (i, 0))

    def body(pr_ref, g_ref, r_ref, o_ref):
        del pr_ref
        o_ref[...] = (g_ref[...].astype(F32) + r_ref[...].astype(F32)).astype(BF16)

    return pl.pallas_call(
        body, name=name,
        grid_spec=pltpu.PrefetchScalarGridSpec(num_scalar_prefetch=1, grid=grid, in_specs=[g_spec, r_spec],
                                               out_specs=r_spec),
        out_shape=jax.ShapeDtypeStruct(got.shape, BF16), compiler_params=_params(len(grid)))(place, g, got)


def chip_sum(part, slots, shp, axis, size, l, place, out, *, name):
    shard = _shard_shape(shp, axis, size)
    hshape = slots.shape[1:]
    if len(shp) == 3:
        grid = (1,)
        p_spec = pl.BlockSpec((None,) + hshape, lambda i, pr: (pr[0], 0, 0, 0))
        s_specs = [pl.BlockSpec((None,) + hshape, functools.partial(lambda i, pr, k: (k, 0, 0, 0), k=k))
                   for k in range(3)]
        o_spec = pl.BlockSpec((None,) + hshape, lambda i, pr: (l, 0, pr[1], 0))
    else:
        tr = _row_tile(hshape[0], hshape[1], 16)
        steps = hshape[0] // tr
        grid = (steps,)
        if axis == 0:
            p_spec = pl.BlockSpec((None, tr, hshape[1]), lambda i, pr: (pr[0], i, 0))
        else:
            p_spec = pl.BlockSpec((tr, hshape[1]), lambda i, pr: (i, pr[0]))
        s_specs = [pl.BlockSpec((None, tr, hshape[1]), functools.partial(lambda i, pr, k: (k, i, 0), k=k))
                   for k in range(3)]
        o_spec = pl.BlockSpec((None, tr, hshape[1]), lambda i, pr: (l, pr[1] * steps + i, 0))
    has_out = out is not None

    def body(pr_ref, p_ref, s0_ref, s1_ref, s2_ref, *rest):
        del pr_ref
        rest[-1][...] = ((p_ref[...].astype(F32) + s0_ref[...].astype(F32)) + s1_ref[...].astype(F32)) \
            + s2_ref[...].astype(F32)

    return pl.pallas_call(
        body, name=name,
        grid_spec=pltpu.PrefetchScalarGridSpec(
            num_scalar_prefetch=1, grid=grid, in_specs=[p_spec] + s_specs + ([ANY] if has_out else []),
            out_specs=o_spec),
        out_shape=jax.ShapeDtypeStruct((2,) + shard, F32), input_output_aliases={5: 0} if has_out else {},
        compiler_params=_params(1))(place, part, slots, slots, slots, *([out] if has_out else []))


GEOM = {name: (axis, size) for (name, _, axis, size) in BIG}
SHAPE = {name: shape for (name, shape, _, _) in BIG}
RIDES_IN_PROJ_L0 = ((0, ("w_pool", "w_branch_a", "w_branch_b", "w_out", "w_up")),)
RIDES_UP_PROJ_L0 = ((0, ("w_down", "w_ple_gate", "w_ple")), (1, ("w_in",)))
RIDES_DOWN_PROJ_L0 = ((1, ("w_pool", "w_branch_a", "w_branch_b", "w_out")),)
RIDES_IN_PROJ_L1 = ((1, ("w_up", "w_down", "w_ple_gate", "w_ple")),)
EARLY_GRADS_L0 = ("w_ple", "w_ple_gate", "w_down", "w_up")
LATE_GRADS_L0 = ("w_out", "w_branch_a", "w_branch_b", "w_pool", "w_in")


def _reduce_start(G, names, place, tag):
    geom = [GEOM[k] for k in names]
    got = run_exchanges([swap_exchange([G[k] for k in names], geom)], name=f"swap_halves_{tag}")
    parts = [pair_sum(G[k], r, *GEOM[k], place, name=f"pair_sum_{k}_{tag}") for k, r in zip(names, got)]
    return scatter_exchange(parts, geom, [SHAPE[k] for k in names]), parts


def _reduce_end(names, parts, slots, place, l, reduced):
    for k, q, s in zip(names, parts, slots):
        reduced[k] = chip_sum(q, s, SHAPE[k], *GEOM[k], l, place, reduced.get(k), name=f"chip_sum_{k}_l{l}")


def _local_step(x, p2, tgt, W0, W1, conv_w, small, place):
    T = x.shape[0]
    as3 = lambda a: a.reshape(2, 1, a.shape[-1])
    mix3, scale3, sgu3 = as3(small["mix_norm"]), as3(small["pool_scale"]), as3(small["sgu_norm"])
    ffn3, ple3, convb3 = as3(small["ffn_norm"]), as3(small["ple_norm"]), as3(small["conv_b"])
    tril = jnp.tril(jnp.ones((CHUNK, CHUNK), F32))
    ws_masked = small["w_spatial"] * tril
    wsm = ws_masked.astype(BF16)
    wsmT = jnp.swapaxes(ws_masked, -1, -2).astype(BF16)
    bT = jnp.swapaxes(small["b_spatial"], -1, -2)
    final3 = small["final_norm"].reshape(1, D)
    W = [dict(W0), dict(W1)]

    def riders(groups):
        return [gather_exchange([W[lyr][k] for k in names], [GEOM[k] for k in names]) for lyr, names in groups]

    def landed(groups, got):
        for lyr, names in groups:
            W[lyr].update(zip(names, got[:len(names)]))
            got = got[len(names):]

    saved = []
    hb = norm_fwd(x, mix3, 0, name="mix_norm_fwd_l0")
    for l in range(2):
        n = lambda s: f"{s}_l{l}"
        Wl = W[l]
        groups = RIDES_IN_PROJ_L0 if l == 0 else RIDES_IN_PROJ_L1
        z, got = mm_nn(hb, Wl["w_in"], 0, name=n("in_proj"), rows=T, tn=1280, out_dtype=BF16, host=riders(groups))
        landed(groups, got)
        a_in = pool_fwd(z, Wl["w_pool"], scale3, l, name=n("pool_fwd"))
        s_in = sgu_fwd(z, sgu3, wsm, bT, l, name=n("sgu_fwd"))
        yab = mm_nn(a_in, Wl["w_branch_a"], 0, name=n("branch_a"), rows=T, out_cols=2 * D, out_dtype=BF16)
        yab = mm_nn(s_in, Wl["w_branch_b"], 0, name=n("branch_b"), rows=T, out=yab, out_cols=2 * D, out_col_off=D,
                    out_dtype=BF16)
        mo = gate_fwd(z, yab, name=n("gate_fwd"))
        x1, h2b = mm_nn(mo, Wl["w_out"], 0, name=n("out_proj"), rows=T, resid=x, norm_gain=ffn3[l:l + 1])
        if l == 0:
            up, got = mm_nn(h2b, Wl["w_up"], 0, name=n("up_proj"), rows=T, tn=DFF, host=riders(RIDES_UP_PROJ_L0))
            landed(RIDES_UP_PROJ_L0, got)
        else:
            up = mm_nn(h2b, Wl["w_up"], 0, name=n("up_proj"), rows=T, tn=DFF)
        f = conv_fwd(up, conv_w, convb3, l, name=n("conv_fwd"))
        if l == 0:
            (x2, h3b), got = mm_nn(f, Wl["w_down"], 0, name=n("down_proj"), rows=T, resid=x1,
                                   norm_gain=ple3[l:l + 1], host=riders(RIDES_DOWN_PROJ_L0))
            landed(RIDES_DOWN_PROJ_L0, got)
        else:
            x2, h3b = mm_nn(f, Wl["w_down"], 0, name=n("down_proj"), rows=T, resid=x1, norm_gain=ple3[l:l + 1])
        pg = mm_nn(h3b, Wl["w_ple_gate"], 0, name=n("ple_gate_proj"), rows=T, out_dtype=BF16)
        e = mm_nn(p2, Wl["w_ple"], 0, name=n("ple_proj"), rows=T, a_row_off=l * T, out_dtype=BF16)
        saved.append(dict(x=x, hb=hb, z=z, a_in=a_in, s_in=s_in, yab=yab, mo=mo, x1=x1, h2b=h2b, up=up, f=f,
                          x2=x2, h3b=h3b, pg=pg, e=e))
        if l == 0:
            x, hb = ple_fwd(x2, pg, e, mix3, 1, name=n("ple_fwd"))
        else:
            x = ple_fwd(x2, pg, e, None, 0, name=n("ple_fwd"))

    loss_acc, dx, dg_final = loss_head(x, final3, tgt, name="loss_head")

    small_grads = [None, None]
    all_names = [t[0] for t in BIG]
    scatter1 = parts1 = slots1 = None
    for l in (1, 0):
        n = lambda s: f"{s}_l{l}"
        a, Wl, G = saved[l], W[l], {}
        de, dpg = ple_bwd(dx, a["pg"], a["e"], name=n("ple_bwd"))
        G["w_ple"] = mm_tn(p2, de, name=n("d_w_ple"), rows=T, ka=PG, nb=D, a_row_off=l * T)
        G["w_ple_gate"] = mm_tn(a["h3b"], dpg, name=n("d_w_ple_gate"), rows=T, ka=D, nb=D)
        dx2, dg_ple = mm_nt(dpg, Wl["w_ple_gate"], 0, name=n("ple_norm_bwd"), rows=T,
                            norm_bwd_of=(a["x2"], ple3, l, dx))
        df = mm_nt(dx2, Wl["w_down"], 0, name=n("d_ffn_act"), rows=T, out_dtype=F32)
        G["w_down"] = mm_tn(a["f"], dx2, name=n("d_w_down"), rows=T, ka=DFF, nb=D, tm=1408)
        if l == 0:
            dup, dcw, dcb, slots1 = conv_bwd(df, a["up"], conv_w, convb3, l, name=n("conv_bwd"), host=[scatter1])
        else:
            dup, dcw, dcb, _ = conv_bwd(df, a["up"], conv_w, convb3, l, name=n("conv_bwd"))
        G["w_up"] = mm_tn(a["h2b"], dup, name=n("d_w_up"), rows=T, ka=D, nb=2 * DFF, tn=DFF, tk=1024)
        if l == 0:
            scatter_early, parts_early = _reduce_start(G, EARLY_GRADS_L0, place, "l0_early")
            (dx1, dg_ffn), slots_early = mm_nt(dup, Wl["w_up"], 0, name=n("ffn_norm_bwd"), rows=T, tk=1408,
                                               norm_bwd_of=(a["x1"], ffn3, l, dx2), host=[scatter_early])
        else:
            dx1, dg_ffn = mm_nt(dup, Wl["w_up"], 0, name=n("ffn_norm_bwd"), rows=T, tk=1408,
                                norm_bwd_of=(a["x1"], ffn3, l, dx2))
        dmo = mm_nt(dx1, Wl["w_out"], 0, name=n("d_gated"), rows=T)
        G["w_out"] = mm_tn(a["mo"], dx1, name=n("d_w_out"), rows=T, ka=D, nb=D)
        dz, dyab = gate_bwd(dmo, a["z"], a["yab"], name=n("gate_bwd"))
        G["w_branch_a"] = mm_tn(a["a_in"], dyab, name=n("d_w_branch_a"), rows=T, ka=D, nb=D)
        G["w_branch_b"] = mm_tn(a["s_in"], dyab, name=n("d_w_branch_b"), rows=T, ka=D, nb=D, b_col_off=D)
        da = mm_nt(dyab, Wl["w_branch_a"], 0, name=n("d_pool_out"), rows=T, kdim=D)
        ds = mm_nt(dyab, Wl["w_branch_b"], 0, name=n("d_sgu_out"), rows=T, kdim=D, a_col_off=D)
        dz, dwp, dsc, dws, dbt, dgs = mixer_bwd(da, ds, a["z"], dz, Wl["w_pool"], scale3, sgu3, wsm, wsmT, bT, l,
                                                name=n("mixer_bwd"))
        G["w_pool"] = dwp.astype(BF16)[None]
        G["w_in"] = mm_tn(a["hb"], dz, name=n("d_w_in"), rows=T, ka=D, nb=5 * D, tn=1280)
        if l == 0:
            scatter_late, parts_late = _reduce_start(G, LATE_GRADS_L0, place, "l0_late")
            (dx, dg_mix), slots_late = mm_nt(dz, Wl["w_in"], 0, name=n("mix_norm_bwd"), rows=T, tk=1280,
                                             norm_bwd_of=(a["x"], mix3, l, dx1), host=[scatter_late])
        else:
            dx, dg_mix = mm_nt(dz, Wl["w_in"], 0, name=n("mix_norm_bwd"), rows=T, tk=1280,
                               norm_bwd_of=(a["x"], mix3, l, dx1))
            scatter1, parts1 = _reduce_start(G, all_names, place, "l1")
        small_grads[l] = dict(
            mix_norm=dg_mix[0], pool_scale=dsc[0], sgu_norm=dgs[0], w_spatial=dws, b_spatial=dbt.T,
            ffn_norm=dg_ffn[0], conv_b=jnp.concatenate([dcb[0, 0], dcb[1, 0]]), ple_norm=dg_ple[0],
            conv_w=jnp.concatenate([dcw[0, :3], dcw[1, :3]], axis=1))
    reduced = {}
    _reduce_end(all_names, parts1, slots1, place, 1, reduced)
    _reduce_end(EARLY_GRADS_L0, parts_early, slots_early, place, 0, reduced)
    _reduce_end(LATE_GRADS_L0, parts_late, slots_late, place, 0, reduced)
    return loss_acc, dx, reduced, small_grads, dg_final[0]


SMALL_ORDER = ("mix_norm", "pool_scale", "sgu_norm", "w_spatial", "b_spatial", "ffn_norm", "conv_b", "ple_norm",
               "conv_w")


def _pack_rows(pieces, row_multiple):
    flat = jnp.concatenate([a.reshape(-1) for a in pieces])
    rows = -(-flat.shape[0] // LANES)
    rows = -(-rows // row_multiple) * row_multiple
    return jnp.pad(flat, (0, rows * LANES - flat.shape[0])).reshape(rows, LANES)


def _unpack(flat, shapes):
    out, off = [], 0
    for shp in shapes:
        size = 1
        for s in shp:
            size *= s
        out.append(flat[off:off + size].reshape(shp))
        off += size
    return out


def kernel(x, p, mix_norm, w_in, w_pool, pool_scale, sgu_norm, w_spatial, b_spatial, w_branch_a, w_branch_b, w_out, ffn_norm, w_up, conv_w, conv_b, w_down, ple_norm, w_ple_gate, w_ple, final_norm, loss_target, m_mix_norm, m_w_in, m_w_pool, m_pool_scale, m_sgu_norm, m_w_spatial, m_b_spatial, m_w_branch_a, m_w_branch_b, m_w_out, m_ffn_norm, m_w_up, m_conv_w, m_conv_b, m_w_down, m_ple_norm, m_w_ple_gate, m_w_ple, m_final_norm, v_mix_norm, v_w_in, v_w_pool, v_pool_scale, v_sgu_norm, v_w_spatial, v_b_spatial, v_w_branch_a, v_w_branch_b, v_w_out, v_ffn_norm, v_w_up, v_conv_w, v_conv_b, v_w_down, v_ple_norm, v_w_ple_gate, v_w_ple, v_final_norm):
    names = ["mix_norm", "w_in", "w_pool", "pool_scale", "sgu_norm", "w_spatial", "b_spatial", "w_branch_a",
             "w_branch_b", "w_out", "ffn_norm", "w_up", "conv_w", "conv_b", "w_down", "ple_norm", "w_ple_gate",
             "w_ple", "final_norm"]
    w = dict(zip(names, [mix_norm, w_in, w_pool, pool_scale, sgu_norm, w_spatial, b_spatial, w_branch_a, w_branch_b,
                         w_out, ffn_norm, w_up, conv_w, conv_b, w_down, ple_norm, w_ple_gate, w_ple, final_norm]))
    m = dict(zip(names, [m_mix_norm, m_w_in, m_w_pool, m_pool_scale, m_sgu_norm, m_w_spatial, m_b_spatial,
                         m_w_branch_a, m_w_branch_b, m_w_out, m_ffn_norm, m_w_up, m_conv_w, m_conv_b, m_w_down,
                         m_ple_norm, m_w_ple_gate, m_w_ple, m_final_norm]))
    v = dict(zip(names, [v_mix_norm, v_w_in, v_w_pool, v_pool_scale, v_sgu_norm, v_w_spatial, v_b_spatial,
                         v_w_branch_a, v_w_branch_b, v_w_out, v_ffn_norm, v_w_up, v_conv_w, v_conv_b, v_w_down,
                         v_ple_norm, v_w_ple_gate, v_w_ple, v_final_norm]))
    T = x.shape[1]
    chip = 2 * lax.axis_index("x") + lax.axis_index("y")
    place = jnp.stack([chip, lax.axis_index("c")]).astype(jnp.int32)

    big_names = [t[0] for t in BIG]
    placed = [{k: place_shard(w[k], l, *GEOM[k], BF16, place, name=f"place_{k}_l{l}") for k in big_names}
              for l in range(2)]
    conv_w8 = jnp.pad(conv_w, ((0, 0), (0, CONV_ROWS - conv_w.shape[1]), (0, 0)))
    conv_placed = place_both_layers(conv_w8, 1, conv_w.shape[2], place, name="place_conv_w")
    w_in0, conv_w_all = run_exchanges([gather_exchange([placed[0]["w_in"]], [GEOM["w_in"]]),
                                       gather_by_layer_exchange(conv_placed, 1, conv_w.shape[2])],
                                      name="gather_first_weights")
    placed[0]["w_in"] = w_in0

    small = {k: w[k] for k in ("mix_norm", "pool_scale", "sgu_norm", "w_spatial", "b_spatial", "ffn_norm",
                               "conv_b", "ple_norm", "final_norm")}
    loss_acc, dx, reduced, small_grads, dg_final = _local_step(
        x.reshape(T, D), p.reshape(2 * T, p.shape[-1]), loss_target.reshape(T, D), placed[0], placed[1], conv_w_all,
        small, place)
    loss = lax.psum(loss_acc[0, 0], ("x", "y", "c"))
    full = run_exchanges([share_exchange([reduced[k] for k in big_names])], name="share_halves")
    grads = dict(zip(big_names, full))

    pieces = [small_grads[l][k] for l in range(2) for k in SMALL_ORDER] + [dg_final]
    shapes = [a.shape for a in pieces]
    total = all_reduce_small(_pack_rows(pieces, 16)).reshape(-1)
    summed = _unpack(total, shapes)
    per_layer = {k: jnp.stack([summed[i], summed[len(SMALL_ORDER) + i]]) for i, k in enumerate(SMALL_ORDER)}
    for k in ("mix_norm", "pool_scale", "sgu_norm", "w_spatial", "b_spatial", "ffn_norm", "conv_b", "ple_norm"):
        grads[k] = per_layer[k]
    grads["final_norm"] = summed[-1]
    cw = conv_w.shape[2]
    grads["conv_w"] = lax.dynamic_slice_in_dim(per_layer["conv_w"], chip * cw, cw, axis=2)

    delta, new_m, new_v = {}, {}, {}
    for name in big_names:
        shp = w[name].shape
        d_, m_, v_ = elementwise(_adamw, [_view2d(a) for a in (w[name], grads[name], m[name], v[name])],
                                 [F32, F32, F32], name=f"adamw_{name}")
        delta[name], new_m[name], new_v[name] = d_.reshape(shp), m_.reshape(shp), v_.reshape(shp)
    small_names = [k for k in names if k not in big_names]
    small_shapes = [w[k].shape for k in small_names]
    packed = [_pack_rows([src[k] for k in small_names], 8) for src in (w, grads, m, v)]
    outs = elementwise(_adamw, packed, [F32, F32, F32], name="adamw_small")
    for dst, o in zip((delta, new_m, new_v), outs):
        for k, a in zip(small_names, _unpack(o.reshape(-1), small_shapes)):
            dst[k] = a

    return (loss, dx.reshape(1, T, D), *[grads[k] for k in names], *[delta[k] for k in names],
            *[new_m[k] for k in names], *[new_v[k] for k in names])
```

```python
import functools

import jax
import jax.numpy as jnp
from jax import lax
from jax.experimental import pallas as pl
from jax.experimental.pallas import tpu as pltpu

F32 = jnp.float32
BF16 = jnp.bfloat16
EPS = 1e-6
D = 1024
POOL_WINDOWS = (2, 4, 8, 16)
PG = 256
POOL_HALO = 16
CHUNK = 128
HEADS = 8
DFF = 2816
CONV_HALO = 8
CONV_TC = 1408
N_CHIPS = 4
LANES = 128
VMEM_LIMIT = 56 * 1024 * 1024
MESH = pl.DeviceIdType.MESH
ANY = pl.BlockSpec(memory_space=pl.ANY)

ADAM_LR = 0.001
ADAM_B1 = 0.9
ADAM_B2 = 0.999
ADAM_EPS = 1e-08
ADAM_WD = 0.01
ADAM_STEP = 10

BIG = (
    ("w_in", (D, 5 * D), 1, 5 * D // N_CHIPS),
    ("w_pool", (4, PG, PG), 1, PG // N_CHIPS),
    ("w_branch_a", (D, D), 0, D // N_CHIPS),
    ("w_branch_b", (D, D), 0, D // N_CHIPS),
    ("w_out", (D, D), 0, D // N_CHIPS),
    ("w_up", (D, 2 * DFF), 1, 2 * DFF // N_CHIPS),
    ("w_down", (DFF, D), 0, DFF // N_CHIPS),
    ("w_ple_gate", (D, D), 0, D // N_CHIPS),
    ("w_ple", (PG, D), 1, D // N_CHIPS),
)
CONV_ROWS = 8


def _params(n_axes):
    return pltpu.CompilerParams(dimension_semantics=("arbitrary",) * n_axes, vmem_limit_bytes=VMEM_LIMIT)


def _gelu(x):
    return 0.5 * x * (1.0 + lax.erf(x * 0.7071067811865476))


def _gelu_grad(x):
    return 0.5 * (1.0 + lax.erf(x * 0.7071067811865476)) + x * jnp.exp(-0.5 * x * x) * 0.3989422804014327


def _shard_shape(shape, axis, size):
    return tuple(size if a == axis else s for a, s in enumerate(shape))


def _block(ref, axis, j, size):
    idx = tuple(pl.ds(j * size, size) if a == axis else slice(None) for a in range(len(ref.shape)))
    return ref.at[idx]


def mm_nn(a, w, l, *, name, rows, out_dtype=F32, resid=None, a_row_off=0, out=None, out_cols=None,
          out_col_off=0, norm_gain=None, host=None, tm=1024, tn=None, tk=None):
    K, N = w.shape[1], w.shape[2]
    tn = tn or N
    tk = tk or K
    nk = K // tk
    out_cols = out_cols or N
    assert rows % tm == 0 and N % tn == 0 and K % tk == 0 and out_col_off % tn == 0 and a_row_off % tm == 0
    has_resid, has_out, has_norm = resid is not None, out is not None, norm_gain is not None
    assert not has_norm or (tn == N and not has_out)
    grid = (N // tn, rows // tm, nk)
    hosting = _Hosting(host)
    n_in = 2 + has_resid + has_norm + has_out
    n_host_in, n_host_out = len(hosting.arrays), len(hosting.out_shapes)
    n_own_out = 1 + has_norm

    def body(*refs):
        refs = list(refs)
        a_ref, w_ref = refs[0], refs[1]
        r_ref = refs[2] if has_resid else None
        g_ref = refs[2 + has_resid] if has_norm else None
        host_in = refs[n_in:n_in + n_host_in]
        o_base = n_in + n_host_in
        o_ref = refs[o_base]
        host_out = refs[o_base + n_own_out:o_base + n_own_out + n_host_out]
        scratch = refs[o_base + n_own_out + n_host_out:]
        if hosting.plan:
            first, last = _first_last(grid)
            sems = scratch[-2:]

            @pl.when(first)
            def _():
                hosting.begin(host_in, host_out, *sems)

        part = jnp.dot(a_ref[...].astype(BF16), w_ref[...], preferred_element_type=F32)

        def finish(r):
            if has_resid:
                r = r + r_ref[...]
            o_ref[...] = r.astype(o_ref.dtype)
            if has_norm:
                scale = lax.rsqrt(jnp.mean(r * r, axis=-1, keepdims=True) + EPS)
                refs[o_base + 1][...] = (r * scale * g_ref[...]).astype(BF16)

        if nk == 1:
            finish(part)
        else:
            acc = scratch[0]
            k = pl.program_id(2)

            @pl.when(k == 0)
            def _():
                acc[...] = part

            @pl.when(k > 0)
            def _():
                acc[...] += part

            @pl.when(k == nk - 1)
            def _():
                finish(acc[...])

        if hosting.plan:
            @pl.when(last)
            def _():
                hosting.finish(host_in, host_out, *sems)

    in_specs = [pl.BlockSpec((tm, tk), lambda j, i, k: (i + a_row_off // tm, k)),
                pl.BlockSpec((None, tk, tn), lambda j, i, k: (l, k, j))]
    args = [a, w]
    if has_resid:
        in_specs.append(pl.BlockSpec((tm, tn), lambda j, i, k: (i, j)))
        args.append(resid)
    if has_norm:
        in_specs.append(pl.BlockSpec((None, 1, tn), lambda j, i, k: (l, 0, 0)))
        args.append(norm_gain)
    aliases = {}
    if has_out:
        in_specs.append(ANY)
        aliases = {len(args): 0}
        args.append(out)
    aliases.update(hosting.aliases(n_in, n_own_out))
    out_specs = [pl.BlockSpec((tm, tn), lambda j, i, k: (i, j + out_col_off // tn))]
    out_shape = [jax.ShapeDtypeStruct((rows, out_cols), out_dtype)]
    if has_norm:
        out_specs.append(pl.BlockSpec((tm, tn), lambda j, i, k: (i, j)))
        out_shape.append(jax.ShapeDtypeStruct((rows, N), BF16))
    res = pl.pallas_call(
        body, name=name, grid=grid,
        in_specs=in_specs + [ANY] * n_host_in,
        out_specs=out_specs + [ANY] * n_host_out,
        out_shape=out_shape + hosting.out_shapes,
        scratch_shapes=([pltpu.VMEM((tm, tn), F32)] if nk > 1 else []) + hosting.scratch(),
        input_output_aliases=aliases, compiler_params=_params(3))(*args, *hosting.arrays)
    own = res[0] if n_own_out == 1 else tuple(res[:n_own_out])
    return (own, list(res[n_own_out:])) if hosting.plan else own


def mm_nt(a, w, l, *, name, rows, kdim=None, a_col_off=0, out_dtype=BF16, norm_bwd_of=None, host=None, tm=1024,
          tn=None, tk=None):
    R = w.shape[1]
    kdim = kdim or w.shape[2]
    tn = tn or R
    tk = tk or kdim
    nk = kdim // tk
    assert rows % tm == 0 and R % tn == 0 and kdim % tk == 0 and a_col_off % tk == 0
    fused = norm_bwd_of is not None
    assert not fused or tn == R
    grid = (R // tn, rows // tm, nk)
    hosting = _Hosting(host)
    n_host_in, n_host_out = len(hosting.arrays), len(hosting.out_shapes)
    n_own_in, n_own_out = (3, 2) if fused else (0, 1)

    def body(a_ref, w_ref, *refs):
        host_in = refs[n_own_in:n_own_in + n_host_in]
        host_out = refs[n_own_in + n_host_in + n_own_out:n_own_in + n_host_in + n_own_out + n_host_out]
        scratch = refs[n_own_in + n_host_in + n_own_out + n_host_out:]
        rest = list(refs[:n_own_in]) + list(refs[n_own_in + n_host_in:n_own_in + n_host_in + n_own_out]) \
            + ([scratch[0]] if nk > 1 else [])
        if hosting.plan:
            first, last = _first_last(grid)
            sems = scratch[-2:]

            @pl.when(first)
            def _():
                hosting.begin(host_in, host_out, *sems)

        part = lax.dot_general(a_ref[...].astype(BF16), w_ref[...], (((1,), (1,)), ((), ())),
                               preferred_element_type=F32)
        i, k = pl.program_id(1), pl.program_id(2)

        def finish(dh):
            if not fused:
                rest[0][...] = dh.astype(rest[0].dtype)
                return
            x_ref, g_ref, dxi_ref, dx_ref, dg_ref = rest[:5]

            @pl.when(i == 0)
            def _():
                dg_ref[...] = jnp.zeros_like(dg_ref)

            xv = x_ref[...]
            r = lax.rsqrt(jnp.mean(xv * xv, axis=-1, keepdims=True) + EPS)
            xh = xv * r
            dhg = dh * g_ref[...]
            dx_ref[...] = dxi_ref[...] + r * (dhg - xh * jnp.mean(dhg * xh, axis=-1, keepdims=True))
            dg_ref[0:1, :] += jnp.sum(dh * xh, axis=0, keepdims=True)

        if nk == 1:
            finish(part)
        else:
            acc = rest[-1]

            @pl.when(k == 0)
            def _():
                acc[...] = part

            @pl.when(k > 0)
            def _():
                acc[...] += part

            @pl.when(k == nk - 1)
            def _():
                finish(acc[...])

        if hosting.plan:
            @pl.when(last)
            def _():
                hosting.finish(host_in, host_out, *sems)

    if a.ndim == 3:
        per = a.shape[2] // tk
        a_spec = pl.BlockSpec((None, tm, tk), lambda j, i, k: (k // per, i, k % per))
    else:
        a_spec = pl.BlockSpec((tm, tk), lambda j, i, k: (i, k + a_col_off // tk))
    in_specs = [a_spec, pl.BlockSpec((None, tn, tk), lambda j, i, k: (l, j, k))]
    args = [a, w]
    row_tile = pl.BlockSpec((tm, tn), lambda j, i, k: (i, j))
    if fused:
        x, gain, gl, dx_in = norm_bwd_of
        in_specs += [row_tile, pl.BlockSpec((None, 1, tn), lambda j, i, k: (gl, 0, 0)), row_tile]
        args += [x, gain, dx_in]
        out_specs = [row_tile, pl.BlockSpec((8, tn), lambda j, i, k: (0, 0))]
        out_shape = [jax.ShapeDtypeStruct((rows, R), F32), jax.ShapeDtypeStruct((8, R), F32)]
    else:
        out_specs, out_shape = [row_tile], [jax.ShapeDtypeStruct((rows, R), out_dtype)]
    res = pl.pallas_call(
        body, name=name, grid=grid, in_specs=in_specs + [ANY] * n_host_in,
        out_specs=out_specs + [ANY] * n_host_out, out_shape=out_shape + hosting.out_shapes,
        scratch_shapes=([pltpu.VMEM((tm, tn), F32)] if nk > 1 else []) + hosting.scratch(),
        input_output_aliases=hosting.aliases(2 + n_own_in, n_own_out), compiler_params=_params(3))(
            *args, *hosting.arrays)
    own = tuple(res[:n_own_out]) if fused else res[0]
    return (own, list(res[n_own_out:])) if hosting.plan else own


def mm_tn(a, b, *, name, rows, ka, nb, a_row_off=0, b_col_off=0, tm=None, tn=None, tk=2048):
    tm = tm or ka
    tn = tn or nb
    tk = min(tk, rows)
    nk = rows // tk
    assert ka % tm == 0 and nb % tn == 0 and rows % tk == 0 and b_col_off % tn == 0 and a_row_off % tk == 0

    def body(a_ref, b_ref, o_ref, acc):
        part = lax.dot_general(a_ref[...].astype(BF16), b_ref[...].astype(BF16), (((0,), (0,)), ((), ())),
                               preferred_element_type=F32)
        k = pl.program_id(2)

        @pl.when(k == 0)
        def _():
            acc[...] = part

        @pl.when(k > 0)
        def _():
            acc[...] += part

        @pl.when(k == nk - 1)
        def _():
            o_ref[...] = acc[...].astype(o_ref.dtype)

    if b.ndim == 3:
        per = b.shape[2] // tn
        b_spec = pl.BlockSpec((None, tk, tn), lambda j, i, k: (j // per, k, j % per))
    else:
        b_spec = pl.BlockSpec((tk, tn), lambda j, i, k: (k, j + b_col_off // tn))
    return pl.pallas_call(
        body, name=name, grid=(nb // tn, ka // tm, nk),
        in_specs=[pl.BlockSpec((tk, tm), lambda j, i, k: (k + a_row_off // tk, i)), b_spec],
        out_specs=pl.BlockSpec((None, tm, tn), lambda j, i, k: (0, i, j)),
        out_shape=jax.ShapeDtypeStruct((1, ka, nb), BF16),
        scratch_shapes=[pltpu.VMEM((tm, tn), F32)], compiler_params=_params(3))(a, b)


def _row_spec(tm, width, col=0):
    return pl.BlockSpec((tm, width), lambda i: (i, col))


def _gain_spec(l, width=D):
    return pl.BlockSpec((None, 1, width), lambda i: (l, 0, 0))


def norm_fwd(x, g3, l, *, name, tm=512):
    T = x.shape[0]

    def body(x_ref, g_ref, o_ref):
        xv = x_ref[...]
        r = lax.rsqrt(jnp.mean(xv * xv, axis=-1, keepdims=True) + EPS)
        o_ref[...] = (xv * r * g_ref[...]).astype(BF16)

    return pl.pallas_call(
        body, name=name, grid=(T // tm,),
        in_specs=[_row_spec(tm, D), _gain_spec(l)], out_specs=_row_spec(tm, D),
        out_shape=jax.ShapeDtypeStruct((T, D), BF16), compiler_params=_params(1))(x, g3)


def _winsum_back(ext, w):
    s, span = ext, 1
    while span < w:
        s = s + pltpu.roll(s, span, 0)
        span *= 2
    return s


def _winsum_fwd(ext, w):
    rows = ext.shape[0]
    s, span = ext, 1
    while span < w:
        s = s + pltpu.roll(s, rows - span, 0)
        span *= 2
    return s


def _pooled(ext, z, t, g, w):
    sl = slice(g * PG, (g + 1) * PG)
    s = _winsum_back(ext[:, sl], w)[POOL_HALO:, :]
    return s / jnp.minimum(t + 1, w).astype(F32) - z[:, sl]


def pool_fwd(z, wpool, scale3, l, *, name, tm=256):
    T = z.shape[0]
    hb = tm // POOL_HALO
    wl = l if wpool.shape[0] > 1 else 0

    def body(z_ref, zp_ref, wp_ref, sc_ref, o_ref):
        i = pl.program_id(0)
        zv = z_ref[...].astype(F32)
        prev = jnp.where(i == 0, 0.0, zp_ref[...].astype(F32))
        ext = jnp.concatenate([prev, zv], axis=0)
        t = i * tm + lax.broadcasted_iota(jnp.int32, (tm, 1), 0)
        for g, w in enumerate(POOL_WINDOWS):
            sl = slice(g * PG, (g + 1) * PG)
            pooled = _pooled(ext, zv, t, g, w)
            q = jnp.dot(pooled.astype(BF16), wp_ref[g], preferred_element_type=F32)
            o_ref[:, sl] = (q * sc_ref[:, sl]).astype(BF16)

    return pl.pallas_call(
        body, name=name, grid=(T // tm,),
        in_specs=[_row_spec(tm, D),
                  pl.BlockSpec((POOL_HALO, D), lambda i: (jnp.maximum(i * hb - 1, 0), 0)),
                  pl.BlockSpec((None, 4, PG, PG), lambda i: (wl, 0, 0, 0)),
                  _gain_spec(l)],
        out_specs=_row_spec(tm, D),
        out_shape=jax.ShapeDtypeStruct((T, D), BF16), compiler_params=_params(1))(z, z, wpool, scale3)


def sgu_fwd(z, g3, wsm, bT, l, *, name, tm=256):
    T = z.shape[0]

    def body(zu_ref, zv_ref, g_ref, ws_ref, b_ref, o_ref):
        gu = _gelu(zu_ref[...].astype(F32))
        gv = _gelu(zv_ref[...].astype(F32))
        rv = lax.rsqrt(jnp.mean(gv * gv, axis=-1, keepdims=True) + EPS)
        vn = (gv * rv * g_ref[...]).astype(BF16)
        for n in range(tm // CHUNK):
            r = slice(n * CHUNK, (n + 1) * CHUNK)
            for h in range(HEADS):
                cs = slice(h * CHUNK, (h + 1) * CHUNK)
                mixed = jnp.dot(ws_ref[h], vn[r, cs], preferred_element_type=F32) + b_ref[:, h:h + 1]
                o_ref[r, cs] = (gu[r, cs] * mixed).astype(BF16)

    return pl.pallas_call(
        body, name=name, grid=(T // tm,),
        in_specs=[_row_spec(tm, D, 1), _row_spec(tm, D, 2), _gain_spec(l),
                  pl.BlockSpec((None, HEADS, CHUNK, CHUNK), lambda i: (l, 0, 0, 0)),
                  pl.BlockSpec((None, CHUNK, HEADS), lambda i: (l, 0, 0))],
        out_specs=_row_spec(tm, D),
        out_shape=jax.ShapeDtypeStruct((T, D), BF16), compiler_params=_params(1))(z, z, g3, wsm, bT)


def gate_fwd(z, yab, *, name, tm=512):
    T = z.shape[0]

    def body(za_ref, zb_ref, y_ref, o_ref):
        ga = jax.nn.sigmoid(za_ref[...].astype(F32))
        gb = jax.nn.sigmoid(zb_ref[...].astype(F32))
        o_ref[...] = (ga * y_ref[:, :D].astype(F32) + gb * y_ref[:, D:].astype(F32)).astype(BF16)

    return pl.pallas_call(
        body, name=name, grid=(T // tm,),
        in_specs=[_row_spec(tm, D, 3), _row_spec(tm, D, 4), _row_spec(tm, 2 * D)],
        out_specs=_row_spec(tm, D),
        out_shape=jax.ShapeDtypeStruct((T, D), BF16), compiler_params=_params(1))(z, z, yab)


def _conv(ext, w_ref, b_ref):
    down1, down2 = pltpu.roll(ext, 1, 0), pltpu.roll(ext, 2, 0)
    c = b_ref[...] + w_ref[0:1, :] * down2
    c = c + w_ref[1:2, :] * down1
    return c + w_ref[2:3, :] * ext, down1, down2


def conv_fwd(up, convw, convb3, l, *, name, tm=256):
    T = up.shape[0]
    tc = CONV_TC
    nc = DFF // tc
    hb = tm // CONV_HALO

    def body(ua_ref, uap_ref, ub_ref, ubp_ref, wa_ref, wb_ref, ba_ref, bb_ref, o_ref):
        i = pl.program_id(1)

        def conv_of(u_ref, p_ref, w_ref, b_ref):
            ext = jnp.concatenate([jnp.where(i == 0, 0.0, p_ref[...]), u_ref[...]], axis=0)
            return _conv(ext, w_ref, b_ref)[0][CONV_HALO:, :]

        ca = conv_of(ua_ref, uap_ref, wa_ref, ba_ref)
        cb = conv_of(ub_ref, ubp_ref, wb_ref, bb_ref)
        o_ref[...] = (_gelu(ca) * cb).astype(BF16)

    def cur(off):
        return pl.BlockSpec((tm, tc), lambda j, i: (i, j + off))

    def prev(off):
        return pl.BlockSpec((CONV_HALO, tc), lambda j, i: (jnp.maximum(i * hb - 1, 0), j + off))

    def wspec(off):
        return pl.BlockSpec((None, CONV_ROWS, tc), lambda j, i: (l, 0, j + off))

    def bspec(off):
        return pl.BlockSpec((None, 1, tc), lambda j, i: (l, 0, j + off))

    return pl.pallas_call(
        body, name=name, grid=(nc, T // tm),
        in_specs=[cur(0), prev(0), cur(nc), prev(nc), wspec(0), wspec(nc), bspec(0), bspec(nc)],
        out_specs=pl.BlockSpec((tm, tc), lambda j, i: (i, j)),
        out_shape=jax.ShapeDtypeStruct((T, DFF), BF16),
        compiler_params=_params(2))(up, up, up, up, convw, convw, convb3, convb3)


def ple_fwd(x2, pg, e, g3, l, *, name, tm=512):
    T = x2.shape[0]
    has_norm = g3 is not None

    def body(x_ref, pg_ref, e_ref, *rest):
        xv = x_ref[...] + jax.nn.sigmoid(pg_ref[...].astype(F32)) * e_ref[...].astype(F32)
        if has_norm:
            g_ref, o_ref, h_ref = rest
            r = lax.rsqrt(jnp.mean(xv * xv, axis=-1, keepdims=True) + EPS)
            h_ref[...] = (xv * r * g_ref[...]).astype(BF16)
        else:
            o_ref, = rest
        o_ref[...] = xv

    x_shape = jax.ShapeDtypeStruct((T, D), F32)
    return pl.pallas_call(
        body, name=name, grid=(T // tm,),
        in_specs=[_row_spec(tm, D)] * 3 + ([_gain_spec(l)] if has_norm else []),
        out_specs=[_row_spec(tm, D)] * 2 if has_norm else _row_spec(tm, D),
        out_shape=[x_shape, jax.ShapeDtypeStruct((T, D), BF16)] if has_norm else x_shape,
        compiler_params=_params(1))(x2, pg, e, *([g3] if has_norm else []))


def loss_head(x, g3, tgt, *, name, tm=512):
    T = x.shape[0]

    def body(x_ref, g_ref, t_ref, loss_ref, dx_ref, dg_ref):
        @pl.when(pl.program_id(0) == 0)
        def _():
            loss_ref[...] = jnp.zeros_like(loss_ref)
            dg_ref[...] = jnp.zeros_like(dg_ref)

        xv, g = x_ref[...], g_ref[...]
        r = lax.rsqrt(jnp.mean(xv * xv, axis=-1, keepdims=True) + EPS)
        xh = xv * r
        err = xh * g - t_ref[...]
        loss_ref[...] += 0.5 * jnp.sum(jnp.mean(err * err, axis=-1, keepdims=True))
        dy = err * (1.0 / D)
        dyg = dy * g
        dx_ref[...] = r * (dyg - xh * jnp.mean(dyg * xh, axis=-1, keepdims=True))
        dg_ref[0:1, :] += jnp.sum(dy * xh, axis=0, keepdims=True)

    return pl.pallas_call(
        body, name=name, grid=(T // tm,),
        in_specs=[_row_spec(tm, D), pl.BlockSpec((1, D), lambda i: (0, 0)), _row_spec(tm, D)],
        out_specs=[pl.BlockSpec((8, LANES), lambda i: (0, 0)), _row_spec(tm, D),
                   pl.BlockSpec((8, D), lambda i: (0, 0))],
        out_shape=[jax.ShapeDtypeStruct((8, LANES), F32), jax.ShapeDtypeStruct((T, D), F32),
                   jax.ShapeDtypeStruct((8, D), F32)],
        compiler_params=_params(1))(x, g3, tgt)


def ple_bwd(dx, pg, e, *, name, tm=512):
    T = dx.shape[0]

    def body(dx_ref, pg_ref, e_ref, de_ref, dpg_ref):
        gate = jax.nn.sigmoid(pg_ref[...].astype(F32))
        dxv = dx_ref[...]
        de_ref[...] = (dxv * gate).astype(BF16)
        dpg_ref[...] = (dxv * e_ref[...].astype(F32) * gate * (1.0 - gate)).astype(BF16)

    return pl.pallas_call(
        body, name=name, grid=(T // tm,),
        in_specs=[_row_spec(tm, D)] * 3, out_specs=[_row_spec(tm, D)] * 2,
        out_shape=[jax.ShapeDtypeStruct((T, D), BF16)] * 2, compiler_params=_params(1))(dx, pg, e)


def conv_bwd(df, up, convw, convb3, l, *, name, host=None, side=None, tm=256):
    T = up.shape[0]
    tc = CONV_TC
    nc = DFF // tc
    hb = tm // CONV_HALO
    nt = T // tm
    rows = tm + 2 * CONV_HALO
    own = slice(CONV_HALO, CONV_HALO + tm)

    hosting = _Hosting(host)
    n_host_in, n_host_out = len(hosting.arrays), len(hosting.out_shapes)
    riding = _SideMatmuls(side, T, nc * nt, lambda j, i: j * nt + i)
    n_side_in, n_side_out = 2 * len(riding.jobs), len(riding.jobs)

    def body(df_ref, dfn_ref, ua_ref, uap_ref, uan_ref, ub_ref, ubp_ref, ubn_ref, wa_ref, wb_ref, ba_ref, bb_ref,
             *rest):
        host_in = rest[:n_host_in]
        side_in = rest[n_host_in:n_host_in + n_side_in]
        o0 = n_host_in + n_side_in
        dup_ref, dcw_ref, dcb_ref = rest[o0:o0 + 3]
        host_out = rest[o0 + 3:o0 + 3 + n_host_out]
        side_out = rest[o0 + 3 + n_host_out:o0 + 3 + n_host_out + n_side_out]
        scratch = rest[o0 + 3 + n_host_out + n_side_out:]
        sems, accs = scratch[:2 if hosting.plan else 0], scratch[2 if hosting.plan else 0:]
        i = pl.program_id(1)
        riding.run(side_in, side_out, accs, pl.program_id(0) * nt + i)
        if hosting.plan:
            first, last = _first_last((nc, nt))

            @pl.when(first)
            def _():
                hosting.begin(host_in, host_out, *sems)

        @pl.when(i == 0)
        def _():
            dcw_ref[...] = jnp.zeros_like(dcw_ref)
            dcb_ref[...] = jnp.zeros_like(dcb_ref)

        def ext_of(c_ref, p_ref, n_ref):
            return jnp.concatenate([jnp.where(i == 0, 0.0, p_ref[...]), c_ref[...],
                                    jnp.where(i == nt - 1, 0.0, n_ref[...])], axis=0)

        ea = ext_of(ua_ref, uap_ref, uan_ref)
        eb = ext_of(ub_ref, ubp_ref, ubn_ref)
        ca, ea1, ea2 = _conv(ea, wa_ref, ba_ref)
        cb, eb1, eb2 = _conv(eb, wb_ref, bb_ref)
        df_ext =jnp.concatenate([jnp.zeros((CONV_HALO, tc), F32), df_ref[...],
                                  jnp.where(i == nt - 1, 0.0, dfn_ref[...])], axis=0)
        cdf = 0.5 * (1.0 + lax.erf(ca * 0.7071067811865476))
        da = df_ext * cb * (cdf + ca * jnp.exp(-0.5 * ca * ca) * 0.3989422804014327)
        db = df_ext * (ca * cdf)

        def finish(h, dc, e, e1, e2, w_ref):
            dup = w_ref[2:3, :] * dc + w_ref[1:2, :] * pltpu.roll(dc, rows - 1, 0)
            dup = dup + w_ref[0:1, :] * pltpu.roll(dc, rows - 2, 0)
            dup_ref[h] = dup[own, :].astype(BF16)
            dco = dc[own, :]
            dcb_ref[h, 0:1, :] += jnp.sum(dco, axis=0, keepdims=True)
            dcw_ref[h, 0:1, :] += jnp.sum(dco * e2[own, :], axis=0, keepdims=True)
            dcw_ref[h, 1:2, :] += jnp.sum(dco * e1[own, :], axis=0, keepdims=True)
            dcw_ref[h, 2:3, :] += jnp.sum(dco * e[own, :], axis=0, keepdims=True)

        finish(0, da, ea, ea1, ea2, wa_ref)
        finish(1, db, eb, eb1, eb2, wb_ref)
        if hosting.plan:
            @pl.when(last)
            def _():
                hosting.finish(host_in, host_out, *sems)

    def nxt(i):
        return jnp.minimum((i + 1) * hb, T // CONV_HALO - 1)

    def prv(i):
        return jnp.maximum(i * hb - 1, 0)

    def up_specs(off):
        return [pl.BlockSpec((tm, tc), lambda j, i: (i, j + off)),
                pl.BlockSpec((CONV_HALO, tc), lambda j, i: (prv(i), j + off)),
                pl.BlockSpec((CONV_HALO, tc), lambda j, i: (nxt(i), j + off))]

    in_specs = [pl.BlockSpec((tm, tc), lambda j, i: (i, j)),
                pl.BlockSpec((CONV_HALO, tc), lambda j, i: (nxt(i), j)),
                *up_specs(0), *up_specs(nc),
                pl.BlockSpec((None, CONV_ROWS, tc), lambda j, i: (l, 0, j)),
                pl.BlockSpec((None, CONV_ROWS, tc), lambda j, i: (l, 0, j + nc)),
                pl.BlockSpec((None, 1, tc), lambda j, i: (l, 0, j)),
                pl.BlockSpec((None, 1, tc), lambda j, i: (l, 0, j + nc))]
    res = pl.pallas_call(
        body, name=name, grid=(nc, nt), in_specs=in_specs + [ANY] * n_host_in + riding.in_specs(),
        out_specs=[pl.BlockSpec((2, tm, tc), lambda j, i: (0, i, j)),
                   pl.BlockSpec((2, 8, tc), lambda j, i: (0, 0, j)),
                   pl.BlockSpec((2, 8, tc), lambda j, i: (0, 0, j))] + [ANY] * n_host_out + riding.out_specs(),
        out_shape=[jax.ShapeDtypeStruct((2, T, DFF), BF16), jax.ShapeDtypeStruct((2, 8, DFF), F32),
                   jax.ShapeDtypeStruct((2, 8, DFF), F32)] + hosting.out_shapes + riding.out_shapes(),
        scratch_shapes=hosting.scratch() + riding.scratch(), input_output_aliases=hosting.aliases(12, 3),
        compiler_params=_params(2))(df, df, up, up, up, up, up, up, convw, convw, convb3, convb3, *hosting.arrays,
                                    *riding.arrays())
    return res[0], res[1], res[2], list(res[3:3 + n_host_out]), list(res[3 + n_host_out:])


def gate_bwd(dmo, z, yab, *, name, tm=512):
    T = z.shape[0]

    def body(dmo_ref, zg_ref, y_ref, dz_ref, dy_ref):
        g = jax.nn.sigmoid(zg_ref[...].astype(F32))
        dmo_v = dmo_ref[...].astype(F32)
        dy_ref[...] = (dmo_v * g).astype(BF16)
        dz_ref[...] = (dmo_v * y_ref[...].astype(F32) * g * (1.0 - g)).astype(BF16)

    return pl.pallas_call(
        body, name=name, grid=(T // tm, 2),
        in_specs=[pl.BlockSpec((tm, D), lambda i, s: (i, 0)),
                  pl.BlockSpec((tm, D), lambda i, s: (i, 3 + s)),
                  pl.BlockSpec((tm, D), lambda i, s: (i, s))],
        out_specs=[pl.BlockSpec((tm, D), lambda i, s: (i, 3 + s)),
                   pl.BlockSpec((tm, D), lambda i, s: (i, s))],
        out_shape=[jax.ShapeDtypeStruct((T, 5 * D), BF16), jax.ShapeDtypeStruct((T, 2 * D), BF16)],
        compiler_params=_params(2))(dmo, z, yab)


def mixer_bwd(da, ds, z, dz, wpool, scale3, g3, wsm, wsmT, bT, l, *, name, side=None, tm=256):
    T = z.shape[0]
    hb = tm // POOL_HALO
    nt = T // tm
    riding = _SideMatmuls(side, T, nt, lambda i: i)
    n_side_in, n_side_out = 2 * len(riding.jobs), len(riding.jobs)

    def body(da_ref, dan_ref, ds_ref, zp_ref, zpp_ref, zu_ref, zv_ref, wp_ref, sc_ref, g_ref, ws_ref, wst_ref,
             b_ref, dzin_ref, *rest):
        del dzin_ref
        side_in = rest[:n_side_in]
        dz_ref, dwp_ref, dsc_ref, dws_ref, dbt_ref, dgs_ref = rest[n_side_in:n_side_in + 6]
        side_out = rest[n_side_in + 6:n_side_in + 6 + n_side_out]
        mixed_scr, dvn_scr, db_scr = rest[n_side_in + 6 + n_side_out:n_side_in + 9 + n_side_out]
        i = pl.program_id(0)
        riding.run(side_in, side_out, rest[n_side_in + 9 + n_side_out:], i)

        @pl.when(i == 0)
        def _():
            dwp_ref[...] = jnp.zeros_like(dwp_ref)
            dsc_ref[...] = jnp.zeros_like(dsc_ref)
            dws_ref[...] = jnp.zeros_like(dws_ref)
            dgs_ref[...] = jnp.zeros_like(dgs_ref)
            db_scr[...] = jnp.zeros_like(db_scr)

        zv_p = zp_ref[...].astype(F32)
        ext = jnp.concatenate([jnp.where(i == 0, 0.0, zpp_ref[...].astype(F32)), zv_p], axis=0)
        da_v = da_ref[...].astype(F32)
        da_ext = jnp.concatenate([da_v, jnp.where(i == nt - 1, 0.0, dan_ref[...].astype(F32))], axis=0)
        t = i * tm + lax.broadcasted_iota(jnp.int32, (tm, 1), 0)
        t_ext = i * tm + lax.broadcasted_iota(jnp.int32, (tm + POOL_HALO, 1), 0)
        for g, w in enumerate(POOL_WINDOWS):
            sl = slice(g * PG, (g + 1) * PG)
            pooled = _pooled(ext, zv_p, t, g, w).astype(BF16)
            q = jnp.dot(pooled, wp_ref[g], preferred_element_type=F32)
            dsc_ref[0:1, sl] += jnp.sum(da_v[:, sl] * q, axis=0, keepdims=True)
            dq_ext = (da_ext[:, sl] * sc_ref[:, sl]).astype(BF16)
            dwp_ref[g] += lax.dot_general(pooled, dq_ext[:tm, :], (((0,), (0,)), ((), ())),
                                          preferred_element_type=F32)
            dpool = lax.dot_general(dq_ext, wp_ref[g], (((1,), (1,)), ((), ())), preferred_element_type=F32)
            spread = _winsum_fwd(dpool / jnp.minimum(t_ext + 1, w).astype(F32), w)
            dz_ref[:, sl] = (spread[:tm, :] - dpool[:tm, :]).astype(BF16)

        zu, zv, ds_v = zu_ref[...].astype(F32), zv_ref[...].astype(F32), ds_ref[...].astype(F32)
        gain = g_ref[...]
        gu, gv = _gelu(zu), _gelu(zv)
        rv = lax.rsqrt(jnp.mean(gv * gv, axis=-1, keepdims=True) + EPS)
        vh = gv * rv
        vn = (vh * gain).astype(BF16)
        dmix = ds_v * gu
        dmix_b = dmix.astype(BF16)
        for n in range(tm // CHUNK):
            r = slice(n * CHUNK, (n + 1) * CHUNK)
            db_scr[...] += dmix[r, :]
            for h in range(HEADS):
                cs = slice(h * CHUNK, (h + 1) * CHUNK)
                mixed_scr[r, cs] = jnp.dot(ws_ref[h], vn[r, cs], preferred_element_type=F32) + b_ref[:, h:h + 1]
                dws_ref[h] += lax.dot_general(dmix_b[r, cs], vn[r, cs], (((1,), (1,)), ((), ())),
                                              preferred_element_type=F32)
                dvn_scr[r, cs] = jnp.dot(wst_ref[h], dmix_b[r, cs], preferred_element_type=F32)
        dz_ref[:, D:2 * D] = (ds_v * mixed_scr[...] * _gelu_grad(zu)).astype(BF16)
        dvn = dvn_scr[...]
        dgs_ref[0:1, :] += jnp.sum(dvn * vh, axis=0, keepdims=True)
        dvg = dvn * gain
        dgv = rv * (dvg - vh * jnp.mean(dvg * vh, axis=-1, keepdims=True))
        dz_ref[:, 2 * D:3 * D] = (dgv * _gelu_grad(zv)).astype(BF16)

        @pl.when(i == nt - 1)
        def _():
            tril = (lax.broadcasted_iota(jnp.int32, (CHUNK, CHUNK), 0)
                    >= lax.broadcasted_iota(jnp.int32, (CHUNK, CHUNK), 1)).astype(F32)
            for h in range(HEADS):
                dws_ref[h] = dws_ref[h] * tril
                dbt_ref[:, h:h + 1] = jnp.sum(db_scr[:, h * CHUNK:(h + 1) * CHUNK], axis=1, keepdims=True)

    const4 = lambda i: (l, 0, 0, 0)
    wl = l if wpool.shape[0] > 1 else 0
    in_specs = [
        _row_spec(tm, D),
        pl.BlockSpec((POOL_HALO, D), lambda i: (jnp.minimum((i + 1) * hb, T // POOL_HALO - 1), 0)),
        _row_spec(tm, D),
        _row_spec(tm, D, 0),
        pl.BlockSpec((POOL_HALO, D), lambda i: (jnp.maximum(i * hb - 1, 0), 0)),
        _row_spec(tm, D, 1), _row_spec(tm, D, 2),
        pl.BlockSpec((None, 4, PG, PG), lambda i: (wl, 0, 0, 0)),
        _gain_spec(l), _gain_spec(l),
        pl.BlockSpec((None, HEADS, CHUNK, CHUNK), const4),
        pl.BlockSpec((None, HEADS, CHUNK, CHUNK), const4),
        pl.BlockSpec((None, CHUNK, HEADS), lambda i: (l, 0, 0)),
        ANY,
    ]
    out_specs = [
        pl.BlockSpec((tm, 3 * D), lambda i: (i, 0)),
        pl.BlockSpec((4, PG, PG), lambda i: (0, 0, 0)),
        pl.BlockSpec((8, D), lambda i: (0, 0)),
        pl.BlockSpec((HEADS, CHUNK, CHUNK), lambda i: (0, 0, 0)),
        pl.BlockSpec((CHUNK, HEADS), lambda i: (0, 0)),
        pl.BlockSpec((8, D), lambda i: (0, 0)),
    ]
    out_shape = [
        jax.ShapeDtypeStruct((T, 5 * D), BF16), jax.ShapeDtypeStruct((4, PG, PG), F32),
        jax.ShapeDtypeStruct((8, D), F32), jax.ShapeDtypeStruct((HEADS, CHUNK, CHUNK), F32),
        jax.ShapeDtypeStruct((CHUNK, HEADS), F32), jax.ShapeDtypeStruct((8, D), F32),
    ]
    res = pl.pallas_call(
        body, name=name, grid=(nt,), in_specs=in_specs + riding.in_specs(), out_specs=out_specs + riding.out_specs(),
        out_shape=out_shape + riding.out_shapes(),
        scratch_shapes=[pltpu.VMEM((tm, D), F32), pltpu.VMEM((tm, D), F32), pltpu.VMEM((CHUNK, D), F32)]
        + riding.scratch(),
        input_output_aliases={13: 0}, compiler_params=_params(1))(
            da, da, ds, z, z, z, z, wpool, scale3, g3, wsm, wsmT, bT, dz, *riding.arrays())
    return (*res[:6], list(res[6:]))


def _row_tile(rows, cols, sub):
    cap = max(sub, (2 * 1024 * 1024) // (4 * cols))
    best = None
    for tr in range(sub, min(rows, cap) + 1, sub):
        if rows % tr == 0:
            best = tr
    return best or rows


def elementwise(fn, ins, out_dtypes, *, name, row_blk_offs=None, rows=None):
    cols = ins[0].shape[1]
    rows = rows or ins[0].shape[0]
    tr = _row_tile(rows, cols, 16)
    offs = row_blk_offs or [0] * len(ins)
    n_in = len(ins)

    def body(*refs):
        outs = fn(*[r[...] for r in refs[:n_in]])
        for o_ref, o in zip(refs[n_in:], outs):
            o_ref[...] = o.astype(o_ref.dtype)

    return pl.pallas_call(
        body, name=name, grid=(rows // tr,),
        in_specs=[pl.BlockSpec((tr, cols), functools.partial(lambda i, o: (i + o * (rows // tr), 0), o=o))
                  for o in offs],
        out_specs=[pl.BlockSpec((tr, cols), lambda i: (i, 0)) for _ in out_dtypes],
        out_shape=[jax.ShapeDtypeStruct((rows, cols), dt) for dt in out_dtypes],
        compiler_params=_params(1))(*ins)


def _adamw(w, g, m, v):
    m = ADAM_B1 * m + (1.0 - ADAM_B1) * g
    v = ADAM_B2 * v + (1.0 - ADAM_B2) * jnp.square(g)
    m_hat = m / (1.0 - ADAM_B1 ** ADAM_STEP)
    v_hat = v / (1.0 - ADAM_B2 ** ADAM_STEP)
    delta = -ADAM_LR * (m_hat / (jnp.sqrt(v_hat) + ADAM_EPS) + ADAM_WD * w)
    return delta, m, v


def _view2d(a):
    return a.reshape(-1, a.shape[-1])


def _place():
    x, y, c = lax.axis_index("x"), lax.axis_index("y"), lax.axis_index("c")
    others = [(1 - x, y), (x, 1 - y), (1 - x, 1 - y)]
    return x, y, c, 2 * x + y, others


def _remote(src, dst, send_sems, recv_sems, k, to):
    return pltpu.make_async_remote_copy(src_ref=src, dst_ref=dst, send_sem=send_sems.at[k], recv_sem=recv_sems.at[k],
                                        device_id=to, device_id_type=MESH)


def _half(ref, axis, j, size, h):
    if len(ref.shape) == 3:
        return ref.at[:, pl.ds(j * size + h * (size // 2), size // 2), :]
    if axis == 0:
        return ref.at[pl.ds(j * size + h * (size // 2), size // 2), :]
    rows = ref.shape[0] // 2
    return ref.at[pl.ds(h * rows, rows), pl.ds(j * size, size)]


def _half_shard_shape(shape, axis, size):
    if len(shape) == 3:
        return (shape[0], size // 2, shape[2])
    if axis == 0:
        return (size // 2, shape[1])
    return (shape[0] // 2, size)


class Exchange:
    def __init__(self, arrays, out_shapes, aliases, n_sems, begin, finish):
        self.arrays, self.out_shapes, self.aliases, self.n_sems = list(arrays), list(out_shapes), aliases, n_sems
        self.begin, self.finish = begin, finish


class _Hosting:
    def __init__(self, plan):
        self.plan = list(plan or [])
        self.arrays = [a for ex in self.plan for a in ex.arrays]
        self.out_shapes = [o for ex in self.plan for o in ex.out_shapes]
        self.n_sems = sum(ex.n_sems for ex in self.plan)

    def scratch(self):
        return [pltpu.SemaphoreType.DMA((self.n_sems,)), pltpu.SemaphoreType.DMA((self.n_sems,))] if self.plan else []

    def aliases(self, in_base, out_base):
        out, i0, o0 = {}, in_base, out_base
        for ex in self.plan:
            out.update({i0 + i: o0 + o for i, o in ex.aliases.items()})
            i0, o0 = i0 + len(ex.arrays), o0 + len(ex.out_shapes)
        return out

    def _each(self, in_refs, out_refs):
        i0 = o0 = s0 = 0
        for ex in self.plan:
            yield ex, in_refs[i0:i0 + len(ex.arrays)], out_refs[o0:o0 + len(ex.out_shapes)], s0
            i0, o0, s0 = i0 + len(ex.arrays), o0 + len(ex.out_shapes), s0 + ex.n_sems

    def begin(self, in_refs, out_refs, send_sems, recv_sems):
        for ex, ins, outs, s0 in self._each(in_refs, out_refs):
            ex.begin(ins, outs, send_sems, recv_sems, s0)

    def finish(self, in_refs, out_refs, send_sems, recv_sems):
        for ex, ins, outs, s0 in self._each(in_refs, out_refs):
            ex.finish(ins, outs, send_sems, recv_sems, s0)


class _SideMatmuls:
    def __init__(self, jobs, T, steps, lin):
        self.jobs, self.lin = list(jobs or []), lin
        self.geo = []
        for a, b, ka, nb, b_col_blk, passes in self.jobs:
            per = steps // passes
            assert steps % passes == 0 and T % per == 0 and ka % passes == 0
            self.geo.append((per, T // per, ka // passes, nb, b_col_blk))

    def arrays(self):
        return [arr for a, b, *_ in self.jobs for arr in (a, b)]

    def in_specs(self):
        lin, specs = self.lin, []
        for per, rs, ka_t, nb, b_col_blk in self.geo:
            specs.append(pl.BlockSpec((rs, ka_t), functools.partial(
                lambda *g, per: (lin(*g) % per, lin(*g) // per), per=per)))
            specs.append(pl.BlockSpec((rs, nb), functools.partial(
                lambda *g, per, blk: (lin(*g) % per, blk), per=per, blk=b_col_blk)))
        return specs

    def out_specs(self):
        lin = self.lin
        return [pl.BlockSpec((None, ka_t, nb), functools.partial(lambda *g, per: (0, lin(*g) // per, 0), per=per))
                for per, rs, ka_t, nb, _ in self.geo]

    def out_shapes(self):
        return [jax.ShapeDtypeStruct((1, ka, nb), BF16) for a, b, ka, nb, *_ in self.jobs]

    def scratch(self):
        return [pltpu.VMEM((ka_t, nb), F32) for per, rs, ka_t, nb, _ in self.geo]

    def run(self, in_refs, out_refs, accs, step):
        for n, (per, rs, ka_t, nb, _) in enumerate(self.geo):
            a_ref, b_ref, o_ref, acc = in_refs[2 * n], in_refs[2 * n + 1], out_refs[n], accs[n]
            part = lax.dot_general(a_ref[...].astype(BF16), b_ref[...].astype(BF16), (((0,), (0,)), ((), ())),
                                   preferred_element_type=F32)
            q = step % per

            @pl.when(q == 0)
            def _():
                acc[...] = part

            @pl.when(q > 0)
            def _():
                acc[...] += part

            @pl.when(q == per - 1)
            def _():
                o_ref[...] = acc[...].astype(BF16)


def _first_last(grid):
    ids = [pl.program_id(a) for a in range(len(grid))]
    first = functools.reduce(jnp.logical_and, [i == 0 for i in ids])
    last = functools.reduce(jnp.logical_and, [i == g - 1 for i, g in zip(ids, grid)])
    return first, last


def run_exchanges(plan, *, name):
    host = _Hosting(plan)
    n_in, n_out = len(host.arrays), len(host.out_shapes)

    def body(*refs):
        ins, outs = refs[:n_in], refs[n_in:n_in + n_out]
        send_sems, recv_sems = refs[n_in + n_out:]
        host.begin(ins, outs, send_sems, recv_sems)
        host.finish(ins, outs, send_sems, recv_sems)

    return pl.pallas_call(
        body, name=name, in_specs=[ANY] * n_in, out_specs=[ANY] * n_out, out_shape=host.out_shapes,
        scratch_shapes=host.scratch(), input_output_aliases=host.aliases(0, 0),
        compiler_params=pltpu.CompilerParams(has_side_effects=True))(*host.arrays)


def place_shard(src, l, axis, size, out_dtype, place, *, name):
    shard = src.shape[1:]
    natural = tuple(size * N_CHIPS if a == axis else s for a, s in enumerate(shard))
    if len(shard) == 3:
        blk = (None,) + shard
        grid = (1,)
        in_map = lambda i, pr: (l, 0, 0, 0)
        out_map = lambda i, pr: (0, 0, pr[0], 0)
    else:
        tr = _row_tile(shard[0], shard[1], 16)
        steps = shard[0] // tr
        blk = (None, tr, shard[1])
        grid = (steps,)
        in_map = lambda i, pr: (l, i, 0)
        if axis == 0:
            out_map = lambda i, pr: (0, pr[0] * steps + i, 0)
        else:
            out_map = lambda i, pr: (0, i, pr[0])

    def body(pr_ref, s_ref, o_ref):
        del pr_ref
        o_ref[...] = s_ref[...].astype(o_ref.dtype)

    return pl.pallas_call(
        body, name=name,
        grid_spec=pltpu.PrefetchScalarGridSpec(
            num_scalar_prefetch=1, grid=grid, in_specs=[pl.BlockSpec(blk, in_map)],
            out_specs=pl.BlockSpec(blk, out_map)),
        out_shape=jax.ShapeDtypeStruct((1,) + natural, out_dtype), compiler_params=_params(1))(place, src)


def place_both_layers(src, axis, size, place, *, name):
    rows, cols = src.shape[1], src.shape[2]

    def body(pr_ref, s_ref, o_ref):
        del pr_ref
        o_ref[...] = s_ref[...]

    return pl.pallas_call(
        body, name=name,
        grid_spec=pltpu.PrefetchScalarGridSpec(
            num_scalar_prefetch=1, grid=(2,), in_specs=[pl.BlockSpec((None, rows, cols), lambda lyr, pr: (lyr, 0, 0))],
            out_specs=pl.BlockSpec((None, rows, cols), lambda lyr, pr: (lyr, 0, pr[0]))),
        out_shape=jax.ShapeDtypeStruct((2, rows, cols * N_CHIPS), src.dtype), compiler_params=_params(1))(place, src)


def gather_exchange(arrays, geom):
    n = len(arrays)

    def begin(ins, outs, send_sems, recv_sems, s0):
        x, y, c, j, others = _place()
        for t, (axis, size) in enumerate(geom):
            mine = _half(outs[t].at[0], axis, j, size, c)
            for k, (ox, oy) in enumerate(others):
                _remote(mine, mine, send_sems, recv_sems, s0 + 6 * t + k, (ox, oy, c)).start()

    def finish(ins, outs, send_sems, recv_sems, s0):
        x, y, c, j, others = _place()
        sib = (x, y, 1 - c)
        passed = []
        for t, (axis, size) in enumerate(geom):
            for k, (ox, oy) in enumerate(others):
                landed = _half(outs[t].at[0], axis, 2 * ox + oy, size, c)
                _remote(landed, landed, send_sems, recv_sems, s0 + 6 * t + k, (ox, oy, c)).wait_recv()
                fwd = _remote(landed, landed, send_sems, recv_sems, s0 + 6 * t + 3 + k, sib)
                fwd.start()
                passed.append(fwd)
        for t, (axis, size) in enumerate(geom):
            for k, (ox, oy) in enumerate(others):
                got = _half(outs[t].at[0], axis, 2 * ox + oy, size, 1 - c)
                _remote(got, got, send_sems, recv_sems, s0 + 6 * t + 3 + k, sib).wait_recv()
        for fwd in passed:
            fwd.wait_send()
        for t, (axis, size) in enumerate(geom):
            mine = _half(outs[t].at[0], axis, j, size, c)
            for k, (ox, oy) in enumerate(others):
                _remote(mine, mine, send_sems, recv_sems, s0 + 6 * t + k, (ox, oy, c)).wait_send()

    return Exchange(arrays, [jax.ShapeDtypeStruct(a.shape, a.dtype) for a in arrays], {t: t for t in range(n)},
                    6 * n, begin, finish)


def gather_by_layer_exchange(array, axis, size):
    def blocks(out, others, lyr):
        return [_block(out.at[lyr], axis, 2 * ox + oy, size) for (ox, oy) in others]

    def begin(ins, outs, send_sems, recv_sems, s0):
        x, y, c, j, others = _place()
        mine = _block(outs[0].at[c], axis, j, size)
        for k, (ox, oy) in enumerate(others):
            _remote(mine, mine, send_sems, recv_sems, s0 + k, (ox, oy, c)).start()

    def finish(ins, outs, send_sems, recv_sems, s0):
        x, y, c, j, others = _place()
        sib = (x, y, 1 - c)
        passed = []
        for k, ((ox, oy), landed) in enumerate(zip(others, blocks(outs[0], others, c))):
            _remote(landed, landed, send_sems, recv_sems, s0 + k, (ox, oy, c)).wait_recv()
            fwd = _remote(landed, landed, send_sems, recv_sems, s0 + 3 + k, sib)
            fwd.start()
            passed.append(fwd)
        for k, got in enumerate(blocks(outs[0], others, 1 - c)):
            _remote(got, got, send_sems, recv_sems, s0 + 3 + k, sib).wait_recv()
        for fwd in passed:
            fwd.wait_send()
        mine = _block(outs[0].at[c], axis, j, size)
        for k, (ox, oy) in enumerate(others):
            _remote(mine, mine, send_sems, recv_sems, s0 + k, (ox, oy, c)).wait_send()

    return Exchange([array], [jax.ShapeDtypeStruct(array.shape, array.dtype)], {0: 0}, 6, begin, finish)


def swap_exchange(grads, geom):
    def pieces(t, g, dst, h):
        axis, size = geom[t]
        if len(g.shape) == 2 and axis == 1:
            rows = g.shape[0] // 2
            return [(g.at[pl.ds(h * rows, rows), :], dst)]
        return [(_half(g, axis, jb, size, h), dst.at[jb]) for jb in range(N_CHIPS)]

    counts = [1 if (len(g.shape) == 3 and a == 1) else N_CHIPS for g, (a, _) in zip(grads, geom)]
    bases = [sum(counts[:t]) for t in range(len(grads))]

    def copies(ins, outs, send_sems, recv_sems, s0):
        x, y, c, _, _ = _place()
        cps = []
        for t in range(len(grads)):
            for q, (src, dst) in enumerate(pieces(t, ins[t].at[0], outs[t], 1 - c)):
                cps.append(_remote(src, dst, send_sems, recv_sems, s0 + bases[t] + q, (x, y, 1 - c)))
        return cps

    def begin(*a):
        for cp in copies(*a):
            cp.start()

    def finish(*a):
        for cp in copies(*a):
            cp.wait()

    out_shapes = []
    for g, (axis, size) in zip(grads, geom):
        shp = g.shape[1:]
        if len(shp) == 2 and axis == 1:
            out_shapes.append(jax.ShapeDtypeStruct((shp[0] // 2, shp[1]), g.dtype))
        else:
            out_shapes.append(jax.ShapeDtypeStruct((N_CHIPS,) + _half_shard_shape(shp, axis, size), g.dtype))
    return Exchange(grads, out_shapes, {}, sum(counts), begin, finish)


def scatter_exchange(parts, geom, shapes):
    def copies(ins, outs, send_sems, recv_sems, s0):
        x, y, c, j, others = _place()
        cps = []
        for t, ((axis, size), shp) in enumerate(zip(geom, shapes)):
            for k, (ox, oy) in enumerate(others):
                jp = 2 * ox + oy
                src = ins[t].at[:, pl.ds(jp * size, size)] if (len(shp) == 2 and axis == 1) else ins[t].at[jp]
                cps.append(_remote(src, outs[t].at[k], send_sems, recv_sems, s0 + 3 * t + k, (ox, oy, c)))
        return cps

    def begin(*a):
        for cp in copies(*a):
            cp.start()

    def finish(*a):
        for cp in copies(*a):
            cp.wait_recv()
        for cp in copies(*a):
            cp.wait_send()

    out_shapes = [jax.ShapeDtypeStruct((3,) + _half_shard_shape(shp, axis, size), p.dtype)
                  for p, (axis, size), shp in zip(parts, geom, shapes)]
    return Exchange(parts, out_shapes, {}, 3 * len(parts), begin, finish)


def share_exchange(grads):
    n = 2 * len(grads)

    def my_half(refs, t, h):
        lyr = refs[t // 2].at[t % 2]
        if len(lyr.shape) == 3:
            rows = lyr.shape[1] // 2
            return lyr.at[:, pl.ds(h * rows, rows), :]
        rows = lyr.shape[0] // 2
        return lyr.at[pl.ds(h * rows, rows), :]

    def begin(ins, outs, send_sems, recv_sems, s0):
        x, y, c, _, _ = _place()
        for t in range(n):
            mine = my_half(outs, t, c)
            _remote(mine, mine, send_sems, recv_sems, s0 + t, (x, y, 1 - c)).start()

    def finish(ins, outs, send_sems, recv_sems, s0):
        x, y, c, _, _ = _place()
        for t in range(n):
            got = my_half(outs, t, 1 - c)
            _remote(got, got, send_sems, recv_sems, s0 + t, (x, y, 1 - c)).wait_recv()
        for t in range(n):
            mine = my_half(outs, t, c)
            _remote(mine, mine, send_sems, recv_sems, s0 + t, (x, y, 1 - c)).wait_send()

    return Exchange(grads, [jax.ShapeDtypeStruct(g.shape, g.dtype) for g in grads],
                    {t: t for t in range(len(grads))}, n, begin, finish)


def all_reduce_small(s):
    rows = s.shape[0]
    half = rows // 2
    assert half % 8 == 0

    def body(s_ref, o_ref, a_ref, b_ref, p_ref, send_sems, recv_sems):
        x, y, c, j, others = _place()
        sib = (x, y, 1 - c)
        swap = _remote(s_ref, a_ref, send_sems, recv_sems, 0, sib)
        swap.start()
        swap.wait()
        p_ref[...] = s_ref[...] + a_ref[...]
        mine = pl.ds(pl.multiple_of(c * half, 8), half)
        b_ref[j] = p_ref[mine, :]
        cps = [_remote(p_ref.at[mine, :], b_ref.at[j], send_sems, recv_sems, 1 + k, (ox, oy, c))
               for k, (ox, oy) in enumerate(others)]
        for cp in cps:
            cp.start()
        for k, (ox, oy) in enumerate(others):
            slot = b_ref.at[2 * ox + oy]
            _remote(slot, slot, send_sems, recv_sems, 1 + k, (ox, oy, c)).wait_recv()
        for cp in cps:
            cp.wait_send()
        o_ref[mine, :] = ((b_ref[0] + b_ref[1]) + b_ref[2]) + b_ref[3]
        back = _remote(o_ref.at[mine, :], o_ref.at[mine, :], send_sems, recv_sems, 4, sib)
        back.start()
        back.wait_send()
        theirs = pl.ds(pl.multiple_of((1 - c) * half, 8), half)
        _remote(o_ref.at[theirs, :], o_ref.at[theirs, :], send_sems, recv_sems, 4, sib).wait_recv()

    vmem = pl.BlockSpec(memory_space=pltpu.VMEM)
    return pl.pallas_call(
        body, name="all_reduce_small", in_specs=[vmem], out_specs=vmem,
        out_shape=jax.ShapeDtypeStruct((rows, LANES), F32),
        scratch_shapes=[pltpu.VMEM((rows, LANES), F32), pltpu.VMEM((N_CHIPS, half, LANES), F32),
                        pltpu.VMEM((rows, LANES), F32), pltpu.SemaphoreType.DMA((5,)),
                        pltpu.SemaphoreType.DMA((5,))],
        compiler_params=pltpu.CompilerParams(vmem_limit_bytes=VMEM_LIMIT, has_side_effects=True))(s)


def pair_sum(g, got, axis, size, place, *, name):
    shp = g.shape[1:]
    if len(shp) == 3:
        hs = size // 2
        grid = (N_CHIPS,)
        g_spec = pl.BlockSpec((None, shp[0], hs, shp[2]), lambda jb, pr: (0, 0, 2 * jb + pr[1], 0))
        r_spec = pl.BlockSpec((None, shp[0], hs, shp[2]), lambda jb, pr: (jb, 0, 0, 0))
    elif axis == 0:
        hs = size // 2
        tr = _row_tile(hs, shp[1], 16)
        steps = hs // tr
        grid = (N_CHIPS, steps)
        g_spec = pl.BlockSpec((None, tr, shp[1]), lambda jb, i, pr: (0, (2 * jb + pr[1]) * steps + i, 0))
        r_spec = pl.BlockSpec((None, tr, shp[1]), lambda jb, i, pr: (jb, i, 0))
    else:
        rows = shp[0] // 2
        tr = _row_tile(rows, shp[1], 16)
        steps = rows // tr
        grid = (steps,)
        g_spec = pl.BlockSpec((None, tr, shp[1]), lambda i, pr: (0, pr[1] * steps + i, 0))
        r_spec = pl.BlockSpec((tr, shp[1]), lambda i, pr: (i, 0))

    def body(pr_ref, g_ref, r_ref, o_ref):
        del pr_ref
        o_ref[...] = (g_ref[...].astype(F32) + r_ref[...].astype(F32)).astype(BF16)

    return pl.pallas_call(
        body, name=name,
        grid_spec=pltpu.PrefetchScalarGridSpec(num_scalar_prefetch=1, grid=grid, in_specs=[g_spec, r_spec],
                                               out_specs=r_spec),
        out_shape=jax.ShapeDtypeStruct(got.shape, BF16), compiler_params=_params(len(grid)))(place, g, got)


def chip_sum(part, slots, shp, axis, size, l, place, out, *, name):
    shard = _shard_shape(shp, axis, size)
    hshape = slots.shape[1:]
    if len(shp) == 3:
        grid = (1,)
        p_spec = pl.BlockSpec((None,) + hshape, lambda i, pr: (pr[0], 0, 0, 0))
        s_specs = [pl.BlockSpec((None,) + hshape, functools.partial(lambda i, pr, k: (k, 0, 0, 0), k=k))
                   for k in range(3)]
        o_spec = pl.BlockSpec((None,) + hshape, lambda i, pr: (l, 0, pr[1], 0))
    else:
        tr = _row_tile(hshape[0], hshape[1], 16)
        steps = hshape[0] // tr
        grid = (steps,)
        if axis == 0:
            p_spec = pl.BlockSpec((None, tr, hshape[1]), lambda i, pr: (pr[0], i, 0))
        else:
            p_spec = pl.BlockSpec((tr, hshape[1]), lambda i, pr: (i, pr[0]))
        s_specs = [pl.BlockSpec((None, tr, hshape[1]), functools.partial(lambda i, pr, k: (k, i, 0), k=k))
                   for k in range(3)]
        o_spec = pl.BlockSpec((None, tr, hshape[1]), lambda i, pr: (l, pr[1] * steps + i, 0))
    has_out = out is not None

    def body(pr_ref, p_ref, s0_ref, s1_ref, s2_ref, *rest):
        del pr_ref
        rest[-1][...] = ((p_ref[...].astype(F32) + s0_ref[...].astype(F32)) + s1_ref[...].astype(F32)) \
            + s2_ref[...].astype(F32)

    return pl.pallas_call(
        body, name=name,
        grid_spec=pltpu.PrefetchScalarGridSpec(
            num_scalar_prefetch=1, grid=grid, in_specs=[p_spec] + s_specs + ([ANY] if has_out else []),
            out_specs=o_spec),
        out_shape=jax.ShapeDtypeStruct((2,) + shard, F32), input_output_aliases={5: 0} if has_out else {},
        compiler_params=_params(1))(place, part, slots, slots, slots, *([out] if has_out else []))


GEOM = {name: (axis, size) for (name, _, axis, size) in BIG}
SHAPE = {name: shape for (name, shape, _, _) in BIG}
RIDES_IN_PROJ_L0 = ((0, ("w_pool", "w_branch_a", "w_branch_b", "w_out", "w_up")),)
RIDES_UP_PROJ_L0 = ((0, ("w_down", "w_ple_gate", "w_ple")), (1, ("w_in",)))
RIDES_DOWN_PROJ_L0 = ((1, ("w_pool", "w_branch_a", "w_branch_b", "w_out")),)
RIDES_IN_PROJ_L1 = ((1, ("w_up", "w_down", "w_ple_gate", "w_ple")),)
EARLY_GRADS_L0 = ("w_ple", "w_ple_gate", "w_down", "w_up")
LATE_GRADS_L0 = ("w_out", "w_branch_a", "w_branch_b", "w_pool", "w_in")


def _reduce_start(G, names, place, tag):
    geom = [GEOM[k] for k in names]
    got = run_exchanges([swap_exchange([G[k] for k in names], geom)], name=f"swap_halves_{tag}")
    parts = [pair_sum(G[k], r, *GEOM[k], place, name=f"pair_sum_{k}_{tag}") for k, r in zip(names, got)]
    return scatter_exchange(parts, geom, [SHAPE[k] for k in names]), parts


def _reduce_end(names, parts, slots, place, l, reduced):
    for k, q, s in zip(names, parts, slots):
        reduced[k] = chip_sum(q, s, SHAPE[k], *GEOM[k], l, place, reduced.get(k), name=f"chip_sum_{k}_l{l}")


def _local_step(x, p2, tgt, W0, W1, conv_w, small, place):
    T = x.shape[0]
    as3 = lambda a: a.reshape(2, 1, a.shape[-1])
    mix3, scale3, sgu3 = as3(small["mix_norm"]), as3(small["pool_scale"]), as3(small["sgu_norm"])
    ffn3, ple3, convb3 = as3(small["ffn_norm"]), as3(small["ple_norm"]), as3(small["conv_b"])
    tril = jnp.tril(jnp.ones((CHUNK, CHUNK), F32))
    ws_masked = small["w_spatial"] * tril
    wsm = ws_masked.astype(BF16)
    wsmT = jnp.swapaxes(ws_masked, -1, -2).astype(BF16)
    bT = jnp.swapaxes(small["b_spatial"], -1, -2)
    final3 = small["final_norm"].reshape(1, D)
    W = [dict(W0), dict(W1)]

    def riders(groups):
        return [gather_exchange([W[lyr][k] for k in names], [GEOM[k] for k in names]) for lyr, names in groups]

    def landed(groups, got):
        for lyr, names in groups:
            W[lyr].update(zip(names, got[:len(names)]))
            got = got[len(names):]

    saved = []
    hb = norm_fwd(x, mix3, 0, name="mix_norm_fwd_l0")
    for l in range(2):
        n = lambda s: f"{s}_l{l}"
        Wl = W[l]
        groups = RIDES_IN_PROJ_L0 if l == 0 else RIDES_IN_PROJ_L1
        z, got = mm_nn(hb, Wl["w_in"], 0, name=n("in_proj"), rows=T, tn=1280, out_dtype=BF16, host=riders(groups))
        landed(groups, got)
        a_in = pool_fwd(z, Wl["w_pool"], scale3, l, name=n("pool_fwd"))
        s_in = sgu_fwd(z, sgu3, wsm, bT, l, name=n("sgu_fwd"))
        yab = mm_nn(a_in, Wl["w_branch_a"], 0, name=n("branch_a"), rows=T, out_cols=2 * D, out_dtype=BF16)
        yab = mm_nn(s_in, Wl["w_branch_b"], 0, name=n("branch_b"), rows=T, out=yab, out_cols=2 * D, out_col_off=D,
                    out_dtype=BF16)
        mo = gate_fwd(z, yab, name=n("gate_fwd"))
        x1, h2b = mm_nn(mo, Wl["w_out"], 0, name=n("out_proj"), rows=T, resid=x, norm_gain=ffn3[l:l + 1])
        if l == 0:
            up, got = mm_nn(h2b, Wl["w_up"], 0, name=n("up_proj"), rows=T, tn=DFF, host=riders(RIDES_UP_PROJ_L0))
            landed(RIDES_UP_PROJ_L0, got)
        else:
            up = mm_nn(h2b, Wl["w_up"], 0, name=n("up_proj"), rows=T, tn=DFF)
        f = conv_fwd(up, conv_w, convb3, l, name=n("conv_fwd"))
        if l == 0:
            (x2, h3b), got = mm_nn(f, Wl["w_down"], 0, name=n("down_proj"), rows=T, resid=x1,
                                   norm_gain=ple3[l:l + 1], host=riders(RIDES_DOWN_PROJ_L0))
            landed(RIDES_DOWN_PROJ_L0, got)
        else:
            x2, h3b = mm_nn(f, Wl["w_down"], 0, name=n("down_proj"), rows=T, resid=x1, norm_gain=ple3[l:l + 1])
        pg = mm_nn(h3b, Wl["w_ple_gate"], 0, name=n("ple_gate_proj"), rows=T, out_dtype=BF16)
        e = mm_nn(p2, Wl["w_ple"], 0, name=n("ple_proj"), rows=T, a_row_off=l * T, out_dtype=BF16)
        saved.append(dict(x=x, hb=hb, z=z, a_in=a_in, s_in=s_in, yab=yab, mo=mo, x1=x1, h2b=h2b, up=up, f=f,
                          x2=x2, h3b=h3b, pg=pg, e=e))
        if l == 0:
            x, hb = ple_fwd(x2, pg, e, mix3, 1, name=n("ple_fwd"))
        else:
            x = ple_fwd(x2, pg, e, None, 0, name=n("ple_fwd"))

    loss_acc, dx, dg_final = loss_head(x, final3, tgt, name="loss_head")

    small_grads = [None, None]
    all_names = [t[0] for t in BIG]
    scatter1 = parts1 = slots1 = None
    for l in (1, 0):
        n = lambda s: f"{s}_l{l}"
        a, Wl, G = saved[l], W[l], {}
        de, dpg = ple_bwd(dx, a["pg"], a["e"], name=n("ple_bwd"))
        G["w_ple"] = mm_tn(p2, de, name=n("d_w_ple"), rows=T, ka=PG, nb=D, a_row_off=l * T)
        G["w_ple_gate"] = mm_tn(a["h3b"], dpg, name=n("d_w_ple_gate"), rows=T, ka=D, nb=D)
        dx2, dg_ple = mm_nt(dpg, Wl["w_ple_gate"], 0, name=n("ple_norm_bwd"), rows=T,
                            norm_bwd_of=(a["x2"], ple3, l, dx))
        df = mm_nt(dx2, Wl["w_down"], 0, name=n("d_ffn_act"), rows=T, out_dtype=F32)
        dup, dcw, dcb, got, (G["w_down"],) = conv_bwd(
            df, a["up"], conv_w, convb3, l, name=n("conv_bwd"), host=[scatter1] if l == 0 else None,
            side=[(a["f"], dx2, DFF, D, 0, 2)])
        if l == 0:
            slots1 = got
        G["w_up"] = mm_tn(a["h2b"], dup, name=n("d_w_up"), rows=T, ka=D, nb=2 * DFF, tn=DFF, tk=1024)
        if l == 0:
            scatter_early, parts_early = _reduce_start(G, EARLY_GRADS_L0, place, "l0_early")
            (dx1, dg_ffn), slots_early = mm_nt(dup, Wl["w_up"], 0, name=n("ffn_norm_bwd"), rows=T, tk=1408,
                                               norm_bwd_of=(a["x1"], ffn3, l, dx2), host=[scatter_early])
        else:
            dx1, dg_ffn = mm_nt(dup, Wl["w_up"], 0, name=n("ffn_norm_bwd"), rows=T, tk=1408,
                                norm_bwd_of=(a["x1"], ffn3, l, dx2))
        dmo = mm_nt(dx1, Wl["w_out"], 0, name=n("d_gated"), rows=T)
        G["w_out"] = mm_tn(a["mo"], dx1, name=n("d_w_out"), rows=T, ka=D, nb=D)
        dz, dyab = gate_bwd(dmo, a["z"], a["yab"], name=n("gate_bwd"))
        da = mm_nt(dyab, Wl["w_branch_a"], 0, name=n("d_pool_out"), rows=T, kdim=D)
        ds = mm_nt(dyab, Wl["w_branch_b"], 0, name=n("d_sgu_out"), rows=T, kdim=D, a_col_off=D)
        dz, dwp, dsc, dws, dbt, dgs, (G["w_branch_a"], G["w_branch_b"]) = mixer_bwd(
            da, ds, a["z"], dz, Wl["w_pool"], scale3, sgu3, wsm, wsmT, bT, l, name=n("mixer_bwd"),
            side=[(a["a_in"], dyab, D, D, 0, 1), (a["s_in"], dyab, D, D, 1, 1)])
        G["w_pool"] = dwp.astype(BF16)[None]
        G["w_in"] = mm_tn(a["hb"], dz, name=n("d_w_in"), rows=T, ka=D, nb=5 * D, tn=1280)
        if l == 0:
            scatter_late, parts_late = _reduce_start(G, LATE_GRADS_L0, place, "l0_late")
            (dx, dg_mix), slots_late = mm_nt(dz, Wl["w_in"], 0, name=n("mix_norm_bwd"), rows=T, tk=1280,
                                             norm_bwd_of=(a["x"], mix3, l, dx1), host=[scatter_late])
        else:
            dx, dg_mix = mm_nt(dz, Wl["w_in"], 0, name=n("mix_norm_bwd"), rows=T, tk=1280,
                               norm_bwd_of=(a["x"], mix3, l, dx1))
            scatter1, parts1 = _reduce_start(G, all_names, place, "l1")
        small_grads[l] = dict(
            mix_norm=dg_mix[0], pool_scale=dsc[0], sgu_norm=dgs[0], w_spatial=dws, b_spatial=dbt.T,
            ffn_norm=dg_ffn[0], conv_b=jnp.concatenate([dcb[0, 0], dcb[1, 0]]), ple_norm=dg_ple[0],
            conv_w=jnp.concatenate([dcw[0, :3], dcw[1, :3]], axis=1))
    reduced = {}
    _reduce_end(all_names, parts1, slots1, place, 1, reduced)
    _reduce_end(EARLY_GRADS_L0, parts_early, slots_early, place, 0, reduced)
    _reduce_end(LATE_GRADS_L0, parts_late, slots_late, place, 0, reduced)
    return loss_acc, dx, reduced, small_grads, dg_final[0]


SMALL_ORDER = ("mix_norm", "pool_scale", "sgu_norm", "w_spatial", "b_spatial", "ffn_norm", "conv_b", "ple_norm",
               "conv_w")


def _pack_rows(pieces, row_multiple):
    flat = jnp.concatenate([a.reshape(-1) for a in pieces])
    rows = -(-flat.shape[0] // LANES)
    rows = -(-rows // row_multiple) * row_multiple
    return jnp.pad(flat, (0, rows * LANES - flat.shape[0])).reshape(rows, LANES)


def _unpack(flat, shapes):
    out, off = [], 0
    for shp in shapes:
        size = 1
        for s in shp:
            size *= s
        out.append(flat[off:off + size].reshape(shp))
        off += size
    return out


def kernel(x, p, mix_norm, w_in, w_pool, pool_scale, sgu_norm, w_spatial, b_spatial, w_branch_a, w_branch_b, w_out, ffn_norm, w_up, conv_w, conv_b, w_down, ple_norm, w_ple_gate, w_ple, final_norm, loss_target, m_mix_norm, m_w_in, m_w_pool, m_pool_scale, m_sgu_norm, m_w_spatial, m_b_spatial, m_w_branch_a, m_w_branch_b, m_w_out, m_ffn_norm, m_w_up, m_conv_w, m_conv_b, m_w_down, m_ple_norm, m_w_ple_gate, m_w_ple, m_final_norm, v_mix_norm, v_w_in, v_w_pool, v_pool_scale, v_sgu_norm, v_w_spatial, v_b_spatial, v_w_branch_a, v_w_branch_b, v_w_out, v_ffn_norm, v_w_up, v_conv_w, v_conv_b, v_w_down, v_ple_norm, v_w_ple_gate, v_w_ple, v_final_norm):
    names = ["mix_norm", "w_in", "w_pool", "pool_scale", "sgu_norm", "w_spatial", "b_spatial", "w_branch_a",
             "w_branch_b", "w_out", "ffn_norm", "w_up", "conv_w", "conv_b", "w_down", "ple_norm", "w_ple_gate",
             "w_ple", "final_norm"]
    w = dict(zip(names, [mix_norm, w_in, w_pool, pool_scale, sgu_norm, w_spatial, b_spatial, w_branch_a, w_branch_b,
                         w_out, ffn_norm, w_up, conv_w, conv_b, w_down, ple_norm, w_ple_gate, w_ple, final_norm]))
    m = dict(zip(names, [m_mix_norm, m_w_in, m_w_pool, m_pool_scale, m_sgu_norm, m_w_spatial, m_b_spatial,
                         m_w_branch_a, m_w_branch_b, m_w_out, m_ffn_norm, m_w_up, m_conv_w, m_conv_b, m_w_down,
                         m_ple_norm, m_w_ple_gate, m_w_ple, m_final_norm]))
    v = dict(zip(names, [v_mix_norm, v_w_in, v_w_pool, v_pool_scale, v_sgu_norm, v_w_spatial, v_b_spatial,
                         v_w_branch_a, v_w_branch_b, v_w_out, v_ffn_norm, v_w_up, v_conv_w, v_conv_b, v_w_down,
                         v_ple_norm, v_w_ple_gate, v_w_ple, v_final_norm]))
    T = x.shape[1]
    chip = 2 * lax.axis_index("x") + lax.axis_index("y")
    place = jnp.stack([chip, lax.axis_index("c")]).astype(jnp.int32)

    big_names = [t[0] for t in BIG]
    placed = [{k: place_shard(w[k], l, *GEOM[k], BF16, place, name=f"place_{k}_l{l}") for k in big_names}
              for l in range(2)]
    conv_w8 = jnp.pad(conv_w, ((0, 0), (0, CONV_ROWS - conv_w.shape[1]), (0, 0)))
    conv_placed = place_both_layers(conv_w8, 1, conv_w.shape[2], place, name="place_conv_w")
    w_in0, conv_w_all = run_exchanges([gather_exchange([placed[0]["w_in"]], [GEOM["w_in"]]),
                                       gather_by_layer_exchange(conv_placed, 1, conv_w.shape[2])],
                                      name="gather_first_weights")
    placed[0]["w_in"] = w_in0

    small = {k: w[k] for k in ("mix_norm", "pool_scale", "sgu_norm", "w_spatial", "b_spatial", "ffn_norm",
                               "conv_b", "ple_norm", "final_norm")}
    loss_acc, dx, reduced, small_grads, dg_final = _local_step(
        x.reshape(T, D), p.reshape(2 * T, p.shape[-1]), loss_target.reshape(T, D), placed[0], placed[1], conv_w_all,
        small, place)
    loss = lax.psum(loss_acc[0, 0], ("x", "y", "c"))
    full = run_exchanges([share_exchange([reduced[k] for k in big_names])], name="share_halves")
    grads = dict(zip(big_names, full))

    pieces = [small_grads[l][k] for l in range(2) for k in SMALL_ORDER] + [dg_final]
    shapes = [a.shape for a in pieces]
    total = all_reduce_small(_pack_rows(pieces, 16)).reshape(-1)
    summed = _unpack(total, shapes)
    per_layer = {k: jnp.stack([summed[i], summed[len(SMALL_ORDER) + i]]) for i, k in enumerate(SMALL_ORDER)}
    for k in ("mix_norm", "pool_scale", "sgu_norm", "w_spatial", "b_spatial", "ffn_norm", "conv_b", "ple_norm"):
        grads[k] = per_layer[k]
    grads["final_norm"] = summed[-1]
    cw = conv_w.shape[2]
    grads["conv_w"] = lax.dynamic_slice_in_dim(per_layer["conv_w"], chip * cw, cw, axis=2)

    delta, new_m, new_v = {}, {}, {}
    for name in big_names:
        shp = w[name].shape
        d_, m_, v_ = elementwise(_adamw, [_view2d(a) for a in (w[name], grads[name], m[name], v[name])],
                                 [F32, F32, F32], name=f"adamw_{name}")
        delta[name], new_m[name], new_v[name] = d_.reshape(shp), m_.reshape(shp), v_.reshape(shp)
    small_names = [k for k in names if k not in big_names]
    small_shapes = [w[k].shape for k in small_names]
    packed = [_pack_rows([src[k] for k in small_names], 8) for src in (w, grads, m, v)]
    outs = elementwise(_adamw, packed, [F32, F32, F32], name="adamw_small")
    for dst, o in zip((delta, new_m, new_v), outs):
        for k, a in zip(small_names, _unpack(o.reshape(-1), small_shapes)):
            dst[k] = a

    return (loss, dx.reshape(1, T, D), *[grads[k] for k in names], *[delta[k] for k in names],
            *[new_m[k] for k in names], *[new_v[k] for k in names])
```

```python
import functools

import jax
import jax.numpy as jnp
from jax import lax
from jax.experimental import pallas as pl
from jax.experimental.pallas import tpu as pltpu

F32 = jnp.float32
BF16 = jnp.bfloat16
EPS = 1e-6
D = 1024
POOL_WINDOWS = (2, 4, 8, 16)
PG = 256
POOL_HALO = 16
CHUNK = 128
HEADS = 8
DFF = 2816
CONV_HALO = 8
CONV_TC = 1408
N_CHIPS = 4
LANES = 128
VMEM_LIMIT = 56 * 1024 * 1024
MESH = pl.DeviceIdType.MESH
ANY = pl.BlockSpec(memory_space=pl.ANY)

ADAM_LR = 0.001
ADAM_B1 = 0.9
ADAM_B2 = 0.999
ADAM_EPS = 1e-08
ADAM_WD = 0.01
ADAM_STEP = 10

BIG = (
    ("w_in", (D, 5 * D), 1, 5 * D // N_CHIPS),
    ("w_pool", (4, PG, PG), 1, PG // N_CHIPS),
    ("w_branch_a", (D, D), 0, D // N_CHIPS),
    ("w_branch_b", (D, D), 0, D // N_CHIPS),
    ("w_out", (D, D), 0, D // N_CHIPS),
    ("w_up", (D, 2 * DFF), 1, 2 * DFF // N_CHIPS),
    ("w_down", (DFF, D), 0, DFF // N_CHIPS),
    ("w_ple_gate", (D, D), 0, D // N_CHIPS),
    ("w_ple", (PG, D), 1, D // N_CHIPS),
)
CONV_ROWS = 8


def _params(n_axes):
    return pltpu.CompilerParams(dimension_semantics=("arbitrary",) * n_axes, vmem_limit_bytes=VMEM_LIMIT)


def _gelu(x):
    return 0.5 * x * (1.0 + lax.erf(x * 0.7071067811865476))


def _gelu_grad(x):
    return 0.5 * (1.0 + lax.erf(x * 0.7071067811865476)) + x * jnp.exp(-0.5 * x * x) * 0.3989422804014327


def _shard_shape(shape, axis, size):
    return tuple(size if a == axis else s for a, s in enumerate(shape))


def _block(ref, axis, j, size):
    idx = tuple(pl.ds(j * size, size) if a == axis else slice(None) for a in range(len(ref.shape)))
    return ref.at[idx]


def mm_nn(a, w, l, *, name, rows, out_dtype=F32, resid=None, a_row_off=0, out=None, out_cols=None,
          out_col_off=0, norm_gain=None, host=None, tm=1024, tn=None, tk=None):
    K, N = w.shape[1], w.shape[2]
    tn = tn or N
    tk = tk or K
    nk = K // tk
    out_cols = out_cols or N
    assert rows % tm == 0 and N % tn == 0 and K % tk == 0 and out_col_off % tn == 0 and a_row_off % tm == 0
    has_resid, has_out, has_norm = resid is not None, out is not None, norm_gain is not None
    assert not has_norm or (tn == N and not has_out)
    grid = (N // tn, rows // tm, nk)
    hosting = _Hosting(host)
    n_in = 2 + has_resid + has_norm + has_out
    n_host_in, n_host_out = len(hosting.arrays), len(hosting.out_shapes)
    n_own_out = 1 + has_norm

    def body(*refs):
        refs = list(refs)
        a_ref, w_ref = refs[0], refs[1]
        r_ref = refs[2] if has_resid else None
        g_ref = refs[2 + has_resid] if has_norm else None
        host_in = refs[n_in:n_in + n_host_in]
        o_base = n_in + n_host_in
        o_ref = refs[o_base]
        host_out = refs[o_base + n_own_out:o_base + n_own_out + n_host_out]
        scratch = refs[o_base + n_own_out + n_host_out:]
        if hosting.plan:
            first, last = _first_last(grid)
            sems = scratch[-2:]

            @pl.when(first)
            def _():
                hosting.begin(host_in, host_out, *sems)

        part = jnp.dot(a_ref[...].astype(BF16), w_ref[...], preferred_element_type=F32)

        def finish(r):
            if has_resid:
                r = r + r_ref[...]
            o_ref[...] = r.astype(o_ref.dtype)
            if has_norm:
                scale = lax.rsqrt(jnp.mean(r * r, axis=-1, keepdims=True) + EPS)
                refs[o_base + 1][...] = (r * scale * g_ref[...]).astype(BF16)

        if nk == 1:
            finish(part)
        else:
            acc = scratch[0]
            k = pl.program_id(2)

            @pl.when(k == 0)
            def _():
                acc[...] = part

            @pl.when(k > 0)
            def _():
                acc[...] += part

            @pl.when(k == nk - 1)
            def _():
                finish(acc[...])

        if hosting.plan:
            @pl.when(last)
            def _():
                hosting.finish(host_in, host_out, *sems)

    in_specs = [pl.BlockSpec((tm, tk), lambda j, i, k: (i + a_row_off // tm, k)),
                pl.BlockSpec((None, tk, tn), lambda j, i, k: (l, k, j))]
    args = [a, w]
    if has_resid:
        in_specs.append(pl.BlockSpec((tm, tn), lambda j, i, k: (i, j)))
        args.append(resid)
    if has_norm:
        in_specs.append(pl.BlockSpec((None, 1, tn), lambda j, i, k: (l, 0, 0)))
        args.append(norm_gain)
    aliases = {}
    if has_out:
        in_specs.append(ANY)
        aliases = {len(args): 0}
        args.append(out)
    aliases.update(hosting.aliases(n_in, n_own_out))
    out_specs = [pl.BlockSpec((tm, tn), lambda j, i, k: (i, j + out_col_off // tn))]
    out_shape = [jax.ShapeDtypeStruct((rows, out_cols), out_dtype)]
    if has_norm:
        out_specs.append(pl.BlockSpec((tm, tn), lambda j, i, k: (i, j)))
        out_shape.append(jax.ShapeDtypeStruct((rows, N), BF16))
    res = pl.pallas_call(
        body, name=name, grid=grid,
        in_specs=in_specs + [ANY] * n_host_in,
        out_specs=out_specs + [ANY] * n_host_out,
        out_shape=out_shape + hosting.out_shapes,
        scratch_shapes=([pltpu.VMEM((tm, tn), F32)] if nk > 1 else []) + hosting.scratch(),
        input_output_aliases=aliases, compiler_params=_params(3))(*args, *hosting.arrays)
    own = res[0] if n_own_out == 1 else tuple(res[:n_own_out])
    return (own, list(res[n_own_out:])) if hosting.plan else own


def mm_nt(a, w, l, *, name, rows, kdim=None, a_col_off=0, out_dtype=BF16, norm_bwd_of=None, host=None, tm=1024,
          tn=None, tk=None):
    R = w.shape[1]
    kdim = kdim or w.shape[2]
    tn = tn or R
    tk = tk or kdim
    nk = kdim // tk
    assert rows % tm == 0 and R % tn == 0 and kdim % tk == 0 and a_col_off % tk == 0
    fused = norm_bwd_of is not None
    assert not fused or tn == R
    grid = (R // tn, rows // tm, nk)
    hosting = _Hosting(host)
    n_host_in, n_host_out = len(hosting.arrays), len(hosting.out_shapes)
    n_own_in, n_own_out = (3, 2) if fused else (0, 1)

    def body(a_ref, w_ref, *refs):
        host_in = refs[n_own_in:n_own_in + n_host_in]
        host_out = refs[n_own_in + n_host_in + n_own_out:n_own_in + n_host_in + n_own_out + n_host_out]
        scratch = refs[n_own_in + n_host_in + n_own_out + n_host_out:]
        rest = list(refs[:n_own_in]) + list(refs[n_own_in + n_host_in:n_own_in + n_host_in + n_own_out]) \
            + ([scratch[0]] if nk > 1 else [])
        if hosting.plan:
            first, last = _first_last(grid)
            sems = scratch[-2:]

            @pl.when(first)
            def _():
                hosting.begin(host_in, host_out, *sems)

        part = lax.dot_general(a_ref[...].astype(BF16), w_ref[...], (((1,), (1,)), ((), ())),
                               preferred_element_type=F32)
        i, k = pl.program_id(1), pl.program_id(2)

        def finish(dh):
            if not fused:
                rest[0][...] = dh.astype(rest[0].dtype)
                return
            x_ref, g_ref, dxi_ref, dx_ref, dg_ref = rest[:5]

            @pl.when(i == 0)
            def _():
                dg_ref[...] = jnp.zeros_like(dg_ref)

            xv = x_ref[...]
            r = lax.rsqrt(jnp.mean(xv * xv, axis=-1, keepdims=True) + EPS)
            xh = xv * r
            dhg = dh * g_ref[...]
            dx_ref[...] = dxi_ref[...] + r * (dhg - xh * jnp.mean(dhg * xh, axis=-1, keepdims=True))
            dg_ref[0:1, :] += jnp.sum(dh * xh, axis=0, keepdims=True)

        if nk == 1:
            finish(part)
        else:
            acc = rest[-1]

            @pl.when(k == 0)
            def _():
                acc[...] = part

            @pl.when(k > 0)
            def _():
                acc[...] += part

            @pl.when(k == nk - 1)
            def _():
                finish(acc[...])

        if hosting.plan:
            @pl.when(last)
            def _():
                hosting.finish(host_in, host_out, *sems)

    if a.ndim == 3:
        per = a.shape[2] // tk
        a_spec = pl.BlockSpec((None, tm, tk), lambda j, i, k: (k // per, i, k % per))
    else:
        a_spec = pl.BlockSpec((tm, tk), lambda j, i, k: (i, k + a_col_off // tk))
    in_specs = [a_spec, pl.BlockSpec((None, tn, tk), lambda j, i, k: (l, j, k))]
    args = [a, w]
    row_tile = pl.BlockSpec((tm, tn), lambda j, i, k: (i, j))
    if fused:
        x, gain, gl, dx_in = norm_bwd_of
        in_specs += [row_tile, pl.BlockSpec((None, 1, tn), lambda j, i, k: (gl, 0, 0)), row_tile]
        args += [x, gain, dx_in]
        out_specs = [row_tile, pl.BlockSpec((8, tn), lambda j, i, k: (0, 0))]
        out_shape = [jax.ShapeDtypeStruct((rows, R), F32), jax.ShapeDtypeStruct((8, R), F32)]
    else:
        out_specs, out_shape = [row_tile], [jax.ShapeDtypeStruct((rows, R), out_dtype)]
    res = pl.pallas_call(
        body, name=name, grid=grid, in_specs=in_specs + [ANY] * n_host_in,
        out_specs=out_specs + [ANY] * n_host_out, out_shape=out_shape + hosting.out_shapes,
        scratch_shapes=([pltpu.VMEM((tm, tn), F32)] if nk > 1 else []) + hosting.scratch(),
        input_output_aliases=hosting.aliases(2 + n_own_in, n_own_out), compiler_params=_params(3))(
            *args, *hosting.arrays)
    own = tuple(res[:n_own_out]) if fused else res[0]
    return (own, list(res[n_own_out:])) if hosting.plan else own


def mm_tn(a, b, *, name, rows, ka, nb, a_row_off=0, b_col_off=0, tm=None, tn=None, tk=2048):
    tm = tm or ka
    tn = tn or nb
    tk = min(tk, rows)
    nk = rows // tk
    assert ka % tm == 0 and nb % tn == 0 and rows % tk == 0 and b_col_off % tn == 0 and a_row_off % tk == 0

    def body(a_ref, b_ref, o_ref, acc):
        part = lax.dot_general(a_ref[...].astype(BF16), b_ref[...].astype(BF16), (((0,), (0,)), ((), ())),
                               preferred_element_type=F32)
        k = pl.program_id(2)

        @pl.when(k == 0)
        def _():
            acc[...] = part

        @pl.when(k > 0)
        def _():
            acc[...] += part

        @pl.when(k == nk - 1)
        def _():
            o_ref[...] = acc[...].astype(o_ref.dtype)

    if b.ndim == 3:
        per = b.shape[2] // tn
        b_spec = pl.BlockSpec((None, tk, tn), lambda j, i, k: (j // per, k, j % per))
    else:
        b_spec = pl.BlockSpec((tk, tn), lambda j, i, k: (k, j + b_col_off // tn))
    return pl.pallas_call(
        body, name=name, grid=(nb // tn, ka // tm, nk),
        in_specs=[pl.BlockSpec((tk, tm), lambda j, i, k: (k + a_row_off // tk, i)), b_spec],
        out_specs=pl.BlockSpec((None, tm, tn), lambda j, i, k: (0, i, j)),
        out_shape=jax.ShapeDtypeStruct((1, ka, nb), BF16),
        scratch_shapes=[pltpu.VMEM((tm, tn), F32)], compiler_params=_params(3))(a, b)


def _row_spec(tm, width, col=0):
    return pl.BlockSpec((tm, width), lambda i: (i, col))


def _gain_spec(l, width=D):
    return pl.BlockSpec((None, 1, width), lambda i: (l, 0, 0))


def norm_fwd(x, g3, l, *, name, tm=1024):
    T = x.shape[0]

    def body(x_ref, g_ref, o_ref):
        xv = x_ref[...]
        r = lax.rsqrt(jnp.mean(xv * xv, axis=-1, keepdims=True) + EPS)
        o_ref[...] = (xv * r * g_ref[...]).astype(BF16)

    return pl.pallas_call(
        body, name=name, grid=(T // tm,),
        in_specs=[_row_spec(tm, D), _gain_spec(l)], out_specs=_row_spec(tm, D),
        out_shape=jax.ShapeDtypeStruct((T, D), BF16), compiler_params=_params(1))(x, g3)


def _winsum_back(ext, w):
    s, span = ext, 1
    while span < w:
        s = s + pltpu.roll(s, span, 0)
        span *= 2
    return s


def _winsum_fwd(ext, w):
    rows = ext.shape[0]
    s, span = ext, 1
    while span < w:
        s = s + pltpu.roll(s, rows - span, 0)
        span *= 2
    return s


def _pooled(ext, z, t, g, w):
    sl = slice(g * PG, (g + 1) * PG)
    s = _winsum_back(ext[:, sl], w)[POOL_HALO:, :]
    return s / jnp.minimum(t + 1, w).astype(F32) - z[:, sl]


def pool_fwd(z, wpool, scale3, l, *, name, tm=256):
    T = z.shape[0]
    hb = tm // POOL_HALO
    wl = l if wpool.shape[0] > 1 else 0

    def body(z_ref, zp_ref, wp_ref, sc_ref, o_ref):
        i = pl.program_id(0)
        zv = z_ref[...].astype(F32)
        prev = jnp.where(i == 0, 0.0, zp_ref[...].astype(F32))
        ext = jnp.concatenate([prev, zv], axis=0)
        t = i * tm + lax.broadcasted_iota(jnp.int32, (tm, 1), 0)
        for g, w in enumerate(POOL_WINDOWS):
            sl = slice(g * PG, (g + 1) * PG)
            pooled = _pooled(ext, zv, t, g, w)
            q = jnp.dot(pooled.astype(BF16), wp_ref[g], preferred_element_type=F32)
            o_ref[:, sl] = (q * sc_ref[:, sl]).astype(BF16)

    return pl.pallas_call(
        body, name=name, grid=(T // tm,),
        in_specs=[_row_spec(tm, D),
                  pl.BlockSpec((POOL_HALO, D), lambda i: (jnp.maximum(i * hb - 1, 0), 0)),
                  pl.BlockSpec((None, 4, PG, PG), lambda i: (wl, 0, 0, 0)),
                  _gain_spec(l)],
        out_specs=_row_spec(tm, D),
        out_shape=jax.ShapeDtypeStruct((T, D), BF16), compiler_params=_params(1))(z, z, wpool, scale3)


def sgu_fwd(z, g3, wsm, bT, l, *, name, tm=256):
    T = z.shape[0]

    def body(zu_ref, zv_ref, g_ref, ws_ref, b_ref, o_ref):
        gu = _gelu(zu_ref[...].astype(F32))
        gv = _gelu(zv_ref[...].astype(F32))
        rv = lax.rsqrt(jnp.mean(gv * gv, axis=-1, keepdims=True) + EPS)
        vn = (gv * rv * g_ref[...]).astype(BF16)
        for n in range(tm // CHUNK):
            r = slice(n * CHUNK, (n + 1) * CHUNK)
            for h in range(HEADS):
                cs = slice(h * CHUNK, (h + 1) * CHUNK)
                mixed = jnp.dot(ws_ref[h], vn[r, cs], preferred_element_type=F32) + b_ref[:, h:h + 1]
                o_ref[r, cs] = (gu[r, cs] * mixed).astype(BF16)

    return pl.pallas_call(
        body, name=name, grid=(T // tm,),
        in_specs=[_row_spec(tm, D, 1), _row_spec(tm, D, 2), _gain_spec(l),
                  pl.BlockSpec((None, HEADS, CHUNK, CHUNK), lambda i: (l, 0, 0, 0)),
                  pl.BlockSpec((None, CHUNK, HEADS), lambda i: (l, 0, 0))],
        out_specs=_row_spec(tm, D),
        out_shape=jax.ShapeDtypeStruct((T, D), BF16), compiler_params=_params(1))(z, z, g3, wsm, bT)


def gate_fwd(z, yab, *, name, tm=1024):
    T = z.shape[0]

    def body(za_ref, zb_ref, y_ref, o_ref):
        ga = jax.nn.sigmoid(za_ref[...].astype(F32))
        gb = jax.nn.sigmoid(zb_ref[...].astype(F32))
        o_ref[...] = (ga * y_ref[:, :D].astype(F32) + gb * y_ref[:, D:].astype(F32)).astype(BF16)

    return pl.pallas_call(
        body, name=name, grid=(T // tm,),
        in_specs=[_row_spec(tm, D, 3), _row_spec(tm, D, 4), _row_spec(tm, 2 * D)],
        out_specs=_row_spec(tm, D),
        out_shape=jax.ShapeDtypeStruct((T, D), BF16), compiler_params=_params(1))(z, z, yab)


def _conv(ext, w_ref, b_ref):
    down1, down2 = pltpu.roll(ext, 1, 0), pltpu.roll(ext, 2, 0)
    c = b_ref[...] + w_ref[0:1, :] * down2
    c = c + w_ref[1:2, :] * down1
    return c + w_ref[2:3, :] * ext, down1, down2


def conv_fwd(up, convw, convb3, l, *, name, tm=256):
    T = up.shape[0]
    tc = CONV_TC
    nc = DFF // tc
    hb = tm // CONV_HALO

    def body(ua_ref, uap_ref, ub_ref, ubp_ref, wa_ref, wb_ref, ba_ref, bb_ref, o_ref):
        i = pl.program_id(1)

        def conv_of(u_ref, p_ref, w_ref, b_ref):
            ext = jnp.concatenate([jnp.where(i == 0, 0.0, p_ref[...]), u_ref[...]], axis=0)
            return _conv(ext, w_ref, b_ref)[0][CONV_HALO:, :]

        ca = conv_of(ua_ref, uap_ref, wa_ref, ba_ref)
        cb = conv_of(ub_ref, ubp_ref, wb_ref, bb_ref)
        o_ref[...] = (_gelu(ca) * cb).astype(BF16)

    def cur(off):
        return pl.BlockSpec((tm, tc), lambda j, i: (i, j + off))

    def prev(off):
        return pl.BlockSpec((CONV_HALO, tc), lambda j, i: (jnp.maximum(i * hb - 1, 0), j + off))

    def wspec(off):
        return pl.BlockSpec((None, CONV_ROWS, tc), lambda j, i: (l, 0, j + off))

    def bspec(off):
        return pl.BlockSpec((None, 1, tc), lambda j, i: (l, 0, j + off))

    return pl.pallas_call(
        body, name=name, grid=(nc, T // tm),
        in_specs=[cur(0), prev(0), cur(nc), prev(nc), wspec(0), wspec(nc), bspec(0), bspec(nc)],
        out_specs=pl.BlockSpec((tm, tc), lambda j, i: (i, j)),
        out_shape=jax.ShapeDtypeStruct((T, DFF), BF16),
        compiler_params=_params(2))(up, up, up, up, convw, convw, convb3, convb3)


def ple_fwd(x2, pg, e, g3, l, *, name, tm=1024):
    T = x2.shape[0]
    has_norm = g3 is not None

    def body(x_ref, pg_ref, e_ref, *rest):
        xv = x_ref[...] + jax.nn.sigmoid(pg_ref[...].astype(F32)) * e_ref[...].astype(F32)
        if has_norm:
            g_ref, o_ref, h_ref = rest
            r = lax.rsqrt(jnp.mean(xv * xv, axis=-1, keepdims=True) + EPS)
            h_ref[...] = (xv * r * g_ref[...]).astype(BF16)
        else:
            o_ref, = rest
        o_ref[...] = xv

    x_shape = jax.ShapeDtypeStruct((T, D), F32)
    return pl.pallas_call(
        body, name=name, grid=(T // tm,),
        in_specs=[_row_spec(tm, D)] * 3 + ([_gain_spec(l)] if has_norm else []),
        out_specs=[_row_spec(tm, D)] * 2 if has_norm else _row_spec(tm, D),
        out_shape=[x_shape, jax.ShapeDtypeStruct((T, D), BF16)] if has_norm else x_shape,
        compiler_params=_params(1))(x2, pg, e, *([g3] if has_norm else []))


def loss_head(x, g3, tgt, *, name, tm=1024):
    T = x.shape[0]

    def body(x_ref, g_ref, t_ref, loss_ref, dx_ref, dg_ref):
        @pl.when(pl.program_id(0) == 0)
        def _():
            loss_ref[...] = jnp.zeros_like(loss_ref)
            dg_ref[...] = jnp.zeros_like(dg_ref)

        xv, g = x_ref[...], g_ref[...]
        r = lax.rsqrt(jnp.mean(xv * xv, axis=-1, keepdims=True) + EPS)
        xh = xv * r
        err = xh * g - t_ref[...]
        loss_ref[...] += 0.5 * jnp.sum(jnp.mean(err * err, axis=-1, keepdims=True))
        dy = err * (1.0 / D)
        dyg = dy * g
        dx_ref[...] = r * (dyg - xh * jnp.mean(dyg * xh, axis=-1, keepdims=True))
        dg_ref[0:1, :] += jnp.sum(dy * xh, axis=0, keepdims=True)

    return pl.pallas_call(
        body, name=name, grid=(T // tm,),
        in_specs=[_row_spec(tm, D), pl.BlockSpec((1, D), lambda i: (0, 0)), _row_spec(tm, D)],
        out_specs=[pl.BlockSpec((8, LANES), lambda i: (0, 0)), _row_spec(tm, D),
                   pl.BlockSpec((8, D), lambda i: (0, 0))],
        out_shape=[jax.ShapeDtypeStruct((8, LANES), F32), jax.ShapeDtypeStruct((T, D), F32),
                   jax.ShapeDtypeStruct((8, D), F32)],
        compiler_params=_params(1))(x, g3, tgt)


def ple_bwd(dx, pg, e, *, name, tm=1024):
    T = dx.shape[0]

    def body(dx_ref, pg_ref, e_ref, de_ref, dpg_ref):
        gate = jax.nn.sigmoid(pg_ref[...].astype(F32))
        dxv = dx_ref[...]
        de_ref[...] = (dxv * gate).astype(BF16)
        dpg_ref[...] = (dxv * e_ref[...].astype(F32) * gate * (1.0 - gate)).astype(BF16)

    return pl.pallas_call(
        body, name=name, grid=(T // tm,),
        in_specs=[_row_spec(tm, D)] * 3, out_specs=[_row_spec(tm, D)] * 2,
        out_shape=[jax.ShapeDtypeStruct((T, D), BF16)] * 2, compiler_params=_params(1))(dx, pg, e)


def conv_bwd(df, up, convw, convb3, l, *, name, host=None, tm=256):
    T = up.shape[0]
    tc = CONV_TC
    nc = DFF // tc
    hb = tm // CONV_HALO
    nt = T // tm
    rows = tm + 2 * CONV_HALO
    own = slice(CONV_HALO, CONV_HALO + tm)

    hosting = _Hosting(host)
    n_host_in, n_host_out = len(hosting.arrays), len(hosting.out_shapes)

    def body(df_ref, dfn_ref, ua_ref, uap_ref, uan_ref, ub_ref, ubp_ref, ubn_ref, wa_ref, wb_ref, ba_ref, bb_ref,
             *rest):
        host_in = rest[:n_host_in]
        dup_ref, dcw_ref, dcb_ref = rest[n_host_in:n_host_in + 3]
        host_out = rest[n_host_in + 3:n_host_in + 3 + n_host_out]
        sems = rest[n_host_in + 3 + n_host_out:]
        i = pl.program_id(1)
        if hosting.plan:
            first, last = _first_last((nc, nt))

            @pl.when(first)
            def _():
                hosting.begin(host_in, host_out, *sems)

        @pl.when(i == 0)
        def _():
            dcw_ref[...] = jnp.zeros_like(dcw_ref)
            dcb_ref[...] = jnp.zeros_like(dcb_ref)

        def ext_of(c_ref, p_ref, n_ref):
            return jnp.concatenate([jnp.where(i == 0, 0.0, p_ref[...]), c_ref[...],
                                    jnp.where(i == nt - 1, 0.0, n_ref[...])], axis=0)

        ea = ext_of(ua_ref, uap_ref, uan_ref)
        eb = ext_of(ub_ref, ubp_ref, ubn_ref)
        ca, ea1, ea2 = _conv(ea, wa_ref, ba_ref)
        cb, eb1, eb2 = _conv(eb, wb_ref, bb_ref)
        df_ext =jnp.concatenate([jnp.zeros((CONV_HALO, tc), F32), df_ref[...],
                                  jnp.where(i == nt - 1, 0.0, dfn_ref[...])], axis=0)
        cdf = 0.5 * (1.0 + lax.erf(ca * 0.7071067811865476))
        da = df_ext * cb * (cdf + ca * jnp.exp(-0.5 * ca * ca) * 0.3989422804014327)
        db = df_ext * (ca * cdf)

        def finish(h, dc, e, e1, e2, w_ref):
            dup = w_ref[2:3, :] * dc + w_ref[1:2, :] * pltpu.roll(dc, rows - 1, 0)
            dup = dup + w_ref[0:1, :] * pltpu.roll(dc, rows - 2, 0)
            dup_ref[h] = dup[own, :].astype(BF16)
            dco = dc[own, :]
            dcb_ref[h, 0:1, :] += jnp.sum(dco, axis=0, keepdims=True)
            dcw_ref[h, 0:1, :] += jnp.sum(dco * e2[own, :], axis=0, keepdims=True)
            dcw_ref[h, 1:2, :] += jnp.sum(dco * e1[own, :], axis=0, keepdims=True)
            dcw_ref[h, 2:3, :] += jnp.sum(dco * e[own, :], axis=0, keepdims=True)

        finish(0, da, ea, ea1, ea2, wa_ref)
        finish(1, db, eb, eb1, eb2, wb_ref)
        if hosting.plan:
            @pl.when(last)
            def _():
                hosting.finish(host_in, host_out, *sems)

    def nxt(i):
        return jnp.minimum((i + 1) * hb, T // CONV_HALO - 1)

    def prv(i):
        return jnp.maximum(i * hb - 1, 0)

    def up_specs(off):
        return [pl.BlockSpec((tm, tc), lambda j, i: (i, j + off)),
                pl.BlockSpec((CONV_HALO, tc), lambda j, i: (prv(i), j + off)),
                pl.BlockSpec((CONV_HALO, tc), lambda j, i: (nxt(i), j + off))]

    in_specs = [pl.BlockSpec((tm, tc), lambda j, i: (i, j)),
                pl.BlockSpec((CONV_HALO, tc), lambda j, i: (nxt(i), j)),
                *up_specs(0), *up_specs(nc),
                pl.BlockSpec((None, CONV_ROWS, tc), lambda j, i: (l, 0, j)),
                pl.BlockSpec((None, CONV_ROWS, tc), lambda j, i: (l, 0, j + nc)),
                pl.BlockSpec((None, 1, tc), lambda j, i: (l, 0, j)),
                pl.BlockSpec((None, 1, tc), lambda j, i: (l, 0, j + nc))]
    res = pl.pallas_call(
        body, name=name, grid=(nc, nt), in_specs=in_specs + [ANY] * n_host_in,
        out_specs=[pl.BlockSpec((2, tm, tc), lambda j, i: (0, i, j)),
                   pl.BlockSpec((2, 8, tc), lambda j, i: (0, 0, j)),
                   pl.BlockSpec((2, 8, tc), lambda j, i: (0, 0, j))] + [ANY] * n_host_out,
        out_shape=[jax.ShapeDtypeStruct((2, T, DFF), BF16), jax.ShapeDtypeStruct((2, 8, DFF), F32),
                   jax.ShapeDtypeStruct((2, 8, DFF), F32)] + hosting.out_shapes,
        scratch_shapes=hosting.scratch(), input_output_aliases=hosting.aliases(12, 3),
        compiler_params=_params(2))(df, df, up, up, up, up, up, up, convw, convw, convb3, convb3, *hosting.arrays)
    return res[0], res[1], res[2], list(res[3:])


def gate_bwd(dmo, z, yab, *, name, tm=1024):
    T = z.shape[0]

    def body(dmo_ref, zg_ref, y_ref, dz_ref, dy_ref):
        g = jax.nn.sigmoid(zg_ref[...].astype(F32))
        dmo_v = dmo_ref[...].astype(F32)
        dy_ref[...] = (dmo_v * g).astype(BF16)
        dz_ref[...] = (dmo_v * y_ref[...].astype(F32) * g * (1.0 - g)).astype(BF16)

    return pl.pallas_call(
        body, name=name, grid=(T // tm, 2),
        in_specs=[pl.BlockSpec((tm, D), lambda i, s: (i, 0)),
                  pl.BlockSpec((tm, D), lambda i, s: (i, 3 + s)),
                  pl.BlockSpec((tm, D), lambda i, s: (i, s))],
        out_specs=[pl.BlockSpec((tm, D), lambda i, s: (i, 3 + s)),
                   pl.BlockSpec((tm, D), lambda i, s: (i, s))],
        out_shape=[jax.ShapeDtypeStruct((T, 5 * D), BF16), jax.ShapeDtypeStruct((T, 2 * D), BF16)],
        compiler_params=_params(2))(dmo, z, yab)


def mixer_bwd(da, ds, z, dz, wpool, scale3, g3, wsm, wsmT, bT, l, *, name, tm=256):
    T = z.shape[0]
    hb = tm // POOL_HALO
    nt = T // tm

    def body(da_ref, dan_ref, ds_ref, zp_ref, zpp_ref, zu_ref, zv_ref, wp_ref, sc_ref, g_ref, ws_ref, wst_ref,
             b_ref, dzin_ref, dz_ref, dwp_ref, dsc_ref, dws_ref, dbt_ref, dgs_ref, mixed_scr, dvn_scr, db_scr):
        del dzin_ref
        i = pl.program_id(0)

        @pl.when(i == 0)
        def _():
            dwp_ref[...] = jnp.zeros_like(dwp_ref)
            dsc_ref[...] = jnp.zeros_like(dsc_ref)
            dws_ref[...] = jnp.zeros_like(dws_ref)
            dgs_ref[...] = jnp.zeros_like(dgs_ref)
            db_scr[...] = jnp.zeros_like(db_scr)

        zv_p = zp_ref[...].astype(F32)
        ext = jnp.concatenate([jnp.where(i == 0, 0.0, zpp_ref[...].astype(F32)), zv_p], axis=0)
        da_v = da_ref[...].astype(F32)
        da_ext = jnp.concatenate([da_v, jnp.where(i == nt - 1, 0.0, dan_ref[...].astype(F32))], axis=0)
        t = i * tm + lax.broadcasted_iota(jnp.int32, (tm, 1), 0)
        t_ext = i * tm + lax.broadcasted_iota(jnp.int32, (tm + POOL_HALO, 1), 0)
        for g, w in enumerate(POOL_WINDOWS):
            sl = slice(g * PG, (g + 1) * PG)
            pooled = _pooled(ext, zv_p, t, g, w).astype(BF16)
            q = jnp.dot(pooled, wp_ref[g], preferred_element_type=F32)
            dsc_ref[0:1, sl] += jnp.sum(da_v[:, sl] * q, axis=0, keepdims=True)
            dq_ext = (da_ext[:, sl] * sc_ref[:, sl]).astype(BF16)
            dwp_ref[g] += lax.dot_general(pooled, dq_ext[:tm, :], (((0,), (0,)), ((), ())),
                                          preferred_element_type=F32)
            dpool = lax.dot_general(dq_ext, wp_ref[g], (((1,), (1,)), ((), ())), preferred_element_type=F32)
            spread = _winsum_fwd(dpool / jnp.minimum(t_ext + 1, w).astype(F32), w)
            dz_ref[:, sl] = (spread[:tm, :] - dpool[:tm, :]).astype(BF16)

        zu, zv, ds_v = zu_ref[...].astype(F32), zv_ref[...].astype(F32), ds_ref[...].astype(F32)
        gain = g_ref[...]
        gu, gv = _gelu(zu), _gelu(zv)
        rv = lax.rsqrt(jnp.mean(gv * gv, axis=-1, keepdims=True) + EPS)
        vh = gv * rv
        vn = (vh * gain).astype(BF16)
        dmix = ds_v * gu
        dmix_b = dmix.astype(BF16)
        for n in range(tm // CHUNK):
            r = slice(n * CHUNK, (n + 1) * CHUNK)
            db_scr[...] += dmix[r, :]
            for h in range(HEADS):
                cs = slice(h * CHUNK, (h + 1) * CHUNK)
                mixed_scr[r, cs] = jnp.dot(ws_ref[h], vn[r, cs], preferred_element_type=F32) + b_ref[:, h:h + 1]
                dws_ref[h] += lax.dot_general(dmix_b[r, cs], vn[r, cs], (((1,), (1,)), ((), ())),
                                              preferred_element_type=F32)
                dvn_scr[r, cs] = jnp.dot(wst_ref[h], dmix_b[r, cs], preferred_element_type=F32)
        dz_ref[:, D:2 * D] = (ds_v * mixed_scr[...] * _gelu_grad(zu)).astype(BF16)
        dvn = dvn_scr[...]
        dgs_ref[0:1, :] += jnp.sum(dvn * vh, axis=0, keepdims=True)
        dvg = dvn * gain
        dgv = rv * (dvg - vh * jnp.mean(dvg * vh, axis=-1, keepdims=True))
        dz_ref[:, 2 * D:3 * D] = (dgv * _gelu_grad(zv)).astype(BF16)

        @pl.when(i == nt - 1)
        def _():
            tril = (lax.broadcasted_iota(jnp.int32, (CHUNK, CHUNK), 0)
                    >= lax.broadcasted_iota(jnp.int32, (CHUNK, CHUNK), 1)).astype(F32)
            for h in range(HEADS):
                dws_ref[h] = dws_ref[h] * tril
                dbt_ref[:, h:h + 1] = jnp.sum(db_scr[:, h * CHUNK:(h + 1) * CHUNK], axis=1, keepdims=True)

    const4 = lambda i: (l, 0, 0, 0)
    wl = l if wpool.shape[0] > 1 else 0
    in_specs = [
        _row_spec(tm, D),
        pl.BlockSpec((POOL_HALO, D), lambda i: (jnp.minimum((i + 1) * hb, T // POOL_HALO - 1), 0)),
        _row_spec(tm, D),
        _row_spec(tm, D, 0),
        pl.BlockSpec((POOL_HALO, D), lambda i: (jnp.maximum(i * hb - 1, 0), 0)),
        _row_spec(tm, D, 1), _row_spec(tm, D, 2),
        pl.BlockSpec((None, 4, PG, PG), lambda i: (wl, 0, 0, 0)),
        _gain_spec(l), _gain_spec(l),
        pl.BlockSpec((None, HEADS, CHUNK, CHUNK), const4),
        pl.BlockSpec((None, HEADS, CHUNK, CHUNK), const4),
        pl.BlockSpec((None, CHUNK, HEADS), lambda i: (l, 0, 0)),
        ANY,
    ]
    out_specs = [
        pl.BlockSpec((tm, 3 * D), lambda i: (i, 0)),
        pl.BlockSpec((4, PG, PG), lambda i: (0, 0, 0)),
        pl.BlockSpec((8, D), lambda i: (0, 0)),
        pl.BlockSpec((HEADS, CHUNK, CHUNK), lambda i: (0, 0, 0)),
        pl.BlockSpec((CHUNK, HEADS), lambda i: (0, 0)),
        pl.BlockSpec((8, D), lambda i: (0, 0)),
    ]
    out_shape = [
        jax.ShapeDtypeStruct((T, 5 * D), BF16), jax.ShapeDtypeStruct((4, PG, PG), F32),
        jax.ShapeDtypeStruct((8, D), F32), jax.ShapeDtypeStruct((HEADS, CHUNK, CHUNK), F32),
        jax.ShapeDtypeStruct((CHUNK, HEADS), F32), jax.ShapeDtypeStruct((8, D), F32),
    ]
    return pl.pallas_call(
        body, name=name, grid=(nt,), in_specs=in_specs, out_specs=out_specs, out_shape=out_shape,
        scratch_shapes=[pltpu.VMEM((tm, D), F32), pltpu.VMEM((tm, D), F32), pltpu.VMEM((CHUNK, D), F32)],
        input_output_aliases={13: 0}, compiler_params=_params(1))(
            da, da, ds, z, z, z, z, wpool, scale3, g3, wsm, wsmT, bT, dz)


def _row_tile(rows, cols, sub):
    cap = max(sub, (2 * 1024 * 1024) // (4 * cols))
    best = None
    for tr in range(sub, min(rows, cap) + 1, sub):
        if rows % tr == 0:
            best = tr
    return best or rows


def elementwise(fn, ins, out_dtypes, *, name, row_blk_offs=None, rows=None):
    cols = ins[0].shape[1]
    rows = rows or ins[0].shape[0]
    tr = _row_tile(rows, cols, 16)
    offs = row_blk_offs or [0] * len(ins)
    n_in = len(ins)

    def body(*refs):
        outs = fn(*[r[...] for r in refs[:n_in]])
        for o_ref, o in zip(refs[n_in:], outs):
            o_ref[...] = o.astype(o_ref.dtype)

    return pl.pallas_call(
        body, name=name, grid=(rows // tr,),
        in_specs=[pl.BlockSpec((tr, cols), functools.partial(lambda i, o: (i + o * (rows // tr), 0), o=o))
                  for o in offs],
        out_specs=[pl.BlockSpec((tr, cols), lambda i: (i, 0)) for _ in out_dtypes],
        out_shape=[jax.ShapeDtypeStruct((rows, cols), dt) for dt in out_dtypes],
        compiler_params=_params(1))(*ins)


def _adamw(w, g, m, v):
    m = ADAM_B1 * m + (1.0 - ADAM_B1) * g
    v = ADAM_B2 * v + (1.0 - ADAM_B2) * jnp.square(g)
    m_hat = m / (1.0 - ADAM_B1 ** ADAM_STEP)
    v_hat = v / (1.0 - ADAM_B2 ** ADAM_STEP)
    delta = -ADAM_LR * (m_hat / (jnp.sqrt(v_hat) + ADAM_EPS) + ADAM_WD * w)
    return delta, m, v


def _view2d(a):
    return a.reshape(-1, a.shape[-1])


def _place():
    x, y, c = lax.axis_index("x"), lax.axis_index("y"), lax.axis_index("c")
    others = [(1 - x, y), (x, 1 - y), (1 - x, 1 - y)]
    return x, y, c, 2 * x + y, others


def _remote(src, dst, send_sems, recv_sems, k, to):
    return pltpu.make_async_remote_copy(src_ref=src, dst_ref=dst, send_sem=send_sems.at[k], recv_sem=recv_sems.at[k],
                                        device_id=to, device_id_type=MESH)


def _half(ref, axis, j, size, h):
    if len(ref.shape) == 3:
        return ref.at[:, pl.ds(j * size + h * (size // 2), size // 2), :]
    if axis == 0:
        return ref.at[pl.ds(j * size + h * (size // 2), size // 2), :]
    rows = ref.shape[0] // 2
    return ref.at[pl.ds(h * rows, rows), pl.ds(j * size, size)]


def _half_shard_shape(shape, axis, size):
    if len(shape) == 3:
        return (shape[0], size // 2, shape[2])
    if axis == 0:
        return (size // 2, shape[1])
    return (shape[0] // 2, size)


class Exchange:
    def __init__(self, arrays, out_shapes, aliases, n_sems, begin, finish):
        self.arrays, self.out_shapes, self.aliases, self.n_sems = list(arrays), list(out_shapes), aliases, n_sems
        self.begin, self.finish = begin, finish


class _Hosting:
    def __init__(self, plan):
        self.plan = list(plan or [])
        self.arrays = [a for ex in self.plan for a in ex.arrays]
        self.out_shapes = [o for ex in self.plan for o in ex.out_shapes]
        self.n_sems = sum(ex.n_sems for ex in self.plan)

    def scratch(self):
        return [pltpu.SemaphoreType.DMA((self.n_sems,)), pltpu.SemaphoreType.DMA((self.n_sems,))] if self.plan else []

    def aliases(self, in_base, out_base):
        out, i0, o0 = {}, in_base, out_base
        for ex in self.plan:
            out.update({i0 + i: o0 + o for i, o in ex.aliases.items()})
            i0, o0 = i0 + len(ex.arrays), o0 + len(ex.out_shapes)
        return out

    def _each(self, in_refs, out_refs):
        i0 = o0 = s0 = 0
        for ex in self.plan:
            yield ex, in_refs[i0:i0 + len(ex.arrays)], out_refs[o0:o0 + len(ex.out_shapes)], s0
            i0, o0, s0 = i0 + len(ex.arrays), o0 + len(ex.out_shapes), s0 + ex.n_sems

    def begin(self, in_refs, out_refs, send_sems, recv_sems):
        for ex, ins, outs, s0 in self._each(in_refs, out_refs):
            ex.begin(ins, outs, send_sems, recv_sems, s0)

    def finish(self, in_refs, out_refs, send_sems, recv_sems):
        for ex, ins, outs, s0 in self._each(in_refs, out_refs):
            ex.finish(ins, outs, send_sems, recv_sems, s0)


def _first_last(grid):
    ids = [pl.program_id(a) for a in range(len(grid))]
    first = functools.reduce(jnp.logical_and, [i == 0 for i in ids])
    last = functools.reduce(jnp.logical_and, [i == g - 1 for i, g in zip(ids, grid)])
    return first, last


def run_exchanges(plan, *, name):
    host = _Hosting(plan)
    n_in, n_out = len(host.arrays), len(host.out_shapes)

    def body(*refs):
        ins, outs = refs[:n_in], refs[n_in:n_in + n_out]
        send_sems, recv_sems = refs[n_in + n_out:]
        host.begin(ins, outs, send_sems, recv_sems)
        host.finish(ins, outs, send_sems, recv_sems)

    return pl.pallas_call(
        body, name=name, in_specs=[ANY] * n_in, out_specs=[ANY] * n_out, out_shape=host.out_shapes,
        scratch_shapes=host.scratch(), input_output_aliases=host.aliases(0, 0),
        compiler_params=pltpu.CompilerParams(has_side_effects=True))(*host.arrays)


def place_shard(src, l, axis, size, out_dtype, place, *, name):
    shard = src.shape[1:]
    natural = tuple(size * N_CHIPS if a == axis else s for a, s in enumerate(shard))
    if len(shard) == 3:
        blk = (None,) + shard
        grid = (1,)
        in_map = lambda i, pr: (l, 0, 0, 0)
        out_map = lambda i, pr: (0, 0, pr[0], 0)
    else:
        tr = _row_tile(shard[0], shard[1], 16)
        steps = shard[0] // tr
        blk = (None, tr, shard[1])
        grid = (steps,)
        in_map = lambda i, pr: (l, i, 0)
        if axis == 0:
            out_map = lambda i, pr: (0, pr[0] * steps + i, 0)
        else:
            out_map = lambda i, pr: (0, i, pr[0])

    def body(pr_ref, s_ref, o_ref):
        del pr_ref
        o_ref[...] = s_ref[...].astype(o_ref.dtype)

    return pl.pallas_call(
        body, name=name,
        grid_spec=pltpu.PrefetchScalarGridSpec(
            num_scalar_prefetch=1, grid=grid, in_specs=[pl.BlockSpec(blk, in_map)],
            out_specs=pl.BlockSpec(blk, out_map)),
        out_shape=jax.ShapeDtypeStruct((1,) + natural, out_dtype), compiler_params=_params(1))(place, src)


def place_both_layers(src, axis, size, place, *, name):
    rows, cols = src.shape[1], src.shape[2]

    def body(pr_ref, s_ref, o_ref):
        del pr_ref
        o_ref[...] = s_ref[...]

    return pl.pallas_call(
        body, name=name,
        grid_spec=pltpu.PrefetchScalarGridSpec(
            num_scalar_prefetch=1, grid=(2,), in_specs=[pl.BlockSpec((None, rows, cols), lambda lyr, pr: (lyr, 0, 0))],
            out_specs=pl.BlockSpec((None, rows, cols), lambda lyr, pr: (lyr, 0, pr[0]))),
        out_shape=jax.ShapeDtypeStruct((2, rows, cols * N_CHIPS), src.dtype), compiler_params=_params(1))(place, src)


def gather_exchange(arrays, geom):
    n = len(arrays)

    def begin(ins, outs, send_sems, recv_sems, s0):
        x, y, c, j, others = _place()
        for t, (axis, size) in enumerate(geom):
            mine = _half(outs[t].at[0], axis, j, size, c)
            for k, (ox, oy) in enumerate(others):
                _remote(mine, mine, send_sems, recv_sems, s0 + 6 * t + k, (ox, oy, c)).start()

    def finish(ins, outs, send_sems, recv_sems, s0):
        x, y, c, j, others = _place()
        sib = (x, y, 1 - c)
        passed = []
        for t, (axis, size) in enumerate(geom):
            for k, (ox, oy) in enumerate(others):
                landed = _half(outs[t].at[0], axis, 2 * ox + oy, size, c)
                _remote(landed, landed, send_sems, recv_sems, s0 + 6 * t + k, (ox, oy, c)).wait_recv()
                fwd = _remote(landed, landed, send_sems, recv_sems, s0 + 6 * t + 3 + k, sib)
                fwd.start()
                passed.append(fwd)
        for t, (axis, size) in enumerate(geom):
            for k, (ox, oy) in enumerate(others):
                got = _half(outs[t].at[0], axis, 2 * ox + oy, size, 1 - c)
                _remote(got, got, send_sems, recv_sems, s0 + 6 * t + 3 + k, sib).wait_recv()
        for fwd in passed:
            fwd.wait_send()
        for t, (axis, size) in enumerate(geom):
            mine = _half(outs[t].at[0], axis, j, size, c)
            for k, (ox, oy) in enumerate(others):
                _remote(mine, mine, send_sems, recv_sems, s0 + 6 * t + k, (ox, oy, c)).wait_send()

    return Exchange(arrays, [jax.ShapeDtypeStruct(a.shape, a.dtype) for a in arrays], {t: t for t in range(n)},
                    6 * n, begin, finish)


def gather_by_layer_exchange(array, axis, size):
    def blocks(out, others, lyr):
        return [_block(out.at[lyr], axis, 2 * ox + oy, size) for (ox, oy) in others]

    def begin(ins, outs, send_sems, recv_sems, s0):
        x, y, c, j, others = _place()
        mine = _block(outs[0].at[c], axis, j, size)
        for k, (ox, oy) in enumerate(others):
            _remote(mine, mine, send_sems, recv_sems, s0 + k, (ox, oy, c)).start()

    def finish(ins, outs, send_sems, recv_sems, s0):
        x, y, c, j, others = _place()
        sib = (x, y, 1 - c)
        passed = []
        for k, ((ox, oy), landed) in enumerate(zip(others, blocks(outs[0], others, c))):
            _remote(landed, landed, send_sems, recv_sems, s0 + k, (ox, oy, c)).wait_recv()
            fwd = _remote(landed, landed, send_sems, recv_sems, s0 + 3 + k, sib)
            fwd.start()
            passed.append(fwd)
        for k, got in enumerate(blocks(outs[0], others, 1 - c)):
            _remote(got, got, send_sems, recv_sems, s0 + 3 + k, sib).wait_recv()
        for fwd in passed:
            fwd.wait_send()
        mine = _block(outs[0].at[c], axis, j, size)
        for k, (ox, oy) in enumerate(others):
            _remote(mine, mine, send_sems, recv_sems, s0 + k, (ox, oy, c)).wait_send()

    return Exchange([array], [jax.ShapeDtypeStruct(array.shape, array.dtype)], {0: 0}, 6, begin, finish)


def swap_exchange(grads, geom):
    def pieces(t, g, dst, h):
        axis, size = geom[t]
        if len(g.shape) == 2 and axis == 1:
            rows = g.shape[0] // 2
            return [(g.at[pl.ds(h * rows, rows), :], dst)]
        return [(_half(g, axis, jb, size, h), dst.at[jb]) for jb in range(N_CHIPS)]

    counts = [1 if (len(g.shape) == 3 and a == 1) else N_CHIPS for g, (a, _) in zip(grads, geom)]
    bases = [sum(counts[:t]) for t in range(len(grads))]

    def copies(ins, outs, send_sems, recv_sems, s0):
        x, y, c, _, _ = _place()
        cps = []
        for t in range(len(grads)):
            for q, (src, dst) in enumerate(pieces(t, ins[t].at[0], outs[t], 1 - c)):
                cps.append(_remote(src, dst, send_sems, recv_sems, s0 + bases[t] + q, (x, y, 1 - c)))
        return cps

    def begin(*a):
        for cp in copies(*a):
            cp.start()

    def finish(*a):
        for cp in copies(*a):
            cp.wait()

    out_shapes = []
    for g, (axis, size) in zip(grads, geom):
        shp = g.shape[1:]
        if len(shp) == 2 and axis == 1:
            out_shapes.append(jax.ShapeDtypeStruct((shp[0] // 2, shp[1]), g.dtype))
        else:
            out_shapes.append(jax.ShapeDtypeStruct((N_CHIPS,) + _half_shard_shape(shp, axis, size), g.dtype))
    return Exchange(grads, out_shapes, {}, sum(counts), begin, finish)


def scatter_exchange(parts, geom, shapes):
    def copies(ins, outs, send_sems, recv_sems, s0):
        x, y, c, j, others = _place()
        cps = []
        for t, ((axis, size), shp) in enumerate(zip(geom, shapes)):
            for k, (ox, oy) in enumerate(others):
                jp = 2 * ox + oy
                src = ins[t].at[:, pl.ds(jp * size, size)] if (len(shp) == 2 and axis == 1) else ins[t].at[jp]
                cps.append(_remote(src, outs[t].at[k], send_sems, recv_sems, s0 + 3 * t + k, (ox, oy, c)))
        return cps

    def begin(*a):
        for cp in copies(*a):
            cp.start()

    def finish(*a):
        for cp in copies(*a):
            cp.wait_recv()
        for cp in copies(*a):
            cp.wait_send()

    out_shapes = [jax.ShapeDtypeStruct((3,) + _half_shard_shape(shp, axis, size), p.dtype)
                  for p, (axis, size), shp in zip(parts, geom, shapes)]
    return Exchange(parts, out_shapes, {}, 3 * len(parts), begin, finish)


def share_exchange(grads, which):
    n = len(which)

    def my_half(refs, t, h):
        lyr = refs[which[t][0]].at[which[t][1]]
        if len(lyr.shape) == 3:
            rows = lyr.shape[1] // 2
            return lyr.at[:, pl.ds(h * rows, rows), :]
        rows = lyr.shape[0] // 2
        return lyr.at[pl.ds(h * rows, rows), :]

    def begin(ins, outs, send_sems, recv_sems, s0):
        x, y, c, _, _ = _place()
        for t in range(n):
            mine = my_half(outs, t, c)
            _remote(mine, mine, send_sems, recv_sems, s0 + t, (x, y, 1 - c)).start()

    def finish(ins, outs, send_sems, recv_sems, s0):
        x, y, c, _, _ = _place()
        for t in range(n):
            got = my_half(outs, t, 1 - c)
            _remote(got, got, send_sems, recv_sems, s0 + t, (x, y, 1 - c)).wait_recv()
        for t in range(n):
            mine = my_half(outs, t, c)
            _remote(mine, mine, send_sems, recv_sems, s0 + t, (x, y, 1 - c)).wait_send()

    return Exchange(grads, [jax.ShapeDtypeStruct(g.shape, g.dtype) for g in grads],
                    {t: t for t in range(len(grads))}, n, begin, finish)


def all_reduce_small(s):
    rows = s.shape[0]
    half = rows // 2
    assert half % 8 == 0

    def body(s_ref, o_ref, a_ref, b_ref, p_ref, send_sems, recv_sems):
        x, y, c, j, others = _place()
        sib = (x, y, 1 - c)
        swap = _remote(s_ref, a_ref, send_sems, recv_sems, 0, sib)
        swap.start()
        swap.wait()
        p_ref[...] = s_ref[...] + a_ref[...]
        mine = pl.ds(pl.multiple_of(c * half, 8), half)
        b_ref[j] = p_ref[mine, :]
        cps = [_remote(p_ref.at[mine, :], b_ref.at[j], send_sems, recv_sems, 1 + k, (ox, oy, c))
               for k, (ox, oy) in enumerate(others)]
        for cp in cps:
            cp.start()
        for k, (ox, oy) in enumerate(others):
            slot = b_ref.at[2 * ox + oy]
            _remote(slot, slot, send_sems, recv_sems, 1 + k, (ox, oy, c)).wait_recv()
        for cp in cps:
            cp.wait_send()
        o_ref[mine, :] = ((b_ref[0] + b_ref[1]) + b_ref[2]) + b_ref[3]
        back = _remote(o_ref.at[mine, :], o_ref.at[mine, :], send_sems, recv_sems, 4, sib)
        back.start()
        back.wait_send()
        theirs = pl.ds(pl.multiple_of((1 - c) * half, 8), half)
        _remote(o_ref.at[theirs, :], o_ref.at[theirs, :], send_sems, recv_sems, 4, sib).wait_recv()

    vmem = pl.BlockSpec(memory_space=pltpu.VMEM)
    return pl.pallas_call(
        body, name="all_reduce_small", in_specs=[vmem], out_specs=vmem,
        out_shape=jax.ShapeDtypeStruct((rows, LANES), F32),
        scratch_shapes=[pltpu.VMEM((rows, LANES), F32), pltpu.VMEM((N_CHIPS, half, LANES), F32),
                        pltpu.VMEM((rows, LANES), F32), pltpu.SemaphoreType.DMA((5,)),
                        pltpu.SemaphoreType.DMA((5,))],
        compiler_params=pltpu.CompilerParams(vmem_limit_bytes=VMEM_LIMIT, has_side_effects=True))(s)


def pair_sum(g, got, axis, size, place, *, name):
    shp = g.shape[1:]
    if len(shp) == 3:
        hs = size // 2
        grid = (N_CHIPS,)
        g_spec = pl.BlockSpec((None, shp[0], hs, shp[2]), lambda jb, pr: (0, 0, 2 * jb + pr[1], 0))
        r_spec = pl.BlockSpec((None, shp[0], hs, shp[2]), lambda jb, pr: (jb, 0, 0, 0))
    elif axis == 0:
        hs = size // 2
        tr = _row_tile(hs, shp[1], 16)
        steps = hs // tr
        grid = (N_CHIPS, steps)
        g_spec = pl.BlockSpec((None, tr, shp[1]), lambda jb, i, pr: (0, (2 * jb + pr[1]) * steps + i, 0))
        r_spec = pl.BlockSpec((None, tr, shp[1]), lambda jb, i, pr: (jb, i, 0))
    else:
        rows = shp[0] // 2
        tr = _row_tile(rows, shp[1], 16)
        steps = rows // tr
        grid = (steps,)
        g_spec = pl.BlockSpec((None, tr, shp[1]), lambda i, pr: (0, pr[1] * steps + i, 0))
        r_spec = pl.BlockSpec((tr, shp[1]), lambda i, pr: (i, 0))

    def body(pr_ref, g_ref, r_ref, o_ref):
        del pr_ref
        o_ref[...] = (g_ref[...].astype(F32) + r_ref[...].astype(F32)).astype(BF16)

    return pl.pallas_call(
        body, name=name,
        grid_spec=pltpu.PrefetchScalarGridSpec(num_scalar_prefetch=1, grid=grid, in_specs=[g_spec, r_spec],
                                               out_specs=r_spec),
        out_shape=jax.ShapeDtypeStruct(got.shape, BF16), compiler_params=_params(len(grid)))(place, g, got)


def chip_sum(part, slots, shp, axis, size, l, place, out, *, name):
    shard = _shard_shape(shp, axis, size)
    hshape = slots.shape[1:]
    if len(shp) == 3:
        grid = (1,)
        p_spec = pl.BlockSpec((None,) + hshape, lambda i, pr: (pr[0], 0, 0, 0))
        s_specs = [pl.BlockSpec((None,) + hshape, functools.partial(lambda i, pr, k: (k, 0, 0, 0), k=k))
                   for k in range(3)]
        o_spec = pl.BlockSpec((None,) + hshape, lambda i, pr: (l, 0, pr[1], 0))
    else:
        tr = _row_tile(hshape[0], hshape[1], 16)
        steps = hshape[0] // tr
        grid = (steps,)
        if axis == 0:
            p_spec = pl.BlockSpec((None, tr, hshape[1]), lambda i, pr: (pr[0], i, 0))
        else:
            p_spec = pl.BlockSpec((tr, hshape[1]), lambda i, pr: (i, pr[0]))
        s_specs = [pl.BlockSpec((None, tr, hshape[1]), functools.partial(lambda i, pr, k: (k, i, 0), k=k))
                   for k in range(3)]
        o_spec = pl.BlockSpec((None, tr, hshape[1]), lambda i, pr: (l, pr[1] * steps + i, 0))
    has_out = out is not None

    def body(pr_ref, p_ref, s0_ref, s1_ref, s2_ref, *rest):
        del pr_ref
        rest[-1][...] = ((p_ref[...].astype(F32) + s0_ref[...].astype(F32)) + s1_ref[...].astype(F32)) \
            + s2_ref[...].astype(F32)

    return pl.pallas_call(
        body, name=name,
        grid_spec=pltpu.PrefetchScalarGridSpec(
            num_scalar_prefetch=1, grid=grid, in_specs=[p_spec] + s_specs + ([ANY] if has_out else []),
            out_specs=o_spec),
        out_shape=jax.ShapeDtypeStruct((2,) + shard, F32), input_output_aliases={5: 0} if has_out else {},
        compiler_params=_params(1))(place, part, slots, slots, slots, *([out] if has_out else []))


GEOM = {name: (axis, size) for (name, _, axis, size) in BIG}
SHAPE = {name: shape for (name, shape, _, _) in BIG}
RIDES_IN_PROJ_L0 = ((0, ("w_pool", "w_branch_a", "w_branch_b", "w_out", "w_up")),)
RIDES_UP_PROJ_L0 = ((0, ("w_down", "w_ple_gate", "w_ple")), (1, ("w_in",)))
RIDES_DOWN_PROJ_L0 = ((1, ("w_pool", "w_branch_a", "w_branch_b", "w_out")),)
RIDES_IN_PROJ_L1 = ((1, ("w_up", "w_down", "w_ple_gate", "w_ple")),)
EARLY_GRADS_L0 = ("w_ple", "w_ple_gate", "w_down", "w_up")
LATE_GRADS_L0 = ("w_out", "w_branch_a", "w_branch_b", "w_pool", "w_in")


def _swap_of(G, names):
    return swap_exchange([G[k] for k in names], [GEOM[k] for k in names])


def _after_swap(G, names, got, place, tag):
    parts = [pair_sum(G[k], r, *GEOM[k], place, name=f"pair_sum_{k}_{tag}") for k, r in zip(names, got)]
    return scatter_exchange(parts, [GEOM[k] for k in names], [SHAPE[k] for k in names]), parts


def _reduce_start(G, names, place, tag):
    got = run_exchanges([_swap_of(G, names)], name=f"swap_halves_{tag}")
    return _after_swap(G, names, got, place, tag)


def _reduce_end(names, parts, slots, place, l, reduced):
    for k, q, s in zip(names, parts, slots):
        reduced[k] = chip_sum(q, s, SHAPE[k], *GEOM[k], l, place, reduced.get(k), name=f"chip_sum_{k}_l{l}")


def _local_step(x, p2, tgt, W0, W1, conv_w, small, place):
    T = x.shape[0]
    as3 = lambda a: a.reshape(2, 1, a.shape[-1])
    mix3, scale3, sgu3 = as3(small["mix_norm"]), as3(small["pool_scale"]), as3(small["sgu_norm"])
    ffn3, ple3, convb3 = as3(small["ffn_norm"]), as3(small["ple_norm"]), as3(small["conv_b"])
    tril = jnp.tril(jnp.ones((CHUNK, CHUNK), F32))
    ws_masked = small["w_spatial"] * tril
    wsm = ws_masked.astype(BF16)
    wsmT = jnp.swapaxes(ws_masked, -1, -2).astype(BF16)
    bT = jnp.swapaxes(small["b_spatial"], -1, -2)
    final3 = small["final_norm"].reshape(1, D)
    W = [dict(W0), dict(W1)]

    def riders(groups):
        return [gather_exchange([W[lyr][k] for k in names], [GEOM[k] for k in names]) for lyr, names in groups]

    def landed(groups, got):
        for lyr, names in groups:
            W[lyr].update(zip(names, got[:len(names)]))
            got = got[len(names):]

    saved = []
    hb = norm_fwd(x, mix3, 0, name="mix_norm_fwd_l0")
    for l in range(2):
        n = lambda s: f"{s}_l{l}"
        Wl = W[l]
        groups = RIDES_IN_PROJ_L0 if l == 0 else RIDES_IN_PROJ_L1
        z, got = mm_nn(hb, Wl["w_in"], 0, name=n("in_proj"), rows=T, tn=1280, out_dtype=BF16, host=riders(groups))
        landed(groups, got)
        a_in = pool_fwd(z, Wl["w_pool"], scale3, l, name=n("pool_fwd"))
        s_in = sgu_fwd(z, sgu3, wsm, bT, l, name=n("sgu_fwd"))
        yab = mm_nn(a_in, Wl["w_branch_a"], 0, name=n("branch_a"), rows=T, out_cols=2 * D, out_dtype=BF16)
        yab = mm_nn(s_in, Wl["w_branch_b"], 0, name=n("branch_b"), rows=T, out=yab, out_cols=2 * D, out_col_off=D,
                    out_dtype=BF16)
        mo = gate_fwd(z, yab, name=n("gate_fwd"))
        x1, h2b = mm_nn(mo, Wl["w_out"], 0, name=n("out_proj"), rows=T, resid=x, norm_gain=ffn3[l:l + 1])
        if l == 0:
            up, got = mm_nn(h2b, Wl["w_up"], 0, name=n("up_proj"), rows=T, tn=DFF, host=riders(RIDES_UP_PROJ_L0))
            landed(RIDES_UP_PROJ_L0, got)
        else:
            up = mm_nn(h2b, Wl["w_up"], 0, name=n("up_proj"), rows=T, tn=DFF)
        f = conv_fwd(up, conv_w, convb3, l, name=n("conv_fwd"))
        if l == 0:
            (x2, h3b), got = mm_nn(f, Wl["w_down"], 0, name=n("down_proj"), rows=T, resid=x1,
                                   norm_gain=ple3[l:l + 1], host=riders(RIDES_DOWN_PROJ_L0))
            landed(RIDES_DOWN_PROJ_L0, got)
        else:
            x2, h3b = mm_nn(f, Wl["w_down"], 0, name=n("down_proj"), rows=T, resid=x1, norm_gain=ple3[l:l + 1])
        pg = mm_nn(h3b, Wl["w_ple_gate"], 0, name=n("ple_gate_proj"), rows=T, out_dtype=BF16)
        e = mm_nn(p2, Wl["w_ple"], 0, name=n("ple_proj"), rows=T, a_row_off=l * T, out_dtype=BF16)
        saved.append(dict(x=x, hb=hb, z=z, a_in=a_in, s_in=s_in, yab=yab, mo=mo, x1=x1, h2b=h2b, up=up, f=f,
                          x2=x2, h3b=h3b, pg=pg, e=e))
        if l == 0:
            x, hb = ple_fwd(x2, pg, e, mix3, 1, name=n("ple_fwd"))
        else:
            x = ple_fwd(x2, pg, e, None, 0, name=n("ple_fwd"))

    loss_acc, dx, dg_final = loss_head(x, final3, tgt, name="loss_head")

    small_grads = [None, None]
    all_names = [t[0] for t in BIG]
    reduced = {}
    swap1 = G1 = scatter1 = parts1 = slots1 = None
    for l in (1, 0):
        n = lambda s: f"{s}_l{l}"
        a, Wl, G = saved[l], W[l], {}
        de, dpg = ple_bwd(dx, a["pg"], a["e"], name=n("ple_bwd"))
        G["w_ple"] = mm_tn(p2, de, name=n("d_w_ple"), rows=T, ka=PG, nb=D, a_row_off=l * T)
        G["w_ple_gate"] = mm_tn(a["h3b"], dpg, name=n("d_w_ple_gate"), rows=T, ka=D, nb=D)
        if l == 0:
            (dx2, dg_ple), got = mm_nt(dpg, Wl["w_ple_gate"], 0, name=n("ple_norm_bwd"), rows=T,
                                       norm_bwd_of=(a["x2"], ple3, l, dx), host=[swap1])
            scatter1, parts1 = _after_swap(G1, all_names, got, place, "l1")
        else:
            dx2, dg_ple = mm_nt(dpg, Wl["w_ple_gate"], 0, name=n("ple_norm_bwd"), rows=T,
                                norm_bwd_of=(a["x2"], ple3, l, dx))
        df = mm_nt(dx2, Wl["w_down"], 0, name=n("d_ffn_act"), rows=T, out_dtype=F32)
        G["w_down"] = mm_tn(a["f"], dx2, name=n("d_w_down"), rows=T, ka=DFF, nb=D, tm=1408)
        if l == 0:
            dup, dcw, dcb, slots1 = conv_bwd(df, a["up"], conv_w, convb3, l, name=n("conv_bwd"), host=[scatter1])
        else:
            dup, dcw, dcb, _ = conv_bwd(df, a["up"], conv_w, convb3, l, name=n("conv_bwd"))
        G["w_up"] = mm_tn(a["h2b"], dup, name=n("d_w_up"), rows=T, ka=D, nb=2 * DFF, tn=DFF, tk=1024)
        if l == 0:
            scatter_early, parts_early = _reduce_start(G, EARLY_GRADS_L0, place, "l0_early")
            (dx1, dg_ffn), slots_early = mm_nt(dup, Wl["w_up"], 0, name=n("ffn_norm_bwd"), rows=T, tk=1408,
                                               norm_bwd_of=(a["x1"], ffn3, l, dx2), host=[scatter_early])
        else:
            dx1, dg_ffn = mm_nt(dup, Wl["w_up"], 0, name=n("ffn_norm_bwd"), rows=T, tk=1408,
                                norm_bwd_of=(a["x1"], ffn3, l, dx2))
        dmo = mm_nt(dx1, Wl["w_out"], 0, name=n("d_gated"), rows=T)
        G["w_out"] = mm_tn(a["mo"], dx1, name=n("d_w_out"), rows=T, ka=D, nb=D)
        dz, dyab = gate_bwd(dmo, a["z"], a["yab"], name=n("gate_bwd"))
        G["w_branch_a"] = mm_tn(a["a_in"], dyab, name=n("d_w_branch_a"), rows=T, ka=D, nb=D)
        G["w_branch_b"] = mm_tn(a["s_in"], dyab, name=n("d_w_branch_b"), rows=T, ka=D, nb=D, b_col_off=D)
        da = mm_nt(dyab, Wl["w_branch_a"], 0, name=n("d_pool_out"), rows=T, kdim=D)
        ds = mm_nt(dyab, Wl["w_branch_b"], 0, name=n("d_sgu_out"), rows=T, kdim=D, a_col_off=D)
        dz, dwp, dsc, dws, dbt, dgs = mixer_bwd(da, ds, a["z"], dz, Wl["w_pool"], scale3, sgu3, wsm, wsmT, bT, l,
                                                name=n("mixer_bwd"))
        G["w_pool"] = dwp.astype(BF16)[None]
        G["w_in"] = mm_tn(a["hb"], dz, name=n("d_w_in"), rows=T, ka=D, nb=5 * D, tn=1280)
        if l == 0:
            scatter_late, parts_late = _reduce_start(G, LATE_GRADS_L0, place, "l0_late")
            _reduce_end(all_names, parts1, slots1, place, 1, reduced)
            _reduce_end(EARLY_GRADS_L0, parts_early, slots_early, place, 0, reduced)
            done = [(t, 1) for t in range(len(all_names))] + [(all_names.index(k), 0) for k in EARLY_GRADS_L0]
            (dx, dg_mix), got = mm_nt(
                dz, Wl["w_in"], 0, name=n("mix_norm_bwd"), rows=T, tk=1280, norm_bwd_of=(a["x"], mix3, l, dx1),
                host=[scatter_late, share_exchange([reduced[k] for k in all_names], done)])
            slots_late = got[:len(LATE_GRADS_L0)]
            reduced.update(zip(all_names, got[len(LATE_GRADS_L0):]))
        else:
            dx, dg_mix = mm_nt(dz, Wl["w_in"], 0, name=n("mix_norm_bwd"), rows=T, tk=1280,
                               norm_bwd_of=(a["x"], mix3, l, dx1))
            swap1, G1 = _swap_of(G, all_names), G
        small_grads[l] = dict(
            mix_norm=dg_mix[0], pool_scale=dsc[0], sgu_norm=dgs[0], w_spatial=dws, b_spatial=dbt.T,
            ffn_norm=dg_ffn[0], conv_b=jnp.concatenate([dcb[0, 0], dcb[1, 0]]), ple_norm=dg_ple[0],
            conv_w=jnp.concatenate([dcw[0, :3], dcw[1, :3]], axis=1))
    _reduce_end(LATE_GRADS_L0, parts_late, slots_late, place, 0, reduced)
    return loss_acc, dx, reduced, small_grads, dg_final[0]


SMALL_ORDER = ("mix_norm", "pool_scale", "sgu_norm", "w_spatial", "b_spatial", "ffn_norm", "conv_b", "ple_norm",
               "conv_w")


def _pack_rows(pieces, row_multiple):
    flat = jnp.concatenate([a.reshape(-1) for a in pieces])
    rows = -(-flat.shape[0] // LANES)
    rows = -(-rows // row_multiple) * row_multiple
    return jnp.pad(flat, (0, rows * LANES - flat.shape[0])).reshape(rows, LANES)


def _unpack(flat, shapes):
    out, off = [], 0
    for shp in shapes:
        size = 1
        for s in shp:
            size *= s
        out.append(flat[off:off + size].reshape(shp))
        off += size
    return out


def kernel(x, p, mix_norm, w_in, w_pool, pool_scale, sgu_norm, w_spatial, b_spatial, w_branch_a, w_branch_b, w_out, ffn_norm, w_up, conv_w, conv_b, w_down, ple_norm, w_ple_gate, w_ple, final_norm, loss_target, m_mix_norm, m_w_in, m_w_pool, m_pool_scale, m_sgu_norm, m_w_spatial, m_b_spatial, m_w_branch_a, m_w_branch_b, m_w_out, m_ffn_norm, m_w_up, m_conv_w, m_conv_b, m_w_down, m_ple_norm, m_w_ple_gate, m_w_ple, m_final_norm, v_mix_norm, v_w_in, v_w_pool, v_pool_scale, v_sgu_norm, v_w_spatial, v_b_spatial, v_w_branch_a, v_w_branch_b, v_w_out, v_ffn_norm, v_w_up, v_conv_w, v_conv_b, v_w_down, v_ple_norm, v_w_ple_gate, v_w_ple, v_final_norm):
    names = ["mix_norm", "w_in", "w_pool", "pool_scale", "sgu_norm", "w_spatial", "b_spatial", "w_branch_a",
             "w_branch_b", "w_out", "ffn_norm", "w_up", "conv_w", "conv_b", "w_down", "ple_norm", "w_ple_gate",
             "w_ple", "final_norm"]
    w = dict(zip(names, [mix_norm, w_in, w_pool, pool_scale, sgu_norm, w_spatial, b_spatial, w_branch_a, w_branch_b,
                         w_out, ffn_norm, w_up, conv_w, conv_b, w_down, ple_norm, w_ple_gate, w_ple, final_norm]))
    m = dict(zip(names, [m_mix_norm, m_w_in, m_w_pool, m_pool_scale, m_sgu_norm, m_w_spatial, m_b_spatial,
                         m_w_branch_a, m_w_branch_b, m_w_out, m_ffn_norm, m_w_up, m_conv_w, m_conv_b, m_w_down,
                         m_ple_norm, m_w_ple_gate, m_w_ple, m_final_norm]))
    v = dict(zip(names, [v_mix_norm, v_w_in, v_w_pool, v_pool_scale, v_sgu_norm, v_w_spatial, v_b_spatial,
                         v_w_branch_a, v_w_branch_b, v_w_out, v_ffn_norm, v_w_up, v_conv_w, v_conv_b, v_w_down,
                         v_ple_norm, v_w_ple_gate, v_w_ple, v_final_norm]))
    T = x.shape[1]
    chip = 2 * lax.axis_index("x") + lax.axis_index("y")
    place = jnp.stack([chip, lax.axis_index("c")]).astype(jnp.int32)

    big_names = [t[0] for t in BIG]
    placed = [{k: place_shard(w[k], l, *GEOM[k], BF16, place, name=f"place_{k}_l{l}") for k in big_names}
              for l in range(2)]
    conv_w8 = jnp.pad(conv_w, ((0, 0), (0, CONV_ROWS - conv_w.shape[1]), (0, 0)))
    conv_placed = place_both_layers(conv_w8, 1, conv_w.shape[2], place, name="place_conv_w")
    w_in0, conv_w_all = run_exchanges([gather_exchange([placed[0]["w_in"]], [GEOM["w_in"]]),
                                       gather_by_layer_exchange(conv_placed, 1, conv_w.shape[2])],
                                      name="gather_first_weights")
    placed[0]["w_in"] = w_in0

    small = {k: w[k] for k in ("mix_norm", "pool_scale", "sgu_norm", "w_spatial", "b_spatial", "ffn_norm",
                               "conv_b", "ple_norm", "final_norm")}
    loss_acc, dx, reduced, small_grads, dg_final = _local_step(
        x.reshape(T, D), p.reshape(2 * T, p.shape[-1]), loss_target.reshape(T, D), placed[0], placed[1], conv_w_all,
        small, place)
    loss = lax.psum(loss_acc[0, 0], ("x", "y", "c"))
    full = run_exchanges([share_exchange([reduced[k] for k in big_names],
                                         [(big_names.index(k), 0) for k in LATE_GRADS_L0])], name="share_last_halves")
    grads = dict(zip(big_names, full))

    pieces = [small_grads[l][k] for l in range(2) for k in SMALL_ORDER] + [dg_final]
    shapes = [a.shape for a in pieces]
    total = all_reduce_small(_pack_rows(pieces, 16)).reshape(-1)
    summed = _unpack(total, shapes)
    per_layer = {k: jnp.stack([summed[i], summed[len(SMALL_ORDER) + i]]) for i, k in enumerate(SMALL_ORDER)}
    for k in ("mix_norm", "pool_scale", "sgu_norm", "w_spatial", "b_spatial", "ffn_norm", "conv_b", "ple_norm"):
        grads[k] = per_layer[k]
    grads["final_norm"] = summed[-1]
    cw = conv_w.shape[2]
    grads["conv_w"] = lax.dynamic_slice_in_dim(per_layer["conv_w"], chip * cw, cw, axis=2)

    delta, new_m, new_v = {}, {}, {}
    for name in big_names:
        shp = w[name].shape
        d_, m_, v_, g_ = elementwise(lambda w_, g_, m_, v_: (*_adamw(w_, g_, m_, v_), g_),
                                     [_view2d(a) for a in (w[name], grads[name], m[name], v[name])],
                                     [F32, F32, F32, F32], name=f"adamw_{name}")
        delta[name], new_m[name], new_v[name] = d_.reshape(shp), m_.reshape(shp), v_.reshape(shp)
        grads[name] = g_.reshape(shp)
    small_names = [k for k in names if k not in big_names]
    small_shapes = [w[k].shape for k in small_names]
    packed = [_pack_rows([src[k] for k in small_names], 8) for src in (w, grads, m, v)]
    outs = elementwise(_adamw, packed, [F32, F32, F32], name="adamw_small")
    for dst, o in zip((delta, new_m, new_v), outs):
        for k, a in zip(small_names, _unpack(o.reshape(-1), small_shapes)):
            dst[k] = a

    return (loss, dx.reshape(1, T, D), *[grads[k] for k in names], *[delta[k] for k in names],
            *[new_m[k] for k in names], *[new_v[k] for k in names])
```

```python
import functools

import jax
import jax.numpy as jnp
from jax import lax
from jax.experimental import pallas as pl
from jax.experimental.pallas import tpu as pltpu

F32 = jnp.float32
BF16 = jnp.bfloat16
EPS = 1e-6
D = 1024
POOL_WINDOWS = (2, 4, 8, 16)
PG = 256
POOL_HALO = 16
CHUNK = 128
HEADS = 8
DFF = 2816
CONV_HALO = 8
CONV_TC = 1408
N_CHIPS = 4
LANES = 128
VMEM_LIMIT = 56 * 1024 * 1024
MESH = pl.DeviceIdType.MESH
ANY = pl.BlockSpec(memory_space=pl.ANY)

ADAM_LR = 0.001
ADAM_B1 = 0.9
ADAM_B2 = 0.999
ADAM_EPS = 1e-08
ADAM_WD = 0.01
ADAM_STEP = 10

BIG = (
    ("w_in", (D, 5 * D), 1, 5 * D // N_CHIPS),
    ("w_pool", (4, PG, PG), 1, PG // N_CHIPS),
    ("w_branch_a", (D, D), 0, D // N_CHIPS),
    ("w_branch_b", (D, D), 0, D // N_CHIPS),
    ("w_out", (D, D), 0, D // N_CHIPS),
    ("w_up", (D, 2 * DFF), 1, 2 * DFF // N_CHIPS),
    ("w_down", (DFF, D), 0, DFF // N_CHIPS),
    ("w_ple_gate", (D, D), 0, D // N_CHIPS),
    ("w_ple", (PG, D), 1, D // N_CHIPS),
)
CONV_ROWS = 8


def _params(n_axes):
    return pltpu.CompilerParams(dimension_semantics=("arbitrary",) * n_axes, vmem_limit_bytes=VMEM_LIMIT)


def _gelu(x):
    return 0.5 * x * (1.0 + lax.erf(x * 0.7071067811865476))


def _gelu_grad(x):
    return 0.5 * (1.0 + lax.erf(x * 0.7071067811865476)) + x * jnp.exp(-0.5 * x * x) * 0.3989422804014327


def _shard_shape(shape, axis, size):
    return tuple(size if a == axis else s for a, s in enumerate(shape))


def _block(ref, axis, j, size):
    idx = tuple(pl.ds(j * size, size) if a == axis else slice(None) for a in range(len(ref.shape)))
    return ref.at[idx]


def mm_nn(a, w, l, *, name, rows, out_dtype=F32, resid=None, a_row_off=0, out=None, out_cols=None,
          out_col_off=0, norm_gain=None, host=None, tm=1024, tn=None, tk=None):
    K, N = w.shape[1], w.shape[2]
    tn = tn or N
    tk = tk or K
    nk = K // tk
    out_cols = out_cols or N
    assert rows % tm == 0 and N % tn == 0 and K % tk == 0 and out_col_off % tn == 0 and a_row_off % tm == 0
    has_resid, has_out, has_norm = resid is not None, out is not None, norm_gain is not None
    assert not has_norm or (tn == N and not has_out)
    grid = (N // tn, rows // tm, nk)
    hosting = _Hosting(host)
    n_in = 2 + has_resid + has_norm + has_out
    n_host_in, n_host_out = len(hosting.arrays), len(hosting.out_shapes)
    n_own_out = 1 + has_norm

    def body(*refs):
        refs = list(refs)
        a_ref, w_ref = refs[0], refs[1]
        r_ref = refs[2] if has_resid else None
        g_ref = refs[2 + has_resid] if has_norm else None
        host_in = refs[n_in:n_in + n_host_in]
        o_base = n_in + n_host_in
        o_ref = refs[o_base]
        host_out = refs[o_base + n_own_out:o_base + n_own_out + n_host_out]
        scratch = refs[o_base + n_own_out + n_host_out:]
        if hosting.plan:
            first, last = _first_last(grid)
            sems = scratch[-2:]

            @pl.when(first)
            def _():
                hosting.begin(host_in, host_out, *sems)

        part = jnp.dot(a_ref[...].astype(BF16), w_ref[...], preferred_element_type=F32)

        def finish(r):
            if has_resid:
                r = r + r_ref[...]
            o_ref[...] = r.astype(o_ref.dtype)
            if has_norm:
                scale = lax.rsqrt(jnp.mean(r * r, axis=-1, keepdims=True) + EPS)
                refs[o_base + 1][...] = (r * scale * g_ref[...]).astype(BF16)

        if nk == 1:
            finish(part)
        else:
            acc = scratch[0]
            k = pl.program_id(2)

            @pl.when(k == 0)
            def _():
                acc[...] = part

            @pl.when(k > 0)
            def _():
                acc[...] += part

            @pl.when(k == nk - 1)
            def _():
                finish(acc[...])

        if hosting.plan:
            @pl.when(last)
            def _():
                hosting.finish(host_in, host_out, *sems)

    in_specs = [pl.BlockSpec((tm, tk), lambda j, i, k: (i + a_row_off // tm, k)),
                pl.BlockSpec((None, tk, tn), lambda j, i, k: (l, k, j))]
    args = [a, w]
    if has_resid:
        in_specs.append(pl.BlockSpec((tm, tn), lambda j, i, k: (i, j)))
        args.append(resid)
    if has_norm:
        in_specs.append(pl.BlockSpec((None, 1, tn), lambda j, i, k: (l, 0, 0)))
        args.append(norm_gain)
    aliases = {}
    if has_out:
        in_specs.append(ANY)
        aliases = {len(args): 0}
        args.append(out)
    aliases.update(hosting.aliases(n_in, n_own_out))
    out_specs = [pl.BlockSpec((tm, tn), lambda j, i, k: (i, j + out_col_off // tn))]
    out_shape = [jax.ShapeDtypeStruct((rows, out_cols), out_dtype)]
    if has_norm:
        out_specs.append(pl.BlockSpec((tm, tn), lambda j, i, k: (i, j)))
        out_shape.append(jax.ShapeDtypeStruct((rows, N), BF16))
    res = pl.pallas_call(
        body, name=name, grid=grid,
        in_specs=in_specs + [ANY] * n_host_in,
        out_specs=out_specs + [ANY] * n_host_out,
        out_shape=out_shape + hosting.out_shapes,
        scratch_shapes=([pltpu.VMEM((tm, tn), F32)] if nk > 1 else []) + hosting.scratch(),
        input_output_aliases=aliases, compiler_params=_params(3))(*args, *hosting.arrays)
    own = res[0] if n_own_out == 1 else tuple(res[:n_own_out])
    return (own, list(res[n_own_out:])) if hosting.plan else own


def mm_nt(a, w, l, *, name, rows, kdim=None, a_col_off=0, out_dtype=BF16, tm=1024, tn=None, tk=None):
    R = w.shape[1]
    kdim = kdim or w.shape[2]
    tn = tn or R
    tk = tk or kdim
    nk = kdim // tk
    assert rows % tm == 0 and R % tn == 0 and kdim % tk == 0 and a_col_off % tk == 0

    def body(a_ref, w_ref, o_ref, *scr):
        part = lax.dot_general(a_ref[...].astype(BF16), w_ref[...], (((1,), (1,)), ((), ())),
                               preferred_element_type=F32)
        if nk == 1:
            o_ref[...] = part.astype(o_ref.dtype)
        else:
            acc = scr[0]
            k = pl.program_id(2)

            @pl.when(k == 0)
            def _():
                acc[...] = part

            @pl.when(k > 0)
            def _():
                acc[...] += part

            @pl.when(k == nk - 1)
            def _():
                o_ref[...] = acc[...].astype(o_ref.dtype)

    return pl.pallas_call(
        body, name=name, grid=(R // tn, rows // tm, nk),
        in_specs=[pl.BlockSpec((tm, tk), lambda j, i, k: (i, k + a_col_off // tk)),
                  pl.BlockSpec((None, tn, tk), lambda j, i, k: (l, j, k))],
        out_specs=pl.BlockSpec((tm, tn), lambda j, i, k: (i, j)),
        out_shape=jax.ShapeDtypeStruct((rows, R), out_dtype),
        scratch_shapes=[pltpu.VMEM((tm, tn), F32)] if nk > 1 else [],
        compiler_params=_params(3))(a, w)


def mm_nt_norm_bwd(a, w, l, x, gain, gain_l, dx_in, *, name, rows, tk, host=None, tm=1024):
    R = w.shape[1]
    kdim = w.shape[2]
    nk = kdim // tk
    n_i = rows // tm
    rq = tm // nk
    assert rows % tm == 0 and kdim % tk == 0 and tm % nk == 0 and rq % 8 == 0
    grid = (n_i + 1, nk)
    hosting = _Hosting(host)
    n_host_in, n_host_out = len(hosting.arrays), len(hosting.out_shapes)

    def body(a_ref, w_ref, x_ref, g_ref, dxi_ref, *refs):
        host_in = refs[:n_host_in]
        dx_ref, dg_ref = refs[n_host_in:n_host_in + 2]
        host_out = refs[n_host_in + 2:n_host_in + 2 + n_host_out]
        acc = refs[n_host_in + 2 + n_host_out]
        sems = refs[n_host_in + 3 + n_host_out:]
        i, k = pl.program_id(0), pl.program_id(1)
        cur = i % 2
        if hosting.plan:
            first, last = _first_last(grid)

            @pl.when(first)
            def _():
                hosting.begin(host_in, host_out, *sems)

        @pl.when(jnp.logical_and(i == 0, k == 0))
        def _():
            dg_ref[...] = jnp.zeros_like(dg_ref)

        def matmul():
            part = lax.dot_general(a_ref[...].astype(BF16), w_ref[...], (((1,), (1,)), ((), ())),
                                   preferred_element_type=F32)
            acc[cur] = jnp.where(k == 0, 0.0, acc[cur]) + part

        def element_wise():
            rows_k = pl.ds(pl.multiple_of(k * rq, rq), rq)
            dh = acc[1 - cur, rows_k, :]
            xv = x_ref[rows_k, :]
            r = lax.rsqrt(jnp.mean(xv * xv, axis=-1, keepdims=True) + EPS)
            xh = xv * r
            dhg = dh * g_ref[...]
            dx_ref[rows_k, :] = dxi_ref[rows_k, :] + r * (dhg - xh * jnp.mean(dhg * xh, axis=-1, keepdims=True))
            dg_ref[0:1, :] += jnp.sum(dh * xh, axis=0, keepdims=True)

        @pl.when(i == 0)
        def _():
            matmul()

        @pl.when(jnp.logical_and(i > 0, i < n_i))
        def _():
            matmul()
            element_wise()

        @pl.when(i == n_i)
        def _():
            element_wise()

        if hosting.plan:
            @pl.when(last)
            def _():
                hosting.finish(host_in, host_out, *sems)

    def this(i):
        return jnp.minimum(i, n_i - 1)

    def before(i):
        return jnp.maximum(i - 1, 0)

    if a.ndim == 3:
        per = a.shape[2] // tk
        a_spec = pl.BlockSpec((None, tm, tk), lambda i, k: (k // per, this(i), k % per))
    else:
        a_spec = pl.BlockSpec((tm, tk), lambda i, k: (this(i), k))
    row_tile = pl.BlockSpec((tm, R), lambda i, k: (before(i), 0))
    res = pl.pallas_call(
        body, name=name, grid=grid,
        in_specs=[a_spec, pl.BlockSpec((None, R, tk), lambda i, k: (l, 0, k)), row_tile,
                  pl.BlockSpec((None, 1, R), lambda i, k: (gain_l, 0, 0)), row_tile] + [ANY] * n_host_in,
        out_specs=[row_tile, pl.BlockSpec((8, R), lambda i, k: (0, 0))] + [ANY] * n_host_out,
        out_shape=[jax.ShapeDtypeStruct((rows, R), F32), jax.ShapeDtypeStruct((8, R), F32)] + hosting.out_shapes,
        scratch_shapes=[pltpu.VMEM((2, tm, R), F32)] + hosting.scratch(),
        input_output_aliases=hosting.aliases(5, 2), compiler_params=_params(2))(
            a, w, x, gain, dx_in, *hosting.arrays)
    own = (res[0], res[1])
    return (own, list(res[2:])) if hosting.plan else own


def mm_tn(a, b, *, name, rows, ka, nb, a_row_off=0, b_col_off=0, tm=None, tn=None, tk=2048):
    tm = tm or ka
    tn = tn or nb
    tk = min(tk, rows)
    nk = rows // tk
    assert ka % tm == 0 and nb % tn == 0 and rows % tk == 0 and b_col_off % tn == 0 and a_row_off % tk == 0

    def body(a_ref, b_ref, o_ref, acc):
        part = lax.dot_general(a_ref[...].astype(BF16), b_ref[...].astype(BF16), (((0,), (0,)), ((), ())),
                               preferred_element_type=F32)
        k = pl.program_id(2)

        @pl.when(k == 0)
        def _():
            acc[...] = part

        @pl.when(k > 0)
        def _():
            acc[...] += part

        @pl.when(k == nk - 1)
        def _():
            o_ref[...] = acc[...].astype(o_ref.dtype)

    if b.ndim == 3:
        per = b.shape[2] // tn
        b_spec = pl.BlockSpec((None, tk, tn), lambda j, i, k: (j // per, k, j % per))
    else:
        b_spec = pl.BlockSpec((tk, tn), lambda j, i, k: (k, j + b_col_off // tn))
    return pl.pallas_call(
        body, name=name, grid=(nb // tn, ka // tm, nk),
        in_specs=[pl.BlockSpec((tk, tm), lambda j, i, k: (k + a_row_off // tk, i)), b_spec],
        out_specs=pl.BlockSpec((None, tm, tn), lambda j, i, k: (0, i, j)),
        out_shape=jax.ShapeDtypeStruct((1, ka, nb), BF16),
        scratch_shapes=[pltpu.VMEM((tm, tn), F32)], compiler_params=_params(3))(a, b)


def _row_spec(tm, width, col=0):
    return pl.BlockSpec((tm, width), lambda i: (i, col))


def _gain_spec(l, width=D):
    return pl.BlockSpec((None, 1, width), lambda i: (l, 0, 0))


def norm_fwd(x, g3, l, *, name, tm=1024):
    T = x.shape[0]

    def body(x_ref, g_ref, o_ref):
        xv = x_ref[...]
        r = lax.rsqrt(jnp.mean(xv * xv, axis=-1, keepdims=True) + EPS)
        o_ref[...] = (xv * r * g_ref[...]).astype(BF16)

    return pl.pallas_call(
        body, name=name, grid=(T // tm,),
        in_specs=[_row_spec(tm, D), _gain_spec(l)], out_specs=_row_spec(tm, D),
        out_shape=jax.ShapeDtypeStruct((T, D), BF16), compiler_params=_params(1))(x, g3)


def _winsum_back(ext, w):
    s, span = ext, 1
    while span < w:
        s = s + pltpu.roll(s, span, 0)
        span *= 2
    return s


def _winsum_fwd(ext, w):
    rows = ext.shape[0]
    s, span = ext, 1
    while span < w:
        s = s + pltpu.roll(s, rows - span, 0)
        span *= 2
    return s


def _pooled(ext, z, t, g, w):
    sl = slice(g * PG, (g + 1) * PG)
    s = _winsum_back(ext[:, sl], w)[POOL_HALO:, :]
    return s / jnp.minimum(t + 1, w).astype(F32) - z[:, sl]


def pool_fwd(z, wpool, scale3, l, *, name, tm=256):
    T = z.shape[0]
    hb = tm // POOL_HALO
    wl = l if wpool.shape[0] > 1 else 0

    def body(z_ref, zp_ref, wp_ref, sc_ref, o_ref):
        i = pl.program_id(0)
        zv = z_ref[...].astype(F32)
        prev = jnp.where(i == 0, 0.0, zp_ref[...].astype(F32))
        ext = jnp.concatenate([prev, zv], axis=0)
        t = i * tm + lax.broadcasted_iota(jnp.int32, (tm, 1), 0)
        for g, w in enumerate(POOL_WINDOWS):
            sl = slice(g * PG, (g + 1) * PG)
            pooled = _pooled(ext, zv, t, g, w)
            q = jnp.dot(pooled.astype(BF16), wp_ref[g], preferred_element_type=F32)
            o_ref[:, sl] = (q * sc_ref[:, sl]).astype(BF16)

    return pl.pallas_call(
        body, name=name, grid=(T // tm,),
        in_specs=[_row_spec(tm, D),
                  pl.BlockSpec((POOL_HALO, D), lambda i: (jnp.maximum(i * hb - 1, 0), 0)),
                  pl.BlockSpec((None, 4, PG, PG), lambda i: (wl, 0, 0, 0)),
                  _gain_spec(l)],
        out_specs=_row_spec(tm, D),
        out_shape=jax.ShapeDtypeStruct((T, D), BF16), compiler_params=_params(1))(z, z, wpool, scale3)


def sgu_fwd(z, g3, wsm, bT, l, *, name, tm=256):
    T = z.shape[0]

    def body(zu_ref, zv_ref, g_ref, ws_ref, b_ref, o_ref):
        gu = _gelu(zu_ref[...].astype(F32))
        gv = _gelu(zv_ref[...].astype(F32))
        rv = lax.rsqrt(jnp.mean(gv * gv, axis=-1, keepdims=True) + EPS)
        vn = (gv * rv * g_ref[...]).astype(BF16)
        for n in range(tm // CHUNK):
            r = slice(n * CHUNK, (n + 1) * CHUNK)
            for h in range(HEADS):
                cs = slice(h * CHUNK, (h + 1) * CHUNK)
                mixed = jnp.dot(ws_ref[h], vn[r, cs], preferred_element_type=F32) + b_ref[:, h:h + 1]
                o_ref[r, cs] = (gu[r, cs] * mixed).astype(BF16)

    return pl.pallas_call(
        body, name=name, grid=(T // tm,),
        in_specs=[_row_spec(tm, D, 1), _row_spec(tm, D, 2), _gain_spec(l),
                  pl.BlockSpec((None, HEADS, CHUNK, CHUNK), lambda i: (l, 0, 0, 0)),
                  pl.BlockSpec((None, CHUNK, HEADS), lambda i: (l, 0, 0))],
        out_specs=_row_spec(tm, D),
        out_shape=jax.ShapeDtypeStruct((T, D), BF16), compiler_params=_params(1))(z, z, g3, wsm, bT)


def gate_fwd(z, yab, *, name, tm=1024):
    T = z.shape[0]

    def body(za_ref, zb_ref, y_ref, o_ref):
        ga = jax.nn.sigmoid(za_ref[...].astype(F32))
        gb = jax.nn.sigmoid(zb_ref[...].astype(F32))
        o_ref[...] = (ga * y_ref[:, :D].astype(F32) + gb * y_ref[:, D:].astype(F32)).astype(BF16)

    return pl.pallas_call(
        body, name=name, grid=(T // tm,),
        in_specs=[_row_spec(tm, D, 3), _row_spec(tm, D, 4), _row_spec(tm, 2 * D)],
        out_specs=_row_spec(tm, D),
        out_shape=jax.ShapeDtypeStruct((T, D), BF16), compiler_params=_params(1))(z, z, yab)


def _conv(ext, w_ref, b_ref):
    down1, down2 = pltpu.roll(ext, 1, 0), pltpu.roll(ext, 2, 0)
    c = b_ref[...] + w_ref[0:1, :] * down2
    c = c + w_ref[1:2, :] * down1
    return c + w_ref[2:3, :] * ext, down1, down2


def conv_fwd(up, convw, convb3, l, *, name, tm=256):
    T = up.shape[0]
    tc = CONV_TC
    nc = DFF // tc
    hb = tm // CONV_HALO

    def body(ua_ref, uap_ref, ub_ref, ubp_ref, wa_ref, wb_ref, ba_ref, bb_ref, o_ref):
        i = pl.program_id(1)

        def conv_of(u_ref, p_ref, w_ref, b_ref):
            ext = jnp.concatenate([jnp.where(i == 0, 0.0, p_ref[...]), u_ref[...]], axis=0)
            return _conv(ext, w_ref, b_ref)[0][CONV_HALO:, :]

        ca = conv_of(ua_ref, uap_ref, wa_ref, ba_ref)
        cb = conv_of(ub_ref, ubp_ref, wb_ref, bb_ref)
        o_ref[...] = (_gelu(ca) * cb).astype(BF16)

    def cur(off):
        return pl.BlockSpec((tm, tc), lambda j, i: (i, j + off))

    def prev(off):
        return pl.BlockSpec((CONV_HALO, tc), lambda j, i: (jnp.maximum(i * hb - 1, 0), j + off))

    def wspec(off):
        return pl.BlockSpec((None, CONV_ROWS, tc), lambda j, i: (l, 0, j + off))

    def bspec(off):
        return pl.BlockSpec((None, 1, tc), lambda j, i: (l, 0, j + off))

    return pl.pallas_call(
        body, name=name, grid=(nc, T // tm),
        in_specs=[cur(0), prev(0), cur(nc), prev(nc), wspec(0), wspec(nc), bspec(0), bspec(nc)],
        out_specs=pl.BlockSpec((tm, tc), lambda j, i: (i, j)),
        out_shape=jax.ShapeDtypeStruct((T, DFF), BF16),
        compiler_params=_params(2))(up, up, up, up, convw, convw, convb3, convb3)


def ple_fwd(x2, pg, e, g3, l, *, name, tm=1024):
    T = x2.shape[0]
    has_norm = g3 is not None

    def body(x_ref, pg_ref, e_ref, *rest):
        xv = x_ref[...] + jax.nn.sigmoid(pg_ref[...].astype(F32)) * e_ref[...].astype(F32)
        if has_norm:
            g_ref, o_ref, h_ref = rest
            r = lax.rsqrt(jnp.mean(xv * xv, axis=-1, keepdims=True) + EPS)
            h_ref[...] = (xv * r * g_ref[...]).astype(BF16)
        else:
            o_ref, = rest
        o_ref[...] = xv

    x_shape = jax.ShapeDtypeStruct((T, D), F32)
    return pl.pallas_call(
        body, name=name, grid=(T // tm,),
        in_specs=[_row_spec(tm, D)] * 3 + ([_gain_spec(l)] if has_norm else []),
        out_specs=[_row_spec(tm, D)] * 2 if has_norm else _row_spec(tm, D),
        out_shape=[x_shape, jax.ShapeDtypeStruct((T, D), BF16)] if has_norm else x_shape,
        compiler_params=_params(1))(x2, pg, e, *([g3] if has_norm else []))


def loss_head(x, g3, tgt, *, name, tm=1024):
    T = x.shape[0]

    def body(x_ref, g_ref, t_ref, loss_ref, dx_ref, dg_ref):
        @pl.when(pl.program_id(0) == 0)
        def _():
            loss_ref[...] = jnp.zeros_like(loss_ref)
            dg_ref[...] = jnp.zeros_like(dg_ref)

        xv, g = x_ref[...], g_ref[...]
        r = lax.rsqrt(jnp.mean(xv * xv, axis=-1, keepdims=True) + EPS)
        xh = xv * r
        err = xh * g - t_ref[...]
        loss_ref[...] += 0.5 * jnp.sum(jnp.mean(err * err, axis=-1, keepdims=True))
        dy = err * (1.0 / D)
        dyg = dy * g
        dx_ref[...] = r * (dyg - xh * jnp.mean(dyg * xh, axis=-1, keepdims=True))
        dg_ref[0:1, :] += jnp.sum(dy * xh, axis=0, keepdims=True)

    return pl.pallas_call(
        body, name=name, grid=(T // tm,),
        in_specs=[_row_spec(tm, D), pl.BlockSpec((1, D), lambda i: (0, 0)), _row_spec(tm, D)],
        out_specs=[pl.BlockSpec((8, LANES), lambda i: (0, 0)), _row_spec(tm, D),
                   pl.BlockSpec((8, D), lambda i: (0, 0))],
        out_shape=[jax.ShapeDtypeStruct((8, LANES), F32), jax.ShapeDtypeStruct((T, D), F32),
                   jax.ShapeDtypeStruct((8, D), F32)],
        compiler_params=_params(1))(x, g3, tgt)


def ple_bwd(dx, pg, e, *, name, tm=1024):
    T = dx.shape[0]

    def body(dx_ref, pg_ref, e_ref, de_ref, dpg_ref):
        gate = jax.nn.sigmoid(pg_ref[...].astype(F32))
        dxv = dx_ref[...]
        de_ref[...] = (dxv * gate).astype(BF16)
        dpg_ref[...] = (dxv * e_ref[...].astype(F32) * gate * (1.0 - gate)).astype(BF16)

    return pl.pallas_call(
        body, name=name, grid=(T // tm,),
        in_specs=[_row_spec(tm, D)] * 3, out_specs=[_row_spec(tm, D)] * 2,
        out_shape=[jax.ShapeDtypeStruct((T, D), BF16)] * 2, compiler_params=_params(1))(dx, pg, e)


def conv_bwd(df, up, convw, convb3, l, *, name, host=None, tm=256):
    T = up.shape[0]
    tc = CONV_TC
    nc = DFF // tc
    hb = tm // CONV_HALO
    nt = T // tm
    rows = tm + 2 * CONV_HALO
    own = slice(CONV_HALO, CONV_HALO + tm)

    hosting = _Hosting(host)
    n_host_in, n_host_out = len(hosting.arrays), len(hosting.out_shapes)

    def body(df_ref, dfn_ref, ua_ref, uap_ref, uan_ref, ub_ref, ubp_ref, ubn_ref, wa_ref, wb_ref, ba_ref, bb_ref,
             *rest):
        host_in = rest[:n_host_in]
        dup_ref, dcw_ref, dcb_ref = rest[n_host_in:n_host_in + 3]
        host_out = rest[n_host_in + 3:n_host_in + 3 + n_host_out]
        sems = rest[n_host_in + 3 + n_host_out:]
        i = pl.program_id(1)
        if hosting.plan:
            first, last = _first_last((nc, nt))

            @pl.when(first)
            def _():
                hosting.begin(host_in, host_out, *sems)

        @pl.when(i == 0)
        def _():
            dcw_ref[...] = jnp.zeros_like(dcw_ref)
            dcb_ref[...] = jnp.zeros_like(dcb_ref)

        def ext_of(c_ref, p_ref, n_ref):
            return jnp.concatenate([jnp.where(i == 0, 0.0, p_ref[...]), c_ref[...],
                                    jnp.where(i == nt - 1, 0.0, n_ref[...])], axis=0)

        ea = ext_of(ua_ref, uap_ref, uan_ref)
        eb = ext_of(ub_ref, ubp_ref, ubn_ref)
        ca, ea1, ea2 = _conv(ea, wa_ref, ba_ref)
        cb, eb1, eb2 = _conv(eb, wb_ref, bb_ref)
        df_ext =jnp.concatenate([jnp.zeros((CONV_HALO, tc), F32), df_ref[...],
                                  jnp.where(i == nt - 1, 0.0, dfn_ref[...])], axis=0)
        cdf = 0.5 * (1.0 + lax.erf(ca * 0.7071067811865476))
        da = df_ext * cb * (cdf + ca * jnp.exp(-0.5 * ca * ca) * 0.3989422804014327)
        db = df_ext * (ca * cdf)

        def finish(h, dc, e, e1, e2, w_ref):
            dup = w_ref[2:3, :] * dc + w_ref[1:2, :] * pltpu.roll(dc, rows - 1, 0)
            dup = dup + w_ref[0:1, :] * pltpu.roll(dc, rows - 2, 0)
            dup_ref[h] = dup[own, :].astype(BF16)
            dco = dc[own, :]
            dcb_ref[h, 0:1, :] += jnp.sum(dco, axis=0, keepdims=True)
            dcw_ref[h, 0:1, :] += jnp.sum(dco * e2[own, :], axis=0, keepdims=True)
            dcw_ref[h, 1:2, :] += jnp.sum(dco * e1[own, :], axis=0, keepdims=True)
            dcw_ref[h, 2:3, :] += jnp.sum(dco * e[own, :], axis=0, keepdims=True)

        finish(0, da, ea, ea1, ea2, wa_ref)
        finish(1, db, eb, eb1, eb2, wb_ref)
        if hosting.plan:
            @pl.when(last)
            def _():
                hosting.finish(host_in, host_out, *sems)

    def nxt(i):
        return jnp.minimum((i + 1) * hb, T // CONV_HALO - 1)

    def prv(i):
        return jnp.maximum(i * hb - 1, 0)

    def up_specs(off):
        return [pl.BlockSpec((tm, tc), lambda j, i: (i, j + off)),
                pl.BlockSpec((CONV_HALO, tc), lambda j, i: (prv(i), j + off)),
                pl.BlockSpec((CONV_HALO, tc), lambda j, i: (nxt(i), j + off))]

    in_specs = [pl.BlockSpec((tm, tc), lambda j, i: (i, j)),
                pl.BlockSpec((CONV_HALO, tc), lambda j, i: (nxt(i), j)),
                *up_specs(0), *up_specs(nc),
                pl.BlockSpec((None, CONV_ROWS, tc), lambda j, i: (l, 0, j)),
                pl.BlockSpec((None, CONV_ROWS, tc), lambda j, i: (l, 0, j + nc)),
                pl.BlockSpec((None, 1, tc), lambda j, i: (l, 0, j)),
                pl.BlockSpec((None, 1, tc), lambda j, i: (l, 0, j + nc))]
    res = pl.pallas_call(
        body, name=name, grid=(nc, nt), in_specs=in_specs + [ANY] * n_host_in,
        out_specs=[pl.BlockSpec((2, tm, tc), lambda j, i: (0, i, j)),
                   pl.BlockSpec((2, 8, tc), lambda j, i: (0, 0, j)),
                   pl.BlockSpec((2, 8, tc), lambda j, i: (0, 0, j))] + [ANY] * n_host_out,
        out_shape=[jax.ShapeDtypeStruct((2, T, DFF), BF16), jax.ShapeDtypeStruct((2, 8, DFF), F32),
                   jax.ShapeDtypeStruct((2, 8, DFF), F32)] + hosting.out_shapes,
        scratch_shapes=hosting.scratch(), input_output_aliases=hosting.aliases(12, 3),
        compiler_params=_params(2))(df, df, up, up, up, up, up, up, convw, convw, convb3, convb3, *hosting.arrays)
    return res[0], res[1], res[2], list(res[3:])


def gate_bwd(dmo, z, yab, *, name, tm=1024):
    T = z.shape[0]

    def body(dmo_ref, zg_ref, y_ref, dz_ref, dy_ref):
        g = jax.nn.sigmoid(zg_ref[...].astype(F32))
        dmo_v = dmo_ref[...].astype(F32)
        dy_ref[...] = (dmo_v * g).astype(BF16)
        dz_ref[...] = (dmo_v * y_ref[...].astype(F32) * g * (1.0 - g)).astype(BF16)

    return pl.pallas_call(
        body, name=name, grid=(T // tm, 2),
        in_specs=[pl.BlockSpec((tm, D), lambda i, s: (i, 0)),
                  pl.BlockSpec((tm, D), lambda i, s: (i, 3 + s)),
                  pl.BlockSpec((tm, D), lambda i, s: (i, s))],
        out_specs=[pl.BlockSpec((tm, D), lambda i, s: (i, 3 + s)),
                   pl.BlockSpec((tm, D), lambda i, s: (i, s))],
        out_shape=[jax.ShapeDtypeStruct((T, 5 * D), BF16), jax.ShapeDtypeStruct((T, 2 * D), BF16)],
        compiler_params=_params(2))(dmo, z, yab)


def mixer_bwd(da, ds, z, dz, wpool, scale3, g3, wsm, wsmT, bT, l, *, name, tm=256):
    T = z.shape[0]
    hb = tm // POOL_HALO
    nt = T // tm

    def body(da_ref, dan_ref, ds_ref, zp_ref, zpp_ref, zu_ref, zv_ref, wp_ref, sc_ref, g_ref, ws_ref, wst_ref,
             b_ref, dzin_ref, dz_ref, dwp_ref, dsc_ref, dws_ref, dbt_ref, dgs_ref, mixed_scr, dvn_scr, db_scr):
        del dzin_ref
        i = pl.program_id(0)

        @pl.when(i == 0)
        def _():
            dwp_ref[...] = jnp.zeros_like(dwp_ref)
            dsc_ref[...] = jnp.zeros_like(dsc_ref)
            dws_ref[...] = jnp.zeros_like(dws_ref)
            dgs_ref[...] = jnp.zeros_like(dgs_ref)
            db_scr[...] = jnp.zeros_like(db_scr)

        zv_p = zp_ref[...].astype(F32)
        ext = jnp.concatenate([jnp.where(i == 0, 0.0, zpp_ref[...].astype(F32)), zv_p], axis=0)
        da_v = da_ref[...].astype(F32)
        da_ext = jnp.concatenate([da_v, jnp.where(i == nt - 1, 0.0, dan_ref[...].astype(F32))], axis=0)
        t = i * tm + lax.broadcasted_iota(jnp.int32, (tm, 1), 0)
        t_ext = i * tm + lax.broadcasted_iota(jnp.int32, (tm + POOL_HALO, 1), 0)
        for g, w in enumerate(POOL_WINDOWS):
            sl = slice(g * PG, (g + 1) * PG)
            pooled = _pooled(ext, zv_p, t, g, w).astype(BF16)
            q = jnp.dot(pooled, wp_ref[g], preferred_element_type=F32)
            dsc_ref[0:1, sl] += jnp.sum(da_v[:, sl] * q, axis=0, keepdims=True)
            dq_ext = (da_ext[:, sl] * sc_ref[:, sl]).astype(BF16)
            dwp_ref[g] += lax.dot_general(pooled, dq_ext[:tm, :], (((0,), (0,)), ((), ())),
                                          preferred_element_type=F32)
            dpool = lax.dot_general(dq_ext, wp_ref[g], (((1,), (1,)), ((), ())), preferred_element_type=F32)
            spread = _winsum_fwd(dpool / jnp.minimum(t_ext + 1, w).astype(F32), w)
            dz_ref[:, sl] = (spread[:tm, :] - dpool[:tm, :]).astype(BF16)

        zu, zv, ds_v = zu_ref[...].astype(F32), zv_ref[...].astype(F32), ds_ref[...].astype(F32)
        gain = g_ref[...]
        gu, gv = _gelu(zu), _gelu(zv)
        rv = lax.rsqrt(jnp.mean(gv * gv, axis=-1, keepdims=True) + EPS)
        vh = gv * rv
        vn = (vh * gain).astype(BF16)
        dmix = ds_v * gu
        dmix_b = dmix.astype(BF16)
        for n in range(tm // CHUNK):
            r = slice(n * CHUNK, (n + 1) * CHUNK)
            db_scr[...] += dmix[r, :]
            for h in range(HEADS):
                cs = slice(h * CHUNK, (h + 1) * CHUNK)
                mixed_scr[r, cs] = jnp.dot(ws_ref[h], vn[r, cs], preferred_element_type=F32) + b_ref[:, h:h + 1]
                dws_ref[h] += lax.dot_general(dmix_b[r, cs], vn[r, cs], (((1,), (1,)), ((), ())),
                                              preferred_element_type=F32)
                dvn_scr[r, cs] = jnp.dot(wst_ref[h], dmix_b[r, cs], preferred_element_type=F32)
        dz_ref[:, D:2 * D] = (ds_v * mixed_scr[...] * _gelu_grad(zu)).astype(BF16)
        dvn = dvn_scr[...]
        dgs_ref[0:1, :] += jnp.sum(dvn * vh, axis=0, keepdims=True)
        dvg = dvn * gain
        dgv = rv * (dvg - vh * jnp.mean(dvg * vh, axis=-1, keepdims=True))
        dz_ref[:, 2 * D:3 * D] = (dgv * _gelu_grad(zv)).astype(BF16)

        @pl.when(i == nt - 1)
        def _():
            tril = (lax.broadcasted_iota(jnp.int32, (CHUNK, CHUNK), 0)
                    >= lax.broadcasted_iota(jnp.int32, (CHUNK, CHUNK), 1)).astype(F32)
            for h in range(HEADS):
                dws_ref[h] = dws_ref[h] * tril
                dbt_ref[:, h:h + 1] = jnp.sum(db_scr[:, h * CHUNK:(h + 1) * CHUNK], axis=1, keepdims=True)

    const4 = lambda i: (l, 0, 0, 0)
    wl = l if wpool.shape[0] > 1 else 0
    in_specs = [
        _row_spec(tm, D),
        pl.BlockSpec((POOL_HALO, D), lambda i: (jnp.minimum((i + 1) * hb, T // POOL_HALO - 1), 0)),
        _row_spec(tm, D),
        _row_spec(tm, D, 0),
        pl.BlockSpec((POOL_HALO, D), lambda i: (jnp.maximum(i * hb - 1, 0), 0)),
        _row_spec(tm, D, 1), _row_spec(tm, D, 2),
        pl.BlockSpec((None, 4, PG, PG), lambda i: (wl, 0, 0, 0)),
        _gain_spec(l), _gain_spec(l),
        pl.BlockSpec((None, HEADS, CHUNK, CHUNK), const4),
        pl.BlockSpec((None, HEADS, CHUNK, CHUNK), const4),
        pl.BlockSpec((None, CHUNK, HEADS), lambda i: (l, 0, 0)),
        ANY,
    ]
    out_specs = [
        pl.BlockSpec((tm, 3 * D), lambda i: (i, 0)),
        pl.BlockSpec((4, PG, PG), lambda i: (0, 0, 0)),
        pl.BlockSpec((8, D), lambda i: (0, 0)),
        pl.BlockSpec((HEADS, CHUNK, CHUNK), lambda i: (0, 0, 0)),
        pl.BlockSpec((CHUNK, HEADS), lambda i: (0, 0)),
        pl.BlockSpec((8, D), lambda i: (0, 0)),
    ]
    out_shape = [
        jax.ShapeDtypeStruct((T, 5 * D), BF16), jax.ShapeDtypeStruct((4, PG, PG), F32),
        jax.ShapeDtypeStruct((8, D), F32), jax.ShapeDtypeStruct((HEADS, CHUNK, CHUNK), F32),
        jax.ShapeDtypeStruct((CHUNK, HEADS), F32), jax.ShapeDtypeStruct((8, D), F32),
    ]
    return pl.pallas_call(
        body, name=name, grid=(nt,), in_specs=in_specs, out_specs=out_specs, out_shape=out_shape,
        scratch_shapes=[pltpu.VMEM((tm, D), F32), pltpu.VMEM((tm, D), F32), pltpu.VMEM((CHUNK, D), F32)],
        input_output_aliases={13: 0}, compiler_params=_params(1))(
            da, da, ds, z, z, z, z, wpool, scale3, g3, wsm, wsmT, bT, dz)


def _row_tile(rows, cols, sub):
    cap = max(sub, (2 * 1024 * 1024) // (4 * cols))
    best = None
    for tr in range(sub, min(rows, cap) + 1, sub):
        if rows % tr == 0:
            best = tr
    return best or rows


def elementwise(fn, ins, out_dtypes, *, name, row_blk_offs=None, rows=None):
    cols = ins[0].shape[1]
    rows = rows or ins[0].shape[0]
    tr = _row_tile(rows, cols, 16)
    offs = row_blk_offs or [0] * len(ins)
    n_in = len(ins)

    def body(*refs):
        outs = fn(*[r[...] for r in refs[:n_in]])
        for o_ref, o in zip(refs[n_in:], outs):
            o_ref[...] = o.astype(o_ref.dtype)

    return pl.pallas_call(
        body, name=name, grid=(rows // tr,),
        in_specs=[pl.BlockSpec((tr, cols), functools.partial(lambda i, o: (i + o * (rows // tr), 0), o=o))
                  for o in offs],
        out_specs=[pl.BlockSpec((tr, cols), lambda i: (i, 0)) for _ in out_dtypes],
        out_shape=[jax.ShapeDtypeStruct((rows, cols), dt) for dt in out_dtypes],
        compiler_params=_params(1))(*ins)


def _adamw(w, g, m, v):
    m = ADAM_B1 * m + (1.0 - ADAM_B1) * g
    v = ADAM_B2 * v + (1.0 - ADAM_B2) * jnp.square(g)
    m_hat = m / (1.0 - ADAM_B1 ** ADAM_STEP)
    v_hat = v / (1.0 - ADAM_B2 ** ADAM_STEP)
    delta = -ADAM_LR * (m_hat / (jnp.sqrt(v_hat) + ADAM_EPS) + ADAM_WD * w)
    return delta, m, v


def _view2d(a):
    return a.reshape(-1, a.shape[-1])


def _place():
    x, y, c = lax.axis_index("x"), lax.axis_index("y"), lax.axis_index("c")
    others = [(1 - x, y), (x, 1 - y), (1 - x, 1 - y)]
    return x, y, c, 2 * x + y, others


def _remote(src, dst, send_sems, recv_sems, k, to):
    return pltpu.make_async_remote_copy(src_ref=src, dst_ref=dst, send_sem=send_sems.at[k], recv_sem=recv_sems.at[k],
                                        device_id=to, device_id_type=MESH)


def _half(ref, axis, j, size, h):
    if len(ref.shape) == 3:
        return ref.at[:, pl.ds(j * size + h * (size // 2), size // 2), :]
    if axis == 0:
        return ref.at[pl.ds(j * size + h * (size // 2), size // 2), :]
    rows = ref.shape[0] // 2
    return ref.at[pl.ds(h * rows, rows), pl.ds(j * size, size)]


def _half_shard_shape(shape, axis, size):
    if len(shape) == 3:
        return (shape[0], size // 2, shape[2])
    if axis == 0:
        return (size // 2, shape[1])
    return (shape[0] // 2, size)


class Exchange:
    def __init__(self, arrays, out_shapes, aliases, n_sems, begin, finish):
        self.arrays, self.out_shapes, self.aliases, self.n_sems = list(arrays), list(out_shapes), aliases, n_sems
        self.begin, self.finish = begin, finish


class _Hosting:
    def __init__(self, plan):
        self.plan = list(plan or [])
        self.arrays = [a for ex in self.plan for a in ex.arrays]
        self.out_shapes = [o for ex in self.plan for o in ex.out_shapes]
        self.n_sems = sum(ex.n_sems for ex in self.plan)

    def scratch(self):
        return [pltpu.SemaphoreType.DMA((self.n_sems,)), pltpu.SemaphoreType.DMA((self.n_sems,))] if self.plan else []

    def aliases(self, in_base, out_base):
        out, i0, o0 = {}, in_base, out_base
        for ex in self.plan:
            out.update({i0 + i: o0 + o for i, o in ex.aliases.items()})
            i0, o0 = i0 + len(ex.arrays), o0 + len(ex.out_shapes)
        return out

    def _each(self, in_refs, out_refs):
        i0 = o0 = s0 = 0
        for ex in self.plan:
            yield ex, in_refs[i0:i0 + len(ex.arrays)], out_refs[o0:o0 + len(ex.out_shapes)], s0
            i0, o0, s0 = i0 + len(ex.arrays), o0 + len(ex.out_shapes), s0 + ex.n_sems

    def begin(self, in_refs, out_refs, send_sems, recv_sems):
        for ex, ins, outs, s0 in self._each(in_refs, out_refs):
            ex.begin(ins, outs, send_sems, recv_sems, s0)

    def finish(self, in_refs, out_refs, send_sems, recv_sems):
        for ex, ins, outs, s0 in self._each(in_refs, out_refs):
            ex.finish(ins, outs, send_sems, recv_sems, s0)


def _first_last(grid):
    ids = [pl.program_id(a) for a in range(len(grid))]
    first = functools.reduce(jnp.logical_and, [i == 0 for i in ids])
    last = functools.reduce(jnp.logical_and, [i == g - 1 for i, g in zip(ids, grid)])
    return first, last


def run_exchanges(plan, *, name):
    host = _Hosting(plan)
    n_in, n_out = len(host.arrays), len(host.out_shapes)

    def body(*refs):
        ins, outs = refs[:n_in], refs[n_in:n_in + n_out]
        send_sems, recv_sems = refs[n_in + n_out:]
        host.begin(ins, outs, send_sems, recv_sems)
        host.finish(ins, outs, send_sems, recv_sems)

    return pl.pallas_call(
        body, name=name, in_specs=[ANY] * n_in, out_specs=[ANY] * n_out, out_shape=host.out_shapes,
        scratch_shapes=host.scratch(), input_output_aliases=host.aliases(0, 0),
        compiler_params=pltpu.CompilerParams(has_side_effects=True))(*host.arrays)


def place_shard(src, l, axis, size, out_dtype, place, *, name):
    shard = src.shape[1:]
    natural = tuple(size * N_CHIPS if a == axis else s for a, s in enumerate(shard))
    if len(shard) == 3:
        blk = (None,) + shard
        grid = (1,)
        in_map = lambda i, pr: (l, 0, 0, 0)
        out_map = lambda i, pr: (0, 0, pr[0], 0)
    else:
        tr = _row_tile(shard[0], shard[1], 16)
        steps = shard[0] // tr
        blk = (None, tr, shard[1])
        grid = (steps,)
        in_map = lambda i, pr: (l, i, 0)
        if axis == 0:
            out_map = lambda i, pr: (0, pr[0] * steps + i, 0)
        else:
            out_map = lambda i, pr: (0, i, pr[0])

    def body(pr_ref, s_ref, o_ref):
        del pr_ref
        o_ref[...] = s_ref[...].astype(o_ref.dtype)

    return pl.pallas_call(
        body, name=name,
        grid_spec=pltpu.PrefetchScalarGridSpec(
            num_scalar_prefetch=1, grid=grid, in_specs=[pl.BlockSpec(blk, in_map)],
            out_specs=pl.BlockSpec(blk, out_map)),
        out_shape=jax.ShapeDtypeStruct((1,) + natural, out_dtype), compiler_params=_params(1))(place, src)


def place_both_layers(src, axis, size, place, *, name):
    rows, cols = src.shape[1], src.shape[2]

    def body(pr_ref, s_ref, o_ref):
        del pr_ref
        o_ref[...] = s_ref[...]

    return pl.pallas_call(
        body, name=name,
        grid_spec=pltpu.PrefetchScalarGridSpec(
            num_scalar_prefetch=1, grid=(2,), in_specs=[pl.BlockSpec((None, rows, cols), lambda lyr, pr: (lyr, 0, 0))],
            out_specs=pl.BlockSpec((None, rows, cols), lambda lyr, pr: (lyr, 0, pr[0]))),
        out_shape=jax.ShapeDtypeStruct((2, rows, cols * N_CHIPS), src.dtype), compiler_params=_params(1))(place, src)


def gather_exchange(arrays, geom):
    n = len(arrays)

    def begin(ins, outs, send_sems, recv_sems, s0):
        x, y, c, j, others = _place()
        for t, (axis, size) in enumerate(geom):
            mine = _half(outs[t].at[0], axis, j, size, c)
            for k, (ox, oy) in enumerate(others):
                _remote(mine, mine, send_sems, recv_sems, s0 + 6 * t + k, (ox, oy, c)).start()

    def finish(ins, outs, send_sems, recv_sems, s0):
        x, y, c, j, others = _place()
        sib = (x, y, 1 - c)
        passed = []
        for t, (axis, size) in enumerate(geom):
            for k, (ox, oy) in enumerate(others):
                landed = _half(outs[t].at[0], axis, 2 * ox + oy, size, c)
                _remote(landed, landed, send_sems, recv_sems, s0 + 6 * t + k, (ox, oy, c)).wait_recv()
                fwd = _remote(landed, landed, send_sems, recv_sems, s0 + 6 * t + 3 + k, sib)
                fwd.start()
                passed.append(fwd)
        for t, (axis, size) in enumerate(geom):
            for k, (ox, oy) in enumerate(others):
                got = _half(outs[t].at[0], axis, 2 * ox + oy, size, 1 - c)
                _remote(got, got, send_sems, recv_sems, s0 + 6 * t + 3 + k, sib).wait_recv()
        for fwd in passed:
            fwd.wait_send()
        for t, (axis, size) in enumerate(geom):
            mine = _half(outs[t].at[0], axis, j, size, c)
            for k, (ox, oy) in enumerate(others):
                _remote(mine, mine, send_sems, recv_sems, s0 + 6 * t + k, (ox, oy, c)).wait_send()

    return Exchange(arrays, [jax.ShapeDtypeStruct(a.shape, a.dtype) for a in arrays], {t: t for t in range(n)},
                    6 * n, begin, finish)


def gather_by_layer_exchange(array, axis, size):
    def blocks(out, others, lyr):
        return [_block(out.at[lyr], axis, 2 * ox + oy, size) for (ox, oy) in others]

    def begin(ins, outs, send_sems, recv_sems, s0):
        x, y, c, j, others = _place()
        mine = _block(outs[0].at[c], axis, j, size)
        for k, (ox, oy) in enumerate(others):
            _remote(mine, mine, send_sems, recv_sems, s0 + k, (ox, oy, c)).start()

    def finish(ins, outs, send_sems, recv_sems, s0):
        x, y, c, j, others = _place()
        sib = (x, y, 1 - c)
        passed = []
        for k, ((ox, oy), landed) in enumerate(zip(others, blocks(outs[0], others, c))):
            _remote(landed, landed, send_sems, recv_sems, s0 + k, (ox, oy, c)).wait_recv()
            fwd = _remote(landed, landed, send_sems, recv_sems, s0 + 3 + k, sib)
            fwd.start()
            passed.append(fwd)
        for k, got in enumerate(blocks(outs[0], others, 1 - c)):
            _remote(got, got, send_sems, recv_sems, s0 + 3 + k, sib).wait_recv()
        for fwd in passed:
            fwd.wait_send()
        mine = _block(outs[0].at[c], axis, j, size)
        for k, (ox, oy) in enumerate(others):
            _remote(mine, mine, send_sems, recv_sems, s0 + k, (ox, oy, c)).wait_send()

    return Exchange([array], [jax.ShapeDtypeStruct(array.shape, array.dtype)], {0: 0}, 6, begin, finish)


def swap_exchange(grads, geom):
    def pieces(t, g, dst, h):
        axis, size = geom[t]
        if len(g.shape) == 2 and axis == 1:
            rows = g.shape[0] // 2
            return [(g.at[pl.ds(h * rows, rows), :], dst)]
        return [(_half(g, axis, jb, size, h), dst.at[jb]) for jb in range(N_CHIPS)]

    counts = [1 if (len(g.shape) == 3 and a == 1) else N_CHIPS for g, (a, _) in zip(grads, geom)]
    bases = [sum(counts[:t]) for t in range(len(grads))]

    def copies(ins, outs, send_sems, recv_sems, s0):
        x, y, c, _, _ = _place()
        cps = []
        for t in range(len(grads)):
            for q, (src, dst) in enumerate(pieces(t, ins[t].at[0], outs[t], 1 - c)):
                cps.append(_remote(src, dst, send_sems, recv_sems, s0 + bases[t] + q, (x, y, 1 - c)))
        return cps

    def begin(*a):
        for cp in copies(*a):
            cp.start()

    def finish(*a):
        for cp in copies(*a):
            cp.wait()

    out_shapes = []
    for g, (axis, size) in zip(grads, geom):
        shp = g.shape[1:]
        if len(shp) == 2 and axis == 1:
            out_shapes.append(jax.ShapeDtypeStruct((shp[0] // 2, shp[1]), g.dtype))
        else:
            out_shapes.append(jax.ShapeDtypeStruct((N_CHIPS,) + _half_shard_shape(shp, axis, size), g.dtype))
    return Exchange(grads, out_shapes, {}, sum(counts), begin, finish)


def scatter_exchange(parts, geom, shapes):
    def copies(ins, outs, send_sems, recv_sems, s0):
        x, y, c, j, others = _place()
        cps = []
        for t, ((axis, size), shp) in enumerate(zip(geom, shapes)):
            for k, (ox, oy) in enumerate(others):
                jp = 2 * ox + oy
                src = ins[t].at[:, pl.ds(jp * size, size)] if (len(shp) == 2 and axis == 1) else ins[t].at[jp]
                cps.append(_remote(src, outs[t].at[k], send_sems, recv_sems, s0 + 3 * t + k, (ox, oy, c)))
        return cps

    def begin(*a):
        for cp in copies(*a):
            cp.start()

    def finish(*a):
        for cp in copies(*a):
            cp.wait_recv()
        for cp in copies(*a):
            cp.wait_send()

    out_shapes = [jax.ShapeDtypeStruct((3,) + _half_shard_shape(shp, axis, size), p.dtype)
                  for p, (axis, size), shp in zip(parts, geom, shapes)]
    return Exchange(parts, out_shapes, {}, 3 * len(parts), begin, finish)


def share_exchange(grads, which):
    n = len(which)

    def my_half(refs, t, h):
        lyr = refs[which[t][0]].at[which[t][1]]
        if len(lyr.shape) == 3:
            rows = lyr.shape[1] // 2
            return lyr.at[:, pl.ds(h * rows, rows), :]
        rows = lyr.shape[0] // 2
        return lyr.at[pl.ds(h * rows, rows), :]

    def begin(ins, outs, send_sems, recv_sems, s0):
        x, y, c, _, _ = _place()
        for t in range(n):
            mine = my_half(outs, t, c)
            _remote(mine, mine, send_sems, recv_sems, s0 + t, (x, y, 1 - c)).start()

    def finish(ins, outs, send_sems, recv_sems, s0):
        x, y, c, _, _ = _place()
        for t in range(n):
            got = my_half(outs, t, 1 - c)
            _remote(got, got, send_sems, recv_sems, s0 + t, (x, y, 1 - c)).wait_recv()
        for t in range(n):
            mine = my_half(outs, t, c)
            _remote(mine, mine, send_sems, recv_sems, s0 + t, (x, y, 1 - c)).wait_send()

    return Exchange(grads, [jax.ShapeDtypeStruct(g.shape, g.dtype) for g in grads],
                    {t: t for t in range(len(grads))}, n, begin, finish)


def all_reduce_small(s):
    rows = s.shape[0]
    half = rows // 2
    assert half % 8 == 0

    def body(s_ref, o_ref, a_ref, b_ref, p_ref, send_sems, recv_sems):
        x, y, c, j, others = _place()
        sib = (x, y, 1 - c)
        swap = _remote(s_ref, a_ref, send_sems, recv_sems, 0, sib)
        swap.start()
        swap.wait()
        p_ref[...] = s_ref[...] + a_ref[...]
        mine = pl.ds(pl.multiple_of(c * half, 8), half)
        b_ref[j] = p_ref[mine, :]
        cps = [_remote(p_ref.at[mine, :], b_ref.at[j], send_sems, recv_sems, 1 + k, (ox, oy, c))
               for k, (ox, oy) in enumerate(others)]
        for cp in cps:
            cp.start()
        for k, (ox, oy) in enumerate(others):
            slot = b_ref.at[2 * ox + oy]
            _remote(slot, slot, send_sems, recv_sems, 1 + k, (ox, oy, c)).wait_recv()
        for cp in cps:
            cp.wait_send()
        o_ref[mine, :] = ((b_ref[0] + b_ref[1]) + b_ref[2]) + b_ref[3]
        back = _remote(o_ref.at[mine, :], o_ref.at[mine, :], send_sems, recv_sems, 4, sib)
        back.start()
        back.wait_send()
        theirs = pl.ds(pl.multiple_of((1 - c) * half, 8), half)
        _remote(o_ref.at[theirs, :], o_ref.at[theirs, :], send_sems, recv_sems, 4, sib).wait_recv()

    vmem = pl.BlockSpec(memory_space=pltpu.VMEM)
    return pl.pallas_call(
        body, name="all_reduce_small", in_specs=[vmem], out_specs=vmem,
        out_shape=jax.ShapeDtypeStruct((rows, LANES), F32),
        scratch_shapes=[pltpu.VMEM((rows, LANES), F32), pltpu.VMEM((N_CHIPS, half, LANES), F32),
                        pltpu.VMEM((rows, LANES), F32), pltpu.SemaphoreType.DMA((5,)),
                        pltpu.SemaphoreType.DMA((5,))],
        compiler_params=pltpu.CompilerParams(vmem_limit_bytes=VMEM_LIMIT, has_side_effects=True))(s)


def pair_sum(g, got, axis, size, place, *, name):
    shp = g.shape[1:]
    if len(shp) == 3:
        hs = size // 2
        grid = (N_CHIPS,)
        g_spec = pl.BlockSpec((None, shp[0], hs, shp[2]), lambda jb, pr: (0, 0, 2 * jb + pr[1], 0))
        r_spec = pl.BlockSpec((None, shp[0], hs, shp[2]), lambda jb, pr: (jb, 0, 0, 0))
    elif axis == 0:
        hs = size // 2
        tr = _row_tile(hs, shp[1], 16)
        steps = hs // tr
        grid = (N_CHIPS, steps)
        g_spec = pl.BlockSpec((None, tr, shp[1]), lambda jb, i, pr: (0, (2 * jb + pr[1]) * steps + i, 0))
        r_spec = pl.BlockSpec((None, tr, shp[1]), lambda jb, i, pr: (jb, i, 0))
    else:
        rows = shp[0] // 2
        tr = _row_tile(rows, shp[1], 16)
        steps = rows // tr
        grid = (steps,)
        g_spec = pl.BlockSpec((None, tr, shp[1]), lambda i, pr: (0, pr[1] * steps + i, 0))
        r_spec = pl.BlockSpec((tr, shp[1]), lambda i, pr: (i, 0))

    def body(pr_ref, g_ref, r_ref, o_ref):
        del pr_ref
        o_ref[...] = (g_ref[...].astype(F32) + r_ref[...].astype(F32)).astype(BF16)

    return pl.pallas_call(
        body, name=name,
        grid_spec=pltpu.PrefetchScalarGridSpec(num_scalar_prefetch=1, grid=grid, in_specs=[g_spec, r_spec],
                                               out_specs=r_spec),
        out_shape=jax.ShapeDtypeStruct(got.shape, BF16), compiler_params=_params(len(grid)))(place, g, got)


def chip_sum(part, slots, shp, axis, size, l, place, out, *, name):
    shard = _shard_shape(shp, axis, size)
    hshape = slots.shape[1:]
    if len(shp) == 3:
        grid = (1,)
        p_spec = pl.BlockSpec((None,) + hshape, lambda i, pr: (pr[0], 0, 0, 0))
        s_specs = [pl.BlockSpec((None,) + hshape, functools.partial(lambda i, pr, k: (k, 0, 0, 0), k=k))
                   for k in range(3)]
        o_spec = pl.BlockSpec((None,) + hshape, lambda i, pr: (l, 0, pr[1], 0))
    else:
        tr = _row_tile(hshape[0], hshape[1], 16)
        steps = hshape[0] // tr
        grid = (steps,)
        if axis == 0:
            p_spec = pl.BlockSpec((None, tr, hshape[1]), lambda i, pr: (pr[0], i, 0))
        else:
            p_spec = pl.BlockSpec((tr, hshape[1]), lambda i, pr: (i, pr[0]))
        s_specs = [pl.BlockSpec((None, tr, hshape[1]), functools.partial(lambda i, pr, k: (k, i, 0), k=k))
                   for k in range(3)]
        o_spec = pl.BlockSpec((None, tr, hshape[1]), lambda i, pr: (l, pr[1] * steps + i, 0))
    has_out = out is not None

    def body(pr_ref, p_ref, s0_ref, s1_ref, s2_ref, *rest):
        del pr_ref
        rest[-1][...] = ((p_ref[...].astype(F32) + s0_ref[...].astype(F32)) + s1_ref[...].astype(F32)) \
            + s2_ref[...].astype(F32)

    return pl.pallas_call(
        body, name=name,
        grid_spec=pltpu.PrefetchScalarGridSpec(
            num_scalar_prefetch=1, grid=grid, in_specs=[p_spec] + s_specs + ([ANY] if has_out else []),
            out_specs=o_spec),
        out_shape=jax.ShapeDtypeStruct((2,) + shard, F32), input_output_aliases={5: 0} if has_out else {},
        compiler_params=_params(1))(place, part, slots, slots, slots, *([out] if has_out else []))


GEOM = {name: (axis, size) for (name, _, axis, size) in BIG}
SHAPE = {name: shape for (name, shape, _, _) in BIG}
RIDES_IN_PROJ_L0 = ((0, ("w_pool", "w_branch_a", "w_branch_b", "w_out", "w_up")),)
RIDES_UP_PROJ_L0 = ((0, ("w_down", "w_ple_gate", "w_ple")), (1, ("w_in",)))
RIDES_DOWN_PROJ_L0 = ((1, ("w_pool", "w_branch_a", "w_branch_b", "w_out")),)
RIDES_IN_PROJ_L1 = ((1, ("w_up", "w_down", "w_ple_gate", "w_ple")),)
EARLY_GRADS_L0 = ("w_ple", "w_ple_gate", "w_down", "w_up")
LATE_GRADS_L0 = ("w_out", "w_branch_a", "w_branch_b", "w_pool", "w_in")


def _swap_of(G, names):
    return swap_exchange([G[k] for k in names], [GEOM[k] for k in names])


def _after_swap(G, names, got, place, tag):
    parts = [pair_sum(G[k], r, *GEOM[k], place, name=f"pair_sum_{k}_{tag}") for k, r in zip(names, got)]
    return scatter_exchange(parts, [GEOM[k] for k in names], [SHAPE[k] for k in names]), parts


def _reduce_start(G, names, place, tag):
    got = run_exchanges([_swap_of(G, names)], name=f"swap_halves_{tag}")
    return _after_swap(G, names, got, place, tag)


def _reduce_end(names, parts, slots, place, l, reduced):
    for k, q, s in zip(names, parts, slots):
        reduced[k] = chip_sum(q, s, SHAPE[k], *GEOM[k], l, place, reduced.get(k), name=f"chip_sum_{k}_l{l}")


def _local_step(x, p2, tgt, W0, W1, conv_w, small, place):
    T = x.shape[0]
    as3 = lambda a: a.reshape(2, 1, a.shape[-1])
    mix3, scale3, sgu3 = as3(small["mix_norm"]), as3(small["pool_scale"]), as3(small["sgu_norm"])
    ffn3, ple3, convb3 = as3(small["ffn_norm"]), as3(small["ple_norm"]), as3(small["conv_b"])
    tril = jnp.tril(jnp.ones((CHUNK, CHUNK), F32))
    ws_masked = small["w_spatial"] * tril
    wsm = ws_masked.astype(BF16)
    wsmT = jnp.swapaxes(ws_masked, -1, -2).astype(BF16)
    bT = jnp.swapaxes(small["b_spatial"], -1, -2)
    final3 = small["final_norm"].reshape(1, D)
    W = [dict(W0), dict(W1)]

    def riders(groups):
        return [gather_exchange([W[lyr][k] for k in names], [GEOM[k] for k in names]) for lyr, names in groups]

    def landed(groups, got):
        for lyr, names in groups:
            W[lyr].update(zip(names, got[:len(names)]))
            got = got[len(names):]

    saved = []
    hb = norm_fwd(x, mix3, 0, name="mix_norm_fwd_l0")
    for l in range(2):
        n = lambda s: f"{s}_l{l}"
        Wl = W[l]
        groups = RIDES_IN_PROJ_L0 if l == 0 else RIDES_IN_PROJ_L1
        z, got = mm_nn(hb, Wl["w_in"], 0, name=n("in_proj"), rows=T, tn=1280, out_dtype=BF16, host=riders(groups))
        landed(groups, got)
        a_in = pool_fwd(z, Wl["w_pool"], scale3, l, name=n("pool_fwd"))
        s_in = sgu_fwd(z, sgu3, wsm, bT, l, name=n("sgu_fwd"))
        yab = mm_nn(a_in, Wl["w_branch_a"], 0, name=n("branch_a"), rows=T, out_cols=2 * D, out_dtype=BF16)
        yab = mm_nn(s_in, Wl["w_branch_b"], 0, name=n("branch_b"), rows=T, out=yab, out_cols=2 * D, out_col_off=D,
                    out_dtype=BF16)
        mo = gate_fwd(z, yab, name=n("gate_fwd"))
        x1, h2b = mm_nn(mo, Wl["w_out"], 0, name=n("out_proj"), rows=T, resid=x, norm_gain=ffn3[l:l + 1])
        if l == 0:
            up, got = mm_nn(h2b, Wl["w_up"], 0, name=n("up_proj"), rows=T, tn=DFF, host=riders(RIDES_UP_PROJ_L0))
            landed(RIDES_UP_PROJ_L0, got)
        else:
            up = mm_nn(h2b, Wl["w_up"], 0, name=n("up_proj"), rows=T, tn=DFF)
        f = conv_fwd(up, conv_w, convb3, l, name=n("conv_fwd"))
        if l == 0:
            (x2, h3b), got = mm_nn(f, Wl["w_down"], 0, name=n("down_proj"), rows=T, resid=x1,
                                   norm_gain=ple3[l:l + 1], host=riders(RIDES_DOWN_PROJ_L0))
            landed(RIDES_DOWN_PROJ_L0, got)
        else:
            x2, h3b = mm_nn(f, Wl["w_down"], 0, name=n("down_proj"), rows=T, resid=x1, norm_gain=ple3[l:l + 1])
        pg = mm_nn(h3b, Wl["w_ple_gate"], 0, name=n("ple_gate_proj"), rows=T, out_dtype=BF16)
        e = mm_nn(p2, Wl["w_ple"], 0, name=n("ple_proj"), rows=T, a_row_off=l * T, out_dtype=BF16)
        saved.append(dict(x=x, hb=hb, z=z, a_in=a_in, s_in=s_in, yab=yab, mo=mo, x1=x1, h2b=h2b, up=up, f=f,
                          x2=x2, h3b=h3b, pg=pg, e=e))
        if l == 0:
            x, hb = ple_fwd(x2, pg, e, mix3, 1, name=n("ple_fwd"))
        else:
            x = ple_fwd(x2, pg, e, None, 0, name=n("ple_fwd"))

    loss_acc, dx, dg_final = loss_head(x, final3, tgt, name="loss_head")

    small_grads = [None, None]
    all_names = [t[0] for t in BIG]
    reduced = {}
    swap1 = G1 = scatter1 = parts1 = slots1 = None
    for l in (1, 0):
        n = lambda s: f"{s}_l{l}"
        a, Wl, G = saved[l], W[l], {}
        de, dpg = ple_bwd(dx, a["pg"], a["e"], name=n("ple_bwd"))
        G["w_ple"] = mm_tn(p2, de, name=n("d_w_ple"), rows=T, ka=PG, nb=D, a_row_off=l * T)
        G["w_ple_gate"] = mm_tn(a["h3b"], dpg, name=n("d_w_ple_gate"), rows=T, ka=D, nb=D)
        if l == 0:
            (dx2, dg_ple), got = mm_nt_norm_bwd(dpg, Wl["w_ple_gate"], 0, a["x2"], ple3, l, dx,
                                                name=n("ple_norm_bwd"), rows=T, tk=256, host=[swap1])
            scatter1, parts1 = _after_swap(G1, all_names, got, place, "l1")
        else:
            dx2, dg_ple = mm_nt_norm_bwd(dpg, Wl["w_ple_gate"], 0, a["x2"], ple3, l, dx,
                                         name=n("ple_norm_bwd"), rows=T, tk=256)
        df = mm_nt(dx2, Wl["w_down"], 0, name=n("d_ffn_act"), rows=T, out_dtype=F32)
        G["w_down"] = mm_tn(a["f"], dx2, name=n("d_w_down"), rows=T, ka=DFF, nb=D, tm=1408)
        if l == 0:
            dup, dcw, dcb, slots1 = conv_bwd(df, a["up"], conv_w, convb3, l, name=n("conv_bwd"), host=[scatter1])
        else:
            dup, dcw, dcb, _ = conv_bwd(df, a["up"], conv_w, convb3, l, name=n("conv_bwd"))
        G["w_up"] = mm_tn(a["h2b"], dup, name=n("d_w_up"), rows=T, ka=D, nb=2 * DFF, tn=DFF, tk=1024)
        if l == 0:
            scatter_early, parts_early = _reduce_start(G, EARLY_GRADS_L0, place, "l0_early")
            (dx1, dg_ffn), slots_early = mm_nt_norm_bwd(dup, Wl["w_up"], 0, a["x1"], ffn3, l, dx2,
                                                        name=n("ffn_norm_bwd"), rows=T, tk=1408, host=[scatter_early])
        else:
            dx1, dg_ffn = mm_nt_norm_bwd(dup, Wl["w_up"], 0, a["x1"], ffn3, l, dx2, name=n("ffn_norm_bwd"), rows=T,
                                         tk=1408)
        dmo = mm_nt(dx1, Wl["w_out"], 0, name=n("d_gated"), rows=T)
        G["w_out"] = mm_tn(a["mo"], dx1, name=n("d_w_out"), rows=T, ka=D, nb=D)
        dz, dyab = gate_bwd(dmo, a["z"], a["yab"], name=n("gate_bwd"))
        G["w_branch_a"] = mm_tn(a["a_in"], dyab, name=n("d_w_branch_a"), rows=T, ka=D, nb=D)
        G["w_branch_b"] = mm_tn(a["s_in"], dyab, name=n("d_w_branch_b"), rows=T, ka=D, nb=D, b_col_off=D)
        da = mm_nt(dyab, Wl["w_branch_a"], 0, name=n("d_pool_out"), rows=T, kdim=D)
        ds = mm_nt(dyab, Wl["w_branch_b"], 0, name=n("d_sgu_out"), rows=T, kdim=D, a_col_off=D)
        dz, dwp, dsc, dws, dbt, dgs = mixer_bwd(da, ds, a["z"], dz, Wl["w_pool"], scale3, sgu3, wsm, wsmT, bT, l,
                                                name=n("mixer_bwd"))
        G["w_pool"] = dwp.astype(BF16)[None]
        G["w_in"] = mm_tn(a["hb"], dz, name=n("d_w_in"), rows=T, ka=D, nb=5 * D, tn=1280)
        if l == 0:
            scatter_late, parts_late = _reduce_start(G, LATE_GRADS_L0, place, "l0_late")
            _reduce_end(all_names, parts1, slots1, place, 1, reduced)
            _reduce_end(EARLY_GRADS_L0, parts_early, slots_early, place, 0, reduced)
            done = [(t, 1) for t in range(len(all_names))] + [(all_names.index(k), 0) for k in EARLY_GRADS_L0]
            (dx, dg_mix), got = mm_nt_norm_bwd(
                dz, Wl["w_in"], 0, a["x"], mix3, l, dx1, name=n("mix_norm_bwd"), rows=T, tk=1280,
                host=[scatter_late, share_exchange([reduced[k] for k in all_names], done)])
            slots_late = got[:len(LATE_GRADS_L0)]
            reduced.update(zip(all_names, got[len(LATE_GRADS_L0):]))
        else:
            dx, dg_mix = mm_nt_norm_bwd(dz, Wl["w_in"], 0, a["x"], mix3, l, dx1, name=n("mix_norm_bwd"), rows=T,
                                        tk=1280)
            swap1, G1 = _swap_of(G, all_names), G
        small_grads[l] = dict(
            mix_norm=dg_mix[0], pool_scale=dsc[0], sgu_norm=dgs[0], w_spatial=dws, b_spatial=dbt.T,
            ffn_norm=dg_ffn[0], conv_b=jnp.concatenate([dcb[0, 0], dcb[1, 0]]), ple_norm=dg_ple[0],
            conv_w=jnp.concatenate([dcw[0, :3], dcw[1, :3]], axis=1))
    _reduce_end(LATE_GRADS_L0, parts_late, slots_late, place, 0, reduced)
    return loss_acc, dx, reduced, small_grads, dg_final[0]


SMALL_ORDER = ("mix_norm", "pool_scale", "sgu_norm", "w_spatial", "b_spatial", "ffn_norm", "conv_b", "ple_norm",
               "conv_w")


def _pack_rows(pieces, row_multiple):
    flat = jnp.concatenate([a.reshape(-1) for a in pieces])
    rows = -(-flat.shape[0] // LANES)
    rows = -(-rows // row_multiple) * row_multiple
    return jnp.pad(flat, (0, rows * LANES - flat.shape[0])).reshape(rows, LANES)


def _unpack(flat, shapes):
    out, off = [], 0
    for shp in shapes:
        size = 1
        for s in shp:
            size *= s
        out.append(flat[off:off + size].reshape(shp))
        off += size
    return out


def kernel(x, p, mix_norm, w_in, w_pool, pool_scale, sgu_norm, w_spatial, b_spatial, w_branch_a, w_branch_b, w_out, ffn_norm, w_up, conv_w, conv_b, w_down, ple_norm, w_ple_gate, w_ple, final_norm, loss_target, m_mix_norm, m_w_in, m_w_pool, m_pool_scale, m_sgu_norm, m_w_spatial, m_b_spatial, m_w_branch_a, m_w_branch_b, m_w_out, m_ffn_norm, m_w_up, m_conv_w, m_conv_b, m_w_down, m_ple_norm, m_w_ple_gate, m_w_ple, m_final_norm, v_mix_norm, v_w_in, v_w_pool, v_pool_scale, v_sgu_norm, v_w_spatial, v_b_spatial, v_w_branch_a, v_w_branch_b, v_w_out, v_ffn_norm, v_w_up, v_conv_w, v_conv_b, v_w_down, v_ple_norm, v_w_ple_gate, v_w_ple, v_final_norm):
    names = ["mix_norm", "w_in", "w_pool", "pool_scale", "sgu_norm", "w_spatial", "b_spatial", "w_branch_a",
             "w_branch_b", "w_out", "ffn_norm", "w_up", "conv_w", "conv_b", "w_down", "ple_norm", "w_ple_gate",
             "w_ple", "final_norm"]
    w = dict(zip(names, [mix_norm, w_in, w_pool, pool_scale, sgu_norm, w_spatial, b_spatial, w_branch_a, w_branch_b,
                         w_out, ffn_norm, w_up, conv_w, conv_b, w_down, ple_norm, w_ple_gate, w_ple, final_norm]))
    m = dict(zip(names, [m_mix_norm, m_w_in, m_w_pool, m_pool_scale, m_sgu_norm, m_w_spatial, m_b_spatial,
                         m_w_branch_a, m_w_branch_b, m_w_out, m_ffn_norm, m_w_up, m_conv_w, m_conv_b, m_w_down,
                         m_ple_norm, m_w_ple_gate, m_w_ple, m_final_norm]))
    v = dict(zip(names, [v_mix_norm, v_w_in, v_w_pool, v_pool_scale, v_sgu_norm, v_w_spatial, v_b_spatial,
                         v_w_branch_a, v_w_branch_b, v_w_out, v_ffn_norm, v_w_up, v_conv_w, v_conv_b, v_w_down,
                         v_ple_norm, v_w_ple_gate, v_w_ple, v_final_norm]))
    T = x.shape[1]
    chip = 2 * lax.axis_index("x") + lax.axis_index("y")
    place = jnp.stack([chip, lax.axis_index("c")]).astype(jnp.int32)

    big_names = [t[0] for t in BIG]
    placed = [{k: place_shard(w[k], l, *GEOM[k], BF16, place, name=f"place_{k}_l{l}") for k in big_names}
              for l in range(2)]
    conv_w8 = jnp.pad(conv_w, ((0, 0), (0, CONV_ROWS - conv_w.shape[1]), (0, 0)))
    conv_placed = place_both_layers(conv_w8, 1, conv_w.shape[2], place, name="place_conv_w")
    w_in0, conv_w_all = run_exchanges([gather_exchange([placed[0]["w_in"]], [GEOM["w_in"]]),
                                       gather_by_layer_exchange(conv_placed, 1, conv_w.shape[2])],
                                      name="gather_first_weights")
    placed[0]["w_in"] = w_in0

    small = {k: w[k] for k in ("mix_norm", "pool_scale", "sgu_norm", "w_spatial", "b_spatial", "ffn_norm",
                               "conv_b", "ple_norm", "final_norm")}
    loss_acc, dx, reduced, small_grads, dg_final = _local_step(
        x.reshape(T, D), p.reshape(2 * T, p.shape[-1]), loss_target.reshape(T, D), placed[0], placed[1], conv_w_all,
        small, place)
    loss = lax.psum(loss_acc[0, 0], ("x", "y", "c"))
    full = run_exchanges([share_exchange([reduced[k] for k in big_names],
                                         [(big_names.index(k), 0) for k in LATE_GRADS_L0])], name="share_last_halves")
    grads = dict(zip(big_names, full))

    pieces = [small_grads[l][k] for l in range(2) for k in SMALL_ORDER] + [dg_final]
    shapes = [a.shape for a in pieces]
    total = all_reduce_small(_pack_rows(pieces, 16)).reshape(-1)
    summed = _unpack(total, shapes)
    per_layer = {k: jnp.stack([summed[i], summed[len(SMALL_ORDER) + i]]) for i, k in enumerate(SMALL_ORDER)}
    for k in ("mix_norm", "pool_scale", "sgu_norm", "w_spatial", "b_spatial", "ffn_norm", "conv_b", "ple_norm"):
        grads[k] = per_layer[k]
    grads["final_norm"] = summed[-1]
    cw = conv_w.shape[2]
    grads["conv_w"] = lax.dynamic_slice_in_dim(per_layer["conv_w"], chip * cw, cw, axis=2)

    delta, new_m, new_v = {}, {}, {}
    for name in big_names:
        shp = w[name].shape
        d_, m_, v_, g_ = elementwise(lambda w_, g_, m_, v_: (*_adamw(w_, g_, m_, v_), g_),
                                     [_view2d(a) for a in (w[name], grads[name], m[name], v[name])],
                                     [F32, F32, F32, F32], name=f"adamw_{name}")
        delta[name], new_m[name], new_v[name] = d_.reshape(shp), m_.reshape(shp), v_.reshape(shp)
        grads[name] = g_.reshape(shp)
    small_names = [k for k in names if k not in big_names]
    small_shapes = [w[k].shape for k in small_names]
    packed = [_pack_rows([src[k] for k in small_names], 8) for src in (w, grads, m, v)]
    outs = elementwise(_adamw, packed, [F32, F32, F32], name="adamw_small")
    for dst, o in zip((delta, new_m, new_v), outs):
        for k, a in zip(small_names, _unpack(o.reshape(-1), small_shapes)):
            dst[k] = a

    return (loss, dx.reshape(1, T, D), *[grads[k] for k in names], *[delta[k] for k in names],
            *[new_m[k] for k in names], *[new_v[k] for k in names])
```

```python
import functools

import jax
import jax.numpy as jnp
from jax import lax
from jax.experimental import pallas as pl
from jax.experimental.pallas import tpu as pltpu

F32 = jnp.float32
BF16 = jnp.bfloat16
EPS = 1e-6
D = 1024
POOL_WINDOWS = (2, 4, 8, 16)
PG = 256
POOL_HALO = 16
CHUNK = 128
HEADS = 8
DFF = 2816
CONV_HALO = 8
CONV_TC = 1408
N_CHIPS = 4
LANES = 128
VMEM_LIMIT = 56 * 1024 * 1024
MESH = pl.DeviceIdType.MESH
ANY = pl.BlockSpec(memory_space=pl.ANY)

ADAM_LR = 0.001
ADAM_B1 = 0.9
ADAM_B2 = 0.999
ADAM_EPS = 1e-08
ADAM_WD = 0.01
ADAM_STEP = 10

BIG = (
    ("w_in", (D, 5 * D), 1, 5 * D // N_CHIPS),
    ("w_pool", (4, PG, PG), 1, PG // N_CHIPS),
    ("w_branch_a", (D, D), 0, D // N_CHIPS),
    ("w_branch_b", (D, D), 0, D // N_CHIPS),
    ("w_out", (D, D), 0, D // N_CHIPS),
    ("w_up", (D, 2 * DFF), 1, 2 * DFF // N_CHIPS),
    ("w_down", (DFF, D), 0, DFF // N_CHIPS),
    ("w_ple_gate", (D, D), 0, D // N_CHIPS),
    ("w_ple", (PG, D), 1, D // N_CHIPS),
)
CONV_ROWS = 8


def _params(n_axes):
    return pltpu.CompilerParams(dimension_semantics=("arbitrary",) * n_axes, vmem_limit_bytes=VMEM_LIMIT)


def _gelu(x):
    return 0.5 * x * (1.0 + lax.erf(x * 0.7071067811865476))


def _gelu_grad(x):
    return 0.5 * (1.0 + lax.erf(x * 0.7071067811865476)) + x * jnp.exp(-0.5 * x * x) * 0.3989422804014327


def _shard_shape(shape, axis, size):
    return tuple(size if a == axis else s for a, s in enumerate(shape))


def _block(ref, axis, j, size):
    idx = tuple(pl.ds(j * size, size) if a == axis else slice(None) for a in range(len(ref.shape)))
    return ref.at[idx]


def mm_nn(a, w, l, *, name, rows, out_dtype=F32, resid=None, a_row_off=0, out=None, out_cols=None,
          out_col_off=0, norm_gain=None, host=None, tm=1024, tn=None, tk=None):
    K, N = w.shape[1], w.shape[2]
    tn = tn or N
    tk = tk or K
    nk = K // tk
    out_cols = out_cols or N
    assert rows % tm == 0 and N % tn == 0 and K % tk == 0 and out_col_off % tn == 0 and a_row_off % tm == 0
    has_resid, has_out, has_norm = resid is not None, out is not None, norm_gain is not None
    assert not has_norm or (tn == N and not has_out)
    grid = (N // tn, rows // tm, nk)
    hosting = _Hosting(host)
    n_in = 2 + has_resid + has_norm + has_out
    n_host_in, n_host_out = len(hosting.arrays), len(hosting.out_shapes)
    n_own_out = 1 + has_norm

    def body(*refs):
        refs = list(refs)
        a_ref, w_ref = refs[0], refs[1]
        r_ref = refs[2] if has_resid else None
        g_ref = refs[2 + has_resid] if has_norm else None
        host_in = refs[n_in:n_in + n_host_in]
        o_base = n_in + n_host_in
        o_ref = refs[o_base]
        host_out = refs[o_base + n_own_out:o_base + n_own_out + n_host_out]
        scratch = refs[o_base + n_own_out + n_host_out:]
        if hosting.plan:
            first, last = _first_last(grid)
            sems = scratch[-2:]
            steps = grid[0] * grid[1] * grid[2]
            step = (pl.program_id(0) * grid[1] + pl.program_id(1)) * grid[2] + pl.program_id(2)

            @pl.when(first)
            def _():
                hosting.begin(host_in, host_out, *sems)

            @pl.when(step == max(steps - 1 - max(steps // 8, 1), 0))
            def _():
                hosting.late(host_in, host_out, *sems)

        part = jnp.dot(a_ref[...].astype(BF16), w_ref[...], preferred_element_type=F32)

        def finish(r):
            if has_resid:
                r = r + r_ref[...]
            o_ref[...] = r.astype(o_ref.dtype)
            if has_norm:
                scale = lax.rsqrt(jnp.mean(r * r, axis=-1, keepdims=True) + EPS)
                refs[o_base + 1][...] = (r * scale * g_ref[...]).astype(BF16)

        if nk == 1:
            finish(part)
        else:
            acc = scratch[0]
            k = pl.program_id(2)

            @pl.when(k == 0)
            def _():
                acc[...] = part

            @pl.when(k > 0)
            def _():
                acc[...] += part

            @pl.when(k == nk - 1)
            def _():
                finish(acc[...])

        if hosting.plan:
            @pl.when(last)
            def _():
                hosting.finish(host_in, host_out, *sems)

    in_specs = [pl.BlockSpec((tm, tk), lambda j, i, k: (i + a_row_off // tm, k)),
                pl.BlockSpec((None, tk, tn), lambda j, i, k: (l, k, j))]
    args = [a, w]
    if has_resid:
        in_specs.append(pl.BlockSpec((tm, tn), lambda j, i, k: (i, j)))
        args.append(resid)
    if has_norm:
        in_specs.append(pl.BlockSpec((None, 1, tn), lambda j, i, k: (l, 0, 0)))
        args.append(norm_gain)
    aliases = {}
    if has_out:
        in_specs.append(ANY)
        aliases = {len(args): 0}
        args.append(out)
    aliases.update(hosting.aliases(n_in, n_own_out))
    out_specs = [pl.BlockSpec((tm, tn), lambda j, i, k: (i, j + out_col_off // tn))]
    out_shape = [jax.ShapeDtypeStruct((rows, out_cols), out_dtype)]
    if has_norm:
        out_specs.append(pl.BlockSpec((tm, tn), lambda j, i, k: (i, j)))
        out_shape.append(jax.ShapeDtypeStruct((rows, N), BF16))
    res = pl.pallas_call(
        body, name=name, grid=grid,
        in_specs=in_specs + [ANY] * n_host_in,
        out_specs=out_specs + [ANY] * n_host_out,
        out_shape=out_shape + hosting.out_shapes,
        scratch_shapes=([pltpu.VMEM((tm, tn), F32)] if nk > 1 else []) + hosting.scratch(),
        input_output_aliases=aliases, compiler_params=_params(3))(*args, *hosting.arrays)
    own = res[0] if n_own_out == 1 else tuple(res[:n_own_out])
    return (own, list(res[n_own_out:])) if hosting.plan else own


def mm_nt(a, w, l, *, name, rows, kdim=None, a_col_off=0, out_dtype=BF16, norm_bwd_of=None, host=None, tm=1024,
          tn=None, tk=None):
    R = w.shape[1]
    kdim = kdim or w.shape[2]
    tn = tn or R
    tk = tk or kdim
    nk = kdim // tk
    assert rows % tm == 0 and R % tn == 0 and kdim % tk == 0 and a_col_off % tk == 0
    fused = norm_bwd_of is not None
    assert not fused or tn == R
    grid = (R // tn, rows // tm, nk)
    hosting = _Hosting(host)
    n_host_in, n_host_out = len(hosting.arrays), len(hosting.out_shapes)
    n_own_in, n_own_out = (3, 2) if fused else (0, 1)

    def body(a_ref, w_ref, *refs):
        host_in = refs[n_own_in:n_own_in + n_host_in]
        host_out = refs[n_own_in + n_host_in + n_own_out:n_own_in + n_host_in + n_own_out + n_host_out]
        scratch = refs[n_own_in + n_host_in + n_own_out + n_host_out:]
        rest = list(refs[:n_own_in]) + list(refs[n_own_in + n_host_in:n_own_in + n_host_in + n_own_out]) \
            + ([scratch[0]] if nk > 1 else [])
        if hosting.plan:
            first, last = _first_last(grid)
            sems = scratch[-2:]

            @pl.when(first)
            def _():
                hosting.begin(host_in, host_out, *sems)

        part = lax.dot_general(a_ref[...].astype(BF16), w_ref[...], (((1,), (1,)), ((), ())),
                               preferred_element_type=F32)
        i, k = pl.program_id(1), pl.program_id(2)

        def finish(dh):
            if not fused:
                rest[0][...] = dh.astype(rest[0].dtype)
                return
            x_ref, g_ref, dxi_ref, dx_ref, dg_ref = rest[:5]

            @pl.when(i == 0)
            def _():
                dg_ref[...] = jnp.zeros_like(dg_ref)

            xv = x_ref[...]
            r = lax.rsqrt(jnp.mean(xv * xv, axis=-1, keepdims=True) + EPS)
            xh = xv * r
            dhg = dh * g_ref[...]
            dx_ref[...] = dxi_ref[...] + r * (dhg - xh * jnp.mean(dhg * xh, axis=-1, keepdims=True))
            dg_ref[0:1, :] += jnp.sum(dh * xh, axis=0, keepdims=True)

        if nk == 1:
            finish(part)
        else:
            acc = rest[-1]

            @pl.when(k == 0)
            def _():
                acc[...] = part

            @pl.when(k > 0)
            def _():
                acc[...] += part

            @pl.when(k == nk - 1)
            def _():
                finish(acc[...])

        if hosting.plan:
            @pl.when(last)
            def _():
                hosting.late(host_in, host_out, *sems)
                hosting.finish(host_in, host_out, *sems)

    if a.ndim == 3:
        per = a.shape[2] // tk
        a_spec = pl.BlockSpec((None, tm, tk), lambda j, i, k: (k // per, i, k % per))
    else:
        a_spec = pl.BlockSpec((tm, tk), lambda j, i, k: (i, k + a_col_off // tk))
    in_specs = [a_spec, pl.BlockSpec((None, tn, tk), lambda j, i, k: (l, j, k))]
    args = [a, w]
    row_tile = pl.BlockSpec((tm, tn), lambda j, i, k: (i, j))
    if fused:
        x, gain, gl, dx_in = norm_bwd_of
        in_specs += [row_tile, pl.BlockSpec((None, 1, tn), lambda j, i, k: (gl, 0, 0)), row_tile]
        args += [x, gain, dx_in]
        out_specs = [row_tile, pl.BlockSpec((8, tn), lambda j, i, k: (0, 0))]
        out_shape = [jax.ShapeDtypeStruct((rows, R), F32), jax.ShapeDtypeStruct((8, R), F32)]
    else:
        out_specs, out_shape = [row_tile], [jax.ShapeDtypeStruct((rows, R), out_dtype)]
    res = pl.pallas_call(
        body, name=name, grid=grid, in_specs=in_specs + [ANY] * n_host_in,
        out_specs=out_specs + [ANY] * n_host_out, out_shape=out_shape + hosting.out_shapes,
        scratch_shapes=([pltpu.VMEM((tm, tn), F32)] if nk > 1 else []) + hosting.scratch(),
        input_output_aliases=hosting.aliases(2 + n_own_in, n_own_out), compiler_params=_params(3))(
            *args, *hosting.arrays)
    own = tuple(res[:n_own_out]) if fused else res[0]
    return (own, list(res[n_own_out:])) if hosting.plan else own


def mm_tn(a, b, *, name, rows, ka, nb, a_row_off=0, b_col_off=0, tm=None, tn=None, tk=2048):
    tm = tm or ka
    tn = tn or nb
    tk = min(tk, rows)
    nk = rows // tk
    assert ka % tm == 0 and nb % tn == 0 and rows % tk == 0 and b_col_off % tn == 0 and a_row_off % tk == 0

    def body(a_ref, b_ref, o_ref, acc):
        part = lax.dot_general(a_ref[...].astype(BF16), b_ref[...].astype(BF16), (((0,), (0,)), ((), ())),
                               preferred_element_type=F32)
        k = pl.program_id(2)

        @pl.when(k == 0)
        def _():
            acc[...] = part

        @pl.when(k > 0)
        def _():
            acc[...] += part

        @pl.when(k == nk - 1)
        def _():
            o_ref[...] = acc[...].astype(o_ref.dtype)

    if b.ndim == 3:
        per = b.shape[2] // tn
        b_spec = pl.BlockSpec((None, tk, tn), lambda j, i, k: (j // per, k, j % per))
    else:
        b_spec = pl.BlockSpec((tk, tn), lambda j, i, k: (k, j + b_col_off // tn))
    return pl.pallas_call(
        body, name=name, grid=(nb // tn, ka // tm, nk),
        in_specs=[pl.BlockSpec((tk, tm), lambda j, i, k: (k + a_row_off // tk, i)), b_spec],
        out_specs=pl.BlockSpec((None, tm, tn), lambda j, i, k: (0, i, j)),
        out_shape=jax.ShapeDtypeStruct((1, ka, nb), BF16),
        scratch_shapes=[pltpu.VMEM((tm, tn), F32)], compiler_params=_params(3))(a, b)


def _row_spec(tm, width, col=0):
    return pl.BlockSpec((tm, width), lambda i: (i, col))


def _gain_spec(l, width=D):
    return pl.BlockSpec((None, 1, width), lambda i: (l, 0, 0))


def norm_fwd(x, g3, l, *, name, host=None, tm=1024):
    T = x.shape[0]
    nt = T // tm
    hosting = _Hosting(host)
    n_host_in, n_host_out = len(hosting.arrays), len(hosting.out_shapes)

    def body(x_ref, g_ref, *rest):
        host_in, o_ref = rest[:n_host_in], rest[n_host_in]
        host_out, sems = rest[n_host_in + 1:n_host_in + 1 + n_host_out], rest[n_host_in + 1 + n_host_out:]
        i = pl.program_id(0)
        if hosting.plan:
            @pl.when(i == 0)
            def _():
                hosting.begin(host_in, host_out, *sems)

        xv = x_ref[...]
        r = lax.rsqrt(jnp.mean(xv * xv, axis=-1, keepdims=True) + EPS)
        o_ref[...] = (xv * r * g_ref[...]).astype(BF16)
        if hosting.plan:
            @pl.when(i == nt - 1)
            def _():
                hosting.late(host_in, host_out, *sems)
                hosting.finish(host_in, host_out, *sems)

    res = pl.pallas_call(
        body, name=name, grid=(nt,),
        in_specs=[_row_spec(tm, D), _gain_spec(l)] + [ANY] * n_host_in,
        out_specs=[_row_spec(tm, D)] + [ANY] * n_host_out,
        out_shape=[jax.ShapeDtypeStruct((T, D), BF16)] + hosting.out_shapes, scratch_shapes=hosting.scratch(),
        input_output_aliases=hosting.aliases(2, 1), compiler_params=_params(1))(x, g3, *hosting.arrays)
    return (res[0], list(res[1:])) if hosting.plan else res[0]


def _winsum_back(ext, w):
    s, span = ext, 1
    while span < w:
        s = s + pltpu.roll(s, span, 0)
        span *= 2
    return s


def _winsum_fwd(ext, w):
    rows = ext.shape[0]
    s, span = ext, 1
    while span < w:
        s = s + pltpu.roll(s, rows - span, 0)
        span *= 2
    return s


def _pooled(ext, z, t, g, w):
    sl = slice(g * PG, (g + 1) * PG)
    s = _winsum_back(ext[:, sl], w)[POOL_HALO:, :]
    return s / jnp.minimum(t + 1, w).astype(F32) - z[:, sl]


def pool_fwd(z, wpool, scale3, l, *, name, tm=256):
    T = z.shape[0]
    hb = tm // POOL_HALO
    wl = l if wpool.shape[0] > 1 else 0

    def body(z_ref, zp_ref, wp_ref, sc_ref, o_ref):
        i = pl.program_id(0)
        zv = z_ref[...].astype(F32)
        prev = jnp.where(i == 0, 0.0, zp_ref[...].astype(F32))
        ext = jnp.concatenate([prev, zv], axis=0)
        t = i * tm + lax.broadcasted_iota(jnp.int32, (tm, 1), 0)
        for g, w in enumerate(POOL_WINDOWS):
            sl = slice(g * PG, (g + 1) * PG)
            pooled = _pooled(ext, zv, t, g, w)
            q = jnp.dot(pooled.astype(BF16), wp_ref[g], preferred_element_type=F32)
            o_ref[:, sl] = (q * sc_ref[:, sl]).astype(BF16)

    return pl.pallas_call(
        body, name=name, grid=(T // tm,),
        in_specs=[_row_spec(tm, D),
                  pl.BlockSpec((POOL_HALO, D), lambda i: (jnp.maximum(i * hb - 1, 0), 0)),
                  pl.BlockSpec((None, 4, PG, PG), lambda i: (wl, 0, 0, 0)),
                  _gain_spec(l)],
        out_specs=_row_spec(tm, D),
        out_shape=jax.ShapeDtypeStruct((T, D), BF16), compiler_params=_params(1))(z, z, wpool, scale3)


def sgu_fwd(z, g3, wsm, bT, l, *, name, tm=256):
    T = z.shape[0]

    def body(zu_ref, zv_ref, g_ref, ws_ref, b_ref, o_ref):
        gu = _gelu(zu_ref[...].astype(F32))
        gv = _gelu(zv_ref[...].astype(F32))
        rv = lax.rsqrt(jnp.mean(gv * gv, axis=-1, keepdims=True) + EPS)
        vn = (gv * rv * g_ref[...]).astype(BF16)
        for n in range(tm // CHUNK):
            r = slice(n * CHUNK, (n + 1) * CHUNK)
            for h in range(HEADS):
                cs = slice(h * CHUNK, (h + 1) * CHUNK)
                mixed = jnp.dot(ws_ref[h], vn[r, cs], preferred_element_type=F32) + b_ref[:, h:h + 1]
                o_ref[r, cs] = (gu[r, cs] * mixed).astype(BF16)

    return pl.pallas_call(
        body, name=name, grid=(T // tm,),
        in_specs=[_row_spec(tm, D, 1), _row_spec(tm, D, 2), _gain_spec(l),
                  pl.BlockSpec((None, HEADS, CHUNK, CHUNK), lambda i: (l, 0, 0, 0)),
                  pl.BlockSpec((None, CHUNK, HEADS), lambda i: (l, 0, 0))],
        out_specs=_row_spec(tm, D),
        out_shape=jax.ShapeDtypeStruct((T, D), BF16), compiler_params=_params(1))(z, z, g3, wsm, bT)


def gate_fwd(z, yab, *, name, tm=1024):
    T = z.shape[0]

    def body(za_ref, zb_ref, y_ref, o_ref):
        ga = jax.nn.sigmoid(za_ref[...].astype(F32))
        gb = jax.nn.sigmoid(zb_ref[...].astype(F32))
        o_ref[...] = (ga * y_ref[:, :D].astype(F32) + gb * y_ref[:, D:].astype(F32)).astype(BF16)

    return pl.pallas_call(
        body, name=name, grid=(T // tm,),
        in_specs=[_row_spec(tm, D, 3), _row_spec(tm, D, 4), _row_spec(tm, 2 * D)],
        out_specs=_row_spec(tm, D),
        out_shape=jax.ShapeDtypeStruct((T, D), BF16), compiler_params=_params(1))(z, z, yab)


def _conv(ext, w_ref, b_ref):
    down1, down2 = pltpu.roll(ext, 1, 0), pltpu.roll(ext, 2, 0)
    c = b_ref[...] + w_ref[0:1, :] * down2
    c = c + w_ref[1:2, :] * down1
    return c + w_ref[2:3, :] * ext, down1, down2


def conv_fwd(up, convw, convb3, l, *, name, tm=256):
    T = up.shape[0]
    tc = CONV_TC
    nc = DFF // tc
    hb = tm // CONV_HALO

    def body(ua_ref, uap_ref, ub_ref, ubp_ref, wa_ref, wb_ref, ba_ref, bb_ref, o_ref):
        i = pl.program_id(1)

        def conv_of(u_ref, p_ref, w_ref, b_ref):
            ext = jnp.concatenate([jnp.where(i == 0, 0.0, p_ref[...]), u_ref[...]], axis=0)
            return _conv(ext, w_ref, b_ref)[0][CONV_HALO:, :]

        ca = conv_of(ua_ref, uap_ref, wa_ref, ba_ref)
        cb = conv_of(ub_ref, ubp_ref, wb_ref, bb_ref)
        o_ref[...] = (_gelu(ca) * cb).astype(BF16)

    def cur(off):
        return pl.BlockSpec((tm, tc), lambda j, i: (i, j + off))

    def prev(off):
        return pl.BlockSpec((CONV_HALO, tc), lambda j, i: (jnp.maximum(i * hb - 1, 0), j + off))

    def wspec(off):
        return pl.BlockSpec((None, CONV_ROWS, tc), lambda j, i: (l, 0, j + off))

    def bspec(off):
        return pl.BlockSpec((None, 1, tc), lambda j, i: (l, 0, j + off))

    return pl.pallas_call(
        body, name=name, grid=(nc, T // tm),
        in_specs=[cur(0), prev(0), cur(nc), prev(nc), wspec(0), wspec(nc), bspec(0), bspec(nc)],
        out_specs=pl.BlockSpec((tm, tc), lambda j, i: (i, j)),
        out_shape=jax.ShapeDtypeStruct((T, DFF), BF16),
        compiler_params=_params(2))(up, up, up, up, convw, convw, convb3, convb3)


def ple_fwd(x2, pg, e, g3, l, *, name, tm=1024):
    T = x2.shape[0]
    has_norm = g3 is not None

    def body(x_ref, pg_ref, e_ref, *rest):
        xv = x_ref[...] + jax.nn.sigmoid(pg_ref[...].astype(F32)) * e_ref[...].astype(F32)
        if has_norm:
            g_ref, o_ref, h_ref = rest
            r = lax.rsqrt(jnp.mean(xv * xv, axis=-1, keepdims=True) + EPS)
            h_ref[...] = (xv * r * g_ref[...]).astype(BF16)
        else:
            o_ref, = rest
        o_ref[...] = xv

    x_shape = jax.ShapeDtypeStruct((T, D), F32)
    return pl.pallas_call(
        body, name=name, grid=(T // tm,),
        in_specs=[_row_spec(tm, D)] * 3 + ([_gain_spec(l)] if has_norm else []),
        out_specs=[_row_spec(tm, D)] * 2 if has_norm else _row_spec(tm, D),
        out_shape=[x_shape, jax.ShapeDtypeStruct((T, D), BF16)] if has_norm else x_shape,
        compiler_params=_params(1))(x2, pg, e, *([g3] if has_norm else []))


def loss_head(x, g3, tgt, *, name, tm=1024):
    T = x.shape[0]

    def body(x_ref, g_ref, t_ref, loss_ref, dx_ref, dg_ref):
        @pl.when(pl.program_id(0) == 0)
        def _():
            loss_ref[...] = jnp.zeros_like(loss_ref)
            dg_ref[...] = jnp.zeros_like(dg_ref)

        xv, g = x_ref[...], g_ref[...]
        r = lax.rsqrt(jnp.mean(xv * xv, axis=-1, keepdims=True) + EPS)
        xh = xv * r
        err = xh * g - t_ref[...]
        loss_ref[...] += 0.5 * jnp.sum(jnp.mean(err * err, axis=-1, keepdims=True))
        dy = err * (1.0 / D)
        dyg = dy * g
        dx_ref[...] = r * (dyg - xh * jnp.mean(dyg * xh, axis=-1, keepdims=True))
        dg_ref[0:1, :] += jnp.sum(dy * xh, axis=0, keepdims=True)

    return pl.pallas_call(
        body, name=name, grid=(T // tm,),
        in_specs=[_row_spec(tm, D), pl.BlockSpec((1, D), lambda i: (0, 0)), _row_spec(tm, D)],
        out_specs=[pl.BlockSpec((8, LANES), lambda i: (0, 0)), _row_spec(tm, D),
                   pl.BlockSpec((8, D), lambda i: (0, 0))],
        out_shape=[jax.ShapeDtypeStruct((8, LANES), F32), jax.ShapeDtypeStruct((T, D), F32),
                   jax.ShapeDtypeStruct((8, D), F32)],
        compiler_params=_params(1))(x, g3, tgt)


def ple_bwd(dx, pg, e, *, name, tm=1024):
    T = dx.shape[0]

    def body(dx_ref, pg_ref, e_ref, de_ref, dpg_ref):
        gate = jax.nn.sigmoid(pg_ref[...].astype(F32))
        dxv = dx_ref[...]
        de_ref[...] = (dxv * gate).astype(BF16)
        dpg_ref[...] = (dxv * e_ref[...].astype(F32) * gate * (1.0 - gate)).astype(BF16)

    return pl.pallas_call(
        body, name=name, grid=(T // tm,),
        in_specs=[_row_spec(tm, D)] * 3, out_specs=[_row_spec(tm, D)] * 2,
        out_shape=[jax.ShapeDtypeStruct((T, D), BF16)] * 2, compiler_params=_params(1))(dx, pg, e)


def conv_bwd(df, up, convw, convb3, l, *, name, host=None, tm=256):
    T = up.shape[0]
    tc = CONV_TC
    nc = DFF // tc
    hb = tm // CONV_HALO
    nt = T // tm
    rows = tm + 2 * CONV_HALO
    own = slice(CONV_HALO, CONV_HALO + tm)

    hosting = _Hosting(host)
    n_host_in, n_host_out = len(hosting.arrays), len(hosting.out_shapes)

    def body(df_ref, dfn_ref, ua_ref, uap_ref, uan_ref, ub_ref, ubp_ref, ubn_ref, wa_ref, wb_ref, ba_ref, bb_ref,
             *rest):
        host_in = rest[:n_host_in]
        dup_ref, dcw_ref, dcb_ref = rest[n_host_in:n_host_in + 3]
        host_out = rest[n_host_in + 3:n_host_in + 3 + n_host_out]
        sems = rest[n_host_in + 3 + n_host_out:]
        i = pl.program_id(1)
        if hosting.plan:
            first, last = _first_last((nc, nt))

            @pl.when(first)
            def _():
                hosting.begin(host_in, host_out, *sems)

        @pl.when(i == 0)
        def _():
            dcw_ref[...] = jnp.zeros_like(dcw_ref)
            dcb_ref[...] = jnp.zeros_like(dcb_ref)

        def ext_of(c_ref, p_ref, n_ref):
            return jnp.concatenate([jnp.where(i == 0, 0.0, p_ref[...]), c_ref[...],
                                    jnp.where(i == nt - 1, 0.0, n_ref[...])], axis=0)

        ea = ext_of(ua_ref, uap_ref, uan_ref)
        eb = ext_of(ub_ref, ubp_ref, ubn_ref)
        ca, ea1, ea2 = _conv(ea, wa_ref, ba_ref)
        cb, eb1, eb2 = _conv(eb, wb_ref, bb_ref)
        df_ext =jnp.concatenate([jnp.zeros((CONV_HALO, tc), F32), df_ref[...],
                                  jnp.where(i == nt - 1, 0.0, dfn_ref[...])], axis=0)
        cdf = 0.5 * (1.0 + lax.erf(ca * 0.7071067811865476))
        da = df_ext * cb * (cdf + ca * jnp.exp(-0.5 * ca * ca) * 0.3989422804014327)
        db = df_ext * (ca * cdf)

        def finish(h, dc, e, e1, e2, w_ref):
            dup = w_ref[2:3, :] * dc + w_ref[1:2, :] * pltpu.roll(dc, rows - 1, 0)
            dup = dup + w_ref[0:1, :] * pltpu.roll(dc, rows - 2, 0)
            dup_ref[h] = dup[own, :].astype(BF16)
            dco = dc[own, :]
            dcb_ref[h, 0:1, :] += jnp.sum(dco, axis=0, keepdims=True)
            dcw_ref[h, 0:1, :] += jnp.sum(dco * e2[own, :], axis=0, keepdims=True)
            dcw_ref[h, 1:2, :] += jnp.sum(dco * e1[own, :], axis=0, keepdims=True)
            dcw_ref[h, 2:3, :] += jnp.sum(dco * e[own, :], axis=0, keepdims=True)

        finish(0, da, ea, ea1, ea2, wa_ref)
        finish(1, db, eb, eb1, eb2, wb_ref)
        if hosting.plan:
            @pl.when(last)
            def _():
                hosting.late(host_in, host_out, *sems)
                hosting.finish(host_in, host_out, *sems)

    def nxt(i):
        return jnp.minimum((i + 1) * hb, T // CONV_HALO - 1)

    def prv(i):
        return jnp.maximum(i * hb - 1, 0)

    def up_specs(off):
        return [pl.BlockSpec((tm, tc), lambda j, i: (i, j + off)),
                pl.BlockSpec((CONV_HALO, tc), lambda j, i: (prv(i), j + off)),
                pl.BlockSpec((CONV_HALO, tc), lambda j, i: (nxt(i), j + off))]

    in_specs = [pl.BlockSpec((tm, tc), lambda j, i: (i, j)),
                pl.BlockSpec((CONV_HALO, tc), lambda j, i: (nxt(i), j)),
                *up_specs(0), *up_specs(nc),
                pl.BlockSpec((None, CONV_ROWS, tc), lambda j, i: (l, 0, j)),
                pl.BlockSpec((None, CONV_ROWS, tc), lambda j, i: (l, 0, j + nc)),
                pl.BlockSpec((None, 1, tc), lambda j, i: (l, 0, j)),
                pl.BlockSpec((None, 1, tc), lambda j, i: (l, 0, j + nc))]
    res = pl.pallas_call(
        body, name=name, grid=(nc, nt), in_specs=in_specs + [ANY] * n_host_in,
        out_specs=[pl.BlockSpec((2, tm, tc), lambda j, i: (0, i, j)),
                   pl.BlockSpec((2, 8, tc), lambda j, i: (0, 0, j)),
                   pl.BlockSpec((2, 8, tc), lambda j, i: (0, 0, j))] + [ANY] * n_host_out,
        out_shape=[jax.ShapeDtypeStruct((2, T, DFF), BF16), jax.ShapeDtypeStruct((2, 8, DFF), F32),
                   jax.ShapeDtypeStruct((2, 8, DFF), F32)] + hosting.out_shapes,
        scratch_shapes=hosting.scratch(), input_output_aliases=hosting.aliases(12, 3),
        compiler_params=_params(2))(df, df, up, up, up, up, up, up, convw, convw, convb3, convb3, *hosting.arrays)
    return res[0], res[1], res[2], list(res[3:])


def gate_bwd(dmo, z, yab, *, name, tm=1024):
    T = z.shape[0]

    def body(dmo_ref, zg_ref, y_ref, dz_ref, dy_ref):
        g = jax.nn.sigmoid(zg_ref[...].astype(F32))
        dmo_v = dmo_ref[...].astype(F32)
        dy_ref[...] = (dmo_v * g).astype(BF16)
        dz_ref[...] = (dmo_v * y_ref[...].astype(F32) * g * (1.0 - g)).astype(BF16)

    return pl.pallas_call(
        body, name=name, grid=(T // tm, 2),
        in_specs=[pl.BlockSpec((tm, D), lambda i, s: (i, 0)),
                  pl.BlockSpec((tm, D), lambda i, s: (i, 3 + s)),
                  pl.BlockSpec((tm, D), lambda i, s: (i, s))],
        out_specs=[pl.BlockSpec((tm, D), lambda i, s: (i, 3 + s)),
                   pl.BlockSpec((tm, D), lambda i, s: (i, s))],
        out_shape=[jax.ShapeDtypeStruct((T, 5 * D), BF16), jax.ShapeDtypeStruct((T, 2 * D), BF16)],
        compiler_params=_params(2))(dmo, z, yab)


def mixer_bwd(da, ds, z, dz, wpool, scale3, g3, wsm, wsmT, bT, l, *, name, tm=256):
    T = z.shape[0]
    hb = tm // POOL_HALO
    nt = T // tm

    def body(da_ref, dan_ref, ds_ref, zp_ref, zpp_ref, zu_ref, zv_ref, wp_ref, sc_ref, g_ref, ws_ref, wst_ref,
             b_ref, dzin_ref, dz_ref, dwp_ref, dsc_ref, dws_ref, dbt_ref, dgs_ref, mixed_scr, dvn_scr, db_scr):
        del dzin_ref
        i = pl.program_id(0)

        @pl.when(i == 0)
        def _():
            dwp_ref[...] = jnp.zeros_like(dwp_ref)
            dsc_ref[...] = jnp.zeros_like(dsc_ref)
            dws_ref[...] = jnp.zeros_like(dws_ref)
            dgs_ref[...] = jnp.zeros_like(dgs_ref)
            db_scr[...] = jnp.zeros_like(db_scr)

        zv_p = zp_ref[...].astype(F32)
        ext = jnp.concatenate([jnp.where(i == 0, 0.0, zpp_ref[...].astype(F32)), zv_p], axis=0)
        da_v = da_ref[...].astype(F32)
        da_ext = jnp.concatenate([da_v, jnp.where(i == nt - 1, 0.0, dan_ref[...].astype(F32))], axis=0)
        t = i * tm + lax.broadcasted_iota(jnp.int32, (tm, 1), 0)
        t_ext = i * tm + lax.broadcasted_iota(jnp.int32, (tm + POOL_HALO, 1), 0)
        for g, w in enumerate(POOL_WINDOWS):
            sl = slice(g * PG, (g + 1) * PG)
            pooled = _pooled(ext, zv_p, t, g, w).astype(BF16)
            q = jnp.dot(pooled, wp_ref[g], preferred_element_type=F32)
            dsc_ref[0:1, sl] += jnp.sum(da_v[:, sl] * q, axis=0, keepdims=True)
            dq_ext = (da_ext[:, sl] * sc_ref[:, sl]).astype(BF16)
            dwp_ref[g] += lax.dot_general(pooled, dq_ext[:tm, :], (((0,), (0,)), ((), ())),
                                          preferred_element_type=F32)
            dpool = lax.dot_general(dq_ext, wp_ref[g], (((1,), (1,)), ((), ())), preferred_element_type=F32)
            spread = _winsum_fwd(dpool / jnp.minimum(t_ext + 1, w).astype(F32), w)
            dz_ref[:, sl] = (spread[:tm, :] - dpool[:tm, :]).astype(BF16)

        zu, zv, ds_v = zu_ref[...].astype(F32), zv_ref[...].astype(F32), ds_ref[...].astype(F32)
        gain = g_ref[...]
        gu, gv = _gelu(zu), _gelu(zv)
        rv = lax.rsqrt(jnp.mean(gv * gv, axis=-1, keepdims=True) + EPS)
        vh = gv * rv
        vn = (vh * gain).astype(BF16)
        dmix = ds_v * gu
        dmix_b = dmix.astype(BF16)
        for n in range(tm // CHUNK):
            r = slice(n * CHUNK, (n + 1) * CHUNK)
            db_scr[...] += dmix[r, :]
            for h in range(HEADS):
                cs = slice(h * CHUNK, (h + 1) * CHUNK)
                mixed_scr[r, cs] = jnp.dot(ws_ref[h], vn[r, cs], preferred_element_type=F32) + b_ref[:, h:h + 1]
                dws_ref[h] += lax.dot_general(dmix_b[r, cs], vn[r, cs], (((1,), (1,)), ((), ())),
                                              preferred_element_type=F32)
                dvn_scr[r, cs] = jnp.dot(wst_ref[h], dmix_b[r, cs], preferred_element_type=F32)
        dz_ref[:, D:2 * D] = (ds_v * mixed_scr[...] * _gelu_grad(zu)).astype(BF16)
        dvn = dvn_scr[...]
        dgs_ref[0:1, :] += jnp.sum(dvn * vh, axis=0, keepdims=True)
        dvg = dvn * gain
        dgv = rv * (dvg - vh * jnp.mean(dvg * vh, axis=-1, keepdims=True))
        dz_ref[:, 2 * D:3 * D] = (dgv * _gelu_grad(zv)).astype(BF16)

        @pl.when(i == nt - 1)
        def _():
            tril = (lax.broadcasted_iota(jnp.int32, (CHUNK, CHUNK), 0)
                    >= lax.broadcasted_iota(jnp.int32, (CHUNK, CHUNK), 1)).astype(F32)
            for h in range(HEADS):
                dws_ref[h] = dws_ref[h] * tril
                dbt_ref[:, h:h + 1] = jnp.sum(db_scr[:, h * CHUNK:(h + 1) * CHUNK], axis=1, keepdims=True)

    const4 = lambda i: (l, 0, 0, 0)
    wl = l if wpool.shape[0] > 1 else 0
    in_specs = [
        _row_spec(tm, D),
        pl.BlockSpec((POOL_HALO, D), lambda i: (jnp.minimum((i + 1) * hb, T // POOL_HALO - 1), 0)),
        _row_spec(tm, D),
        _row_spec(tm, D, 0),
        pl.BlockSpec((POOL_HALO, D), lambda i: (jnp.maximum(i * hb - 1, 0), 0)),
        _row_spec(tm, D, 1), _row_spec(tm, D, 2),
        pl.BlockSpec((None, 4, PG, PG), lambda i: (wl, 0, 0, 0)),
        _gain_spec(l), _gain_spec(l),
        pl.BlockSpec((None, HEADS, CHUNK, CHUNK), const4),
        pl.BlockSpec((None, HEADS, CHUNK, CHUNK), const4),
        pl.BlockSpec((None, CHUNK, HEADS), lambda i: (l, 0, 0)),
        ANY,
    ]
    out_specs = [
        pl.BlockSpec((tm, 3 * D), lambda i: (i, 0)),
        pl.BlockSpec((4, PG, PG), lambda i: (0, 0, 0)),
        pl.BlockSpec((8, D), lambda i: (0, 0)),
        pl.BlockSpec((HEADS, CHUNK, CHUNK), lambda i: (0, 0, 0)),
        pl.BlockSpec((CHUNK, HEADS), lambda i: (0, 0)),
        pl.BlockSpec((8, D), lambda i: (0, 0)),
    ]
    out_shape = [
        jax.ShapeDtypeStruct((T, 5 * D), BF16), jax.ShapeDtypeStruct((4, PG, PG), F32),
        jax.ShapeDtypeStruct((8, D), F32), jax.ShapeDtypeStruct((HEADS, CHUNK, CHUNK), F32),
        jax.ShapeDtypeStruct((CHUNK, HEADS), F32), jax.ShapeDtypeStruct((8, D), F32),
    ]
    return pl.pallas_call(
        body, name=name, grid=(nt,), in_specs=in_specs, out_specs=out_specs, out_shape=out_shape,
        scratch_shapes=[pltpu.VMEM((tm, D), F32), pltpu.VMEM((tm, D), F32), pltpu.VMEM((CHUNK, D), F32)],
        input_output_aliases={13: 0}, compiler_params=_params(1))(
            da, da, ds, z, z, z, z, wpool, scale3, g3, wsm, wsmT, bT, dz)


def _row_tile(rows, cols, sub):
    cap = max(sub, (2 * 1024 * 1024) // (4 * cols))
    best = None
    for tr in range(sub, min(rows, cap) + 1, sub):
        if rows % tr == 0:
            best = tr
    return best or rows


def elementwise(fn, ins, out_dtypes, *, name, row_blk_offs=None, rows=None):
    cols = ins[0].shape[1]
    rows = rows or ins[0].shape[0]
    tr = _row_tile(rows, cols, 16)
    offs = row_blk_offs or [0] * len(ins)
    n_in = len(ins)

    def body(*refs):
        outs = fn(*[r[...] for r in refs[:n_in]])
        for o_ref, o in zip(refs[n_in:], outs):
            o_ref[...] = o.astype(o_ref.dtype)

    return pl.pallas_call(
        body, name=name, grid=(rows // tr,),
        in_specs=[pl.BlockSpec((tr, cols), functools.partial(lambda i, o: (i + o * (rows // tr), 0), o=o))
                  for o in offs],
        out_specs=[pl.BlockSpec((tr, cols), lambda i: (i, 0)) for _ in out_dtypes],
        out_shape=[jax.ShapeDtypeStruct((rows, cols), dt) for dt in out_dtypes],
        compiler_params=_params(1))(*ins)


def _adamw(w, g, m, v):
    m = ADAM_B1 * m + (1.0 - ADAM_B1) * g
    v = ADAM_B2 * v + (1.0 - ADAM_B2) * jnp.square(g)
    m_hat = m / (1.0 - ADAM_B1 ** ADAM_STEP)
    v_hat = v / (1.0 - ADAM_B2 ** ADAM_STEP)
    delta = -ADAM_LR * (m_hat / (jnp.sqrt(v_hat) + ADAM_EPS) + ADAM_WD * w)
    return delta, m, v


def _view2d(a):
    return a.reshape(-1, a.shape[-1])


def _place():
    x, y, c = lax.axis_index("x"), lax.axis_index("y"), lax.axis_index("c")
    others = [(1 - x, y), (x, 1 - y), (1 - x, 1 - y)]
    return x, y, c, 2 * x + y, others


def _remote(src, dst, send_sems, recv_sems, k, to):
    return pltpu.make_async_remote_copy(src_ref=src, dst_ref=dst, send_sem=send_sems.at[k], recv_sem=recv_sems.at[k],
                                        device_id=to, device_id_type=MESH)


def _half(ref, axis, j, size, h):
    if len(ref.shape) == 3:
        return ref.at[:, pl.ds(j * size + h * (size // 2), size // 2), :]
    if axis == 0:
        return ref.at[pl.ds(j * size + h * (size // 2), size // 2), :]
    rows = ref.shape[0] // 2
    return ref.at[pl.ds(h * rows, rows), pl.ds(j * size, size)]


def _half_shard_shape(shape, axis, size):
    if len(shape) == 3:
        return (shape[0], size // 2, shape[2])
    if axis == 0:
        return (size // 2, shape[1])
    return (shape[0] // 2, size)


class Exchange:
    def __init__(self, arrays, out_shapes, aliases, n_sems, begin, finish, late=None):
        self.arrays, self.out_shapes, self.aliases, self.n_sems = list(arrays), list(out_shapes), aliases, n_sems
        self.begin, self.finish, self.late = begin, finish, late


class _Hosting:
    def __init__(self, plan):
        self.plan = list(plan or [])
        self.arrays = [a for ex in self.plan for a in ex.arrays]
        self.out_shapes = [o for ex in self.plan for o in ex.out_shapes]
        self.n_sems = sum(ex.n_sems for ex in self.plan)

    def scratch(self):
        return [pltpu.SemaphoreType.DMA((self.n_sems,)), pltpu.SemaphoreType.DMA((self.n_sems,))] if self.plan else []

    def aliases(self, in_base, out_base):
        out, i0, o0 = {}, in_base, out_base
        for ex in self.plan:
            out.update({i0 + i: o0 + o for i, o in ex.aliases.items()})
            i0, o0 = i0 + len(ex.arrays), o0 + len(ex.out_shapes)
        return out

    def _each(self, in_refs, out_refs):
        i0 = o0 = s0 = 0
        for ex in self.plan:
            yield ex, in_refs[i0:i0 + len(ex.arrays)], out_refs[o0:o0 + len(ex.out_shapes)], s0
            i0, o0, s0 = i0 + len(ex.arrays), o0 + len(ex.out_shapes), s0 + ex.n_sems

    def begin(self, in_refs, out_refs, send_sems, recv_sems):
        for ex, ins, outs, s0 in self._each(in_refs, out_refs):
            ex.begin(ins, outs, send_sems, recv_sems, s0)

    def late(self, in_refs, out_refs, send_sems, recv_sems):
        for ex, ins, outs, s0 in self._each(in_refs, out_refs):
            if ex.late is not None:
                ex.late(ins, outs, send_sems, recv_sems, s0)

    def finish(self, in_refs, out_refs, send_sems, recv_sems):
        for ex, ins, outs, s0 in self._each(in_refs, out_refs):
            ex.finish(ins, outs, send_sems, recv_sems, s0)


def _first_last(grid):
    ids = [pl.program_id(a) for a in range(len(grid))]
    first = functools.reduce(jnp.logical_and, [i == 0 for i in ids])
    last = functools.reduce(jnp.logical_and, [i == g - 1 for i, g in zip(ids, grid)])
    return first, last


def run_exchanges(plan, *, name):
    host = _Hosting(plan)
    n_in, n_out = len(host.arrays), len(host.out_shapes)

    def body(*refs):
        ins, outs = refs[:n_in], refs[n_in:n_in + n_out]
        send_sems, recv_sems = refs[n_in + n_out:]
        host.begin(ins, outs, send_sems, recv_sems)
        host.late(ins, outs, send_sems, recv_sems)
        host.finish(ins, outs, send_sems, recv_sems)

    return pl.pallas_call(
        body, name=name, in_specs=[ANY] * n_in, out_specs=[ANY] * n_out, out_shape=host.out_shapes,
        scratch_shapes=host.scratch(), input_output_aliases=host.aliases(0, 0),
        compiler_params=pltpu.CompilerParams(has_side_effects=True))(*host.arrays)


def place_shard(src, l, axis, size, out_dtype, place, *, name):
    shard = src.shape[1:]
    natural = tuple(size * N_CHIPS if a == axis else s for a, s in enumerate(shard))
    if len(shard) == 3:
        blk = (None,) + shard
        grid = (1,)
        in_map = lambda i, pr: (l, 0, 0, 0)
        out_map = lambda i, pr: (0, 0, pr[0], 0)
    else:
        tr = _row_tile(shard[0], shard[1], 16)
        steps = shard[0] // tr
        blk = (None, tr, shard[1])
        grid = (steps,)
        in_map = lambda i, pr: (l, i, 0)
        if axis == 0:
            out_map = lambda i, pr: (0, pr[0] * steps + i, 0)
        else:
            out_map = lambda i, pr: (0, i, pr[0])

    def body(pr_ref, s_ref, o_ref):
        del pr_ref
        o_ref[...] = s_ref[...].astype(o_ref.dtype)

    return pl.pallas_call(
        body, name=name,
        grid_spec=pltpu.PrefetchScalarGridSpec(
            num_scalar_prefetch=1, grid=grid, in_specs=[pl.BlockSpec(blk, in_map)],
            out_specs=pl.BlockSpec(blk, out_map)),
        out_shape=jax.ShapeDtypeStruct((1,) + natural, out_dtype), compiler_params=_params(1))(place, src)


def place_both_layers(src, axis, size, place, *, name):
    rows, cols = src.shape[1], src.shape[2]

    def body(pr_ref, s_ref, o_ref):
        del pr_ref
        o_ref[...] = s_ref[...]

    return pl.pallas_call(
        body, name=name,
        grid_spec=pltpu.PrefetchScalarGridSpec(
            num_scalar_prefetch=1, grid=(2,), in_specs=[pl.BlockSpec((None, rows, cols), lambda lyr, pr: (lyr, 0, 0))],
            out_specs=pl.BlockSpec((None, rows, cols), lambda lyr, pr: (lyr, 0, pr[0]))),
        out_shape=jax.ShapeDtypeStruct((2, rows, cols * N_CHIPS), src.dtype), compiler_params=_params(1))(place, src)


def gather_exchange(arrays, geom):
    n = len(arrays)

    def begin(ins, outs, send_sems, recv_sems, s0):
        x, y, c, j, others = _place()
        for t, (axis, size) in enumerate(geom):
            mine = _half(outs[t].at[0], axis, j, size, c)
            for k, (ox, oy) in enumerate(others):
                _remote(mine, mine, send_sems, recv_sems, s0 + 6 * t + k, (ox, oy, c)).start()

    def passes_on(outs, send_sems, recv_sems, s0, c, sib, others):
        return [_remote(landed, landed, send_sems, recv_sems, s0 + 6 * t + 3 + k, sib)
                for t, (axis, size) in enumerate(geom) for k, (ox, oy) in enumerate(others)
                for landed in [_half(outs[t].at[0], axis, 2 * ox + oy, size, c)]]

    def late(ins, outs, send_sems, recv_sems, s0):
        x, y, c, j, others = _place()
        for t, (axis, size) in enumerate(geom):
            for k, (ox, oy) in enumerate(others):
                landed = _half(outs[t].at[0], axis, 2 * ox + oy, size, c)
                _remote(landed, landed, send_sems, recv_sems, s0 + 6 * t + k, (ox, oy, c)).wait_recv()
        for fwd in passes_on(outs, send_sems, recv_sems, s0, c, (x, y, 1 - c), others):
            fwd.start()

    def finish(ins, outs, send_sems, recv_sems, s0):
        x, y, c, j, others = _place()
        sib = (x, y, 1 - c)
        for t, (axis, size) in enumerate(geom):
            for k, (ox, oy) in enumerate(others):
                got = _half(outs[t].at[0], axis, 2 * ox + oy, size, 1 - c)
                _remote(got, got, send_sems, recv_sems, s0 + 6 * t + 3 + k, sib).wait_recv()
        for fwd in passes_on(outs, send_sems, recv_sems, s0, c, sib, others):
            fwd.wait_send()
        for t, (axis, size) in enumerate(geom):
            mine = _half(outs[t].at[0], axis, j, size, c)
            for k, (ox, oy) in enumerate(others):
                _remote(mine, mine, send_sems, recv_sems, s0 + 6 * t + k, (ox, oy, c)).wait_send()

    return Exchange(arrays, [jax.ShapeDtypeStruct(a.shape, a.dtype) for a in arrays], {t: t for t in range(n)},
                    6 * n, begin, finish, late)


def gather_by_layer_exchange(array, axis, size):
    def blocks(out, others, lyr):
        return [_block(out.at[lyr], axis, 2 * ox + oy, size) for (ox, oy) in others]

    def begin(ins, outs, send_sems, recv_sems, s0):
        x, y, c, j, others = _place()
        mine = _block(outs[0].at[c], axis, j, size)
        for k, (ox, oy) in enumerate(others):
            _remote(mine, mine, send_sems, recv_sems, s0 + k, (ox, oy, c)).start()

    def finish(ins, outs, send_sems, recv_sems, s0):
        x, y, c, j, others = _place()
        sib = (x, y, 1 - c)
        passed = []
        for k, ((ox, oy), landed) in enumerate(zip(others, blocks(outs[0], others, c))):
            _remote(landed, landed, send_sems, recv_sems, s0 + k, (ox, oy, c)).wait_recv()
            fwd = _remote(landed, landed, send_sems, recv_sems, s0 + 3 + k, sib)
            fwd.start()
            passed.append(fwd)
        for k, got in enumerate(blocks(outs[0], others, 1 - c)):
            _remote(got, got, send_sems, recv_sems, s0 + 3 + k, sib).wait_recv()
        for fwd in passed:
            fwd.wait_send()
        mine = _block(outs[0].at[c], axis, j, size)
        for k, (ox, oy) in enumerate(others):
            _remote(mine, mine, send_sems, recv_sems, s0 + k, (ox, oy, c)).wait_send()

    return Exchange([array], [jax.ShapeDtypeStruct(array.shape, array.dtype)], {0: 0}, 6, begin, finish)


def swap_exchange(grads, geom):
    def pieces(t, g, dst, h):
        axis, size = geom[t]
        if len(g.shape) == 2 and axis == 1:
            rows = g.shape[0] // 2
            return [(g.at[pl.ds(h * rows, rows), :], dst)]
        return [(_half(g, axis, jb, size, h), dst.at[jb]) for jb in range(N_CHIPS)]

    counts = [1 if (len(g.shape) == 3 and a == 1) else N_CHIPS for g, (a, _) in zip(grads, geom)]
    bases = [sum(counts[:t]) for t in range(len(grads))]

    def copies(ins, outs, send_sems, recv_sems, s0):
        x, y, c, _, _ = _place()
        cps = []
        for t in range(len(grads)):
            for q, (src, dst) in enumerate(pieces(t, ins[t].at[0], outs[t], 1 - c)):
                cps.append(_remote(src, dst, send_sems, recv_sems, s0 + bases[t] + q, (x, y, 1 - c)))
        return cps

    def begin(*a):
        for cp in copies(*a):
            cp.start()

    def finish(*a):
        for cp in copies(*a):
            cp.wait()

    out_shapes = []
    for g, (axis, size) in zip(grads, geom):
        shp = g.shape[1:]
        if len(shp) == 2 and axis == 1:
            out_shapes.append(jax.ShapeDtypeStruct((shp[0] // 2, shp[1]), g.dtype))
        else:
            out_shapes.append(jax.ShapeDtypeStruct((N_CHIPS,) + _half_shard_shape(shp, axis, size), g.dtype))
    return Exchange(grads, out_shapes, {}, sum(counts), begin, finish)


def scatter_exchange(parts, geom, shapes):
    def copies(ins, outs, send_sems, recv_sems, s0):
        x, y, c, j, others = _place()
        cps = []
        for t, ((axis, size), shp) in enumerate(zip(geom, shapes)):
            for k, (ox, oy) in enumerate(others):
                jp = 2 * ox + oy
                src = ins[t].at[:, pl.ds(jp * size, size)] if (len(shp) == 2 and axis == 1) else ins[t].at[jp]
                cps.append(_remote(src, outs[t].at[k], send_sems, recv_sems, s0 + 3 * t + k, (ox, oy, c)))
        return cps

    def begin(*a):
        for cp in copies(*a):
            cp.start()

    def finish(*a):
        for cp in copies(*a):
            cp.wait_recv()
        for cp in copies(*a):
            cp.wait_send()

    out_shapes = [jax.ShapeDtypeStruct((3,) + _half_shard_shape(shp, axis, size), p.dtype)
                  for p, (axis, size), shp in zip(parts, geom, shapes)]
    return Exchange(parts, out_shapes, {}, 3 * len(parts), begin, finish)


def share_exchange(grads, which):
    n = len(which)

    def my_half(refs, t, h):
        lyr = refs[which[t][0]].at[which[t][1]]
        if len(lyr.shape) == 3:
            rows = lyr.shape[1] // 2
            return lyr.at[:, pl.ds(h * rows, rows), :]
        rows = lyr.shape[0] // 2
        return lyr.at[pl.ds(h * rows, rows), :]

    def begin(ins, outs, send_sems, recv_sems, s0):
        x, y, c, _, _ = _place()
        for t in range(n):
            mine = my_half(outs, t, c)
            _remote(mine, mine, send_sems, recv_sems, s0 + t, (x, y, 1 - c)).start()

    def finish(ins, outs, send_sems, recv_sems, s0):
        x, y, c, _, _ = _place()
        for t in range(n):
            got = my_half(outs, t, 1 - c)
            _remote(got, got, send_sems, recv_sems, s0 + t, (x, y, 1 - c)).wait_recv()
        for t in range(n):
            mine = my_half(outs, t, c)
            _remote(mine, mine, send_sems, recv_sems, s0 + t, (x, y, 1 - c)).wait_send()

    return Exchange(grads, [jax.ShapeDtypeStruct(g.shape, g.dtype) for g in grads],
                    {t: t for t in range(len(grads))}, n, begin, finish)


def all_reduce_small(s):
    rows = s.shape[0]
    half = rows // 2
    assert half % 8 == 0

    def body(s_ref, o_ref, a_ref, b_ref, p_ref, send_sems, recv_sems):
        x, y, c, j, others = _place()
        sib = (x, y, 1 - c)
        swap = _remote(s_ref, a_ref, send_sems, recv_sems, 0, sib)
        swap.start()
        swap.wait()
        p_ref[...] = s_ref[...] + a_ref[...]
        mine = pl.ds(pl.multiple_of(c * half, 8), half)
        b_ref[j] = p_ref[mine, :]
        cps = [_remote(p_ref.at[mine, :], b_ref.at[j], send_sems, recv_sems, 1 + k, (ox, oy, c))
               for k, (ox, oy) in enumerate(others)]
        for cp in cps:
            cp.start()
        for k, (ox, oy) in enumerate(others):
            slot = b_ref.at[2 * ox + oy]
            _remote(slot, slot, send_sems, recv_sems, 1 + k, (ox, oy, c)).wait_recv()
        for cp in cps:
            cp.wait_send()
        o_ref[mine, :] = ((b_ref[0] + b_ref[1]) + b_ref[2]) + b_ref[3]
        back = _remote(o_ref.at[mine, :], o_ref.at[mine, :], send_sems, recv_sems, 4, sib)
        back.start()
        back.wait_send()
        theirs = pl.ds(pl.multiple_of((1 - c) * half, 8), half)
        _remote(o_ref.at[theirs, :], o_ref.at[theirs, :], send_sems, recv_sems, 4, sib).wait_recv()

    vmem = pl.BlockSpec(memory_space=pltpu.VMEM)
    return pl.pallas_call(
        body, name="all_reduce_small", in_specs=[vmem], out_specs=vmem,
        out_shape=jax.ShapeDtypeStruct((rows, LANES), F32),
        scratch_shapes=[pltpu.VMEM((rows, LANES), F32), pltpu.VMEM((N_CHIPS, half, LANES), F32),
                        pltpu.VMEM((rows, LANES), F32), pltpu.SemaphoreType.DMA((5,)),
                        pltpu.SemaphoreType.DMA((5,))],
        compiler_params=pltpu.CompilerParams(vmem_limit_bytes=VMEM_LIMIT, has_side_effects=True))(s)


def pair_sum(g, got, axis, size, place, *, name):
    shp = g.shape[1:]
    if len(shp) == 3:
        hs = size // 2
        grid = (N_CHIPS,)
        g_spec = pl.BlockSpec((None, shp[0], hs, shp[2]), lambda jb, pr: (0, 0, 2 * jb + pr[1], 0))
        r_spec = pl.BlockSpec((None, shp[0], hs, shp[2]), lambda jb, pr: (jb, 0, 0, 0))
    elif axis == 0:
        hs = size // 2
        tr = _row_tile(hs, shp[1], 16)
        steps = hs // tr
        grid = (N_CHIPS, steps)
        g_spec = pl.BlockSpec((None, tr, shp[1]), lambda jb, i, pr: (0, (2 * jb + pr[1]) * steps + i, 0))
        r_spec = pl.BlockSpec((None, tr, shp[1]), lambda jb, i, pr: (jb, i, 0))
    else:
        rows = shp[0] // 2
        tr = _row_tile(rows, shp[1], 16)
        steps = rows // tr
        grid = (steps,)
        g_spec = pl.BlockSpec((None, tr, shp[1]), lambda i, pr: (0, pr[1] * steps + i, 0))
        r_spec = pl.BlockSpec((tr, shp[1]), lambda i, pr: (i, 0))

    def body(pr_ref, g_ref, r_ref, o_ref):
        del pr_ref
        o_ref[...] = (g_ref[...].astype(F32) + r_ref[...].astype(F32)).astype(BF16)

    return pl.pallas_call(
        body, name=name,
        grid_spec=pltpu.PrefetchScalarGridSpec(num_scalar_prefetch=1, grid=grid, in_specs=[g_spec, r_spec],
                                               out_specs=r_spec),
        out_shape=jax.ShapeDtypeStruct(got.shape, BF16), compiler_params=_params(len(grid)))(place, g, got)


def chip_sum(part, slots, shp, axis, size, l, place, out, *, name):
    shard = _shard_shape(shp, axis, size)
    hshape = slots.shape[1:]
    if len(shp) == 3:
        grid = (1,)
        p_spec = pl.BlockSpec((None,) + hshape, lambda i, pr: (pr[0], 0, 0, 0))
        s_specs = [pl.BlockSpec((None,) + hshape, functools.partial(lambda i, pr, k: (k, 0, 0, 0), k=k))
                   for k in range(3)]
        o_spec = pl.BlockSpec((None,) + hshape, lambda i, pr: (l, 0, pr[1], 0))
    else:
        tr = _row_tile(hshape[0], hshape[1], 16)
        steps = hshape[0] // tr
        grid = (steps,)
        if axis == 0:
            p_spec = pl.BlockSpec((None, tr, hshape[1]), lambda i, pr: (pr[0], i, 0))
        else:
            p_spec = pl.BlockSpec((tr, hshape[1]), lambda i, pr: (i, pr[0]))
        s_specs = [pl.BlockSpec((None, tr, hshape[1]), functools.partial(lambda i, pr, k: (k, i, 0), k=k))
                   for k in range(3)]
        o_spec = pl.BlockSpec((None, tr, hshape[1]), lambda i, pr: (l, pr[1] * steps + i, 0))
    has_out = out is not None

    def body(pr_ref, p_ref, s0_ref, s1_ref, s2_ref, *rest):
        del pr_ref
        rest[-1][...] = ((p_ref[...].astype(F32) + s0_ref[...].astype(F32)) + s1_ref[...].astype(F32)) \
            + s2_ref[...].astype(F32)

    return pl.pallas_call(
        body, name=name,
        grid_spec=pltpu.PrefetchScalarGridSpec(
            num_scalar_prefetch=1, grid=grid, in_specs=[p_spec] + s_specs + ([ANY] if has_out else []),
            out_specs=o_spec),
        out_shape=jax.ShapeDtypeStruct((2,) + shard, F32), input_output_aliases={5: 0} if has_out else {},
        compiler_params=_params(1))(place, part, slots, slots, slots, *([out] if has_out else []))


GEOM = {name: (axis, size) for (name, _, axis, size) in BIG}
SHAPE = {name: shape for (name, shape, _, _) in BIG}
RIDES_IN_PROJ_L0 = ((0, ("w_pool", "w_branch_a", "w_branch_b", "w_out", "w_up")),)
RIDES_UP_PROJ_L0 = ((0, ("w_down", "w_ple_gate", "w_ple")), (1, ("w_in",)))
RIDES_DOWN_PROJ_L0 = ((1, ("w_pool", "w_branch_a", "w_branch_b", "w_out")),)
RIDES_IN_PROJ_L1 = ((1, ("w_up", "w_down", "w_ple_gate", "w_ple")),)
EARLY_GRADS_L0 = ("w_ple", "w_ple_gate", "w_down", "w_up")
LATE_GRADS_L0 = ("w_out", "w_branch_a", "w_branch_b", "w_pool", "w_in")


def _swap_of(G, names):
    return swap_exchange([G[k] for k in names], [GEOM[k] for k in names])


def _after_swap(G, names, got, place, tag):
    parts = [pair_sum(G[k], r, *GEOM[k], place, name=f"pair_sum_{k}_{tag}") for k, r in zip(names, got)]
    return scatter_exchange(parts, [GEOM[k] for k in names], [SHAPE[k] for k in names]), parts


def _reduce_start(G, names, place, tag):
    got = run_exchanges([_swap_of(G, names)], name=f"swap_halves_{tag}")
    return _after_swap(G, names, got, place, tag)


def _reduce_end(names, parts, slots, place, l, reduced):
    for k, q, s in zip(names, parts, slots):
        reduced[k] = chip_sum(q, s, SHAPE[k], *GEOM[k], l, place, reduced.get(k), name=f"chip_sum_{k}_l{l}")


def _local_step(x, p2, tgt, W0, W1, conv_w, small, place):
    T = x.shape[0]
    as3 = lambda a: a.reshape(2, 1, a.shape[-1])
    mix3, scale3, sgu3 = as3(small["mix_norm"]), as3(small["pool_scale"]), as3(small["sgu_norm"])
    ffn3, ple3, convb3 = as3(small["ffn_norm"]), as3(small["ple_norm"]), as3(small["conv_b"])
    tril = jnp.tril(jnp.ones((CHUNK, CHUNK), F32))
    ws_masked = small["w_spatial"] * tril
    wsm = ws_masked.astype(BF16)
    wsmT = jnp.swapaxes(ws_masked, -1, -2).astype(BF16)
    bT = jnp.swapaxes(small["b_spatial"], -1, -2)
    final3 = small["final_norm"].reshape(1, D)
    W = [dict(W0), dict(W1)]

    def riders(groups):
        return [gather_exchange([W[lyr][k] for k in names], [GEOM[k] for k in names]) for lyr, names in groups]

    def landed(groups, got):
        for lyr, names in groups:
            W[lyr].update(zip(names, got[:len(names)]))
            got = got[len(names):]

    saved = []
    hb, (W[0]["w_in"], conv_w) = norm_fwd(
        x, mix3, 0, name="mix_norm_fwd_l0",
        host=[gather_exchange([W[0]["w_in"]], [GEOM["w_in"]]),
              gather_by_layer_exchange(conv_w, 1, conv_w.shape[2] // N_CHIPS)])
    for l in range(2):
        n = lambda s: f"{s}_l{l}"
        Wl = W[l]
        groups = RIDES_IN_PROJ_L0 if l == 0 else RIDES_IN_PROJ_L1
        z, got = mm_nn(hb, Wl["w_in"], 0, name=n("in_proj"), rows=T, tn=1280, out_dtype=BF16, host=riders(groups))
        landed(groups, got)
        a_in = pool_fwd(z, Wl["w_pool"], scale3, l, name=n("pool_fwd"))
        s_in = sgu_fwd(z, sgu3, wsm, bT, l, name=n("sgu_fwd"))
        yab = mm_nn(a_in, Wl["w_branch_a"], 0, name=n("branch_a"), rows=T, out_cols=2 * D, out_dtype=BF16)
        yab = mm_nn(s_in, Wl["w_branch_b"], 0, name=n("branch_b"), rows=T, out=yab, out_cols=2 * D, out_col_off=D,
                    out_dtype=BF16)
        mo = gate_fwd(z, yab, name=n("gate_fwd"))
        x1, h2b = mm_nn(mo, Wl["w_out"], 0, name=n("out_proj"), rows=T, resid=x, norm_gain=ffn3[l:l + 1])
        if l == 0:
            up, got = mm_nn(h2b, Wl["w_up"], 0, name=n("up_proj"), rows=T, tn=DFF, host=riders(RIDES_UP_PROJ_L0))
            landed(RIDES_UP_PROJ_L0, got)
        else:
            up = mm_nn(h2b, Wl["w_up"], 0, name=n("up_proj"), rows=T, tn=DFF)
        f = conv_fwd(up, conv_w, convb3, l, name=n("conv_fwd"))
        if l == 0:
            (x2, h3b), got = mm_nn(f, Wl["w_down"], 0, name=n("down_proj"), rows=T, resid=x1,
                                   norm_gain=ple3[l:l + 1], host=riders(RIDES_DOWN_PROJ_L0))
            landed(RIDES_DOWN_PROJ_L0, got)
        else:
            x2, h3b = mm_nn(f, Wl["w_down"], 0, name=n("down_proj"), rows=T, resid=x1, norm_gain=ple3[l:l + 1])
        pg = mm_nn(h3b, Wl["w_ple_gate"], 0, name=n("ple_gate_proj"), rows=T, out_dtype=BF16)
        e = mm_nn(p2, Wl["w_ple"], 0, name=n("ple_proj"), rows=T, a_row_off=l * T, out_dtype=BF16)
        saved.append(dict(x=x, hb=hb, z=z, a_in=a_in, s_in=s_in, yab=yab, mo=mo, x1=x1, h2b=h2b, up=up, f=f,
                          x2=x2, h3b=h3b, pg=pg, e=e))
        if l == 0:
            x, hb = ple_fwd(x2, pg, e, mix3, 1, name=n("ple_fwd"))
        else:
            x = ple_fwd(x2, pg, e, None, 0, name=n("ple_fwd"))

    loss_acc, dx, dg_final = loss_head(x, final3, tgt, name="loss_head")

    small_grads = [None, None]
    all_names = [t[0] for t in BIG]
    reduced = {}
    swap1 = G1 = scatter1 = parts1 = slots1 = None
    for l in (1, 0):
        n = lambda s: f"{s}_l{l}"
        a, Wl, G = saved[l], W[l], {}
        de, dpg = ple_bwd(dx, a["pg"], a["e"], name=n("ple_bwd"))
        G["w_ple"] = mm_tn(p2, de, name=n("d_w_ple"), rows=T, ka=PG, nb=D, a_row_off=l * T)
        G["w_ple_gate"] = mm_tn(a["h3b"], dpg, name=n("d_w_ple_gate"), rows=T, ka=D, nb=D)
        if l == 0:
            (dx2, dg_ple), got = mm_nt(dpg, Wl["w_ple_gate"], 0, name=n("ple_norm_bwd"), rows=T,
                                       norm_bwd_of=(a["x2"], ple3, l, dx), host=[swap1])
            scatter1, parts1 = _after_swap(G1, all_names, got, place, "l1")
        else:
            dx2, dg_ple = mm_nt(dpg, Wl["w_ple_gate"], 0, name=n("ple_norm_bwd"), rows=T,
                                norm_bwd_of=(a["x2"], ple3, l, dx))
        df = mm_nt(dx2, Wl["w_down"], 0, name=n("d_ffn_act"), rows=T, out_dtype=F32)
        G["w_down"] = mm_tn(a["f"], dx2, name=n("d_w_down"), rows=T, ka=DFF, nb=D, tm=1408)
        if l == 0:
            dup, dcw, dcb, slots1 = conv_bwd(df, a["up"], conv_w, convb3, l, name=n("conv_bwd"), host=[scatter1])
        else:
            dup, dcw, dcb, _ = conv_bwd(df, a["up"], conv_w, convb3, l, name=n("conv_bwd"))
        G["w_up"] = mm_tn(a["h2b"], dup, name=n("d_w_up"), rows=T, ka=D, nb=2 * DFF, tn=DFF, tk=1024)
        if l == 0:
            scatter_early, parts_early = _reduce_start(G, EARLY_GRADS_L0, place, "l0_early")
            (dx1, dg_ffn), slots_early = mm_nt(dup, Wl["w_up"], 0, name=n("ffn_norm_bwd"), rows=T, tk=1408,
                                               norm_bwd_of=(a["x1"], ffn3, l, dx2), host=[scatter_early])
        else:
            dx1, dg_ffn = mm_nt(dup, Wl["w_up"], 0, name=n("ffn_norm_bwd"), rows=T, tk=1408,
                                norm_bwd_of=(a["x1"], ffn3, l, dx2))
        dmo = mm_nt(dx1, Wl["w_out"], 0, name=n("d_gated"), rows=T)
        G["w_out"] = mm_tn(a["mo"], dx1, name=n("d_w_out"), rows=T, ka=D, nb=D)
        dz, dyab = gate_bwd(dmo, a["z"], a["yab"], name=n("gate_bwd"))
        G["w_branch_a"] = mm_tn(a["a_in"], dyab, name=n("d_w_branch_a"), rows=T, ka=D, nb=D)
        G["w_branch_b"] = mm_tn(a["s_in"], dyab, name=n("d_w_branch_b"), rows=T, ka=D, nb=D, b_col_off=D)
        da = mm_nt(dyab, Wl["w_branch_a"], 0, name=n("d_pool_out"), rows=T, kdim=D)
        ds = mm_nt(dyab, Wl["w_branch_b"], 0, name=n("d_sgu_out"), rows=T, kdim=D, a_col_off=D)
        dz, dwp, dsc, dws, dbt, dgs = mixer_bwd(da, ds, a["z"], dz, Wl["w_pool"], scale3, sgu3, wsm, wsmT, bT, l,
                                                name=n("mixer_bwd"))
        G["w_pool"] = dwp.astype(BF16)[None]
        G["w_in"] = mm_tn(a["hb"], dz, name=n("d_w_in"), rows=T, ka=D, nb=5 * D, tn=1280)
        if l == 0:
            scatter_late, parts_late = _reduce_start(G, LATE_GRADS_L0, place, "l0_late")
            _reduce_end(all_names, parts1, slots1, place, 1, reduced)
            _reduce_end(EARLY_GRADS_L0, parts_early, slots_early, place, 0, reduced)
            done = [(t, 1) for t in range(len(all_names))] + [(all_names.index(k), 0) for k in EARLY_GRADS_L0]
            (dx, dg_mix), got = mm_nt(
                dz, Wl["w_in"], 0, name=n("mix_norm_bwd"), rows=T, tk=1280, norm_bwd_of=(a["x"], mix3, l, dx1),
                host=[scatter_late, share_exchange([reduced[k] for k in all_names], done)])
            slots_late = got[:len(LATE_GRADS_L0)]
            reduced.update(zip(all_names, got[len(LATE_GRADS_L0):]))
        else:
            dx, dg_mix = mm_nt(dz, Wl["w_in"], 0, name=n("mix_norm_bwd"), rows=T, tk=1280,
                               norm_bwd_of=(a["x"], mix3, l, dx1))
            swap1, G1 = _swap_of(G, all_names), G
        small_grads[l] = dict(
            mix_norm=dg_mix[0], pool_scale=dsc[0], sgu_norm=dgs[0], w_spatial=dws, b_spatial=dbt.T,
            ffn_norm=dg_ffn[0], conv_b=jnp.concatenate([dcb[0, 0], dcb[1, 0]]), ple_norm=dg_ple[0],
            conv_w=jnp.concatenate([dcw[0, :3], dcw[1, :3]], axis=1))
    _reduce_end(LATE_GRADS_L0, parts_late, slots_late, place, 0, reduced)
    return loss_acc, dx, reduced, small_grads, dg_final[0]


SMALL_ORDER = ("mix_norm", "pool_scale", "sgu_norm", "w_spatial", "b_spatial", "ffn_norm", "conv_b", "ple_norm",
               "conv_w")


def _pack_rows(pieces, row_multiple):
    flat = jnp.concatenate([a.reshape(-1) for a in pieces])
    rows = -(-flat.shape[0] // LANES)
    rows = -(-rows // row_multiple) * row_multiple
    return jnp.pad(flat, (0, rows * LANES - flat.shape[0])).reshape(rows, LANES)


def _unpack(flat, shapes):
    out, off = [], 0
    for shp in shapes:
        size = 1
        for s in shp:
            size *= s
        out.append(flat[off:off + size].reshape(shp))
        off += size
    return out


def kernel(x, p, mix_norm, w_in, w_pool, pool_scale, sgu_norm, w_spatial, b_spatial, w_branch_a, w_branch_b, w_out, ffn_norm, w_up, conv_w, conv_b, w_down, ple_norm, w_ple_gate, w_ple, final_norm, loss_target, m_mix_norm, m_w_in, m_w_pool, m_pool_scale, m_sgu_norm, m_w_spatial, m_b_spatial, m_w_branch_a, m_w_branch_b, m_w_out, m_ffn_norm, m_w_up, m_conv_w, m_conv_b, m_w_down, m_ple_norm, m_w_ple_gate, m_w_ple, m_final_norm, v_mix_norm, v_w_in, v_w_pool, v_pool_scale, v_sgu_norm, v_w_spatial, v_b_spatial, v_w_branch_a, v_w_branch_b, v_w_out, v_ffn_norm, v_w_up, v_conv_w, v_conv_b, v_w_down, v_ple_norm, v_w_ple_gate, v_w_ple, v_final_norm):
    names = ["mix_norm", "w_in", "w_pool", "pool_scale", "sgu_norm", "w_spatial", "b_spatial", "w_branch_a",
             "w_branch_b", "w_out", "ffn_norm", "w_up", "conv_w", "conv_b", "w_down", "ple_norm", "w_ple_gate",
             "w_ple", "final_norm"]
    w = dict(zip(names, [mix_norm, w_in, w_pool, pool_scale, sgu_norm, w_spatial, b_spatial, w_branch_a, w_branch_b,
                         w_out, ffn_norm, w_up, conv_w, conv_b, w_down, ple_norm, w_ple_gate, w_ple, final_norm]))
    m = dict(zip(names, [m_mix_norm, m_w_in, m_w_pool, m_pool_scale, m_sgu_norm, m_w_spatial, m_b_spatial,
                         m_w_branch_a, m_w_branch_b, m_w_out, m_ffn_norm, m_w_up, m_conv_w, m_conv_b, m_w_down,
                         m_ple_norm, m_w_ple_gate, m_w_ple, m_final_norm]))
    v = dict(zip(names, [v_mix_norm, v_w_in, v_w_pool, v_pool_scale, v_sgu_norm, v_w_spatial, v_b_spatial,
                         v_w_branch_a, v_w_branch_b, v_w_out, v_ffn_norm, v_w_up, v_conv_w, v_conv_b, v_w_down,
                         v_ple_norm, v_w_ple_gate, v_w_ple, v_final_norm]))
    T = x.shape[1]
    chip = 2 * lax.axis_index("x") + lax.axis_index("y")
    place = jnp.stack([chip, lax.axis_index("c")]).astype(jnp.int32)

    big_names = [t[0] for t in BIG]
    placed = [{k: place_shard(w[k], l, *GEOM[k], BF16, place, name=f"place_{k}_l{l}") for k in big_names}
              for l in range(2)]
    conv_w8 = jnp.pad(conv_w, ((0, 0), (0, CONV_ROWS - conv_w.shape[1]), (0, 0)))
    conv_placed = place_both_layers(conv_w8, 1, conv_w.shape[2], place, name="place_conv_w")

    small = {k: w[k] for k in ("mix_norm", "pool_scale", "sgu_norm", "w_spatial", "b_spatial", "ffn_norm",
                               "conv_b", "ple_norm", "final_norm")}
    loss_acc, dx, reduced, small_grads, dg_final = _local_step(
        x.reshape(T, D), p.reshape(2 * T, p.shape[-1]), loss_target.reshape(T, D), placed[0], placed[1], conv_placed,
        small, place)
    loss = lax.psum(loss_acc[0, 0], ("x", "y", "c"))
    full = run_exchanges([share_exchange([reduced[k] for k in big_names],
                                         [(big_names.index(k), 0) for k in LATE_GRADS_L0])], name="share_last_halves")
    grads = dict(zip(big_names, full))

    pieces = [small_grads[l][k] for l in range(2) for k in SMALL_ORDER] + [dg_final]
    shapes = [a.shape for a in pieces]
    total = all_reduce_small(_pack_rows(pieces, 16)).reshape(-1)
    summed = _unpack(total, shapes)
    per_layer = {k: jnp.stack([summed[i], summed[len(SMALL_ORDER) + i]]) for i, k in enumerate(SMALL_ORDER)}
    for k in ("mix_norm", "pool_scale", "sgu_norm", "w_spatial", "b_spatial", "ffn_norm", "conv_b", "ple_norm"):
        grads[k] = per_layer[k]
    grads["final_norm"] = summed[-1]
    cw = conv_w.shape[2]
    grads["conv_w"] = lax.dynamic_slice_in_dim(per_layer["conv_w"], chip * cw, cw, axis=2)

    delta, new_m, new_v = {}, {}, {}
    for name in big_names:
        shp = w[name].shape
        d_, m_, v_, g_ = elementwise(lambda w_, g_, m_, v_: (*_adamw(w_, g_, m_, v_), g_),
                                     [_view2d(a) for a in (w[name], grads[name], m[name], v[name])],
                                     [F32, F32, F32, F32], name=f"adamw_{name}")
        delta[name], new_m[name], new_v[name] = d_.reshape(shp), m_.reshape(shp), v_.reshape(shp)
        grads[name] = g_.reshape(shp)
    small_names = [k for k in names if k not in big_names]
    small_shapes = [w[k].shape for k in small_names]
    packed = [_pack_rows([src[k] for k in small_names], 8) for src in (w, grads, m, v)]
    outs = elementwise(_adamw, packed, [F32, F32, F32], name="adamw_small")
    for dst, o in zip((delta, new_m, new_v), outs):
        for k, a in zip(small_names, _unpack(o.reshape(-1), small_shapes)):
            dst[k] = a

    return (loss, dx.reshape(1, T, D), *[grads[k] for k in names], *[delta[k] for k in names],
            *[new_m[k] for k in names], *[new_v[k] for k in names])
```

```python
import functools

import jax
import jax.numpy as jnp
from jax import lax
from jax.experimental import pallas as pl
from jax.experimental.pallas import tpu as pltpu

F32 = jnp.float32
BF16 = jnp.bfloat16
EPS = 1e-6
D = 1024
POOL_WINDOWS = (2, 4, 8, 16)
PG = 256
POOL_HALO = 16
CHUNK = 128
HEADS = 8
DFF = 2816
CONV_HALO = 8
CONV_TC = 1408
N_CHIPS = 4
LANES = 128
VMEM_LIMIT = 56 * 1024 * 1024
MESH = pl.DeviceIdType.MESH
ANY = pl.BlockSpec(memory_space=pl.ANY)

ADAM_LR = 0.001
ADAM_B1 = 0.9
ADAM_B2 = 0.999
ADAM_EPS = 1e-08
ADAM_WD = 0.01
ADAM_STEP = 10

BIG = (
    ("w_in", (D, 5 * D), 1, 5 * D // N_CHIPS),
    ("w_pool", (4, PG, PG), 1, PG // N_CHIPS),
    ("w_branch_a", (D, D), 0, D // N_CHIPS),
    ("w_branch_b", (D, D), 0, D // N_CHIPS),
    ("w_out", (D, D), 0, D // N_CHIPS),
    ("w_up", (D, 2 * DFF), 1, 2 * DFF // N_CHIPS),
    ("w_down", (DFF, D), 0, DFF // N_CHIPS),
    ("w_ple_gate", (D, D), 0, D // N_CHIPS),
    ("w_ple", (PG, D), 1, D // N_CHIPS),
)
CONV_ROWS = 8


def _params(n_axes):
    return pltpu.CompilerParams(dimension_semantics=("arbitrary",) * n_axes, vmem_limit_bytes=VMEM_LIMIT)


def _gelu(x):
    return 0.5 * x * (1.0 + lax.erf(x * 0.7071067811865476))


def _gelu_grad(x):
    return 0.5 * (1.0 + lax.erf(x * 0.7071067811865476)) + x * jnp.exp(-0.5 * x * x) * 0.3989422804014327


def _shard_shape(shape, axis, size):
    return tuple(size if a == axis else s for a, s in enumerate(shape))


def _block(ref, axis, j, size):
    idx = tuple(pl.ds(j * size, size) if a == axis else slice(None) for a in range(len(ref.shape)))
    return ref.at[idx]


def mm_nn(a, w, l, *, name, rows, out_dtype=F32, resid=None, a_row_off=0, out=None, out_cols=None,
          out_col_off=0, norm_gain=None, host=None, tm=1024, tn=None, tk=None):
    K, N = w.shape[1], w.shape[2]
    tn = tn or N
    tk = tk or K
    nk = K // tk
    out_cols = out_cols or N
    assert rows % tm == 0 and N % tn == 0 and K % tk == 0 and out_col_off % tn == 0 and a_row_off % tm == 0
    has_resid, has_out, has_norm = resid is not None, out is not None, norm_gain is not None
    assert not has_norm or (tn == N and not has_out)
    grid = (N // tn, rows // tm, nk)
    hosting = _Hosting(host)
    n_in = 2 + has_resid + has_norm + has_out
    n_host_in, n_host_out = len(hosting.arrays), len(hosting.out_shapes)
    n_own_out = 1 + has_norm

    def body(*refs):
        refs = list(refs)
        a_ref, w_ref = refs[0], refs[1]
        r_ref = refs[2] if has_resid else None
        g_ref = refs[2 + has_resid] if has_norm else None
        host_in = refs[n_in:n_in + n_host_in]
        o_base = n_in + n_host_in
        o_ref = refs[o_base]
        host_out = refs[o_base + n_own_out:o_base + n_own_out + n_host_out]
        scratch = refs[o_base + n_own_out + n_host_out:]
        if hosting.plan:
            first, last = _first_last(grid)
            sems = scratch[-2:]

            @pl.when(first)
            def _():
                hosting.begin(host_in, host_out, *sems)

        part = jnp.dot(a_ref[...].astype(BF16), w_ref[...], preferred_element_type=F32)

        def finish(r):
            if has_resid:
                r = r + r_ref[...]
            o_ref[...] = r.astype(o_ref.dtype)
            if has_norm:
                scale = lax.rsqrt(jnp.mean(r * r, axis=-1, keepdims=True) + EPS)
                refs[o_base + 1][...] = (r * scale * g_ref[...]).astype(BF16)

        if nk == 1:
            finish(part)
        else:
            acc = scratch[0]
            k = pl.program_id(2)

            @pl.when(k == 0)
            def _():
                acc[...] = part

            @pl.when(k > 0)
            def _():
                acc[...] += part

            @pl.when(k == nk - 1)
            def _():
                finish(acc[...])

        if hosting.plan:
            @pl.when(last)
            def _():
                hosting.late(host_in, host_out, *sems)
                hosting.finish(host_in, host_out, *sems)

    in_specs = [pl.BlockSpec((tm, tk), lambda j, i, k: (i + a_row_off // tm, k)),
                pl.BlockSpec((None, tk, tn), lambda j, i, k: (l, k, j))]
    args = [a, w]
    if has_resid:
        in_specs.append(pl.BlockSpec((tm, tn), lambda j, i, k: (i, j)))
        args.append(resid)
    if has_norm:
        in_specs.append(pl.BlockSpec((None, 1, tn), lambda j, i, k: (l, 0, 0)))
        args.append(norm_gain)
    aliases = {}
    if has_out:
        in_specs.append(ANY)
        aliases = {len(args): 0}
        args.append(out)
    aliases.update(hosting.aliases(n_in, n_own_out))
    out_specs = [pl.BlockSpec((tm, tn), lambda j, i, k: (i, j + out_col_off // tn))]
    out_shape = [jax.ShapeDtypeStruct((rows, out_cols), out_dtype)]
    if has_norm:
        out_specs.append(pl.BlockSpec((tm, tn), lambda j, i, k: (i, j)))
        out_shape.append(jax.ShapeDtypeStruct((rows, N), BF16))
    res = pl.pallas_call(
        body, name=name, grid=grid,
        in_specs=in_specs + [ANY] * n_host_in,
        out_specs=out_specs + [ANY] * n_host_out,
        out_shape=out_shape + hosting.out_shapes,
        scratch_shapes=([pltpu.VMEM((tm, tn), F32)] if nk > 1 else []) + hosting.scratch(),
        input_output_aliases=aliases, compiler_params=_params(3))(*args, *hosting.arrays)
    own = res[0] if n_own_out == 1 else tuple(res[:n_own_out])
    return (own, list(res[n_own_out:])) if hosting.plan else own


def mm_nt(a, w, l, *, name, rows, kdim=None, a_col_off=0, out_dtype=BF16, norm_bwd_of=None, host=None, tm=1024,
          tn=None, tk=None):
    R = w.shape[1]
    kdim = kdim or w.shape[2]
    tn = tn or R
    tk = tk or kdim
    nk = kdim // tk
    assert rows % tm == 0 and R % tn == 0 and kdim % tk == 0 and a_col_off % tk == 0
    fused = norm_bwd_of is not None
    assert not fused or tn == R
    grid = (R // tn, rows // tm, nk)
    hosting = _Hosting(host)
    n_host_in, n_host_out = len(hosting.arrays), len(hosting.out_shapes)
    n_own_in, n_own_out = (3, 2) if fused else (0, 1)

    def body(a_ref, w_ref, *refs):
        host_in = refs[n_own_in:n_own_in + n_host_in]
        host_out = refs[n_own_in + n_host_in + n_own_out:n_own_in + n_host_in + n_own_out + n_host_out]
        scratch = refs[n_own_in + n_host_in + n_own_out + n_host_out:]
        rest = list(refs[:n_own_in]) + list(refs[n_own_in + n_host_in:n_own_in + n_host_in + n_own_out]) \
            + ([scratch[0]] if nk > 1 else [])
        if hosting.plan:
            first, last = _first_last(grid)
            sems = scratch[-2:]

            @pl.when(first)
            def _():
                hosting.begin(host_in, host_out, *sems)

        part = lax.dot_general(a_ref[...].astype(BF16), w_ref[...], (((1,), (1,)), ((), ())),
                               preferred_element_type=F32)
        i, k = pl.program_id(1), pl.program_id(2)

        def finish(dh):
            if not fused:
                rest[0][...] = dh.astype(rest[0].dtype)
                return
            x_ref, g_ref, dxi_ref, dx_ref, dg_ref = rest[:5]

            @pl.when(i == 0)
            def _():
                dg_ref[...] = jnp.zeros_like(dg_ref)

            xv = x_ref[...]
            r = lax.rsqrt(jnp.mean(xv * xv, axis=-1, keepdims=True) + EPS)
            xh = xv * r
            dhg = dh * g_ref[...]
            dx_ref[...] = dxi_ref[...] + r * (dhg - xh * jnp.mean(dhg * xh, axis=-1, keepdims=True))
            dg_ref[0:1, :] += jnp.sum(dh * xh, axis=0, keepdims=True)

        if nk == 1:
            finish(part)
        else:
            acc = rest[-1]

            @pl.when(k == 0)
            def _():
                acc[...] = part

            @pl.when(k > 0)
            def _():
                acc[...] += part

            @pl.when(k == nk - 1)
            def _():
                finish(acc[...])

        if hosting.plan:
            @pl.when(last)
            def _():
                hosting.late(host_in, host_out, *sems)
                hosting.finish(host_in, host_out, *sems)

    if a.ndim == 3:
        per = a.shape[2] // tk
        a_spec = pl.BlockSpec((None, tm, tk), lambda j, i, k: (k // per, i, k % per))
    else:
        a_spec = pl.BlockSpec((tm, tk), lambda j, i, k: (i, k + a_col_off // tk))
    in_specs = [a_spec, pl.BlockSpec((None, tn, tk), lambda j, i, k: (l, j, k))]
    args = [a, w]
    row_tile = pl.BlockSpec((tm, tn), lambda j, i, k: (i, j))
    if fused:
        x, gain, gl, dx_in = norm_bwd_of
        in_specs += [row_tile, pl.BlockSpec((None, 1, tn), lambda j, i, k: (gl, 0, 0)), row_tile]
        args += [x, gain, dx_in]
        out_specs = [row_tile, pl.BlockSpec((8, tn), lambda j, i, k: (0, 0))]
        out_shape = [jax.ShapeDtypeStruct((rows, R), F32), jax.ShapeDtypeStruct((8, R), F32)]
    else:
        out_specs, out_shape = [row_tile], [jax.ShapeDtypeStruct((rows, R), out_dtype)]
    res = pl.pallas_call(
        body, name=name, grid=grid, in_specs=in_specs + [ANY] * n_host_in,
        out_specs=out_specs + [ANY] * n_host_out, out_shape=out_shape + hosting.out_shapes,
        scratch_shapes=([pltpu.VMEM((tm, tn), F32)] if nk > 1 else []) + hosting.scratch(),
        input_output_aliases=hosting.aliases(2 + n_own_in, n_own_out), compiler_params=_params(3))(
            *args, *hosting.arrays)
    own = tuple(res[:n_own_out]) if fused else res[0]
    return (own, list(res[n_own_out:])) if hosting.plan else own


def mm_tn(a, b, *, name, rows, ka, nb, a_row_off=0, b_col_off=0, tm=None, tn=None, tk=2048):
    tm = tm or ka
    tn = tn or nb
    tk = min(tk, rows)
    nk = rows // tk
    assert ka % tm == 0 and nb % tn == 0 and rows % tk == 0 and b_col_off % tn == 0 and a_row_off % tk == 0

    def body(a_ref, b_ref, o_ref, acc):
        part = lax.dot_general(a_ref[...].astype(BF16), b_ref[...].astype(BF16), (((0,), (0,)), ((), ())),
                               preferred_element_type=F32)
        k = pl.program_id(2)

        @pl.when(k == 0)
        def _():
            acc[...] = part

        @pl.when(k > 0)
        def _():
            acc[...] += part

        @pl.when(k == nk - 1)
        def _():
            o_ref[...] = acc[...].astype(o_ref.dtype)

    if b.ndim == 3:
        per = b.shape[2] // tn
        b_spec = pl.BlockSpec((None, tk, tn), lambda j, i, k: (j // per, k, j % per))
    else:
        b_spec = pl.BlockSpec((tk, tn), lambda j, i, k: (k, j + b_col_off // tn))
    return pl.pallas_call(
        body, name=name, grid=(nb // tn, ka // tm, nk),
        in_specs=[pl.BlockSpec((tk, tm), lambda j, i, k: (k + a_row_off // tk, i)), b_spec],
        out_specs=pl.BlockSpec((None, tm, tn), lambda j, i, k: (0, i, j)),
        out_shape=jax.ShapeDtypeStruct((1, ka, nb), BF16),
        scratch_shapes=[pltpu.VMEM((tm, tn), F32)], compiler_params=_params(3))(a, b)


def _row_spec(tm, width, col=0):
    return pl.BlockSpec((tm, width), lambda i: (i, col))


def _gain_spec(l, width=D):
    return pl.BlockSpec((None, 1, width), lambda i: (l, 0, 0))


def norm_fwd(x, g3, l, *, name, host=None, tm=1024):
    T = x.shape[0]
    nt = T // tm
    hosting = _Hosting(host)
    n_host_in, n_host_out = len(hosting.arrays), len(hosting.out_shapes)

    def body(x_ref, g_ref, *rest):
        host_in, o_ref = rest[:n_host_in], rest[n_host_in]
        host_out, sems = rest[n_host_in + 1:n_host_in + 1 + n_host_out], rest[n_host_in + 1 + n_host_out:]
        i = pl.program_id(0)
        if hosting.plan:
            @pl.when(i == 0)
            def _():
                hosting.begin(host_in, host_out, *sems)

        xv = x_ref[...]
        r = lax.rsqrt(jnp.mean(xv * xv, axis=-1, keepdims=True) + EPS)
        o_ref[...] = (xv * r * g_ref[...]).astype(BF16)
        if hosting.plan:
            @pl.when(i == nt - 1)
            def _():
                hosting.late(host_in, host_out, *sems)
                hosting.finish(host_in, host_out, *sems)

    res = pl.pallas_call(
        body, name=name, grid=(nt,),
        in_specs=[_row_spec(tm, D), _gain_spec(l)] + [ANY] * n_host_in,
        out_specs=[_row_spec(tm, D)] + [ANY] * n_host_out,
        out_shape=[jax.ShapeDtypeStruct((T, D), BF16)] + hosting.out_shapes, scratch_shapes=hosting.scratch(),
        input_output_aliases=hosting.aliases(2, 1), compiler_params=_params(1))(x, g3, *hosting.arrays)
    return (res[0], list(res[1:])) if hosting.plan else res[0]


def _winsum_back(ext, w):
    s, span = ext, 1
    while span < w:
        s = s + pltpu.roll(s, span, 0)
        span *= 2
    return s


def _winsum_fwd(ext, w):
    rows = ext.shape[0]
    s, span = ext, 1
    while span < w:
        s = s + pltpu.roll(s, rows - span, 0)
        span *= 2
    return s


def _pooled(ext, z, t, g, w):
    sl = slice(g * PG, (g + 1) * PG)
    s = _winsum_back(ext[:, sl], w)[POOL_HALO:, :]
    return s * (1.0 / jnp.minimum(t + 1, w).astype(F32)) - z[:, sl]


def pool_fwd(z, wpool, scale3, l, *, name, tm=256):
    T = z.shape[0]
    hb = tm // POOL_HALO
    wl = l if wpool.shape[0] > 1 else 0

    def body(z_ref, zp_ref, wp_ref, sc_ref, o_ref):
        i = pl.program_id(0)
        zv = z_ref[...].astype(F32)
        prev = jnp.where(i == 0, 0.0, zp_ref[...].astype(F32))
        ext = jnp.concatenate([prev, zv], axis=0)
        t = i * tm + lax.broadcasted_iota(jnp.int32, (tm, 1), 0)
        for g, w in enumerate(POOL_WINDOWS):
            sl = slice(g * PG, (g + 1) * PG)
            pooled = _pooled(ext, zv, t, g, w)
            q = jnp.dot(pooled.astype(BF16), wp_ref[g], preferred_element_type=F32)
            o_ref[:, sl] = (q * sc_ref[:, sl]).astype(BF16)

    return pl.pallas_call(
        body, name=name, grid=(T // tm,),
        in_specs=[_row_spec(tm, D),
                  pl.BlockSpec((POOL_HALO, D), lambda i: (jnp.maximum(i * hb - 1, 0), 0)),
                  pl.BlockSpec((None, 4, PG, PG), lambda i: (wl, 0, 0, 0)),
                  _gain_spec(l)],
        out_specs=_row_spec(tm, D),
        out_shape=jax.ShapeDtypeStruct((T, D), BF16), compiler_params=_params(1))(z, z, wpool, scale3)


def sgu_fwd(z, g3, wsm, bT, l, *, name, tm=256):
    T = z.shape[0]

    def body(zu_ref, zv_ref, g_ref, ws_ref, b_ref, o_ref):
        gu = _gelu(zu_ref[...].astype(F32))
        gv = _gelu(zv_ref[...].astype(F32))
        rv = lax.rsqrt(jnp.mean(gv * gv, axis=-1, keepdims=True) + EPS)
        vn = (gv * rv * g_ref[...]).astype(BF16)
        for n in range(tm // CHUNK):
            r = slice(n * CHUNK, (n + 1) * CHUNK)
            for h in range(HEADS):
                cs = slice(h * CHUNK, (h + 1) * CHUNK)
                mixed = jnp.dot(ws_ref[h], vn[r, cs], preferred_element_type=F32) + b_ref[:, h:h + 1]
                o_ref[r, cs] = (gu[r, cs] * mixed).astype(BF16)

    return pl.pallas_call(
        body, name=name, grid=(T // tm,),
        in_specs=[_row_spec(tm, D, 1), _row_spec(tm, D, 2), _gain_spec(l),
                  pl.BlockSpec((None, HEADS, CHUNK, CHUNK), lambda i: (l, 0, 0, 0)),
                  pl.BlockSpec((None, CHUNK, HEADS), lambda i: (l, 0, 0))],
        out_specs=_row_spec(tm, D),
        out_shape=jax.ShapeDtypeStruct((T, D), BF16), compiler_params=_params(1))(z, z, g3, wsm, bT)


def gate_fwd(z, yab, *, name, tm=1024):
    T = z.shape[0]

    def body(za_ref, zb_ref, y_ref, o_ref):
        ga = jax.nn.sigmoid(za_ref[...].astype(F32))
        gb = jax.nn.sigmoid(zb_ref[...].astype(F32))
        o_ref[...] = (ga * y_ref[:, :D].astype(F32) + gb * y_ref[:, D:].astype(F32)).astype(BF16)

    return pl.pallas_call(
        body, name=name, grid=(T // tm,),
        in_specs=[_row_spec(tm, D, 3), _row_spec(tm, D, 4), _row_spec(tm, 2 * D)],
        out_specs=_row_spec(tm, D),
        out_shape=jax.ShapeDtypeStruct((T, D), BF16), compiler_params=_params(1))(z, z, yab)


def _conv(ext, w_ref, b_ref):
    down1, down2 = pltpu.roll(ext, 1, 0), pltpu.roll(ext, 2, 0)
    c = b_ref[...] + w_ref[0:1, :] * down2
    c = c + w_ref[1:2, :] * down1
    return c + w_ref[2:3, :] * ext, down1, down2


def conv_fwd(up, convw, convb3, l, *, name, tm=256):
    T = up.shape[0]
    tc = CONV_TC
    nc = DFF // tc
    hb = tm // CONV_HALO

    def body(ua_ref, uap_ref, ub_ref, ubp_ref, wa_ref, wb_ref, ba_ref, bb_ref, o_ref):
        i = pl.program_id(1)

        def conv_of(u_ref, p_ref, w_ref, b_ref):
            ext = jnp.concatenate([jnp.where(i == 0, 0.0, p_ref[...]), u_ref[...]], axis=0)
            return _conv(ext, w_ref, b_ref)[0][CONV_HALO:, :]

        ca = conv_of(ua_ref, uap_ref, wa_ref, ba_ref)
        cb = conv_of(ub_ref, ubp_ref, wb_ref, bb_ref)
        o_ref[...] = (_gelu(ca) * cb).astype(BF16)

    def cur(off):
        return pl.BlockSpec((tm, tc), lambda j, i: (i, j + off))

    def prev(off):
        return pl.BlockSpec((CONV_HALO, tc), lambda j, i: (jnp.maximum(i * hb - 1, 0), j + off))

    def wspec(off):
        return pl.BlockSpec((None, CONV_ROWS, tc), lambda j, i: (l, 0, j + off))

    def bspec(off):
        return pl.BlockSpec((None, 1, tc), lambda j, i: (l, 0, j + off))

    return pl.pallas_call(
        body, name=name, grid=(nc, T // tm),
        in_specs=[cur(0), prev(0), cur(nc), prev(nc), wspec(0), wspec(nc), bspec(0), bspec(nc)],
        out_specs=pl.BlockSpec((tm, tc), lambda j, i: (i, j)),
        out_shape=jax.ShapeDtypeStruct((T, DFF), BF16),
        compiler_params=_params(2))(up, up, up, up, convw, convw, convb3, convb3)


def ple_fwd(x2, pg, e, g3, l, *, name, tm=1024):
    T = x2.shape[0]
    has_norm = g3 is not None

    def body(x_ref, pg_ref, e_ref, *rest):
        xv = x_ref[...] + jax.nn.sigmoid(pg_ref[...].astype(F32)) * e_ref[...].astype(F32)
        if has_norm:
            g_ref, o_ref, h_ref = rest
            r = lax.rsqrt(jnp.mean(xv * xv, axis=-1, keepdims=True) + EPS)
            h_ref[...] = (xv * r * g_ref[...]).astype(BF16)
        else:
            o_ref, = rest
        o_ref[...] = xv

    x_shape = jax.ShapeDtypeStruct((T, D), F32)
    return pl.pallas_call(
        body, name=name, grid=(T // tm,),
        in_specs=[_row_spec(tm, D)] * 3 + ([_gain_spec(l)] if has_norm else []),
        out_specs=[_row_spec(tm, D)] * 2 if has_norm else _row_spec(tm, D),
        out_shape=[x_shape, jax.ShapeDtypeStruct((T, D), BF16)] if has_norm else x_shape,
        compiler_params=_params(1))(x2, pg, e, *([g3] if has_norm else []))


def loss_head(x, g3, tgt, *, name, tm=1024):
    T = x.shape[0]

    def body(x_ref, g_ref, t_ref, loss_ref, dx_ref, dg_ref):
        @pl.when(pl.program_id(0) == 0)
        def _():
            loss_ref[...] = jnp.zeros_like(loss_ref)
            dg_ref[...] = jnp.zeros_like(dg_ref)

        xv, g = x_ref[...], g_ref[...]
        r = lax.rsqrt(jnp.mean(xv * xv, axis=-1, keepdims=True) + EPS)
        xh = xv * r
        err = xh * g - t_ref[...]
        loss_ref[...] += 0.5 * jnp.sum(jnp.mean(err * err, axis=-1, keepdims=True))
        dy = err * (1.0 / D)
        dyg = dy * g
        dx_ref[...] = r * (dyg - xh * jnp.mean(dyg * xh, axis=-1, keepdims=True))
        dg_ref[0:1, :] += jnp.sum(dy * xh, axis=0, keepdims=True)

    return pl.pallas_call(
        body, name=name, grid=(T // tm,),
        in_specs=[_row_spec(tm, D), pl.BlockSpec((1, D), lambda i: (0, 0)), _row_spec(tm, D)],
        out_specs=[pl.BlockSpec((8, LANES), lambda i: (0, 0)), _row_spec(tm, D),
                   pl.BlockSpec((8, D), lambda i: (0, 0))],
        out_shape=[jax.ShapeDtypeStruct((8, LANES), F32), jax.ShapeDtypeStruct((T, D), F32),
                   jax.ShapeDtypeStruct((8, D), F32)],
        compiler_params=_params(1))(x, g3, tgt)


def ple_bwd(dx, pg, e, *, name, tm=1024):
    T = dx.shape[0]

    def body(dx_ref, pg_ref, e_ref, de_ref, dpg_ref):
        gate = jax.nn.sigmoid(pg_ref[...].astype(F32))
        dxv = dx_ref[...]
        de_ref[...] = (dxv * gate).astype(BF16)
        dpg_ref[...] = (dxv * e_ref[...].astype(F32) * gate * (1.0 - gate)).astype(BF16)

    return pl.pallas_call(
        body, name=name, grid=(T // tm,),
        in_specs=[_row_spec(tm, D)] * 3, out_specs=[_row_spec(tm, D)] * 2,
        out_shape=[jax.ShapeDtypeStruct((T, D), BF16)] * 2, compiler_params=_params(1))(dx, pg, e)


def conv_bwd(df, up, convw, convb3, l, *, name, host=None, tm=256):
    T = up.shape[0]
    tc = CONV_TC
    nc = DFF // tc
    hb = tm // CONV_HALO
    nt = T // tm
    rows = tm + 2 * CONV_HALO
    own = slice(CONV_HALO, CONV_HALO + tm)

    hosting = _Hosting(host)
    n_host_in, n_host_out = len(hosting.arrays), len(hosting.out_shapes)

    def body(df_ref, dfn_ref, ua_ref, uap_ref, uan_ref, ub_ref, ubp_ref, ubn_ref, wa_ref, wb_ref, ba_ref, bb_ref,
             *rest):
        host_in = rest[:n_host_in]
        dup_ref, dcw_ref, dcb_ref = rest[n_host_in:n_host_in + 3]
        host_out = rest[n_host_in + 3:n_host_in + 3 + n_host_out]
        sems = rest[n_host_in + 3 + n_host_out:]
        i = pl.program_id(1)
        if hosting.plan:
            first, last = _first_last((nc, nt))

            @pl.when(first)
            def _():
                hosting.begin(host_in, host_out, *sems)

        @pl.when(i == 0)
        def _():
            dcw_ref[...] = jnp.zeros_like(dcw_ref)
            dcb_ref[...] = jnp.zeros_like(dcb_ref)

        def ext_of(c_ref, p_ref, n_ref):
            return jnp.concatenate([jnp.where(i == 0, 0.0, p_ref[...]), c_ref[...],
                                    jnp.where(i == nt - 1, 0.0, n_ref[...])], axis=0)

        ea = ext_of(ua_ref, uap_ref, uan_ref)
        eb = ext_of(ub_ref, ubp_ref, ubn_ref)
        ca, ea1, ea2 = _conv(ea, wa_ref, ba_ref)
        cb, eb1, eb2 = _conv(eb, wb_ref, bb_ref)
        df_ext =jnp.concatenate([jnp.zeros((CONV_HALO, tc), F32), df_ref[...],
                                  jnp.where(i == nt - 1, 0.0, dfn_ref[...])], axis=0)
        cdf = 0.5 * (1.0 + lax.erf(ca * 0.7071067811865476))
        da = df_ext * cb * (cdf + ca * jnp.exp(-0.5 * ca * ca) * 0.3989422804014327)
        db = df_ext * (ca * cdf)

        def finish(h, dc, e, e1, e2, w_ref):
            dup = w_ref[2:3, :] * dc + w_ref[1:2, :] * pltpu.roll(dc, rows - 1, 0)
            dup = dup + w_ref[0:1, :] * pltpu.roll(dc, rows - 2, 0)
            dup_ref[h] = dup[own, :].astype(BF16)
            dco = dc[own, :]
            dcb_ref[h, 0:1, :] += jnp.sum(dco, axis=0, keepdims=True)
            dcw_ref[h, 0:1, :] += jnp.sum(dco * e2[own, :], axis=0, keepdims=True)
            dcw_ref[h, 1:2, :] += jnp.sum(dco * e1[own, :], axis=0, keepdims=True)
            dcw_ref[h, 2:3, :] += jnp.sum(dco * e[own, :], axis=0, keepdims=True)

        finish(0, da, ea, ea1, ea2, wa_ref)
        finish(1, db, eb, eb1, eb2, wb_ref)
        if hosting.plan:
            @pl.when(last)
            def _():
                hosting.late(host_in, host_out, *sems)
                hosting.finish(host_in, host_out, *sems)

    def nxt(i):
        return jnp.minimum((i + 1) * hb, T // CONV_HALO - 1)

    def prv(i):
        return jnp.maximum(i * hb - 1, 0)

    def up_specs(off):
        return [pl.BlockSpec((tm, tc), lambda j, i: (i, j + off)),
                pl.BlockSpec((CONV_HALO, tc), lambda j, i: (prv(i), j + off)),
                pl.BlockSpec((CONV_HALO, tc), lambda j, i: (nxt(i), j + off))]

    in_specs = [pl.BlockSpec((tm, tc), lambda j, i: (i, j)),
                pl.BlockSpec((CONV_HALO, tc), lambda j, i: (nxt(i), j)),
                *up_specs(0), *up_specs(nc),
                pl.BlockSpec((None, CONV_ROWS, tc), lambda j, i: (l, 0, j)),
                pl.BlockSpec((None, CONV_ROWS, tc), lambda j, i: (l, 0, j + nc)),
                pl.BlockSpec((None, 1, tc), lambda j, i: (l, 0, j)),
                pl.BlockSpec((None, 1, tc), lambda j, i: (l, 0, j + nc))]
    res = pl.pallas_call(
        body, name=name, grid=(nc, nt), in_specs=in_specs + [ANY] * n_host_in,
        out_specs=[pl.BlockSpec((2, tm, tc), lambda j, i: (0, i, j)),
                   pl.BlockSpec((2, 8, tc), lambda j, i: (0, 0, j)),
                   pl.BlockSpec((2, 8, tc), lambda j, i: (0, 0, j))] + [ANY] * n_host_out,
        out_shape=[jax.ShapeDtypeStruct((2, T, DFF), BF16), jax.ShapeDtypeStruct((2, 8, DFF), F32),
                   jax.ShapeDtypeStruct((2, 8, DFF), F32)] + hosting.out_shapes,
        scratch_shapes=hosting.scratch(), input_output_aliases=hosting.aliases(12, 3),
        compiler_params=_params(2))(df, df, up, up, up, up, up, up, convw, convw, convb3, convb3, *hosting.arrays)
    return res[0], res[1], res[2], list(res[3:])


def gate_bwd(dmo, z, yab, *, name, tm=1024):
    T = z.shape[0]

    def body(dmo_ref, zg_ref, y_ref, dz_ref, dy_ref):
        g = jax.nn.sigmoid(zg_ref[...].astype(F32))
        dmo_v = dmo_ref[...].astype(F32)
        dy_ref[...] = (dmo_v * g).astype(BF16)
        dz_ref[...] = (dmo_v * y_ref[...].astype(F32) * g * (1.0 - g)).astype(BF16)

    return pl.pallas_call(
        body, name=name, grid=(T // tm, 2),
        in_specs=[pl.BlockSpec((tm, D), lambda i, s: (i, 0)),
                  pl.BlockSpec((tm, D), lambda i, s: (i, 3 + s)),
                  pl.BlockSpec((tm, D), lambda i, s: (i, s))],
        out_specs=[pl.BlockSpec((tm, D), lambda i, s: (i, 3 + s)),
                   pl.BlockSpec((tm, D), lambda i, s: (i, s))],
        out_shape=[jax.ShapeDtypeStruct((T, 5 * D), BF16), jax.ShapeDtypeStruct((T, 2 * D), BF16)],
        compiler_params=_params(2))(dmo, z, yab)


def mixer_bwd(da, ds, z, dz, wpool, scale3, g3, wsm, wsmT, bT, l, *, name, tm=256):
    T = z.shape[0]
    hb = tm // POOL_HALO
    nt = T // tm

    def body(da_ref, dan_ref, ds_ref, zp_ref, zpp_ref, zu_ref, zv_ref, wp_ref, sc_ref, g_ref, ws_ref, wst_ref,
             b_ref, dzin_ref, dz_ref, dwp_ref, dsc_ref, dws_ref, dbt_ref, dgs_ref, mixed_scr, dvn_scr, db_scr):
        del dzin_ref
        i = pl.program_id(0)

        @pl.when(i == 0)
        def _():
            dwp_ref[...] = jnp.zeros_like(dwp_ref)
            dsc_ref[...] = jnp.zeros_like(dsc_ref)
            dws_ref[...] = jnp.zeros_like(dws_ref)
            dgs_ref[...] = jnp.zeros_like(dgs_ref)
            db_scr[...] = jnp.zeros_like(db_scr)

        zv_p = zp_ref[...].astype(F32)
        ext = jnp.concatenate([jnp.where(i == 0, 0.0, zpp_ref[...].astype(F32)), zv_p], axis=0)
        da_v = da_ref[...].astype(F32)
        da_ext = jnp.concatenate([da_v, jnp.where(i == nt - 1, 0.0, dan_ref[...].astype(F32))], axis=0)
        t = i * tm + lax.broadcasted_iota(jnp.int32, (tm, 1), 0)
        t_ext = i * tm + lax.broadcasted_iota(jnp.int32, (tm + POOL_HALO, 1), 0)
        for g, w in enumerate(POOL_WINDOWS):
            sl = slice(g * PG, (g + 1) * PG)
            pooled = _pooled(ext, zv_p, t, g, w).astype(BF16)
            q = jnp.dot(pooled, wp_ref[g], preferred_element_type=F32)
            dsc_ref[0:1, sl] += jnp.sum(da_v[:, sl] * q, axis=0, keepdims=True)
            dq_ext = (da_ext[:, sl] * sc_ref[:, sl]).astype(BF16)
            dwp_ref[g] += lax.dot_general(pooled, dq_ext[:tm, :], (((0,), (0,)), ((), ())),
                                          preferred_element_type=F32)
            dpool = lax.dot_general(dq_ext, wp_ref[g], (((1,), (1,)), ((), ())), preferred_element_type=F32)
            spread = _winsum_fwd(dpool * (1.0 / jnp.minimum(t_ext + 1, w).astype(F32)), w)
            dz_ref[:, sl] = (spread[:tm, :] - dpool[:tm, :]).astype(BF16)

        zu, zv, ds_v = zu_ref[...].astype(F32), zv_ref[...].astype(F32), ds_ref[...].astype(F32)
        gain = g_ref[...]
        gu, gv = _gelu(zu), _gelu(zv)
        rv = lax.rsqrt(jnp.mean(gv * gv, axis=-1, keepdims=True) + EPS)
        vh = gv * rv
        vn = (vh * gain).astype(BF16)
        dmix = ds_v * gu
        dmix_b = dmix.astype(BF16)
        for n in range(tm // CHUNK):
            r = slice(n * CHUNK, (n + 1) * CHUNK)
            db_scr[...] += dmix[r, :]
            for h in range(HEADS):
                cs = slice(h * CHUNK, (h + 1) * CHUNK)
                mixed_scr[r, cs] = jnp.dot(ws_ref[h], vn[r, cs], preferred_element_type=F32) + b_ref[:, h:h + 1]
                dws_ref[h] += lax.dot_general(dmix_b[r, cs], vn[r, cs], (((1,), (1,)), ((), ())),
                                              preferred_element_type=F32)
                dvn_scr[r, cs] = jnp.dot(wst_ref[h], dmix_b[r, cs], preferred_element_type=F32)
        dz_ref[:, D:2 * D] = (ds_v * mixed_scr[...] * _gelu_grad(zu)).astype(BF16)
        dvn = dvn_scr[...]
        dgs_ref[0:1, :] += jnp.sum(dvn * vh, axis=0, keepdims=True)
        dvg = dvn * gain
        dgv = rv * (dvg - vh * jnp.mean(dvg * vh, axis=-1, keepdims=True))
        dz_ref[:, 2 * D:3 * D] = (dgv * _gelu_grad(zv)).astype(BF16)

        @pl.when(i == nt - 1)
        def _():
            tril = (lax.broadcasted_iota(jnp.int32, (CHUNK, CHUNK), 0)
                    >= lax.broadcasted_iota(jnp.int32, (CHUNK, CHUNK), 1)).astype(F32)
            for h in range(HEADS):
                dws_ref[h] = dws_ref[h] * tril
                dbt_ref[:, h:h + 1] = jnp.sum(db_scr[:, h * CHUNK:(h + 1) * CHUNK], axis=1, keepdims=True)

    const4 = lambda i: (l, 0, 0, 0)
    wl = l if wpool.shape[0] > 1 else 0
    in_specs = [
        _row_spec(tm, D),
        pl.BlockSpec((POOL_HALO, D), lambda i: (jnp.minimum((i + 1) * hb, T // POOL_HALO - 1), 0)),
        _row_spec(tm, D),
        _row_spec(tm, D, 0),
        pl.BlockSpec((POOL_HALO, D), lambda i: (jnp.maximum(i * hb - 1, 0), 0)),
        _row_spec(tm, D, 1), _row_spec(tm, D, 2),
        pl.BlockSpec((None, 4, PG, PG), lambda i: (wl, 0, 0, 0)),
        _gain_spec(l), _gain_spec(l),
        pl.BlockSpec((None, HEADS, CHUNK, CHUNK), const4),
        pl.BlockSpec((None, HEADS, CHUNK, CHUNK), const4),
        pl.BlockSpec((None, CHUNK, HEADS), lambda i: (l, 0, 0)),
        ANY,
    ]
    out_specs = [
        pl.BlockSpec((tm, 3 * D), lambda i: (i, 0)),
        pl.BlockSpec((4, PG, PG), lambda i: (0, 0, 0)),
        pl.BlockSpec((8, D), lambda i: (0, 0)),
        pl.BlockSpec((HEADS, CHUNK, CHUNK), lambda i: (0, 0, 0)),
        pl.BlockSpec((CHUNK, HEADS), lambda i: (0, 0)),
        pl.BlockSpec((8, D), lambda i: (0, 0)),
    ]
    out_shape = [
        jax.ShapeDtypeStruct((T, 5 * D), BF16), jax.ShapeDtypeStruct((4, PG, PG), F32),
        jax.ShapeDtypeStruct((8, D), F32), jax.ShapeDtypeStruct((HEADS, CHUNK, CHUNK), F32),
        jax.ShapeDtypeStruct((CHUNK, HEADS), F32), jax.ShapeDtypeStruct((8, D), F32),
    ]
    return pl.pallas_call(
        body, name=name, grid=(nt,), in_specs=in_specs, out_specs=out_specs, out_shape=out_shape,
        scratch_shapes=[pltpu.VMEM((tm, D), F32), pltpu.VMEM((tm, D), F32), pltpu.VMEM((CHUNK, D), F32)],
        input_output_aliases={13: 0}, compiler_params=_params(1))(
            da, da, ds, z, z, z, z, wpool, scale3, g3, wsm, wsmT, bT, dz)


def _row_tile(rows, cols, sub):
    cap = max(sub, (2 * 1024 * 1024) // (4 * cols))
    best = None
    for tr in range(sub, min(rows, cap) + 1, sub):
        if rows % tr == 0:
            best = tr
    return best or rows


def elementwise(fn, ins, out_dtypes, *, name, row_blk_offs=None, rows=None):
    cols = ins[0].shape[1]
    rows = rows or ins[0].shape[0]
    tr = _row_tile(rows, cols, 16)
    offs = row_blk_offs or [0] * len(ins)
    n_in = len(ins)

    def body(*refs):
        outs = fn(*[r[...] for r in refs[:n_in]])
        for o_ref, o in zip(refs[n_in:], outs):
            o_ref[...] = o.astype(o_ref.dtype)

    return pl.pallas_call(
        body, name=name, grid=(rows // tr,),
        in_specs=[pl.BlockSpec((tr, cols), functools.partial(lambda i, o: (i + o * (rows // tr), 0), o=o))
                  for o in offs],
        out_specs=[pl.BlockSpec((tr, cols), lambda i: (i, 0)) for _ in out_dtypes],
        out_shape=[jax.ShapeDtypeStruct((rows, cols), dt) for dt in out_dtypes],
        compiler_params=_params(1))(*ins)


def _adamw(w, g, m, v):
    m = ADAM_B1 * m + (1.0 - ADAM_B1) * g
    v = ADAM_B2 * v + (1.0 - ADAM_B2) * jnp.square(g)
    m_hat = m / (1.0 - ADAM_B1 ** ADAM_STEP)
    v_hat = v / (1.0 - ADAM_B2 ** ADAM_STEP)
    delta = -ADAM_LR * (m_hat / (jnp.sqrt(v_hat) + ADAM_EPS) + ADAM_WD * w)
    return delta, m, v


def _view2d(a):
    return a.reshape(-1, a.shape[-1])


def _place():
    x, y, c = lax.axis_index("x"), lax.axis_index("y"), lax.axis_index("c")
    others = [(1 - x, y), (x, 1 - y), (1 - x, 1 - y)]
    return x, y, c, 2 * x + y, others


def _remote(src, dst, send_sems, recv_sems, k, to):
    return pltpu.make_async_remote_copy(src_ref=src, dst_ref=dst, send_sem=send_sems.at[k], recv_sem=recv_sems.at[k],
                                        device_id=to, device_id_type=MESH)


def _half(ref, axis, j, size, h):
    if len(ref.shape) == 3:
        return ref.at[:, pl.ds(j * size + h * (size // 2), size // 2), :]
    if axis == 0:
        return ref.at[pl.ds(j * size + h * (size // 2), size // 2), :]
    rows = ref.shape[0] // 2
    return ref.at[pl.ds(h * rows, rows), pl.ds(j * size, size)]


def _half_shard_shape(shape, axis, size):
    if len(shape) == 3:
        return (shape[0], size // 2, shape[2])
    if axis == 0:
        return (size // 2, shape[1])
    return (shape[0] // 2, size)


class Exchange:
    def __init__(self, arrays, out_shapes, aliases, n_sems, begin, finish, late=None):
        self.arrays, self.out_shapes, self.aliases, self.n_sems = list(arrays), list(out_shapes), aliases, n_sems
        self.begin, self.finish, self.late = begin, finish, late


class _Hosting:
    def __init__(self, plan):
        self.plan = list(plan or [])
        self.arrays = [a for ex in self.plan for a in ex.arrays]
        self.out_shapes = [o for ex in self.plan for o in ex.out_shapes]
        self.n_sems = sum(ex.n_sems for ex in self.plan)

    def scratch(self):
        return [pltpu.SemaphoreType.DMA((self.n_sems,)), pltpu.SemaphoreType.DMA((self.n_sems,))] if self.plan else []

    def aliases(self, in_base, out_base):
        out, i0, o0 = {}, in_base, out_base
        for ex in self.plan:
            out.update({i0 + i: o0 + o for i, o in ex.aliases.items()})
            i0, o0 = i0 + len(ex.arrays), o0 + len(ex.out_shapes)
        return out

    def _each(self, in_refs, out_refs):
        i0 = o0 = s0 = 0
        for ex in self.plan:
            yield ex, in_refs[i0:i0 + len(ex.arrays)], out_refs[o0:o0 + len(ex.out_shapes)], s0
            i0, o0, s0 = i0 + len(ex.arrays), o0 + len(ex.out_shapes), s0 + ex.n_sems

    def begin(self, in_refs, out_refs, send_sems, recv_sems):
        for ex, ins, outs, s0 in self._each(in_refs, out_refs):
            ex.begin(ins, outs, send_sems, recv_sems, s0)

    def late(self, in_refs, out_refs, send_sems, recv_sems):
        for ex, ins, outs, s0 in self._each(in_refs, out_refs):
            if ex.late is not None:
                ex.late(ins, outs, send_sems, recv_sems, s0)

    def finish(self, in_refs, out_refs, send_sems, recv_sems):
        for ex, ins, outs, s0 in self._each(in_refs, out_refs):
            ex.finish(ins, outs, send_sems, recv_sems, s0)


def _first_last(grid):
    ids = [pl.program_id(a) for a in range(len(grid))]
    first = functools.reduce(jnp.logical_and, [i == 0 for i in ids])
    last = functools.reduce(jnp.logical_and, [i == g - 1 for i, g in zip(ids, grid)])
    return first, last


def run_exchanges(plan, *, name):
    host = _Hosting(plan)
    n_in, n_out = len(host.arrays), len(host.out_shapes)

    def body(*refs):
        ins, outs = refs[:n_in], refs[n_in:n_in + n_out]
        send_sems, recv_sems = refs[n_in + n_out:]
        host.begin(ins, outs, send_sems, recv_sems)
        host.late(ins, outs, send_sems, recv_sems)
        host.finish(ins, outs, send_sems, recv_sems)

    return pl.pallas_call(
        body, name=name, in_specs=[ANY] * n_in, out_specs=[ANY] * n_out, out_shape=host.out_shapes,
        scratch_shapes=host.scratch(), input_output_aliases=host.aliases(0, 0),
        compiler_params=pltpu.CompilerParams(has_side_effects=True))(*host.arrays)


def place_shard(src, l, axis, size, out_dtype, place, *, name):
    shard = src.shape[1:]
    natural = tuple(size * N_CHIPS if a == axis else s for a, s in enumerate(shard))
    if len(shard) == 3:
        blk = (None,) + shard
        grid = (1,)
        in_map = lambda i, pr: (l, 0, 0, 0)
        out_map = lambda i, pr: (0, 0, pr[0], 0)
    else:
        tr = _row_tile(shard[0], shard[1], 16)
        steps = shard[0] // tr
        blk = (None, tr, shard[1])
        grid = (steps,)
        in_map = lambda i, pr: (l, i, 0)
        if axis == 0:
            out_map = lambda i, pr: (0, pr[0] * steps + i, 0)
        else:
            out_map = lambda i, pr: (0, i, pr[0])

    def body(pr_ref, s_ref, o_ref):
        del pr_ref
        o_ref[...] = s_ref[...].astype(o_ref.dtype)

    return pl.pallas_call(
        body, name=name,
        grid_spec=pltpu.PrefetchScalarGridSpec(
            num_scalar_prefetch=1, grid=grid, in_specs=[pl.BlockSpec(blk, in_map)],
            out_specs=pl.BlockSpec(blk, out_map)),
        out_shape=jax.ShapeDtypeStruct((1,) + natural, out_dtype), compiler_params=_params(1))(place, src)


def place_both_layers(src, axis, size, place, *, name):
    rows, cols = src.shape[1], src.shape[2]

    def body(pr_ref, s_ref, o_ref):
        del pr_ref
        o_ref[...] = s_ref[...]

    return pl.pallas_call(
        body, name=name,
        grid_spec=pltpu.PrefetchScalarGridSpec(
            num_scalar_prefetch=1, grid=(2,), in_specs=[pl.BlockSpec((None, rows, cols), lambda lyr, pr: (lyr, 0, 0))],
            out_specs=pl.BlockSpec((None, rows, cols), lambda lyr, pr: (lyr, 0, pr[0]))),
        out_shape=jax.ShapeDtypeStruct((2, rows, cols * N_CHIPS), src.dtype), compiler_params=_params(1))(place, src)


def gather_exchange(arrays, geom):
    n = len(arrays)

    def begin(ins, outs, send_sems, recv_sems, s0):
        x, y, c, j, others = _place()
        for t, (axis, size) in enumerate(geom):
            mine = _half(outs[t].at[0], axis, j, size, c)
            for k, (ox, oy) in enumerate(others):
                _remote(mine, mine, send_sems, recv_sems, s0 + 6 * t + k, (ox, oy, c)).start()

    def passes_on(outs, send_sems, recv_sems, s0, c, sib, others):
        return [_remote(landed, landed, send_sems, recv_sems, s0 + 6 * t + 3 + k, sib)
                for t, (axis, size) in enumerate(geom) for k, (ox, oy) in enumerate(others)
                for landed in [_half(outs[t].at[0], axis, 2 * ox + oy, size, c)]]

    def late(ins, outs, send_sems, recv_sems, s0):
        x, y, c, j, others = _place()
        for t, (axis, size) in enumerate(geom):
            for k, (ox, oy) in enumerate(others):
                landed = _half(outs[t].at[0], axis, 2 * ox + oy, size, c)
                _remote(landed, landed, send_sems, recv_sems, s0 + 6 * t + k, (ox, oy, c)).wait_recv()
        for fwd in passes_on(outs, send_sems, recv_sems, s0, c, (x, y, 1 - c), others):
            fwd.start()

    def finish(ins, outs, send_sems, recv_sems, s0):
        x, y, c, j, others = _place()
        sib = (x, y, 1 - c)
        for t, (axis, size) in enumerate(geom):
            for k, (ox, oy) in enumerate(others):
                got = _half(outs[t].at[0], axis, 2 * ox + oy, size, 1 - c)
                _remote(got, got, send_sems, recv_sems, s0 + 6 * t + 3 + k, sib).wait_recv()
        for fwd in passes_on(outs, send_sems, recv_sems, s0, c, sib, others):
            fwd.wait_send()
        for t, (axis, size) in enumerate(geom):
            mine = _half(outs[t].at[0], axis, j, size, c)
            for k, (ox, oy) in enumerate(others):
                _remote(mine, mine, send_sems, recv_sems, s0 + 6 * t + k, (ox, oy, c)).wait_send()

    return Exchange(arrays, [jax.ShapeDtypeStruct(a.shape, a.dtype) for a in arrays], {t: t for t in range(n)},
                    6 * n, begin, finish, late)


def gather_by_layer_exchange(array, axis, size):
    def blocks(out, others, lyr):
        return [_block(out.at[lyr], axis, 2 * ox + oy, size) for (ox, oy) in others]

    def begin(ins, outs, send_sems, recv_sems, s0):
        x, y, c, j, others = _place()
        mine = _block(outs[0].at[c], axis, j, size)
        for k, (ox, oy) in enumerate(others):
            _remote(mine, mine, send_sems, recv_sems, s0 + k, (ox, oy, c)).start()

    def finish(ins, outs, send_sems, recv_sems, s0):
        x, y, c, j, others = _place()
        sib = (x, y, 1 - c)
        passed = []
        for k, ((ox, oy), landed) in enumerate(zip(others, blocks(outs[0], others, c))):
            _remote(landed, landed, send_sems, recv_sems, s0 + k, (ox, oy, c)).wait_recv()
            fwd = _remote(landed, landed, send_sems, recv_sems, s0 + 3 + k, sib)
            fwd.start()
            passed.append(fwd)
        for k, got in enumerate(blocks(outs[0], others, 1 - c)):
            _remote(got, got, send_sems, recv_sems, s0 + 3 + k, sib).wait_recv()
        for fwd in passed:
            fwd.wait_send()
        mine = _block(outs[0].at[c], axis, j, size)
        for k, (ox, oy) in enumerate(others):
            _remote(mine, mine, send_sems, recv_sems, s0 + k, (ox, oy, c)).wait_send()

    return Exchange([array], [jax.ShapeDtypeStruct(array.shape, array.dtype)], {0: 0}, 6, begin, finish)


def swap_exchange(grads, geom):
    def pieces(t, g, dst, h):
        axis, size = geom[t]
        if len(g.shape) == 2 and axis == 1:
            rows = g.shape[0] // 2
            return [(g.at[pl.ds(h * rows, rows), :], dst)]
        return [(_half(g, axis, jb, size, h), dst.at[jb]) for jb in range(N_CHIPS)]

    counts = [1 if (len(g.shape) == 3 and a == 1) else N_CHIPS for g, (a, _) in zip(grads, geom)]
    bases = [sum(counts[:t]) for t in range(len(grads))]

    def copies(ins, outs, send_sems, recv_sems, s0):
        x, y, c, _, _ = _place()
        cps = []
        for t in range(len(grads)):
            for q, (src, dst) in enumerate(pieces(t, ins[t].at[0], outs[t], 1 - c)):
                cps.append(_remote(src, dst, send_sems, recv_sems, s0 + bases[t] + q, (x, y, 1 - c)))
        return cps

    def begin(*a):
        for cp in copies(*a):
            cp.start()

    def finish(*a):
        for cp in copies(*a):
            cp.wait()

    out_shapes = []
    for g, (axis, size) in zip(grads, geom):
        shp = g.shape[1:]
        if len(shp) == 2 and axis == 1:
            out_shapes.append(jax.ShapeDtypeStruct((shp[0] // 2, shp[1]), g.dtype))
        else:
            out_shapes.append(jax.ShapeDtypeStruct((N_CHIPS,) + _half_shard_shape(shp, axis, size), g.dtype))
    return Exchange(grads, out_shapes, {}, sum(counts), begin, finish)


def scatter_exchange(parts, geom, shapes):
    def copies(ins, outs, send_sems, recv_sems, s0):
        x, y, c, j, others = _place()
        cps = []
        for t, ((axis, size), shp) in enumerate(zip(geom, shapes)):
            for k, (ox, oy) in enumerate(others):
                jp = 2 * ox + oy
                src = ins[t].at[:, pl.ds(jp * size, size)] if (len(shp) == 2 and axis == 1) else ins[t].at[jp]
                cps.append(_remote(src, outs[t].at[k], send_sems, recv_sems, s0 + 3 * t + k, (ox, oy, c)))
        return cps

    def begin(*a):
        for cp in copies(*a):
            cp.start()

    def finish(*a):
        for cp in copies(*a):
            cp.wait_recv()
        for cp in copies(*a):
            cp.wait_send()

    out_shapes = [jax.ShapeDtypeStruct((3,) + _half_shard_shape(shp, axis, size), p.dtype)
                  for p, (axis, size), shp in zip(parts, geom, shapes)]
    return Exchange(parts, out_shapes, {}, 3 * len(parts), begin, finish)


def share_exchange(grads, which):
    n = len(which)

    def my_half(refs, t, h):
        lyr = refs[which[t][0]].at[which[t][1]]
        if len(lyr.shape) == 3:
            rows = lyr.shape[1] // 2
            return lyr.at[:, pl.ds(h * rows, rows), :]
        rows = lyr.shape[0] // 2
        return lyr.at[pl.ds(h * rows, rows), :]

    def begin(ins, outs, send_sems, recv_sems, s0):
        x, y, c, _, _ = _place()
        for t in range(n):
            mine = my_half(outs, t, c)
            _remote(mine, mine, send_sems, recv_sems, s0 + t, (x, y, 1 - c)).start()

    def finish(ins, outs, send_sems, recv_sems, s0):
        x, y, c, _, _ = _place()
        for t in range(n):
            got = my_half(outs, t, 1 - c)
            _remote(got, got, send_sems, recv_sems, s0 + t, (x, y, 1 - c)).wait_recv()
        for t in range(n):
            mine = my_half(outs, t, c)
            _remote(mine, mine, send_sems, recv_sems, s0 + t, (x, y, 1 - c)).wait_send()

    return Exchange(grads, [jax.ShapeDtypeStruct(g.shape, g.dtype) for g in grads],
                    {t: t for t in range(len(grads))}, n, begin, finish)


def all_reduce_small(s):
    rows = s.shape[0]
    half = rows // 2
    assert half % 8 == 0

    def body(s_ref, o_ref, a_ref, b_ref, p_ref, send_sems, recv_sems):
        x, y, c, j, others = _place()
        sib = (x, y, 1 - c)
        swap = _remote(s_ref, a_ref, send_sems, recv_sems, 0, sib)
        swap.start()
        swap.wait()
        p_ref[...] = s_ref[...] + a_ref[...]
        mine = pl.ds(pl.multiple_of(c * half, 8), half)
        b_ref[j] = p_ref[mine, :]
        cps = [_remote(p_ref.at[mine, :], b_ref.at[j], send_sems, recv_sems, 1 + k, (ox, oy, c))
               for k, (ox, oy) in enumerate(others)]
        for cp in cps:
            cp.start()
        for k, (ox, oy) in enumerate(others):
            slot = b_ref.at[2 * ox + oy]
            _remote(slot, slot, send_sems, recv_sems, 1 + k, (ox, oy, c)).wait_recv()
        for cp in cps:
            cp.wait_send()
        o_ref[mine, :] = ((b_ref[0] + b_ref[1]) + b_ref[2]) + b_ref[3]
        back = _remote(o_ref.at[mine, :], o_ref.at[mine, :], send_sems, recv_sems, 4, sib)
        back.start()
        back.wait_send()
        theirs = pl.ds(pl.multiple_of((1 - c) * half, 8), half)
        _remote(o_ref.at[theirs, :], o_ref.at[theirs, :], send_sems, recv_sems, 4, sib).wait_recv()

    vmem = pl.BlockSpec(memory_space=pltpu.VMEM)
    return pl.pallas_call(
        body, name="all_reduce_small", in_specs=[vmem], out_specs=vmem,
        out_shape=jax.ShapeDtypeStruct((rows, LANES), F32),
        scratch_shapes=[pltpu.VMEM((rows, LANES), F32), pltpu.VMEM((N_CHIPS, half, LANES), F32),
                        pltpu.VMEM((rows, LANES), F32), pltpu.SemaphoreType.DMA((5,)),
                        pltpu.SemaphoreType.DMA((5,))],
        compiler_params=pltpu.CompilerParams(vmem_limit_bytes=VMEM_LIMIT, has_side_effects=True))(s)


def pair_sum(g, got, axis, size, place, *, name):
    shp = g.shape[1:]
    if len(shp) == 3:
        hs = size // 2
        grid = (N_CHIPS,)
        g_spec = pl.BlockSpec((None, shp[0], hs, shp[2]), lambda jb, pr: (0, 0, 2 * jb + pr[1], 0))
        r_spec = pl.BlockSpec((None, shp[0], hs, shp[2]), lambda jb, pr: (jb, 0, 0, 0))
    elif axis == 0:
        hs = size // 2
        tr = _row_tile(hs, shp[1], 16)
        steps = hs // tr
        grid = (N_CHIPS, steps)
        g_spec = pl.BlockSpec((None, tr, shp[1]), lambda jb, i, pr: (0, (2 * jb + pr[1]) * steps + i, 0))
        r_spec = pl.BlockSpec((None, tr, shp[1]), lambda jb, i, pr: (jb, i, 0))
    else:
        rows = shp[0] // 2
        tr = _row_tile(rows, shp[1], 16)
        steps = rows // tr
        grid = (steps,)
        g_spec = pl.BlockSpec((None, tr, shp[1]), lambda i, pr: (0, pr[1] * steps + i, 0))
        r_spec = pl.BlockSpec((tr, shp[1]), lambda i, pr: (i, 0))

    def body(pr_ref, g_ref, r_ref, o_ref):
        del pr_ref
        o_ref[...] = (g_ref[...].astype(F32) + r_ref[...].astype(F32)).astype(BF16)

    return pl.pallas_call(
        body, name=name,
        grid_spec=pltpu.PrefetchScalarGridSpec(num_scalar_prefetch=1, grid=grid, in_specs=[g_spec, r_spec],
                                               out_specs=r_spec),
        out_shape=jax.ShapeDtypeStruct(got.shape, BF16), compiler_params=_params(len(grid)))(place, g, got)


def chip_sum(part, slots, shp, axis, size, l, place, out, *, name):
    shard = _shard_shape(shp, axis, size)
    hshape = slots.shape[1:]
    if len(shp) == 3:
        grid = (1,)
        p_spec = pl.BlockSpec((None,) + hshape, lambda i, pr: (pr[0], 0, 0, 0))
        s_specs = [pl.BlockSpec((None,) + hshape, functools.partial(lambda i, pr, k: (k, 0, 0, 0), k=k))
                   for k in range(3)]
        o_spec = pl.BlockSpec((None,) + hshape, lambda i, pr: (l, 0, pr[1], 0))
    else:
        tr = _row_tile(hshape[0], hshape[1], 16)
        steps = hshape[0] // tr
        grid = (steps,)
        if axis == 0:
            p_spec = pl.BlockSpec((None, tr, hshape[1]), lambda i, pr: (pr[0], i, 0))
        else:
            p_spec = pl.BlockSpec((tr, hshape[1]), lambda i, pr: (i, pr[0]))
        s_specs = [pl.BlockSpec((None, tr, hshape[1]), functools.partial(lambda i, pr, k: (k, i, 0), k=k))
                   for k in range(3)]
        o_spec = pl.BlockSpec((None, tr, hshape[1]), lambda i, pr: (l, pr[1] * steps + i, 0))
    has_out = out is not None

    def body(pr_ref, p_ref, s0_ref, s1_ref, s2_ref, *rest):
        del pr_ref
        rest[-1][...] = ((p_ref[...].astype(F32) + s0_ref[...].astype(F32)) + s1_ref[...].astype(F32)) \
            + s2_ref[...].astype(F32)

    return pl.pallas_call(
        body, name=name,
        grid_spec=pltpu.PrefetchScalarGridSpec(
            num_scalar_prefetch=1, grid=grid, in_specs=[p_spec] + s_specs + ([ANY] if has_out else []),
            out_specs=o_spec),
        out_shape=jax.ShapeDtypeStruct((2,) + shard, F32), input_output_aliases={5: 0} if has_out else {},
        compiler_params=_params(1))(place, part, slots, slots, slots, *([out] if has_out else []))


GEOM = {name: (axis, size) for (name, _, axis, size) in BIG}
SHAPE = {name: shape for (name, shape, _, _) in BIG}
RIDES_IN_PROJ_L0 = ((0, ("w_pool", "w_branch_a", "w_branch_b", "w_out", "w_up")),)
RIDES_UP_PROJ_L0 = ((0, ("w_down", "w_ple_gate", "w_ple")), (1, ("w_in",)))
RIDES_DOWN_PROJ_L0 = ((1, ("w_pool", "w_branch_a", "w_branch_b", "w_out")),)
RIDES_IN_PROJ_L1 = ((1, ("w_up", "w_down", "w_ple_gate", "w_ple")),)
EARLY_GRADS_L0 = ("w_ple", "w_ple_gate", "w_down", "w_up")
LATE_GRADS_L0 = ("w_out", "w_branch_a", "w_branch_b", "w_pool", "w_in")


def _swap_of(G, names):
    return swap_exchange([G[k] for k in names], [GEOM[k] for k in names])


def _after_swap(G, names, got, place, tag):
    parts = [pair_sum(G[k], r, *GEOM[k], place, name=f"pair_sum_{k}_{tag}") for k, r in zip(names, got)]
    return scatter_exchange(parts, [GEOM[k] for k in names], [SHAPE[k] for k in names]), parts


def _reduce_start(G, names, place, tag):
    got = run_exchanges([_swap_of(G, names)], name=f"swap_halves_{tag}")
    return _after_swap(G, names, got, place, tag)


def _reduce_end(names, parts, slots, place, l, reduced):
    for k, q, s in zip(names, parts, slots):
        reduced[k] = chip_sum(q, s, SHAPE[k], *GEOM[k], l, place, reduced.get(k), name=f"chip_sum_{k}_l{l}")


def _local_step(x, p2, tgt, W0, W1, conv_w, small, place):
    T = x.shape[0]
    as3 = lambda a: a.reshape(2, 1, a.shape[-1])
    mix3, scale3, sgu3 = as3(small["mix_norm"]), as3(small["pool_scale"]), as3(small["sgu_norm"])
    ffn3, ple3, convb3 = as3(small["ffn_norm"]), as3(small["ple_norm"]), as3(small["conv_b"])
    tril = jnp.tril(jnp.ones((CHUNK, CHUNK), F32))
    ws_masked = small["w_spatial"] * tril
    wsm = ws_masked.astype(BF16)
    wsmT = jnp.swapaxes(ws_masked, -1, -2).astype(BF16)
    bT = jnp.swapaxes(small["b_spatial"], -1, -2)
    final3 = small["final_norm"].reshape(1, D)
    W = [dict(W0), dict(W1)]

    def riders(groups):
        return [gather_exchange([W[lyr][k] for k in names], [GEOM[k] for k in names]) for lyr, names in groups]

    def landed(groups, got):
        for lyr, names in groups:
            W[lyr].update(zip(names, got[:len(names)]))
            got = got[len(names):]

    saved = []
    hb, (W[0]["w_in"], conv_w) = norm_fwd(
        x, mix3, 0, name="mix_norm_fwd_l0",
        host=[gather_exchange([W[0]["w_in"]], [GEOM["w_in"]]),
              gather_by_layer_exchange(conv_w, 1, conv_w.shape[2] // N_CHIPS)])
    for l in range(2):
        n = lambda s: f"{s}_l{l}"
        Wl = W[l]
        groups = RIDES_IN_PROJ_L0 if l == 0 else RIDES_IN_PROJ_L1
        z, got = mm_nn(hb, Wl["w_in"], 0, name=n("in_proj"), rows=T, tn=1280, out_dtype=BF16, host=riders(groups))
        landed(groups, got)
        a_in = pool_fwd(z, Wl["w_pool"], scale3, l, name=n("pool_fwd"))
        s_in = sgu_fwd(z, sgu3, wsm, bT, l, name=n("sgu_fwd"))
        yab = mm_nn(a_in, Wl["w_branch_a"], 0, name=n("branch_a"), rows=T, out_cols=2 * D, out_dtype=BF16)
        yab = mm_nn(s_in, Wl["w_branch_b"], 0, name=n("branch_b"), rows=T, out=yab, out_cols=2 * D, out_col_off=D,
                    out_dtype=BF16)
        mo = gate_fwd(z, yab, name=n("gate_fwd"))
        x1, h2b = mm_nn(mo, Wl["w_out"], 0, name=n("out_proj"), rows=T, resid=x, norm_gain=ffn3[l:l + 1])
        if l == 0:
            up, got = mm_nn(h2b, Wl["w_up"], 0, name=n("up_proj"), rows=T, tn=DFF, host=riders(RIDES_UP_PROJ_L0))
            landed(RIDES_UP_PROJ_L0, got)
        else:
            up = mm_nn(h2b, Wl["w_up"], 0, name=n("up_proj"), rows=T, tn=DFF)
        f = conv_fwd(up, conv_w, convb3, l, name=n("conv_fwd"))
        if l == 0:
            (x2, h3b), got = mm_nn(f, Wl["w_down"], 0, name=n("down_proj"), rows=T, resid=x1,
                                   norm_gain=ple3[l:l + 1], host=riders(RIDES_DOWN_PROJ_L0))
            landed(RIDES_DOWN_PROJ_L0, got)
        else:
            x2, h3b = mm_nn(f, Wl["w_down"], 0, name=n("down_proj"), rows=T, resid=x1, norm_gain=ple3[l:l + 1])
        pg = mm_nn(h3b, Wl["w_ple_gate"], 0, name=n("ple_gate_proj"), rows=T, out_dtype=BF16)
        e = mm_nn(p2, Wl["w_ple"], 0, name=n("ple_proj"), rows=T, a_row_off=l * T, out_dtype=BF16)
        saved.append(dict(x=x, hb=hb, z=z, a_in=a_in, s_in=s_in, yab=yab, mo=mo, x1=x1, h2b=h2b, up=up, f=f,
                          x2=x2, h3b=h3b, pg=pg, e=e))
        if l == 0:
            x, hb = ple_fwd(x2, pg, e, mix3, 1, name=n("ple_fwd"))
        else:
            x = ple_fwd(x2, pg, e, None, 0, name=n("ple_fwd"))

    loss_acc, dx, dg_final = loss_head(x, final3, tgt, name="loss_head")

    small_grads = [None, None]
    all_names = [t[0] for t in BIG]
    reduced = {}
    swap1 = G1 = scatter1 = parts1 = slots1 = None
    for l in (1, 0):
        n = lambda s: f"{s}_l{l}"
        a, Wl, G = saved[l], W[l], {}
        de, dpg = ple_bwd(dx, a["pg"], a["e"], name=n("ple_bwd"))
        G["w_ple"] = mm_tn(p2, de, name=n("d_w_ple"), rows=T, ka=PG, nb=D, a_row_off=l * T)
        G["w_ple_gate"] = mm_tn(a["h3b"], dpg, name=n("d_w_ple_gate"), rows=T, ka=D, nb=D)
        if l == 0:
            (dx2, dg_ple), got = mm_nt(dpg, Wl["w_ple_gate"], 0, name=n("ple_norm_bwd"), rows=T,
                                       norm_bwd_of=(a["x2"], ple3, l, dx), host=[swap1])
            scatter1, parts1 = _after_swap(G1, all_names, got, place, "l1")
        else:
            dx2, dg_ple = mm_nt(dpg, Wl["w_ple_gate"], 0, name=n("ple_norm_bwd"), rows=T,
                                norm_bwd_of=(a["x2"], ple3, l, dx))
        df = mm_nt(dx2, Wl["w_down"], 0, name=n("d_ffn_act"), rows=T, out_dtype=F32)
        G["w_down"] = mm_tn(a["f"], dx2, name=n("d_w_down"), rows=T, ka=DFF, nb=D, tm=1408)
        if l == 0:
            dup, dcw, dcb, slots1 = conv_bwd(df, a["up"], conv_w, convb3, l, name=n("conv_bwd"), host=[scatter1])
        else:
            dup, dcw, dcb, _ = conv_bwd(df, a["up"], conv_w, convb3, l, name=n("conv_bwd"))
        G["w_up"] = mm_tn(a["h2b"], dup, name=n("d_w_up"), rows=T, ka=D, nb=2 * DFF, tn=DFF, tk=1024)
        if l == 0:
            scatter_early, parts_early = _reduce_start(G, EARLY_GRADS_L0, place, "l0_early")
            (dx1, dg_ffn), slots_early = mm_nt(dup, Wl["w_up"], 0, name=n("ffn_norm_bwd"), rows=T, tk=1408,
                                               norm_bwd_of=(a["x1"], ffn3, l, dx2), host=[scatter_early])
        else:
            dx1, dg_ffn = mm_nt(dup, Wl["w_up"], 0, name=n("ffn_norm_bwd"), rows=T, tk=1408,
                                norm_bwd_of=(a["x1"], ffn3, l, dx2))
        dmo = mm_nt(dx1, Wl["w_out"], 0, name=n("d_gated"), rows=T)
        G["w_out"] = mm_tn(a["mo"], dx1, name=n("d_w_out"), rows=T, ka=D, nb=D)
        dz, dyab = gate_bwd(dmo, a["z"], a["yab"], name=n("gate_bwd"))
        G["w_branch_a"] = mm_tn(a["a_in"], dyab, name=n("d_w_branch_a"), rows=T, ka=D, nb=D)
        G["w_branch_b"] = mm_tn(a["s_in"], dyab, name=n("d_w_branch_b"), rows=T, ka=D, nb=D, b_col_off=D)
        da = mm_nt(dyab, Wl["w_branch_a"], 0, name=n("d_pool_out"), rows=T, kdim=D)
        ds = mm_nt(dyab, Wl["w_branch_b"], 0, name=n("d_sgu_out"), rows=T, kdim=D, a_col_off=D)
        dz, dwp, dsc, dws, dbt, dgs = mixer_bwd(da, ds, a["z"], dz, Wl["w_pool"], scale3, sgu3, wsm, wsmT, bT, l,
                                                name=n("mixer_bwd"))
        G["w_pool"] = dwp.astype(BF16)[None]
        G["w_in"] = mm_tn(a["hb"], dz, name=n("d_w_in"), rows=T, ka=D, nb=5 * D, tn=1280)
        if l == 0:
            scatter_late, parts_late = _reduce_start(G, LATE_GRADS_L0, place, "l0_late")
            _reduce_end(all_names, parts1, slots1, place, 1, reduced)
            _reduce_end(EARLY_GRADS_L0, parts_early, slots_early, place, 0, reduced)
            done = [(t, 1) for t in range(len(all_names))] + [(all_names.index(k), 0) for k in EARLY_GRADS_L0]
            (dx, dg_mix), got = mm_nt(
                dz, Wl["w_in"], 0, name=n("mix_norm_bwd"), rows=T, tk=1280, norm_bwd_of=(a["x"], mix3, l, dx1),
                host=[scatter_late, share_exchange([reduced[k] for k in all_names], done)])
            slots_late = got[:len(LATE_GRADS_L0)]
            reduced.update(zip(all_names, got[len(LATE_GRADS_L0):]))
        else:
            dx, dg_mix = mm_nt(dz, Wl["w_in"], 0, name=n("mix_norm_bwd"), rows=T, tk=1280,
                               norm_bwd_of=(a["x"], mix3, l, dx1))
            swap1, G1 = _swap_of(G, all_names), G
        small_grads[l] = dict(
            mix_norm=dg_mix[0], pool_scale=dsc[0], sgu_norm=dgs[0], w_spatial=dws, b_spatial=dbt.T,
            ffn_norm=dg_ffn[0], conv_b=jnp.concatenate([dcb[0, 0], dcb[1, 0]]), ple_norm=dg_ple[0],
            conv_w=jnp.concatenate([dcw[0, :3], dcw[1, :3]], axis=1))
    _reduce_end(LATE_GRADS_L0, parts_late, slots_late, place, 0, reduced)
    return loss_acc, dx, reduced, small_grads, dg_final[0]


SMALL_ORDER = ("mix_norm", "pool_scale", "sgu_norm", "w_spatial", "b_spatial", "ffn_norm", "conv_b", "ple_norm",
               "conv_w")


def _pack_rows(pieces, row_multiple):
    flat = jnp.concatenate([a.reshape(-1) for a in pieces])
    rows = -(-flat.shape[0] // LANES)
    rows = -(-rows // row_multiple) * row_multiple
    return jnp.pad(flat, (0, rows * LANES - flat.shape[0])).reshape(rows, LANES)


def _unpack(flat, shapes):
    out, off = [], 0
    for shp in shapes:
        size = 1
        for s in shp:
            size *= s
        out.append(flat[off:off + size].reshape(shp))
        off += size
    return out


def kernel(x, p, mix_norm, w_in, w_pool, pool_scale, sgu_norm, w_spatial, b_spatial, w_branch_a, w_branch_b, w_out, ffn_norm, w_up, conv_w, conv_b, w_down, ple_norm, w_ple_gate, w_ple, final_norm, loss_target, m_mix_norm, m_w_in, m_w_pool, m_pool_scale, m_sgu_norm, m_w_spatial, m_b_spatial, m_w_branch_a, m_w_branch_b, m_w_out, m_ffn_norm, m_w_up, m_conv_w, m_conv_b, m_w_down, m_ple_norm, m_w_ple_gate, m_w_ple, m_final_norm, v_mix_norm, v_w_in, v_w_pool, v_pool_scale, v_sgu_norm, v_w_spatial, v_b_spatial, v_w_branch_a, v_w_branch_b, v_w_out, v_ffn_norm, v_w_up, v_conv_w, v_conv_b, v_w_down, v_ple_norm, v_w_ple_gate, v_w_ple, v_final_norm):
    names = ["mix_norm", "w_in", "w_pool", "pool_scale", "sgu_norm", "w_spatial", "b_spatial", "w_branch_a",
             "w_branch_b", "w_out", "ffn_norm", "w_up", "conv_w", "conv_b", "w_down", "ple_norm", "w_ple_gate",
             "w_ple", "final_norm"]
    w = dict(zip(names, [mix_norm, w_in, w_pool, pool_scale, sgu_norm, w_spatial, b_spatial, w_branch_a, w_branch_b,
                         w_out, ffn_norm, w_up, conv_w, conv_b, w_down, ple_norm, w_ple_gate, w_ple, final_norm]))
    m = dict(zip(names, [m_mix_norm, m_w_in, m_w_pool, m_pool_scale, m_sgu_norm, m_w_spatial, m_b_spatial,
                         m_w_branch_a, m_w_branch_b, m_w_out, m_ffn_norm, m_w_up, m_conv_w, m_conv_b, m_w_down,
                         m_ple_norm, m_w_ple_gate, m_w_ple, m_final_norm]))
    v = dict(zip(names, [v_mix_norm, v_w_in, v_w_pool, v_pool_scale, v_sgu_norm, v_w_spatial, v_b_spatial,
                         v_w_branch_a, v_w_branch_b, v_w_out, v_ffn_norm, v_w_up, v_conv_w, v_conv_b, v_w_down,
                         v_ple_norm, v_w_ple_gate, v_w_ple, v_final_norm]))
    T = x.shape[1]
    chip = 2 * lax.axis_index("x") + lax.axis_index("y")
    place = jnp.stack([chip, lax.axis_index("c")]).astype(jnp.int32)

    big_names = [t[0] for t in BIG]
    placed = [{k: place_shard(w[k], l, *GEOM[k], BF16, place, name=f"place_{k}_l{l}") for k in big_names}
              for l in range(2)]
    conv_w8 = jnp.pad(conv_w, ((0, 0), (0, CONV_ROWS - conv_w.shape[1]), (0, 0)))
    conv_placed = place_both_layers(conv_w8, 1, conv_w.shape[2], place, name="place_conv_w")

    small = {k: w[k] for k in ("mix_norm", "pool_scale", "sgu_norm", "w_spatial", "b_spatial", "ffn_norm",
                               "conv_b", "ple_norm", "final_norm")}
    loss_acc, dx, reduced, small_grads, dg_final = _local_step(
        x.reshape(T, D), p.reshape(2 * T, p.shape[-1]), loss_target.reshape(T, D), placed[0], placed[1], conv_placed,
        small, place)
    loss = lax.psum(loss_acc[0, 0], ("x", "y", "c"))
    full = run_exchanges([share_exchange([reduced[k] for k in big_names],
                                         [(big_names.index(k), 0) for k in LATE_GRADS_L0])], name="share_last_halves")
    grads = dict(zip(big_names, full))

    pieces = [small_grads[l][k] for l in range(2) for k in SMALL_ORDER] + [dg_final]
    shapes = [a.shape for a in pieces]
    total = all_reduce_small(_pack_rows(pieces, 16)).reshape(-1)
    summed = _unpack(total, shapes)
    per_layer = {k: jnp.stack([summed[i], summed[len(SMALL_ORDER) + i]]) for i, k in enumerate(SMALL_ORDER)}
    for k in ("mix_norm", "pool_scale", "sgu_norm", "w_spatial", "b_spatial", "ffn_norm", "conv_b", "ple_norm"):
        grads[k] = per_layer[k]
    grads["final_norm"] = summed[-1]
    cw = conv_w.shape[2]
    grads["conv_w"] = lax.dynamic_slice_in_dim(per_layer["conv_w"], chip * cw, cw, axis=2)

    delta, new_m, new_v = {}, {}, {}
    for name in big_names:
        shp = w[name].shape
        d_, m_, v_, g_ = elementwise(lambda w_, g_, m_, v_: (*_adamw(w_, g_, m_, v_), g_),
                                     [_view2d(a) for a in (w[name], grads[name], m[name], v[name])],
                                     [F32, F32, F32, F32], name=f"adamw_{name}")
        delta[name], new_m[name], new_v[name] = d_.reshape(shp), m_.reshape(shp), v_.reshape(shp)
        grads[name] = g_.reshape(shp)
    small_names = [k for k in names if k not in big_names]
    small_shapes = [w[k].shape for k in small_names]
    packed = [_pack_rows([src[k] for k in small_names], 8) for src in (w, grads, m, v)]
    outs = elementwise(_adamw, packed, [F32, F32, F32], name="adamw_small")
    for dst, o in zip((delta, new_m, new_v), outs):
        for k, a in zip(small_names, _unpack(o.reshape(-1), small_shapes)):
            dst[k] = a

    return (loss, dx.reshape(1, T, D), *[grads[k] for k in names], *[delta[k] for k in names],
            *[new_m[k] for k in names], *[new_v[k] for k in names])
```

```python
import functools

import jax
import jax.numpy as jnp
from jax import lax
from jax.experimental import pallas as pl
from jax.experimental.pallas import tpu as pltpu

F32 = jnp.float32
BF16 = jnp.bfloat16
EPS = 1e-6
D = 1024
POOL_WINDOWS = (2, 4, 8, 16)
PG = 256
POOL_HALO = 16
CHUNK = 128
HEADS = 8
DFF = 2816
CONV_HALO = 8
CONV_TC = 1408
N_CHIPS = 4
LANES = 128
VMEM_LIMIT = 56 * 1024 * 1024
MESH = pl.DeviceIdType.MESH
ANY = pl.BlockSpec(memory_space=pl.ANY)

ADAM_LR = 0.001
ADAM_B1 = 0.9
ADAM_B2 = 0.999
ADAM_EPS = 1e-08
ADAM_WD = 0.01
ADAM_STEP = 10

BIG = (
    ("w_in", (D, 5 * D), 1, 5 * D // N_CHIPS),
    ("w_pool", (4, PG, PG), 1, PG // N_CHIPS),
    ("w_branch_a", (D, D), 0, D // N_CHIPS),
    ("w_branch_b", (D, D), 0, D // N_CHIPS),
    ("w_out", (D, D), 0, D // N_CHIPS),
    ("w_up", (D, 2 * DFF), 1, 2 * DFF // N_CHIPS),
    ("w_down", (DFF, D), 0, DFF // N_CHIPS),
    ("w_ple_gate", (D, D), 0, D // N_CHIPS),
    ("w_ple", (PG, D), 1, D // N_CHIPS),
)
CONV_ROWS = 8


def _params(n_axes):
    return pltpu.CompilerParams(dimension_semantics=("arbitrary",) * n_axes, vmem_limit_bytes=VMEM_LIMIT)


def _gelu(x):
    return 0.5 * x * (1.0 + lax.erf(x * 0.7071067811865476))


def _gelu_grad(x):
    return 0.5 * (1.0 + lax.erf(x * 0.7071067811865476)) + x * jnp.exp(-0.5 * x * x) * 0.3989422804014327


def _shard_shape(shape, axis, size):
    return tuple(size if a == axis else s for a, s in enumerate(shape))


def _block(ref, axis, j, size):
    idx = tuple(pl.ds(j * size, size) if a == axis else slice(None) for a in range(len(ref.shape)))
    return ref.at[idx]


def mm_nn(a, w, l, *, name, rows, out_dtype=F32, resid=None, a_row_off=0, out=None, out_cols=None,
          out_col_off=0, norm_gain=None, host=None, tm=1024, tn=None, tk=None):
    K, N = w.shape[1], w.shape[2]
    tn = tn or N
    tk = tk or K
    nk = K // tk
    out_cols = out_cols or N
    assert rows % tm == 0 and N % tn == 0 and K % tk == 0 and out_col_off % tn == 0 and a_row_off % tm == 0
    has_resid, has_out, has_norm = resid is not None, out is not None, norm_gain is not None
    assert not has_norm or (tn == N and not has_out)
    grid = (N // tn, rows // tm, nk)
    hosting = _Hosting(host)
    n_in = 2 + has_resid + has_norm + has_out
    n_host_in, n_host_out = len(hosting.arrays), len(hosting.out_shapes)
    n_own_out = 1 + has_norm

    def body(*refs):
        refs = list(refs)
        a_ref, w_ref = refs[0], refs[1]
        r_ref = refs[2] if has_resid else None
        g_ref = refs[2 + has_resid] if has_norm else None
        host_in = refs[n_in:n_in + n_host_in]
        o_base = n_in + n_host_in
        o_ref = refs[o_base]
        host_out = refs[o_base + n_own_out:o_base + n_own_out + n_host_out]
        scratch = refs[o_base + n_own_out + n_host_out:]
        if hosting.plan:
            first, last = _first_last(grid)
            sems = scratch[-2:]

            @pl.when(first)
            def _():
                hosting.begin(host_in, host_out, *sems)

        part = jnp.dot(a_ref[...].astype(BF16), w_ref[...], preferred_element_type=F32)

        def finish(r):
            if has_resid:
                r = r + r_ref[...]
            o_ref[...] = r.astype(o_ref.dtype)
            if has_norm:
                scale = lax.rsqrt(jnp.mean(r * r, axis=-1, keepdims=True) + EPS)
                refs[o_base + 1][...] = (r * scale * g_ref[...]).astype(BF16)

        if nk == 1:
            finish(part)
        else:
            acc = scratch[0]
            k = pl.program_id(2)

            @pl.when(k == 0)
            def _():
                acc[...] = part

            @pl.when(k > 0)
            def _():
                acc[...] += part

            @pl.when(k == nk - 1)
            def _():
                finish(acc[...])

        if hosting.plan:
            @pl.when(last)
            def _():
                hosting.finish(host_in, host_out, *sems)

    in_specs = [pl.BlockSpec((tm, tk), lambda j, i, k: (i + a_row_off // tm, k)),
                pl.BlockSpec((None, tk, tn), lambda j, i, k: (l, k, j))]
    args = [a, w]
    if has_resid:
        in_specs.append(pl.BlockSpec((tm, tn), lambda j, i, k: (i, j)))
        args.append(resid)
    if has_norm:
        in_specs.append(pl.BlockSpec((None, 1, tn), lambda j, i, k: (l, 0, 0)))
        args.append(norm_gain)
    aliases = {}
    if has_out:
        in_specs.append(ANY)
        aliases = {len(args): 0}
        args.append(out)
    aliases.update(hosting.aliases(n_in, n_own_out))
    out_specs = [pl.BlockSpec((tm, tn), lambda j, i, k: (i, j + out_col_off // tn))]
    out_shape = [jax.ShapeDtypeStruct((rows, out_cols), out_dtype)]
    if has_norm:
        out_specs.append(pl.BlockSpec((tm, tn), lambda j, i, k: (i, j)))
        out_shape.append(jax.ShapeDtypeStruct((rows, N), BF16))
    res = pl.pallas_call(
        body, name=name, grid=grid,
        in_specs=in_specs + [ANY] * n_host_in,
        out_specs=out_specs + [ANY] * n_host_out,
        out_shape=out_shape + hosting.out_shapes,
        scratch_shapes=([pltpu.VMEM((tm, tn), F32)] if nk > 1 else []) + hosting.scratch(),
        input_output_aliases=aliases, compiler_params=_params(3))(*args, *hosting.arrays)
    own = res[0] if n_own_out == 1 else tuple(res[:n_own_out])
    return (own, list(res[n_own_out:])) if hosting.plan else own


def mm_nt(a, w, l, *, name, rows, kdim=None, a_col_off=0, out_dtype=BF16, norm_bwd_of=None, host=None, tm=1024,
          tn=None, tk=None):
    R = w.shape[1]
    kdim = kdim or w.shape[2]
    tn = tn or R
    tk = tk or kdim
    nk = kdim // tk
    assert rows % tm == 0 and R % tn == 0 and kdim % tk == 0 and a_col_off % tk == 0
    fused = norm_bwd_of is not None
    assert not fused or tn == R
    grid = (R // tn, rows // tm, nk)
    hosting = _Hosting(host)
    n_host_in, n_host_out = len(hosting.arrays), len(hosting.out_shapes)
    n_own_in, n_own_out = (3, 2) if fused else (0, 1)

    def body(a_ref, w_ref, *refs):
        host_in = refs[n_own_in:n_own_in + n_host_in]
        host_out = refs[n_own_in + n_host_in + n_own_out:n_own_in + n_host_in + n_own_out + n_host_out]
        scratch = refs[n_own_in + n_host_in + n_own_out + n_host_out:]
        rest = list(refs[:n_own_in]) + list(refs[n_own_in + n_host_in:n_own_in + n_host_in + n_own_out]) \
            + ([scratch[0]] if nk > 1 else [])
        if hosting.plan:
            first, last = _first_last(grid)
            sems = scratch[-2:]

            @pl.when(first)
            def _():
                hosting.begin(host_in, host_out, *sems)

        part = lax.dot_general(a_ref[...].astype(BF16), w_ref[...], (((1,), (1,)), ((), ())),
                               preferred_element_type=F32)
        i, k = pl.program_id(1), pl.program_id(2)

        def finish(dh):
            if not fused:
                rest[0][...] = dh.astype(rest[0].dtype)
                return
            x_ref, g_ref, dxi_ref, dx_ref, dg_ref = rest[:5]

            @pl.when(i == 0)
            def _():
                dg_ref[...] = jnp.zeros_like(dg_ref)

            xv = x_ref[...]
            r = lax.rsqrt(jnp.mean(xv * xv, axis=-1, keepdims=True) + EPS)
            xh = xv * r
            dhg = dh * g_ref[...]
            dx_ref[...] = dxi_ref[...] + r * (dhg - xh * jnp.mean(dhg * xh, axis=-1, keepdims=True))
            dg_ref[0:1, :] += jnp.sum(dh * xh, axis=0, keepdims=True)

        if nk == 1:
            finish(part)
        else:
            acc = rest[-1]

            @pl.when(k == 0)
            def _():
                acc[...] = part

            @pl.when(k > 0)
            def _():
                acc[...] += part

            @pl.when(k == nk - 1)
            def _():
                finish(acc[...])

        if hosting.plan:
            @pl.when(last)
            def _():
                hosting.finish(host_in, host_out, *sems)

    if a.ndim == 3:
        per = a.shape[2] // tk
        a_spec = pl.BlockSpec((None, tm, tk), lambda j, i, k: (k // per, i, k % per))
    else:
        a_spec = pl.BlockSpec((tm, tk), lambda j, i, k: (i, k + a_col_off // tk))
    in_specs = [a_spec, pl.BlockSpec((None, tn, tk), lambda j, i, k: (l, j, k))]
    args = [a, w]
    row_tile = pl.BlockSpec((tm, tn), lambda j, i, k: (i, j))
    if fused:
        x, gain, gl, dx_in = norm_bwd_of
        in_specs += [row_tile, pl.BlockSpec((None, 1, tn), lambda j, i, k: (gl, 0, 0)), row_tile]
        args += [x, gain, dx_in]
        out_specs = [row_tile, pl.BlockSpec((8, tn), lambda j, i, k: (0, 0))]
        out_shape = [jax.ShapeDtypeStruct((rows, R), F32), jax.ShapeDtypeStruct((8, R), F32)]
    else:
        out_specs, out_shape = [row_tile], [jax.ShapeDtypeStruct((rows, R), out_dtype)]
    res = pl.pallas_call(
        body, name=name, grid=grid, in_specs=in_specs + [ANY] * n_host_in,
        out_specs=out_specs + [ANY] * n_host_out, out_shape=out_shape + hosting.out_shapes,
        scratch_shapes=([pltpu.VMEM((tm, tn), F32)] if nk > 1 else []) + hosting.scratch(),
        input_output_aliases=hosting.aliases(2 + n_own_in, n_own_out), compiler_params=_params(3))(
            *args, *hosting.arrays)
    own = tuple(res[:n_own_out]) if fused else res[0]
    return (own, list(res[n_own_out:])) if hosting.plan else own


def mm_tn(a, b, *, name, rows, ka, nb, a_row_off=0, b_col_off=0, tm=None, tn=None, tk=2048):
    tm = tm or ka
    tn = tn or nb
    tk = min(tk, rows)
    nk = rows // tk
    assert ka % tm == 0 and nb % tn == 0 and rows % tk == 0 and b_col_off % tn == 0 and a_row_off % tk == 0

    def body(a_ref, b_ref, o_ref, acc):
        part = lax.dot_general(a_ref[...].astype(BF16), b_ref[...].astype(BF16), (((0,), (0,)), ((), ())),
                               preferred_element_type=F32)
        k = pl.program_id(2)

        @pl.when(k == 0)
        def _():
            acc[...] = part

        @pl.when(k > 0)
        def _():
            acc[...] += part

        @pl.when(k == nk - 1)
        def _():
            o_ref[...] = acc[...].astype(o_ref.dtype)

    if b.ndim == 3:
        per = b.shape[2] // tn
        b_spec = pl.BlockSpec((None, tk, tn), lambda j, i, k: (j // per, k, j % per))
    else:
        b_spec = pl.BlockSpec((tk, tn), lambda j, i, k: (k, j + b_col_off // tn))
    return pl.pallas_call(
        body, name=name, grid=(nb // tn, ka // tm, nk),
        in_specs=[pl.BlockSpec((tk, tm), lambda j, i, k: (k + a_row_off // tk, i)), b_spec],
        out_specs=pl.BlockSpec((None, tm, tn), lambda j, i, k: (0, i, j)),
        out_shape=jax.ShapeDtypeStruct((1, ka, nb), BF16),
        scratch_shapes=[pltpu.VMEM((tm, tn), F32)], compiler_params=_params(3))(a, b)


def _row_spec(tm, width, col=0):
    return pl.BlockSpec((tm, width), lambda i: (i, col))


def _gain_spec(l, width=D):
    return pl.BlockSpec((None, 1, width), lambda i: (l, 0, 0))


def norm_fwd(x, g3, l, *, name, tm=1024):
    T = x.shape[0]

    def body(x_ref, g_ref, o_ref):
        xv = x_ref[...]
        r = lax.rsqrt(jnp.mean(xv * xv, axis=-1, keepdims=True) + EPS)
        o_ref[...] = (xv * r * g_ref[...]).astype(BF16)

    return pl.pallas_call(
        body, name=name, grid=(T // tm,),
        in_specs=[_row_spec(tm, D), _gain_spec(l)], out_specs=_row_spec(tm, D),
        out_shape=jax.ShapeDtypeStruct((T, D), BF16), compiler_params=_params(1))(x, g3)


def _winsum_back(ext, w):
    s, span = ext, 1
    while span < w:
        s = s + pltpu.roll(s, span, 0)
        span *= 2
    return s


def _winsum_fwd(ext, w):
    rows = ext.shape[0]
    s, span = ext, 1
    while span < w:
        s = s + pltpu.roll(s, rows - span, 0)
        span *= 2
    return s


def _pooled(ext, z, t, g, w):
    sl = slice(g * PG, (g + 1) * PG)
    s = _winsum_back(ext[:, sl], w)[POOL_HALO:, :]
    return s / jnp.minimum(t + 1, w).astype(F32) - z[:, sl]


def pool_fwd(z, wpool, scale3, l, *, name, tm=256):
    T = z.shape[0]
    hb = tm // POOL_HALO
    wl = l if wpool.shape[0] > 1 else 0

    def body(z_ref, zp_ref, wp_ref, sc_ref, o_ref):
        i = pl.program_id(0)
        zv = z_ref[...].astype(F32)
        prev = jnp.where(i == 0, 0.0, zp_ref[...].astype(F32))
        ext = jnp.concatenate([prev, zv], axis=0)
        t = i * tm + lax.broadcasted_iota(jnp.int32, (tm, 1), 0)
        for g, w in enumerate(POOL_WINDOWS):
            sl = slice(g * PG, (g + 1) * PG)
            pooled = _pooled(ext, zv, t, g, w)
            q = jnp.dot(pooled.astype(BF16), wp_ref[g], preferred_element_type=F32)
            o_ref[:, sl] = (q * sc_ref[:, sl]).astype(BF16)

    return pl.pallas_call(
        body, name=name, grid=(T // tm,),
        in_specs=[_row_spec(tm, D),
                  pl.BlockSpec((POOL_HALO, D), lambda i: (jnp.maximum(i * hb - 1, 0), 0)),
                  pl.BlockSpec((None, 4, PG, PG), lambda i: (wl, 0, 0, 0)),
                  _gain_spec(l)],
        out_specs=_row_spec(tm, D),
        out_shape=jax.ShapeDtypeStruct((T, D), BF16), compiler_params=_params(1))(z, z, wpool, scale3)


def sgu_fwd(z, g3, wsm, bT, l, *, name, tm=256):
    T = z.shape[0]

    def body(zu_ref, zv_ref, g_ref, ws_ref, b_ref, o_ref):
        gu = _gelu(zu_ref[...].astype(F32))
        gv = _gelu(zv_ref[...].astype(F32))
        rv = lax.rsqrt(jnp.mean(gv * gv, axis=-1, keepdims=True) + EPS)
        vn = (gv * rv * g_ref[...]).astype(BF16)
        for n in range(tm // CHUNK):
            r = slice(n * CHUNK, (n + 1) * CHUNK)
            for h in range(HEADS):
                cs = slice(h * CHUNK, (h + 1) * CHUNK)
                mixed = jnp.dot(ws_ref[h], vn[r, cs], preferred_element_type=F32) + b_ref[:, h:h + 1]
                o_ref[r, cs] = (gu[r, cs] * mixed).astype(BF16)

    return pl.pallas_call(
        body, name=name, grid=(T // tm,),
        in_specs=[_row_spec(tm, D, 1), _row_spec(tm, D, 2), _gain_spec(l),
                  pl.BlockSpec((None, HEADS, CHUNK, CHUNK), lambda i: (l, 0, 0, 0)),
                  pl.BlockSpec((None, CHUNK, HEADS), lambda i: (l, 0, 0))],
        out_specs=_row_spec(tm, D),
        out_shape=jax.ShapeDtypeStruct((T, D), BF16), compiler_params=_params(1))(z, z, g3, wsm, bT)


def gate_fwd(z, yab, *, name, tm=1024):
    T = z.shape[0]

    def body(za_ref, zb_ref, y_ref, o_ref):
        ga = jax.nn.sigmoid(za_ref[...].astype(F32))
        gb = jax.nn.sigmoid(zb_ref[...].astype(F32))
        o_ref[...] = (ga * y_ref[:, :D].astype(F32) + gb * y_ref[:, D:].astype(F32)).astype(BF16)

    return pl.pallas_call(
        body, name=name, grid=(T // tm,),
        in_specs=[_row_spec(tm, D, 3), _row_spec(tm, D, 4), _row_spec(tm, 2 * D)],
        out_specs=_row_spec(tm, D),
        out_shape=jax.ShapeDtypeStruct((T, D), BF16), compiler_params=_params(1))(z, z, yab)


def _conv(ext, w_ref, b_ref):
    down1, down2 = pltpu.roll(ext, 1, 0), pltpu.roll(ext, 2, 0)
    c = b_ref[...] + w_ref[0:1, :] * down2
    c = c + w_ref[1:2, :] * down1
    return c + w_ref[2:3, :] * ext, down1, down2


def conv_fwd(up, convw, convb3, l, *, name, tm=256):
    T = up.shape[0]
    tc = CONV_TC
    nc = DFF // tc
    hb = tm // CONV_HALO

    def body(ua_ref, uap_ref, ub_ref, ubp_ref, wa_ref, wb_ref, ba_ref, bb_ref, o_ref):
        i = pl.program_id(1)

        def conv_of(u_ref, p_ref, w_ref, b_ref):
            ext = jnp.concatenate([jnp.where(i == 0, 0.0, p_ref[...]), u_ref[...]], axis=0)
            return _conv(ext, w_ref, b_ref)[0][CONV_HALO:, :]

        ca = conv_of(ua_ref, uap_ref, wa_ref, ba_ref)
        cb = conv_of(ub_ref, ubp_ref, wb_ref, bb_ref)
        o_ref[...] = (_gelu(ca) * cb).astype(BF16)

    def cur(off):
        return pl.BlockSpec((tm, tc), lambda j, i: (i, j + off))

    def prev(off):
        return pl.BlockSpec((CONV_HALO, tc), lambda j, i: (jnp.maximum(i * hb - 1, 0), j + off))

    def wspec(off):
        return pl.BlockSpec((None, CONV_ROWS, tc), lambda j, i: (l, 0, j + off))

    def bspec(off):
        return pl.BlockSpec((None, 1, tc), lambda j, i: (l, 0, j + off))

    return pl.pallas_call(
        body, name=name, grid=(nc, T // tm),
        in_specs=[cur(0), prev(0), cur(nc), prev(nc), wspec(0), wspec(nc), bspec(0), bspec(nc)],
        out_specs=pl.BlockSpec((tm, tc), lambda j, i: (i, j)),
        out_shape=jax.ShapeDtypeStruct((T, DFF), BF16),
        compiler_params=_params(2))(up, up, up, up, convw, convw, convb3, convb3)


def ple_fwd(x2, pg, e, g3, l, *, name, tm=1024):
    T = x2.shape[0]
    has_norm = g3 is not None

    def body(x_ref, pg_ref, e_ref, *rest):
        xv = x_ref[...] + jax.nn.sigmoid(pg_ref[...].astype(F32)) * e_ref[...].astype(F32)
        if has_norm:
            g_ref, o_ref, h_ref = rest
            r = lax.rsqrt(jnp.mean(xv * xv, axis=-1, keepdims=True) + EPS)
            h_ref[...] = (xv * r * g_ref[...]).astype(BF16)
        else:
            o_ref, = rest
        o_ref[...] = xv

    x_shape = jax.ShapeDtypeStruct((T, D), F32)
    return pl.pallas_call(
        body, name=name, grid=(T // tm,),
        in_specs=[_row_spec(tm, D)] * 3 + ([_gain_spec(l)] if has_norm else []),
        out_specs=[_row_spec(tm, D)] * 2 if has_norm else _row_spec(tm, D),
        out_shape=[x_shape, jax.ShapeDtypeStruct((T, D), BF16)] if has_norm else x_shape,
        compiler_params=_params(1))(x2, pg, e, *([g3] if has_norm else []))


def loss_head(x, g3, tgt, *, name, tm=1024):
    T = x.shape[0]

    def body(x_ref, g_ref, t_ref, loss_ref, dx_ref, dg_ref):
        @pl.when(pl.program_id(0) == 0)
        def _():
            loss_ref[...] = jnp.zeros_like(loss_ref)
            dg_ref[...] = jnp.zeros_like(dg_ref)

        xv, g = x_ref[...], g_ref[...]
        r = lax.rsqrt(jnp.mean(xv * xv, axis=-1, keepdims=True) + EPS)
        xh = xv * r
        err = xh * g - t_ref[...]
        loss_ref[...] += 0.5 * jnp.sum(jnp.mean(err * err, axis=-1, keepdims=True))
        dy = err * (1.0 / D)
        dyg = dy * g
        dx_ref[...] = r * (dyg - xh * jnp.mean(dyg * xh, axis=-1, keepdims=True))
        dg_ref[0:1, :] += jnp.sum(dy * xh, axis=0, keepdims=True)

    return pl.pallas_call(
        body, name=name, grid=(T // tm,),
        in_specs=[_row_spec(tm, D), pl.BlockSpec((1, D), lambda i: (0, 0)), _row_spec(tm, D)],
        out_specs=[pl.BlockSpec((8, LANES), lambda i: (0, 0)), _row_spec(tm, D),
                   pl.BlockSpec((8, D), lambda i: (0, 0))],
        out_shape=[jax.ShapeDtypeStruct((8, LANES), F32), jax.ShapeDtypeStruct((T, D), F32),
                   jax.ShapeDtypeStruct((8, D), F32)],
        compiler_params=_params(1))(x, g3, tgt)


def ple_bwd(dx, pg, e, *, name, tm=1024):
    T = dx.shape[0]

    def body(dx_ref, pg_ref, e_ref, de_ref, dpg_ref):
        gate = jax.nn.sigmoid(pg_ref[...].astype(F32))
        dxv = dx_ref[...]
        de_ref[...] = (dxv * gate).astype(BF16)
        dpg_ref[...] = (dxv * e_ref[...].astype(F32) * gate * (1.0 - gate)).astype(BF16)

    return pl.pallas_call(
        body, name=name, grid=(T // tm,),
        in_specs=[_row_spec(tm, D)] * 3, out_specs=[_row_spec(tm, D)] * 2,
        out_shape=[jax.ShapeDtypeStruct((T, D), BF16)] * 2, compiler_params=_params(1))(dx, pg, e)


def conv_bwd(df, up, convw, convb3, l, *, name, host=None, tm=256):
    T = up.shape[0]
    tc = CONV_TC
    nc = DFF // tc
    hb = tm // CONV_HALO
    nt = T // tm
    rows = tm + 2 * CONV_HALO
    own = slice(CONV_HALO, CONV_HALO + tm)

    hosting = _Hosting(host)
    n_host_in, n_host_out = len(hosting.arrays), len(hosting.out_shapes)

    def body(df_ref, dfn_ref, ua_ref, uap_ref, uan_ref, ub_ref, ubp_ref, ubn_ref, wa_ref, wb_ref, ba_ref, bb_ref,
             *rest):
        host_in = rest[:n_host_in]
        dup_ref, dcw_ref, dcb_ref = rest[n_host_in:n_host_in + 3]
        host_out = rest[n_host_in + 3:n_host_in + 3 + n_host_out]
        sems = rest[n_host_in + 3 + n_host_out:]
        i = pl.program_id(1)
        if hosting.plan:
            first, last = _first_last((nc, nt))

            @pl.when(first)
            def _():
                hosting.begin(host_in, host_out, *sems)

        @pl.when(i == 0)
        def _():
            dcw_ref[...] = jnp.zeros_like(dcw_ref)
            dcb_ref[...] = jnp.zeros_like(dcb_ref)

        def ext_of(c_ref, p_ref, n_ref):
            return jnp.concatenate([jnp.where(i == 0, 0.0, p_ref[...]), c_ref[...],
                                    jnp.where(i == nt - 1, 0.0, n_ref[...])], axis=0)

        ea = ext_of(ua_ref, uap_ref, uan_ref)
        eb = ext_of(ub_ref, ubp_ref, ubn_ref)
        ca, ea1, ea2 = _conv(ea, wa_ref, ba_ref)
        cb, eb1, eb2 = _conv(eb, wb_ref, bb_ref)
        df_ext =jnp.concatenate([jnp.zeros((CONV_HALO, tc), F32), df_ref[...],
                                  jnp.where(i == nt - 1, 0.0, dfn_ref[...])], axis=0)
        cdf = 0.5 * (1.0 + lax.erf(ca * 0.7071067811865476))
        da = df_ext * cb * (cdf + ca * jnp.exp(-0.5 * ca * ca) * 0.3989422804014327)
        db = df_ext * (ca * cdf)

        def finish(h, dc, e, e1, e2, w_ref):
            dup = w_ref[2:3, :] * dc + w_ref[1:2, :] * pltpu.roll(dc, rows - 1, 0)
            dup = dup + w_ref[0:1, :] * pltpu.roll(dc, rows - 2, 0)
            dup_ref[h] = dup[own, :].astype(BF16)
            dco = dc[own, :]
            dcb_ref[h, 0:1, :] += jnp.sum(dco, axis=0, keepdims=True)
            dcw_ref[h, 0:1, :] += jnp.sum(dco * e2[own, :], axis=0, keepdims=True)
            dcw_ref[h, 1:2, :] += jnp.sum(dco * e1[own, :], axis=0, keepdims=True)
            dcw_ref[h, 2:3, :] += jnp.sum(dco * e[own, :], axis=0, keepdims=True)

        finish(0, da, ea, ea1, ea2, wa_ref)
        finish(1, db, eb, eb1, eb2, wb_ref)
        if hosting.plan:
            @pl.when(last)
            def _():
                hosting.finish(host_in, host_out, *sems)

    def nxt(i):
        return jnp.minimum((i + 1) * hb, T // CONV_HALO - 1)

    def prv(i):
        return jnp.maximum(i * hb - 1, 0)

    def up_specs(off):
        return [pl.BlockSpec((tm, tc), lambda j, i: (i, j + off)),
                pl.BlockSpec((CONV_HALO, tc), lambda j, i: (prv(i), j + off)),
                pl.BlockSpec((CONV_HALO, tc), lambda j, i: (nxt(i), j + off))]

    in_specs = [pl.BlockSpec((tm, tc), lambda j, i: (i, j)),
                pl.BlockSpec((CONV_HALO, tc), lambda j, i: (nxt(i), j)),
                *up_specs(0), *up_specs(nc),
                pl.BlockSpec((None, CONV_ROWS, tc), lambda j, i: (l, 0, j)),
                pl.BlockSpec((None, CONV_ROWS, tc), lambda j, i: (l, 0, j + nc)),
                pl.BlockSpec((None, 1, tc), lambda j, i: (l, 0, j)),
                pl.BlockSpec((None, 1, tc), lambda j, i: (l, 0, j + nc))]
    res = pl.pallas_call(
        body, name=name, grid=(nc, nt), in_specs=in_specs + [ANY] * n_host_in,
        out_specs=[pl.BlockSpec((2, tm, tc), lambda j, i: (0, i, j)),
                   pl.BlockSpec((2, 8, tc), lambda j, i: (0, 0, j)),
                   pl.BlockSpec((2, 8, tc), lambda j, i: (0, 0, j))] + [ANY] * n_host_out,
        out_shape=[jax.ShapeDtypeStruct((2, T, DFF), BF16), jax.ShapeDtypeStruct((2, 8, DFF), F32),
                   jax.ShapeDtypeStruct((2, 8, DFF), F32)] + hosting.out_shapes,
        scratch_shapes=hosting.scratch(), input_output_aliases=hosting.aliases(12, 3),
        compiler_params=_params(2))(df, df, up, up, up, up, up, up, convw, convw, convb3, convb3, *hosting.arrays)
    return res[0], res[1], res[2], list(res[3:])


def gate_bwd(dmo, z, yab, *, name, tm=1024):
    T = z.shape[0]

    def body(dmo_ref, zg_ref, y_ref, dz_ref, dy_ref):
        g = jax.nn.sigmoid(zg_ref[...].astype(F32))
        dmo_v = dmo_ref[...].astype(F32)
        dy_ref[...] = (dmo_v * g).astype(BF16)
        dz_ref[...] = (dmo_v * y_ref[...].astype(F32) * g * (1.0 - g)).astype(BF16)

    return pl.pallas_call(
        body, name=name, grid=(T // tm, 2),
        in_specs=[pl.BlockSpec((tm, D), lambda i, s: (i, 0)),
                  pl.BlockSpec((tm, D), lambda i, s: (i, 3 + s)),
                  pl.BlockSpec((tm, D), lambda i, s: (i, s))],
        out_specs=[pl.BlockSpec((tm, D), lambda i, s: (i, 3 + s)),
                   pl.BlockSpec((tm, D), lambda i, s: (i, s))],
        out_shape=[jax.ShapeDtypeStruct((T, 5 * D), BF16), jax.ShapeDtypeStruct((T, 2 * D), BF16)],
        compiler_params=_params(2))(dmo, z, yab)


def mixer_bwd(da, ds, z, dz, wpool, scale3, g3, wsm, wsmT, bT, l, *, name, tm=256):
    T = z.shape[0]
    hb = tm // POOL_HALO
    nt = T // tm

    def body(da_ref, dan_ref, ds_ref, zp_ref, zpp_ref, zu_ref, zv_ref, wp_ref, sc_ref, g_ref, ws_ref, wst_ref,
             b_ref, dzin_ref, dz_ref, dwp_ref, dsc_ref, dws_ref, dbt_ref, dgs_ref, mixed_scr, dvn_scr, db_scr):
        del dzin_ref
        i = pl.program_id(0)

        @pl.when(i == 0)
        def _():
            dwp_ref[...] = jnp.zeros_like(dwp_ref)
            dsc_ref[...] = jnp.zeros_like(dsc_ref)
            dws_ref[...] = jnp.zeros_like(dws_ref)
            dgs_ref[...] = jnp.zeros_like(dgs_ref)
            db_scr[...] = jnp.zeros_like(db_scr)

        zv_p = zp_ref[...].astype(F32)
        ext = jnp.concatenate([jnp.where(i == 0, 0.0, zpp_ref[...].astype(F32)), zv_p], axis=0)
        da_v = da_ref[...].astype(F32)
        da_ext = jnp.concatenate([da_v, jnp.where(i == nt - 1, 0.0, dan_ref[...].astype(F32))], axis=0)
        t = i * tm + lax.broadcasted_iota(jnp.int32, (tm, 1), 0)
        t_ext = i * tm + lax.broadcasted_iota(jnp.int32, (tm + POOL_HALO, 1), 0)
        for g, w in enumerate(POOL_WINDOWS):
            sl = slice(g * PG, (g + 1) * PG)
            pooled = _pooled(ext, zv_p, t, g, w).astype(BF16)
            q = jnp.dot(pooled, wp_ref[g], preferred_element_type=F32)
            dsc_ref[0:1, sl] += jnp.sum(da_v[:, sl] * q, axis=0, keepdims=True)
            dq_ext = (da_ext[:, sl] * sc_ref[:, sl]).astype(BF16)
            dwp_ref[g] += lax.dot_general(pooled, dq_ext[:tm, :], (((0,), (0,)), ((), ())),
                                          preferred_element_type=F32)
            dpool = lax.dot_general(dq_ext, wp_ref[g], (((1,), (1,)), ((), ())), preferred_element_type=F32)
            spread = _winsum_fwd(dpool / jnp.minimum(t_ext + 1, w).astype(F32), w)
            dz_ref[:, sl] = (spread[:tm, :] - dpool[:tm, :]).astype(BF16)

        zu, zv, ds_v = zu_ref[...].astype(F32), zv_ref[...].astype(F32), ds_ref[...].astype(F32)
        gain = g_ref[...]
        gu, gv = _gelu(zu), _gelu(zv)
        rv = lax.rsqrt(jnp.mean(gv * gv, axis=-1, keepdims=True) + EPS)
        vh = gv * rv
        vn = (vh * gain).astype(BF16)
        dmix = ds_v * gu
        dmix_b = dmix.astype(BF16)
        for n in range(tm // CHUNK):
            r = slice(n * CHUNK, (n + 1) * CHUNK)
            db_scr[...] += dmix[r, :]
            for h in range(HEADS):
                cs = slice(h * CHUNK, (h + 1) * CHUNK)
                mixed_scr[r, cs] = jnp.dot(ws_ref[h], vn[r, cs], preferred_element_type=F32) + b_ref[:, h:h + 1]
                dws_ref[h] += lax.dot_general(dmix_b[r, cs], vn[r, cs], (((1,), (1,)), ((), ())),
                                              preferred_element_type=F32)
                dvn_scr[r, cs] = jnp.dot(wst_ref[h], dmix_b[r, cs], preferred_element_type=F32)
        dz_ref[:, D:2 * D] = (ds_v * mixed_scr[...] * _gelu_grad(zu)).astype(BF16)
        dvn = dvn_scr[...]
        dgs_ref[0:1, :] += jnp.sum(dvn * vh, axis=0, keepdims=True)
        dvg = dvn * gain
        dgv = rv * (dvg - vh * jnp.mean(dvg * vh, axis=-1, keepdims=True))
        dz_ref[:, 2 * D:3 * D] = (dgv * _gelu_grad(zv)).astype(BF16)

        @pl.when(i == nt - 1)
        def _():
            tril = (lax.broadcasted_iota(jnp.int32, (CHUNK, CHUNK), 0)
                    >= lax.broadcasted_iota(jnp.int32, (CHUNK, CHUNK), 1)).astype(F32)
            for h in range(HEADS):
                dws_ref[h] = dws_ref[h] * tril
                dbt_ref[:, h:h + 1] = jnp.sum(db_scr[:, h * CHUNK:(h + 1) * CHUNK], axis=1, keepdims=True)

    const4 = lambda i: (l, 0, 0, 0)
    wl = l if wpool.shape[0] > 1 else 0
    in_specs = [
        _row_spec(tm, D),
        pl.BlockSpec((POOL_HALO, D), lambda i: (jnp.minimum((i + 1) * hb, T // POOL_HALO - 1), 0)),
        _row_spec(tm, D),
        _row_spec(tm, D, 0),
        pl.BlockSpec((POOL_HALO, D), lambda i: (jnp.maximum(i * hb - 1, 0), 0)),
        _row_spec(tm, D, 1), _row_spec(tm, D, 2),
        pl.BlockSpec((None, 4, PG, PG), lambda i: (wl, 0, 0, 0)),
        _gain_spec(l), _gain_spec(l),
        pl.BlockSpec((None, HEADS, CHUNK, CHUNK), const4),
        pl.BlockSpec((None, HEADS, CHUNK, CHUNK), const4),
        pl.BlockSpec((None, CHUNK, HEADS), lambda i: (l, 0, 0)),
        ANY,
    ]
    out_specs = [
        pl.BlockSpec((tm, 3 * D), lambda i: (i, 0)),
        pl.BlockSpec((4, PG, PG), lambda i: (0, 0, 0)),
        pl.BlockSpec((8, D), lambda i: (0, 0)),
        pl.BlockSpec((HEADS, CHUNK, CHUNK), lambda i: (0, 0, 0)),
        pl.BlockSpec((CHUNK, HEADS), lambda i: (0, 0)),
        pl.BlockSpec((8, D), lambda i: (0, 0)),
    ]
    out_shape = [
        jax.ShapeDtypeStruct((T, 5 * D), BF16), jax.ShapeDtypeStruct((4, PG, PG), F32),
        jax.ShapeDtypeStruct((8, D), F32), jax.ShapeDtypeStruct((HEADS, CHUNK, CHUNK), F32),
        jax.ShapeDtypeStruct((CHUNK, HEADS), F32), jax.ShapeDtypeStruct((8, D), F32),
    ]
    return pl.pallas_call(
        body, name=name, grid=(nt,), in_specs=in_specs, out_specs=out_specs, out_shape=out_shape,
        scratch_shapes=[pltpu.VMEM((tm, D), F32), pltpu.VMEM((tm, D), F32), pltpu.VMEM((CHUNK, D), F32)],
        input_output_aliases={13: 0}, compiler_params=_params(1))(
            da, da, ds, z, z, z, z, wpool, scale3, g3, wsm, wsmT, bT, dz)


def _row_tile(rows, cols, sub):
    cap = max(sub, (2 * 1024 * 1024) // (4 * cols))
    best = None
    for tr in range(sub, min(rows, cap) + 1, sub):
        if rows % tr == 0:
            best = tr
    return best or rows


def elementwise(fn, ins, out_dtypes, *, name, row_blk_offs=None, rows=None):
    cols = ins[0].shape[1]
    rows = rows or ins[0].shape[0]
    tr = _row_tile(rows, cols, 16)
    offs = row_blk_offs or [0] * len(ins)
    n_in = len(ins)

    def body(*refs):
        outs = fn(*[r[...] for r in refs[:n_in]])
        for o_ref, o in zip(refs[n_in:], outs):
            o_ref[...] = o.astype(o_ref.dtype)

    return pl.pallas_call(
        body, name=name, grid=(rows // tr,),
        in_specs=[pl.BlockSpec((tr, cols), functools.partial(lambda i, o: (i + o * (rows // tr), 0), o=o))
                  for o in offs],
        out_specs=[pl.BlockSpec((tr, cols), lambda i: (i, 0)) for _ in out_dtypes],
        out_shape=[jax.ShapeDtypeStruct((rows, cols), dt) for dt in out_dtypes],
        compiler_params=_params(1))(*ins)


def _adamw(w, g, m, v):
    m = ADAM_B1 * m + (1.0 - ADAM_B1) * g
    v = ADAM_B2 * v + (1.0 - ADAM_B2) * jnp.square(g)
    m_hat = m / (1.0 - ADAM_B1 ** ADAM_STEP)
    v_hat = v / (1.0 - ADAM_B2 ** ADAM_STEP)
    delta = -ADAM_LR * (m_hat / (jnp.sqrt(v_hat) + ADAM_EPS) + ADAM_WD * w)
    return delta, m, v


def _view2d(a):
    return a.reshape(-1, a.shape[-1])


def _place():
    x, y, c = lax.axis_index("x"), lax.axis_index("y"), lax.axis_index("c")
    others = [(1 - x, y), (x, 1 - y), (1 - x, 1 - y)]
    return x, y, c, 2 * x + y, others


def _remote(src, dst, send_sems, recv_sems, k, to):
    return pltpu.make_async_remote_copy(src_ref=src, dst_ref=dst, send_sem=send_sems.at[k], recv_sem=recv_sems.at[k],
                                        device_id=to, device_id_type=MESH)


def _half(ref, axis, j, size, h):
    if len(ref.shape) == 3:
        return ref.at[:, pl.ds(j * size + h * (size // 2), size // 2), :]
    if axis == 0:
        return ref.at[pl.ds(j * size + h * (size // 2), size // 2), :]
    rows = ref.shape[0] // 2
    return ref.at[pl.ds(h * rows, rows), pl.ds(j * size, size)]


def _half_shard_shape(shape, axis, size):
    if len(shape) == 3:
        return (shape[0], size // 2, shape[2])
    if axis == 0:
        return (size // 2, shape[1])
    return (shape[0] // 2, size)


class Exchange:
    def __init__(self, arrays, out_shapes, aliases, n_sems, begin, finish):
        self.arrays, self.out_shapes, self.aliases, self.n_sems = list(arrays), list(out_shapes), aliases, n_sems
        self.begin, self.finish = begin, finish


class _Hosting:
    def __init__(self, plan):
        self.plan = list(plan or [])
        self.arrays = [a for ex in self.plan for a in ex.arrays]
        self.out_shapes = [o for ex in self.plan for o in ex.out_shapes]
        self.n_sems = sum(ex.n_sems for ex in self.plan)

    def scratch(self):
        return [pltpu.SemaphoreType.DMA((self.n_sems,)), pltpu.SemaphoreType.DMA((self.n_sems,))] if self.plan else []

    def aliases(self, in_base, out_base):
        out, i0, o0 = {}, in_base, out_base
        for ex in self.plan:
            out.update({i0 + i: o0 + o for i, o in ex.aliases.items()})
            i0, o0 = i0 + len(ex.arrays), o0 + len(ex.out_shapes)
        return out

    def _each(self, in_refs, out_refs):
        i0 = o0 = s0 = 0
        for ex in self.plan:
            yield ex, in_refs[i0:i0 + len(ex.arrays)], out_refs[o0:o0 + len(ex.out_shapes)], s0
            i0, o0, s0 = i0 + len(ex.arrays), o0 + len(ex.out_shapes), s0 + ex.n_sems

    def begin(self, in_refs, out_refs, send_sems, recv_sems):
        for ex, ins, outs, s0 in self._each(in_refs, out_refs):
            ex.begin(ins, outs, send_sems, recv_sems, s0)

    def finish(self, in_refs, out_refs, send_sems, recv_sems):
        for ex, ins, outs, s0 in self._each(in_refs, out_refs):
            ex.finish(ins, outs, send_sems, recv_sems, s0)


def _first_last(grid):
    ids = [pl.program_id(a) for a in range(len(grid))]
    first = functools.reduce(jnp.logical_and, [i == 0 for i in ids])
    last = functools.reduce(jnp.logical_and, [i == g - 1 for i, g in zip(ids, grid)])
    return first, last


def run_exchanges(plan, *, name):
    host = _Hosting(plan)
    n_in, n_out = len(host.arrays), len(host.out_shapes)

    def body(*refs):
        ins, outs = refs[:n_in], refs[n_in:n_in + n_out]
        send_sems, recv_sems = refs[n_in + n_out:]
        host.begin(ins, outs, send_sems, recv_sems)
        host.finish(ins, outs, send_sems, recv_sems)

    return pl.pallas_call(
        body, name=name, in_specs=[ANY] * n_in, out_specs=[ANY] * n_out, out_shape=host.out_shapes,
        scratch_shapes=host.scratch(), input_output_aliases=host.aliases(0, 0),
        compiler_params=pltpu.CompilerParams(has_side_effects=True))(*host.arrays)


def place_shard(src, l, axis, size, out_dtype, place, *, name):
    shard = src.shape[1:]
    natural = tuple(size * N_CHIPS if a == axis else s for a, s in enumerate(shard))
    if len(shard) == 3:
        blk = (None,) + shard
        grid = (1,)
        in_map = lambda i, pr: (l, 0, 0, 0)
        out_map = lambda i, pr: (0, 0, pr[0], 0)
    else:
        tr = _row_tile(shard[0], shard[1], 16)
        steps = shard[0] // tr
        blk = (None, tr, shard[1])
        grid = (steps,)
        in_map = lambda i, pr: (l, i, 0)
        if axis == 0:
            out_map = lambda i, pr: (0, pr[0] * steps + i, 0)
        else:
            out_map = lambda i, pr: (0, i, pr[0])

    def body(pr_ref, s_ref, o_ref):
        del pr_ref
        o_ref[...] = s_ref[...].astype(o_ref.dtype)

    return pl.pallas_call(
        body, name=name,
        grid_spec=pltpu.PrefetchScalarGridSpec(
            num_scalar_prefetch=1, grid=grid, in_specs=[pl.BlockSpec(blk, in_map)],
            out_specs=pl.BlockSpec(blk, out_map)),
        out_shape=jax.ShapeDtypeStruct((1,) + natural, out_dtype), compiler_params=_params(1))(place, src)


def place_both_layers(src, axis, size, place, *, name):
    rows, cols = src.shape[1], src.shape[2]

    def body(pr_ref, s_ref, o_ref):
        del pr_ref
        o_ref[...] = s_ref[...]

    return pl.pallas_call(
        body, name=name,
        grid_spec=pltpu.PrefetchScalarGridSpec(
            num_scalar_prefetch=1, grid=(2,), in_specs=[pl.BlockSpec((None, rows, cols), lambda lyr, pr: (lyr, 0, 0))],
            out_specs=pl.BlockSpec((None, rows, cols), lambda lyr, pr: (lyr, 0, pr[0]))),
        out_shape=jax.ShapeDtypeStruct((2, rows, cols * N_CHIPS), src.dtype), compiler_params=_params(1))(place, src)


def gather_exchange(arrays, geom):
    n = len(arrays)

    def begin(ins, outs, send_sems, recv_sems, s0):
        x, y, c, j, others = _place()
        for t, (axis, size) in enumerate(geom):
            mine = _half(outs[t].at[0], axis, j, size, c)
            for k, (ox, oy) in enumerate(others):
                _remote(mine, mine, send_sems, recv_sems, s0 + 6 * t + k, (ox, oy, c)).start()

    def finish(ins, outs, send_sems, recv_sems, s0):
        x, y, c, j, others = _place()
        sib = (x, y, 1 - c)
        passed = []
        for t, (axis, size) in enumerate(geom):
            for k, (ox, oy) in enumerate(others):
                landed = _half(outs[t].at[0], axis, 2 * ox + oy, size, c)
                _remote(landed, landed, send_sems, recv_sems, s0 + 6 * t + k, (ox, oy, c)).wait_recv()
                fwd = _remote(landed, landed, send_sems, recv_sems, s0 + 6 * t + 3 + k, sib)
                fwd.start()
                passed.append(fwd)
        for t, (axis, size) in enumerate(geom):
            for k, (ox, oy) in enumerate(others):
                got = _half(outs[t].at[0], axis, 2 * ox + oy, size, 1 - c)
                _remote(got, got, send_sems, recv_sems, s0 + 6 * t + 3 + k, sib).wait_recv()
        for fwd in passed:
            fwd.wait_send()
        for t, (axis, size) in enumerate(geom):
            mine = _half(outs[t].at[0], axis, j, size, c)
            for k, (ox, oy) in enumerate(others):
                _remote(mine, mine, send_sems, recv_sems, s0 + 6 * t + k, (ox, oy, c)).wait_send()

    return Exchange(arrays, [jax.ShapeDtypeStruct(a.shape, a.dtype) for a in arrays], {t: t for t in range(n)},
                    6 * n, begin, finish)


def gather_by_layer_exchange(array, axis, size):
    def blocks(out, others, lyr):
        return [_block(out.at[lyr], axis, 2 * ox + oy, size) for (ox, oy) in others]

    def begin(ins, outs, send_sems, recv_sems, s0):
        x, y, c, j, others = _place()
        mine = _block(outs[0].at[c], axis, j, size)
        for k, (ox, oy) in enumerate(others):
            _remote(mine, mine, send_sems, recv_sems, s0 + k, (ox, oy, c)).start()

    def finish(ins, outs, send_sems, recv_sems, s0):
        x, y, c, j, others = _place()
        sib = (x, y, 1 - c)
        passed = []
        for k, ((ox, oy), landed) in enumerate(zip(others, blocks(outs[0], others, c))):
            _remote(landed, landed, send_sems, recv_sems, s0 + k, (ox, oy, c)).wait_recv()
            fwd = _remote(landed, landed, send_sems, recv_sems, s0 + 3 + k, sib)
            fwd.start()
            passed.append(fwd)
        for k, got in enumerate(blocks(outs[0], others, 1 - c)):
            _remote(got, got, send_sems, recv_sems, s0 + 3 + k, sib).wait_recv()
        for fwd in passed:
            fwd.wait_send()
        mine = _block(outs[0].at[c], axis, j, size)
        for k, (ox, oy) in enumerate(others):
            _remote(mine, mine, send_sems, recv_sems, s0 + k, (ox, oy, c)).wait_send()

    return Exchange([array], [jax.ShapeDtypeStruct(array.shape, array.dtype)], {0: 0}, 6, begin, finish)


def swap_exchange(grads, geom):
    def pieces(t, g, dst, h):
        axis, size = geom[t]
        if len(g.shape) == 2 and axis == 1:
            rows = g.shape[0] // 2
            return [(g.at[pl.ds(h * rows, rows), :], dst)]
        return [(_half(g, axis, jb, size, h), dst.at[jb]) for jb in range(N_CHIPS)]

    counts = [1 if (len(g.shape) == 3 and a == 1) else N_CHIPS for g, (a, _) in zip(grads, geom)]
    bases = [sum(counts[:t]) for t in range(len(grads))]

    def copies(ins, outs, send_sems, recv_sems, s0):
        x, y, c, _, _ = _place()
        cps = []
        for t in range(len(grads)):
            for q, (src, dst) in enumerate(pieces(t, ins[t].at[0], outs[t], 1 - c)):
                cps.append(_remote(src, dst, send_sems, recv_sems, s0 + bases[t] + q, (x, y, 1 - c)))
        return cps

    def begin(*a):
        for cp in copies(*a):
            cp.start()

    def finish(*a):
        for cp in copies(*a):
            cp.wait()

    out_shapes = []
    for g, (axis, size) in zip(grads, geom):
        shp = g.shape[1:]
        if len(shp) == 2 and axis == 1:
            out_shapes.append(jax.ShapeDtypeStruct((shp[0] // 2, shp[1]), g.dtype))
        else:
            out_shapes.append(jax.ShapeDtypeStruct((N_CHIPS,) + _half_shard_shape(shp, axis, size), g.dtype))
    return Exchange(grads, out_shapes, {}, sum(counts), begin, finish)


def scatter_exchange(parts, geom, shapes):
    def copies(ins, outs, send_sems, recv_sems, s0):
        x, y, c, j, others = _place()
        cps = []
        for t, ((axis, size), shp) in enumerate(zip(geom, shapes)):
            for k, (ox, oy) in enumerate(others):
                jp = 2 * ox + oy
                src = ins[t].at[:, pl.ds(jp * size, size)] if (len(shp) == 2 and axis == 1) else ins[t].at[jp]
                cps.append(_remote(src, outs[t].at[k], send_sems, recv_sems, s0 + 3 * t + k, (ox, oy, c)))
        return cps

    def begin(*a):
        for cp in copies(*a):
            cp.start()

    def finish(*a):
        for cp in copies(*a):
            cp.wait_recv()
        for cp in copies(*a):
            cp.wait_send()

    out_shapes = [jax.ShapeDtypeStruct((3,) + _half_shard_shape(shp, axis, size), p.dtype)
                  for p, (axis, size), shp in zip(parts, geom, shapes)]
    return Exchange(parts, out_shapes, {}, 3 * len(parts), begin, finish)


def share_exchange(grads, which):
    n = len(which)

    def my_half(refs, t, h):
        lyr = refs[which[t][0]].at[which[t][1]]
        if len(lyr.shape) == 3:
            rows = lyr.shape[1] // 2
            return lyr.at[:, pl.ds(h * rows, rows), :]
        rows = lyr.shape[0] // 2
        return lyr.at[pl.ds(h * rows, rows), :]

    def begin(ins, outs, send_sems, recv_sems, s0):
        x, y, c, _, _ = _place()
        for t in range(n):
            mine = my_half(outs, t, c)
            _remote(mine, mine, send_sems, recv_sems, s0 + t, (x, y, 1 - c)).start()

    def finish(ins, outs, send_sems, recv_sems, s0):
        x, y, c, _, _ = _place()
        for t in range(n):
            got = my_half(outs, t, 1 - c)
            _remote(got, got, send_sems, recv_sems, s0 + t, (x, y, 1 - c)).wait_recv()
        for t in range(n):
            mine = my_half(outs, t, c)
            _remote(mine, mine, send_sems, recv_sems, s0 + t, (x, y, 1 - c)).wait_send()

    return Exchange(grads, [jax.ShapeDtypeStruct(g.shape, g.dtype) for g in grads],
                    {t: t for t in range(len(grads))}, n, begin, finish)


def all_reduce_small(s):
    rows = s.shape[0]
    half = rows // 2
    assert half % 8 == 0

    def body(s_ref, o_ref, a_ref, b_ref, p_ref, send_sems, recv_sems):
        x, y, c, j, others = _place()
        sib = (x, y, 1 - c)
        swap = _remote(s_ref, a_ref, send_sems, recv_sems, 0, sib)
        swap.start()
        swap.wait()
        p_ref[...] = s_ref[...] + a_ref[...]
        mine = pl.ds(pl.multiple_of(c * half, 8), half)
        b_ref[j] = p_ref[mine, :]
        cps = [_remote(p_ref.at[mine, :], b_ref.at[j], send_sems, recv_sems, 1 + k, (ox, oy, c))
               for k, (ox, oy) in enumerate(others)]
        for cp in cps:
            cp.start()
        for k, (ox, oy) in enumerate(others):
            slot = b_ref.at[2 * ox + oy]
            _remote(slot, slot, send_sems, recv_sems, 1 + k, (ox, oy, c)).wait_recv()
        for cp in cps:
            cp.wait_send()
        o_ref[mine, :] = ((b_ref[0] + b_ref[1]) + b_ref[2]) + b_ref[3]
        back = _remote(o_ref.at[mine, :], o_ref.at[mine, :], send_sems, recv_sems, 4, sib)
        back.start()
        back.wait_send()
        theirs = pl.ds(pl.multiple_of((1 - c) * half, 8), half)
        _remote(o_ref.at[theirs, :], o_ref.at[theirs, :], send_sems, recv_sems, 4, sib).wait_recv()

    vmem = pl.BlockSpec(memory_space=pltpu.VMEM)
    return pl.pallas_call(
        body, name="all_reduce_small", in_specs=[vmem], out_specs=vmem,
        out_shape=jax.ShapeDtypeStruct((rows, LANES), F32),
        scratch_shapes=[pltpu.VMEM((rows, LANES), F32), pltpu.VMEM((N_CHIPS, half, LANES), F32),
                        pltpu.VMEM((rows, LANES), F32), pltpu.SemaphoreType.DMA((5,)),
                        pltpu.SemaphoreType.DMA((5,))],
        compiler_params=pltpu.CompilerParams(vmem_limit_bytes=VMEM_LIMIT, has_side_effects=True))(s)


def pair_sum(g, got, axis, size, place, *, name):
    shp = g.shape[1:]
    if len(shp) == 3:
        hs = size // 2
        grid = (N_CHIPS,)
        g_spec = pl.BlockSpec((None, shp[0], hs, shp[2]), lambda jb, pr: (0, 0, 2 * jb + pr[1], 0))
        r_spec = pl.BlockSpec((None, shp[0], hs, shp[2]), lambda jb, pr: (jb, 0, 0, 0))
    elif axis == 0:
        hs = size // 2
        tr = _row_tile(hs, shp[1], 16)
        steps = hs // tr
        grid = (N_CHIPS, steps)
        g_spec = pl.BlockSpec((None, tr, shp[1]), lambda jb, i, pr: (0, (2 * jb + pr[1]) * steps + i, 0))
        r_spec = pl.BlockSpec((None, tr, shp[1]), lambda jb, i, pr: (jb, i, 0))
    else:
        rows = shp[0] // 2
        tr = _row_tile(rows, shp[1], 16)
        steps = rows // tr
        grid = (steps,)
        g_spec = pl.BlockSpec((None, tr, shp[1]), lambda i, pr: (0, pr[1] * steps + i, 0))
        r_spec = pl.BlockSpec((tr, shp[1]), lambda i, pr: (i, 0))

    def body(pr_ref, g_ref, r_ref, o_ref):
        del pr_ref
        o_ref[...] = (g_ref[...].astype(F32) + r_ref[...].astype(F32)).astype(BF16)

    return pl.pallas_call(
        body, name=name,
        grid_spec=pltpu.PrefetchScalarGridSpec(num_scalar_prefetch=1, grid=grid, in_specs=[g_spec, r_spec],
                                               out_specs=r_spec),
        out_shape=jax.ShapeDtypeStruct(got.shape, BF16), compiler_params=_params(len(grid)))(place, g, got)


def chip_sum(part, slots, shp, axis, size, l, place, out, *, name):
    shard = _shard_shape(shp, axis, size)
    hshape = slots.shape[1:]
    if len(shp) == 3:
        grid = (1,)
        p_spec = pl.BlockSpec((None,) + hshape, lambda i, pr: (pr[0], 0, 0, 0))
        s_specs = [pl.BlockSpec((None,) + hshape, functools.partial(lambda i, pr, k: (k, 0, 0, 0), k=k))
                   for k in range(3)]
        o_spec = pl.BlockSpec((None,) + hshape, lambda i, pr: (l, 0, pr[1], 0))
    else:
        tr = _row_tile(hshape[0], hshape[1], 16)
        steps = hshape[0] // tr
        grid = (steps,)
        if axis == 0:
            p_spec = pl.BlockSpec((None, tr, hshape[1]), lambda i, pr: (pr[0], i, 0))
        else:
            p_spec = pl.BlockSpec((tr, hshape[1]), lambda i, pr: (i, pr[0]))
        s_specs = [pl.BlockSpec((None, tr, hshape[1]), functools.partial(lambda i, pr, k: (k, i, 0), k=k))
                   for k in range(3)]
        o_spec = pl.BlockSpec((None, tr, hshape[1]), lambda i, pr: (l, pr[1] * steps + i, 0))
    has_out = out is not None

    def body(pr_ref, p_ref, s0_ref, s1_ref, s2_ref, *rest):
        del pr_ref
        rest[-1][...] = ((p_ref[...].astype(F32) + s0_ref[...].astype(F32)) + s1_ref[...].astype(F32)) \
            + s2_ref[...].astype(F32)

    return pl.pallas_call(
        body, name=name,
        grid_spec=pltpu.PrefetchScalarGridSpec(
            num_scalar_prefetch=1, grid=grid, in_specs=[p_spec] + s_specs + ([ANY] if has_out else []),
            out_specs=o_spec),
        out_shape=jax.ShapeDtypeStruct((2,) + shard, F32), input_output_aliases={5: 0} if has_out else {},
        compiler_params=_params(1))(place, part, slots, slots, slots, *([out] if has_out else []))


GEOM = {name: (axis, size) for (name, _, axis, size) in BIG}
SHAPE = {name: shape for (name, shape, _, _) in BIG}
RIDES_IN_PROJ_L0 = ((0, ("w_pool", "w_branch_a", "w_branch_b", "w_out", "w_up")),)
RIDES_UP_PROJ_L0 = ((0, ("w_down", "w_ple_gate", "w_ple")), (1, ("w_in",)))
RIDES_DOWN_PROJ_L0 = ((1, ("w_pool", "w_branch_a", "w_branch_b", "w_out")),)
RIDES_IN_PROJ_L1 = ((1, ("w_up", "w_down", "w_ple_gate", "w_ple")),)
EARLY_GRADS_L0 = ("w_ple", "w_ple_gate", "w_down", "w_up")
LATE_GRADS_L0 = ("w_out", "w_branch_a", "w_branch_b", "w_pool", "w_in")


def _swap_of(G, names):
    return swap_exchange([G[k] for k in names], [GEOM[k] for k in names])


def _after_swap(G, names, got, place, tag):
    parts = [pair_sum(G[k], r, *GEOM[k], place, name=f"pair_sum_{k}_{tag}") for k, r in zip(names, got)]
    return scatter_exchange(parts, [GEOM[k] for k in names], [SHAPE[k] for k in names]), parts


def _reduce_start(G, names, place, tag):
    got = run_exchanges([_swap_of(G, names)], name=f"swap_halves_{tag}")
    return _after_swap(G, names, got, place, tag)


def _reduce_end(names, parts, slots, place, l, reduced):
    for k, q, s in zip(names, parts, slots):
        reduced[k] = chip_sum(q, s, SHAPE[k], *GEOM[k], l, place, reduced.get(k), name=f"chip_sum_{k}_l{l}")


def _local_step(x, p2, tgt, W0, W1, conv_w, small, place):
    T = x.shape[0]
    as3 = lambda a: a.reshape(2, 1, a.shape[-1])
    mix3, scale3, sgu3 = as3(small["mix_norm"]), as3(small["pool_scale"]), as3(small["sgu_norm"])
    ffn3, ple3, convb3 = as3(small["ffn_norm"]), as3(small["ple_norm"]), as3(small["conv_b"])
    tril = jnp.tril(jnp.ones((CHUNK, CHUNK), F32))
    ws_masked = small["w_spatial"] * tril
    wsm = ws_masked.astype(BF16)
    wsmT = jnp.swapaxes(ws_masked, -1, -2).astype(BF16)
    bT = jnp.swapaxes(small["b_spatial"], -1, -2)
    final3 = small["final_norm"].reshape(1, D)
    W = [dict(W0), dict(W1)]

    def riders(groups):
        return [gather_exchange([W[lyr][k] for k in names], [GEOM[k] for k in names]) for lyr, names in groups]

    def landed(groups, got):
        for lyr, names in groups:
            W[lyr].update(zip(names, got[:len(names)]))
            got = got[len(names):]

    saved = []
    hb = norm_fwd(x, mix3, 0, name="mix_norm_fwd_l0")
    for l in range(2):
        n = lambda s: f"{s}_l{l}"
        Wl = W[l]
        groups = RIDES_IN_PROJ_L0 if l == 0 else RIDES_IN_PROJ_L1
        z, got = mm_nn(hb, Wl["w_in"], 0, name=n("in_proj"), rows=T, tn=1280, tm=2048, out_dtype=BF16,
                       host=riders(groups))
        landed(groups, got)
        a_in = pool_fwd(z, Wl["w_pool"], scale3, l, name=n("pool_fwd"))
        s_in = sgu_fwd(z, sgu3, wsm, bT, l, name=n("sgu_fwd"))
        yab = mm_nn(a_in, Wl["w_branch_a"], 0, name=n("branch_a"), rows=T, out_cols=2 * D, out_dtype=BF16, tm=2048)
        yab = mm_nn(s_in, Wl["w_branch_b"], 0, name=n("branch_b"), rows=T, out=yab, out_cols=2 * D, out_col_off=D,
                    out_dtype=BF16, tm=2048)
        mo = gate_fwd(z, yab, name=n("gate_fwd"))
        x1, h2b = mm_nn(mo, Wl["w_out"], 0, name=n("out_proj"), rows=T, resid=x, norm_gain=ffn3[l:l + 1])
        if l == 0:
            up, got = mm_nn(h2b, Wl["w_up"], 0, name=n("up_proj"), rows=T, tn=DFF, host=riders(RIDES_UP_PROJ_L0))
            landed(RIDES_UP_PROJ_L0, got)
        else:
            up = mm_nn(h2b, Wl["w_up"], 0, name=n("up_proj"), rows=T, tn=DFF)
        f = conv_fwd(up, conv_w, convb3, l, name=n("conv_fwd"))
        if l == 0:
            (x2, h3b), got = mm_nn(f, Wl["w_down"], 0, name=n("down_proj"), rows=T, resid=x1,
                                   norm_gain=ple3[l:l + 1], host=riders(RIDES_DOWN_PROJ_L0))
            landed(RIDES_DOWN_PROJ_L0, got)
        else:
            x2, h3b = mm_nn(f, Wl["w_down"], 0, name=n("down_proj"), rows=T, resid=x1, norm_gain=ple3[l:l + 1])
        pg = mm_nn(h3b, Wl["w_ple_gate"], 0, name=n("ple_gate_proj"), rows=T, out_dtype=BF16, tm=2048)
        e = mm_nn(p2, Wl["w_ple"], 0, name=n("ple_proj"), rows=T, a_row_off=l * T, out_dtype=BF16)
        saved.append(dict(x=x, hb=hb, z=z, a_in=a_in, s_in=s_in, yab=yab, mo=mo, x1=x1, h2b=h2b, up=up, f=f,
                          x2=x2, h3b=h3b, pg=pg, e=e))
        if l == 0:
            x, hb = ple_fwd(x2, pg, e, mix3, 1, name=n("ple_fwd"))
        else:
            x = ple_fwd(x2, pg, e, None, 0, name=n("ple_fwd"))

    loss_acc, dx, dg_final = loss_head(x, final3, tgt, name="loss_head")

    small_grads = [None, None]
    all_names = [t[0] for t in BIG]
    reduced = {}
    swap1 = G1 = scatter1 = parts1 = slots1 = None
    for l in (1, 0):
        n = lambda s: f"{s}_l{l}"
        a, Wl, G = saved[l], W[l], {}
        de, dpg = ple_bwd(dx, a["pg"], a["e"], name=n("ple_bwd"))
        G["w_ple"] = mm_tn(p2, de, name=n("d_w_ple"), rows=T, ka=PG, nb=D, a_row_off=l * T)
        G["w_ple_gate"] = mm_tn(a["h3b"], dpg, name=n("d_w_ple_gate"), rows=T, ka=D, nb=D)
        if l == 0:
            (dx2, dg_ple), got = mm_nt(dpg, Wl["w_ple_gate"], 0, name=n("ple_norm_bwd"), rows=T,
                                       norm_bwd_of=(a["x2"], ple3, l, dx), host=[swap1])
            scatter1, parts1 = _after_swap(G1, all_names, got, place, "l1")
        else:
            dx2, dg_ple = mm_nt(dpg, Wl["w_ple_gate"], 0, name=n("ple_norm_bwd"), rows=T,
                                norm_bwd_of=(a["x2"], ple3, l, dx))
        df = mm_nt(dx2, Wl["w_down"], 0, name=n("d_ffn_act"), rows=T, out_dtype=F32)
        G["w_down"] = mm_tn(a["f"], dx2, name=n("d_w_down"), rows=T, ka=DFF, nb=D, tm=1408)
        if l == 0:
            dup, dcw, dcb, slots1 = conv_bwd(df, a["up"], conv_w, convb3, l, name=n("conv_bwd"), host=[scatter1])
        else:
            dup, dcw, dcb, _ = conv_bwd(df, a["up"], conv_w, convb3, l, name=n("conv_bwd"))
        G["w_up"] = mm_tn(a["h2b"], dup, name=n("d_w_up"), rows=T, ka=D, nb=2 * DFF, tn=DFF, tk=1024)
        if l == 0:
            scatter_early, parts_early = _reduce_start(G, EARLY_GRADS_L0, place, "l0_early")
            (dx1, dg_ffn), slots_early = mm_nt(dup, Wl["w_up"], 0, name=n("ffn_norm_bwd"), rows=T, tk=1408,
                                               norm_bwd_of=(a["x1"], ffn3, l, dx2), host=[scatter_early])
        else:
            dx1, dg_ffn = mm_nt(dup, Wl["w_up"], 0, name=n("ffn_norm_bwd"), rows=T, tk=1408,
                                norm_bwd_of=(a["x1"], ffn3, l, dx2))
        dmo = mm_nt(dx1, Wl["w_out"], 0, name=n("d_gated"), rows=T)
        G["w_out"] = mm_tn(a["mo"], dx1, name=n("d_w_out"), rows=T, ka=D, nb=D)
        dz, dyab = gate_bwd(dmo, a["z"], a["yab"], name=n("gate_bwd"))
        G["w_branch_a"] = mm_tn(a["a_in"], dyab, name=n("d_w_branch_a"), rows=T, ka=D, nb=D)
        G["w_branch_b"] = mm_tn(a["s_in"], dyab, name=n("d_w_branch_b"), rows=T, ka=D, nb=D, b_col_off=D)
        da = mm_nt(dyab, Wl["w_branch_a"], 0, name=n("d_pool_out"), rows=T, kdim=D)
        ds = mm_nt(dyab, Wl["w_branch_b"], 0, name=n("d_sgu_out"), rows=T, kdim=D, a_col_off=D)
        dz, dwp, dsc, dws, dbt, dgs = mixer_bwd(da, ds, a["z"], dz, Wl["w_pool"], scale3, sgu3, wsm, wsmT, bT, l,
                                                name=n("mixer_bwd"))
        G["w_pool"] = dwp.astype(BF16)[None]
        G["w_in"] = mm_tn(a["hb"], dz, name=n("d_w_in"), rows=T, ka=D, nb=5 * D, tn=2560, tk=1024)
        if l == 0:
            scatter_late, parts_late = _reduce_start(G, LATE_GRADS_L0, place, "l0_late")
            _reduce_end(all_names, parts1, slots1, place, 1, reduced)
            _reduce_end(EARLY_GRADS_L0, parts_early, slots_early, place, 0, reduced)
            done = [(t, 1) for t in range(len(all_names))] + [(all_names.index(k), 0) for k in EARLY_GRADS_L0]
            (dx, dg_mix), got = mm_nt(
                dz, Wl["w_in"], 0, name=n("mix_norm_bwd"), rows=T, tk=1280, norm_bwd_of=(a["x"], mix3, l, dx1),
                host=[scatter_late, share_exchange([reduced[k] for k in all_names], done)])
            slots_late = got[:len(LATE_GRADS_L0)]
            reduced.update(zip(all_names, got[len(LATE_GRADS_L0):]))
        else:
            dx, dg_mix = mm_nt(dz, Wl["w_in"], 0, name=n("mix_norm_bwd"), rows=T, tk=1280,
                               norm_bwd_of=(a["x"], mix3, l, dx1))
            swap1, G1 = _swap_of(G, all_names), G
        small_grads[l] = dict(
            mix_norm=dg_mix[0], pool_scale=dsc[0], sgu_norm=dgs[0], w_spatial=dws, b_spatial=dbt.T,
            ffn_norm=dg_ffn[0], conv_b=jnp.concatenate([dcb[0, 0], dcb[1, 0]]), ple_norm=dg_ple[0],
            conv_w=jnp.concatenate([dcw[0, :3], dcw[1, :3]], axis=1))
    _reduce_end(LATE_GRADS_L0, parts_late, slots_late, place, 0, reduced)
    return loss_acc, dx, reduced, small_grads, dg_final[0]


SMALL_ORDER = ("mix_norm", "pool_scale", "sgu_norm", "w_spatial", "b_spatial", "ffn_norm", "conv_b", "ple_norm",
               "conv_w")


def _pack_rows(pieces, row_multiple):
    flat = jnp.concatenate([a.reshape(-1) for a in pieces])
    rows = -(-flat.shape[0] // LANES)
    rows = -(-rows // row_multiple) * row_multiple
    return jnp.pad(flat, (0, rows * LANES - flat.shape[0])).reshape(rows, LANES)


def _unpack(flat, shapes):
    out, off = [], 0
    for shp in shapes:
        size = 1
        for s in shp:
            size *= s
        out.append(flat[off:off + size].reshape(shp))
        off += size
    return out


def kernel(x, p, mix_norm, w_in, w_pool, pool_scale, sgu_norm, w_spatial, b_spatial, w_branch_a, w_branch_b, w_out, ffn_norm, w_up, conv_w, conv_b, w_down, ple_norm, w_ple_gate, w_ple, final_norm, loss_target, m_mix_norm, m_w_in, m_w_pool, m_pool_scale, m_sgu_norm, m_w_spatial, m_b_spatial, m_w_branch_a, m_w_branch_b, m_w_out, m_ffn_norm, m_w_up, m_conv_w, m_conv_b, m_w_down, m_ple_norm, m_w_ple_gate, m_w_ple, m_final_norm, v_mix_norm, v_w_in, v_w_pool, v_pool_scale, v_sgu_norm, v_w_spatial, v_b_spatial, v_w_branch_a, v_w_branch_b, v_w_out, v_ffn_norm, v_w_up, v_conv_w, v_conv_b, v_w_down, v_ple_norm, v_w_ple_gate, v_w_ple, v_final_norm):
    names = ["mix_norm", "w_in", "w_pool", "pool_scale", "sgu_norm", "w_spatial", "b_spatial", "w_branch_a",
             "w_branch_b", "w_out", "ffn_norm", "w_up", "conv_w", "conv_b", "w_down", "ple_norm", "w_ple_gate",
             "w_ple", "final_norm"]
    w = dict(zip(names, [mix_norm, w_in, w_pool, pool_scale, sgu_norm, w_spatial, b_spatial, w_branch_a, w_branch_b,
                         w_out, ffn_norm, w_up, conv_w, conv_b, w_down, ple_norm, w_ple_gate, w_ple, final_norm]))
    m = dict(zip(names, [m_mix_norm, m_w_in, m_w_pool, m_pool_scale, m_sgu_norm, m_w_spatial, m_b_spatial,
                         m_w_branch_a, m_w_branch_b, m_w_out, m_ffn_norm, m_w_up, m_conv_w, m_conv_b, m_w_down,
                         m_ple_norm, m_w_ple_gate, m_w_ple, m_final_norm]))
    v = dict(zip(names, [v_mix_norm, v_w_in, v_w_pool, v_pool_scale, v_sgu_norm, v_w_spatial, v_b_spatial,
                         v_w_branch_a, v_w_branch_b, v_w_out, v_ffn_norm, v_w_up, v_conv_w, v_conv_b, v_w_down,
                         v_ple_norm, v_w_ple_gate, v_w_ple, v_final_norm]))
    T = x.shape[1]
    chip = 2 * lax.axis_index("x") + lax.axis_index("y")
    place = jnp.stack([chip, lax.axis_index("c")]).astype(jnp.int32)

    big_names = [t[0] for t in BIG]
    placed = [{k: place_shard(w[k], l, *GEOM[k], BF16, place, name=f"place_{k}_l{l}") for k in big_names}
              for l in range(2)]
    conv_w8 = jnp.pad(conv_w, ((0, 0), (0, CONV_ROWS - conv_w.shape[1]), (0, 0)))
    conv_placed = place_both_layers(conv_w8, 1, conv_w.shape[2], place, name="place_conv_w")
    w_in0, conv_w_all = run_exchanges([gather_exchange([placed[0]["w_in"]], [GEOM["w_in"]]),
                                       gather_by_layer_exchange(conv_placed, 1, conv_w.shape[2])],
                                      name="gather_first_weights")
    placed[0]["w_in"] = w_in0

    small = {k: w[k] for k in ("mix_norm", "pool_scale", "sgu_norm", "w_spatial", "b_spatial", "ffn_norm",
                               "conv_b", "ple_norm", "final_norm")}
    loss_acc, dx, reduced, small_grads, dg_final = _local_step(
        x.reshape(T, D), p.reshape(2 * T, p.shape[-1]), loss_target.reshape(T, D), placed[0], placed[1], conv_w_all,
        small, place)
    loss = lax.psum(loss_acc[0, 0], ("x", "y", "c"))
    full = run_exchanges([share_exchange([reduced[k] for k in big_names],
                                         [(big_names.index(k), 0) for k in LATE_GRADS_L0])], name="share_last_halves")
    grads = dict(zip(big_names, full))

    pieces = [small_grads[l][k] for l in range(2) for k in SMALL_ORDER] + [dg_final]
    shapes = [a.shape for a in pieces]
    total = all_reduce_small(_pack_rows(pieces, 16)).reshape(-1)
    summed = _unpack(total, shapes)
    per_layer = {k: jnp.stack([summed[i], summed[len(SMALL_ORDER) + i]]) for i, k in enumerate(SMALL_ORDER)}
    for k in ("mix_norm", "pool_scale", "sgu_norm", "w_spatial", "b_spatial", "ffn_norm", "conv_b", "ple_norm"):
        grads[k] = per_layer[k]
    grads["final_norm"] = summed[-1]
    cw = conv_w.shape[2]
    grads["conv_w"] = lax.dynamic_slice_in_dim(per_layer["conv_w"], chip * cw, cw, axis=2)

    delta, new_m, new_v = {}, {}, {}
    for name in big_names:
        shp = w[name].shape
        d_, m_, v_, g_ = elementwise(lambda w_, g_, m_, v_: (*_adamw(w_, g_, m_, v_), g_),
                                     [_view2d(a) for a in (w[name], grads[name], m[name], v[name])],
                                     [F32, F32, F32, F32], name=f"adamw_{name}")
        delta[name], new_m[name], new_v[name] = d_.reshape(shp), m_.reshape(shp), v_.reshape(shp)
        grads[name] = g_.reshape(shp)
    small_names = [k for k in names if k not in big_names]
    small_shapes = [w[k].shape for k in small_names]
    packed = [_pack_rows([src[k] for k in small_names], 8) for src in (w, grads, m, v)]
    outs = elementwise(_adamw, packed, [F32, F32, F32], name="adamw_small")
    for dst, o in zip((delta, new_m, new_v), outs):
        for k, a in zip(small_names, _unpack(o.reshape(-1), small_shapes)):
            dst[k] = a

    return (loss, dx.reshape(1, T, D), *[grads[k] for k in names], *[delta[k] for k in names],
            *[new_m[k] for k in names], *[new_v[k] for k in names])
```

```python
import functools

import jax
import jax.numpy as jnp
from jax import lax
from jax.experimental import pallas as pl
from jax.experimental.pallas import tpu as pltpu

F32 = jnp.float32
BF16 = jnp.bfloat16
EPS = 1e-6
D = 1024
POOL_WINDOWS = (2, 4, 8, 16)
PG = 256
POOL_HALO = 16
CHUNK = 128
HEADS = 8
DFF = 2816
CONV_HALO = 8
CONV_TC = 1408
N_CHIPS = 4
LANES = 128
VMEM_LIMIT = 56 * 1024 * 1024
MESH = pl.DeviceIdType.MESH
ANY = pl.BlockSpec(memory_space=pl.ANY)

ADAM_LR = 0.001
ADAM_B1 = 0.9
ADAM_B2 = 0.999
ADAM_EPS = 1e-08
ADAM_WD = 0.01
ADAM_STEP = 10

BIG = (
    ("w_in", (D, 5 * D), 1, 5 * D // N_CHIPS),
    ("w_pool", (4, PG, PG), 1, PG // N_CHIPS),
    ("w_branch_a", (D, D), 0, D // N_CHIPS),
    ("w_branch_b", (D, D), 0, D // N_CHIPS),
    ("w_out", (D, D), 0, D // N_CHIPS),
    ("w_up", (D, 2 * DFF), 1, 2 * DFF // N_CHIPS),
    ("w_down", (DFF, D), 0, DFF // N_CHIPS),
    ("w_ple_gate", (D, D), 0, D // N_CHIPS),
    ("w_ple", (PG, D), 1, D // N_CHIPS),
)
CONV_ROWS = 8


def _params(n_axes):
    return pltpu.CompilerParams(dimension_semantics=("arbitrary",) * n_axes, vmem_limit_bytes=VMEM_LIMIT)


def _gelu(x):
    return 0.5 * x * (1.0 + lax.erf(x * 0.7071067811865476))


def _gelu_grad(x):
    return 0.5 * (1.0 + lax.erf(x * 0.7071067811865476)) + x * jnp.exp(-0.5 * x * x) * 0.3989422804014327


def _shard_shape(shape, axis, size):
    return tuple(size if a == axis else s for a, s in enumerate(shape))


def _block(ref, axis, j, size):
    idx = tuple(pl.ds(j * size, size) if a == axis else slice(None) for a in range(len(ref.shape)))
    return ref.at[idx]


def mm_nn(a, w, l, *, name, rows, out_dtype=F32, resid=None, a_row_off=0, out=None, out_cols=None,
          out_col_off=0, norm_gain=None, host=None, tm=1024, tn=None, tk=None):
    K, N = w.shape[1], w.shape[2]
    tn = tn or N
    tk = tk or K
    nk = K // tk
    out_cols = out_cols or N
    assert rows % tm == 0 and N % tn == 0 and K % tk == 0 and out_col_off % tn == 0 and a_row_off % tm == 0
    has_resid, has_out, has_norm = resid is not None, out is not None, norm_gain is not None
    assert not has_norm or (tn == N and not has_out)
    grid = (N // tn, rows // tm, nk)
    hosting = _Hosting(host)
    n_in = 2 + has_resid + has_norm + has_out
    n_host_in, n_host_out = len(hosting.arrays), len(hosting.out_shapes)
    n_own_out = 1 + has_norm

    def body(*refs):
        refs = list(refs)
        a_ref, w_ref = refs[0], refs[1]
        r_ref = refs[2] if has_resid else None
        g_ref = refs[2 + has_resid] if has_norm else None
        host_in = refs[n_in:n_in + n_host_in]
        o_base = n_in + n_host_in
        o_ref = refs[o_base]
        host_out = refs[o_base + n_own_out:o_base + n_own_out + n_host_out]
        scratch = refs[o_base + n_own_out + n_host_out:]
        if hosting.plan:
            first, last = _first_last(grid)
            sems = scratch[-2:]

            @pl.when(first)
            def _():
                hosting.begin(host_in, host_out, *sems)

        part = jnp.dot(a_ref[...].astype(BF16), w_ref[...], preferred_element_type=F32)

        def finish(r):
            if has_resid:
                r = r + r_ref[...]
            o_ref[...] = r.astype(o_ref.dtype)
            if has_norm:
                scale = lax.rsqrt(jnp.mean(r * r, axis=-1, keepdims=True) + EPS)
                refs[o_base + 1][...] = (r * scale * g_ref[...]).astype(BF16)

        if nk == 1:
            finish(part)
        else:
            acc = scratch[0]
            k = pl.program_id(2)

            @pl.when(k == 0)
            def _():
                acc[...] = part

            @pl.when(k > 0)
            def _():
                acc[...] += part

            @pl.when(k == nk - 1)
            def _():
                finish(acc[...])

        if hosting.plan:
            @pl.when(last)
            def _():
                hosting.finish(host_in, host_out, *sems)

    in_specs = [pl.BlockSpec((tm, tk), lambda j, i, k: (i + a_row_off // tm, k)),
                pl.BlockSpec((None, tk, tn), lambda j, i, k: (l, k, j))]
    args = [a, w]
    if has_resid:
        in_specs.append(pl.BlockSpec((tm, tn), lambda j, i, k: (i, j)))
        args.append(resid)
    if has_norm:
        in_specs.append(pl.BlockSpec((None, 1, tn), lambda j, i, k: (l, 0, 0)))
        args.append(norm_gain)
    aliases = {}
    if has_out:
        in_specs.append(ANY)
        aliases = {len(args): 0}
        args.append(out)
    aliases.update(hosting.aliases(n_in, n_own_out))
    out_specs = [pl.BlockSpec((tm, tn), lambda j, i, k: (i, j + out_col_off // tn))]
    out_shape = [jax.ShapeDtypeStruct((rows, out_cols), out_dtype)]
    if has_norm:
        out_specs.append(pl.BlockSpec((tm, tn), lambda j, i, k: (i, j)))
        out_shape.append(jax.ShapeDtypeStruct((rows, N), BF16))
    res = pl.pallas_call(
        body, name=name, grid=grid,
        in_specs=in_specs + [ANY] * n_host_in,
        out_specs=out_specs + [ANY] * n_host_out,
        out_shape=out_shape + hosting.out_shapes,
        scratch_shapes=([pltpu.VMEM((tm, tn), F32)] if nk > 1 else []) + hosting.scratch(),
        input_output_aliases=aliases, compiler_params=_params(3))(*args, *hosting.arrays)
    own = res[0] if n_own_out == 1 else tuple(res[:n_own_out])
    return (own, list(res[n_own_out:])) if hosting.plan else own


def mm_nt(a, w, l, *, name, rows, kdim=None, a_col_off=0, out_dtype=BF16, norm_bwd_of=None, host=None, tm=1024,
          tn=None, tk=None):
    R = w.shape[1]
    kdim = kdim or w.shape[2]
    tn = tn or R
    tk = tk or kdim
    nk = kdim // tk
    assert rows % tm == 0 and R % tn == 0 and kdim % tk == 0 and a_col_off % tk == 0
    fused = norm_bwd_of is not None
    assert not fused or tn == R
    grid = (R // tn, rows // tm, nk)
    hosting = _Hosting(host)
    n_host_in, n_host_out = len(hosting.arrays), len(hosting.out_shapes)
    n_own_in, n_own_out = (3, 2) if fused else (0, 1)

    def body(a_ref, w_ref, *refs):
        host_in = refs[n_own_in:n_own_in + n_host_in]
        host_out = refs[n_own_in + n_host_in + n_own_out:n_own_in + n_host_in + n_own_out + n_host_out]
        scratch = refs[n_own_in + n_host_in + n_own_out + n_host_out:]
        rest = list(refs[:n_own_in]) + list(refs[n_own_in + n_host_in:n_own_in + n_host_in + n_own_out]) \
            + ([scratch[0]] if nk > 1 else [])
        if hosting.plan:
            first, last = _first_last(grid)
            sems = scratch[-2:]

            @pl.when(first)
            def _():
                hosting.begin(host_in, host_out, *sems)

        part = lax.dot_general(a_ref[...].astype(BF16), w_ref[...], (((1,), (1,)), ((), ())),
                               preferred_element_type=F32)
        i, k = pl.program_id(1), pl.program_id(2)

        def finish(dh):
            if not fused:
                rest[0][...] = dh.astype(rest[0].dtype)
                return
            x_ref, g_ref, dxi_ref, dx_ref, dg_ref = rest[:5]

            @pl.when(i == 0)
            def _():
                dg_ref[...] = jnp.zeros_like(dg_ref)

            xv = x_ref[...]
            r = lax.rsqrt(jnp.mean(xv * xv, axis=-1, keepdims=True) + EPS)
            xh = xv * r
            dhg = dh * g_ref[...]
            dx_ref[...] = dxi_ref[...] + r * (dhg - xh * jnp.mean(dhg * xh, axis=-1, keepdims=True))
            dg_ref[0:1, :] += jnp.sum(dh * xh, axis=0, keepdims=True)

        if nk == 1:
            finish(part)
        else:
            acc = rest[-1]

            @pl.when(k == 0)
            def _():
                acc[...] = part

            @pl.when(k > 0)
            def _():
                acc[...] += part

            @pl.when(k == nk - 1)
            def _():
                finish(acc[...])

        if hosting.plan:
            @pl.when(last)
            def _():
                hosting.finish(host_in, host_out, *sems)

    if a.ndim == 3:
        per = a.shape[2] // tk
        a_spec = pl.BlockSpec((None, tm, tk), lambda j, i, k: (k // per, i, k % per))
    else:
        a_spec = pl.BlockSpec((tm, tk), lambda j, i, k: (i, k + a_col_off // tk))
    in_specs = [a_spec, pl.BlockSpec((None, tn, tk), lambda j, i, k: (l, j, k))]
    args = [a, w]
    row_tile = pl.BlockSpec((tm, tn), lambda j, i, k: (i, j))
    if fused:
        x, gain, gl, dx_in = norm_bwd_of
        in_specs += [row_tile, pl.BlockSpec((None, 1, tn), lambda j, i, k: (gl, 0, 0)), row_tile]
        args += [x, gain, dx_in]
        out_specs = [row_tile, pl.BlockSpec((8, tn), lambda j, i, k: (0, 0))]
        out_shape = [jax.ShapeDtypeStruct((rows, R), F32), jax.ShapeDtypeStruct((8, R), F32)]
    else:
        out_specs, out_shape = [row_tile], [jax.ShapeDtypeStruct((rows, R), out_dtype)]
    res = pl.pallas_call(
        body, name=name, grid=grid, in_specs=in_specs + [ANY] * n_host_in,
        out_specs=out_specs + [ANY] * n_host_out, out_shape=out_shape + hosting.out_shapes,
        scratch_shapes=([pltpu.VMEM((tm, tn), F32)] if nk > 1 else []) + hosting.scratch(),
        input_output_aliases=hosting.aliases(2 + n_own_in, n_own_out), compiler_params=_params(3))(
            *args, *hosting.arrays)
    own = tuple(res[:n_own_out]) if fused else res[0]
    return (own, list(res[n_own_out:])) if hosting.plan else own


def mm_tn(a, b, *, name, rows, ka, nb, a_row_off=0, b_col_off=0, tm=None, tn=None, tk=2048):
    tm = tm or ka
    tn = tn or nb
    tk = min(tk, rows)
    nk = rows // tk
    assert ka % tm == 0 and nb % tn == 0 and rows % tk == 0 and b_col_off % tn == 0 and a_row_off % tk == 0

    def body(a_ref, b_ref, o_ref, acc):
        part = lax.dot_general(a_ref[...].astype(BF16), b_ref[...].astype(BF16), (((0,), (0,)), ((), ())),
                               preferred_element_type=F32)
        k = pl.program_id(2)

        @pl.when(k == 0)
        def _():
            acc[...] = part

        @pl.when(k > 0)
        def _():
            acc[...] += part

        @pl.when(k == nk - 1)
        def _():
            o_ref[...] = acc[...].astype(o_ref.dtype)

    if b.ndim == 3:
        per = b.shape[2] // tn
        b_spec = pl.BlockSpec((None, tk, tn), lambda j, i, k: (j // per, k, j % per))
    else:
        b_spec = pl.BlockSpec((tk, tn), lambda j, i, k: (k, j + b_col_off // tn))
    return pl.pallas_call(
        body, name=name, grid=(nb // tn, ka // tm, nk),
        in_specs=[pl.BlockSpec((tk, tm), lambda j, i, k: (k + a_row_off // tk, i)), b_spec],
        out_specs=pl.BlockSpec((None, tm, tn), lambda j, i, k: (0, i, j)),
        out_shape=jax.ShapeDtypeStruct((1, ka, nb), BF16),
        scratch_shapes=[pltpu.VMEM((tm, tn), F32)], compiler_params=_params(3))(a, b)


def _row_spec(tm, width, col=0):
    return pl.BlockSpec((tm, width), lambda i: (i, col))


def _gain_spec(l, width=D):
    return pl.BlockSpec((None, 1, width), lambda i: (l, 0, 0))


def norm_fwd(x, g3, l, *, name, tm=1024):
    T = x.shape[0]

    def body(x_ref, g_ref, o_ref):
        xv = x_ref[...]
        r = lax.rsqrt(jnp.mean(xv * xv, axis=-1, keepdims=True) + EPS)
        o_ref[...] = (xv * r * g_ref[...]).astype(BF16)

    return pl.pallas_call(
        body, name=name, grid=(T // tm,),
        in_specs=[_row_spec(tm, D), _gain_spec(l)], out_specs=_row_spec(tm, D),
        out_shape=jax.ShapeDtypeStruct((T, D), BF16), compiler_params=_params(1))(x, g3)


def _winsum_back(ext, w):
    s, span = ext, 1
    while span < w:
        s = s + pltpu.roll(s, span, 0)
        span *= 2
    return s


def _winsum_fwd(ext, w):
    rows = ext.shape[0]
    s, span = ext, 1
    while span < w:
        s = s + pltpu.roll(s, rows - span, 0)
        span *= 2
    return s


def _pooled(ext, z, t, g, w):
    sl = slice(g * PG, (g + 1) * PG)
    s = _winsum_back(ext[:, sl], w)[POOL_HALO:, :]
    return s / jnp.minimum(t + 1, w).astype(F32) - z[:, sl]


def pool_fwd(z, wpool, scale3, l, *, name, tm=256):
    T = z.shape[0]
    hb = tm // POOL_HALO
    wl = l if wpool.shape[0] > 1 else 0

    def body(z_ref, zp_ref, wp_ref, sc_ref, o_ref):
        i = pl.program_id(0)
        zv = z_ref[...].astype(F32)
        prev = jnp.where(i == 0, 0.0, zp_ref[...].astype(F32))
        ext = jnp.concatenate([prev, zv], axis=0)
        t = i * tm + lax.broadcasted_iota(jnp.int32, (tm, 1), 0)
        for g, w in enumerate(POOL_WINDOWS):
            sl = slice(g * PG, (g + 1) * PG)
            pooled = _pooled(ext, zv, t, g, w)
            q = jnp.dot(pooled.astype(BF16), wp_ref[g], preferred_element_type=F32)
            o_ref[:, sl] = (q * sc_ref[:, sl]).astype(BF16)

    return pl.pallas_call(
        body, name=name, grid=(T // tm,),
        in_specs=[_row_spec(tm, D),
                  pl.BlockSpec((POOL_HALO, D), lambda i: (jnp.maximum(i * hb - 1, 0), 0)),
                  pl.BlockSpec((None, 4, PG, PG), lambda i: (wl, 0, 0, 0)),
                  _gain_spec(l)],
        out_specs=_row_spec(tm, D),
        out_shape=jax.ShapeDtypeStruct((T, D), BF16), compiler_params=_params(1))(z, z, wpool, scale3)


def sgu_fwd(z, g3, wsm, bT, l, *, name, tm=256):
    T = z.shape[0]

    def body(zu_ref, zv_ref, g_ref, ws_ref, b_ref, o_ref):
        gu = _gelu(zu_ref[...].astype(F32))
        gv = _gelu(zv_ref[...].astype(F32))
        rv = lax.rsqrt(jnp.mean(gv * gv, axis=-1, keepdims=True) + EPS)
        vn = (gv * rv * g_ref[...]).astype(BF16)
        for n in range(tm // CHUNK):
            r = slice(n * CHUNK, (n + 1) * CHUNK)
            for h in range(HEADS):
                cs = slice(h * CHUNK, (h + 1) * CHUNK)
                mixed = jnp.dot(ws_ref[h], vn[r, cs], preferred_element_type=F32) + b_ref[:, h:h + 1]
                o_ref[r, cs] = (gu[r, cs] * mixed).astype(BF16)

    return pl.pallas_call(
        body, name=name, grid=(T // tm,),
        in_specs=[_row_spec(tm, D, 1), _row_spec(tm, D, 2), _gain_spec(l),
                  pl.BlockSpec((None, HEADS, CHUNK, CHUNK), lambda i: (l, 0, 0, 0)),
                  pl.BlockSpec((None, CHUNK, HEADS), lambda i: (l, 0, 0))],
        out_specs=_row_spec(tm, D),
        out_shape=jax.ShapeDtypeStruct((T, D), BF16), compiler_params=_params(1))(z, z, g3, wsm, bT)


def gate_fwd(z, yab, *, name, tm=1024):
    T = z.shape[0]

    def body(za_ref, zb_ref, y_ref, o_ref):
        ga = jax.nn.sigmoid(za_ref[...].astype(F32))
        gb = jax.nn.sigmoid(zb_ref[...].astype(F32))
        o_ref[...] = (ga * y_ref[:, :D].astype(F32) + gb * y_ref[:, D:].astype(F32)).astype(BF16)

    return pl.pallas_call(
        body, name=name, grid=(T // tm,),
        in_specs=[_row_spec(tm, D, 3), _row_spec(tm, D, 4), _row_spec(tm, 2 * D)],
        out_specs=_row_spec(tm, D),
        out_shape=jax.ShapeDtypeStruct((T, D), BF16), compiler_params=_params(1))(z, z, yab)


def _conv(ext, w_ref, b_ref):
    down1, down2 = pltpu.roll(ext, 1, 0), pltpu.roll(ext, 2, 0)
    c = b_ref[...] + w_ref[0:1, :] * down2
    c = c + w_ref[1:2, :] * down1
    return c + w_ref[2:3, :] * ext, down1, down2


def conv_fwd(up, convw, convb3, l, *, name, tm=512):
    T = up.shape[0]
    tc = CONV_TC
    nc = DFF // tc
    hb = tm // CONV_HALO

    def body(ua_ref, uap_ref, ub_ref, ubp_ref, wa_ref, wb_ref, ba_ref, bb_ref, o_ref):
        i = pl.program_id(1)

        def conv_of(u_ref, p_ref, w_ref, b_ref):
            ext = jnp.concatenate([jnp.where(i == 0, 0.0, p_ref[...]), u_ref[...]], axis=0)
            return _conv(ext, w_ref, b_ref)[0][CONV_HALO:, :]

        ca = conv_of(ua_ref, uap_ref, wa_ref, ba_ref)
        cb = conv_of(ub_ref, ubp_ref, wb_ref, bb_ref)
        o_ref[...] = (_gelu(ca) * cb).astype(BF16)

    def cur(off):
        return pl.BlockSpec((tm, tc), lambda j, i: (i, j + off))

    def prev(off):
        return pl.BlockSpec((CONV_HALO, tc), lambda j, i: (jnp.maximum(i * hb - 1, 0), j + off))

    def wspec(off):
        return pl.BlockSpec((None, CONV_ROWS, tc), lambda j, i: (l, 0, j + off))

    def bspec(off):
        return pl.BlockSpec((None, 1, tc), lambda j, i: (l, 0, j + off))

    return pl.pallas_call(
        body, name=name, grid=(nc, T // tm),
        in_specs=[cur(0), prev(0), cur(nc), prev(nc), wspec(0), wspec(nc), bspec(0), bspec(nc)],
        out_specs=pl.BlockSpec((tm, tc), lambda j, i: (i, j)),
        out_shape=jax.ShapeDtypeStruct((T, DFF), BF16),
        compiler_params=_params(2))(up, up, up, up, convw, convw, convb3, convb3)


def ple_fwd(x2, pg, e, g3, l, *, name, tm=1024):
    T = x2.shape[0]
    has_norm = g3 is not None

    def body(x_ref, pg_ref, e_ref, *rest):
        xv = x_ref[...] + jax.nn.sigmoid(pg_ref[...].astype(F32)) * e_ref[...].astype(F32)
        if has_norm:
            g_ref, o_ref, h_ref = rest
            r = lax.rsqrt(jnp.mean(xv * xv, axis=-1, keepdims=True) + EPS)
            h_ref[...] = (xv * r * g_ref[...]).astype(BF16)
        else:
            o_ref, = rest
        o_ref[...] = xv

    x_shape = jax.ShapeDtypeStruct((T, D), F32)
    return pl.pallas_call(
        body, name=name, grid=(T // tm,),
        in_specs=[_row_spec(tm, D)] * 3 + ([_gain_spec(l)] if has_norm else []),
        out_specs=[_row_spec(tm, D)] * 2 if has_norm else _row_spec(tm, D),
        out_shape=[x_shape, jax.ShapeDtypeStruct((T, D), BF16)] if has_norm else x_shape,
        compiler_params=_params(1))(x2, pg, e, *([g3] if has_norm else []))


def loss_head(x, g3, tgt, *, name, tm=1024):
    T = x.shape[0]

    def body(x_ref, g_ref, t_ref, loss_ref, dx_ref, dg_ref):
        @pl.when(pl.program_id(0) == 0)
        def _():
            loss_ref[...] = jnp.zeros_like(loss_ref)
            dg_ref[...] = jnp.zeros_like(dg_ref)

        xv, g = x_ref[...], g_ref[...]
        r = lax.rsqrt(jnp.mean(xv * xv, axis=-1, keepdims=True) + EPS)
        xh = xv * r
        err = xh * g - t_ref[...]
        loss_ref[...] += 0.5 * jnp.sum(jnp.mean(err * err, axis=-1, keepdims=True))
        dy = err * (1.0 / D)
        dyg = dy * g
        dx_ref[...] = r * (dyg - xh * jnp.mean(dyg * xh, axis=-1, keepdims=True))
        dg_ref[0:1, :] += jnp.sum(dy * xh, axis=0, keepdims=True)

    return pl.pallas_call(
        body, name=name, grid=(T // tm,),
        in_specs=[_row_spec(tm, D), pl.BlockSpec((1, D), lambda i: (0, 0)), _row_spec(tm, D)],
        out_specs=[pl.BlockSpec((8, LANES), lambda i: (0, 0)), _row_spec(tm, D),
                   pl.BlockSpec((8, D), lambda i: (0, 0))],
        out_shape=[jax.ShapeDtypeStruct((8, LANES), F32), jax.ShapeDtypeStruct((T, D), F32),
                   jax.ShapeDtypeStruct((8, D), F32)],
        compiler_params=_params(1))(x, g3, tgt)


def ple_bwd(dx, pg, e, *, name, tm=1024):
    T = dx.shape[0]

    def body(dx_ref, pg_ref, e_ref, de_ref, dpg_ref):
        gate = jax.nn.sigmoid(pg_ref[...].astype(F32))
        dxv = dx_ref[...]
        de_ref[...] = (dxv * gate).astype(BF16)
        dpg_ref[...] = (dxv * e_ref[...].astype(F32) * gate * (1.0 - gate)).astype(BF16)

    return pl.pallas_call(
        body, name=name, grid=(T // tm,),
        in_specs=[_row_spec(tm, D)] * 3, out_specs=[_row_spec(tm, D)] * 2,
        out_shape=[jax.ShapeDtypeStruct((T, D), BF16)] * 2, compiler_params=_params(1))(dx, pg, e)


def conv_bwd(df, up, convw, convb3, l, *, name, host=None, tm=256):
    T = up.shape[0]
    tc = CONV_TC
    nc = DFF // tc
    hb = tm // CONV_HALO
    nt = T // tm
    rows = tm + 2 * CONV_HALO
    own = slice(CONV_HALO, CONV_HALO + tm)

    hosting = _Hosting(host)
    n_host_in, n_host_out = len(hosting.arrays), len(hosting.out_shapes)

    def body(df_ref, dfn_ref, ua_ref, uap_ref, uan_ref, ub_ref, ubp_ref, ubn_ref, wa_ref, wb_ref, ba_ref, bb_ref,
             *rest):
        host_in = rest[:n_host_in]
        dup_ref, dcw_ref, dcb_ref = rest[n_host_in:n_host_in + 3]
        host_out = rest[n_host_in + 3:n_host_in + 3 + n_host_out]
        sems = rest[n_host_in + 3 + n_host_out:]
        i = pl.program_id(1)
        if hosting.plan:
            first, last = _first_last((nc, nt))

            @pl.when(first)
            def _():
                hosting.begin(host_in, host_out, *sems)

        @pl.when(i == 0)
        def _():
            dcw_ref[...] = jnp.zeros_like(dcw_ref)
            dcb_ref[...] = jnp.zeros_like(dcb_ref)

        def ext_of(c_ref, p_ref, n_ref):
            return jnp.concatenate([jnp.where(i == 0, 0.0, p_ref[...]), c_ref[...],
                                    jnp.where(i == nt - 1, 0.0, n_ref[...])], axis=0)

        ea = ext_of(ua_ref, uap_ref, uan_ref)
        eb = ext_of(ub_ref, ubp_ref, ubn_ref)
        ca, ea1, ea2 = _conv(ea, wa_ref, ba_ref)
        cb, eb1, eb2 = _conv(eb, wb_ref, bb_ref)
        df_ext =jnp.concatenate([jnp.zeros((CONV_HALO, tc), F32), df_ref[...],
                                  jnp.where(i == nt - 1, 0.0, dfn_ref[...])], axis=0)
        cdf = 0.5 * (1.0 + lax.erf(ca * 0.7071067811865476))
        da = df_ext * cb * (cdf + ca * jnp.exp(-0.5 * ca * ca) * 0.3989422804014327)
        db = df_ext * (ca * cdf)

        def finish(h, dc, e, e1, e2, w_ref):
            dup = w_ref[2:3, :] * dc + w_ref[1:2, :] * pltpu.roll(dc, rows - 1, 0)
            dup = dup + w_ref[0:1, :] * pltpu.roll(dc, rows - 2, 0)
            dup_ref[h] = dup[own, :].astype(BF16)
            dco = dc[own, :]
            dcb_ref[h, 0:1, :] += jnp.sum(dco, axis=0, keepdims=True)
            dcw_ref[h, 0:1, :] += jnp.sum(dco * e2[own, :], axis=0, keepdims=True)
            dcw_ref[h, 1:2, :] += jnp.sum(dco * e1[own, :], axis=0, keepdims=True)
            dcw_ref[h, 2:3, :] += jnp.sum(dco * e[own, :], axis=0, keepdims=True)

        finish(0, da, ea, ea1, ea2, wa_ref)
        finish(1, db, eb, eb1, eb2, wb_ref)
        if hosting.plan:
            @pl.when(last)
            def _():
                hosting.finish(host_in, host_out, *sems)

    def nxt(i):
        return jnp.minimum((i + 1) * hb, T // CONV_HALO - 1)

    def prv(i):
        return jnp.maximum(i * hb - 1, 0)

    def up_specs(off):
        return [pl.BlockSpec((tm, tc), lambda j, i: (i, j + off)),
                pl.BlockSpec((CONV_HALO, tc), lambda j, i: (prv(i), j + off)),
                pl.BlockSpec((CONV_HALO, tc), lambda j, i: (nxt(i), j + off))]

    in_specs = [pl.BlockSpec((tm, tc), lambda j, i: (i, j)),
                pl.BlockSpec((CONV_HALO, tc), lambda j, i: (nxt(i), j)),
                *up_specs(0), *up_specs(nc),
                pl.BlockSpec((None, CONV_ROWS, tc), lambda j, i: (l, 0, j)),
                pl.BlockSpec((None, CONV_ROWS, tc), lambda j, i: (l, 0, j + nc)),
                pl.BlockSpec((None, 1, tc), lambda j, i: (l, 0, j)),
                pl.BlockSpec((None, 1, tc), lambda j, i: (l, 0, j + nc))]
    res = pl.pallas_call(
        body, name=name, grid=(nc, nt), in_specs=in_specs + [ANY] * n_host_in,
        out_specs=[pl.BlockSpec((2, tm, tc), lambda j, i: (0, i, j)),
                   pl.BlockSpec((2, 8, tc), lambda j, i: (0, 0, j)),
                   pl.BlockSpec((2, 8, tc), lambda j, i: (0, 0, j))] + [ANY] * n_host_out,
        out_shape=[jax.ShapeDtypeStruct((2, T, DFF), BF16), jax.ShapeDtypeStruct((2, 8, DFF), F32),
                   jax.ShapeDtypeStruct((2, 8, DFF), F32)] + hosting.out_shapes,
        scratch_shapes=hosting.scratch(), input_output_aliases=hosting.aliases(12, 3),
        compiler_params=_params(2))(df, df, up, up, up, up, up, up, convw, convw, convb3, convb3, *hosting.arrays)
    return res[0], res[1], res[2], list(res[3:])


def gate_bwd(dmo, z, yab, *, name, tm=1024):
    T = z.shape[0]

    def body(dmo_ref, zg_ref, y_ref, dz_ref, dy_ref):
        g = jax.nn.sigmoid(zg_ref[...].astype(F32))
        dmo_v = dmo_ref[...].astype(F32)
        dy_ref[...] = (dmo_v * g).astype(BF16)
        dz_ref[...] = (dmo_v * y_ref[...].astype(F32) * g * (1.0 - g)).astype(BF16)

    return pl.pallas_call(
        body, name=name, grid=(T // tm, 2),
        in_specs=[pl.BlockSpec((tm, D), lambda i, s: (i, 0)),
                  pl.BlockSpec((tm, D), lambda i, s: (i, 3 + s)),
                  pl.BlockSpec((tm, D), lambda i, s: (i, s))],
        out_specs=[pl.BlockSpec((tm, D), lambda i, s: (i, 3 + s)),
                   pl.BlockSpec((tm, D), lambda i, s: (i, s))],
        out_shape=[jax.ShapeDtypeStruct((T, 5 * D), BF16), jax.ShapeDtypeStruct((T, 2 * D), BF16)],
        compiler_params=_params(2))(dmo, z, yab)


def mixer_bwd(da, ds, z, dz, wpool, scale3, g3, wsm, wsmT, bT, l, *, name, tm=256):
    T = z.shape[0]
    hb = tm // POOL_HALO
    nt = T // tm

    def body(da_ref, dan_ref, ds_ref, zp_ref, zpp_ref, zu_ref, zv_ref, wp_ref, sc_ref, g_ref, ws_ref, wst_ref,
             b_ref, dzin_ref, dz_ref, dwp_ref, dsc_ref, dws_ref, dbt_ref, dgs_ref, mixed_scr, dvn_scr, db_scr):
        del dzin_ref
        i = pl.program_id(0)

        @pl.when(i == 0)
        def _():
            dwp_ref[...] = jnp.zeros_like(dwp_ref)
            dsc_ref[...] = jnp.zeros_like(dsc_ref)
            dws_ref[...] = jnp.zeros_like(dws_ref)
            dgs_ref[...] = jnp.zeros_like(dgs_ref)
            db_scr[...] = jnp.zeros_like(db_scr)

        zv_p = zp_ref[...].astype(F32)
        ext = jnp.concatenate([jnp.where(i == 0, 0.0, zpp_ref[...].astype(F32)), zv_p], axis=0)
        da_v = da_ref[...].astype(F32)
        da_ext = jnp.concatenate([da_v, jnp.where(i == nt - 1, 0.0, dan_ref[...].astype(F32))], axis=0)
        t = i * tm + lax.broadcasted_iota(jnp.int32, (tm, 1), 0)
        t_ext = i * tm + lax.broadcasted_iota(jnp.int32, (tm + POOL_HALO, 1), 0)
        for g, w in enumerate(POOL_WINDOWS):
            sl = slice(g * PG, (g + 1) * PG)
            pooled = _pooled(ext, zv_p, t, g, w).astype(BF16)
            q = jnp.dot(pooled, wp_ref[g], preferred_element_type=F32)
            dsc_ref[0:1, sl] += jnp.sum(da_v[:, sl] * q, axis=0, keepdims=True)
            dq_ext = (da_ext[:, sl] * sc_ref[:, sl]).astype(BF16)
            dwp_ref[g] += lax.dot_general(pooled, dq_ext[:tm, :], (((0,), (0,)), ((), ())),
                                          preferred_element_type=F32)
            dpool = lax.dot_general(dq_ext, wp_ref[g], (((1,), (1,)), ((), ())), preferred_element_type=F32)
            spread = _winsum_fwd(dpool / jnp.minimum(t_ext + 1, w).astype(F32), w)
            dz_ref[:, sl] = (spread[:tm, :] - dpool[:tm, :]).astype(BF16)

        zu, zv, ds_v = zu_ref[...].astype(F32), zv_ref[...].astype(F32), ds_ref[...].astype(F32)
        gain = g_ref[...]
        gu, gv = _gelu(zu), _gelu(zv)
        rv = lax.rsqrt(jnp.mean(gv * gv, axis=-1, keepdims=True) + EPS)
        vh = gv * rv
        vn = (vh * gain).astype(BF16)
        dmix = ds_v * gu
        dmix_b = dmix.astype(BF16)
        for n in range(tm // CHUNK):
            r = slice(n * CHUNK, (n + 1) * CHUNK)
            db_scr[...] += dmix[r, :]
            for h in range(HEADS):
                cs = slice(h * CHUNK, (h + 1) * CHUNK)
                mixed_scr[r, cs] = jnp.dot(ws_ref[h], vn[r, cs], preferred_element_type=F32) + b_ref[:, h:h + 1]
                dws_ref[h] += lax.dot_general(dmix_b[r, cs], vn[r, cs], (((1,), (1,)), ((), ())),
                                              preferred_element_type=F32)
                dvn_scr[r, cs] = jnp.dot(wst_ref[h], dmix_b[r, cs], preferred_element_type=F32)
        dz_ref[:, D:2 * D] = (ds_v * mixed_scr[...] * _gelu_grad(zu)).astype(BF16)
        dvn = dvn_scr[...]
        dgs_ref[0:1, :] += jnp.sum(dvn * vh, axis=0, keepdims=True)
        dvg = dvn * gain
        dgv = rv * (dvg - vh * jnp.mean(dvg * vh, axis=-1, keepdims=True))
        dz_ref[:, 2 * D:3 * D] = (dgv * _gelu_grad(zv)).astype(BF16)

        @pl.when(i == nt - 1)
        def _():
            tril = (lax.broadcasted_iota(jnp.int32, (CHUNK, CHUNK), 0)
                    >= lax.broadcasted_iota(jnp.int32, (CHUNK, CHUNK), 1)).astype(F32)
            for h in range(HEADS):
                dws_ref[h] = dws_ref[h] * tril
                dbt_ref[:, h:h + 1] = jnp.sum(db_scr[:, h * CHUNK:(h + 1) * CHUNK], axis=1, keepdims=True)

    const4 = lambda i: (l, 0, 0, 0)
    wl = l if wpool.shape[0] > 1 else 0
    in_specs = [
        _row_spec(tm, D),
        pl.BlockSpec((POOL_HALO, D), lambda i: (jnp.minimum((i + 1) * hb, T // POOL_HALO - 1), 0)),
        _row_spec(tm, D),
        _row_spec(tm, D, 0),
        pl.BlockSpec((POOL_HALO, D), lambda i: (jnp.maximum(i * hb - 1, 0), 0)),
        _row_spec(tm, D, 1), _row_spec(tm, D, 2),
        pl.BlockSpec((None, 4, PG, PG), lambda i: (wl, 0, 0, 0)),
        _gain_spec(l), _gain_spec(l),
        pl.BlockSpec((None, HEADS, CHUNK, CHUNK), const4),
        pl.BlockSpec((None, HEADS, CHUNK, CHUNK), const4),
        pl.BlockSpec((None, CHUNK, HEADS), lambda i: (l, 0, 0)),
        ANY,
    ]
    out_specs = [
        pl.BlockSpec((tm, 3 * D), lambda i: (i, 0)),
        pl.BlockSpec((4, PG, PG), lambda i: (0, 0, 0)),
        pl.BlockSpec((8, D), lambda i: (0, 0)),
        pl.BlockSpec((HEADS, CHUNK, CHUNK), lambda i: (0, 0, 0)),
        pl.BlockSpec((CHUNK, HEADS), lambda i: (0, 0)),
        pl.BlockSpec((8, D), lambda i: (0, 0)),
    ]
    out_shape = [
        jax.ShapeDtypeStruct((T, 5 * D), BF16), jax.ShapeDtypeStruct((4, PG, PG), F32),
        jax.ShapeDtypeStruct((8, D), F32), jax.ShapeDtypeStruct((HEADS, CHUNK, CHUNK), F32),
        jax.ShapeDtypeStruct((CHUNK, HEADS), F32), jax.ShapeDtypeStruct((8, D), F32),
    ]
    return pl.pallas_call(
        body, name=name, grid=(nt,), in_specs=in_specs, out_specs=out_specs, out_shape=out_shape,
        scratch_shapes=[pltpu.VMEM((tm, D), F32), pltpu.VMEM((tm, D), F32), pltpu.VMEM((CHUNK, D), F32)],
        input_output_aliases={13: 0}, compiler_params=_params(1))(
            da, da, ds, z, z, z, z, wpool, scale3, g3, wsm, wsmT, bT, dz)


def _row_tile(rows, cols, sub):
    cap = max(sub, (2 * 1024 * 1024) // (4 * cols))
    best = None
    for tr in range(sub, min(rows, cap) + 1, sub):
        if rows % tr == 0:
            best = tr
    return best or rows


def elementwise(fn, ins, out_dtypes, *, name, row_blk_offs=None, rows=None):
    cols = ins[0].shape[1]
    rows = rows or ins[0].shape[0]
    tr = _row_tile(rows, cols, 16)
    offs = row_blk_offs or [0] * len(ins)
    n_in = len(ins)

    def body(*refs):
        outs = fn(*[r[...] for r in refs[:n_in]])
        for o_ref, o in zip(refs[n_in:], outs):
            o_ref[...] = o.astype(o_ref.dtype)

    return pl.pallas_call(
        body, name=name, grid=(rows // tr,),
        in_specs=[pl.BlockSpec((tr, cols), functools.partial(lambda i, o: (i + o * (rows // tr), 0), o=o))
                  for o in offs],
        out_specs=[pl.BlockSpec((tr, cols), lambda i: (i, 0)) for _ in out_dtypes],
        out_shape=[jax.ShapeDtypeStruct((rows, cols), dt) for dt in out_dtypes],
        compiler_params=_params(1))(*ins)


def _adamw(w, g, m, v):
    m = ADAM_B1 * m + (1.0 - ADAM_B1) * g
    v = ADAM_B2 * v + (1.0 - ADAM_B2) * jnp.square(g)
    m_hat = m / (1.0 - ADAM_B1 ** ADAM_STEP)
    v_hat = v / (1.0 - ADAM_B2 ** ADAM_STEP)
    delta = -ADAM_LR * (m_hat / (jnp.sqrt(v_hat) + ADAM_EPS) + ADAM_WD * w)
    return delta, m, v


def _view2d(a):
    return a.reshape(-1, a.shape[-1])


def _place():
    x, y, c = lax.axis_index("x"), lax.axis_index("y"), lax.axis_index("c")
    others = [(1 - x, y), (x, 1 - y), (1 - x, 1 - y)]
    return x, y, c, 2 * x + y, others


def _remote(src, dst, send_sems, recv_sems, k, to):
    return pltpu.make_async_remote_copy(src_ref=src, dst_ref=dst, send_sem=send_sems.at[k], recv_sem=recv_sems.at[k],
                                        device_id=to, device_id_type=MESH)


def _half(ref, axis, j, size, h):
    if len(ref.shape) == 3:
        return ref.at[:, pl.ds(j * size + h * (size // 2), size // 2), :]
    if axis == 0:
        return ref.at[pl.ds(j * size + h * (size // 2), size // 2), :]
    rows = ref.shape[0] // 2
    return ref.at[pl.ds(h * rows, rows), pl.ds(j * size, size)]


def _half_shard_shape(shape, axis, size):
    if len(shape) == 3:
        return (shape[0], size // 2, shape[2])
    if axis == 0:
        return (size // 2, shape[1])
    return (shape[0] // 2, size)


class Exchange:
    def __init__(self, arrays, out_shapes, aliases, n_sems, begin, finish):
        self.arrays, self.out_shapes, self.aliases, self.n_sems = list(arrays), list(out_shapes), aliases, n_sems
        self.begin, self.finish = begin, finish


class _Hosting:
    def __init__(self, plan):
        self.plan = list(plan or [])
        self.arrays = [a for ex in self.plan for a in ex.arrays]
        self.out_shapes = [o for ex in self.plan for o in ex.out_shapes]
        self.n_sems = sum(ex.n_sems for ex in self.plan)

    def scratch(self):
        return [pltpu.SemaphoreType.DMA((self.n_sems,)), pltpu.SemaphoreType.DMA((self.n_sems,))] if self.plan else []

    def aliases(self, in_base, out_base):
        out, i0, o0 = {}, in_base, out_base
        for ex in self.plan:
            out.update({i0 + i: o0 + o for i, o in ex.aliases.items()})
            i0, o0 = i0 + len(ex.arrays), o0 + len(ex.out_shapes)
        return out

    def _each(self, in_refs, out_refs):
        i0 = o0 = s0 = 0
        for ex in self.plan:
            yield ex, in_refs[i0:i0 + len(ex.arrays)], out_refs[o0:o0 + len(ex.out_shapes)], s0
            i0, o0, s0 = i0 + len(ex.arrays), o0 + len(ex.out_shapes), s0 + ex.n_sems

    def begin(self, in_refs, out_refs, send_sems, recv_sems):
        for ex, ins, outs, s0 in self._each(in_refs, out_refs):
            ex.begin(ins, outs, send_sems, recv_sems, s0)

    def finish(self, in_refs, out_refs, send_sems, recv_sems):
        for ex, ins, outs, s0 in self._each(in_refs, out_refs):
            ex.finish(ins, outs, send_sems, recv_sems, s0)


def _first_last(grid):
    ids = [pl.program_id(a) for a in range(len(grid))]
    first = functools.reduce(jnp.logical_and, [i == 0 for i in ids])
    last = functools.reduce(jnp.logical_and, [i == g - 1 for i, g in zip(ids, grid)])
    return first, last


def run_exchanges(plan, *, name):
    host = _Hosting(plan)
    n_in, n_out = len(host.arrays), len(host.out_shapes)

    def body(*refs):
        ins, outs = refs[:n_in], refs[n_in:n_in + n_out]
        send_sems, recv_sems = refs[n_in + n_out:]
        host.begin(ins, outs, send_sems, recv_sems)
        host.finish(ins, outs, send_sems, recv_sems)

    return pl.pallas_call(
        body, name=name, in_specs=[ANY] * n_in, out_specs=[ANY] * n_out, out_shape=host.out_shapes,
        scratch_shapes=host.scratch(), input_output_aliases=host.aliases(0, 0),
        compiler_params=pltpu.CompilerParams(has_side_effects=True))(*host.arrays)


def place_shard(src, l, axis, size, out_dtype, place, *, name):
    shard = src.shape[1:]
    natural = tuple(size * N_CHIPS if a == axis else s for a, s in enumerate(shard))
    if len(shard) == 3:
        blk = (None,) + shard
        grid = (1,)
        in_map = lambda i, pr: (l, 0, 0, 0)
        out_map = lambda i, pr: (0, 0, pr[0], 0)
    else:
        tr = _row_tile(shard[0], shard[1], 16)
        steps = shard[0] // tr
        blk = (None, tr, shard[1])
        grid = (steps,)
        in_map = lambda i, pr: (l, i, 0)
        if axis == 0:
            out_map = lambda i, pr: (0, pr[0] * steps + i, 0)
        else:
            out_map = lambda i, pr: (0, i, pr[0])

    def body(pr_ref, s_ref, o_ref):
        del pr_ref
        o_ref[...] = s_ref[...].astype(o_ref.dtype)

    return pl.pallas_call(
        body, name=name,
        grid_spec=pltpu.PrefetchScalarGridSpec(
            num_scalar_prefetch=1, grid=grid, in_specs=[pl.BlockSpec(blk, in_map)],
            out_specs=pl.BlockSpec(blk, out_map)),
        out_shape=jax.ShapeDtypeStruct((1,) + natural, out_dtype), compiler_params=_params(1))(place, src)


def place_both_layers(src, axis, size, place, *, name):
    rows, cols = src.shape[1], src.shape[2]

    def body(pr_ref, s_ref, o_ref):
        del pr_ref
        o_ref[...] = s_ref[...]

    return pl.pallas_call(
        body, name=name,
        grid_spec=pltpu.PrefetchScalarGridSpec(
            num_scalar_prefetch=1, grid=(2,), in_specs=[pl.BlockSpec((None, rows, cols), lambda lyr, pr: (lyr, 0, 0))],
            out_specs=pl.BlockSpec((None, rows, cols), lambda lyr, pr: (lyr, 0, pr[0]))),
        out_shape=jax.ShapeDtypeStruct((2, rows, cols * N_CHIPS), src.dtype), compiler_params=_params(1))(place, src)


def gather_exchange(arrays, geom):
    n = len(arrays)

    def begin(ins, outs, send_sems, recv_sems, s0):
        x, y, c, j, others = _place()
        for t, (axis, size) in enumerate(geom):
            mine = _half(outs[t].at[0], axis, j, size, c)
            for k, (ox, oy) in enumerate(others):
                _remote(mine, mine, send_sems, recv_sems, s0 + 6 * t + k, (ox, oy, c)).start()

    def finish(ins, outs, send_sems, recv_sems, s0):
        x, y, c, j, others = _place()
        sib = (x, y, 1 - c)
        passed = []
        for t, (axis, size) in enumerate(geom):
            for k, (ox, oy) in enumerate(others):
                landed = _half(outs[t].at[0], axis, 2 * ox + oy, size, c)
                _remote(landed, landed, send_sems, recv_sems, s0 + 6 * t + k, (ox, oy, c)).wait_recv()
                fwd = _remote(landed, landed, send_sems, recv_sems, s0 + 6 * t + 3 + k, sib)
                fwd.start()
                passed.append(fwd)
        for t, (axis, size) in enumerate(geom):
            for k, (ox, oy) in enumerate(others):
                got = _half(outs[t].at[0], axis, 2 * ox + oy, size, 1 - c)
                _remote(got, got, send_sems, recv_sems, s0 + 6 * t + 3 + k, sib).wait_recv()
        for fwd in passed:
            fwd.wait_send()
        for t, (axis, size) in enumerate(geom):
            mine = _half(outs[t].at[0], axis, j, size, c)
            for k, (ox, oy) in enumerate(others):
                _remote(mine, mine, send_sems, recv_sems, s0 + 6 * t + k, (ox, oy, c)).wait_send()

    return Exchange(arrays, [jax.ShapeDtypeStruct(a.shape, a.dtype) for a in arrays], {t: t for t in range(n)},
                    6 * n, begin, finish)


def gather_by_layer_exchange(array, axis, size):
    def blocks(out, others, lyr):
        return [_block(out.at[lyr], axis, 2 * ox + oy, size) for (ox, oy) in others]

    def begin(ins, outs, send_sems, recv_sems, s0):
        x, y, c, j, others = _place()
        mine = _block(outs[0].at[c], axis, j, size)
        for k, (ox, oy) in enumerate(others):
            _remote(mine, mine, send_sems, recv_sems, s0 + k, (ox, oy, c)).start()

    def finish(ins, outs, send_sems, recv_sems, s0):
        x, y, c, j, others = _place()
        sib = (x, y, 1 - c)
        passed = []
        for k, ((ox, oy), landed) in enumerate(zip(others, blocks(outs[0], others, c))):
            _remote(landed, landed, send_sems, recv_sems, s0 + k, (ox, oy, c)).wait_recv()
            fwd = _remote(landed, landed, send_sems, recv_sems, s0 + 3 + k, sib)
            fwd.start()
            passed.append(fwd)
        for k, got in enumerate(blocks(outs[0], others, 1 - c)):
            _remote(got, got, send_sems, recv_sems, s0 + 3 + k, sib).wait_recv()
        for fwd in passed:
            fwd.wait_send()
        mine = _block(outs[0].at[c], axis, j, size)
        for k, (ox, oy) in enumerate(others):
            _remote(mine, mine, send_sems, recv_sems, s0 + k, (ox, oy, c)).wait_send()

    return Exchange([array], [jax.ShapeDtypeStruct(array.shape, array.dtype)], {0: 0}, 6, begin, finish)


def swap_exchange(grads, geom):
    def pieces(t, g, dst, h):
        axis, size = geom[t]
        if len(g.shape) == 2 and axis == 1:
            rows = g.shape[0] // 2
            return [(g.at[pl.ds(h * rows, rows), :], dst)]
        return [(_half(g, axis, jb, size, h), dst.at[jb]) for jb in range(N_CHIPS)]

    counts = [1 if (len(g.shape) == 3 and a == 1) else N_CHIPS for g, (a, _) in zip(grads, geom)]
    bases = [sum(counts[:t]) for t in range(len(grads))]

    def copies(ins, outs, send_sems, recv_sems, s0):
        x, y, c, _, _ = _place()
        cps = []
        for t in range(len(grads)):
            for q, (src, dst) in enumerate(pieces(t, ins[t].at[0], outs[t], 1 - c)):
                cps.append(_remote(src, dst, send_sems, recv_sems, s0 + bases[t] + q, (x, y, 1 - c)))
        return cps

    def begin(*a):
        for cp in copies(*a):
            cp.start()

    def finish(*a):
        for cp in copies(*a):
            cp.wait()

    out_shapes = []
    for g, (axis, size) in zip(grads, geom):
        shp = g.shape[1:]
        if len(shp) == 2 and axis == 1:
            out_shapes.append(jax.ShapeDtypeStruct((shp[0] // 2, shp[1]), g.dtype))
        else:
            out_shapes.append(jax.ShapeDtypeStruct((N_CHIPS,) + _half_shard_shape(shp, axis, size), g.dtype))
    return Exchange(grads, out_shapes, {}, sum(counts), begin, finish)


def scatter_exchange(parts, geom, shapes):
    def copies(ins, outs, send_sems, recv_sems, s0):
        x, y, c, j, others = _place()
        cps = []
        for t, ((axis, size), shp) in enumerate(zip(geom, shapes)):
            for k, (ox, oy) in enumerate(others):
                jp = 2 * ox + oy
                src = ins[t].at[:, pl.ds(jp * size, size)] if (len(shp) == 2 and axis == 1) else ins[t].at[jp]
                cps.append(_remote(src, outs[t].at[k], send_sems, recv_sems, s0 + 3 * t + k, (ox, oy, c)))
        return cps

    def begin(*a):
        for cp in copies(*a):
            cp.start()

    def finish(*a):
        for cp in copies(*a):
            cp.wait_recv()
        for cp in copies(*a):
            cp.wait_send()

    out_shapes = [jax.ShapeDtypeStruct((3,) + _half_shard_shape(shp, axis, size), p.dtype)
                  for p, (axis, size), shp in zip(parts, geom, shapes)]
    return Exchange(parts, out_shapes, {}, 3 * len(parts), begin, finish)


def share_exchange(grads, which):
    n = len(which)

    def my_half(refs, t, h):
        lyr = refs[which[t][0]].at[which[t][1]]
        if len(lyr.shape) == 3:
            rows = lyr.shape[1] // 2
            return lyr.at[:, pl.ds(h * rows, rows), :]
        rows = lyr.shape[0] // 2
        return lyr.at[pl.ds(h * rows, rows), :]

    def begin(ins, outs, send_sems, recv_sems, s0):
        x, y, c, _, _ = _place()
        for t in range(n):
            mine = my_half(outs, t, c)
            _remote(mine, mine, send_sems, recv_sems, s0 + t, (x, y, 1 - c)).start()

    def finish(ins, outs, send_sems, recv_sems, s0):
        x, y, c, _, _ = _place()
        for t in range(n):
            got = my_half(outs, t, 1 - c)
            _remote(got, got, send_sems, recv_sems, s0 + t, (x, y, 1 - c)).wait_recv()
        for t in range(n):
            mine = my_half(outs, t, c)
            _remote(mine, mine, send_sems, recv_sems, s0 + t, (x, y, 1 - c)).wait_send()

    return Exchange(grads, [jax.ShapeDtypeStruct(g.shape, g.dtype) for g in grads],
                    {t: t for t in range(len(grads))}, n, begin, finish)


def all_reduce_small(s):
    rows = s.shape[0]
    half = rows // 2
    assert half % 8 == 0

    def body(s_ref, o_ref, a_ref, b_ref, p_ref, send_sems, recv_sems):
        x, y, c, j, others = _place()
        sib = (x, y, 1 - c)
        swap = _remote(s_ref, a_ref, send_sems, recv_sems, 0, sib)
        swap.start()
        swap.wait()
        p_ref[...] = s_ref[...] + a_ref[...]
        mine = pl.ds(pl.multiple_of(c * half, 8), half)
        b_ref[j] = p_ref[mine, :]
        cps = [_remote(p_ref.at[mine, :], b_ref.at[j], send_sems, recv_sems, 1 + k, (ox, oy, c))
               for k, (ox, oy) in enumerate(others)]
        for cp in cps:
            cp.start()
        for k, (ox, oy) in enumerate(others):
            slot = b_ref.at[2 * ox + oy]
            _remote(slot, slot, send_sems, recv_sems, 1 + k, (ox, oy, c)).wait_recv()
        for cp in cps:
            cp.wait_send()
        o_ref[mine, :] = ((b_ref[0] + b_ref[1]) + b_ref[2]) + b_ref[3]
        back = _remote(o_ref.at[mine, :], o_ref.at[mine, :], send_sems, recv_sems, 4, sib)
        back.start()
        back.wait_send()
        theirs = pl.ds(pl.multiple_of((1 - c) * half, 8), half)
        _remote(o_ref.at[theirs, :], o_ref.at[theirs, :], send_sems, recv_sems, 4, sib).wait_recv()

    vmem = pl.BlockSpec(memory_space=pltpu.VMEM)
    return pl.pallas_call(
        body, name="all_reduce_small", in_specs=[vmem], out_specs=vmem,
        out_shape=jax.ShapeDtypeStruct((rows, LANES), F32),
        scratch_shapes=[pltpu.VMEM((rows, LANES), F32), pltpu.VMEM((N_CHIPS, half, LANES), F32),
                        pltpu.VMEM((rows, LANES), F32), pltpu.SemaphoreType.DMA((5,)),
                        pltpu.SemaphoreType.DMA((5,))],
        compiler_params=pltpu.CompilerParams(vmem_limit_bytes=VMEM_LIMIT, has_side_effects=True))(s)


def pair_sum(g, got, axis, size, place, *, name):
    shp = g.shape[1:]
    if len(shp) == 3:
        hs = size // 2
        grid = (N_CHIPS,)
        g_spec = pl.BlockSpec((None, shp[0], hs, shp[2]), lambda jb, pr: (0, 0, 2 * jb + pr[1], 0))
        r_spec = pl.BlockSpec((None, shp[0], hs, shp[2]), lambda jb, pr: (jb, 0, 0, 0))
    elif axis == 0:
        hs = size // 2
        tr = _row_tile(hs, shp[1], 16)
        steps = hs // tr
        grid = (N_CHIPS, steps)
        g_spec = pl.BlockSpec((None, tr, shp[1]), lambda jb, i, pr: (0, (2 * jb + pr[1]) * steps + i, 0))
        r_spec = pl.BlockSpec((None, tr, shp[1]), lambda jb, i, pr: (jb, i, 0))
    else:
        rows = shp[0] // 2
        tr = _row_tile(rows, shp[1], 16)
        steps = rows // tr
        grid = (steps,)
        g_spec = pl.BlockSpec((None, tr, shp[1]), lambda i, pr: (0, pr[1] * steps + i, 0))
        r_spec = pl.BlockSpec((tr, shp[1]), lambda i, pr: (i, 0))

    def body(pr_ref, g_ref, r_ref, o_ref):
        del pr_ref
        o_ref[...] = (g_ref[...].astype(F32) + r_ref[...].astype(F32)).astype(BF16)

    return pl.pallas_call(
        body, name=name,
        grid_spec=pltpu.PrefetchScalarGridSpec(num_scalar_prefetch=1, grid=grid, in_specs=[g_spec, r_spec],
                                               out_specs=r_spec),
        out_shape=jax.ShapeDtypeStruct(got.shape, BF16), compiler_params=_params(len(grid)))(place, g, got)


def chip_sum(part, slots, shp, axis, size, l, place, out, *, name):
    shard = _shard_shape(shp, axis, size)
    hshape = slots.shape[1:]
    if len(shp) == 3:
        grid = (1,)
        p_spec = pl.BlockSpec((None,) + hshape, lambda i, pr: (pr[0], 0, 0, 0))
        s_specs = [pl.BlockSpec((None,) + hshape, functools.partial(lambda i, pr, k: (k, 0, 0, 0), k=k))
                   for k in range(3)]
        o_spec = pl.BlockSpec((None,) + hshape, lambda i, pr: (l, 0, pr[1], 0))
    else:
        tr = _row_tile(hshape[0], hshape[1], 16)
        steps = hshape[0] // tr
        grid = (steps,)
        if axis == 0:
            p_spec = pl.BlockSpec((None, tr, hshape[1]), lambda i, pr: (pr[0], i, 0))
        else:
            p_spec = pl.BlockSpec((tr, hshape[1]), lambda i, pr: (i, pr[0]))
        s_specs = [pl.BlockSpec((None, tr, hshape[1]), functools.partial(lambda i, pr, k: (k, i, 0), k=k))
                   for k in range(3)]
        o_spec = pl.BlockSpec((None, tr, hshape[1]), lambda i, pr: (l, pr[1] * steps + i, 0))
    has_out = out is not None

    def body(pr_ref, p_ref, s0_ref, s1_ref, s2_ref, *rest):
        del pr_ref
        rest[-1][...] = ((p_ref[...].astype(F32) + s0_ref[...].astype(F32)) + s1_ref[...].astype(F32)) \
            + s2_ref[...].astype(F32)

    return pl.pallas_call(
        body, name=name,
        grid_spec=pltpu.PrefetchScalarGridSpec(
            num_scalar_prefetch=1, grid=grid, in_specs=[p_spec] + s_specs + ([ANY] if has_out else []),
            out_specs=o_spec),
        out_shape=jax.ShapeDtypeStruct((2,) + shard, F32), input_output_aliases={5: 0} if has_out else {},
        compiler_params=_params(1))(place, part, slots, slots, slots, *([out] if has_out else []))


GEOM = {name: (axis, size) for (name, _, axis, size) in BIG}
SHAPE = {name: shape for (name, shape, _, _) in BIG}
RIDES_IN_PROJ_L0 = ((0, ("w_pool", "w_branch_a", "w_branch_b", "w_out", "w_up")),)
RIDES_UP_PROJ_L0 = ((0, ("w_down", "w_ple_gate", "w_ple")), (1, ("w_in",)))
RIDES_DOWN_PROJ_L0 = ((1, ("w_pool", "w_branch_a", "w_branch_b", "w_out", "w_ple_gate", "w_ple")),)
RIDES_IN_PROJ_L1 = ((1, ("w_up", "w_down")),)
EARLY_GRADS_L0 = ("w_ple", "w_ple_gate", "w_down", "w_up")
LATE_GRADS_L0 = ("w_out", "w_branch_a", "w_branch_b", "w_pool", "w_in")


def _swap_of(G, names):
    return swap_exchange([G[k] for k in names], [GEOM[k] for k in names])


def _after_swap(G, names, got, place, tag):
    parts = [pair_sum(G[k], r, *GEOM[k], place, name=f"pair_sum_{k}_{tag}") for k, r in zip(names, got)]
    return scatter_exchange(parts, [GEOM[k] for k in names], [SHAPE[k] for k in names]), parts


def _reduce_start(G, names, place, tag):
    got = run_exchanges([_swap_of(G, names)], name=f"swap_halves_{tag}")
    return _after_swap(G, names, got, place, tag)


def _reduce_end(names, parts, slots, place, l, reduced):
    for k, q, s in zip(names, parts, slots):
        reduced[k] = chip_sum(q, s, SHAPE[k], *GEOM[k], l, place, reduced.get(k), name=f"chip_sum_{k}_l{l}")


def _local_step(x, p2, tgt, W0, W1, conv_w, small, place):
    T = x.shape[0]
    as3 = lambda a: a.reshape(2, 1, a.shape[-1])
    mix3, scale3, sgu3 = as3(small["mix_norm"]), as3(small["pool_scale"]), as3(small["sgu_norm"])
    ffn3, ple3, convb3 = as3(small["ffn_norm"]), as3(small["ple_norm"]), as3(small["conv_b"])
    tril = jnp.tril(jnp.ones((CHUNK, CHUNK), F32))
    ws_masked = small["w_spatial"] * tril
    wsm = ws_masked.astype(BF16)
    wsmT = jnp.swapaxes(ws_masked, -1, -2).astype(BF16)
    bT = jnp.swapaxes(small["b_spatial"], -1, -2)
    final3 = small["final_norm"].reshape(1, D)
    W = [dict(W0), dict(W1)]

    def riders(groups):
        return [gather_exchange([W[lyr][k] for k in names], [GEOM[k] for k in names]) for lyr, names in groups]

    def landed(groups, got):
        for lyr, names in groups:
            W[lyr].update(zip(names, got[:len(names)]))
            got = got[len(names):]

    saved = []
    hb = norm_fwd(x, mix3, 0, name="mix_norm_fwd_l0")
    for l in range(2):
        n = lambda s: f"{s}_l{l}"
        Wl = W[l]
        groups = RIDES_IN_PROJ_L0 if l == 0 else RIDES_IN_PROJ_L1
        z, got = mm_nn(hb, Wl["w_in"], 0, name=n("in_proj"), rows=T, tn=1280, out_dtype=BF16, host=riders(groups))
        landed(groups, got)
        a_in = pool_fwd(z, Wl["w_pool"], scale3, l, name=n("pool_fwd"))
        s_in = sgu_fwd(z, sgu3, wsm, bT, l, name=n("sgu_fwd"))
        yab = mm_nn(a_in, Wl["w_branch_a"], 0, name=n("branch_a"), rows=T, out_cols=2 * D, out_dtype=BF16)
        yab = mm_nn(s_in, Wl["w_branch_b"], 0, name=n("branch_b"), rows=T, out=yab, out_cols=2 * D, out_col_off=D,
                    out_dtype=BF16)
        mo = gate_fwd(z, yab, name=n("gate_fwd"))
        x1, h2b = mm_nn(mo, Wl["w_out"], 0, name=n("out_proj"), rows=T, resid=x, norm_gain=ffn3[l:l + 1])
        if l == 0:
            up, got = mm_nn(h2b, Wl["w_up"], 0, name=n("up_proj"), rows=T, tn=DFF, host=riders(RIDES_UP_PROJ_L0))
            landed(RIDES_UP_PROJ_L0, got)
        else:
            up = mm_nn(h2b, Wl["w_up"], 0, name=n("up_proj"), rows=T, tn=DFF)
        f = conv_fwd(up, conv_w, convb3, l, name=n("conv_fwd"))
        if l == 0:
            (x2, h3b), got = mm_nn(f, Wl["w_down"], 0, name=n("down_proj"), rows=T, resid=x1,
                                   norm_gain=ple3[l:l + 1], host=riders(RIDES_DOWN_PROJ_L0))
            landed(RIDES_DOWN_PROJ_L0, got)
        else:
            x2, h3b = mm_nn(f, Wl["w_down"], 0, name=n("down_proj"), rows=T, resid=x1, norm_gain=ple3[l:l + 1])
        pg = mm_nn(h3b, Wl["w_ple_gate"], 0, name=n("ple_gate_proj"), rows=T, out_dtype=BF16)
        e = mm_nn(p2, Wl["w_ple"], 0, name=n("ple_proj"), rows=T, a_row_off=l * T, out_dtype=BF16)
        saved.append(dict(x=x, hb=hb, z=z, a_in=a_in, s_in=s_in, yab=yab, mo=mo, x1=x1, h2b=h2b, up=up, f=f,
                          x2=x2, h3b=h3b, pg=pg, e=e))
        if l == 0:
            x, hb = ple_fwd(x2, pg, e, mix3, 1, name=n("ple_fwd"))
        else:
            x = ple_fwd(x2, pg, e, None, 0, name=n("ple_fwd"))

    loss_acc, dx, dg_final = loss_head(x, final3, tgt, name="loss_head")

    small_grads = [None, None]
    all_names = [t[0] for t in BIG]
    reduced = {}
    swap1 = G1 = scatter1 = parts1 = slots1 = None
    for l in (1, 0):
        n = lambda s: f"{s}_l{l}"
        a, Wl, G = saved[l], W[l], {}
        de, dpg = ple_bwd(dx, a["pg"], a["e"], name=n("ple_bwd"))
        G["w_ple"] = mm_tn(p2, de, name=n("d_w_ple"), rows=T, ka=PG, nb=D, a_row_off=l * T)
        G["w_ple_gate"] = mm_tn(a["h3b"], dpg, name=n("d_w_ple_gate"), rows=T, ka=D, nb=D)
        if l == 0:
            (dx2, dg_ple), got = mm_nt(dpg, Wl["w_ple_gate"], 0, name=n("ple_norm_bwd"), rows=T,
                                       norm_bwd_of=(a["x2"], ple3, l, dx), host=[swap1])
            scatter1, parts1 = _after_swap(G1, all_names, got, place, "l1")
        else:
            dx2, dg_ple = mm_nt(dpg, Wl["w_ple_gate"], 0, name=n("ple_norm_bwd"), rows=T,
                                norm_bwd_of=(a["x2"], ple3, l, dx))
        df = mm_nt(dx2, Wl["w_down"], 0, name=n("d_ffn_act"), rows=T, out_dtype=F32)
        G["w_down"] = mm_tn(a["f"], dx2, name=n("d_w_down"), rows=T, ka=DFF, nb=D, tm=1408)
        if l == 0:
            dup, dcw, dcb, slots1 = conv_bwd(df, a["up"], conv_w, convb3, l, name=n("conv_bwd"), host=[scatter1])
        else:
            dup, dcw, dcb, _ = conv_bwd(df, a["up"], conv_w, convb3, l, name=n("conv_bwd"))
        G["w_up"] = mm_tn(a["h2b"], dup, name=n("d_w_up"), rows=T, ka=D, nb=2 * DFF, tn=DFF, tk=1024)
        if l == 0:
            scatter_early, parts_early = _reduce_start(G, EARLY_GRADS_L0, place, "l0_early")
            (dx1, dg_ffn), slots_early = mm_nt(dup, Wl["w_up"], 0, name=n("ffn_norm_bwd"), rows=T, tk=1408,
                                               norm_bwd_of=(a["x1"], ffn3, l, dx2), host=[scatter_early])
        else:
            dx1, dg_ffn = mm_nt(dup, Wl["w_up"], 0, name=n("ffn_norm_bwd"), rows=T, tk=1408,
                                norm_bwd_of=(a["x1"], ffn3, l, dx2))
        dmo = mm_nt(dx1, Wl["w_out"], 0, name=n("d_gated"), rows=T)
        G["w_out"] = mm_tn(a["mo"], dx1, name=n("d_w_out"), rows=T, ka=D, nb=D)
        dz, dyab = gate_bwd(dmo, a["z"], a["yab"], name=n("gate_bwd"))
        G["w_branch_a"] = mm_tn(a["a_in"], dyab, name=n("d_w_branch_a"), rows=T, ka=D, nb=D)
        G["w_branch_b"] = mm_tn(a["s_in"], dyab, name=n("d_w_branch_b"), rows=T, ka=D, nb=D, b_col_off=D)
        da = mm_nt(dyab, Wl["w_branch_a"], 0, name=n("d_pool_out"), rows=T, kdim=D)
        ds = mm_nt(dyab, Wl["w_branch_b"], 0, name=n("d_sgu_out"), rows=T, kdim=D, a_col_off=D)
        dz, dwp, dsc, dws, dbt, dgs = mixer_bwd(da, ds, a["z"], dz, Wl["w_pool"], scale3, sgu3, wsm, wsmT, bT, l,
                                                name=n("mixer_bwd"))
        G["w_pool"] = dwp.astype(BF16)[None]
        G["w_in"] = mm_tn(a["hb"], dz, name=n("d_w_in"), rows=T, ka=D, nb=5 * D, tn=1280)
        if l == 0:
            scatter_late, parts_late = _reduce_start(G, LATE_GRADS_L0, place, "l0_late")
            _reduce_end(all_names, parts1, slots1, place, 1, reduced)
            _reduce_end(EARLY_GRADS_L0, parts_early, slots_early, place, 0, reduced)
            done = [(t, 1) for t in range(len(all_names))] + [(all_names.index(k), 0) for k in EARLY_GRADS_L0]
            (dx, dg_mix), got = mm_nt(
                dz, Wl["w_in"], 0, name=n("mix_norm_bwd"), rows=T, tk=1280, norm_bwd_of=(a["x"], mix3, l, dx1),
                host=[scatter_late, share_exchange([reduced[k] for k in all_names], done)])
            slots_late = got[:len(LATE_GRADS_L0)]
            reduced.update(zip(all_names, got[len(LATE_GRADS_L0):]))
        else:
            dx, dg_mix = mm_nt(dz, Wl["w_in"], 0, name=n("mix_norm_bwd"), rows=T, tk=1280,
                               norm_bwd_of=(a["x"], mix3, l, dx1))
            swap1, G1 = _swap_of(G, all_names), G
        small_grads[l] = dict(
            mix_norm=dg_mix[0], pool_scale=dsc[0], sgu_norm=dgs[0], w_spatial=dws, b_spatial=dbt.T,
            ffn_norm=dg_ffn[0], conv_b=jnp.concatenate([dcb[0, 0], dcb[1, 0]]), ple_norm=dg_ple[0],
            conv_w=jnp.concatenate([dcw[0, :3], dcw[1, :3]], axis=1))
    _reduce_end(LATE_GRADS_L0, parts_late, slots_late, place, 0, reduced)
    return loss_acc, dx, reduced, small_grads, dg_final[0]


SMALL_ORDER = ("mix_norm", "pool_scale", "sgu_norm", "w_spatial", "b_spatial", "ffn_norm", "conv_b", "ple_norm",
               "conv_w")


def _pack_rows(pieces, row_multiple):
    flat = jnp.concatenate([a.reshape(-1) for a in pieces])
    rows = -(-flat.shape[0] // LANES)
    rows = -(-rows // row_multiple) * row_multiple
    return jnp.pad(flat, (0, rows * LANES - flat.shape[0])).reshape(rows, LANES)


def _unpack(flat, shapes):
    out, off = [], 0
    for shp in shapes:
        size = 1
        for s in shp:
            size *= s
        out.append(flat[off:off + size].reshape(shp))
        off += size
    return out


def kernel(x, p, mix_norm, w_in, w_pool, pool_scale, sgu_norm, w_spatial, b_spatial, w_branch_a, w_branch_b, w_out, ffn_norm, w_up, conv_w, conv_b, w_down, ple_norm, w_ple_gate, w_ple, final_norm, loss_target, m_mix_norm, m_w_in, m_w_pool, m_pool_scale, m_sgu_norm, m_w_spatial, m_b_spatial, m_w_branch_a, m_w_branch_b, m_w_out, m_ffn_norm, m_w_up, m_conv_w, m_conv_b, m_w_down, m_ple_norm, m_w_ple_gate, m_w_ple, m_final_norm, v_mix_norm, v_w_in, v_w_pool, v_pool_scale, v_sgu_norm, v_w_spatial, v_b_spatial, v_w_branch_a, v_w_branch_b, v_w_out, v_ffn_norm, v_w_up, v_conv_w, v_conv_b, v_w_down, v_ple_norm, v_w_ple_gate, v_w_ple, v_final_norm):
    names = ["mix_norm", "w_in", "w_pool", "pool_scale", "sgu_norm", "w_spatial", "b_spatial", "w_branch_a",
             "w_branch_b", "w_out", "ffn_norm", "w_up", "conv_w", "conv_b", "w_down", "ple_norm", "w_ple_gate",
             "w_ple", "final_norm"]
    w = dict(zip(names, [mix_norm, w_in, w_pool, pool_scale, sgu_norm, w_spatial, b_spatial, w_branch_a, w_branch_b,
                         w_out, ffn_norm, w_up, conv_w, conv_b, w_down, ple_norm, w_ple_gate, w_ple, final_norm]))
    m = dict(zip(names, [m_mix_norm, m_w_in, m_w_pool, m_pool_scale, m_sgu_norm, m_w_spatial, m_b_spatial,
                         m_w_branch_a, m_w_branch_b, m_w_out, m_ffn_norm, m_w_up, m_conv_w, m_conv_b, m_w_down,
                         m_ple_norm, m_w_ple_gate, m_w_ple, m_final_norm]))
    v = dict(zip(names, [v_mix_norm, v_w_in, v_w_pool, v_pool_scale, v_sgu_norm, v_w_spatial, v_b_spatial,
                         v_w_branch_a, v_w_branch_b, v_w_out, v_ffn_norm, v_w_up, v_conv_w, v_conv_b, v_w_down,
                         v_ple_norm, v_w_ple_gate, v_w_ple, v_final_norm]))
    T = x.shape[1]
    chip = 2 * lax.axis_index("x") + lax.axis_index("y")
    place = jnp.stack([chip, lax.axis_index("c")]).astype(jnp.int32)

    big_names = [t[0] for t in BIG]
    placed = [{k: place_shard(w[k], l, *GEOM[k], BF16, place, name=f"place_{k}_l{l}") for k in big_names}
              for l in range(2)]
    conv_w8 = jnp.pad(conv_w, ((0, 0), (0, CONV_ROWS - conv_w.shape[1]), (0, 0)))
    conv_placed = place_both_layers(conv_w8, 1, conv_w.shape[2], place, name="place_conv_w")
    w_in0, conv_w_all = run_exchanges([gather_exchange([placed[0]["w_in"]], [GEOM["w_in"]]),
                                       gather_by_layer_exchange(conv_placed, 1, conv_w.shape[2])],
                                      name="gather_first_weights")
    placed[0]["w_in"] = w_in0

    small = {k: w[k] for k in ("mix_norm", "pool_scale", "sgu_norm", "w_spatial", "b_spatial", "ffn_norm",
                               "conv_b", "ple_norm", "final_norm")}
    loss_acc, dx, reduced, small_grads, dg_final = _local_step(
        x.reshape(T, D), p.reshape(2 * T, p.shape[-1]), loss_target.reshape(T, D), placed[0], placed[1], conv_w_all,
        small, place)
    full = run_exchanges([share_exchange([reduced[k] for k in big_names],
                                         [(big_names.index(k), 0) for k in LATE_GRADS_L0])], name="share_last_halves")
    grads = dict(zip(big_names, full))

    pieces = [small_grads[l][k] for l in range(2) for k in SMALL_ORDER] + [dg_final, loss_acc[0, 0].reshape(1)]
    shapes = [a.shape for a in pieces]
    total = all_reduce_small(_pack_rows(pieces, 16)).reshape(-1)
    summed = _unpack(total, shapes)
    per_layer = {k: jnp.stack([summed[i], summed[len(SMALL_ORDER) + i]]) for i, k in enumerate(SMALL_ORDER)}
    for k in ("mix_norm", "pool_scale", "sgu_norm", "w_spatial", "b_spatial", "ffn_norm", "conv_b", "ple_norm"):
        grads[k] = per_layer[k]
    grads["final_norm"] = summed[-2]
    loss = summed[-1].reshape(())
    cw = conv_w.shape[2]
    grads["conv_w"] = lax.dynamic_slice_in_dim(per_layer["conv_w"], chip * cw, cw, axis=2)

    delta, new_m, new_v = {}, {}, {}
    for name in big_names:
        shp = w[name].shape
        d_, m_, v_, g_ = elementwise(lambda w_, g_, m_, v_: (*_adamw(w_, g_, m_, v_), g_),
                                     [_view2d(a) for a in (w[name], grads[name], m[name], v[name])],
                                     [F32, F32, F32, F32], name=f"adamw_{name}")
        delta[name], new_m[name], new_v[name] = d_.reshape(shp), m_.reshape(shp), v_.reshape(shp)
        grads[name] = g_.reshape(shp)
    small_names = [k for k in names if k not in big_names]
    small_shapes = [w[k].shape for k in small_names]
    packed = [_pack_rows([src[k] for k in small_names], 8) for src in (w, grads, m, v)]
    outs = elementwise(_adamw, packed, [F32, F32, F32], name="adamw_small")
    for dst, o in zip((delta, new_m, new_v), outs):
        for k, a in zip(small_names, _unpack(o.reshape(-1), small_shapes)):
            dst[k] = a

    return (loss, dx.reshape(1, T, D), *[grads[k] for k in names], *[delta[k] for k in names],
            *[new_m[k] for k in names], *[new_v[k] for k in names])
```

```python
import functools

import jax
import jax.numpy as jnp
from jax import lax
from jax.experimental import pallas as pl
from jax.experimental.pallas import tpu as pltpu

F32 = jnp.float32
BF16 = jnp.bfloat16
EPS = 1e-6
D = 1024
POOL_WINDOWS = (2, 4, 8, 16)
PG = 256
POOL_HALO = 16
CHUNK = 128
HEADS = 8
DFF = 2816
CONV_HALO = 8
CONV_TC = 1408
N_CHIPS = 4
LANES = 128
VMEM_LIMIT = 56 * 1024 * 1024
MESH = pl.DeviceIdType.MESH
ANY = pl.BlockSpec(memory_space=pl.ANY)

ADAM_LR = 0.001
ADAM_B1 = 0.9
ADAM_B2 = 0.999
ADAM_EPS = 1e-08
ADAM_WD = 0.01
ADAM_STEP = 10

BIG = (
    ("w_in", (D, 5 * D), 1, 5 * D // N_CHIPS),
    ("w_pool", (4, PG, PG), 1, PG // N_CHIPS),
    ("w_branch_a", (D, D), 0, D // N_CHIPS),
    ("w_branch_b", (D, D), 0, D // N_CHIPS),
    ("w_out", (D, D), 0, D // N_CHIPS),
    ("w_up", (D, 2 * DFF), 1, 2 * DFF // N_CHIPS),
    ("w_down", (DFF, D), 0, DFF // N_CHIPS),
    ("w_ple_gate", (D, D), 0, D // N_CHIPS),
    ("w_ple", (PG, D), 1, D // N_CHIPS),
)
CONV_ROWS = 8


def _params(n_axes):
    return pltpu.CompilerParams(dimension_semantics=("arbitrary",) * n_axes, vmem_limit_bytes=VMEM_LIMIT)


def _gelu(x):
    return 0.5 * x * (1.0 + lax.erf(x * 0.7071067811865476))


def _gelu_grad(x):
    return 0.5 * (1.0 + lax.erf(x * 0.7071067811865476)) + x * jnp.exp(-0.5 * x * x) * 0.3989422804014327


def _shard_shape(shape, axis, size):
    return tuple(size if a == axis else s for a, s in enumerate(shape))


def _block(ref, axis, j, size):
    idx = tuple(pl.ds(j * size, size) if a == axis else slice(None) for a in range(len(ref.shape)))
    return ref.at[idx]


def mm_nn(a, w, l, *, name, rows, out_dtype=F32, resid=None, a_row_off=0, out=None, out_cols=None,
          out_col_off=0, norm_gain=None, host=None, tm=1024, tn=None, tk=None):
    K, N = w.shape[1], w.shape[2]
    tn = tn or N
    tk = tk or K
    nk = K // tk
    out_cols = out_cols or N
    assert rows % tm == 0 and N % tn == 0 and K % tk == 0 and out_col_off % tn == 0 and a_row_off % tm == 0
    has_resid, has_out, has_norm = resid is not None, out is not None, norm_gain is not None
    assert not has_norm or (tn == N and not has_out)
    grid = (N // tn, rows // tm, nk)
    hosting = _Hosting(host)
    n_in = 2 + has_resid + has_norm + has_out
    n_host_in, n_host_out = len(hosting.arrays), len(hosting.out_shapes)
    n_own_out = 1 + has_norm

    def body(*refs):
        refs = list(refs)
        a_ref, w_ref = refs[0], refs[1]
        r_ref = refs[2] if has_resid else None
        g_ref = refs[2 + has_resid] if has_norm else None
        host_in = refs[n_in:n_in + n_host_in]
        o_base = n_in + n_host_in
        o_ref = refs[o_base]
        host_out = refs[o_base + n_own_out:o_base + n_own_out + n_host_out]
        scratch = refs[o_base + n_own_out + n_host_out:]
        if hosting.plan:
            first, last = _first_last(grid)
            sems = scratch[-2:]

            @pl.when(first)
            def _():
                hosting.begin(host_in, host_out, *sems)

        part = jnp.dot(a_ref[...].astype(BF16), w_ref[...], preferred_element_type=F32)

        def finish(r):
            if has_resid:
                r = r + r_ref[...]
            o_ref[...] = r.astype(o_ref.dtype)
            if has_norm:
                scale = lax.rsqrt(jnp.mean(r * r, axis=-1, keepdims=True) + EPS)
                refs[o_base + 1][...] = (r * scale * g_ref[...]).astype(BF16)

        if nk == 1:
            finish(part)
        else:
            acc = scratch[0]
            k = pl.program_id(2)

            @pl.when(k == 0)
            def _():
                acc[...] = part

            @pl.when(k > 0)
            def _():
                acc[...] += part

            @pl.when(k == nk - 1)
            def _():
                finish(acc[...])

        if hosting.plan:
            @pl.when(last)
            def _():
                hosting.finish(host_in, host_out, *sems)

    in_specs = [pl.BlockSpec((tm, tk), lambda j, i, k: (i + a_row_off // tm, k)),
                pl.BlockSpec((None, tk, tn), lambda j, i, k: (l, k, j))]
    args = [a, w]
    if has_resid:
        in_specs.append(pl.BlockSpec((tm, tn), lambda j, i, k: (i, j)))
        args.append(resid)
    if has_norm:
        in_specs.append(pl.BlockSpec((None, 1, tn), lambda j, i, k: (l, 0, 0)))
        args.append(norm_gain)
    aliases = {}
    if has_out:
        in_specs.append(ANY)
        aliases = {len(args): 0}
        args.append(out)
    aliases.update(hosting.aliases(n_in, n_own_out))
    out_specs = [pl.BlockSpec((tm, tn), lambda j, i, k: (i, j + out_col_off // tn))]
    out_shape = [jax.ShapeDtypeStruct((rows, out_cols), out_dtype)]
    if has_norm:
        out_specs.append(pl.BlockSpec((tm, tn), lambda j, i, k: (i, j)))
        out_shape.append(jax.ShapeDtypeStruct((rows, N), BF16))
    res = pl.pallas_call(
        body, name=name, grid=grid,
        in_specs=in_specs + [ANY] * n_host_in,
        out_specs=out_specs + [ANY] * n_host_out,
        out_shape=out_shape + hosting.out_shapes,
        scratch_shapes=([pltpu.VMEM((tm, tn), F32)] if nk > 1 else []) + hosting.scratch(),
        input_output_aliases=aliases, compiler_params=_params(3))(*args, *hosting.arrays)
    own = res[0] if n_own_out == 1 else tuple(res[:n_own_out])
    return (own, list(res[n_own_out:])) if hosting.plan else own


def mm_nt(a, w, l, *, name, rows, kdim=None, a_col_off=0, out_dtype=BF16, norm_bwd_of=None, host=None, tm=1024,
          tn=None, tk=None):
    R = w.shape[1]
    kdim = kdim or w.shape[2]
    tn = tn or R
    tk = tk or kdim
    nk = kdim // tk
    assert rows % tm == 0 and R % tn == 0 and kdim % tk == 0 and a_col_off % tk == 0
    fused = norm_bwd_of is not None
    assert not fused or tn == R
    grid = (R // tn, rows // tm, nk)
    hosting = _Hosting(host)
    n_host_in, n_host_out = len(hosting.arrays), len(hosting.out_shapes)
    n_own_in, n_own_out = (3, 2) if fused else (0, 1)

    def body(a_ref, w_ref, *refs):
        host_in = refs[n_own_in:n_own_in + n_host_in]
        host_out = refs[n_own_in + n_host_in + n_own_out:n_own_in + n_host_in + n_own_out + n_host_out]
        scratch = refs[n_own_in + n_host_in + n_own_out + n_host_out:]
        rest = list(refs[:n_own_in]) + list(refs[n_own_in + n_host_in:n_own_in + n_host_in + n_own_out]) \
            + ([scratch[0]] if nk > 1 else [])
        if hosting.plan:
            first, last = _first_last(grid)
            sems = scratch[-2:]

            @pl.when(first)
            def _():
                hosting.begin(host_in, host_out, *sems)

        part = lax.dot_general(a_ref[...].astype(BF16), w_ref[...], (((1,), (1,)), ((), ())),
                               preferred_element_type=F32)
        i, k = pl.program_id(1), pl.program_id(2)

        def finish(dh):
            if not fused:
                rest[0][...] = dh.astype(rest[0].dtype)
                return
            x_ref, g_ref, dxi_ref, dx_ref, dg_ref = rest[:5]

            @pl.when(i == 0)
            def _():
                dg_ref[...] = jnp.zeros_like(dg_ref)

            xv = x_ref[...]
            r = lax.rsqrt(jnp.mean(xv * xv, axis=-1, keepdims=True) + EPS)
            xh = xv * r
            dhg = dh * g_ref[...]
            dx_ref[...] = dxi_ref[...] + r * (dhg - xh * jnp.mean(dhg * xh, axis=-1, keepdims=True))
            dg_ref[0:1, :] += jnp.sum(dh * xh, axis=0, keepdims=True)

        if nk == 1:
            finish(part)
        else:
            acc = rest[-1]

            @pl.when(k == 0)
            def _():
                acc[...] = part

            @pl.when(k > 0)
            def _():
                acc[...] += part

            @pl.when(k == nk - 1)
            def _():
                finish(acc[...])

        if hosting.plan:
            @pl.when(last)
            def _():
                hosting.finish(host_in, host_out, *sems)

    if a.ndim == 3:
        per = a.shape[2] // tk
        a_spec = pl.BlockSpec((None, tm, tk), lambda j, i, k: (k // per, i, k % per))
    else:
        a_spec = pl.BlockSpec((tm, tk), lambda j, i, k: (i, k + a_col_off // tk))
    in_specs = [a_spec, pl.BlockSpec((None, tn, tk), lambda j, i, k: (l, j, k))]
    args = [a, w]
    row_tile = pl.BlockSpec((tm, tn), lambda j, i, k: (i, j))
    if fused:
        x, gain, gl, dx_in = norm_bwd_of
        in_specs += [row_tile, pl.BlockSpec((None, 1, tn), lambda j, i, k: (gl, 0, 0)), row_tile]
        args += [x, gain, dx_in]
        out_specs = [row_tile, pl.BlockSpec((8, tn), lambda j, i, k: (0, 0))]
        out_shape = [jax.ShapeDtypeStruct((rows, R), F32), jax.ShapeDtypeStruct((8, R), F32)]
    else:
        out_specs, out_shape = [row_tile], [jax.ShapeDtypeStruct((rows, R), out_dtype)]
    res = pl.pallas_call(
        body, name=name, grid=grid, in_specs=in_specs + [ANY] * n_host_in,
        out_specs=out_specs + [ANY] * n_host_out, out_shape=out_shape + hosting.out_shapes,
        scratch_shapes=([pltpu.VMEM((tm, tn), F32)] if nk > 1 else []) + hosting.scratch(),
        input_output_aliases=hosting.aliases(2 + n_own_in, n_own_out), compiler_params=_params(3))(
            *args, *hosting.arrays)
    own = tuple(res[:n_own_out]) if fused else res[0]
    return (own, list(res[n_own_out:])) if hosting.plan else own


def mm_tn(a, b, *, name, rows, ka, nb, a_row_off=0, b_col_off=0, tm=None, tn=None, tk=2048):
    tm = tm or ka
    tn = tn or nb
    tk = min(tk, rows)
    nk = rows // tk
    assert ka % tm == 0 and nb % tn == 0 and rows % tk == 0 and b_col_off % tn == 0 and a_row_off % tk == 0

    def body(a_ref, b_ref, o_ref, acc):
        part = lax.dot_general(a_ref[...].astype(BF16), b_ref[...].astype(BF16), (((0,), (0,)), ((), ())),
                               preferred_element_type=F32)
        k = pl.program_id(2)

        @pl.when(k == 0)
        def _():
            acc[...] = part

        @pl.when(k > 0)
        def _():
            acc[...] += part

        @pl.when(k == nk - 1)
        def _():
            o_ref[...] = acc[...].astype(o_ref.dtype)

    if b.ndim == 3:
        per = b.shape[2] // tn
        b_spec = pl.BlockSpec((None, tk, tn), lambda j, i, k: (j // per, k, j % per))
    else:
        b_spec = pl.BlockSpec((tk, tn), lambda j, i, k: (k, j + b_col_off // tn))
    return pl.pallas_call(
        body, name=name, grid=(nb // tn, ka // tm, nk),
        in_specs=[pl.BlockSpec((tk, tm), lambda j, i, k: (k + a_row_off // tk, i)), b_spec],
        out_specs=pl.BlockSpec((None, tm, tn), lambda j, i, k: (0, i, j)),
        out_shape=jax.ShapeDtypeStruct((1, ka, nb), BF16),
        scratch_shapes=[pltpu.VMEM((tm, tn), F32)], compiler_params=_params(3))(a, b)


def _row_spec(tm, width, col=0):
    return pl.BlockSpec((tm, width), lambda i: (i, col))


def _gain_spec(l, width=D):
    return pl.BlockSpec((None, 1, width), lambda i: (l, 0, 0))


def norm_fwd(x, g3, l, *, name, tm=1024):
    T = x.shape[0]

    def body(x_ref, g_ref, o_ref):
        xv = x_ref[...]
        r = lax.rsqrt(jnp.mean(xv * xv, axis=-1, keepdims=True) + EPS)
        o_ref[...] = (xv * r * g_ref[...]).astype(BF16)

    return pl.pallas_call(
        body, name=name, grid=(T // tm,),
        in_specs=[_row_spec(tm, D), _gain_spec(l)], out_specs=_row_spec(tm, D),
        out_shape=jax.ShapeDtypeStruct((T, D), BF16), compiler_params=_params(1))(x, g3)


def _winsum_back(ext, w):
    s, span = ext, 1
    while span < w:
        s = s + pltpu.roll(s, span, 0)
        span *= 2
    return s


def _winsum_fwd(ext, w):
    rows = ext.shape[0]
    s, span = ext, 1
    while span < w:
        s = s + pltpu.roll(s, rows - span, 0)
        span *= 2
    return s


def _pooled(ext, z, t, g, w):
    sl = slice(g * PG, (g + 1) * PG)
    s = _winsum_back(ext[:, sl], w)[POOL_HALO:, :]
    return s / jnp.minimum(t + 1, w).astype(F32) - z[:, sl]


def pool_fwd(z, wpool, scale3, l, *, name, tm=512):
    T = z.shape[0]
    hb = tm // POOL_HALO
    wl = l if wpool.shape[0] > 1 else 0

    def body(z_ref, zp_ref, wp_ref, sc_ref, o_ref):
        i = pl.program_id(0)
        zv = z_ref[...].astype(F32)
        prev = jnp.where(i == 0, 0.0, zp_ref[...].astype(F32))
        ext = jnp.concatenate([prev, zv], axis=0)
        t = i * tm + lax.broadcasted_iota(jnp.int32, (tm, 1), 0)
        for g, w in enumerate(POOL_WINDOWS):
            sl = slice(g * PG, (g + 1) * PG)
            pooled = _pooled(ext, zv, t, g, w)
            q = jnp.dot(pooled.astype(BF16), wp_ref[g], preferred_element_type=F32)
            o_ref[:, sl] = (q * sc_ref[:, sl]).astype(BF16)

    return pl.pallas_call(
        body, name=name, grid=(T // tm,),
        in_specs=[_row_spec(tm, D),
                  pl.BlockSpec((POOL_HALO, D), lambda i: (jnp.maximum(i * hb - 1, 0), 0)),
                  pl.BlockSpec((None, 4, PG, PG), lambda i: (wl, 0, 0, 0)),
                  _gain_spec(l)],
        out_specs=_row_spec(tm, D),
        out_shape=jax.ShapeDtypeStruct((T, D), BF16), compiler_params=_params(1))(z, z, wpool, scale3)


def sgu_fwd(z, g3, wsm, bT, l, *, name, tm=512):
    T = z.shape[0]

    def body(zu_ref, zv_ref, g_ref, ws_ref, b_ref, o_ref):
        gu = _gelu(zu_ref[...].astype(F32))
        gv = _gelu(zv_ref[...].astype(F32))
        rv = lax.rsqrt(jnp.mean(gv * gv, axis=-1, keepdims=True) + EPS)
        vn = (gv * rv * g_ref[...]).astype(BF16)
        for n in range(tm // CHUNK):
            r = slice(n * CHUNK, (n + 1) * CHUNK)
            for h in range(HEADS):
                cs = slice(h * CHUNK, (h + 1) * CHUNK)
                mixed = jnp.dot(ws_ref[h], vn[r, cs], preferred_element_type=F32) + b_ref[:, h:h + 1]
                o_ref[r, cs] = (gu[r, cs] * mixed).astype(BF16)

    return pl.pallas_call(
        body, name=name, grid=(T // tm,),
        in_specs=[_row_spec(tm, D, 1), _row_spec(tm, D, 2), _gain_spec(l),
                  pl.BlockSpec((None, HEADS, CHUNK, CHUNK), lambda i: (l, 0, 0, 0)),
                  pl.BlockSpec((None, CHUNK, HEADS), lambda i: (l, 0, 0))],
        out_specs=_row_spec(tm, D),
        out_shape=jax.ShapeDtypeStruct((T, D), BF16), compiler_params=_params(1))(z, z, g3, wsm, bT)


def gate_fwd(z, yab, *, name, tm=1024):
    T = z.shape[0]

    def body(za_ref, zb_ref, y_ref, o_ref):
        ga = jax.nn.sigmoid(za_ref[...].astype(F32))
        gb = jax.nn.sigmoid(zb_ref[...].astype(F32))
        o_ref[...] = (ga * y_ref[:, :D].astype(F32) + gb * y_ref[:, D:].astype(F32)).astype(BF16)

    return pl.pallas_call(
        body, name=name, grid=(T // tm,),
        in_specs=[_row_spec(tm, D, 3), _row_spec(tm, D, 4), _row_spec(tm, 2 * D)],
        out_specs=_row_spec(tm, D),
        out_shape=jax.ShapeDtypeStruct((T, D), BF16), compiler_params=_params(1))(z, z, yab)


def _conv(ext, w_ref, b_ref):
    down1, down2 = pltpu.roll(ext, 1, 0), pltpu.roll(ext, 2, 0)
    c = b_ref[...] + w_ref[0:1, :] * down2
    c = c + w_ref[1:2, :] * down1
    return c + w_ref[2:3, :] * ext, down1, down2


def conv_fwd(up, convw, convb3, l, *, name, tm=512):
    T = up.shape[0]
    tc = CONV_TC
    nc = DFF // tc
    hb = tm // CONV_HALO

    def body(ua_ref, uap_ref, ub_ref, ubp_ref, wa_ref, wb_ref, ba_ref, bb_ref, o_ref):
        i = pl.program_id(1)

        def conv_of(u_ref, p_ref, w_ref, b_ref):
            ext = jnp.concatenate([jnp.where(i == 0, 0.0, p_ref[...]), u_ref[...]], axis=0)
            return _conv(ext, w_ref, b_ref)[0][CONV_HALO:, :]

        ca = conv_of(ua_ref, uap_ref, wa_ref, ba_ref)
        cb = conv_of(ub_ref, ubp_ref, wb_ref, bb_ref)
        o_ref[...] = (_gelu(ca) * cb).astype(BF16)

    def cur(off):
        return pl.BlockSpec((tm, tc), lambda j, i: (i, j + off))

    def prev(off):
        return pl.BlockSpec((CONV_HALO, tc), lambda j, i: (jnp.maximum(i * hb - 1, 0), j + off))

    def wspec(off):
        return pl.BlockSpec((None, CONV_ROWS, tc), lambda j, i: (l, 0, j + off))

    def bspec(off):
        return pl.BlockSpec((None, 1, tc), lambda j, i: (l, 0, j + off))

    return pl.pallas_call(
        body, name=name, grid=(nc, T // tm),
        in_specs=[cur(0), prev(0), cur(nc), prev(nc), wspec(0), wspec(nc), bspec(0), bspec(nc)],
        out_specs=pl.BlockSpec((tm, tc), lambda j, i: (i, j)),
        out_shape=jax.ShapeDtypeStruct((T, DFF), BF16),
        compiler_params=_params(2))(up, up, up, up, convw, convw, convb3, convb3)


def ple_fwd(x2, pg, e, g3, l, *, name, tm=1024):
    T = x2.shape[0]
    has_norm = g3 is not None

    def body(x_ref, pg_ref, e_ref, *rest):
        xv = x_ref[...] + jax.nn.sigmoid(pg_ref[...].astype(F32)) * e_ref[...].astype(F32)
        if has_norm:
            g_ref, o_ref, h_ref = rest
            r = lax.rsqrt(jnp.mean(xv * xv, axis=-1, keepdims=True) + EPS)
            h_ref[...] = (xv * r * g_ref[...]).astype(BF16)
        else:
            o_ref, = rest
        o_ref[...] = xv

    x_shape = jax.ShapeDtypeStruct((T, D), F32)
    return pl.pallas_call(
        body, name=name, grid=(T // tm,),
        in_specs=[_row_spec(tm, D)] * 3 + ([_gain_spec(l)] if has_norm else []),
        out_specs=[_row_spec(tm, D)] * 2 if has_norm else _row_spec(tm, D),
        out_shape=[x_shape, jax.ShapeDtypeStruct((T, D), BF16)] if has_norm else x_shape,
        compiler_params=_params(1))(x2, pg, e, *([g3] if has_norm else []))


def loss_head(x, g3, tgt, *, name, tm=1024):
    T = x.shape[0]

    def body(x_ref, g_ref, t_ref, loss_ref, dx_ref, dg_ref):
        @pl.when(pl.program_id(0) == 0)
        def _():
            loss_ref[...] = jnp.zeros_like(loss_ref)
            dg_ref[...] = jnp.zeros_like(dg_ref)

        xv, g = x_ref[...], g_ref[...]
        r = lax.rsqrt(jnp.mean(xv * xv, axis=-1, keepdims=True) + EPS)
        xh = xv * r
        err = xh * g - t_ref[...]
        loss_ref[...] += 0.5 * jnp.sum(jnp.mean(err * err, axis=-1, keepdims=True))
        dy = err * (1.0 / D)
        dyg = dy * g
        dx_ref[...] = r * (dyg - xh * jnp.mean(dyg * xh, axis=-1, keepdims=True))
        dg_ref[0:1, :] += jnp.sum(dy * xh, axis=0, keepdims=True)

    return pl.pallas_call(
        body, name=name, grid=(T // tm,),
        in_specs=[_row_spec(tm, D), pl.BlockSpec((1, D), lambda i: (0, 0)), _row_spec(tm, D)],
        out_specs=[pl.BlockSpec((8, LANES), lambda i: (0, 0)), _row_spec(tm, D),
                   pl.BlockSpec((8, D), lambda i: (0, 0))],
        out_shape=[jax.ShapeDtypeStruct((8, LANES), F32), jax.ShapeDtypeStruct((T, D), F32),
                   jax.ShapeDtypeStruct((8, D), F32)],
        compiler_params=_params(1))(x, g3, tgt)


def ple_bwd(dx, pg, e, *, name, tm=1024):
    T = dx.shape[0]

    def body(dx_ref, pg_ref, e_ref, de_ref, dpg_ref):
        gate = jax.nn.sigmoid(pg_ref[...].astype(F32))
        dxv = dx_ref[...]
        de_ref[...] = (dxv * gate).astype(BF16)
        dpg_ref[...] = (dxv * e_ref[...].astype(F32) * gate * (1.0 - gate)).astype(BF16)

    return pl.pallas_call(
        body, name=name, grid=(T // tm,),
        in_specs=[_row_spec(tm, D)] * 3, out_specs=[_row_spec(tm, D)] * 2,
        out_shape=[jax.ShapeDtypeStruct((T, D), BF16)] * 2, compiler_params=_params(1))(dx, pg, e)


def conv_bwd(df, up, convw, convb3, l, *, name, host=None, tm=256):
    T = up.shape[0]
    tc = CONV_TC
    nc = DFF // tc
    hb = tm // CONV_HALO
    nt = T // tm
    rows = tm + 2 * CONV_HALO
    own = slice(CONV_HALO, CONV_HALO + tm)

    hosting = _Hosting(host)
    n_host_in, n_host_out = len(hosting.arrays), len(hosting.out_shapes)

    def body(df_ref, dfn_ref, ua_ref, uap_ref, uan_ref, ub_ref, ubp_ref, ubn_ref, wa_ref, wb_ref, ba_ref, bb_ref,
             *rest):
        host_in = rest[:n_host_in]
        dup_ref, dcw_ref, dcb_ref = rest[n_host_in:n_host_in + 3]
        host_out = rest[n_host_in + 3:n_host_in + 3 + n_host_out]
        sems = rest[n_host_in + 3 + n_host_out:]
        i = pl.program_id(1)
        if hosting.plan:
            first, last = _first_last((nc, nt))

            @pl.when(first)
            def _():
                hosting.begin(host_in, host_out, *sems)

        @pl.when(i == 0)
        def _():
            dcw_ref[...] = jnp.zeros_like(dcw_ref)
            dcb_ref[...] = jnp.zeros_like(dcb_ref)

        def ext_of(c_ref, p_ref, n_ref):
            return jnp.concatenate([jnp.where(i == 0, 0.0, p_ref[...]), c_ref[...],
                                    jnp.where(i == nt - 1, 0.0, n_ref[...])], axis=0)

        ea = ext_of(ua_ref, uap_ref, uan_ref)
        eb = ext_of(ub_ref, ubp_ref, ubn_ref)
        ca, ea1, ea2 = _conv(ea, wa_ref, ba_ref)
        cb, eb1, eb2 = _conv(eb, wb_ref, bb_ref)
        df_ext =jnp.concatenate([jnp.zeros((CONV_HALO, tc), F32), df_ref[...],
                                  jnp.where(i == nt - 1, 0.0, dfn_ref[...])], axis=0)
        cdf = 0.5 * (1.0 + lax.erf(ca * 0.7071067811865476))
        da = df_ext * cb * (cdf + ca * jnp.exp(-0.5 * ca * ca) * 0.3989422804014327)
        db = df_ext * (ca * cdf)

        def finish(h, dc, e, e1, e2, w_ref):
            dup = w_ref[2:3, :] * dc + w_ref[1:2, :] * pltpu.roll(dc, rows - 1, 0)
            dup = dup + w_ref[0:1, :] * pltpu.roll(dc, rows - 2, 0)
            dup_ref[h] = dup[own, :].astype(BF16)
            dco = dc[own, :]
            dcb_ref[h, 0:1, :] += jnp.sum(dco, axis=0, keepdims=True)
            dcw_ref[h, 0:1, :] += jnp.sum(dco * e2[own, :], axis=0, keepdims=True)
            dcw_ref[h, 1:2, :] += jnp.sum(dco * e1[own, :], axis=0, keepdims=True)
            dcw_ref[h, 2:3, :] += jnp.sum(dco * e[own, :], axis=0, keepdims=True)

        finish(0, da, ea, ea1, ea2, wa_ref)
        finish(1, db, eb, eb1, eb2, wb_ref)
        if hosting.plan:
            @pl.when(last)
            def _():
                hosting.finish(host_in, host_out, *sems)

    def nxt(i):
        return jnp.minimum((i + 1) * hb, T // CONV_HALO - 1)

    def prv(i):
        return jnp.maximum(i * hb - 1, 0)

    def up_specs(off):
        return [pl.BlockSpec((tm, tc), lambda j, i: (i, j + off)),
                pl.BlockSpec((CONV_HALO, tc), lambda j, i: (prv(i), j + off)),
                pl.BlockSpec((CONV_HALO, tc), lambda j, i: (nxt(i), j + off))]

    in_specs = [pl.BlockSpec((tm, tc), lambda j, i: (i, j)),
                pl.BlockSpec((CONV_HALO, tc), lambda j, i: (nxt(i), j)),
                *up_specs(0), *up_specs(nc),
                pl.BlockSpec((None, CONV_ROWS, tc), lambda j, i: (l, 0, j)),
                pl.BlockSpec((None, CONV_ROWS, tc), lambda j, i: (l, 0, j + nc)),
                pl.BlockSpec((None, 1, tc), lambda j, i: (l, 0, j)),
                pl.BlockSpec((None, 1, tc), lambda j, i: (l, 0, j + nc))]
    res = pl.pallas_call(
        body, name=name, grid=(nc, nt), in_specs=in_specs + [ANY] * n_host_in,
        out_specs=[pl.BlockSpec((2, tm, tc), lambda j, i: (0, i, j)),
                   pl.BlockSpec((2, 8, tc), lambda j, i: (0, 0, j)),
                   pl.BlockSpec((2, 8, tc), lambda j, i: (0, 0, j))] + [ANY] * n_host_out,
        out_shape=[jax.ShapeDtypeStruct((2, T, DFF), BF16), jax.ShapeDtypeStruct((2, 8, DFF), F32),
                   jax.ShapeDtypeStruct((2, 8, DFF), F32)] + hosting.out_shapes,
        scratch_shapes=hosting.scratch(), input_output_aliases=hosting.aliases(12, 3),
        compiler_params=_params(2))(df, df, up, up, up, up, up, up, convw, convw, convb3, convb3, *hosting.arrays)
    return res[0], res[1], res[2], list(res[3:])


def gate_bwd(dmo, z, yab, *, name, tm=1024):
    T = z.shape[0]

    def body(dmo_ref, zg_ref, y_ref, dz_ref, dy_ref):
        g = jax.nn.sigmoid(zg_ref[...].astype(F32))
        dmo_v = dmo_ref[...].astype(F32)
        dy_ref[...] = (dmo_v * g).astype(BF16)
        dz_ref[...] = (dmo_v * y_ref[...].astype(F32) * g * (1.0 - g)).astype(BF16)

    return pl.pallas_call(
        body, name=name, grid=(T // tm, 2),
        in_specs=[pl.BlockSpec((tm, D), lambda i, s: (i, 0)),
                  pl.BlockSpec((tm, D), lambda i, s: (i, 3 + s)),
                  pl.BlockSpec((tm, D), lambda i, s: (i, s))],
        out_specs=[pl.BlockSpec((tm, D), lambda i, s: (i, 3 + s)),
                   pl.BlockSpec((tm, D), lambda i, s: (i, s))],
        out_shape=[jax.ShapeDtypeStruct((T, 5 * D), BF16), jax.ShapeDtypeStruct((T, 2 * D), BF16)],
        compiler_params=_params(2))(dmo, z, yab)


def mixer_bwd(da, ds, z, dz, wpool, scale3, g3, wsm, wsmT, bT, l, *, name, tm=256):
    T = z.shape[0]
    hb = tm // POOL_HALO
    nt = T // tm

    def body(da_ref, dan_ref, ds_ref, zp_ref, zpp_ref, zu_ref, zv_ref, wp_ref, sc_ref, g_ref, ws_ref, wst_ref,
             b_ref, dzin_ref, dz_ref, dwp_ref, dsc_ref, dws_ref, dbt_ref, dgs_ref, mixed_scr, dvn_scr, db_scr):
        del dzin_ref
        i = pl.program_id(0)

        @pl.when(i == 0)
        def _():
            dwp_ref[...] = jnp.zeros_like(dwp_ref)
            dsc_ref[...] = jnp.zeros_like(dsc_ref)
            dws_ref[...] = jnp.zeros_like(dws_ref)
            dgs_ref[...] = jnp.zeros_like(dgs_ref)
            db_scr[...] = jnp.zeros_like(db_scr)

        zv_p = zp_ref[...].astype(F32)
        ext = jnp.concatenate([jnp.where(i == 0, 0.0, zpp_ref[...].astype(F32)), zv_p], axis=0)
        da_v = da_ref[...].astype(F32)
        da_ext = jnp.concatenate([da_v, jnp.where(i == nt - 1, 0.0, dan_ref[...].astype(F32))], axis=0)
        t = i * tm + lax.broadcasted_iota(jnp.int32, (tm, 1), 0)
        t_ext = i * tm + lax.broadcasted_iota(jnp.int32, (tm + POOL_HALO, 1), 0)
        for g, w in enumerate(POOL_WINDOWS):
            sl = slice(g * PG, (g + 1) * PG)
            pooled = _pooled(ext, zv_p, t, g, w).astype(BF16)
            q = jnp.dot(pooled, wp_ref[g], preferred_element_type=F32)
            dsc_ref[0:1, sl] += jnp.sum(da_v[:, sl] * q, axis=0, keepdims=True)
            dq_ext = (da_ext[:, sl] * sc_ref[:, sl]).astype(BF16)
            dwp_ref[g] += lax.dot_general(pooled, dq_ext[:tm, :], (((0,), (0,)), ((), ())),
                                          preferred_element_type=F32)
            dpool = lax.dot_general(dq_ext, wp_ref[g], (((1,), (1,)), ((), ())), preferred_element_type=F32)
            spread = _winsum_fwd(dpool / jnp.minimum(t_ext + 1, w).astype(F32), w)
            dz_ref[:, sl] = (spread[:tm, :] - dpool[:tm, :]).astype(BF16)

        zu, zv, ds_v = zu_ref[...].astype(F32), zv_ref[...].astype(F32), ds_ref[...].astype(F32)
        gain = g_ref[...]
        gu, gv = _gelu(zu), _gelu(zv)
        rv = lax.rsqrt(jnp.mean(gv * gv, axis=-1, keepdims=True) + EPS)
        vh = gv * rv
        vn = (vh * gain).astype(BF16)
        dmix = ds_v * gu
        dmix_b = dmix.astype(BF16)
        for n in range(tm // CHUNK):
            r = slice(n * CHUNK, (n + 1) * CHUNK)
            db_scr[...] += dmix[r, :]
            for h in range(HEADS):
                cs = slice(h * CHUNK, (h + 1) * CHUNK)
                mixed_scr[r, cs] = jnp.dot(ws_ref[h], vn[r, cs], preferred_element_type=F32) + b_ref[:, h:h + 1]
                dws_ref[h] += lax.dot_general(dmix_b[r, cs], vn[r, cs], (((1,), (1,)), ((), ())),
                                              preferred_element_type=F32)
                dvn_scr[r, cs] = jnp.dot(wst_ref[h], dmix_b[r, cs], preferred_element_type=F32)
        dz_ref[:, D:2 * D] = (ds_v * mixed_scr[...] * _gelu_grad(zu)).astype(BF16)
        dvn = dvn_scr[...]
        dgs_ref[0:1, :] += jnp.sum(dvn * vh, axis=0, keepdims=True)
        dvg = dvn * gain
        dgv = rv * (dvg - vh * jnp.mean(dvg * vh, axis=-1, keepdims=True))
        dz_ref[:, 2 * D:3 * D] = (dgv * _gelu_grad(zv)).astype(BF16)

        @pl.when(i == nt - 1)
        def _():
            tril = (lax.broadcasted_iota(jnp.int32, (CHUNK, CHUNK), 0)
                    >= lax.broadcasted_iota(jnp.int32, (CHUNK, CHUNK), 1)).astype(F32)
            for h in range(HEADS):
                dws_ref[h] = dws_ref[h] * tril
                dbt_ref[:, h:h + 1] = jnp.sum(db_scr[:, h * CHUNK:(h + 1) * CHUNK], axis=1, keepdims=True)

    const4 = lambda i: (l, 0, 0, 0)
    wl = l if wpool.shape[0] > 1 else 0
    in_specs = [
        _row_spec(tm, D),
        pl.BlockSpec((POOL_HALO, D), lambda i: (jnp.minimum((i + 1) * hb, T // POOL_HALO - 1), 0)),
        _row_spec(tm, D),
        _row_spec(tm, D, 0),
        pl.BlockSpec((POOL_HALO, D), lambda i: (jnp.maximum(i * hb - 1, 0), 0)),
        _row_spec(tm, D, 1), _row_spec(tm, D, 2),
        pl.BlockSpec((None, 4, PG, PG), lambda i: (wl, 0, 0, 0)),
        _gain_spec(l), _gain_spec(l),
        pl.BlockSpec((None, HEADS, CHUNK, CHUNK), const4),
        pl.BlockSpec((None, HEADS, CHUNK, CHUNK), const4),
        pl.BlockSpec((None, CHUNK, HEADS), lambda i: (l, 0, 0)),
        ANY,
    ]
    out_specs = [
        pl.BlockSpec((tm, 3 * D), lambda i: (i, 0)),
        pl.BlockSpec((4, PG, PG), lambda i: (0, 0, 0)),
        pl.BlockSpec((8, D), lambda i: (0, 0)),
        pl.BlockSpec((HEADS, CHUNK, CHUNK), lambda i: (0, 0, 0)),
        pl.BlockSpec((CHUNK, HEADS), lambda i: (0, 0)),
        pl.BlockSpec((8, D), lambda i: (0, 0)),
    ]
    out_shape = [
        jax.ShapeDtypeStruct((T, 5 * D), BF16), jax.ShapeDtypeStruct((4, PG, PG), F32),
        jax.ShapeDtypeStruct((8, D), F32), jax.ShapeDtypeStruct((HEADS, CHUNK, CHUNK), F32),
        jax.ShapeDtypeStruct((CHUNK, HEADS), F32), jax.ShapeDtypeStruct((8, D), F32),
    ]
    return pl.pallas_call(
        body, name=name, grid=(nt,), in_specs=in_specs, out_specs=out_specs, out_shape=out_shape,
        scratch_shapes=[pltpu.VMEM((tm, D), F32), pltpu.VMEM((tm, D), F32), pltpu.VMEM((CHUNK, D), F32)],
        input_output_aliases={13: 0}, compiler_params=_params(1))(
            da, da, ds, z, z, z, z, wpool, scale3, g3, wsm, wsmT, bT, dz)


def _row_tile(rows, cols, sub):
    cap = max(sub, (2 * 1024 * 1024) // (4 * cols))
    best = None
    for tr in range(sub, min(rows, cap) + 1, sub):
        if rows % tr == 0:
            best = tr
    return best or rows


def elementwise(fn, ins, out_dtypes, *, name, row_blk_offs=None, rows=None):
    cols = ins[0].shape[1]
    rows = rows or ins[0].shape[0]
    tr = _row_tile(rows, cols, 16)
    offs = row_blk_offs or [0] * len(ins)
    n_in = len(ins)

    def body(*refs):
        outs = fn(*[r[...] for r in refs[:n_in]])
        for o_ref, o in zip(refs[n_in:], outs):
            o_ref[...] = o.astype(o_ref.dtype)

    return pl.pallas_call(
        body, name=name, grid=(rows // tr,),
        in_specs=[pl.BlockSpec((tr, cols), functools.partial(lambda i, o: (i + o * (rows // tr), 0), o=o))
                  for o in offs],
        out_specs=[pl.BlockSpec((tr, cols), lambda i: (i, 0)) for _ in out_dtypes],
        out_shape=[jax.ShapeDtypeStruct((rows, cols), dt) for dt in out_dtypes],
        compiler_params=_params(1))(*ins)


def _adamw(w, g, m, v):
    m = ADAM_B1 * m + (1.0 - ADAM_B1) * g
    v = ADAM_B2 * v + (1.0 - ADAM_B2) * jnp.square(g)
    m_hat = m / (1.0 - ADAM_B1 ** ADAM_STEP)
    v_hat = v / (1.0 - ADAM_B2 ** ADAM_STEP)
    delta = -ADAM_LR * (m_hat / (jnp.sqrt(v_hat) + ADAM_EPS) + ADAM_WD * w)
    return delta, m, v


def _view2d(a):
    return a.reshape(-1, a.shape[-1])


def _place():
    x, y, c = lax.axis_index("x"), lax.axis_index("y"), lax.axis_index("c")
    others = [(1 - x, y), (x, 1 - y), (1 - x, 1 - y)]
    return x, y, c, 2 * x + y, others


def _remote(src, dst, send_sems, recv_sems, k, to):
    return pltpu.make_async_remote_copy(src_ref=src, dst_ref=dst, send_sem=send_sems.at[k], recv_sem=recv_sems.at[k],
                                        device_id=to, device_id_type=MESH)


def _half(ref, axis, j, size, h):
    if len(ref.shape) == 3:
        return ref.at[:, pl.ds(j * size + h * (size // 2), size // 2), :]
    if axis == 0:
        return ref.at[pl.ds(j * size + h * (size // 2), size // 2), :]
    rows = ref.shape[0] // 2
    return ref.at[pl.ds(h * rows, rows), pl.ds(j * size, size)]


def _half_shard_shape(shape, axis, size):
    if len(shape) == 3:
        return (shape[0], size // 2, shape[2])
    if axis == 0:
        return (size // 2, shape[1])
    return (shape[0] // 2, size)


class Exchange:
    def __init__(self, arrays, out_shapes, aliases, n_sems, begin, finish):
        self.arrays, self.out_shapes, self.aliases, self.n_sems = list(arrays), list(out_shapes), aliases, n_sems
        self.begin, self.finish = begin, finish


class _Hosting:
    def __init__(self, plan):
        self.plan = list(plan or [])
        self.arrays = [a for ex in self.plan for a in ex.arrays]
        self.out_shapes = [o for ex in self.plan for o in ex.out_shapes]
        self.n_sems = sum(ex.n_sems for ex in self.plan)

    def scratch(self):
        return [pltpu.SemaphoreType.DMA((self.n_sems,)), pltpu.SemaphoreType.DMA((self.n_sems,))] if self.plan else []

    def aliases(self, in_base, out_base):
        out, i0, o0 = {}, in_base, out_base
        for ex in self.plan:
            out.update({i0 + i: o0 + o for i, o in ex.aliases.items()})
            i0, o0 = i0 + len(ex.arrays), o0 + len(ex.out_shapes)
        return out

    def _each(self, in_refs, out_refs):
        i0 = o0 = s0 = 0
        for ex in self.plan:
            yield ex, in_refs[i0:i0 + len(ex.arrays)], out_refs[o0:o0 + len(ex.out_shapes)], s0
            i0, o0, s0 = i0 + len(ex.arrays), o0 + len(ex.out_shapes), s0 + ex.n_sems

    def begin(self, in_refs, out_refs, send_sems, recv_sems):
        for ex, ins, outs, s0 in self._each(in_refs, out_refs):
            ex.begin(ins, outs, send_sems, recv_sems, s0)

    def finish(self, in_refs, out_refs, send_sems, recv_sems):
        for ex, ins, outs, s0 in self._each(in_refs, out_refs):
            ex.finish(ins, outs, send_sems, recv_sems, s0)


def _first_last(grid):
    ids = [pl.program_id(a) for a in range(len(grid))]
    first = functools.reduce(jnp.logical_and, [i == 0 for i in ids])
    last = functools.reduce(jnp.logical_and, [i == g - 1 for i, g in zip(ids, grid)])
    return first, last


def run_exchanges(plan, *, name):
    host = _Hosting(plan)
    n_in, n_out = len(host.arrays), len(host.out_shapes)

    def body(*refs):
        ins, outs = refs[:n_in], refs[n_in:n_in + n_out]
        send_sems, recv_sems = refs[n_in + n_out:]
        host.begin(ins, outs, send_sems, recv_sems)
        host.finish(ins, outs, send_sems, recv_sems)

    return pl.pallas_call(
        body, name=name, in_specs=[ANY] * n_in, out_specs=[ANY] * n_out, out_shape=host.out_shapes,
        scratch_shapes=host.scratch(), input_output_aliases=host.aliases(0, 0),
        compiler_params=pltpu.CompilerParams(has_side_effects=True))(*host.arrays)


def place_shard(src, l, axis, size, out_dtype, place, *, name):
    shard = src.shape[1:]
    natural = tuple(size * N_CHIPS if a == axis else s for a, s in enumerate(shard))
    if len(shard) == 3:
        blk = (None,) + shard
        grid = (1,)
        in_map = lambda i, pr: (l, 0, 0, 0)
        out_map = lambda i, pr: (0, 0, pr[0], 0)
    else:
        tr = _row_tile(shard[0], shard[1], 16)
        steps = shard[0] // tr
        blk = (None, tr, shard[1])
        grid = (steps,)
        in_map = lambda i, pr: (l, i, 0)
        if axis == 0:
            out_map = lambda i, pr: (0, pr[0] * steps + i, 0)
        else:
            out_map = lambda i, pr: (0, i, pr[0])

    def body(pr_ref, s_ref, o_ref):
        del pr_ref
        o_ref[...] = s_ref[...].astype(o_ref.dtype)

    return pl.pallas_call(
        body, name=name,
        grid_spec=pltpu.PrefetchScalarGridSpec(
            num_scalar_prefetch=1, grid=grid, in_specs=[pl.BlockSpec(blk, in_map)],
            out_specs=pl.BlockSpec(blk, out_map)),
        out_shape=jax.ShapeDtypeStruct((1,) + natural, out_dtype), compiler_params=_params(1))(place, src)


def place_both_layers(src, axis, size, place, *, name):
    rows, cols = src.shape[1], src.shape[2]

    def body(pr_ref, s_ref, o_ref):
        del pr_ref
        o_ref[...] = s_ref[...]

    return pl.pallas_call(
        body, name=name,
        grid_spec=pltpu.PrefetchScalarGridSpec(
            num_scalar_prefetch=1, grid=(2,), in_specs=[pl.BlockSpec((None, rows, cols), lambda lyr, pr: (lyr, 0, 0))],
            out_specs=pl.BlockSpec((None, rows, cols), lambda lyr, pr: (lyr, 0, pr[0]))),
        out_shape=jax.ShapeDtypeStruct((2, rows, cols * N_CHIPS), src.dtype), compiler_params=_params(1))(place, src)


def gather_exchange(arrays, geom):
    n = len(arrays)

    def begin(ins, outs, send_sems, recv_sems, s0):
        x, y, c, j, others = _place()
        for t, (axis, size) in enumerate(geom):
            mine = _half(outs[t].at[0], axis, j, size, c)
            for k, (ox, oy) in enumerate(others):
                _remote(mine, mine, send_sems, recv_sems, s0 + 6 * t + k, (ox, oy, c)).start()

    def finish(ins, outs, send_sems, recv_sems, s0):
        x, y, c, j, others = _place()
        sib = (x, y, 1 - c)
        passed = []
        for t, (axis, size) in enumerate(geom):
            for k, (ox, oy) in enumerate(others):
                landed = _half(outs[t].at[0], axis, 2 * ox + oy, size, c)
                _remote(landed, landed, send_sems, recv_sems, s0 + 6 * t + k, (ox, oy, c)).wait_recv()
                fwd = _remote(landed, landed, send_sems, recv_sems, s0 + 6 * t + 3 + k, sib)
                fwd.start()
                passed.append(fwd)
        for t, (axis, size) in enumerate(geom):
            for k, (ox, oy) in enumerate(others):
                got = _half(outs[t].at[0], axis, 2 * ox + oy, size, 1 - c)
                _remote(got, got, send_sems, recv_sems, s0 + 6 * t + 3 + k, sib).wait_recv()
        for fwd in passed:
            fwd.wait_send()
        for t, (axis, size) in enumerate(geom):
            mine = _half(outs[t].at[0], axis, j, size, c)
            for k, (ox, oy) in enumerate(others):
                _remote(mine, mine, send_sems, recv_sems, s0 + 6 * t + k, (ox, oy, c)).wait_send()

    return Exchange(arrays, [jax.ShapeDtypeStruct(a.shape, a.dtype) for a in arrays], {t: t for t in range(n)},
                    6 * n, begin, finish)


def gather_by_layer_exchange(array, axis, size):
    def blocks(out, others, lyr):
        return [_block(out.at[lyr], axis, 2 * ox + oy, size) for (ox, oy) in others]

    def begin(ins, outs, send_sems, recv_sems, s0):
        x, y, c, j, others = _place()
        mine = _block(outs[0].at[c], axis, j, size)
        for k, (ox, oy) in enumerate(others):
            _remote(mine, mine, send_sems, recv_sems, s0 + k, (ox, oy, c)).start()

    def finish(ins, outs, send_sems, recv_sems, s0):
        x, y, c, j, others = _place()
        sib = (x, y, 1 - c)
        passed = []
        for k, ((ox, oy), landed) in enumerate(zip(others, blocks(outs[0], others, c))):
            _remote(landed, landed, send_sems, recv_sems, s0 + k, (ox, oy, c)).wait_recv()
            fwd = _remote(landed, landed, send_sems, recv_sems, s0 + 3 + k, sib)
            fwd.start()
            passed.append(fwd)
        for k, got in enumerate(blocks(outs[0], others, 1 - c)):
            _remote(got, got, send_sems, recv_sems, s0 + 3 + k, sib).wait_recv()
        for fwd in passed:
            fwd.wait_send()
        mine = _block(outs[0].at[c], axis, j, size)
        for k, (ox, oy) in enumerate(others):
            _remote(mine, mine, send_sems, recv_sems, s0 + k, (ox, oy, c)).wait_send()

    return Exchange([array], [jax.ShapeDtypeStruct(array.shape, array.dtype)], {0: 0}, 6, begin, finish)


def swap_exchange(grads, geom):
    def pieces(t, g, dst, h):
        axis, size = geom[t]
        if len(g.shape) == 2 and axis == 1:
            rows = g.shape[0] // 2
            return [(g.at[pl.ds(h * rows, rows), :], dst)]
        return [(_half(g, axis, jb, size, h), dst.at[jb]) for jb in range(N_CHIPS)]

    counts = [1 if (len(g.shape) == 3 and a == 1) else N_CHIPS for g, (a, _) in zip(grads, geom)]
    bases = [sum(counts[:t]) for t in range(len(grads))]

    def copies(ins, outs, send_sems, recv_sems, s0):
        x, y, c, _, _ = _place()
        cps = []
        for t in range(len(grads)):
            for q, (src, dst) in enumerate(pieces(t, ins[t].at[0], outs[t], 1 - c)):
                cps.append(_remote(src, dst, send_sems, recv_sems, s0 + bases[t] + q, (x, y, 1 - c)))
        return cps

    def begin(*a):
        for cp in copies(*a):
            cp.start()

    def finish(*a):
        for cp in copies(*a):
            cp.wait()

    out_shapes = []
    for g, (axis, size) in zip(grads, geom):
        shp = g.shape[1:]
        if len(shp) == 2 and axis == 1:
            out_shapes.append(jax.ShapeDtypeStruct((shp[0] // 2, shp[1]), g.dtype))
        else:
            out_shapes.append(jax.ShapeDtypeStruct((N_CHIPS,) + _half_shard_shape(shp, axis, size), g.dtype))
    return Exchange(grads, out_shapes, {}, sum(counts), begin, finish)


def scatter_exchange(parts, geom, shapes):
    def copies(ins, outs, send_sems, recv_sems, s0):
        x, y, c, j, others = _place()
        cps = []
        for t, ((axis, size), shp) in enumerate(zip(geom, shapes)):
            for k, (ox, oy) in enumerate(others):
                jp = 2 * ox + oy
                src = ins[t].at[:, pl.ds(jp * size, size)] if (len(shp) == 2 and axis == 1) else ins[t].at[jp]
                cps.append(_remote(src, outs[t].at[k], send_sems, recv_sems, s0 + 3 * t + k, (ox, oy, c)))
        return cps

    def begin(*a):
        for cp in copies(*a):
            cp.start()

    def finish(*a):
        for cp in copies(*a):
            cp.wait_recv()
        for cp in copies(*a):
            cp.wait_send()

    out_shapes = [jax.ShapeDtypeStruct((3,) + _half_shard_shape(shp, axis, size), p.dtype)
                  for p, (axis, size), shp in zip(parts, geom, shapes)]
    return Exchange(parts, out_shapes, {}, 3 * len(parts), begin, finish)


def share_exchange(grads, which):
    n = len(which)

    def my_half(refs, t, h):
        lyr = refs[which[t][0]].at[which[t][1]]
        if len(lyr.shape) == 3:
            rows = lyr.shape[1] // 2
            return lyr.at[:, pl.ds(h * rows, rows), :]
        rows = lyr.shape[0] // 2
        return lyr.at[pl.ds(h * rows, rows), :]

    def begin(ins, outs, send_sems, recv_sems, s0):
        x, y, c, _, _ = _place()
        for t in range(n):
            mine = my_half(outs, t, c)
            _remote(mine, mine, send_sems, recv_sems, s0 + t, (x, y, 1 - c)).start()

    def finish(ins, outs, send_sems, recv_sems, s0):
        x, y, c, _, _ = _place()
        for t in range(n):
            got = my_half(outs, t, 1 - c)
            _remote(got, got, send_sems, recv_sems, s0 + t, (x, y, 1 - c)).wait_recv()
        for t in range(n):
            mine = my_half(outs, t, c)
            _remote(mine, mine, send_sems, recv_sems, s0 + t, (x, y, 1 - c)).wait_send()

    return Exchange(grads, [jax.ShapeDtypeStruct(g.shape, g.dtype) for g in grads],
                    {t: t for t in range(len(grads))}, n, begin, finish)


def all_reduce_small(s):
    rows = s.shape[0]
    half = rows // 2
    assert half % 8 == 0

    def body(s_ref, o_ref, a_ref, b_ref, p_ref, send_sems, recv_sems):
        x, y, c, j, others = _place()
        sib = (x, y, 1 - c)
        swap = _remote(s_ref, a_ref, send_sems, recv_sems, 0, sib)
        swap.start()
        swap.wait()
        p_ref[...] = s_ref[...] + a_ref[...]
        mine = pl.ds(pl.multiple_of(c * half, 8), half)
        b_ref[j] = p_ref[mine, :]
        cps = [_remote(p_ref.at[mine, :], b_ref.at[j], send_sems, recv_sems, 1 + k, (ox, oy, c))
               for k, (ox, oy) in enumerate(others)]
        for cp in cps:
            cp.start()
        for k, (ox, oy) in enumerate(others):
            slot = b_ref.at[2 * ox + oy]
            _remote(slot, slot, send_sems, recv_sems, 1 + k, (ox, oy, c)).wait_recv()
        for cp in cps:
            cp.wait_send()
        o_ref[mine, :] = ((b_ref[0] + b_ref[1]) + b_ref[2]) + b_ref[3]
        back = _remote(o_ref.at[mine, :], o_ref.at[mine, :], send_sems, recv_sems, 4, sib)
        back.start()
        back.wait_send()
        theirs = pl.ds(pl.multiple_of((1 - c) * half, 8), half)
        _remote(o_ref.at[theirs, :], o_ref.at[theirs, :], send_sems, recv_sems, 4, sib).wait_recv()

    vmem = pl.BlockSpec(memory_space=pltpu.VMEM)
    return pl.pallas_call(
        body, name="all_reduce_small", in_specs=[vmem], out_specs=vmem,
        out_shape=jax.ShapeDtypeStruct((rows, LANES), F32),
        scratch_shapes=[pltpu.VMEM((rows, LANES), F32), pltpu.VMEM((N_CHIPS, half, LANES), F32),
                        pltpu.VMEM((rows, LANES), F32), pltpu.SemaphoreType.DMA((5,)),
                        pltpu.SemaphoreType.DMA((5,))],
        compiler_params=pltpu.CompilerParams(vmem_limit_bytes=VMEM_LIMIT, has_side_effects=True))(s)


def pair_sum(g, got, axis, size, place, *, name):
    shp = g.shape[1:]
    if len(shp) == 3:
        hs = size // 2
        grid = (N_CHIPS,)
        g_spec = pl.BlockSpec((None, shp[0], hs, shp[2]), lambda jb, pr: (0, 0, 2 * jb + pr[1], 0))
        r_spec = pl.BlockSpec((None, shp[0], hs, shp[2]), lambda jb, pr: (jb, 0, 0, 0))
    elif axis == 0:
        hs = size // 2
        tr = _row_tile(hs, shp[1], 16)
        steps = hs // tr
        grid = (N_CHIPS, steps)
        g_spec = pl.BlockSpec((None, tr, shp[1]), lambda jb, i, pr: (0, (2 * jb + pr[1]) * steps + i, 0))
        r_spec = pl.BlockSpec((None, tr, shp[1]), lambda jb, i, pr: (jb, i, 0))
    else:
        rows = shp[0] // 2
        tr = _row_tile(rows, shp[1], 16)
        steps = rows // tr
        grid = (steps,)
        g_spec = pl.BlockSpec((None, tr, shp[1]), lambda i, pr: (0, pr[1] * steps + i, 0))
        r_spec = pl.BlockSpec((tr, shp[1]), lambda i, pr: (i, 0))

    def body(pr_ref, g_ref, r_ref, o_ref):
        del pr_ref
        o_ref[...] = (g_ref[...].astype(F32) + r_ref[...].astype(F32)).astype(BF16)

    return pl.pallas_call(
        body, name=name,
        grid_spec=pltpu.PrefetchScalarGridSpec(num_scalar_prefetch=1, grid=grid, in_specs=[g_spec, r_spec],
                                               out_specs=r_spec),
        out_shape=jax.ShapeDtypeStruct(got.shape, BF16), compiler_params=_params(len(grid)))(place, g, got)


def chip_sum(part, slots, shp, axis, size, l, place, out, *, name):
    shard = _shard_shape(shp, axis, size)
    hshape = slots.shape[1:]
    if len(shp) == 3:
        grid = (1,)
        p_spec = pl.BlockSpec((None,) + hshape, lambda i, pr: (pr[0], 0, 0, 0))
        s_specs = [pl.BlockSpec((None,) + hshape, functools.partial(lambda i, pr, k: (k, 0, 0, 0), k=k))
                   for k in range(3)]
        o_spec = pl.BlockSpec((None,) + hshape, lambda i, pr: (l, 0, pr[1], 0))
    else:
        tr = _row_tile(hshape[0], hshape[1], 16)
        steps = hshape[0] // tr
        grid = (steps,)
        if axis == 0:
            p_spec = pl.BlockSpec((None, tr, hshape[1]), lambda i, pr: (pr[0], i, 0))
        else:
            p_spec = pl.BlockSpec((tr, hshape[1]), lambda i, pr: (i, pr[0]))
        s_specs = [pl.BlockSpec((None, tr, hshape[1]), functools.partial(lambda i, pr, k: (k, i, 0), k=k))
                   for k in range(3)]
        o_spec = pl.BlockSpec((None, tr, hshape[1]), lambda i, pr: (l, pr[1] * steps + i, 0))
    has_out = out is not None

    def body(pr_ref, p_ref, s0_ref, s1_ref, s2_ref, *rest):
        del pr_ref
        rest[-1][...] = ((p_ref[...].astype(F32) + s0_ref[...].astype(F32)) + s1_ref[...].astype(F32)) \
            + s2_ref[...].astype(F32)

    return pl.pallas_call(
        body, name=name,
        grid_spec=pltpu.PrefetchScalarGridSpec(
            num_scalar_prefetch=1, grid=grid, in_specs=[p_spec] + s_specs + ([ANY] if has_out else []),
            out_specs=o_spec),
        out_shape=jax.ShapeDtypeStruct((2,) + shard, F32), input_output_aliases={5: 0} if has_out else {},
        compiler_params=_params(1))(place, part, slots, slots, slots, *([out] if has_out else []))


GEOM = {name: (axis, size) for (name, _, axis, size) in BIG}
SHAPE = {name: shape for (name, shape, _, _) in BIG}
RIDES_IN_PROJ_L0 = ((0, ("w_pool", "w_branch_a", "w_branch_b", "w_out", "w_up")),)
RIDES_UP_PROJ_L0 = ((0, ("w_down", "w_ple_gate", "w_ple")), (1, ("w_in",)))
RIDES_DOWN_PROJ_L0 = ((1, ("w_pool", "w_branch_a", "w_branch_b", "w_out", "w_ple_gate", "w_ple")),)
RIDES_IN_PROJ_L1 = ((1, ("w_up", "w_down")),)
EARLY_GRADS_L0 = ("w_ple", "w_ple_gate", "w_down", "w_up")
LATE_GRADS_L0 = ("w_out", "w_branch_a", "w_branch_b", "w_pool", "w_in")


def _swap_of(G, names):
    return swap_exchange([G[k] for k in names], [GEOM[k] for k in names])


def _after_swap(G, names, got, place, tag):
    parts = [pair_sum(G[k], r, *GEOM[k], place, name=f"pair_sum_{k}_{tag}") for k, r in zip(names, got)]
    return scatter_exchange(parts, [GEOM[k] for k in names], [SHAPE[k] for k in names]), parts


def _reduce_start(G, names, place, tag):
    got = run_exchanges([_swap_of(G, names)], name=f"swap_halves_{tag}")
    return _after_swap(G, names, got, place, tag)


def _reduce_end(names, parts, slots, place, l, reduced):
    for k, q, s in zip(names, parts, slots):
        reduced[k] = chip_sum(q, s, SHAPE[k], *GEOM[k], l, place, reduced.get(k), name=f"chip_sum_{k}_l{l}")


def _local_step(x, p2, tgt, W0, W1, conv_w, small, place):
    T = x.shape[0]
    as3 = lambda a: a.reshape(2, 1, a.shape[-1])
    mix3, scale3, sgu3 = as3(small["mix_norm"]), as3(small["pool_scale"]), as3(small["sgu_norm"])
    ffn3, ple3, convb3 = as3(small["ffn_norm"]), as3(small["ple_norm"]), as3(small["conv_b"])
    tril = jnp.tril(jnp.ones((CHUNK, CHUNK), F32))
    ws_masked = small["w_spatial"] * tril
    wsm = ws_masked.astype(BF16)
    wsmT = jnp.swapaxes(ws_masked, -1, -2).astype(BF16)
    bT = jnp.swapaxes(small["b_spatial"], -1, -2)
    final3 = small["final_norm"].reshape(1, D)
    W = [dict(W0), dict(W1)]

    def riders(groups):
        return [gather_exchange([W[lyr][k] for k in names], [GEOM[k] for k in names]) for lyr, names in groups]

    def landed(groups, got):
        for lyr, names in groups:
            W[lyr].update(zip(names, got[:len(names)]))
            got = got[len(names):]

    saved = []
    hb = norm_fwd(x, mix3, 0, name="mix_norm_fwd_l0")
    for l in range(2):
        n = lambda s: f"{s}_l{l}"
        Wl = W[l]
        groups = RIDES_IN_PROJ_L0 if l == 0 else RIDES_IN_PROJ_L1
        z, got = mm_nn(hb, Wl["w_in"], 0, name=n("in_proj"), rows=T, tn=1280, out_dtype=BF16, host=riders(groups))
        landed(groups, got)
        a_in = pool_fwd(z, Wl["w_pool"], scale3, l, name=n("pool_fwd"))
        s_in = sgu_fwd(z, sgu3, wsm, bT, l, name=n("sgu_fwd"))
        yab = mm_nn(a_in, Wl["w_branch_a"], 0, name=n("branch_a"), rows=T, out_cols=2 * D, out_dtype=BF16)
        yab = mm_nn(s_in, Wl["w_branch_b"], 0, name=n("branch_b"), rows=T, out=yab, out_cols=2 * D, out_col_off=D,
                    out_dtype=BF16)
        mo = gate_fwd(z, yab, name=n("gate_fwd"))
        x1, h2b = mm_nn(mo, Wl["w_out"], 0, name=n("out_proj"), rows=T, resid=x, norm_gain=ffn3[l:l + 1])
        if l == 0:
            up, got = mm_nn(h2b, Wl["w_up"], 0, name=n("up_proj"), rows=T, tn=DFF, host=riders(RIDES_UP_PROJ_L0))
            landed(RIDES_UP_PROJ_L0, got)
        else:
            up = mm_nn(h2b, Wl["w_up"], 0, name=n("up_proj"), rows=T, tn=DFF)
        f = conv_fwd(up, conv_w, convb3, l, name=n("conv_fwd"))
        if l == 0:
            (x2, h3b), got = mm_nn(f, Wl["w_down"], 0, name=n("down_proj"), rows=T, resid=x1,
                                   norm_gain=ple3[l:l + 1], host=riders(RIDES_DOWN_PROJ_L0))
            landed(RIDES_DOWN_PROJ_L0, got)
        else:
            x2, h3b = mm_nn(f, Wl["w_down"], 0, name=n("down_proj"), rows=T, resid=x1, norm_gain=ple3[l:l + 1])
        pg = mm_nn(h3b, Wl["w_ple_gate"], 0, name=n("ple_gate_proj"), rows=T, out_dtype=BF16)
        e = mm_nn(p2, Wl["w_ple"], 0, name=n("ple_proj"), rows=T, a_row_off=l * T, out_dtype=BF16)
        saved.append(dict(x=x, hb=hb, z=z, a_in=a_in, s_in=s_in, yab=yab, mo=mo, x1=x1, h2b=h2b, up=up, f=f,
                          x2=x2, h3b=h3b, pg=pg, e=e))
        if l == 0:
            x, hb = ple_fwd(x2, pg, e, mix3, 1, name=n("ple_fwd"))
        else:
            x = ple_fwd(x2, pg, e, None, 0, name=n("ple_fwd"))

    loss_acc, dx, dg_final = loss_head(x, final3, tgt, name="loss_head")

    small_grads = [None, None]
    all_names = [t[0] for t in BIG]
    reduced = {}
    swap1 = G1 = scatter1 = parts1 = slots1 = None
    for l in (1, 0):
        n = lambda s: f"{s}_l{l}"
        a, Wl, G = saved[l], W[l], {}
        de, dpg = ple_bwd(dx, a["pg"], a["e"], name=n("ple_bwd"))
        G["w_ple"] = mm_tn(p2, de, name=n("d_w_ple"), rows=T, ka=PG, nb=D, a_row_off=l * T)
        G["w_ple_gate"] = mm_tn(a["h3b"], dpg, name=n("d_w_ple_gate"), rows=T, ka=D, nb=D)
        if l == 0:
            (dx2, dg_ple), got = mm_nt(dpg, Wl["w_ple_gate"], 0, name=n("ple_norm_bwd"), rows=T,
                                       norm_bwd_of=(a["x2"], ple3, l, dx), host=[swap1])
            scatter1, parts1 = _after_swap(G1, all_names, got, place, "l1")
        else:
            dx2, dg_ple = mm_nt(dpg, Wl["w_ple_gate"], 0, name=n("ple_norm_bwd"), rows=T,
                                norm_bwd_of=(a["x2"], ple3, l, dx))
        df = mm_nt(dx2, Wl["w_down"], 0, name=n("d_ffn_act"), rows=T, out_dtype=F32)
        G["w_down"] = mm_tn(a["f"], dx2, name=n("d_w_down"), rows=T, ka=DFF, nb=D, tm=1408)
        if l == 0:
            dup, dcw, dcb, slots1 = conv_bwd(df, a["up"], conv_w, convb3, l, name=n("conv_bwd"), host=[scatter1])
        else:
            dup, dcw, dcb, _ = conv_bwd(df, a["up"], conv_w, convb3, l, name=n("conv_bwd"))
        G["w_up"] = mm_tn(a["h2b"], dup, name=n("d_w_up"), rows=T, ka=D, nb=2 * DFF, tn=DFF, tk=1024)
        if l == 0:
            scatter_early, parts_early = _reduce_start(G, EARLY_GRADS_L0, place, "l0_early")
            (dx1, dg_ffn), slots_early = mm_nt(dup, Wl["w_up"], 0, name=n("ffn_norm_bwd"), rows=T, tk=1408,
                                               norm_bwd_of=(a["x1"], ffn3, l, dx2), host=[scatter_early])
        else:
            dx1, dg_ffn = mm_nt(dup, Wl["w_up"], 0, name=n("ffn_norm_bwd"), rows=T, tk=1408,
                                norm_bwd_of=(a["x1"], ffn3, l, dx2))
        dmo = mm_nt(dx1, Wl["w_out"], 0, name=n("d_gated"), rows=T)
        G["w_out"] = mm_tn(a["mo"], dx1, name=n("d_w_out"), rows=T, ka=D, nb=D)
        dz, dyab = gate_bwd(dmo, a["z"], a["yab"], name=n("gate_bwd"))
        G["w_branch_a"] = mm_tn(a["a_in"], dyab, name=n("d_w_branch_a"), rows=T, ka=D, nb=D)
        G["w_branch_b"] = mm_tn(a["s_in"], dyab, name=n("d_w_branch_b"), rows=T, ka=D, nb=D, b_col_off=D)
        da = mm_nt(dyab, Wl["w_branch_a"], 0, name=n("d_pool_out"), rows=T, kdim=D)
        ds = mm_nt(dyab, Wl["w_branch_b"], 0, name=n("d_sgu_out"), rows=T, kdim=D, a_col_off=D)
        dz, dwp, dsc, dws, dbt, dgs = mixer_bwd(da, ds, a["z"], dz, Wl["w_pool"], scale3, sgu3, wsm, wsmT, bT, l,
                                                name=n("mixer_bwd"))
        G["w_pool"] = dwp.astype(BF16)[None]
        G["w_in"] = mm_tn(a["hb"], dz, name=n("d_w_in"), rows=T, ka=D, nb=5 * D, tn=1280)
        if l == 0:
            scatter_late, parts_late = _reduce_start(G, LATE_GRADS_L0, place, "l0_late")
            _reduce_end(all_names, parts1, slots1, place, 1, reduced)
            _reduce_end(EARLY_GRADS_L0, parts_early, slots_early, place, 0, reduced)
            done = [(t, 1) for t in range(len(all_names))] + [(all_names.index(k), 0) for k in EARLY_GRADS_L0]
            (dx, dg_mix), got = mm_nt(
                dz, Wl["w_in"], 0, name=n("mix_norm_bwd"), rows=T, tk=1280, norm_bwd_of=(a["x"], mix3, l, dx1),
                host=[scatter_late, share_exchange([reduced[k] for k in all_names], done)])
            slots_late = got[:len(LATE_GRADS_L0)]
            reduced.update(zip(all_names, got[len(LATE_GRADS_L0):]))
        else:
            dx, dg_mix = mm_nt(dz, Wl["w_in"], 0, name=n("mix_norm_bwd"), rows=T, tk=1280,
                               norm_bwd_of=(a["x"], mix3, l, dx1))
            swap1, G1 = _swap_of(G, all_names), G
        small_grads[l] = dict(
            mix_norm=dg_mix[0], pool_scale=dsc[0], sgu_norm=dgs[0], w_spatial=dws, b_spatial=dbt.T,
            ffn_norm=dg_ffn[0], conv_b=jnp.concatenate([dcb[0, 0], dcb[1, 0]]), ple_norm=dg_ple[0],
            conv_w=jnp.concatenate([dcw[0, :3], dcw[1, :3]], axis=1))
    _reduce_end(LATE_GRADS_L0, parts_late, slots_late, place, 0, reduced)
    return loss_acc, dx, reduced, small_grads, dg_final[0]


SMALL_ORDER = ("mix_norm", "pool_scale", "sgu_norm", "w_spatial", "b_spatial", "ffn_norm", "conv_b", "ple_norm",
               "conv_w")


def _pack_rows(pieces, row_multiple):
    flat = jnp.concatenate([a.reshape(-1) for a in pieces])
    rows = -(-flat.shape[0] // LANES)
    rows = -(-rows // row_multiple) * row_multiple
    return jnp.pad(flat, (0, rows * LANES - flat.shape[0])).reshape(rows, LANES)


def _unpack(flat, shapes):
    out, off = [], 0
    for shp in shapes:
        size = 1
        for s in shp:
            size *= s
        out.append(flat[off:off + size].reshape(shp))
        off += size
    return out


def kernel(x, p, mix_norm, w_in, w_pool, pool_scale, sgu_norm, w_spatial, b_spatial, w_branch_a, w_branch_b, w_out, ffn_norm, w_up, conv_w, conv_b, w_down, ple_norm, w_ple_gate, w_ple, final_norm, loss_target, m_mix_norm, m_w_in, m_w_pool, m_pool_scale, m_sgu_norm, m_w_spatial, m_b_spatial, m_w_branch_a, m_w_branch_b, m_w_out, m_ffn_norm, m_w_up, m_conv_w, m_conv_b, m_w_down, m_ple_norm, m_w_ple_gate, m_w_ple, m_final_norm, v_mix_norm, v_w_in, v_w_pool, v_pool_scale, v_sgu_norm, v_w_spatial, v_b_spatial, v_w_branch_a, v_w_branch_b, v_w_out, v_ffn_norm, v_w_up, v_conv_w, v_conv_b, v_w_down, v_ple_norm, v_w_ple_gate, v_w_ple, v_final_norm):
    names = ["mix_norm", "w_in", "w_pool", "pool_scale", "sgu_norm", "w_spatial", "b_spatial", "w_branch_a",
             "w_branch_b", "w_out", "ffn_norm", "w_up", "conv_w", "conv_b", "w_down", "ple_norm", "w_ple_gate",
             "w_ple", "final_norm"]
    w = dict(zip(names, [mix_norm, w_in, w_pool, pool_scale, sgu_norm, w_spatial, b_spatial, w_branch_a, w_branch_b,
                         w_out, ffn_norm, w_up, conv_w, conv_b, w_down, ple_norm, w_ple_gate, w_ple, final_norm]))
    m = dict(zip(names, [m_mix_norm, m_w_in, m_w_pool, m_pool_scale, m_sgu_norm, m_w_spatial, m_b_spatial,
                         m_w_branch_a, m_w_branch_b, m_w_out, m_ffn_norm, m_w_up, m_conv_w, m_conv_b, m_w_down,
                         m_ple_norm, m_w_ple_gate, m_w_ple, m_final_norm]))
    v = dict(zip(names, [v_mix_norm, v_w_in, v_w_pool, v_pool_scale, v_sgu_norm, v_w_spatial, v_b_spatial,
                         v_w_branch_a, v_w_branch_b, v_w_out, v_ffn_norm, v_w_up, v_conv_w, v_conv_b, v_w_down,
                         v_ple_norm, v_w_ple_gate, v_w_ple, v_final_norm]))
    T = x.shape[1]
    chip = 2 * lax.axis_index("x") + lax.axis_index("y")
    place = jnp.stack([chip, lax.axis_index("c")]).astype(jnp.int32)

    big_names = [t[0] for t in BIG]
    placed = [{k: place_shard(w[k], l, *GEOM[k], BF16, place, name=f"place_{k}_l{l}") for k in big_names}
              for l in range(2)]
    conv_w8 = jnp.pad(conv_w, ((0, 0), (0, CONV_ROWS - conv_w.shape[1]), (0, 0)))
    conv_placed = place_both_layers(conv_w8, 1, conv_w.shape[2], place, name="place_conv_w")
    w_in0, conv_w_all = run_exchanges([gather_exchange([placed[0]["w_in"]], [GEOM["w_in"]]),
                                       gather_by_layer_exchange(conv_placed, 1, conv_w.shape[2])],
                                      name="gather_first_weights")
    placed[0]["w_in"] = w_in0

    small = {k: w[k] for k in ("mix_norm", "pool_scale", "sgu_norm", "w_spatial", "b_spatial", "ffn_norm",
                               "conv_b", "ple_norm", "final_norm")}
    loss_acc, dx, reduced, small_grads, dg_final = _local_step(
        x.reshape(T, D), p.reshape(2 * T, p.shape[-1]), loss_target.reshape(T, D), placed[0], placed[1], conv_w_all,
        small, place)
    full = run_exchanges([share_exchange([reduced[k] for k in big_names],
                                         [(big_names.index(k), 0) for k in LATE_GRADS_L0])], name="share_last_halves")
    grads = dict(zip(big_names, full))

    pieces = [small_grads[l][k] for l in range(2) for k in SMALL_ORDER] + [dg_final, loss_acc[0, 0].reshape(1)]
    shapes = [a.shape for a in pieces]
    total = all_reduce_small(_pack_rows(pieces, 16)).reshape(-1)
    summed = _unpack(total, shapes)
    per_layer = {k: jnp.stack([summed[i], summed[len(SMALL_ORDER) + i]]) for i, k in enumerate(SMALL_ORDER)}
    for k in ("mix_norm", "pool_scale", "sgu_norm", "w_spatial", "b_spatial", "ffn_norm", "conv_b", "ple_norm"):
        grads[k] = per_layer[k]
    grads["final_norm"] = summed[-2]
    loss = summed[-1].reshape(())
    cw = conv_w.shape[2]
    grads["conv_w"] = lax.dynamic_slice_in_dim(per_layer["conv_w"], chip * cw, cw, axis=2)

    delta, new_m, new_v = {}, {}, {}
    for name in big_names:
        shp = w[name].shape
        d_, m_, v_, g_ = elementwise(lambda w_, g_, m_, v_: (*_adamw(w_, g_, m_, v_), g_),
                                     [_view2d(a) for a in (w[name], grads[name], m[name], v[name])],
                                     [F32, F32, F32, F32], name=f"adamw_{name}")
        delta[name], new_m[name], new_v[name] = d_.reshape(shp), m_.reshape(shp), v_.reshape(shp)
        grads[name] = g_.reshape(shp)
    small_names = [k for k in names if k not in big_names]
    small_shapes = [w[k].shape for k in small_names]
    packed = [_pack_rows([src[k] for k in small_names], 8) for src in (w, grads, m, v)]
    outs = elementwise(_adamw, packed, [F32, F32, F32], name="adamw_small")
    for dst, o in zip((delta, new_m, new_v), outs):
        for k, a in zip(small_names, _unpack(o.reshape(-1), small_shapes)):
            dst[k] = a

    return (loss, dx.reshape(1, T, D), *[grads[k] for k in names], *[delta[k] for k in names],
            *[new_m[k] for k in names], *[new_v[k] for k in names])
```

```python
import functools

import jax
import jax.numpy as jnp
from jax import lax
from jax.experimental import pallas as pl
from jax.experimental.pallas import tpu as pltpu

F32 = jnp.float32
BF16 = jnp.bfloat16
EPS = 1e-6
D = 1024
POOL_WINDOWS = (2, 4, 8, 16)
PG = 256
POOL_HALO = 16
CHUNK = 128
HEADS = 8
DFF = 2816
CONV_HALO = 8
CONV_TC = 1408
N_CHIPS = 4
LANES = 128
VMEM_LIMIT = 56 * 1024 * 1024
MESH = pl.DeviceIdType.MESH
ANY = pl.BlockSpec(memory_space=pl.ANY)

ADAM_LR = 0.001
ADAM_B1 = 0.9
ADAM_B2 = 0.999
ADAM_EPS = 1e-08
ADAM_WD = 0.01
ADAM_STEP = 10

BIG = (
    ("w_in", (D, 5 * D), 1, 5 * D // N_CHIPS),
    ("w_pool", (4, PG, PG), 1, PG // N_CHIPS),
    ("w_branch_a", (D, D), 0, D // N_CHIPS),
    ("w_branch_b", (D, D), 0, D // N_CHIPS),
    ("w_out", (D, D), 0, D // N_CHIPS),
    ("w_up", (D, 2 * DFF), 1, 2 * DFF // N_CHIPS),
    ("w_down", (DFF, D), 0, DFF // N_CHIPS),
    ("w_ple_gate", (D, D), 0, D // N_CHIPS),
    ("w_ple", (PG, D), 1, D // N_CHIPS),
)
CONV_ROWS = 8


def _params(n_axes):
    return pltpu.CompilerParams(dimension_semantics=("arbitrary",) * n_axes, vmem_limit_bytes=VMEM_LIMIT)


def _gelu(x):
    return 0.5 * x * (1.0 + lax.erf(x * 0.7071067811865476))


def _gelu_grad(x):
    return 0.5 * (1.0 + lax.erf(x * 0.7071067811865476)) + x * jnp.exp(-0.5 * x * x) * 0.3989422804014327


def _shard_shape(shape, axis, size):
    return tuple(size if a == axis else s for a, s in enumerate(shape))


def _block(ref, axis, j, size):
    idx = tuple(pl.ds(j * size, size) if a == axis else slice(None) for a in range(len(ref.shape)))
    return ref.at[idx]


def mm_nn(a, w, l, *, name, rows, out_dtype=F32, resid=None, a_row_off=0, out=None, out_cols=None,
          out_col_off=0, norm_gain=None, host=None, tm=1024, tn=None, tk=None):
    K, N = w.shape[1], w.shape[2]
    tn = tn or N
    tk = tk or K
    nk = K // tk
    out_cols = out_cols or N
    assert rows % tm == 0 and N % tn == 0 and K % tk == 0 and out_col_off % tn == 0 and a_row_off % tm == 0
    has_resid, has_out, has_norm = resid is not None, out is not None, norm_gain is not None
    assert not has_norm or (tn == N and not has_out)
    grid = (N // tn, rows // tm, nk)
    hosting = _Hosting(host)
    n_in = 2 + has_resid + has_norm + has_out
    n_host_in, n_host_out = len(hosting.arrays), len(hosting.out_shapes)
    n_own_out = 1 + has_norm

    def body(*refs):
        refs = list(refs)
        a_ref, w_ref = refs[0], refs[1]
        r_ref = refs[2] if has_resid else None
        g_ref = refs[2 + has_resid] if has_norm else None
        host_in = refs[n_in:n_in + n_host_in]
        o_base = n_in + n_host_in
        o_ref = refs[o_base]
        host_out = refs[o_base + n_own_out:o_base + n_own_out + n_host_out]
        scratch = refs[o_base + n_own_out + n_host_out:]
        if hosting.plan:
            first, last = _first_last(grid)
            sems = scratch[-2:]

            @pl.when(first)
            def _():
                hosting.begin(host_in, host_out, *sems)

        part = jnp.dot(a_ref[...].astype(BF16), w_ref[...], preferred_element_type=F32)

        def finish(r):
            if has_resid:
                r = r + r_ref[...]
            o_ref[...] = r.astype(o_ref.dtype)
            if has_norm:
                scale = lax.rsqrt(jnp.mean(r * r, axis=-1, keepdims=True) + EPS)
                refs[o_base + 1][...] = (r * scale * g_ref[...]).astype(BF16)

        if nk == 1:
            finish(part)
        else:
            acc = scratch[0]
            k = pl.program_id(2)

            @pl.when(k == 0)
            def _():
                acc[...] = part

            @pl.when(k > 0)
            def _():
                acc[...] += part

            @pl.when(k == nk - 1)
            def _():
                finish(acc[...])

        if hosting.plan:
            @pl.when(last)
            def _():
                hosting.finish(host_in, host_out, *sems)

    in_specs = [pl.BlockSpec((tm, tk), lambda j, i, k: (i + a_row_off // tm, k)),
                pl.BlockSpec((None, tk, tn), lambda j, i, k: (l, k, j))]
    args = [a, w]
    if has_resid:
        in_specs.append(pl.BlockSpec((tm, tn), lambda j, i, k: (i, j)))
        args.append(resid)
    if has_norm:
        in_specs.append(pl.BlockSpec((None, 1, tn), lambda j, i, k: (l, 0, 0)))
        args.append(norm_gain)
    aliases = {}
    if has_out:
        in_specs.append(ANY)
        aliases = {len(args): 0}
        args.append(out)
    aliases.update(hosting.aliases(n_in, n_own_out))
    out_specs = [pl.BlockSpec((tm, tn), lambda j, i, k: (i, j + out_col_off // tn))]
    out_shape = [jax.ShapeDtypeStruct((rows, out_cols), out_dtype)]
    if has_norm:
        out_specs.append(pl.BlockSpec((tm, tn), lambda j, i, k: (i, j)))
        out_shape.append(jax.ShapeDtypeStruct((rows, N), BF16))
    res = pl.pallas_call(
        body, name=name, grid=grid,
        in_specs=in_specs + [ANY] * n_host_in,
        out_specs=out_specs + [ANY] * n_host_out,
        out_shape=out_shape + hosting.out_shapes,
        scratch_shapes=([pltpu.VMEM((tm, tn), F32)] if nk > 1 else []) + hosting.scratch(),
        input_output_aliases=aliases, compiler_params=_params(3))(*args, *hosting.arrays)
    own = res[0] if n_own_out == 1 else tuple(res[:n_own_out])
    return (own, list(res[n_own_out:])) if hosting.plan else own


def mm_nt(a, w, l, *, name, rows, kdim=None, a_col_off=0, out_dtype=BF16, norm_bwd_of=None, host=None, tm=1024,
          tn=None, tk=None):
    R = w.shape[1]
    kdim = kdim or w.shape[2]
    tn = tn or R
    tk = tk or kdim
    nk = kdim // tk
    assert rows % tm == 0 and R % tn == 0 and kdim % tk == 0 and a_col_off % tk == 0
    fused = norm_bwd_of is not None
    assert not fused or tn == R
    grid = (R // tn, rows // tm, nk)
    hosting = _Hosting(host)
    n_host_in, n_host_out = len(hosting.arrays), len(hosting.out_shapes)
    n_own_in, n_own_out = (3, 2) if fused else (0, 1)

    def body(a_ref, w_ref, *refs):
        host_in = refs[n_own_in:n_own_in + n_host_in]
        host_out = refs[n_own_in + n_host_in + n_own_out:n_own_in + n_host_in + n_own_out + n_host_out]
        scratch = refs[n_own_in + n_host_in + n_own_out + n_host_out:]
        rest = list(refs[:n_own_in]) + list(refs[n_own_in + n_host_in:n_own_in + n_host_in + n_own_out]) \
            + ([scratch[0]] if nk > 1 else [])
        if hosting.plan:
            first, last = _first_last(grid)
            sems = scratch[-2:]

            @pl.when(first)
            def _():
                hosting.begin(host_in, host_out, *sems)

        part = lax.dot_general(a_ref[...].astype(BF16), w_ref[...], (((1,), (1,)), ((), ())),
                               preferred_element_type=F32)
        i, k = pl.program_id(1), pl.program_id(2)

        def finish(dh):
            if not fused:
                rest[0][...] = dh.astype(rest[0].dtype)
                return
            x_ref, g_ref, dxi_ref, dx_ref, dg_ref = rest[:5]

            @pl.when(i == 0)
            def _():
                dg_ref[...] = jnp.zeros_like(dg_ref)

            xv = x_ref[...]
            r = lax.rsqrt(jnp.mean(xv * xv, axis=-1, keepdims=True) + EPS)
            xh = xv * r
            dhg = dh * g_ref[...]
            dx_ref[...] = dxi_ref[...] + r * (dhg - xh * jnp.mean(dhg * xh, axis=-1, keepdims=True))
            dg_ref[0:1, :] += jnp.sum(dh * xh, axis=0, keepdims=True)

        if nk == 1:
            finish(part)
        else:
            acc = rest[-1]

            @pl.when(k == 0)
            def _():
                acc[...] = part

            @pl.when(k > 0)
            def _():
                acc[...] += part

            @pl.when(k == nk - 1)
            def _():
                finish(acc[...])

        if hosting.plan:
            @pl.when(last)
            def _():
                hosting.finish(host_in, host_out, *sems)

    if a.ndim == 3:
        per = a.shape[2] // tk
        a_spec = pl.BlockSpec((None, tm, tk), lambda j, i, k: (k // per, i, k % per))
    else:
        a_spec = pl.BlockSpec((tm, tk), lambda j, i, k: (i, k + a_col_off // tk))
    in_specs = [a_spec, pl.BlockSpec((None, tn, tk), lambda j, i, k: (l, j, k))]
    args = [a, w]
    row_tile = pl.BlockSpec((tm, tn), lambda j, i, k: (i, j))
    if fused:
        x, gain, gl, dx_in = norm_bwd_of
        in_specs += [row_tile, pl.BlockSpec((None, 1, tn), lambda j, i, k: (gl, 0, 0)), row_tile]
        args += [x, gain, dx_in]
        out_specs = [row_tile, pl.BlockSpec((8, tn), lambda j, i, k: (0, 0))]
        out_shape = [jax.ShapeDtypeStruct((rows, R), F32), jax.ShapeDtypeStruct((8, R), F32)]
    else:
        out_specs, out_shape = [row_tile], [jax.ShapeDtypeStruct((rows, R), out_dtype)]
    res = pl.pallas_call(
        body, name=name, grid=grid, in_specs=in_specs + [ANY] * n_host_in,
        out_specs=out_specs + [ANY] * n_host_out, out_shape=out_shape + hosting.out_shapes,
        scratch_shapes=([pltpu.VMEM((tm, tn), F32)] if nk > 1 else []) + hosting.scratch(),
        input_output_aliases=hosting.aliases(2 + n_own_in, n_own_out), compiler_params=_params(3))(
            *args, *hosting.arrays)
    own = tuple(res[:n_own_out]) if fused else res[0]
    return (own, list(res[n_own_out:])) if hosting.plan else own


def mm_tn(a, b, *, name, rows, ka, nb, a_row_off=0, b_col_off=0, tm=None, tn=None, tk=2048):
    tm = tm or ka
    tn = tn or nb
    tk = min(tk, rows)
    nk = rows // tk
    assert ka % tm == 0 and nb % tn == 0 and rows % tk == 0 and b_col_off % tn == 0 and a_row_off % tk == 0

    def body(a_ref, b_ref, o_ref, acc):
        part = lax.dot_general(a_ref[...].astype(BF16), b_ref[...].astype(BF16), (((0,), (0,)), ((), ())),
                               preferred_element_type=F32)
        k = pl.program_id(2)

        @pl.when(k == 0)
        def _():
            acc[...] = part

        @pl.when(k > 0)
        def _():
            acc[...] += part

        @pl.when(k == nk - 1)
        def _():
            o_ref[...] = acc[...].astype(o_ref.dtype)

    if b.ndim == 3:
        per = b.shape[2] // tn
        b_spec = pl.BlockSpec((None, tk, tn), lambda j, i, k: (j // per, k, j % per))
    else:
        b_spec = pl.BlockSpec((tk, tn), lambda j, i, k: (k, j + b_col_off // tn))
    return pl.pallas_call(
        body, name=name, grid=(nb // tn, ka // tm, nk),
        in_specs=[pl.BlockSpec((tk, tm), lambda j, i, k: (k + a_row_off // tk, i)), b_spec],
        out_specs=pl.BlockSpec((None, tm, tn), lambda j, i, k: (0, i, j)),
        out_shape=jax.ShapeDtypeStruct((1, ka, nb), BF16),
        scratch_shapes=[pltpu.VMEM((tm, tn), F32)], compiler_params=_params(3))(a, b)


def _row_spec(tm, width, col=0):
    return pl.BlockSpec((tm, width), lambda i: (i, col))


def _gain_spec(l, width=D):
    return pl.BlockSpec((None, 1, width), lambda i: (l, 0, 0))


def norm_fwd(x, g3, l, *, name, tm=1024):
    T = x.shape[0]

    def body(x_ref, g_ref, o_ref):
        xv = x_ref[...]
        r = lax.rsqrt(jnp.mean(xv * xv, axis=-1, keepdims=True) + EPS)
        o_ref[...] = (xv * r * g_ref[...]).astype(BF16)

    return pl.pallas_call(
        body, name=name, grid=(T // tm,),
        in_specs=[_row_spec(tm, D), _gain_spec(l)], out_specs=_row_spec(tm, D),
        out_shape=jax.ShapeDtypeStruct((T, D), BF16), compiler_params=_params(1))(x, g3)


def _winsum_back(ext, w):
    s, span = ext, 1
    while span < w:
        s = s + pltpu.roll(s, span, 0)
        span *= 2
    return s


def _winsum_fwd(ext, w):
    rows = ext.shape[0]
    s, span = ext, 1
    while span < w:
        s = s + pltpu.roll(s, rows - span, 0)
        span *= 2
    return s


def _pooled(ext, z, t, g, w):
    sl = slice(g * PG, (g + 1) * PG)
    s = _winsum_back(ext[:, sl], w)[POOL_HALO:, :]
    return s / jnp.minimum(t + 1, w).astype(F32) - z[:, sl]


def pool_fwd(z, wpool, scale3, l, *, name, tm=512):
    T = z.shape[0]
    hb = tm // POOL_HALO
    wl = l if wpool.shape[0] > 1 else 0

    def body(z_ref, zp_ref, wp_ref, sc_ref, o_ref):
        i = pl.program_id(0)
        zv = z_ref[...].astype(F32)
        prev = jnp.where(i == 0, 0.0, zp_ref[...].astype(F32))
        ext = jnp.concatenate([prev, zv], axis=0)
        t = i * tm + lax.broadcasted_iota(jnp.int32, (tm, 1), 0)
        for g, w in enumerate(POOL_WINDOWS):
            sl = slice(g * PG, (g + 1) * PG)
            pooled = _pooled(ext, zv, t, g, w)
            q = jnp.dot(pooled.astype(BF16), wp_ref[g], preferred_element_type=F32)
            o_ref[:, sl] = (q * sc_ref[:, sl]).astype(BF16)

    return pl.pallas_call(
        body, name=name, grid=(T // tm,),
        in_specs=[_row_spec(tm, D),
                  pl.BlockSpec((POOL_HALO, D), lambda i: (jnp.maximum(i * hb - 1, 0), 0)),
                  pl.BlockSpec((None, 4, PG, PG), lambda i: (wl, 0, 0, 0)),
                  _gain_spec(l)],
        out_specs=_row_spec(tm, D),
        out_shape=jax.ShapeDtypeStruct((T, D), BF16), compiler_params=_params(1))(z, z, wpool, scale3)


def sgu_fwd(z, g3, wsm, bT, l, *, name, tm=512):
    T = z.shape[0]

    def body(zu_ref, zv_ref, g_ref, ws_ref, b_ref, o_ref):
        gu = _gelu(zu_ref[...].astype(F32))
        gv = _gelu(zv_ref[...].astype(F32))
        rv = lax.rsqrt(jnp.mean(gv * gv, axis=-1, keepdims=True) + EPS)
        vn = (gv * rv * g_ref[...]).astype(BF16)
        for n in range(tm // CHUNK):
            r = slice(n * CHUNK, (n + 1) * CHUNK)
            for h in range(HEADS):
                cs = slice(h * CHUNK, (h + 1) * CHUNK)
                mixed = jnp.dot(ws_ref[h], vn[r, cs], preferred_element_type=F32) + b_ref[:, h:h + 1]
                o_ref[r, cs] = (gu[r, cs] * mixed).astype(BF16)

    return pl.pallas_call(
        body, name=name, grid=(T // tm,),
        in_specs=[_row_spec(tm, D, 1), _row_spec(tm, D, 2), _gain_spec(l),
                  pl.BlockSpec((None, HEADS, CHUNK, CHUNK), lambda i: (l, 0, 0, 0)),
                  pl.BlockSpec((None, CHUNK, HEADS), lambda i: (l, 0, 0))],
        out_specs=_row_spec(tm, D),
        out_shape=jax.ShapeDtypeStruct((T, D), BF16), compiler_params=_params(1))(z, z, g3, wsm, bT)


def gate_fwd(z, yab, *, name, tm=1024):
    T = z.shape[0]

    def body(za_ref, zb_ref, y_ref, o_ref):
        ga = jax.nn.sigmoid(za_ref[...].astype(F32))
        gb = jax.nn.sigmoid(zb_ref[...].astype(F32))
        o_ref[...] = (ga * y_ref[:, :D].astype(F32) + gb * y_ref[:, D:].astype(F32)).astype(BF16)

    return pl.pallas_call(
        body, name=name, grid=(T // tm,),
        in_specs=[_row_spec(tm, D, 3), _row_spec(tm, D, 4), _row_spec(tm, 2 * D)],
        out_specs=_row_spec(tm, D),
        out_shape=jax.ShapeDtypeStruct((T, D), BF16), compiler_params=_params(1))(z, z, yab)


def _conv(ext, w_ref, b_ref):
    down1, down2 = pltpu.roll(ext, 1, 0), pltpu.roll(ext, 2, 0)
    c = b_ref[...] + w_ref[0:1, :] * down2
    c = c + w_ref[1:2, :] * down1
    return c + w_ref[2:3, :] * ext, down1, down2


def conv_fwd(up, convw, convb3, l, *, name, tm=512):
    T = up.shape[0]
    tc = CONV_TC
    nc = DFF // tc
    hb = tm // CONV_HALO

    def body(ua_ref, uap_ref, ub_ref, ubp_ref, wa_ref, wb_ref, ba_ref, bb_ref, o_ref):
        i = pl.program_id(1)

        def conv_of(u_ref, p_ref, w_ref, b_ref):
            ext = jnp.concatenate([jnp.where(i == 0, 0.0, p_ref[...]), u_ref[...]], axis=0)
            return _conv(ext, w_ref, b_ref)[0][CONV_HALO:, :]

        ca = conv_of(ua_ref, uap_ref, wa_ref, ba_ref)
        cb = conv_of(ub_ref, ubp_ref, wb_ref, bb_ref)
        o_ref[...] = (_gelu(ca) * cb).astype(BF16)

    def cur(off):
        return pl.BlockSpec((tm, tc), lambda j, i: (i, j + off))

    def prev(off):
        return pl.BlockSpec((CONV_HALO, tc), lambda j, i: (jnp.maximum(i * hb - 1, 0), j + off))

    def wspec(off):
        return pl.BlockSpec((None, CONV_ROWS, tc), lambda j, i: (l, 0, j + off))

    def bspec(off):
        return pl.BlockSpec((None, 1, tc), lambda j, i: (l, 0, j + off))

    return pl.pallas_call(
        body, name=name, grid=(nc, T // tm),
        in_specs=[cur(0), prev(0), cur(nc), prev(nc), wspec(0), wspec(nc), bspec(0), bspec(nc)],
        out_specs=pl.BlockSpec((tm, tc), lambda j, i: (i, j)),
        out_shape=jax.ShapeDtypeStruct((T, DFF), BF16),
        compiler_params=_params(2))(up, up, up, up, convw, convw, convb3, convb3)


def ple_fwd(x2, pg, e, g3, l, *, name, tm=1024):
    T = x2.shape[0]
    has_norm = g3 is not None

    def body(x_ref, pg_ref, e_ref, *rest):
        xv = x_ref[...] + jax.nn.sigmoid(pg_ref[...].astype(F32)) * e_ref[...].astype(F32)
        if has_norm:
            g_ref, o_ref, h_ref = rest
            r = lax.rsqrt(jnp.mean(xv * xv, axis=-1, keepdims=True) + EPS)
            h_ref[...] = (xv * r * g_ref[...]).astype(BF16)
        else:
            o_ref, = rest
        o_ref[...] = xv

    x_shape = jax.ShapeDtypeStruct((T, D), F32)
    return pl.pallas_call(
        body, name=name, grid=(T // tm,),
        in_specs=[_row_spec(tm, D)] * 3 + ([_gain_spec(l)] if has_norm else []),
        out_specs=[_row_spec(tm, D)] * 2 if has_norm else _row_spec(tm, D),
        out_shape=[x_shape, jax.ShapeDtypeStruct((T, D), BF16)] if has_norm else x_shape,
        compiler_params=_params(1))(x2, pg, e, *([g3] if has_norm else []))


def loss_head(x, g3, tgt, *, name, tm=1024):
    T = x.shape[0]

    def body(x_ref, g_ref, t_ref, loss_ref, dx_ref, dg_ref):
        @pl.when(pl.program_id(0) == 0)
        def _():
            loss_ref[...] = jnp.zeros_like(loss_ref)
            dg_ref[...] = jnp.zeros_like(dg_ref)

        xv, g = x_ref[...], g_ref[...]
        r = lax.rsqrt(jnp.mean(xv * xv, axis=-1, keepdims=True) + EPS)
        xh = xv * r
        err = xh * g - t_ref[...]
        loss_ref[...] += 0.5 * jnp.sum(jnp.mean(err * err, axis=-1, keepdims=True))
        dy = err * (1.0 / D)
        dyg = dy * g
        dx_ref[...] = r * (dyg - xh * jnp.mean(dyg * xh, axis=-1, keepdims=True))
        dg_ref[0:1, :] += jnp.sum(dy * xh, axis=0, keepdims=True)

    return pl.pallas_call(
        body, name=name, grid=(T // tm,),
        in_specs=[_row_spec(tm, D), pl.BlockSpec((1, D), lambda i: (0, 0)), _row_spec(tm, D)],
        out_specs=[pl.BlockSpec((8, LANES), lambda i: (0, 0)), _row_spec(tm, D),
                   pl.BlockSpec((8, D), lambda i: (0, 0))],
        out_shape=[jax.ShapeDtypeStruct((8, LANES), F32), jax.ShapeDtypeStruct((T, D), F32),
                   jax.ShapeDtypeStruct((8, D), F32)],
        compiler_params=_params(1))(x, g3, tgt)


def ple_bwd(dx, pg, e, *, name, tm=1024):
    T = dx.shape[0]

    def body(dx_ref, pg_ref, e_ref, de_ref, dpg_ref):
        gate = jax.nn.sigmoid(pg_ref[...].astype(F32))
        dxv = dx_ref[...]
        de_ref[...] = (dxv * gate).astype(BF16)
        dpg_ref[...] = (dxv * e_ref[...].astype(F32) * gate * (1.0 - gate)).astype(BF16)

    return pl.pallas_call(
        body, name=name, grid=(T // tm,),
        in_specs=[_row_spec(tm, D)] * 3, out_specs=[_row_spec(tm, D)] * 2,
        out_shape=[jax.ShapeDtypeStruct((T, D), BF16)] * 2, compiler_params=_params(1))(dx, pg, e)


def conv_bwd(df, up, convw, convb3, l, *, name, host=None, tm=256):
    T = up.shape[0]
    tc = CONV_TC
    nc = DFF // tc
    hb = tm // CONV_HALO
    nt = T // tm
    rows = tm + 2 * CONV_HALO
    own = slice(CONV_HALO, CONV_HALO + tm)

    hosting = _Hosting(host)
    n_host_in, n_host_out = len(hosting.arrays), len(hosting.out_shapes)

    def body(df_ref, dfn_ref, ua_ref, uap_ref, uan_ref, ub_ref, ubp_ref, ubn_ref, wa_ref, wb_ref, ba_ref, bb_ref,
             *rest):
        host_in = rest[:n_host_in]
        dup_ref, dcw_ref, dcb_ref = rest[n_host_in:n_host_in + 3]
        host_out = rest[n_host_in + 3:n_host_in + 3 + n_host_out]
        sems = rest[n_host_in + 3 + n_host_out:]
        i = pl.program_id(1)
        if hosting.plan:
            first, last = _first_last((nc, nt))

            @pl.when(first)
            def _():
                hosting.begin(host_in, host_out, *sems)

        @pl.when(i == 0)
        def _():
            dcw_ref[...] = jnp.zeros_like(dcw_ref)
            dcb_ref[...] = jnp.zeros_like(dcb_ref)

        def ext_of(c_ref, p_ref, n_ref):
            return jnp.concatenate([jnp.where(i == 0, 0.0, p_ref[...]), c_ref[...],
                                    jnp.where(i == nt - 1, 0.0, n_ref[...])], axis=0)

        ea = ext_of(ua_ref, uap_ref, uan_ref)
        eb = ext_of(ub_ref, ubp_ref, ubn_ref)
        ca, ea1, ea2 = _conv(ea, wa_ref, ba_ref)
        cb, eb1, eb2 = _conv(eb, wb_ref, bb_ref)
        df_ext =jnp.concatenate([jnp.zeros((CONV_HALO, tc), F32), df_ref[...],
                                  jnp.where(i == nt - 1, 0.0, dfn_ref[...])], axis=0)
        cdf = 0.5 * (1.0 + lax.erf(ca * 0.7071067811865476))
        da = df_ext * cb * (cdf + ca * jnp.exp(-0.5 * ca * ca) * 0.3989422804014327)
        db = df_ext * (ca * cdf)

        def finish(h, dc, e, e1, e2, w_ref):
            dup = w_ref[2:3, :] * dc + w_ref[1:2, :] * pltpu.roll(dc, rows - 1, 0)
            dup = dup + w_ref[0:1, :] * pltpu.roll(dc, rows - 2, 0)
            dup_ref[h] = dup[own, :].astype(BF16)
            dco = dc[own, :]
            dcb_ref[h, 0:1, :] += jnp.sum(dco, axis=0, keepdims=True)
            dcw_ref[h, 0:1, :] += jnp.sum(dco * e2[own, :], axis=0, keepdims=True)
            dcw_ref[h, 1:2, :] += jnp.sum(dco * e1[own, :], axis=0, keepdims=True)
            dcw_ref[h, 2:3, :] += jnp.sum(dco * e[own, :], axis=0, keepdims=True)

        finish(0, da, ea, ea1, ea2, wa_ref)
        finish(1, db, eb, eb1, eb2, wb_ref)
        if hosting.plan:
            @pl.when(last)
            def _():
                hosting.finish(host_in, host_out, *sems)

    def nxt(i):
        return jnp.minimum((i + 1) * hb, T // CONV_HALO - 1)

    def prv(i):
        return jnp.maximum(i * hb - 1, 0)

    def up_specs(off):
        return [pl.BlockSpec((tm, tc), lambda j, i: (i, j + off)),
                pl.BlockSpec((CONV_HALO, tc), lambda j, i: (prv(i), j + off)),
                pl.BlockSpec((CONV_HALO, tc), lambda j, i: (nxt(i), j + off))]

    in_specs = [pl.BlockSpec((tm, tc), lambda j, i: (i, j)),
                pl.BlockSpec((CONV_HALO, tc), lambda j, i: (nxt(i), j)),
                *up_specs(0), *up_specs(nc),
                pl.BlockSpec((None, CONV_ROWS, tc), lambda j, i: (l, 0, j)),
                pl.BlockSpec((None, CONV_ROWS, tc), lambda j, i: (l, 0, j + nc)),
                pl.BlockSpec((None, 1, tc), lambda j, i: (l, 0, j)),
                pl.BlockSpec((None, 1, tc), lambda j, i: (l, 0, j + nc))]
    res = pl.pallas_call(
        body, name=name, grid=(nc, nt), in_specs=in_specs + [ANY] * n_host_in,
        out_specs=[pl.BlockSpec((2, tm, tc), lambda j, i: (0, i, j)),
                   pl.BlockSpec((2, 8, tc), lambda j, i: (0, 0, j)),
                   pl.BlockSpec((2, 8, tc), lambda j, i: (0, 0, j))] + [ANY] * n_host_out,
        out_shape=[jax.ShapeDtypeStruct((2, T, DFF), BF16), jax.ShapeDtypeStruct((2, 8, DFF), F32),
                   jax.ShapeDtypeStruct((2, 8, DFF), F32)] + hosting.out_shapes,
        scratch_shapes=hosting.scratch(), input_output_aliases=hosting.aliases(12, 3),
        compiler_params=_params(2))(df, df, up, up, up, up, up, up, convw, convw, convb3, convb3, *hosting.arrays)
    return res[0], res[1], res[2], list(res[3:])


def gate_bwd(dmo, z, yab, *, name, tm=1024):
    T = z.shape[0]

    def body(dmo_ref, zg_ref, y_ref, dz_ref, dy_ref):
        g = jax.nn.sigmoid(zg_ref[...].astype(F32))
        dmo_v = dmo_ref[...].astype(F32)
        dy_ref[...] = (dmo_v * g).astype(BF16)
        dz_ref[...] = (dmo_v * y_ref[...].astype(F32) * g * (1.0 - g)).astype(BF16)

    return pl.pallas_call(
        body, name=name, grid=(T // tm, 2),
        in_specs=[pl.BlockSpec((tm, D), lambda i, s: (i, 0)),
                  pl.BlockSpec((tm, D), lambda i, s: (i, 3 + s)),
                  pl.BlockSpec((tm, D), lambda i, s: (i, s))],
        out_specs=[pl.BlockSpec((tm, D), lambda i, s: (i, 3 + s)),
                   pl.BlockSpec((tm, D), lambda i, s: (i, s))],
        out_shape=[jax.ShapeDtypeStruct((T, 5 * D), BF16), jax.ShapeDtypeStruct((T, 2 * D), BF16)],
        compiler_params=_params(2))(dmo, z, yab)


def mixer_bwd(da, ds, z, dz, wpool, scale3, g3, wsm, wsmT, bT, l, *, name, tm=512):
    T = z.shape[0]
    hb = tm // POOL_HALO
    nt = T // tm

    def body(da_ref, dan_ref, ds_ref, zp_ref, zpp_ref, zu_ref, zv_ref, wp_ref, sc_ref, g_ref, ws_ref, wst_ref,
             b_ref, dzin_ref, dz_ref, dwp_ref, dsc_ref, dws_ref, dbt_ref, dgs_ref, mixed_scr, dvn_scr, db_scr):
        del dzin_ref
        i = pl.program_id(0)

        @pl.when(i == 0)
        def _():
            dwp_ref[...] = jnp.zeros_like(dwp_ref)
            dsc_ref[...] = jnp.zeros_like(dsc_ref)
            dws_ref[...] = jnp.zeros_like(dws_ref)
            dgs_ref[...] = jnp.zeros_like(dgs_ref)
            db_scr[...] = jnp.zeros_like(db_scr)

        zv_p = zp_ref[...].astype(F32)
        ext = jnp.concatenate([jnp.where(i == 0, 0.0, zpp_ref[...].astype(F32)), zv_p], axis=0)
        da_v = da_ref[...].astype(F32)
        da_ext = jnp.concatenate([da_v, jnp.where(i == nt - 1, 0.0, dan_ref[...].astype(F32))], axis=0)
        t = i * tm + lax.broadcasted_iota(jnp.int32, (tm, 1), 0)
        t_ext = i * tm + lax.broadcasted_iota(jnp.int32, (tm + POOL_HALO, 1), 0)
        for g, w in enumerate(POOL_WINDOWS):
            sl = slice(g * PG, (g + 1) * PG)
            pooled = _pooled(ext, zv_p, t, g, w).astype(BF16)
            q = jnp.dot(pooled, wp_ref[g], preferred_element_type=F32)
            dsc_ref[0:1, sl] += jnp.sum(da_v[:, sl] * q, axis=0, keepdims=True)
            dq_ext = (da_ext[:, sl] * sc_ref[:, sl]).astype(BF16)
            dwp_ref[g] += lax.dot_general(pooled, dq_ext[:tm, :], (((0,), (0,)), ((), ())),
                                          preferred_element_type=F32)
            dpool = lax.dot_general(dq_ext, wp_ref[g], (((1,), (1,)), ((), ())), preferred_element_type=F32)
            spread = _winsum_fwd(dpool / jnp.minimum(t_ext + 1, w).astype(F32), w)
            dz_ref[:, sl] = (spread[:tm, :] - dpool[:tm, :]).astype(BF16)

        zu, zv, ds_v = zu_ref[...].astype(F32), zv_ref[...].astype(F32), ds_ref[...].astype(F32)
        gain = g_ref[...]
        gu, gv = _gelu(zu), _gelu(zv)
        rv = lax.rsqrt(jnp.mean(gv * gv, axis=-1, keepdims=True) + EPS)
        vh = gv * rv
        vn = (vh * gain).astype(BF16)
        dmix = ds_v * gu
        dmix_b = dmix.astype(BF16)
        for n in range(tm // CHUNK):
            r = slice(n * CHUNK, (n + 1) * CHUNK)
            db_scr[...] += dmix[r, :]
            for h in range(HEADS):
                cs = slice(h * CHUNK, (h + 1) * CHUNK)
                mixed_scr[r, cs] = jnp.dot(ws_ref[h], vn[r, cs], preferred_element_type=F32) + b_ref[:, h:h + 1]
                dws_ref[h] += lax.dot_general(dmix_b[r, cs], vn[r, cs], (((1,), (1,)), ((), ())),
                                              preferred_element_type=F32)
                dvn_scr[r, cs] = jnp.dot(wst_ref[h], dmix_b[r, cs], preferred_element_type=F32)
        dz_ref[:, D:2 * D] = (ds_v * mixed_scr[...] * _gelu_grad(zu)).astype(BF16)
        dvn = dvn_scr[...]
        dgs_ref[0:1, :] += jnp.sum(dvn * vh, axis=0, keepdims=True)
        dvg = dvn * gain
        dgv = rv * (dvg - vh * jnp.mean(dvg * vh, axis=-1, keepdims=True))
        dz_ref[:, 2 * D:3 * D] = (dgv * _gelu_grad(zv)).astype(BF16)

        @pl.when(i == nt - 1)
        def _():
            tril = (lax.broadcasted_iota(jnp.int32, (CHUNK, CHUNK), 0)
                    >= lax.broadcasted_iota(jnp.int32, (CHUNK, CHUNK), 1)).astype(F32)
            for h in range(HEADS):
                dws_ref[h] = dws_ref[h] * tril
                dbt_ref[:, h:h + 1] = jnp.sum(db_scr[:, h * CHUNK:(h + 1) * CHUNK], axis=1, keepdims=True)

    const4 = lambda i: (l, 0, 0, 0)
    wl = l if wpool.shape[0] > 1 else 0
    in_specs = [
        _row_spec(tm, D),
        pl.BlockSpec((POOL_HALO, D), lambda i: (jnp.minimum((i + 1) * hb, T // POOL_HALO - 1), 0)),
        _row_spec(tm, D),
        _row_spec(tm, D, 0),
        pl.BlockSpec((POOL_HALO, D), lambda i: (jnp.maximum(i * hb - 1, 0), 0)),
        _row_spec(tm, D, 1), _row_spec(tm, D, 2),
        pl.BlockSpec((None, 4, PG, PG), lambda i: (wl, 0, 0, 0)),
        _gain_spec(l), _gain_spec(l),
        pl.BlockSpec((None, HEADS, CHUNK, CHUNK), const4),
        pl.BlockSpec((None, HEADS, CHUNK, CHUNK), const4),
        pl.BlockSpec((None, CHUNK, HEADS), lambda i: (l, 0, 0)),
        ANY,
    ]
    out_specs = [
        pl.BlockSpec((tm, 3 * D), lambda i: (i, 0)),
        pl.BlockSpec((4, PG, PG), lambda i: (0, 0, 0)),
        pl.BlockSpec((8, D), lambda i: (0, 0)),
        pl.BlockSpec((HEADS, CHUNK, CHUNK), lambda i: (0, 0, 0)),
        pl.BlockSpec((CHUNK, HEADS), lambda i: (0, 0)),
        pl.BlockSpec((8, D), lambda i: (0, 0)),
    ]
    out_shape = [
        jax.ShapeDtypeStruct((T, 5 * D), BF16), jax.ShapeDtypeStruct((4, PG, PG), F32),
        jax.ShapeDtypeStruct((8, D), F32), jax.ShapeDtypeStruct((HEADS, CHUNK, CHUNK), F32),
        jax.ShapeDtypeStruct((CHUNK, HEADS), F32), jax.ShapeDtypeStruct((8, D), F32),
    ]
    return pl.pallas_call(
        body, name=name, grid=(nt,), in_specs=in_specs, out_specs=out_specs, out_shape=out_shape,
        scratch_shapes=[pltpu.VMEM((tm, D), F32), pltpu.VMEM((tm, D), F32), pltpu.VMEM((CHUNK, D), F32)],
        input_output_aliases={13: 0}, compiler_params=_params(1))(
            da, da, ds, z, z, z, z, wpool, scale3, g3, wsm, wsmT, bT, dz)


def _row_tile(rows, cols, sub):
    cap = max(sub, (2 * 1024 * 1024) // (4 * cols))
    best = None
    for tr in range(sub, min(rows, cap) + 1, sub):
        if rows % tr == 0:
            best = tr
    return best or rows


def elementwise(fn, ins, out_dtypes, *, name, row_blk_offs=None, rows=None):
    cols = ins[0].shape[1]
    rows = rows or ins[0].shape[0]
    tr = _row_tile(rows, cols, 16)
    offs = row_blk_offs or [0] * len(ins)
    n_in = len(ins)

    def body(*refs):
        outs = fn(*[r[...] for r in refs[:n_in]])
        for o_ref, o in zip(refs[n_in:], outs):
            o_ref[...] = o.astype(o_ref.dtype)

    return pl.pallas_call(
        body, name=name, grid=(rows // tr,),
        in_specs=[pl.BlockSpec((tr, cols), functools.partial(lambda i, o: (i + o * (rows // tr), 0), o=o))
                  for o in offs],
        out_specs=[pl.BlockSpec((tr, cols), lambda i: (i, 0)) for _ in out_dtypes],
        out_shape=[jax.ShapeDtypeStruct((rows, cols), dt) for dt in out_dtypes],
        compiler_params=_params(1))(*ins)


def _adamw(w, g, m, v):
    m = ADAM_B1 * m + (1.0 - ADAM_B1) * g
    v = ADAM_B2 * v + (1.0 - ADAM_B2) * jnp.square(g)
    m_hat = m / (1.0 - ADAM_B1 ** ADAM_STEP)
    v_hat = v / (1.0 - ADAM_B2 ** ADAM_STEP)
    delta = -ADAM_LR * (m_hat / (jnp.sqrt(v_hat) + ADAM_EPS) + ADAM_WD * w)
    return delta, m, v


def _view2d(a):
    return a.reshape(-1, a.shape[-1])


def _place():
    x, y, c = lax.axis_index("x"), lax.axis_index("y"), lax.axis_index("c")
    others = [(1 - x, y), (x, 1 - y), (1 - x, 1 - y)]
    return x, y, c, 2 * x + y, others


def _remote(src, dst, send_sems, recv_sems, k, to):
    return pltpu.make_async_remote_copy(src_ref=src, dst_ref=dst, send_sem=send_sems.at[k], recv_sem=recv_sems.at[k],
                                        device_id=to, device_id_type=MESH)


def _half(ref, axis, j, size, h):
    if len(ref.shape) == 3:
        return ref.at[:, pl.ds(j * size + h * (size // 2), size // 2), :]
    if axis == 0:
        return ref.at[pl.ds(j * size + h * (size // 2), size // 2), :]
    rows = ref.shape[0] // 2
    return ref.at[pl.ds(h * rows, rows), pl.ds(j * size, size)]


def _half_shard_shape(shape, axis, size):
    if len(shape) == 3:
        return (shape[0], size // 2, shape[2])
    if axis == 0:
        return (size // 2, shape[1])
    return (shape[0] // 2, size)


class Exchange:
    def __init__(self, arrays, out_shapes, aliases, n_sems, begin, finish):
        self.arrays, self.out_shapes, self.aliases, self.n_sems = list(arrays), list(out_shapes), aliases, n_sems
        self.begin, self.finish = begin, finish


class _Hosting:
    def __init__(self, plan):
        self.plan = list(plan or [])
        self.arrays = [a for ex in self.plan for a in ex.arrays]
        self.out_shapes = [o for ex in self.plan for o in ex.out_shapes]
        self.n_sems = sum(ex.n_sems for ex in self.plan)

    def scratch(self):
        return [pltpu.SemaphoreType.DMA((self.n_sems,)), pltpu.SemaphoreType.DMA((self.n_sems,))] if self.plan else []

    def aliases(self, in_base, out_base):
        out, i0, o0 = {}, in_base, out_base
        for ex in self.plan:
            out.update({i0 + i: o0 + o for i, o in ex.aliases.items()})
            i0, o0 = i0 + len(ex.arrays), o0 + len(ex.out_shapes)
        return out

    def _each(self, in_refs, out_refs):
        i0 = o0 = s0 = 0
        for ex in self.plan:
            yield ex, in_refs[i0:i0 + len(ex.arrays)], out_refs[o0:o0 + len(ex.out_shapes)], s0
            i0, o0, s0 = i0 + len(ex.arrays), o0 + len(ex.out_shapes), s0 + ex.n_sems

    def begin(self, in_refs, out_refs, send_sems, recv_sems):
        for ex, ins, outs, s0 in self._each(in_refs, out_refs):
            ex.begin(ins, outs, send_sems, recv_sems, s0)

    def finish(self, in_refs, out_refs, send_sems, recv_sems):
        for ex, ins, outs, s0 in self._each(in_refs, out_refs):
            ex.finish(ins, outs, send_sems, recv_sems, s0)


def _first_last(grid):
    ids = [pl.program_id(a) for a in range(len(grid))]
    first = functools.reduce(jnp.logical_and, [i == 0 for i in ids])
    last = functools.reduce(jnp.logical_and, [i == g - 1 for i, g in zip(ids, grid)])
    return first, last


def run_exchanges(plan, *, name):
    host = _Hosting(plan)
    n_in, n_out = len(host.arrays), len(host.out_shapes)

    def body(*refs):
        ins, outs = refs[:n_in], refs[n_in:n_in + n_out]
        send_sems, recv_sems = refs[n_in + n_out:]
        host.begin(ins, outs, send_sems, recv_sems)
        host.finish(ins, outs, send_sems, recv_sems)

    return pl.pallas_call(
        body, name=name, in_specs=[ANY] * n_in, out_specs=[ANY] * n_out, out_shape=host.out_shapes,
        scratch_shapes=host.scratch(), input_output_aliases=host.aliases(0, 0),
        compiler_params=pltpu.CompilerParams(has_side_effects=True))(*host.arrays)


def place_shard(src, l, axis, size, out_dtype, place, *, name):
    shard = src.shape[1:]
    natural = tuple(size * N_CHIPS if a == axis else s for a, s in enumerate(shard))
    if len(shard) == 3:
        blk = (None,) + shard
        grid = (1,)
        in_map = lambda i, pr: (l, 0, 0, 0)
        out_map = lambda i, pr: (0, 0, pr[0], 0)
    else:
        tr = _row_tile(shard[0], shard[1], 16)
        steps = shard[0] // tr
        blk = (None, tr, shard[1])
        grid = (steps,)
        in_map = lambda i, pr: (l, i, 0)
        if axis == 0:
            out_map = lambda i, pr: (0, pr[0] * steps + i, 0)
        else:
            out_map = lambda i, pr: (0, i, pr[0])

    def body(pr_ref, s_ref, o_ref):
        del pr_ref
        o_ref[...] = s_ref[...].astype(o_ref.dtype)

    return pl.pallas_call(
        body, name=name,
        grid_spec=pltpu.PrefetchScalarGridSpec(
            num_scalar_prefetch=1, grid=grid, in_specs=[pl.BlockSpec(blk, in_map)],
            out_specs=pl.BlockSpec(blk, out_map)),
        out_shape=jax.ShapeDtypeStruct((1,) + natural, out_dtype), compiler_params=_params(1))(place, src)


def place_both_layers(src, axis, size, place, *, name):
    rows, cols = src.shape[1], src.shape[2]

    def body(pr_ref, s_ref, o_ref):
        del pr_ref
        o_ref[...] = s_ref[...]

    return pl.pallas_call(
        body, name=name,
        grid_spec=pltpu.PrefetchScalarGridSpec(
            num_scalar_prefetch=1, grid=(2,), in_specs=[pl.BlockSpec((None, rows, cols), lambda lyr, pr: (lyr, 0, 0))],
            out_specs=pl.BlockSpec((None, rows, cols), lambda lyr, pr: (lyr, 0, pr[0]))),
        out_shape=jax.ShapeDtypeStruct((2, rows, cols * N_CHIPS), src.dtype), compiler_params=_params(1))(place, src)


def gather_exchange(arrays, geom):
    n = len(arrays)

    def begin(ins, outs, send_sems, recv_sems, s0):
        x, y, c, j, others = _place()
        for t, (axis, size) in enumerate(geom):
            mine = _half(outs[t].at[0], axis, j, size, c)
            for k, (ox, oy) in enumerate(others):
                _remote(mine, mine, send_sems, recv_sems, s0 + 6 * t + k, (ox, oy, c)).start()

    def finish(ins, outs, send_sems, recv_sems, s0):
        x, y, c, j, others = _place()
        sib = (x, y, 1 - c)
        passed = []
        for t, (axis, size) in enumerate(geom):
            for k, (ox, oy) in enumerate(others):
                landed = _half(outs[t].at[0], axis, 2 * ox + oy, size, c)
                _remote(landed, landed, send_sems, recv_sems, s0 + 6 * t + k, (ox, oy, c)).wait_recv()
                fwd = _remote(landed, landed, send_sems, recv_sems, s0 + 6 * t + 3 + k, sib)
                fwd.start()
                passed.append(fwd)
        for t, (axis, size) in enumerate(geom):
            for k, (ox, oy) in enumerate(others):
                got = _half(outs[t].at[0], axis, 2 * ox + oy, size, 1 - c)
                _remote(got, got, send_sems, recv_sems, s0 + 6 * t + 3 + k, sib).wait_recv()
        for fwd in passed:
            fwd.wait_send()
        for t, (axis, size) in enumerate(geom):
            mine = _half(outs[t].at[0], axis, j, size, c)
            for k, (ox, oy) in enumerate(others):
                _remote(mine, mine, send_sems, recv_sems, s0 + 6 * t + k, (ox, oy, c)).wait_send()

    return Exchange(arrays, [jax.ShapeDtypeStruct(a.shape, a.dtype) for a in arrays], {t: t for t in range(n)},
                    6 * n, begin, finish)


def gather_by_layer_exchange(array, axis, size):
    def blocks(out, others, lyr):
        return [_block(out.at[lyr], axis, 2 * ox + oy, size) for (ox, oy) in others]

    def begin(ins, outs, send_sems, recv_sems, s0):
        x, y, c, j, others = _place()
        mine = _block(outs[0].at[c], axis, j, size)
        for k, (ox, oy) in enumerate(others):
            _remote(mine, mine, send_sems, recv_sems, s0 + k, (ox, oy, c)).start()

    def finish(ins, outs, send_sems, recv_sems, s0):
        x, y, c, j, others = _place()
        sib = (x, y, 1 - c)
        passed = []
        for k, ((ox, oy), landed) in enumerate(zip(others, blocks(outs[0], others, c))):
            _remote(landed, landed, send_sems, recv_sems, s0 + k, (ox, oy, c)).wait_recv()
            fwd = _remote(landed, landed, send_sems, recv_sems, s0 + 3 + k, sib)
            fwd.start()
            passed.append(fwd)
        for k, got in enumerate(blocks(outs[0], others, 1 - c)):
            _remote(got, got, send_sems, recv_sems, s0 + 3 + k, sib).wait_recv()
        for fwd in passed:
            fwd.wait_send()
        mine = _block(outs[0].at[c], axis, j, size)
        for k, (ox, oy) in enumerate(others):
            _remote(mine, mine, send_sems, recv_sems, s0 + k, (ox, oy, c)).wait_send()

    return Exchange([array], [jax.ShapeDtypeStruct(array.shape, array.dtype)], {0: 0}, 6, begin, finish)


def swap_exchange(grads, geom):
    def pieces(t, g, dst, h):
        axis, size = geom[t]
        if len(g.shape) == 2 and axis == 1:
            rows = g.shape[0] // 2
            return [(g.at[pl.ds(h * rows, rows), :], dst)]
        return [(_half(g, axis, jb, size, h), dst.at[jb]) for jb in range(N_CHIPS)]

    counts = [1 if (len(g.shape) == 3 and a == 1) else N_CHIPS for g, (a, _) in zip(grads, geom)]
    bases = [sum(counts[:t]) for t in range(len(grads))]

    def copies(ins, outs, send_sems, recv_sems, s0):
        x, y, c, _, _ = _place()
        cps = []
        for t in range(len(grads)):
            for q, (src, dst) in enumerate(pieces(t, ins[t].at[0], outs[t], 1 - c)):
                cps.append(_remote(src, dst, send_sems, recv_sems, s0 + bases[t] + q, (x, y, 1 - c)))
        return cps

    def begin(*a):
        for cp in copies(*a):
            cp.start()

    def finish(*a):
        for cp in copies(*a):
            cp.wait()

    out_shapes = []
    for g, (axis, size) in zip(grads, geom):
        shp = g.shape[1:]
        if len(shp) == 2 and axis == 1:
            out_shapes.append(jax.ShapeDtypeStruct((shp[0] // 2, shp[1]), g.dtype))
        else:
            out_shapes.append(jax.ShapeDtypeStruct((N_CHIPS,) + _half_shard_shape(shp, axis, size), g.dtype))
    return Exchange(grads, out_shapes, {}, sum(counts), begin, finish)


def scatter_exchange(parts, geom, shapes):
    def copies(ins, outs, send_sems, recv_sems, s0):
        x, y, c, j, others = _place()
        cps = []
        for t, ((axis, size), shp) in enumerate(zip(geom, shapes)):
            for k, (ox, oy) in enumerate(others):
                jp = 2 * ox + oy
                src = ins[t].at[:, pl.ds(jp * size, size)] if (len(shp) == 2 and axis == 1) else ins[t].at[jp]
                cps.append(_remote(src, outs[t].at[k], send_sems, recv_sems, s0 + 3 * t + k, (ox, oy, c)))
        return cps

    def begin(*a):
        for cp in copies(*a):
            cp.start()

    def finish(*a):
        for cp in copies(*a):
            cp.wait_recv()
        for cp in copies(*a):
            cp.wait_send()

    out_shapes = [jax.ShapeDtypeStruct((3,) + _half_shard_shape(shp, axis, size), p.dtype)
                  for p, (axis, size), shp in zip(parts, geom, shapes)]
    return Exchange(parts, out_shapes, {}, 3 * len(parts), begin, finish)


def share_exchange(grads, which):
    n = len(which)

    def my_half(refs, t, h):
        lyr = refs[which[t][0]].at[which[t][1]]
        if len(lyr.shape) == 3:
            rows = lyr.shape[1] // 2
            return lyr.at[:, pl.ds(h * rows, rows), :]
        rows = lyr.shape[0] // 2
        return lyr.at[pl.ds(h * rows, rows), :]

    def begin(ins, outs, send_sems, recv_sems, s0):
        x, y, c, _, _ = _place()
        for t in range(n):
            mine = my_half(outs, t, c)
            _remote(mine, mine, send_sems, recv_sems, s0 + t, (x, y, 1 - c)).start()

    def finish(ins, outs, send_sems, recv_sems, s0):
        x, y, c, _, _ = _place()
        for t in range(n):
            got = my_half(outs, t, 1 - c)
            _remote(got, got, send_sems, recv_sems, s0 + t, (x, y, 1 - c)).wait_recv()
        for t in range(n):
            mine = my_half(outs, t, c)
            _remote(mine, mine, send_sems, recv_sems, s0 + t, (x, y, 1 - c)).wait_send()

    return Exchange(grads, [jax.ShapeDtypeStruct(g.shape, g.dtype) for g in grads],
                    {t: t for t in range(len(grads))}, n, begin, finish)


def all_reduce_small(s):
    rows = s.shape[0]
    half = rows // 2
    assert half % 8 == 0

    def body(s_ref, o_ref, a_ref, b_ref, p_ref, send_sems, recv_sems):
        x, y, c, j, others = _place()
        sib = (x, y, 1 - c)
        swap = _remote(s_ref, a_ref, send_sems, recv_sems, 0, sib)
        swap.start()
        swap.wait()
        p_ref[...] = s_ref[...] + a_ref[...]
        mine = pl.ds(pl.multiple_of(c * half, 8), half)
        b_ref[j] = p_ref[mine, :]
        cps = [_remote(p_ref.at[mine, :], b_ref.at[j], send_sems, recv_sems, 1 + k, (ox, oy, c))
               for k, (ox, oy) in enumerate(others)]
        for cp in cps:
            cp.start()
        for k, (ox, oy) in enumerate(others):
            slot = b_ref.at[2 * ox + oy]
            _remote(slot, slot, send_sems, recv_sems, 1 + k, (ox, oy, c)).wait_recv()
        for cp in cps:
            cp.wait_send()
        o_ref[mine, :] = ((b_ref[0] + b_ref[1]) + b_ref[2]) + b_ref[3]
        back = _remote(o_ref.at[mine, :], o_ref.at[mine, :], send_sems, recv_sems, 4, sib)
        back.start()
        back.wait_send()
        theirs = pl.ds(pl.multiple_of((1 - c) * half, 8), half)
        _remote(o_ref.at[theirs, :], o_ref.at[theirs, :], send_sems, recv_sems, 4, sib).wait_recv()

    vmem = pl.BlockSpec(memory_space=pltpu.VMEM)
    return pl.pallas_call(
        body, name="all_reduce_small", in_specs=[vmem], out_specs=vmem,
        out_shape=jax.ShapeDtypeStruct((rows, LANES), F32),
        scratch_shapes=[pltpu.VMEM((rows, LANES), F32), pltpu.VMEM((N_CHIPS, half, LANES), F32),
                        pltpu.VMEM((rows, LANES), F32), pltpu.SemaphoreType.DMA((5,)),
                        pltpu.SemaphoreType.DMA((5,))],
        compiler_params=pltpu.CompilerParams(vmem_limit_bytes=VMEM_LIMIT, has_side_effects=True))(s)


def pair_sum(g, got, axis, size, place, *, name):
    shp = g.shape[1:]
    if len(shp) == 3:
        hs = size // 2
        grid = (N_CHIPS,)
        g_spec = pl.BlockSpec((None, shp[0], hs, shp[2]), lambda jb, pr: (0, 0, 2 * jb + pr[1], 0))
        r_spec = pl.BlockSpec((None, shp[0], hs, shp[2]), lambda jb, pr: (jb, 0, 0, 0))
    elif axis == 0:
        hs = size // 2
        tr = _row_tile(hs, shp[1], 16)
        steps = hs // tr
        grid = (N_CHIPS, steps)
        g_spec = pl.BlockSpec((None, tr, shp[1]), lambda jb, i, pr: (0, (2 * jb + pr[1]) * steps + i, 0))
        r_spec = pl.BlockSpec((None, tr, shp[1]), lambda jb, i, pr: (jb, i, 0))
    else:
        rows = shp[0] // 2
        tr = _row_tile(rows, shp[1], 16)
        steps = rows // tr
        grid = (steps,)
        g_spec = pl.BlockSpec((None, tr, shp[1]), lambda i, pr: (0, pr[1] * steps + i, 0))
        r_spec = pl.BlockSpec((tr, shp[1]), lambda i, pr: (i, 0))

    def body(pr_ref, g_ref, r_ref, o_ref):
        del pr_ref
        o_ref[...] = (g_ref[...].astype(F32) + r_ref[...].astype(F32)).astype(BF16)

    return pl.pallas_call(
        body, name=name,
        grid_spec=pltpu.PrefetchScalarGridSpec(num_scalar_prefetch=1, grid=grid, in_specs=[g_spec, r_spec],
                                               out_specs=r_spec),
        out_shape=jax.ShapeDtypeStruct(got.shape, BF16), compiler_params=_params(len(grid)))(place, g, got)


def chip_sum(part, slots, shp, axis, size, l, place, out, *, name):
    shard = _shard_shape(shp, axis, size)
    hshape = slots.shape[1:]
    if len(shp) == 3:
        grid = (1,)
        p_spec = pl.BlockSpec((None,) + hshape, lambda i, pr: (pr[0], 0, 0, 0))
        s_specs = [pl.BlockSpec((None,) + hshape, functools.partial(lambda i, pr, k: (k, 0, 0, 0), k=k))
                   for k in range(3)]
        o_spec = pl.BlockSpec((None,) + hshape, lambda i, pr: (l, 0, pr[1], 0))
    else:
        tr = _row_tile(hshape[0], hshape[1], 16)
        steps = hshape[0] // tr
        grid = (steps,)
        if axis == 0:
            p_spec = pl.BlockSpec((None, tr, hshape[1]), lambda i, pr: (pr[0], i, 0))
        else:
            p_spec = pl.BlockSpec((tr, hshape[1]), lambda i, pr: (i, pr[0]))
        s_specs = [pl.BlockSpec((None, tr, hshape[1]), functools.partial(lambda i, pr, k: (k, i, 0), k=k))
                   for k in range(3)]
        o_spec = pl.BlockSpec((None, tr, hshape[1]), lambda i, pr: (l, pr[1] * steps + i, 0))
    has_out = out is not None

    def body(pr_ref, p_ref, s0_ref, s1_ref, s2_ref, *rest):
        del pr_ref
        rest[-1][...] = ((p_ref[...].astype(F32) + s0_ref[...].astype(F32)) + s1_ref[...].astype(F32)) \
            + s2_ref[...].astype(F32)

    return pl.pallas_call(
        body, name=name,
        grid_spec=pltpu.PrefetchScalarGridSpec(
            num_scalar_prefetch=1, grid=grid, in_specs=[p_spec] + s_specs + ([ANY] if has_out else []),
            out_specs=o_spec),
        out_shape=jax.ShapeDtypeStruct((2,) + shard, F32), input_output_aliases={5: 0} if has_out else {},
        compiler_params=_params(1))(place, part, slots, slots, slots, *([out] if has_out else []))


GEOM = {name: (axis, size) for (name, _, axis, size) in BIG}
SHAPE = {name: shape for (name, shape, _, _) in BIG}
RIDES_IN_PROJ_L0 = ((0, ("w_pool", "w_branch_a", "w_branch_b", "w_out", "w_up")),)
RIDES_UP_PROJ_L0 = ((0, ("w_down", "w_ple_gate", "w_ple")), (1, ("w_in",)))
RIDES_DOWN_PROJ_L0 = ((1, ("w_pool", "w_branch_a", "w_branch_b", "w_out", "w_ple_gate", "w_ple")),)
RIDES_IN_PROJ_L1 = ((1, ("w_up", "w_down")),)
EARLY_GRADS_L0 = ("w_ple", "w_ple_gate", "w_down", "w_up")
LATE_GRADS_L0 = ("w_out", "w_branch_a", "w_branch_b", "w_pool", "w_in")


def _swap_of(G, names):
    return swap_exchange([G[k] for k in names], [GEOM[k] for k in names])


def _after_swap(G, names, got, place, tag):
    parts = [pair_sum(G[k], r, *GEOM[k], place, name=f"pair_sum_{k}_{tag}") for k, r in zip(names, got)]
    return scatter_exchange(parts, [GEOM[k] for k in names], [SHAPE[k] for k in names]), parts


def _reduce_start(G, names, place, tag):
    got = run_exchanges([_swap_of(G, names)], name=f"swap_halves_{tag}")
    return _after_swap(G, names, got, place, tag)


def _reduce_end(names, parts, slots, place, l, reduced):
    for k, q, s in zip(names, parts, slots):
        reduced[k] = chip_sum(q, s, SHAPE[k], *GEOM[k], l, place, reduced.get(k), name=f"chip_sum_{k}_l{l}")


def _local_step(x, p2, tgt, W0, W1, conv_w, small, place):
    T = x.shape[0]
    as3 = lambda a: a.reshape(2, 1, a.shape[-1])
    mix3, scale3, sgu3 = as3(small["mix_norm"]), as3(small["pool_scale"]), as3(small["sgu_norm"])
    ffn3, ple3, convb3 = as3(small["ffn_norm"]), as3(small["ple_norm"]), as3(small["conv_b"])
    tril = jnp.tril(jnp.ones((CHUNK, CHUNK), F32))
    ws_masked = small["w_spatial"] * tril
    wsm = ws_masked.astype(BF16)
    wsmT = jnp.swapaxes(ws_masked, -1, -2).astype(BF16)
    bT = jnp.swapaxes(small["b_spatial"], -1, -2)
    final3 = small["final_norm"].reshape(1, D)
    W = [dict(W0), dict(W1)]

    def riders(groups):
        return [gather_exchange([W[lyr][k] for k in names], [GEOM[k] for k in names]) for lyr, names in groups]

    def landed(groups, got):
        for lyr, names in groups:
            W[lyr].update(zip(names, got[:len(names)]))
            got = got[len(names):]

    saved = []
    hb = norm_fwd(x, mix3, 0, name="mix_norm_fwd_l0")
    for l in range(2):
        n = lambda s: f"{s}_l{l}"
        Wl = W[l]
        groups = RIDES_IN_PROJ_L0 if l == 0 else RIDES_IN_PROJ_L1
        z, got = mm_nn(hb, Wl["w_in"], 0, name=n("in_proj"), rows=T, tn=1280, out_dtype=BF16, host=riders(groups))
        landed(groups, got)
        a_in = pool_fwd(z, Wl["w_pool"], scale3, l, name=n("pool_fwd"))
        s_in = sgu_fwd(z, sgu3, wsm, bT, l, name=n("sgu_fwd"))
        yab = mm_nn(a_in, Wl["w_branch_a"], 0, name=n("branch_a"), rows=T, out_cols=2 * D, out_dtype=BF16)
        yab = mm_nn(s_in, Wl["w_branch_b"], 0, name=n("branch_b"), rows=T, out=yab, out_cols=2 * D, out_col_off=D,
                    out_dtype=BF16)
        mo = gate_fwd(z, yab, name=n("gate_fwd"))
        x1, h2b = mm_nn(mo, Wl["w_out"], 0, name=n("out_proj"), rows=T, resid=x, norm_gain=ffn3[l:l + 1])
        if l == 0:
            up, got = mm_nn(h2b, Wl["w_up"], 0, name=n("up_proj"), rows=T, tn=DFF, host=riders(RIDES_UP_PROJ_L0))
            landed(RIDES_UP_PROJ_L0, got)
        else:
            up = mm_nn(h2b, Wl["w_up"], 0, name=n("up_proj"), rows=T, tn=DFF)
        f = conv_fwd(up, conv_w, convb3, l, name=n("conv_fwd"))
        if l == 0:
            (x2, h3b), got = mm_nn(f, Wl["w_down"], 0, name=n("down_proj"), rows=T, resid=x1,
                                   norm_gain=ple3[l:l + 1], host=riders(RIDES_DOWN_PROJ_L0))
            landed(RIDES_DOWN_PROJ_L0, got)
        else:
            x2, h3b = mm_nn(f, Wl["w_down"], 0, name=n("down_proj"), rows=T, resid=x1, norm_gain=ple3[l:l + 1])
        pg = mm_nn(h3b, Wl["w_ple_gate"], 0, name=n("ple_gate_proj"), rows=T, out_dtype=BF16)
        e = mm_nn(p2, Wl["w_ple"], 0, name=n("ple_proj"), rows=T, a_row_off=l * T, out_dtype=BF16)
        saved.append(dict(x=x, hb=hb, z=z, a_in=a_in, s_in=s_in, yab=yab, mo=mo, x1=x1, h2b=h2b, up=up, f=f,
                          x2=x2, h3b=h3b, pg=pg, e=e))
        if l == 0:
            x, hb = ple_fwd(x2, pg, e, mix3, 1, name=n("ple_fwd"))
        else:
            x = ple_fwd(x2, pg, e, None, 0, name=n("ple_fwd"))

    loss_acc, dx, dg_final = loss_head(x, final3, tgt, name="loss_head")

    small_grads = [None, None]
    all_names = [t[0] for t in BIG]
    reduced = {}
    swap1 = G1 = scatter1 = parts1 = slots1 = None
    for l in (1, 0):
        n = lambda s: f"{s}_l{l}"
        a, Wl, G = saved[l], W[l], {}
        de, dpg = ple_bwd(dx, a["pg"], a["e"], name=n("ple_bwd"))
        G["w_ple"] = mm_tn(p2, de, name=n("d_w_ple"), rows=T, ka=PG, nb=D, a_row_off=l * T)
        G["w_ple_gate"] = mm_tn(a["h3b"], dpg, name=n("d_w_ple_gate"), rows=T, ka=D, nb=D)
        if l == 0:
            (dx2, dg_ple), got = mm_nt(dpg, Wl["w_ple_gate"], 0, name=n("ple_norm_bwd"), rows=T,
                                       norm_bwd_of=(a["x2"], ple3, l, dx), host=[swap1])
            scatter1, parts1 = _after_swap(G1, all_names, got, place, "l1")
        else:
            dx2, dg_ple = mm_nt(dpg, Wl["w_ple_gate"], 0, name=n("ple_norm_bwd"), rows=T,
                                norm_bwd_of=(a["x2"], ple3, l, dx))
        df = mm_nt(dx2, Wl["w_down"], 0, name=n("d_ffn_act"), rows=T, out_dtype=F32)
        G["w_down"] = mm_tn(a["f"], dx2, name=n("d_w_down"), rows=T, ka=DFF, nb=D, tm=1408)
        if l == 0:
            dup, dcw, dcb, slots1 = conv_bwd(df, a["up"], conv_w, convb3, l, name=n("conv_bwd"), host=[scatter1])
        else:
            dup, dcw, dcb, _ = conv_bwd(df, a["up"], conv_w, convb3, l, name=n("conv_bwd"))
        G["w_up"] = mm_tn(a["h2b"], dup, name=n("d_w_up"), rows=T, ka=D, nb=2 * DFF, tn=DFF, tk=1024)
        if l == 0:
            scatter_early, parts_early = _reduce_start(G, EARLY_GRADS_L0, place, "l0_early")
            (dx1, dg_ffn), slots_early = mm_nt(dup, Wl["w_up"], 0, name=n("ffn_norm_bwd"), rows=T, tk=1408,
                                               norm_bwd_of=(a["x1"], ffn3, l, dx2), host=[scatter_early])
        else:
            dx1, dg_ffn = mm_nt(dup, Wl["w_up"], 0, name=n("ffn_norm_bwd"), rows=T, tk=1408,
                                norm_bwd_of=(a["x1"], ffn3, l, dx2))
        dmo = mm_nt(dx1, Wl["w_out"], 0, name=n("d_gated"), rows=T)
        G["w_out"] = mm_tn(a["mo"], dx1, name=n("d_w_out"), rows=T, ka=D, nb=D)
        dz, dyab = gate_bwd(dmo, a["z"], a["yab"], name=n("gate_bwd"))
        G["w_branch_a"] = mm_tn(a["a_in"], dyab, name=n("d_w_branch_a"), rows=T, ka=D, nb=D)
        G["w_branch_b"] = mm_tn(a["s_in"], dyab, name=n("d_w_branch_b"), rows=T, ka=D, nb=D, b_col_off=D)
        da = mm_nt(dyab, Wl["w_branch_a"], 0, name=n("d_pool_out"), rows=T, kdim=D)
        ds = mm_nt(dyab, Wl["w_branch_b"], 0, name=n("d_sgu_out"), rows=T, kdim=D, a_col_off=D)
        dz, dwp, dsc, dws, dbt, dgs = mixer_bwd(da, ds, a["z"], dz, Wl["w_pool"], scale3, sgu3, wsm, wsmT, bT, l,
                                                name=n("mixer_bwd"))
        G["w_pool"] = dwp.astype(BF16)[None]
        G["w_in"] = mm_tn(a["hb"], dz, name=n("d_w_in"), rows=T, ka=D, nb=5 * D, tn=1280)
        if l == 0:
            scatter_late, parts_late = _reduce_start(G, LATE_GRADS_L0, place, "l0_late")
            _reduce_end(all_names, parts1, slots1, place, 1, reduced)
            _reduce_end(EARLY_GRADS_L0, parts_early, slots_early, place, 0, reduced)
            done = [(t, 1) for t in range(len(all_names))] + [(all_names.index(k), 0) for k in EARLY_GRADS_L0]
            (dx, dg_mix), got = mm_nt(
                dz, Wl["w_in"], 0, name=n("mix_norm_bwd"), rows=T, tk=1280, norm_bwd_of=(a["x"], mix3, l, dx1),
                host=[scatter_late, share_exchange([reduced[k] for k in all_names], done)])
            slots_late = got[:len(LATE_GRADS_L0)]
            reduced.update(zip(all_names, got[len(LATE_GRADS_L0):]))
        else:
            dx, dg_mix = mm_nt(dz, Wl["w_in"], 0, name=n("mix_norm_bwd"), rows=T, tk=1280,
                               norm_bwd_of=(a["x"], mix3, l, dx1))
            swap1, G1 = _swap_of(G, all_names), G
        small_grads[l] = dict(
            mix_norm=dg_mix[0], pool_scale=dsc[0], sgu_norm=dgs[0], w_spatial=dws, b_spatial=dbt.T,
            ffn_norm=dg_ffn[0], conv_b=jnp.concatenate([dcb[0, 0], dcb[1, 0]]), ple_norm=dg_ple[0],
            conv_w=jnp.concatenate([dcw[0, :3], dcw[1, :3]], axis=1))
    _reduce_end(LATE_GRADS_L0, parts_late, slots_late, place, 0, reduced)
    return loss_acc, dx, reduced, small_grads, dg_final[0]


SMALL_ORDER = ("mix_norm", "pool_scale", "sgu_norm", "w_spatial", "b_spatial", "ffn_norm", "conv_b", "ple_norm",
               "conv_w")


def _pack_rows(pieces, row_multiple):
    flat = jnp.concatenate([a.reshape(-1) for a in pieces])
    rows = -(-flat.shape[0] // LANES)
    rows = -(-rows // row_multiple) * row_multiple
    return jnp.pad(flat, (0, rows * LANES - flat.shape[0])).reshape(rows, LANES)


def _unpack(flat, shapes):
    out, off = [], 0
    for shp in shapes:
        size = 1
        for s in shp:
            size *= s
        out.append(flat[off:off + size].reshape(shp))
        off += size
    return out


def kernel(x, p, mix_norm, w_in, w_pool, pool_scale, sgu_norm, w_spatial, b_spatial, w_branch_a, w_branch_b, w_out, ffn_norm, w_up, conv_w, conv_b, w_down, ple_norm, w_ple_gate, w_ple, final_norm, loss_target, m_mix_norm, m_w_in, m_w_pool, m_pool_scale, m_sgu_norm, m_w_spatial, m_b_spatial, m_w_branch_a, m_w_branch_b, m_w_out, m_ffn_norm, m_w_up, m_conv_w, m_conv_b, m_w_down, m_ple_norm, m_w_ple_gate, m_w_ple, m_final_norm, v_mix_norm, v_w_in, v_w_pool, v_pool_scale, v_sgu_norm, v_w_spatial, v_b_spatial, v_w_branch_a, v_w_branch_b, v_w_out, v_ffn_norm, v_w_up, v_conv_w, v_conv_b, v_w_down, v_ple_norm, v_w_ple_gate, v_w_ple, v_final_norm):
    names = ["mix_norm", "w_in", "w_pool", "pool_scale", "sgu_norm", "w_spatial", "b_spatial", "w_branch_a",
             "w_branch_b", "w_out", "ffn_norm", "w_up", "conv_w", "conv_b", "w_down", "ple_norm", "w_ple_gate",
             "w_ple", "final_norm"]
    w = dict(zip(names, [mix_norm, w_in, w_pool, pool_scale, sgu_norm, w_spatial, b_spatial, w_branch_a, w_branch_b,
                         w_out, ffn_norm, w_up, conv_w, conv_b, w_down, ple_norm, w_ple_gate, w_ple, final_norm]))
    m = dict(zip(names, [m_mix_norm, m_w_in, m_w_pool, m_pool_scale, m_sgu_norm, m_w_spatial, m_b_spatial,
                         m_w_branch_a, m_w_branch_b, m_w_out, m_ffn_norm, m_w_up, m_conv_w, m_conv_b, m_w_down,
                         m_ple_norm, m_w_ple_gate, m_w_ple, m_final_norm]))
    v = dict(zip(names, [v_mix_norm, v_w_in, v_w_pool, v_pool_scale, v_sgu_norm, v_w_spatial, v_b_spatial,
                         v_w_branch_a, v_w_branch_b, v_w_out, v_ffn_norm, v_w_up, v_conv_w, v_conv_b, v_w_down,
                         v_ple_norm, v_w_ple_gate, v_w_ple, v_final_norm]))
    T = x.shape[1]
    chip = 2 * lax.axis_index("x") + lax.axis_index("y")
    place = jnp.stack([chip, lax.axis_index("c")]).astype(jnp.int32)

    big_names = [t[0] for t in BIG]
    placed = [{k: place_shard(w[k], l, *GEOM[k], BF16, place, name=f"place_{k}_l{l}") for k in big_names}
              for l in range(2)]
    conv_w8 = jnp.pad(conv_w, ((0, 0), (0, CONV_ROWS - conv_w.shape[1]), (0, 0)))
    conv_placed = place_both_layers(conv_w8, 1, conv_w.shape[2], place, name="place_conv_w")
    w_in0, conv_w_all = run_exchanges([gather_exchange([placed[0]["w_in"]], [GEOM["w_in"]]),
                                       gather_by_layer_exchange(conv_placed, 1, conv_w.shape[2])],
                                      name="gather_first_weights")
    placed[0]["w_in"] = w_in0

    small = {k: w[k] for k in ("mix_norm", "pool_scale", "sgu_norm", "w_spatial", "b_spatial", "ffn_norm",
                               "conv_b", "ple_norm", "final_norm")}
    loss_acc, dx, reduced, small_grads, dg_final = _local_step(
        x.reshape(T, D), p.reshape(2 * T, p.shape[-1]), loss_target.reshape(T, D), placed[0], placed[1], conv_w_all,
        small, place)
    full = run_exchanges([share_exchange([reduced[k] for k in big_names],
                                         [(big_names.index(k), 0) for k in LATE_GRADS_L0])], name="share_last_halves")
    grads = dict(zip(big_names, full))

    pieces = [small_grads[l][k] for l in range(2) for k in SMALL_ORDER] + [dg_final, loss_acc[0, 0].reshape(1)]
    shapes = [a.shape for a in pieces]
    total = all_reduce_small(_pack_rows(pieces, 16)).reshape(-1)
    summed = _unpack(total, shapes)
    per_layer = {k: jnp.stack([summed[i], summed[len(SMALL_ORDER) + i]]) for i, k in enumerate(SMALL_ORDER)}
    for k in ("mix_norm", "pool_scale", "sgu_norm", "w_spatial", "b_spatial", "ffn_norm", "conv_b", "ple_norm"):
        grads[k] = per_layer[k]
    grads["final_norm"] = summed[-2]
    loss = summed[-1].reshape(())
    cw = conv_w.shape[2]
    grads["conv_w"] = lax.dynamic_slice_in_dim(per_layer["conv_w"], chip * cw, cw, axis=2)

    delta, new_m, new_v = {}, {}, {}
    for name in big_names:
        shp = w[name].shape
        d_, m_, v_, g_ = elementwise(lambda w_, g_, m_, v_: (*_adamw(w_, g_, m_, v_), g_),
                                     [_view2d(a) for a in (w[name], grads[name], m[name], v[name])],
                                     [F32, F32, F32, F32], name=f"adamw_{name}")
        delta[name], new_m[name], new_v[name] = d_.reshape(shp), m_.reshape(shp), v_.reshape(shp)
        grads[name] = g_.reshape(shp)
    small_names = [k for k in names if k not in big_names]
    small_shapes = [w[k].shape for k in small_names]
    packed = [_pack_rows([src[k] for k in small_names], 8) for src in (w, grads, m, v)]
    outs = elementwise(_adamw, packed, [F32, F32, F32], name="adamw_small")
    for dst, o in zip((delta, new_m, new_v), outs):
        for k, a in zip(small_names, _unpack(o.reshape(-1), small_shapes)):
            dst[k] = a

    return (loss, dx.reshape(1, T, D), *[grads[k] for k in names], *[delta[k] for k in names],
            *[new_m[k] for k in names], *[new_v[k] for k in names])
```
